```python
import jax, jax.numpy as jnp
from jax import lax
import numpy as np

D_MODEL = 1024
BATCH = 16
SEQ = 2048
DEPTH = 4

N_META = 16
N_A_LAYERS = DEPTH // 2
N_B_LAYERS = DEPTH - N_A_LAYERS
CONV_WIDTH = 31
HEAD_DIM = 64
N_HEADS = D_MODEL // HEAD_DIM
N_KV_HEADS = 4
Q_PER_KV = N_HEADS // N_KV_HEADS
WINDOW = 128
BLOCK = 128
D_FF = -(-8 * D_MODEL // (3 * 256)) * 256
NORM_EPS = 1e-6
NEG_INF = -1e30

kernel_name = "yoco_conformer_swa_sink_hybrid"


def rms_norm(x, g):
    xf = x.astype(jnp.float32)
    y = xf * lax.rsqrt(jnp.mean(xf * xf, axis=-1, keepdims=True) + NORM_EPS)
    return (y * g.astype(jnp.float32)).astype(x.dtype)


def layer_norm(x, g, b):
    xf = x.astype(jnp.float32)
    mu = jnp.mean(xf, axis=-1, keepdims=True)
    var = jnp.mean(jnp.square(xf - mu), axis=-1, keepdims=True)
    y = (xf - mu) * lax.rsqrt(var + NORM_EPS)
    return (y * g.astype(jnp.float32) + b.astype(jnp.float32)).astype(x.dtype)


def conformer_conv(u, w_in, b_in, dw, ln_g, ln_b, w_out, b_out):
    d = u.shape[-1]
    a = u @ w_in + b_in
    a = a[..., :d] * jax.nn.sigmoid(a[..., d:])
    a = jnp.pad(a, ((0, 0), (CONV_WIDTH - 1, 0), (0, 0)))
    c = lax.conv_general_dilated(a, dw[:, None, :], window_strides=(1,), padding='VALID',
                                 dimension_numbers=('NWC', 'WIO', 'NWC'),
                                 feature_group_count=d)
    c = jax.nn.silu(layer_norm(c, ln_g, ln_b))
    return c @ w_out + b_out


def swiglu(u, w_gate, w_up, w_down):
    return (jax.nn.silu(u @ w_gate) * (u @ w_up)) @ w_down


def shared_kv(h, g, w_kv, k_norm):
    b, l, _ = h.shape
    kv = rms_norm(h, g) @ w_kv
    k, v = jnp.split(kv, 2, axis=-1)
    k = rms_norm(k.reshape(b, l, N_KV_HEADS, HEAD_DIM), k_norm)
    v = v.reshape(b, l, N_KV_HEADS, HEAD_DIM)
    return k, v


def band_keys(t, nb):
    b = t.shape[0]
    tp = jnp.pad(t, ((0, 0), (BLOCK, 0), (0, 0), (0, 0)))
    tb = tp.reshape(b, nb + 1, BLOCK, N_KV_HEADS, HEAD_DIM)
    band = jnp.concatenate([tb[:, :-1], tb[:, 1:]], axis=2)
    meta = jnp.broadcast_to(t[:, None, :N_META], (b, nb, N_META, N_KV_HEADS, HEAD_DIM))
    return jnp.concatenate([meta, band], axis=2)


def window_mask(nb):
    qpos = jnp.arange(nb)[:, None] * BLOCK + jnp.arange(BLOCK)[None, :]
    band_pos = jnp.arange(nb)[:, None] * BLOCK - BLOCK + jnp.arange(2 * BLOCK)[None, :]
    diff = qpos[:, :, None] - band_pos[:, None, :]
    band_ok = (diff >= 0) & (diff < WINDOW) & (band_pos[:, None, :] >= N_META)
    meta_ok = jnp.arange(N_META)[None, None, :] <= qpos[:, :, None]
    return jnp.concatenate([meta_ok, band_ok], axis=-1)


def sliding_window_attention(u, w_q, q_g, sinks, w_o, k, v):
    b, l, _ = u.shape
    nb = l // BLOCK
    q = (u @ w_q).reshape(b, l, N_KV_HEADS, Q_PER_KV, HEAD_DIM)
    q = rms_norm(q, q_g).reshape(b, nb, BLOCK, N_KV_HEADS, Q_PER_KV, HEAD_DIM)
    kw = band_keys(k, nb)
    vw = band_keys(v, nb)
    s = jnp.einsum('bnqgrd,bnkgd->bgrnqk', q, kw).astype(jnp.float32) * (HEAD_DIM ** -0.5)
    s = jnp.where(window_mask(nb), s, NEG_INF)
    sink = sinks.astype(jnp.float32).reshape(1, N_KV_HEADS, Q_PER_KV, 1, 1, 1)
    m = jnp.maximum(jnp.max(s, axis=-1, keepdims=True), sink)
    p = jnp.exp(s - m)
    denom = jnp.sum(p, axis=-1, keepdims=True) + jnp.exp(sink - m)
    p = (p / denom).astype(vw.dtype)
    o = jnp.einsum('bgrnqk,bnkgd->bnqgrd', p, vw).reshape(b, l, N_HEADS * HEAD_DIM)
    return o @ w_o


def _fwd_setup_inputs(seed: int = 0) -> dict:
    key = jax.random.key(seed)
    ks = jax.random.split(key, 21)
    out_scale = (2 * DEPTH) ** -0.5
    kvd = 2 * N_KV_HEADS * HEAD_DIM
    qd = N_HEADS * HEAD_DIM

    def nrm(k, shape, scale):
        return jax.random.normal(k, shape, jnp.float32) * scale

    def gain(k, shape):
        return 1.0 + nrm(k, shape, 0.02)

    return {
        "x": nrm(ks[0], (BATCH, SEQ, D_MODEL), 1.0),
        "meta_tokens": nrm(ks[1], (N_META, D_MODEL), 1.0),
        "norm_mix": gain(ks[2], (DEPTH, D_MODEL)),
        "norm_ffn": gain(ks[3], (DEPTH, D_MODEL)),
        "conv_w_in": nrm(ks[4], (N_A_LAYERS, D_MODEL, 2 * D_MODEL), D_MODEL ** -0.5),
        "conv_b_in": nrm(ks[5], (N_A_LAYERS, 2 * D_MODEL), 0.02),
        "conv_dw": nrm(ks[6], (N_A_LAYERS, CONV_WIDTH, D_MODEL), CONV_WIDTH ** -0.5),
        "conv_ln_g": gain(ks[7], (N_A_LAYERS, D_MODEL)),
        "conv_ln_b": nrm(ks[8], (N_A_LAYERS, D_MODEL), 0.02),
        "conv_w_out": nrm(ks[9], (N_A_LAYERS, D_MODEL, D_MODEL), D_MODEL ** -0.5 * out_scale),
        "conv_b_out": nrm(ks[10], (N_A_LAYERS, D_MODEL), 0.02),
        "kv_norm": gain(ks[11], (D_MODEL,)),
        "w_kv": nrm(ks[12], (D_MODEL, kvd), D_MODEL ** -0.5),
        "k_norm": gain(ks[13], (HEAD_DIM,)),
        "w_q": nrm(ks[14], (N_B_LAYERS, D_MODEL, qd), D_MODEL ** -0.5),
        "q_norm": gain(ks[15], (N_B_LAYERS, HEAD_DIM)),
        "attn_sinks": nrm(ks[16], (N_B_LAYERS, N_HEADS), 0.5),
        "w_o": nrm(ks[17], (N_B_LAYERS, qd, D_MODEL), qd ** -0.5 * out_scale),
        "ffn_w_gate": nrm(ks[18], (DEPTH, D_MODEL, D_FF), D_MODEL ** -0.5),
        "ffn_w_up": nrm(ks[19], (DEPTH, D_MODEL, D_FF), D_MODEL ** -0.5),
        "ffn_w_down": nrm(ks[20], (DEPTH, D_FF, D_MODEL), D_FF ** -0.5 * out_scale),
    }


def _fwd_reference(x, meta_tokens, norm_mix, norm_ffn, conv_w_in, conv_b_in, conv_dw, conv_ln_g, conv_ln_b,
              conv_w_out, conv_b_out, kv_norm, w_kv, k_norm, w_q, q_norm, attn_sinks, w_o,
              ffn_w_gate, ffn_w_up, ffn_w_down):
    b, seq, d = x.shape
    l = N_META + seq
    lp = -(-l // BLOCK) * BLOCK
    meta = jnp.broadcast_to(meta_tokens.astype(x.dtype)[None], (b, N_META, d))
    h = jnp.concatenate([meta, x, jnp.zeros((b, lp - l, d), x.dtype)], axis=1)
    k_sh = None
    v_sh = None
    for layer in range(DEPTH):
        u = rms_norm(h, norm_mix[layer])
        if layer < N_A_LAYERS:
            i = layer
            h = h + conformer_conv(u, conv_w_in[i], conv_b_in[i], conv_dw[i], conv_ln_g[i],
                                   conv_ln_b[i], conv_w_out[i], conv_b_out[i])
        else:
            if layer == N_A_LAYERS:
                k_sh, v_sh = shared_kv(h, kv_norm, w_kv, k_norm)
            j = layer - N_A_LAYERS
            h = h + sliding_window_attention(u, w_q[j], q_norm[j], attn_sinks[j], w_o[j], k_sh, v_sh)
        h = h + swiglu(rms_norm(h, norm_ffn[layer]), ffn_w_gate[layer], ffn_w_up[layer], ffn_w_down[layer])
    return h[:, N_META:l]


import jax as _jax
import jax.numpy as _jnp

TWIN_FORMAT = 'train_step'
FWD_PARAMS = ['x', 'meta_tokens', 'norm_mix', 'norm_ffn', 'conv_w_in', 'conv_b_in', 'conv_dw', 'conv_ln_g', 'conv_ln_b', 'conv_w_out', 'conv_b_out', 'kv_norm', 'w_kv', 'k_norm', 'w_q', 'q_norm', 'attn_sinks', 'w_o', 'ffn_w_gate', 'ffn_w_up', 'ffn_w_down']
TWIN_WEIGHTS = ['meta_tokens', 'norm_mix', 'norm_ffn', 'conv_w_in', 'conv_b_in', 'conv_dw', 'conv_ln_g', 'conv_ln_b', 'conv_w_out', 'conv_b_out', 'kv_norm', 'w_kv', 'k_norm', 'w_q', 'q_norm', 'attn_sinks', 'w_o', 'ffn_w_gate', 'ffn_w_up', 'ffn_w_down']
TWIN_DIFF_INPUT = 'x'
TWIN_INPUTS = ['x', 'meta_tokens', 'norm_mix', 'norm_ffn', 'conv_w_in', 'conv_b_in', 'conv_dw', 'conv_ln_g', 'conv_ln_b', 'conv_w_out', 'conv_b_out', 'kv_norm', 'w_kv', 'k_norm', 'w_q', 'q_norm', 'attn_sinks', 'w_o', 'ffn_w_gate', 'ffn_w_up', 'ffn_w_down', 'loss_target', 'm_meta_tokens', 'm_norm_mix', 'm_norm_ffn', 'm_conv_w_in', 'm_conv_b_in', 'm_conv_dw', 'm_conv_ln_g', 'm_conv_ln_b', 'm_conv_w_out', 'm_conv_b_out', 'm_kv_norm', 'm_w_kv', 'm_k_norm', 'm_w_q', 'm_q_norm', 'm_attn_sinks', 'm_w_o', 'm_ffn_w_gate', 'm_ffn_w_up', 'm_ffn_w_down', 'v_meta_tokens', 'v_norm_mix', 'v_norm_ffn', 'v_conv_w_in', 'v_conv_b_in', 'v_conv_dw', 'v_conv_ln_g', 'v_conv_ln_b', 'v_conv_w_out', 'v_conv_b_out', 'v_kv_norm', 'v_w_kv', 'v_k_norm', 'v_w_q', 'v_q_norm', 'v_attn_sinks', 'v_w_o', 'v_ffn_w_gate', 'v_ffn_w_up', 'v_ffn_w_down']
TWIN_OUTPUTS = ['loss', 'grad_x', 'grad_meta_tokens', 'grad_norm_mix', 'grad_norm_ffn', 'grad_conv_w_in', 'grad_conv_b_in', 'grad_conv_dw', 'grad_conv_ln_g', 'grad_conv_ln_b', 'grad_conv_w_out', 'grad_conv_b_out', 'grad_kv_norm', 'grad_w_kv', 'grad_k_norm', 'grad_w_q', 'grad_q_norm', 'grad_attn_sinks', 'grad_w_o', 'grad_ffn_w_gate', 'grad_ffn_w_up', 'grad_ffn_w_down', 'delta_meta_tokens', 'delta_norm_mix', 'delta_norm_ffn', 'delta_conv_w_in', 'delta_conv_b_in', 'delta_conv_dw', 'delta_conv_ln_g', 'delta_conv_ln_b', 'delta_conv_w_out', 'delta_conv_b_out', 'delta_kv_norm', 'delta_w_kv', 'delta_k_norm', 'delta_w_q', 'delta_q_norm', 'delta_attn_sinks', 'delta_w_o', 'delta_ffn_w_gate', 'delta_ffn_w_up', 'delta_ffn_w_down', 'new_m_meta_tokens', 'new_m_norm_mix', 'new_m_norm_ffn', 'new_m_conv_w_in', 'new_m_conv_b_in', 'new_m_conv_dw', 'new_m_conv_ln_g', 'new_m_conv_ln_b', 'new_m_conv_w_out', 'new_m_conv_b_out', 'new_m_kv_norm', 'new_m_w_kv', 'new_m_k_norm', 'new_m_w_q', 'new_m_q_norm', 'new_m_attn_sinks', 'new_m_w_o', 'new_m_ffn_w_gate', 'new_m_ffn_w_up', 'new_m_ffn_w_down', 'new_v_meta_tokens', 'new_v_norm_mix', 'new_v_norm_ffn', 'new_v_conv_w_in', 'new_v_conv_b_in', 'new_v_conv_dw', 'new_v_conv_ln_g', 'new_v_conv_ln_b', 'new_v_conv_w_out', 'new_v_conv_b_out', 'new_v_kv_norm', 'new_v_w_kv', 'new_v_k_norm', 'new_v_w_q', 'new_v_q_norm', 'new_v_attn_sinks', 'new_v_w_o', 'new_v_ffn_w_gate', 'new_v_ffn_w_up', 'new_v_ffn_w_down']
TWIN_LEAF_KINDS = {'loss': 'loss', 'grad_x': 'grad_x', 'grad_meta_tokens': 'grad_w', 'grad_norm_mix': 'grad_w', 'grad_norm_ffn': 'grad_w', 'grad_conv_w_in': 'grad_w', 'grad_conv_b_in': 'grad_w', 'grad_conv_dw': 'grad_w', 'grad_conv_ln_g': 'grad_w', 'grad_conv_ln_b': 'grad_w', 'grad_conv_w_out': 'grad_w', 'grad_conv_b_out': 'grad_w', 'grad_kv_norm': 'grad_w', 'grad_w_kv': 'grad_w', 'grad_k_norm': 'grad_w', 'grad_w_q': 'grad_w', 'grad_q_norm': 'grad_w', 'grad_attn_sinks': 'grad_w', 'grad_w_o': 'grad_w', 'grad_ffn_w_gate': 'grad_w', 'grad_ffn_w_up': 'grad_w', 'grad_ffn_w_down': 'grad_w', 'delta_meta_tokens': 'delta_w', 'delta_norm_mix': 'delta_w', 'delta_norm_ffn': 'delta_w', 'delta_conv_w_in': 'delta_w', 'delta_conv_b_in': 'delta_w', 'delta_conv_dw': 'delta_w', 'delta_conv_ln_g': 'delta_w', 'delta_conv_ln_b': 'delta_w', 'delta_conv_w_out': 'delta_w', 'delta_conv_b_out': 'delta_w', 'delta_kv_norm': 'delta_w', 'delta_w_kv': 'delta_w', 'delta_k_norm': 'delta_w', 'delta_w_q': 'delta_w', 'delta_q_norm': 'delta_w', 'delta_attn_sinks': 'delta_w', 'delta_w_o': 'delta_w', 'delta_ffn_w_gate': 'delta_w', 'delta_ffn_w_up': 'delta_w', 'delta_ffn_w_down': 'delta_w', 'new_m_meta_tokens': 'new_m', 'new_m_norm_mix': 'new_m', 'new_m_norm_ffn': 'new_m', 'new_m_conv_w_in': 'new_m', 'new_m_conv_b_in': 'new_m', 'new_m_conv_dw': 'new_m', 'new_m_conv_ln_g': 'new_m', 'new_m_conv_ln_b': 'new_m', 'new_m_conv_w_out': 'new_m', 'new_m_conv_b_out': 'new_m', 'new_m_kv_norm': 'new_m', 'new_m_w_kv': 'new_m', 'new_m_k_norm': 'new_m', 'new_m_w_q': 'new_m', 'new_m_q_norm': 'new_m', 'new_m_attn_sinks': 'new_m', 'new_m_w_o': 'new_m', 'new_m_ffn_w_gate': 'new_m', 'new_m_ffn_w_up': 'new_m', 'new_m_ffn_w_down': 'new_m', 'new_v_meta_tokens': 'new_v', 'new_v_norm_mix': 'new_v', 'new_v_norm_ffn': 'new_v', 'new_v_conv_w_in': 'new_v', 'new_v_conv_b_in': 'new_v', 'new_v_conv_dw': 'new_v', 'new_v_conv_ln_g': 'new_v', 'new_v_conv_ln_b': 'new_v', 'new_v_conv_w_out': 'new_v', 'new_v_conv_b_out': 'new_v', 'new_v_kv_norm': 'new_v', 'new_v_w_kv': 'new_v', 'new_v_k_norm': 'new_v', 'new_v_w_q': 'new_v', 'new_v_q_norm': 'new_v', 'new_v_attn_sinks': 'new_v', 'new_v_w_o': 'new_v', 'new_v_ffn_w_gate': 'new_v', 'new_v_ffn_w_up': 'new_v', 'new_v_ffn_w_down': 'new_v'}


def _forward(args):
    return _fwd_reference(*[args[k] for k in FWD_PARAMS])


def _output_shape():
    out = _jax.eval_shape(lambda: _forward(_fwd_setup_inputs(0)))
    return out.shape, out.dtype

N_MICROBATCH = 1
ADAM_LR = 0.001
ADAM_B1 = 0.9
ADAM_B2 = 0.999
ADAM_EPS = 1e-08
ADAM_WD = 0.01
ADAM_STEP = 10
PER_EXAMPLE_BATCH_AXIS = {'x': 0, 'loss_target': 0}
SHARED_INPUTS = []
_WEIGHT_DTYPES = {'meta_tokens': _jnp.float32, 'norm_mix': _jnp.float32, 'norm_ffn': _jnp.float32, 'conv_w_in': _jnp.float32, 'conv_b_in': _jnp.float32, 'conv_dw': _jnp.float32, 'conv_ln_g': _jnp.float32, 'conv_ln_b': _jnp.float32, 'conv_w_out': _jnp.float32, 'conv_b_out': _jnp.float32, 'kv_norm': _jnp.float32, 'w_kv': _jnp.float32, 'k_norm': _jnp.float32, 'w_q': _jnp.float32, 'q_norm': _jnp.float32, 'attn_sinks': _jnp.float32, 'w_o': _jnp.float32, 'ffn_w_gate': _jnp.float32, 'ffn_w_up': _jnp.float32, 'ffn_w_down': _jnp.float32}
MOMENT_SCALE = {'meta_tokens': 1.102656e-02, 'norm_mix': 5.797920e-02, 'norm_ffn': 3.101444e+00, 'conv_w_in': 5.794845e-02, 'conv_b_in': 5.942917e-01, 'conv_dw': 1.047375e-01, 'conv_ln_g': 1.945838e+00, 'conv_ln_b': 1.523670e+00, 'conv_w_out': 9.381576e-01, 'conv_b_out': 5.225091e+00, 'kv_norm': 3.020282e-01, 'w_kv': 2.874191e-01, 'k_norm': 1.423859e+00, 'w_q': 1.646316e-02, 'q_norm': 7.181515e-01, 'attn_sinks': 2.313615e-02, 'w_o': 3.509426e-01, 'ffn_w_gate': 4.800993e-02, 'ffn_w_up': 5.036767e-02, 'ffn_w_down': 2.336849e-01}


def _to_microbatches(a, axis):
    t = _jnp.moveaxis(a, axis, 0)
    t = t.reshape((N_MICROBATCH, t.shape[0] // N_MICROBATCH) + t.shape[1:])
    return _jnp.moveaxis(t, 1, axis + 1)


def setup_inputs(seed: int = 0) -> dict:
    inp = _fwd_setup_inputs(seed)
    key = _jax.random.fold_in(_jax.random.key(seed), 7919)
    shape, _ = _output_shape()
    out = dict(inp)
    out["loss_target"] = _jax.random.normal(_jax.random.fold_in(key, 0), shape, _jnp.float32)
    for i, name in enumerate(TWIN_WEIGHTS):
        w = inp[name].astype(_jnp.float32)
        if MOMENT_SCALE is None:
            s = _jnp.sqrt(_jnp.mean(_jnp.square(w)) + 1e-30)
        else:
            s = MOMENT_SCALE[name]
        km, kv = _jax.random.split(_jax.random.fold_in(key, i + 1))
        out[name] = w
        out["m_" + name] = s * _jax.random.normal(km, w.shape, _jnp.float32)
        out["v_" + name] = (s * s) * _jax.random.uniform(kv, w.shape, _jnp.float32, 0.5, 1.5)
    if N_MICROBATCH > 1:
        for name, axis in PER_EXAMPLE_BATCH_AXIS.items():
            out[name] = _to_microbatches(out[name], axis)
    return {'x': out['x'], 'meta_tokens': out['meta_tokens'], 'norm_mix': out['norm_mix'], 'norm_ffn': out['norm_ffn'], 'conv_w_in': out['conv_w_in'], 'conv_b_in': out['conv_b_in'], 'conv_dw': out['conv_dw'], 'conv_ln_g': out['conv_ln_g'], 'conv_ln_b': out['conv_ln_b'], 'conv_w_out': out['conv_w_out'], 'conv_b_out': out['conv_b_out'], 'kv_norm': out['kv_norm'], 'w_kv': out['w_kv'], 'k_norm': out['k_norm'], 'w_q': out['w_q'], 'q_norm': out['q_norm'], 'attn_sinks': out['attn_sinks'], 'w_o': out['w_o'], 'ffn_w_gate': out['ffn_w_gate'], 'ffn_w_up': out['ffn_w_up'], 'ffn_w_down': out['ffn_w_down'], 'loss_target': out['loss_target'], 'm_meta_tokens': out['m_meta_tokens'], 'm_norm_mix': out['m_norm_mix'], 'm_norm_ffn': out['m_norm_ffn'], 'm_conv_w_in': out['m_conv_w_in'], 'm_conv_b_in': out['m_conv_b_in'], 'm_conv_dw': out['m_conv_dw'], 'm_conv_ln_g': out['m_conv_ln_g'], 'm_conv_ln_b': out['m_conv_ln_b'], 'm_conv_w_out': out['m_conv_w_out'], 'm_conv_b_out': out['m_conv_b_out'], 'm_kv_norm': out['m_kv_norm'], 'm_w_kv': out['m_w_kv'], 'm_k_norm': out['m_k_norm'], 'm_w_q': out['m_w_q'], 'm_q_norm': out['m_q_norm'], 'm_attn_sinks': out['m_attn_sinks'], 'm_w_o': out['m_w_o'], 'm_ffn_w_gate': out['m_ffn_w_gate'], 'm_ffn_w_up': out['m_ffn_w_up'], 'm_ffn_w_down': out['m_ffn_w_down'], 'v_meta_tokens': out['v_meta_tokens'], 'v_norm_mix': out['v_norm_mix'], 'v_norm_ffn': out['v_norm_ffn'], 'v_conv_w_in': out['v_conv_w_in'], 'v_conv_b_in': out['v_conv_b_in'], 'v_conv_dw': out['v_conv_dw'], 'v_conv_ln_g': out['v_conv_ln_g'], 'v_conv_ln_b': out['v_conv_ln_b'], 'v_conv_w_out': out['v_conv_w_out'], 'v_conv_b_out': out['v_conv_b_out'], 'v_kv_norm': out['v_kv_norm'], 'v_w_kv': out['v_w_kv'], 'v_k_norm': out['v_k_norm'], 'v_w_q': out['v_w_q'], 'v_q_norm': out['v_q_norm'], 'v_attn_sinks': out['v_attn_sinks'], 'v_w_o': out['v_w_o'], 'v_ffn_w_gate': out['v_ffn_w_gate'], 'v_ffn_w_up': out['v_ffn_w_up'], 'v_ffn_w_down': out['v_ffn_w_down']}


def _loss(weights, diff, rest, loss_target):
    with _jax.named_scope("forward"):
        args = {**rest, TWIN_DIFF_INPUT: diff, **{k: w.astype(_WEIGHT_DTYPES[k]) for k, w in weights.items()}}
        y = _forward(args)
    with _jax.named_scope("loss_head"):
        err = _jnp.square(y.astype(_jnp.float32) - loss_target)
        return 0.5 * _jnp.sum(_jnp.mean(err, axis=-1)) if err.ndim else 0.5 * err


def _adamw(w, g, m, v):
    m = ADAM_B1 * m + (1.0 - ADAM_B1) * g
    v = ADAM_B2 * v + (1.0 - ADAM_B2) * _jnp.square(g)
    m_hat = m / (1.0 - ADAM_B1 ** ADAM_STEP)
    v_hat = v / (1.0 - ADAM_B2 ** ADAM_STEP)
    delta = -ADAM_LR * (m_hat / (_jnp.sqrt(v_hat) + ADAM_EPS) + ADAM_WD * w)
    return delta, m, v


def reference(x, meta_tokens, norm_mix, norm_ffn, conv_w_in, conv_b_in, conv_dw, conv_ln_g, conv_ln_b, conv_w_out, conv_b_out, kv_norm, w_kv, k_norm, w_q, q_norm, attn_sinks, w_o, ffn_w_gate, ffn_w_up, ffn_w_down, loss_target, m_meta_tokens, m_norm_mix, m_norm_ffn, m_conv_w_in, m_conv_b_in, m_conv_dw, m_conv_ln_g, m_conv_ln_b, m_conv_w_out, m_conv_b_out, m_kv_norm, m_w_kv, m_k_norm, m_w_q, m_q_norm, m_attn_sinks, m_w_o, m_ffn_w_gate, m_ffn_w_up, m_ffn_w_down, v_meta_tokens, v_norm_mix, v_norm_ffn, v_conv_w_in, v_conv_b_in, v_conv_dw, v_conv_ln_g, v_conv_ln_b, v_conv_w_out, v_conv_b_out, v_kv_norm, v_w_kv, v_k_norm, v_w_q, v_q_norm, v_attn_sinks, v_w_o, v_ffn_w_gate, v_ffn_w_up, v_ffn_w_down):
    given = dict(x=x, meta_tokens=meta_tokens, norm_mix=norm_mix, norm_ffn=norm_ffn, conv_w_in=conv_w_in, conv_b_in=conv_b_in, conv_dw=conv_dw, conv_ln_g=conv_ln_g, conv_ln_b=conv_ln_b, conv_w_out=conv_w_out, conv_b_out=conv_b_out, kv_norm=kv_norm, w_kv=w_kv, k_norm=k_norm, w_q=w_q, q_norm=q_norm, attn_sinks=attn_sinks, w_o=w_o, ffn_w_gate=ffn_w_gate, ffn_w_up=ffn_w_up, ffn_w_down=ffn_w_down, loss_target=loss_target, m_meta_tokens=m_meta_tokens, m_norm_mix=m_norm_mix, m_norm_ffn=m_norm_ffn, m_conv_w_in=m_conv_w_in, m_conv_b_in=m_conv_b_in, m_conv_dw=m_conv_dw, m_conv_ln_g=m_conv_ln_g, m_conv_ln_b=m_conv_ln_b, m_conv_w_out=m_conv_w_out, m_conv_b_out=m_conv_b_out, m_kv_norm=m_kv_norm, m_w_kv=m_w_kv, m_k_norm=m_k_norm, m_w_q=m_w_q, m_q_norm=m_q_norm, m_attn_sinks=m_attn_sinks, m_w_o=m_w_o, m_ffn_w_gate=m_ffn_w_gate, m_ffn_w_up=m_ffn_w_up, m_ffn_w_down=m_ffn_w_down, v_meta_tokens=v_meta_tokens, v_norm_mix=v_norm_mix, v_norm_ffn=v_norm_ffn, v_conv_w_in=v_conv_w_in, v_conv_b_in=v_conv_b_in, v_conv_dw=v_conv_dw, v_conv_ln_g=v_conv_ln_g, v_conv_ln_b=v_conv_ln_b, v_conv_w_out=v_conv_w_out, v_conv_b_out=v_conv_b_out, v_kv_norm=v_kv_norm, v_w_kv=v_w_kv, v_k_norm=v_k_norm, v_w_q=v_w_q, v_q_norm=v_q_norm, v_attn_sinks=v_attn_sinks, v_w_o=v_w_o, v_ffn_w_gate=v_ffn_w_gate, v_ffn_w_up=v_ffn_w_up, v_ffn_w_down=v_ffn_w_down)
    weights = {n: given[n] for n in TWIN_WEIGHTS}
    shared = {n: given[n] for n in SHARED_INPUTS}
    per_example = {n: given[n] for n in ['x']}
    grad_fn = _jax.value_and_grad(_loss, argnums=(0, 1))

    def one_microbatch(ex, loss_target):
        ex = dict(ex)
        diff = ex.pop(TWIN_DIFF_INPUT)
        return grad_fn(weights, diff, {**shared, **ex}, loss_target)

    if N_MICROBATCH == 1:
        loss, (grad_w, grad_x) = one_microbatch(per_example, given["loss_target"])
    else:
        def body(carry, xs):
            loss_sum, grad_sum = carry
            l_k, (gw_k, gx_k) = one_microbatch(xs[0], xs[1])
            with _jax.named_scope("update"):
                return (loss_sum + l_k, _jax.tree.map(_jnp.add, grad_sum, gw_k)), gx_k

        init = (_jnp.zeros((), _jnp.float32), _jax.tree.map(_jnp.zeros_like, weights))
        (loss, grad_w), grad_x = _jax.lax.scan(body, init, (per_example, given["loss_target"]))
    with _jax.named_scope("update"):
        delta_w, new_m, new_v = {}, {}, {}
        for n in TWIN_WEIGHTS:
            delta_w[n], new_m[n], new_v[n] = _adamw(weights[n], grad_w[n], given["m_" + n], given["v_" + n])
    return (loss, grad_x, *[grad_w[n] for n in TWIN_WEIGHTS], *[delta_w[n] for n in TWIN_WEIGHTS],
            *[new_m[n] for n in TWIN_WEIGHTS], *[new_v[n] for n in TWIN_WEIGHTS])
```

```python
import functools

import jax
import jax.numpy as jnp
from jax import lax
from jax.experimental import pallas as pl
from jax.experimental.pallas import tpu as pltpu

F32 = jnp.float32
BF = jnp.bfloat16

D = 1024
DFF = 2816
NH = 16
NKV = 4
HD = 64
KVD = NKV * HD
NMETA = 16
CW = 31
HALO = 32
CHUNK = 32
QB = 128
EPS = 1e-6
NEG = -1e30
NDEV = 8
SCALE = HD ** -0.5

LR, B1, B2, AEPS, WD, STEP = 0.001, 0.9, 0.999, 1e-08, 0.01, 10

VMEM_LIMIT = 56 * 2 ** 20
MESH = pl.DeviceIdType.MESH


def _cp(n):
    return pltpu.CompilerParams(dimension_semantics=("arbitrary",) * n, vmem_limit_bytes=VMEM_LIMIT)


def _row(tm, c):
    return pl.BlockSpec((tm, c), lambda i: (i, 0))


def _res(shape):
    return pl.BlockSpec(shape, lambda i: (0,) * len(shape), pipeline_mode=pl.Buffered(1))


def _lay(l, shape):
    return pl.BlockSpec((None,) + tuple(shape), lambda i: (l,) + (0,) * len(shape), pipeline_mode=pl.Buffered(1))


def _acc(shape):
    return pl.BlockSpec(shape, lambda i: (0,) * len(shape))


def _sds(shape, dt):
    return jax.ShapeDtypeStruct(tuple(shape), dt)


def _dot(a, b):
    return jnp.dot(a.astype(BF), b.astype(BF), preferred_element_type=F32)


def _dot_nt(a, b):
    return lax.dot_general(a.astype(BF), b.astype(BF), (((1,), (1,)), ((), ())), preferred_element_type=F32)


def _dot_tn(a, b):
    return lax.dot_general(a.astype(BF), b.astype(BF), (((0,), (0,)), ((), ())), preferred_element_type=F32)


def _rstd(x):
    return lax.rsqrt(jnp.mean(x * x, axis=-1, keepdims=True) + EPS)


def _rms_bwd(x, g, dy):
    r = _rstd(x)
    z = dy * g
    dx = r * z - x * (r * r * r * jnp.mean(z * x, axis=-1, keepdims=True))
    return dx, jnp.sum(dy * x * r, axis=0, keepdims=True)


def _sig(x):
    return jax.nn.sigmoid(x)


def _fold8(x):
    out = x[0:8]
    for k in range(1, x.shape[0] // 8):
        out = out + x[8 * k:8 * k + 8]
    return out


def _init(ref, first):
    @pl.when(first)
    def _():
        ref[...] = jnp.zeros_like(ref)


def embed(x, meta8, lp):
    bl, seq, _ = x.shape
    npad = lp - NMETA - seq

    def body(x_ref, m_ref, h_ref, mbuf, zbuf, sems):
        mbuf[...] = jnp.concatenate([m_ref[p] for p in range(NDEV)], axis=1)
        zbuf[...] = jnp.zeros_like(zbuf)
        cps = []
        for b in range(bl):
            cps.append(pltpu.make_async_copy(x_ref.at[b], h_ref.at[b, pl.ds(NMETA, seq)], sems.at[3 * b]))
            cps.append(pltpu.make_async_copy(mbuf, h_ref.at[b, pl.ds(0, NMETA)], sems.at[3 * b + 1]))
            cps.append(pltpu.make_async_copy(zbuf, h_ref.at[b, pl.ds(NMETA + seq, npad)], sems.at[3 * b + 2]))
        for c in cps:
            c.start()
        for c in cps:
            c.wait()

    return pl.pallas_call(
        body, name="embed", out_shape=_sds((bl, lp, D), F32),
        in_specs=[pl.BlockSpec(memory_space=pl.ANY), pl.BlockSpec(memory_space=pltpu.VMEM)],
        out_specs=pl.BlockSpec(memory_space=pl.ANY),
        scratch_shapes=[pltpu.VMEM((NMETA, D), F32), pltpu.VMEM((npad, D), F32), pltpu.SemaphoreType.DMA((3 * bl,))],
    )(x, meta8)


def conv_in_fwd(h, nm, l, w_in, b_in, i, tm):
    t = h.shape[0]

    def body(h_ref, g_ref, w_ref, b_ref, u_ref, big_ref, a_ref):
        x = h_ref[...]
        ub = (x * _rstd(x) * g_ref[...]).astype(BF)
        u_ref[...] = ub
        big = jnp.dot(ub, w_ref[...], preferred_element_type=F32) + b_ref[...]
        big_ref[...] = big
        a_ref[...] = big[:, :D] * _sig(big[:, D:])

    return pl.pallas_call(
        body, name=f"conv_in_fwd{i}", grid=(t // tm,),
        in_specs=[_row(tm, D), _lay(l, (1, D)), _lay(i, (D, 2 * D)), _lay(i, (1, 2 * D))],
        out_specs=[_row(tm, D), _row(tm, 2 * D), _row(tm, D)],
        out_shape=[_sds((t, D), BF), _sds((t, 2 * D), F32), _sds((t, D), F32)],
        compiler_params=_cp(1),
    )(h, nm, w_in, b_in)


def _prev_halo(tm):
    return pl.BlockSpec((HALO, D), lambda i: (jnp.maximum(i * (tm // HALO) - 1, 0), 0))


def _next_halo(tm, t):
    return pl.BlockSpec((HALO, D), lambda i: (jnp.minimum((i + 1) * (tm // HALO), t // HALO - 1), 0))


def conv_mid_fwd(a, dw, ln_g, ln_b, i, tm, tpb):
    t = a.shape[0]

    def body(a_ref, halo_ref, dw_ref, g_ref, b_ref, c_ref, s_ref, ext):
        first = pl.program_id(0) % tpb == 0
        ext[0:HALO] = jnp.where(first, 0.0, halo_ref[...])
        ext[HALO:] = a_ref[...]

        def chunk(k, carry):
            r0 = pl.multiple_of(k * CHUNK, CHUNK)
            win = ext[pl.ds(r0, 2 * CHUNK), :]
            c = jnp.zeros((CHUNK, D), F32)
            for j in range(CW):
                c = c + dw_ref[j:j + 1, :] * win[j + 2:j + 2 + CHUNK]
            c_ref[pl.ds(r0, CHUNK), :] = c
            mu = jnp.mean(c, axis=-1, keepdims=True)
            xc = c - mu
            n = xc * lax.rsqrt(jnp.mean(xc * xc, axis=-1, keepdims=True) + EPS) * g_ref[...] + b_ref[...]
            s_ref[pl.ds(r0, CHUNK), :] = (n * _sig(n)).astype(BF)
            return carry

        lax.fori_loop(0, tm // CHUNK, chunk, 0)

    return pl.pallas_call(
        body, name=f"conv_mid_fwd{i}", grid=(t // tm,),
        in_specs=[_row(tm, D), _prev_halo(tm), _lay(i, (CW, D)), _lay(i, (1, D)), _lay(i, (1, D))],
        out_specs=[_row(tm, D), _row(tm, D)],
        out_shape=[_sds((t, D), F32), _sds((t, D), BF)],
        scratch_shapes=[pltpu.VMEM((tm + HALO, D), F32)],
        compiler_params=_cp(1),
    )(a, a, dw, ln_g, ln_b)


def mixer_out_fwd(h, s, w, lw, bias, nf, l, tm, name):
    t = h.shape[0]

    def body(*refs):
        if bias is None:
            h_ref, s_ref, w_ref, g_ref, h1_ref, u_ref = refs
            y = 0.0
        else:
            h_ref, s_ref, w_ref, b_ref, g_ref, h1_ref, u_ref = refs
            y = b_ref[...]
        h1 = h_ref[...] + (jnp.dot(s_ref[...], w_ref[...], preferred_element_type=F32) + y)
        h1_ref[...] = h1
        u_ref[...] = (h1 * _rstd(h1) * g_ref[...]).astype(BF)

    ins = [h, s, w] + ([] if bias is None else [bias]) + [nf]
    specs = [_row(tm, D), _row(tm, D), _lay(lw, (D, D))] + ([] if bias is None else [_lay(lw, (1, D))]) + [_lay(l, (1, D))]
    return pl.pallas_call(
        body, name=name, grid=(t // tm,), in_specs=specs,
        out_specs=[_row(tm, D), _row(tm, D)], out_shape=[_sds((t, D), F32), _sds((t, D), BF)],
        compiler_params=_cp(1),
    )(*ins)


def ffn_up_fwd(u, wg, wu, l, tm):
    t = u.shape[0]

    def body(u_ref, wg_ref, wu_ref, g_ref, up_ref, hid_ref):
        ub = u_ref[...]
        g = jnp.dot(ub, wg_ref[...], preferred_element_type=F32)
        up = jnp.dot(ub, wu_ref[...], preferred_element_type=F32)
        g_ref[...] = g.astype(BF)
        up_ref[...] = up.astype(BF)
        hid_ref[...] = (g * _sig(g) * up).astype(BF)

    return pl.pallas_call(
        body, name=f"ffn_up_fwd{l}", grid=(t // tm,),
        in_specs=[_row(tm, D), _lay(l, (D, DFF)), _lay(l, (D, DFF))],
        out_specs=[_row(tm, DFF)] * 3, out_shape=[_sds((t, DFF), BF)] * 3,
        compiler_params=_cp(1),
    )(u, wg, wu)


def ffn_down_fwd(hid, h1, wd, l, tm):
    t = h1.shape[0]

    def body(hid_ref, h1_ref, w_ref, h2_ref):
        h2_ref[...] = h1_ref[...] + jnp.dot(hid_ref[...], w_ref[...], preferred_element_type=F32)

    return pl.pallas_call(
        body, name=f"ffn_down_fwd{l}", grid=(t // tm,),
        in_specs=[_row(tm, DFF), _row(tm, D), _lay(l, (DFF, D))],
        out_specs=_row(tm, D), out_shape=_sds((t, D), F32),
        compiler_params=_cp(1),
    )(hid, h1, wd)


def _seg_rms(x, g, nseg):
    outs = []
    for s in range(nseg):
        xs = x[:, HD * s:HD * s + HD]
        outs.append(xs * _rstd(xs) * g)
    return jnp.concatenate(outs, axis=1)


def kv_fwd(h, kvn, w_kv, kng, tm):
    t = h.shape[0]

    def body(h_ref, g_ref, w_ref, kg_ref, kn_ref, kv_ref, k_ref, v_ref):
        x = h_ref[...]
        kn = (x * _rstd(x) * g_ref[...]).astype(BF)
        kn_ref[...] = kn
        kv = jnp.dot(kn, w_ref[...], preferred_element_type=F32)
        kv_ref[...] = kv
        k_ref[...] = _seg_rms(kv[:, :KVD], kg_ref[...], NKV).astype(BF)
        v_ref[...] = kv[:, KVD:].astype(BF)

    return pl.pallas_call(
        body, name="kv_fwd", grid=(t // tm,),
        in_specs=[_row(tm, D), _res((1, D)), _res((D, 2 * KVD)), _res((1, HD))],
        out_specs=[_row(tm, D), _row(tm, 2 * KVD), _row(tm, KVD), _row(tm, KVD)],
        out_shape=[_sds((t, D), BF), _sds((t, 2 * KVD), F32), _sds((t, KVD), BF), _sds((t, KVD), BF)],
        compiler_params=_cp(1),
    )(h, kvn, w_kv, kng)


def q_fwd(h, nm, l, w_q, j, tm):
    t = h.shape[0]

    def body(h_ref, g_ref, w_ref, u_ref, q_ref):
        x = h_ref[...]
        ub = (x * _rstd(x) * g_ref[...]).astype(BF)
        u_ref[...] = ub
        q_ref[...] = jnp.dot(ub, w_ref[...], preferred_element_type=F32)

    return pl.pallas_call(
        body, name=f"q_fwd{j}", grid=(t // tm,),
        in_specs=[_row(tm, D), _lay(l, (1, D)), _lay(j, (D, D))],
        out_specs=[_row(tm, D), _row(tm, D)], out_shape=[_sds((t, D), BF), _sds((t, D), F32)],
        compiler_params=_cp(1),
    )(h, nm, w_q)


def _attn_specs(nb, lp):
    cur = lambda c: pl.BlockSpec((QB, c), lambda b, n: (b * nb + n, 0))
    prev = lambda c: pl.BlockSpec((QB, c), lambda b, n: (b * nb + jnp.maximum(n - 1, 0), 0))
    meta = lambda c: pl.BlockSpec((NMETA, c), lambda b, n: (b * (lp // NMETA), 0))
    return cur, prev, meta


def _attn_masks(n):
    qi = lax.broadcasted_iota(jnp.int32, (QB, QB), 0)
    kj = lax.broadcasted_iota(jnp.int32, (QB, QB), 1)
    m_cur = (kj <= qi) & (n * QB + kj >= NMETA)
    m_prev = (kj > qi) & ((n - 1) * QB + kj >= NMETA)
    qm = lax.broadcasted_iota(jnp.int32, (QB, NMETA), 0)
    km = lax.broadcasted_iota(jnp.int32, (QB, NMETA), 1)
    m_meta = km <= n * QB + qm
    return m_cur, m_prev, m_meta


def attn_fwd(q, k, v, qg, sinks, j, bl, lp):
    t = q.shape[0]
    nb = lp // QB
    cur, prev, meta = _attn_specs(nb, lp)

    def body(q_ref, kc_ref, kp_ref, km_ref, vc_ref, vp_ref, vm_ref, qg_ref, sk_ref, o_ref, lse_ref):
        n = pl.program_id(1)
        m_cur, m_prev, m_meta = _attn_masks(n)
        lane = lax.broadcasted_iota(jnp.int32, (QB, NH), 1)
        lse = jnp.zeros((QB, NH), F32)
        for h in range(NH):
            g = h // (NH // NKV)
            hs, gs = slice(HD * h, HD * h + HD), slice(HD * g, HD * g + HD)
            qh = q_ref[:, hs]
            qn = (qh * _rstd(qh) * qg_ref[...]).astype(BF)
            s_c = jnp.where(m_cur, _dot_nt(qn, kc_ref[:, gs]) * SCALE, NEG)
            s_p = jnp.where(m_prev, _dot_nt(qn, kp_ref[:, gs]) * SCALE, NEG)
            s_m = jnp.where(m_meta, _dot_nt(qn, km_ref[:, gs]) * SCALE, NEG)
            sink = sk_ref[:, h:h + 1]
            mx = jnp.maximum(jnp.maximum(jnp.max(s_c, -1, keepdims=True), jnp.max(s_p, -1, keepdims=True)),
                             jnp.maximum(jnp.max(s_m, -1, keepdims=True), sink))
            p_c, p_p, p_m = jnp.exp(s_c - mx), jnp.exp(s_p - mx), jnp.exp(s_m - mx)
            den = (jnp.sum(p_c, -1, keepdims=True) + jnp.sum(p_p, -1, keepdims=True)
                   + jnp.sum(p_m, -1, keepdims=True) + jnp.exp(sink - mx))
            inv = 1.0 / den
            o = _dot(p_c * inv, vc_ref[:, gs]) + _dot(p_p * inv, vp_ref[:, gs]) + _dot(p_m * inv, vm_ref[:, gs])
            o_ref[:, hs] = o.astype(BF)
            lse = jnp.where(lane == h, mx + jnp.log(den), lse)
        lse_ref[...] = lse

    return pl.pallas_call(
        body, name=f"attn_fwd{j}", grid=(bl, nb),
        in_specs=[cur(D), cur(KVD), prev(KVD), meta(KVD), cur(KVD), prev(KVD), meta(KVD),
                  pl.BlockSpec((None, 1, HD), lambda b, n: (j, 0, 0)), pl.BlockSpec((None, 1, NH), lambda b, n: (j, 0, 0))],
        out_specs=[cur(D), cur(NH)], out_shape=[_sds((t, D), BF), _sds((t, NH), F32)],
        compiler_params=_cp(2),
    )(q, k, k, k, v, v, v, qg, sinks)


def loss_fwd(h, tgt):
    bl, lp, _ = h.shape
    seq = tgt.shape[1]
    cb = 256

    def body(h_ref, t_ref, dh_ref, loss_ref):
        _init(loss_ref, (pl.program_id(0) == 0) & (pl.program_id(1) == 0))
        err = h_ref[NMETA:NMETA + seq, :] - t_ref[...]
        dh_ref[...] = jnp.zeros_like(dh_ref)
        dh_ref[NMETA:NMETA + seq, :] = err * (1.0 / D)
        loss_ref[...] += (0.5 / D) * jnp.sum(err * err)

    return pl.pallas_call(
        body, name="loss_fwd", grid=(bl, D // cb),
        in_specs=[pl.BlockSpec((None, lp, cb), lambda b, c: (b, 0, c)), pl.BlockSpec((None, seq, cb), lambda b, c: (b, 0, c))],
        out_specs=[pl.BlockSpec((None, lp, cb), lambda b, c: (b, 0, c)), pl.BlockSpec((8, 128), lambda b, c: (0, 0))],
        out_shape=[_sds((bl, lp, D), F32), _sds((8, 128), F32)],
        compiler_params=_cp(2),
    )(h, tgt)


def ffn_bwd_x(dh2, g, up, h1, nf, l, wd, wg, wu, tm):
    t = dh2.shape[0]

    def body(dh2_ref, g_ref, up_ref, h1_ref, nf_ref, wd_ref, wg_ref, wu_ref, dg_ref, du_ref, dh1_ref, dnf_ref):
        _init(dnf_ref, pl.program_id(0) == 0)
        dh2v = dh2_ref[...]
        dhid = _dot_nt(dh2v, wd_ref[...])
        gv = g_ref[...].astype(F32)
        uv = up_ref[...].astype(F32)
        sg = _sig(gv)
        dgv = (dhid * uv * (sg * (1.0 + gv * (1.0 - sg)))).astype(BF)
        duv = (dhid * (gv * sg)).astype(BF)
        dg_ref[...] = dgv
        du_ref[...] = duv
        dnorm = _dot_nt(dgv, wg_ref[...]) + _dot_nt(duv, wu_ref[...])
        dx, dnf = _rms_bwd(h1_ref[...], nf_ref[...], dnorm)
        dh1_ref[...] = dh2v + dx
        dnf_ref[...] += dnf

    return pl.pallas_call(
        body, name=f"ffn_bwd_x{l}", grid=(t // tm,),
        in_specs=[_row(tm, D), _row(tm, DFF), _row(tm, DFF), _row(tm, D), _lay(l, (1, D)),
                  _lay(l, (DFF, D)), _lay(l, (D, DFF)), _lay(l, (D, DFF))],
        out_specs=[_row(tm, DFF), _row(tm, DFF), _row(tm, D), _acc((1, D))],
        out_shape=[_sds((t, DFF), BF), _sds((t, DFF), BF), _sds((t, D), F32), _sds((1, D), F32)],
        compiler_params=_cp(1),
    )(dh2, g, up, h1, nf, wd, wg, wu)


def mm_tn(x, dy, tm, name, split=False):
    t, kk = x.shape
    nn = dy.shape[1]
    n8 = nn // NDEV
    nsteps = t // tm

    def body(x_ref, dy_ref, o_ref, acc):
        i = pl.program_id(0)
        _init(acc, i == 0)
        acc[...] += _dot_tn(x_ref[...], dy_ref[...])

        @pl.when(i == nsteps - 1)
        def _():
            if split:
                for p in range(NDEV):
                    o_ref[p] = acc[:, p * n8:(p + 1) * n8].astype(BF)
            else:
                o_ref[...] = acc[...].astype(BF)

    oshape = (NDEV, kk, n8) if split else (kk, nn)
    return pl.pallas_call(
        body, name=name, grid=(nsteps,), in_specs=[_row(tm, kk), _row(tm, nn)],
        out_specs=_acc(oshape), out_shape=_sds(oshape, BF), scratch_shapes=[pltpu.VMEM((kk, nn), F32)],
        compiler_params=_cp(1),
    )(x, dy)


def proj_bwd(dy, w, lw, wshape, h, g, lg, dh_in, tm, name):
    t = h.shape[0]
    nn = dy.shape[1]
    wspec = _res(wshape) if lw is None else _lay(lw, wshape)
    gspec = _res((1, D)) if lg is None else _lay(lg, (1, D))

    def body(dy_ref, w_ref, h_ref, g_ref, dhin_ref, dh_ref, dg_ref):
        _init(dg_ref, pl.program_id(0) == 0)
        du = _dot_nt(dy_ref[...], w_ref[...])
        dx, dg = _rms_bwd(h_ref[...], g_ref[...], du)
        dh_ref[...] = dhin_ref[...] + dx
        dg_ref[...] += dg

    return pl.pallas_call(
        body, name=name, grid=(t // tm,),
        in_specs=[_row(tm, nn), wspec, _row(tm, D), gspec, _row(tm, D)],
        out_specs=[_row(tm, D), _acc((1, D))], out_shape=[_sds((t, D), F32), _sds((1, D), F32)],
        compiler_params=_cp(1),
    )(dy, w, h, g, dh_in)


def out_proj_bwd(dh1, w, lw, tm, name):
    t = dh1.shape[0]

    def body(dh1_ref, w_ref, do_ref):
        do_ref[...] = _dot_nt(dh1_ref[...], w_ref[...]).astype(BF)

    return pl.pallas_call(
        body, name=name, grid=(t // tm,), in_specs=[_row(tm, D), _lay(lw, (D, D))],
        out_specs=_row(tm, D), out_shape=_sds((t, D), BF), compiler_params=_cp(1),
    )(dh1, w)


def attn_bwd(q, k, v, do, o, lse, qg, sinks, j, bl, lp):
    t = q.shape[0]
    nb = lp // QB
    cur, prev, meta = _attn_specs(nb, lp)
    rq = NH // NKV

    def body(q_ref, kc_ref, kp_ref, km_ref, vc_ref, vp_ref, vm_ref, do_ref, o_ref, lse_ref, qg_ref, sk_ref,
             dq_ref, dk_ref, dv_ref, dqg_ref, dsk_ref):
        b, n = pl.program_id(0), pl.program_id(1)
        _init(dk_ref, n == 0)
        _init(dv_ref, n == 0)
        _init(dqg_ref, (b == 0) & (n == 0))
        _init(dsk_ref, (b == 0) & (n == 0))
        m_cur, m_prev, m_meta = _attn_masks(n)
        lane = lax.broadcasted_iota(jnp.int32, (1, NH), 1)
        dqg = jnp.zeros((1, HD), F32)
        dsk = jnp.zeros((1, NH), F32)
        dk_c, dk_p, dk_m, dv_c, dv_p, dv_m = [], [], [], [], [], []
        for g in range(NKV):
            gs = slice(HD * g, HD * g + HD)
            kc, kp, km = kc_ref[:, gs], kp_ref[:, gs], km_ref[:, gs]
            vc, vp, vm = vc_ref[:, gs], vp_ref[:, gs], vm_ref[:, gs]
            acc = [0.0] * 6
            for r in range(rq):
                h = g * rq + r
                hs = slice(HD * h, HD * h + HD)
                qh = q_ref[:, hs]
                rs = _rstd(qh)
                qn = (qh * rs * qg_ref[...]).astype(BF)
                ls = lse_ref[:, h:h + 1]
                p_c = jnp.where(m_cur, jnp.exp(_dot_nt(qn, kc) * SCALE - ls), 0.0)
                p_p = jnp.where(m_prev, jnp.exp(_dot_nt(qn, kp) * SCALE - ls), 0.0)
                p_m = jnp.where(m_meta, jnp.exp(_dot_nt(qn, km) * SCALE - ls), 0.0)
                doh = do_ref[:, hs]
                delta = jnp.sum(doh.astype(F32) * o_ref[:, hs].astype(F32), axis=-1, keepdims=True)
                ds_c = (p_c * (_dot_nt(doh, vc) - delta)).astype(BF)
                ds_p = (p_p * (_dot_nt(doh, vp) - delta)).astype(BF)
                ds_m = (p_m * (_dot_nt(doh, vm) - delta)).astype(BF)
                dqn = (_dot(ds_c, kc) + _dot(ds_p, kp) + _dot(ds_m, km)) * SCALE
                new = [_dot_tn(ds_c, qn) * SCALE, _dot_tn(ds_p, qn) * SCALE, _dot_tn(ds_m, qn) * SCALE,
                       _dot_tn(p_c, doh), _dot_tn(p_p, doh), _dot_tn(p_m, doh)]
                acc = [a + x for a, x in zip(acc, new)]
                sink = sk_ref[:, h:h + 1]
                dsk = dsk + jnp.where(lane == h, -jnp.sum(jnp.exp(sink - ls) * delta), 0.0)
                z = dqn * qg_ref[...]
                dq_ref[:, hs] = rs * z - qh * (rs * rs * rs * jnp.mean(z * qh, axis=-1, keepdims=True))
                dqg = dqg + jnp.sum(dqn * qh * rs, axis=0, keepdims=True)
            for lst, x in zip((dk_c, dk_p, dk_m, dv_c, dv_p, dv_m), acc):
                lst.append(x)
        cat = lambda xs: jnp.concatenate(xs, axis=1)
        r_cur = pl.ds(pl.multiple_of(n * QB, QB), QB)
        r_prev = pl.ds(pl.multiple_of(jnp.maximum(n - 1, 0) * QB, QB), QB)
        dk_ref[r_cur, :] += cat(dk_c)
        dv_ref[r_cur, :] += cat(dv_c)
        dk_ref[r_prev, :] += cat(dk_p)
        dv_ref[r_prev, :] += cat(dv_p)
        dk_ref[0:NMETA, :] += cat(dk_m)
        dv_ref[0:NMETA, :] += cat(dv_m)
        dqg_ref[...] += dqg
        dsk_ref[...] += dsk

    kvspec = pl.BlockSpec((None, lp, KVD), lambda b, n: (b, 0, 0))
    return pl.pallas_call(
        body, name=f"attn_bwd{j}", grid=(bl, nb),
        in_specs=[cur(D), cur(KVD), prev(KVD), meta(KVD), cur(KVD), prev(KVD), meta(KVD), cur(D), cur(D), cur(NH),
                  pl.BlockSpec((None, 1, HD), lambda b, n: (j, 0, 0)), pl.BlockSpec((None, 1, NH), lambda b, n: (j, 0, 0))],
        out_specs=[cur(D), kvspec, kvspec, pl.BlockSpec((1, HD), lambda b, n: (0, 0)), pl.BlockSpec((1, NH), lambda b, n: (0, 0))],
        out_shape=[_sds((t, D), F32), _sds((bl, lp, KVD), F32), _sds((bl, lp, KVD), F32), _sds((1, HD), F32), _sds((1, NH), F32)],
        compiler_params=_cp(2),
    )(q, k, k, k, v, v, v, do, o, lse, qg, sinks)


def kv_bwd_pre(dk0, dk1, dv0, dv1, kv, kng, tm):
    t = kv.shape[0]

    def body(dk0_ref, dk1_ref, dv0_ref, dv1_ref, kv_ref, g_ref, dkv_ref, dg_ref):
        _init(dg_ref, pl.program_id(0) == 0)
        dk = dk0_ref[...] + dk1_ref[...]
        dg = jnp.zeros((1, HD), F32)
        outs = []
        for s in range(NKV):
            sl = slice(HD * s, HD * s + HD)
            dx, dgs = _rms_bwd(kv_ref[:, sl], g_ref[...], dk[:, sl])
            outs.append(dx)
            dg = dg + dgs
        dkv_ref[:, :KVD] = jnp.concatenate(outs, axis=1).astype(BF)
        dkv_ref[:, KVD:] = (dv0_ref[...] + dv1_ref[...]).astype(BF)
        dg_ref[...] += dg

    return pl.pallas_call(
        body, name="kv_bwd_pre", grid=(t // tm,),
        in_specs=[_row(tm, KVD)] * 4 + [_row(tm, 2 * KVD), _res((1, HD))],
        out_specs=[_row(tm, 2 * KVD), _acc((1, HD))], out_shape=[_sds((t, 2 * KVD), BF), _sds((1, HD), F32)],
        compiler_params=_cp(1),
    )(dk0, dk1, dv0, dv1, kv, kng)


def conv_out_bwd(dh1, c, ln_g, ln_b, w_out, i, tm):
    t = dh1.shape[0]

    def body(dh1_ref, c_ref, g_ref, b_ref, w_ref, dc_ref, dg_ref, db_ref, dbo_ref):
        first = pl.program_id(0) == 0
        _init(dg_ref, first)
        _init(db_ref, first)
        _init(dbo_ref, first)
        dh1v = dh1_ref[...]
        ds = _dot_nt(dh1v, w_ref[...])
        cv = c_ref[...]
        xc = cv - jnp.mean(cv, axis=-1, keepdims=True)
        rstd = lax.rsqrt(jnp.mean(xc * xc, axis=-1, keepdims=True) + EPS)
        xh = xc * rstd
        n = xh * g_ref[...] + b_ref[...]
        sg = _sig(n)
        dn = ds * (sg * (1.0 + n * (1.0 - sg)))
        dxh = dn * g_ref[...]
        dc_ref[...] = rstd * (dxh - jnp.mean(dxh, axis=-1, keepdims=True) - xh * jnp.mean(dxh * xh, axis=-1, keepdims=True))
        dg_ref[...] += jnp.sum(dn * xh, axis=0, keepdims=True)
        db_ref[...] += jnp.sum(dn, axis=0, keepdims=True)
        dbo_ref[...] += jnp.sum(dh1v, axis=0, keepdims=True)

    return pl.pallas_call(
        body, name=f"conv_out_bwd{i}", grid=(t // tm,),
        in_specs=[_row(tm, D), _row(tm, D), _lay(i, (1, D)), _lay(i, (1, D)), _lay(i, (D, D))],
        out_specs=[_row(tm, D), _acc((1, D)), _acc((1, D)), _acc((1, D))],
        out_shape=[_sds((t, D), F32)] + [_sds((1, D), F32)] * 3,
        compiler_params=_cp(1),
    )(dh1, c, ln_g, ln_b, w_out)


def conv_mid_bwd(dc, a, big, dw, i, tm, tpb):
    t = dc.shape[0]
    nsteps = t // tm

    def body(dc_ref, nxt_ref, a_ref, prv_ref, big_ref, dw_ref, da_ref, dbin_ref, ddw_ref, dce, ae, wacc, bacc):
        i_ = pl.program_id(0)
        _init(wacc, i_ == 0)
        _init(bacc, i_ == 0)
        dce[0:tm] = dc_ref[...]
        dce[tm:] = jnp.where(i_ % tpb == tpb - 1, 0.0, nxt_ref[...])
        ae[0:HALO] = jnp.where(i_ % tpb == 0, 0.0, prv_ref[...])
        ae[HALO:] = a_ref[...]

        def chunk(k, carry):
            r0 = pl.multiple_of(k * CHUNK, CHUNK)
            wdc = dce[pl.ds(r0, 2 * CHUNK), :]
            wa = ae[pl.ds(r0, 2 * CHUNK), :]
            dcc = wdc[0:CHUNK]
            da = jnp.zeros((CHUNK, D), F32)
            for j in range(CW):
                da = da + dw_ref[j:j + 1, :] * wdc[CW - 1 - j:CW - 1 - j + CHUNK]
                wacc[j] += _fold8(dcc * wa[j + 2:j + 2 + CHUNK])
            bv = big_ref[pl.ds(r0, CHUNK), :]
            a1, sg = bv[:, :D], _sig(bv[:, D:])
            d1 = da * sg
            d2 = da * a1 * sg * (1.0 - sg)
            da_ref[pl.ds(r0, CHUNK), 0:D] = d1.astype(BF)
            da_ref[pl.ds(r0, CHUNK), D:2 * D] = d2.astype(BF)
            bacc[:, 0:D] += _fold8(d1)
            bacc[:, D:2 * D] += _fold8(d2)
            return carry

        lax.fori_loop(0, tm // CHUNK, chunk, 0)

        @pl.when(i_ == nsteps - 1)
        def _():
            dbin_ref[...] = jnp.sum(bacc[...], axis=0, keepdims=True)
            ddw_ref[...] = jnp.sum(wacc[...], axis=1)

    return pl.pallas_call(
        body, name=f"conv_mid_bwd{i}", grid=(nsteps,),
        in_specs=[_row(tm, D), _next_halo(tm, t), _row(tm, D), _prev_halo(tm), _row(tm, 2 * D), _lay(i, (CW, D))],
        out_specs=[_row(tm, 2 * D), _acc((1, 2 * D)), _acc((CW + 1, D))],
        out_shape=[_sds((t, 2 * D), BF), _sds((1, 2 * D), F32), _sds((CW + 1, D), F32)],
        scratch_shapes=[pltpu.VMEM((tm + HALO, D), F32), pltpu.VMEM((tm + HALO, D), F32),
                        pltpu.VMEM((CW + 1, 8, D), F32), pltpu.VMEM((8, 2 * D), F32)],
        compiler_params=_cp(1),
    )(dc, dc, a, a, big, dw)


def input_grads(dh0, seq):
    bl, lp, _ = dh0.shape

    def body(dh_ref, gx_ref, gm_ref, mbuf, sems):
        cps = [pltpu.make_async_copy(dh_ref.at[b, pl.ds(NMETA, seq)], gx_ref.at[b], sems.at[b]) for b in range(bl)]
        cps += [pltpu.make_async_copy(dh_ref.at[b, pl.ds(0, NMETA)], mbuf.at[b], sems.at[bl + b]) for b in range(bl)]
        for c in cps:
            c.start()
        for c in cps:
            c.wait()
        acc = mbuf[0]
        for b in range(1, bl):
            acc = acc + mbuf[b]
        gm_ref[...] = acc

    return pl.pallas_call(
        body, name="input_grads", out_shape=[_sds((bl, seq, D), F32), _sds((NMETA, D), F32)],
        in_specs=[pl.BlockSpec(memory_space=pl.ANY)],
        out_specs=[pl.BlockSpec(memory_space=pl.ANY), pl.BlockSpec(memory_space=pltpu.VMEM)],
        scratch_shapes=[pltpu.VMEM((bl, NMETA, D), F32), pltpu.SemaphoreType.DMA((2 * bl,))],
    )(dh0)


def local_step(x, tgt, meta8, w):
    bl, seq, _ = x.shape
    lp = -(-(NMETA + seq) // QB) * QB
    tpb = 4
    tm = lp // tpb
    t = bl * lp
    na, nbl = 2, 2
    flat = lambda a: a.reshape(t, D)

    h = flat(embed(x, meta8, lp))
    saved = []
    kvs = None
    for l in range(4):
        rec = {"h": h}
        if l < na:
            rec["u"], rec["big"], rec["a"] = conv_in_fwd(h, w["norm_mix"], l, w["conv_w_in"], w["conv_b_in"], l, tm)
            rec["c"], rec["s"] = conv_mid_fwd(rec["a"], w["conv_dw"], w["conv_ln_g"], w["conv_ln_b"], l, tm, tpb)
            rec["h1"], rec["u2"] = mixer_out_fwd(h, rec["s"], w["conv_w_out"], l, w["conv_b_out"], w["norm_ffn"], l, tm,
                                                 f"conv_out_fwd{l}")
        else:
            j = l - na
            if kvs is None:
                kvs = dict(zip(("kn", "kv", "k", "v"), kv_fwd(h, w["kv_norm"], w["w_kv"], w["k_norm"], tm)))
                kvs["h"] = h
            rec["u"], rec["q"] = q_fwd(h, w["norm_mix"], l, w["w_q"], j, tm)
            rec["o"], rec["lse"] = attn_fwd(rec["q"], kvs["k"], kvs["v"], w["q_norm"], w["attn_sinks"], j, bl, lp)
            rec["h1"], rec["u2"] = mixer_out_fwd(h, rec["o"], w["w_o"], j, None, w["norm_ffn"], l, tm, f"attn_out_fwd{j}")
        rec["g"], rec["up"], rec["hid"] = ffn_up_fwd(rec["u2"], w["ffn_w_gate"], w["ffn_w_up"], l, tm // 2)
        h = ffn_down_fwd(rec["hid"], rec["h1"], w["ffn_w_down"], l, tm)
        saved.append(rec)

    dh3, loss_blk = loss_fwd(h.reshape(bl, lp, D), tgt)
    dh = flat(dh3)

    big, small = {}, {}
    dks, dvs = [], []
    for l in reversed(range(4)):
        rec = saved[l]
        dg, du, dh1, small[f"norm_ffn{l}"] = ffn_bwd_x(dh, rec["g"], rec["up"], rec["h1"], w["norm_ffn"], l,
                                                       w["ffn_w_down"], w["ffn_w_gate"], w["ffn_w_up"], tm // 2)
        big[f"ffn_w_down{l}"] = mm_tn(rec["hid"], dh, tm, f"dw_down{l}")
        big[f"ffn_w_gate{l}"] = mm_tn(rec["u2"], dg, tm, f"dw_gate{l}", split=True)
        big[f"ffn_w_up{l}"] = mm_tn(rec["u2"], du, tm, f"dw_up{l}", split=True)
        if l >= na:
            j = l - na
            do = out_proj_bwd(dh1, w["w_o"], j, tm, f"attn_out_bwd{j}")
            big[f"w_o{j}"] = mm_tn(rec["o"], dh1, tm, f"dw_o{j}")
            dq, dk, dv, small[f"q_norm{j}"], small[f"attn_sinks{j}"] = attn_bwd(
                rec["q"], kvs["k"], kvs["v"], do, rec["o"], rec["lse"], w["q_norm"], w["attn_sinks"], j, bl, lp)
            dks.append(dk.reshape(t, KVD))
            dvs.append(dv.reshape(t, KVD))
            big[f"w_q{j}"] = mm_tn(rec["u"], dq, tm, f"dw_q{j}")
            dh, small[f"norm_mix{l}"] = proj_bwd(dq, w["w_q"], j, (D, D), rec["h"], w["norm_mix"], l, dh1, tm, f"q_bwd{j}")
            if l == na:
                dkv, small["k_norm"] = kv_bwd_pre(dks[0], dks[1], dvs[0], dvs[1], kvs["kv"], w["k_norm"], tm)
                big["w_kv"] = mm_tn(kvs["kn"], dkv, tm, "dw_kv")
                dh, small["kv_norm"] = proj_bwd(dkv, w["w_kv"], None, (D, 2 * KVD), kvs["h"], w["kv_norm"], None, dh, tm, "kv_bwd")
        else:
            dc, small[f"conv_ln_g{l}"], small[f"conv_ln_b{l}"], small[f"conv_b_out{l}"] = conv_out_bwd(
                dh1, rec["c"], w["conv_ln_g"], w["conv_ln_b"], w["conv_w_out"], l, tm)
            big[f"conv_w_out{l}"] = mm_tn(rec["s"], dh1, tm, f"dw_conv_out{l}")
            da, small[f"conv_b_in{l}"], ddw = conv_mid_bwd(dc, rec["a"], rec["big"], w["conv_dw"], l, tm, tpb)
            small[f"conv_dw{l}"] = ddw[:CW]
            big[f"conv_w_in{l}"] = mm_tn(rec["u"], da, tm, f"dw_conv_in{l}", split=True)
            dh, small[f"norm_mix{l}"] = proj_bwd(da, w["conv_w_in"], l, (D, 2 * D), rec["h"], w["norm_mix"], l, dh1, tm,
                                                 f"conv_in_bwd{l}")
    grad_x, small["meta_tokens"] = input_grads(dh.reshape(bl, lp, D), seq)
    return loss_blk, grad_x, big, small


def _my_index():
    return 4 * lax.axis_index("x") + 2 * lax.axis_index("y") + lax.axis_index("c")


def _coords(idx):
    return (idx // 4, (idx // 2) % 2, idx % 2)


def all_gather_weights(shards, kinds):
    n = len(shards)

    def out_shape(s, kind):
        if kind == "rows":
            return _sds(s.shape[:-2] + (NDEV * s.shape[-2], s.shape[-1]), s.dtype)
        return _sds((NDEV,) + s.shape, s.dtype)

    def body(*refs):
        srcs, outs = refs[:n], refs[n:2 * n]
        send_sems, recv_sems, local_sems = refs[2 * n:]
        x, y, c = lax.axis_index("x"), lax.axis_index("y"), lax.axis_index("c")
        me, sibling = (x, y, c), (x, y, 1 - c)
        chips = [(1 - x, y), (x, 1 - y), (1 - x, 1 - y)]

        def slot(a, owner):
            idx = 4 * owner[0] + 2 * owner[1] + owner[2]
            if kinds[a] == "rows":
                r = srcs[a].shape[-2]
                if len(srcs[a].shape) == 3:
                    return outs[a].at[:, pl.ds(idx * r, r), :]
                return outs[a].at[pl.ds(idx * r, r), :]
            return outs[a].at[idx]

        def copy(a, k, block, to, src=None):
            return pltpu.make_async_remote_copy(
                src_ref=slot(a, block) if src is None else src, dst_ref=slot(a, block),
                send_sem=send_sems.at[a, k], recv_sem=recv_sems.at[a, k], device_id=to, device_id_type=MESH)

        mine = [pltpu.make_async_copy(srcs[a], slot(a, me), local_sems.at[a]) for a in range(n)]
        for cp in mine:
            cp.start()
        first = []
        for a in range(n):
            first.append(copy(a, 0, me, sibling, src=srcs[a]))
            first += [copy(a, 1 + j, me, (*chip, c), src=srcs[a]) for j, chip in enumerate(chips)]
        for cp in first:
            cp.start()
        passed = []
        for a in range(n):
            for j, chip in enumerate(chips):
                copy(a, 1 + j, (*chip, c), me).wait_recv()
                cp = copy(a, 4 + j, (*chip, c), sibling)
                cp.start()
                passed.append(cp)
        for a in range(n):
            copy(a, 0, sibling, me).wait_recv()
            for j, chip in enumerate(chips):
                copy(a, 4 + j, (*chip, 1 - c), me).wait_recv()
        for cp in first + passed:
            cp.wait_send()
        for cp in mine:
            cp.wait()

    any_spec = pl.BlockSpec(memory_space=pl.ANY)
    return pl.pallas_call(
        body, name="all_gather_weights", out_shape=[out_shape(s, k) for s, k in zip(shards, kinds)],
        in_specs=[any_spec] * n, out_specs=[any_spec] * n,
        scratch_shapes=[pltpu.SemaphoreType.DMA((n, 7)), pltpu.SemaphoreType.DMA((n, 7)), pltpu.SemaphoreType.DMA((n,))],
    )(*shards)


def exchange_grad_pieces(groups):
    flat = [(arr, kind, gi, li) for gi, (arrs, kind) in enumerate(groups) for li, arr in enumerate(arrs)]
    n = len(flat)

    def piece_shape(arr, kind):
        return (arr.shape[0] // NDEV, arr.shape[1]) if kind == "rows" else tuple(arr.shape[1:])

    out_shapes = [_sds((len(arrs), NDEV) + piece_shape(arrs[0], kind), arrs[0].dtype) for arrs, kind in groups]

    def body(*refs):
        srcs, outs = refs[:n], refs[n:n + len(groups)]
        send_sems, recv_sems, local_sems = refs[n + len(groups):]
        me = _my_index()

        def piece(a, p):
            kind = flat[a][1]
            if kind == "rows":
                r = srcs[a].shape[0] // NDEV
                return srcs[a].at[pl.ds(p * r, r), :]
            return srcs[a].at[p]

        def land(a, q):
            return outs[flat[a][2]].at[flat[a][3], q]

        def copy(a, k):
            peer = me ^ k
            return pltpu.make_async_remote_copy(
                src_ref=piece(a, peer), dst_ref=land(a, me), send_sem=send_sems.at[a, k - 1], recv_sem=recv_sems.at[a, k - 1],
                device_id=_coords(peer), device_id_type=MESH)

        mine = [pltpu.make_async_copy(piece(a, me), land(a, me), local_sems.at[a]) for a in range(n)]
        for cp in mine:
            cp.start()
        sends = [copy(a, k) for a in range(n) for k in range(1, NDEV)]
        for cp in sends:
            cp.start()
        for a in range(n):
            for k in range(1, NDEV):
                peer = me ^ k
                pltpu.make_async_remote_copy(
                    src_ref=piece(a, me), dst_ref=land(a, peer), send_sem=send_sems.at[a, k - 1],
                    recv_sem=recv_sems.at[a, k - 1], device_id=_coords(peer), device_id_type=MESH).wait_recv()
        for cp in sends:
            cp.wait_send()
        for cp in mine:
            cp.wait()

    any_spec = pl.BlockSpec(memory_space=pl.ANY)
    return pl.pallas_call(
        body, name="exchange_grad_pieces", out_shape=out_shapes,
        in_specs=[any_spec] * n, out_specs=[any_spec] * len(groups),
        scratch_shapes=[pltpu.SemaphoreType.DMA((n, 7)), pltpu.SemaphoreType.DMA((n, 7)), pltpu.SemaphoreType.DMA((n,))],
    )(*[f[0] for f in flat])


def all_reduce_small(buf):
    rows = buf.shape[0]

    def body(x_ref, o_ref, g_ref, send_sems, recv_sems):
        me = _my_index()
        sends = []
        for k in range(1, NDEV):
            peer = me ^ k
            cp = pltpu.make_async_remote_copy(
                src_ref=x_ref, dst_ref=g_ref.at[me], send_sem=send_sems.at[k - 1], recv_sem=recv_sems.at[k - 1],
                device_id=_coords(peer), device_id_type=MESH)
            cp.start()
            sends.append(cp)
        g_ref[me] = x_ref[...]
        for k in range(1, NDEV):
            peer = me ^ k
            pltpu.make_async_remote_copy(
                src_ref=x_ref, dst_ref=g_ref.at[peer], send_sem=send_sems.at[k - 1], recv_sem=recv_sems.at[k - 1],
                device_id=_coords(peer), device_id_type=MESH).wait_recv()
        for cp in sends:
            cp.wait_send()
        acc = g_ref[0]
        for p in range(1, NDEV):
            acc = acc + g_ref[p]
        o_ref[...] = acc

    return pl.pallas_call(
        body, name="all_reduce_small", out_shape=_sds((rows, D), F32),
        in_specs=[pl.BlockSpec(memory_space=pltpu.VMEM)], out_specs=pl.BlockSpec(memory_space=pltpu.VMEM),
        scratch_shapes=[pltpu.VMEM((NDEV, rows, D), F32), pltpu.SemaphoreType.DMA((7,)), pltpu.SemaphoreType.DMA((7,))],
    )(buf)


def cast_bf16(ws):
    n = len(ws)

    def body(*refs):
        for a in range(n):
            refs[n + a][...] = refs[a][...].astype(BF)

    return pl.pallas_call(
        body, name="cast_bf16", out_shape=[_sds(x.shape, BF) for x in ws],
        compiler_params=pltpu.CompilerParams(vmem_limit_bytes=VMEM_LIMIT),
    )(*ws)


def join_columns(w8, name):
    _, lyr, kk, n8 = w8.shape

    def body(x_ref, o_ref):
        o_ref[...] = jnp.concatenate([x_ref[p] for p in range(NDEV)], axis=1)

    return pl.pallas_call(
        body, name=name, grid=(lyr,),
        in_specs=[pl.BlockSpec((NDEV, None, kk, n8), lambda l: (0, l, 0, 0))],
        out_specs=pl.BlockSpec((None, kk, NDEV * n8), lambda l: (l, 0, 0)),
        out_shape=_sds((lyr, kk, NDEV * n8), w8.dtype), compiler_params=_cp(1),
    )(w8)


def _adamw_math(w, m, v, g):
    m2 = B1 * m + (1.0 - B1) * g
    v2 = B2 * v + (1.0 - B2) * (g * g)
    mh = m2 / (1.0 - B1 ** STEP)
    vh = v2 / (1.0 - B2 ** STEP)
    return -LR * (mh / (jnp.sqrt(vh) + AEPS) + WD * w), m2, v2


def adamw_big(w, m, v, parts, name):
    lyr, r, c = w.shape
    tr = 256 if r % 256 == 0 else r

    def body(w_ref, m_ref, v_ref, p_ref, g_ref, d_ref, m2_ref, v2_ref):
        g = p_ref[0].astype(F32)
        for q in range(1, NDEV):
            g = g + p_ref[q].astype(F32)
        g_ref[...] = g
        d_ref[...], m2_ref[...], v2_ref[...] = _adamw_math(w_ref[...], m_ref[...], v_ref[...], g)

    spec = pl.BlockSpec((None, tr, c), lambda l, i: (l, i, 0))
    return pl.pallas_call(
        body, name=name, grid=(lyr, r // tr),
        in_specs=[spec, spec, spec, pl.BlockSpec((None, NDEV, tr, c), lambda l, i: (l, 0, i, 0))],
        out_specs=[spec] * 4, out_shape=[_sds((lyr, r, c), F32)] * 4, compiler_params=_cp(2),
    )(w, m, v, parts)


def adamw_small(w, m, v, g, name):
    def body(w_ref, m_ref, v_ref, g_ref, d_ref, m2_ref, v2_ref):
        d_ref[...], m2_ref[...], v2_ref[...] = _adamw_math(w_ref[...], m_ref[...], v_ref[...], g_ref[...])

    return pl.pallas_call(body, name=name, out_shape=[_sds(w.shape, F32)] * 3)(w, m, v, g)


NAMES = ["meta_tokens", "norm_mix", "norm_ffn", "conv_w_in", "conv_b_in", "conv_dw", "conv_ln_g", "conv_ln_b", "conv_w_out",
         "conv_b_out", "kv_norm", "w_kv", "k_norm", "w_q", "q_norm", "attn_sinks", "w_o", "ffn_w_gate", "ffn_w_up", "ffn_w_down"]
BIG = {"conv_w_in": "pieces", "conv_w_out": "rows", "w_kv": "rows", "w_q": "rows", "w_o": "rows",
       "ffn_w_gate": "pieces", "ffn_w_up": "pieces", "ffn_w_down": "rows"}
REP_ROWS = 16


def _pad_cols(a, width):
    return jnp.pad(a, ((0, 0), (0, width - a.shape[1])))


def _pack_rep(p):
    rows = [p["norm_mix"], p["norm_ffn"], p["kv_norm"].reshape(1, D), _pad_cols(p["k_norm"].reshape(1, HD), D),
            _pad_cols(p["q_norm"], D), _pad_cols(p["attn_sinks"], D), jnp.zeros((2, D), F32)]
    return jnp.concatenate(rows, axis=0)


def _unpack_rep(a):
    return {"norm_mix": a[0:4], "norm_ffn": a[4:8], "kv_norm": a[8], "k_norm": a[9, :HD], "q_norm": a[10:12, :HD],
            "attn_sinks": a[12:14, :NH]}


SH_NAMES = ["meta_tokens", "conv_b_in", "conv_dw", "conv_ln_g", "conv_ln_b", "conv_b_out"]


def _pack_sh(p):
    c = D // NDEV
    rows = [p["meta_tokens"], p["conv_b_in"].reshape(4, c), p["conv_dw"].reshape(2 * CW, c), p["conv_ln_g"], p["conv_ln_b"],
            p["conv_b_out"]]
    return jnp.concatenate(rows, axis=0)


def _unpack_sh(a):
    c = D // NDEV
    return {"meta_tokens": a[0:16], "conv_b_in": a[16:20].reshape(2, 2 * c), "conv_dw": a[20:82].reshape(2, CW, c),
            "conv_ln_g": a[82:84], "conv_ln_b": a[84:86], "conv_b_out": a[86:88]}


def kernel(x, meta_tokens, norm_mix, norm_ffn, conv_w_in, conv_b_in, conv_dw, conv_ln_g, conv_ln_b, conv_w_out, conv_b_out, kv_norm, w_kv, k_norm, w_q, q_norm, attn_sinks, w_o, ffn_w_gate, ffn_w_up, ffn_w_down, loss_target, m_meta_tokens, m_norm_mix, m_norm_ffn, m_conv_w_in, m_conv_b_in, m_conv_dw, m_conv_ln_g, m_conv_ln_b, m_conv_w_out, m_conv_b_out, m_kv_norm, m_w_kv, m_k_norm, m_w_q, m_q_norm, m_attn_sinks, m_w_o, m_ffn_w_gate, m_ffn_w_up, m_ffn_w_down, v_meta_tokens, v_norm_mix, v_norm_ffn, v_conv_w_in, v_conv_b_in, v_conv_dw, v_conv_ln_g, v_conv_ln_b, v_conv_w_out, v_conv_b_out, v_kv_norm, v_w_kv, v_k_norm, v_w_q, v_q_norm, v_attn_sinks, v_w_o, v_ffn_w_gate, v_ffn_w_up, v_ffn_w_down):
    wts = dict(zip(NAMES, (meta_tokens, norm_mix, norm_ffn, conv_w_in, conv_b_in, conv_dw, conv_ln_g, conv_ln_b, conv_w_out,
                           conv_b_out, kv_norm, w_kv, k_norm, w_q, q_norm, attn_sinks, w_o, ffn_w_gate, ffn_w_up, ffn_w_down)))
    mom = dict(zip(NAMES, (m_meta_tokens, m_norm_mix, m_norm_ffn, m_conv_w_in, m_conv_b_in, m_conv_dw, m_conv_ln_g, m_conv_ln_b,
                           m_conv_w_out, m_conv_b_out, m_kv_norm, m_w_kv, m_k_norm, m_w_q, m_q_norm, m_attn_sinks, m_w_o,
                           m_ffn_w_gate, m_ffn_w_up, m_ffn_w_down)))
    var = dict(zip(NAMES, (v_meta_tokens, v_norm_mix, v_norm_ffn, v_conv_w_in, v_conv_b_in, v_conv_dw, v_conv_ln_g, v_conv_ln_b,
                           v_conv_w_out, v_conv_b_out, v_kv_norm, v_w_kv, v_k_norm, v_w_q, v_q_norm, v_attn_sinks, v_w_o,
                           v_ffn_w_gate, v_ffn_w_up, v_ffn_w_down)))
    me = _my_index()
    c8 = D // NDEV

    big_names = list(BIG)
    shards = cast_bf16([wts[k] for k in big_names])
    vec_names = ["meta_tokens", "conv_b_in", "conv_dw", "conv_ln_g", "conv_ln_b", "conv_b_out"]
    gathered = all_gather_weights(list(shards) + [wts[k] for k in vec_names],
                                  [BIG[k] for k in big_names] + ["flat"] * len(vec_names))
    full = dict(zip(big_names + vec_names, gathered))
    w = {}
    for k in big_names:
        w[k] = join_columns(full[k] if full[k].ndim == 4 else full[k][:, None], "join_" + k) if BIG[k] == "pieces" else full[k]
    join_vec = lambda a: jnp.moveaxis(a, 0, -2).reshape(a.shape[1:-1] + (NDEV * a.shape[-1],))
    w["conv_b_in"] = join_vec(full["conv_b_in"]).reshape(2, 1, 2 * D)
    w["conv_dw"] = join_vec(full["conv_dw"])
    for k in ("conv_ln_g", "conv_ln_b", "conv_b_out"):
        w[k] = join_vec(full[k]).reshape(2, 1, D)
    w["norm_mix"] = norm_mix.reshape(4, 1, D)
    w["norm_ffn"] = norm_ffn.reshape(4, 1, D)
    w["kv_norm"] = kv_norm.reshape(1, D)
    w["k_norm"] = k_norm.reshape(1, HD)
    w["q_norm"] = q_norm.reshape(2, 1, HD)
    w["attn_sinks"] = attn_sinks.reshape(2, 1, NH)

    loss_blk, grad_x, gbig, gs = local_step(x, loss_target, full["meta_tokens"], w)

    stack = lambda k, n: jnp.concatenate([gs[f"{k}{i}"] for i in range(n)], axis=0)
    rep = {"norm_mix": stack("norm_mix", 4), "norm_ffn": stack("norm_ffn", 4), "kv_norm": gs["kv_norm"], "k_norm": gs["k_norm"],
           "q_norm": stack("q_norm", 2), "attn_sinks": stack("attn_sinks", 2)}
    loss_row = _pad_cols(loss_blk[0:1, 0:1], D)
    packed = jnp.concatenate(
        [_pack_rep(rep)[:14], loss_row, jnp.zeros((1, D), F32), gs["meta_tokens"], stack("conv_b_in", 2).reshape(4, D),
         stack("conv_dw", 2), stack("conv_ln_g", 2), stack("conv_ln_b", 2), stack("conv_b_out", 2)], axis=0)
    red = all_reduce_small(packed)
    loss = red[14, 0]
    cols = lambda a, width: lax.dynamic_slice_in_dim(a, me * width, width, axis=1)
    g_sh = jnp.concatenate(
        [cols(red[16:32], c8), cols(red[32:36].reshape(2, 2 * D), 2 * c8).reshape(4, c8), cols(red[36:98], c8),
         cols(red[98:100], c8), cols(red[100:102], c8), cols(red[102:104], c8)], axis=0)
    g_rep = red[0:REP_ROWS].at[14:].set(0.0)

    groups = []
    for k in big_names:
        lyr = wts[k].shape[0] if wts[k].ndim == 3 else 1
        names = [k] if wts[k].ndim == 2 else [f"{k}{i}" for i in range(lyr)]
        groups.append(([gbig[nm] for nm in names], BIG[k]))
    recv = exchange_grad_pieces(groups)

    grads, delta, new_m, new_v = {}, {}, {}, {}
    for k, parts in zip(big_names, recv):
        as3 = (lambda a: a[None]) if wts[k].ndim == 2 else (lambda a: a)
        outs = adamw_big(as3(wts[k]), as3(mom[k]), as3(var[k]), parts, "adamw_" + k)
        grads[k], delta[k], new_m[k], new_v[k] = [o[0] if wts[k].ndim == 2 else o for o in outs]
    d_rep, m_rep, v_rep = adamw_small(_pack_rep(wts), _pack_rep(mom), _pack_rep(var), g_rep, "adamw_rep")
    d_sh, m_sh, v_sh = adamw_small(_pack_sh(wts), _pack_sh(mom), _pack_sh(var), g_sh, "adamw_sh")
    for dst, a_rep, a_sh in ((grads, g_rep, g_sh), (delta, d_rep, d_sh), (new_m, m_rep, m_sh), (new_v, v_rep, v_sh)):
        dst.update(_unpack_rep(a_rep))
        dst.update(_unpack_sh(a_sh))
    return (loss, grad_x, *[grads[k] for k in NAMES], *[delta[k] for k in NAMES], *[new_m[k] for k in NAMES],
            *[new_v[k] for k in NAMES])
```

```python
import functools

import jax
import jax.numpy as jnp
from jax import lax
from jax.experimental import pallas as pl
from jax.experimental.pallas import tpu as pltpu

F32 = jnp.float32
BF = jnp.bfloat16

D = 1024
DFF = 2816
NH = 16
NKV = 4
HD = 64
KVD = NKV * HD
NMETA = 16
CW = 31
HALO = 32
CHUNK = 32
QB = 128
EPS = 1e-6
NEG = -1e30
NDEV = 8
SCALE = HD ** -0.5

LR, B1, B2, AEPS, WD, STEP = 0.001, 0.9, 0.999, 1e-08, 0.01, 10

VMEM_LIMIT = 56 * 2 ** 20
MESH = pl.DeviceIdType.MESH


def _cp(n):
    return pltpu.CompilerParams(dimension_semantics=("arbitrary",) * n, vmem_limit_bytes=VMEM_LIMIT)


def _row(tm, c):
    return pl.BlockSpec((tm, c), lambda i: (i, 0))


def _res(shape):
    return pl.BlockSpec(shape, lambda i: (0,) * len(shape), pipeline_mode=pl.Buffered(1))


def _lay(l, shape):
    return pl.BlockSpec((None,) + tuple(shape), lambda i: (l,) + (0,) * len(shape), pipeline_mode=pl.Buffered(1))


def _acc(shape):
    return pl.BlockSpec(shape, lambda i: (0,) * len(shape))


def _sds(shape, dt):
    return jax.ShapeDtypeStruct(tuple(shape), dt)


def _dot(a, b):
    return jnp.dot(a.astype(BF), b.astype(BF), preferred_element_type=F32)


def _dot_nt(a, b):
    return lax.dot_general(a.astype(BF), b.astype(BF), (((1,), (1,)), ((), ())), preferred_element_type=F32)


def _dot_tn(a, b):
    return lax.dot_general(a.astype(BF), b.astype(BF), (((0,), (0,)), ((), ())), preferred_element_type=F32)


def _rstd(x):
    return lax.rsqrt(jnp.mean(x * x, axis=-1, keepdims=True) + EPS)


def _rms_bwd(x, g, dy):
    r = _rstd(x)
    z = dy * g
    dx = r * z - x * (r * r * r * jnp.mean(z * x, axis=-1, keepdims=True))
    return dx, jnp.sum(dy * x * r, axis=0, keepdims=True)


def _sig(x):
    return jax.nn.sigmoid(x)


def _fold8(x):
    out = x[0:8]
    for k in range(1, x.shape[0] // 8):
        out = out + x[8 * k:8 * k + 8]
    return out


def _init(ref, first):
    @pl.when(first)
    def _():
        ref[...] = jnp.zeros_like(ref)


def _my_index():
    return 4 * lax.axis_index("x") + 2 * lax.axis_index("y") + lax.axis_index("c")


def _coords(idx):
    return (idx // 4, (idx // 2) % 2, idx % 2)


def _xch_shapes(xch):
    return [_sds((NDEV,) + ((a.shape[0] // NDEV, a.shape[1]) if k == "rows" else tuple(a.shape[1:])), a.dtype) for a, k in xch]


def _xch_scratch(n):
    return [pltpu.SemaphoreType.DMA((n, NDEV - 1)), pltpu.SemaphoreType.DMA((n, NDEV - 1)), pltpu.SemaphoreType.DMA((n,))]


def _xch_copies(kinds, srcs, outs, sems, arrivals):
    send_sems, recv_sems, local_sems = sems
    me = _my_index()

    def piece(a, p):
        if kinds[a] == "rows":
            r = srcs[a].shape[0] // NDEV
            return srcs[a].at[pl.ds(p * r, r), :]
        return srcs[a].at[p]

    def remote(a, k, src, slot):
        return pltpu.make_async_remote_copy(
            src_ref=src, dst_ref=outs[a].at[slot], send_sem=send_sems.at[a, k - 1], recv_sem=recv_sems.at[a, k - 1],
            device_id=_coords(me ^ k), device_id_type=MESH)

    n = len(kinds)
    local = [pltpu.make_async_copy(piece(a, me), outs[a].at[me], local_sems.at[a]) for a in range(n)]
    sends = [remote(a, k, piece(a, me ^ k), me) for a in range(n) for k in range(1, NDEV)]
    recvs = [remote(a, k, piece(a, me), me ^ k) for a in range(n) for k in range(1, NDEV)] if arrivals else []
    return local, sends, recvs


def _xch_start(kinds, srcs, outs, sems):
    local, sends, _ = _xch_copies(kinds, srcs, outs, sems, False)
    for cp in local + sends:
        cp.start()


def _xch_wait(kinds, srcs, outs, sems):
    local, sends, recvs = _xch_copies(kinds, srcs, outs, sems, True)
    for cp in recvs:
        cp.wait_recv()
    for cp in sends:
        cp.wait_send()
    for cp in local:
        cp.wait()


def _call(body, name, grid, in_specs, out_specs, out_shape, args, scratch=(), xch=()):
    n_in, n_out, n_x, n_s = len(in_specs), len(out_specs), len(xch), len(scratch)
    kinds = [k for _, k in xch]

    def wrapped(*refs):
        ins, srcs = refs[:n_in], refs[n_in:n_in + n_x]
        outs, arrivals = refs[n_in + n_x:n_in + n_x + n_out], refs[n_in + n_x + n_out:n_in + 2 * n_x + n_out]
        rest = refs[n_in + 2 * n_x + n_out:]
        if n_x:
            ids = [pl.program_id(d) for d in range(len(grid))]
            first, last = ids[0] == 0, ids[0] == grid[0] - 1
            for d in range(1, len(grid)):
                first, last = first & (ids[d] == 0), last & (ids[d] == grid[d] - 1)

            @pl.when(first)
            def _():
                _xch_start(kinds, srcs, arrivals, rest[n_s:])

        body(*ins, *outs, *rest[:n_s])
        if n_x:
            @pl.when(last)
            def _():
                _xch_wait(kinds, srcs, arrivals, rest[n_s:])

    any_spec = pl.BlockSpec(memory_space=pl.ANY)
    res = pl.pallas_call(
        wrapped, name=name, grid=grid, in_specs=list(in_specs) + [any_spec] * n_x, out_specs=list(out_specs) + [any_spec] * n_x,
        out_shape=list(out_shape) + _xch_shapes(xch), scratch_shapes=list(scratch) + (_xch_scratch(n_x) if n_x else []),
        compiler_params=_cp(len(grid)),
    )(*args, *[a for a, _ in xch])
    return res[:n_out], res[n_out:]


def exchange_grads(xch):
    kinds = [k for _, k in xch]
    n = len(xch)

    def body(*refs):
        _xch_start(kinds, refs[:n], refs[n:2 * n], refs[2 * n:])
        _xch_wait(kinds, refs[:n], refs[n:2 * n], refs[2 * n:])

    any_spec = pl.BlockSpec(memory_space=pl.ANY)
    return pl.pallas_call(
        body, name="exchange_grads", out_shape=_xch_shapes(xch), in_specs=[any_spec] * n, out_specs=[any_spec] * n,
        scratch_shapes=_xch_scratch(n),
    )(*[a for a, _ in xch])


def embed(x, meta8, lp):
    bl, seq, _ = x.shape
    c8 = D // NDEV
    cb = 2 * c8

    def body(x_ref, m_ref, h_ref):
        h_ref[0:NMETA, :] = jnp.concatenate([m_ref[0], m_ref[1]], axis=1)
        h_ref[NMETA:NMETA + seq, :] = x_ref[...]
        h_ref[NMETA + seq:, :] = jnp.zeros((lp - NMETA - seq, cb), F32)

    return pl.pallas_call(
        body, name="embed", grid=(bl, D // cb),
        in_specs=[pl.BlockSpec((None, seq, cb), lambda b, c: (b, 0, c)), pl.BlockSpec((2, NMETA, c8), lambda b, c: (c, 0, 0))],
        out_specs=pl.BlockSpec((None, lp, cb), lambda b, c: (b, 0, c)), out_shape=_sds((bl, lp, D), F32),
        compiler_params=_cp(2),
    )(x, meta8)


def conv_in_fwd(h, nm, l, w_in, b_in, i, tm):
    t = h.shape[0]

    def body(h_ref, g_ref, w_ref, b_ref, u_ref, big_ref, a_ref):
        x = h_ref[...]
        ub = (x * _rstd(x) * g_ref[...]).astype(BF)
        u_ref[...] = ub
        big = jnp.dot(ub, w_ref[...], preferred_element_type=F32) + b_ref[...]
        big_ref[...] = big
        a_ref[...] = big[:, :D] * _sig(big[:, D:])

    return pl.pallas_call(
        body, name=f"conv_in_fwd{i}", grid=(t // tm,),
        in_specs=[_row(tm, D), _lay(l, (1, D)), _lay(i, (D, 2 * D)), _lay(i, (1, 2 * D))],
        out_specs=[_row(tm, D), _row(tm, 2 * D), _row(tm, D)],
        out_shape=[_sds((t, D), BF), _sds((t, 2 * D), F32), _sds((t, D), F32)],
        compiler_params=_cp(1),
    )(h, nm, w_in, b_in)


def _prev_halo(tm):
    return pl.BlockSpec((HALO, D), lambda i: (jnp.maximum(i * (tm // HALO) - 1, 0), 0))


def _next_halo(tm, t):
    return pl.BlockSpec((HALO, D), lambda i: (jnp.minimum((i + 1) * (tm // HALO), t // HALO - 1), 0))


def conv_mid_fwd(a, dw, ln_g, ln_b, i, tm, tpb):
    t = a.shape[0]

    def body(a_ref, halo_ref, dw_ref, g_ref, b_ref, c_ref, s_ref, ext):
        first = pl.program_id(0) % tpb == 0
        ext[0:HALO] = jnp.where(first, 0.0, halo_ref[...])
        ext[HALO:] = a_ref[...]

        def chunk(k, carry):
            r0 = pl.multiple_of(k * CHUNK, CHUNK)
            win = ext[pl.ds(r0, 2 * CHUNK), :]
            c = jnp.zeros((CHUNK, D), F32)
            for j in range(CW):
                c = c + dw_ref[j:j + 1, :] * win[j + 2:j + 2 + CHUNK]
            c_ref[pl.ds(r0, CHUNK), :] = c
            mu = jnp.mean(c, axis=-1, keepdims=True)
            xc = c - mu
            n = xc * lax.rsqrt(jnp.mean(xc * xc, axis=-1, keepdims=True) + EPS) * g_ref[...] + b_ref[...]
            s_ref[pl.ds(r0, CHUNK), :] = (n * _sig(n)).astype(BF)
            return carry

        lax.fori_loop(0, tm // CHUNK, chunk, 0)

    return pl.pallas_call(
        body, name=f"conv_mid_fwd{i}", grid=(t // tm,),
        in_specs=[_row(tm, D), _prev_halo(tm), _lay(i, (CW, D)), _lay(i, (1, D)), _lay(i, (1, D))],
        out_specs=[_row(tm, D), _row(tm, D)],
        out_shape=[_sds((t, D), F32), _sds((t, D), BF)],
        scratch_shapes=[pltpu.VMEM((tm + HALO, D), F32)],
        compiler_params=_cp(1),
    )(a, a, dw, ln_g, ln_b)


def mixer_out_fwd(h, s, w, lw, bias, nf, l, tm, name):
    t = h.shape[0]

    def body(*refs):
        if bias is None:
            h_ref, s_ref, w_ref, g_ref, h1_ref, u_ref = refs
            y = 0.0
        else:
            h_ref, s_ref, w_ref, b_ref, g_ref, h1_ref, u_ref = refs
            y = b_ref[...]
        h1 = h_ref[...] + (jnp.dot(s_ref[...], w_ref[...], preferred_element_type=F32) + y)
        h1_ref[...] = h1
        u_ref[...] = (h1 * _rstd(h1) * g_ref[...]).astype(BF)

    ins = [h, s, w] + ([] if bias is None else [bias]) + [nf]
    specs = [_row(tm, D), _row(tm, D), _lay(lw, (D, D))] + ([] if bias is None else [_lay(lw, (1, D))]) + [_lay(l, (1, D))]
    return pl.pallas_call(
        body, name=name, grid=(t // tm,), in_specs=specs,
        out_specs=[_row(tm, D), _row(tm, D)], out_shape=[_sds((t, D), F32), _sds((t, D), BF)],
        compiler_params=_cp(1),
    )(*ins)


def ffn_up_fwd(u, wg, wu, l, tm):
    t = u.shape[0]

    def body(u_ref, wg_ref, wu_ref, g_ref, up_ref, hid_ref):
        ub = u_ref[...]
        g = jnp.dot(ub, wg_ref[...], preferred_element_type=F32)
        up = jnp.dot(ub, wu_ref[...], preferred_element_type=F32)
        g_ref[...] = g.astype(BF)
        up_ref[...] = up.astype(BF)
        hid_ref[...] = (g * _sig(g) * up).astype(BF)

    return pl.pallas_call(
        body, name=f"ffn_up_fwd{l}", grid=(t // tm,),
        in_specs=[_row(tm, D), _lay(l, (D, DFF)), _lay(l, (D, DFF))],
        out_specs=[_row(tm, DFF)] * 3, out_shape=[_sds((t, DFF), BF)] * 3,
        compiler_params=_cp(1),
    )(u, wg, wu)


def ffn_down_fwd(hid, h1, wd, l, tm):
    t = h1.shape[0]

    def body(hid_ref, h1_ref, w_ref, h2_ref):
        h2_ref[...] = h1_ref[...] + jnp.dot(hid_ref[...], w_ref[...], preferred_element_type=F32)

    return pl.pallas_call(
        body, name=f"ffn_down_fwd{l}", grid=(t // tm,),
        in_specs=[_row(tm, DFF), _row(tm, D), _lay(l, (DFF, D))],
        out_specs=_row(tm, D), out_shape=_sds((t, D), F32),
        compiler_params=_cp(1),
    )(hid, h1, wd)


def _seg_rms(x, g, nseg):
    outs = []
    for s in range(nseg):
        xs = x[:, HD * s:HD * s + HD]
        outs.append(xs * _rstd(xs) * g)
    return jnp.concatenate(outs, axis=1)


def kv_fwd(h, kvn, w_kv, kng, tm):
    t = h.shape[0]

    def body(h_ref, g_ref, w_ref, kg_ref, kn_ref, kv_ref, k_ref, v_ref):
        x = h_ref[...]
        kn = (x * _rstd(x) * g_ref[...]).astype(BF)
        kn_ref[...] = kn
        kv = jnp.dot(kn, w_ref[...], preferred_element_type=F32)
        kv_ref[...] = kv
        k_ref[...] = _seg_rms(kv[:, :KVD], kg_ref[...], NKV).astype(BF)
        v_ref[...] = kv[:, KVD:].astype(BF)

    return pl.pallas_call(
        body, name="kv_fwd", grid=(t // tm,),
        in_specs=[_row(tm, D), _res((1, D)), _res((D, 2 * KVD)), _res((1, HD))],
        out_specs=[_row(tm, D), _row(tm, 2 * KVD), _row(tm, KVD), _row(tm, KVD)],
        out_shape=[_sds((t, D), BF), _sds((t, 2 * KVD), F32), _sds((t, KVD), BF), _sds((t, KVD), BF)],
        compiler_params=_cp(1),
    )(h, kvn, w_kv, kng)


def q_fwd(h, nm, l, w_q, j, tm):
    t = h.shape[0]

    def body(h_ref, g_ref, w_ref, u_ref, q_ref):
        x = h_ref[...]
        ub = (x * _rstd(x) * g_ref[...]).astype(BF)
        u_ref[...] = ub
        q_ref[...] = jnp.dot(ub, w_ref[...], preferred_element_type=F32)

    return pl.pallas_call(
        body, name=f"q_fwd{j}", grid=(t // tm,),
        in_specs=[_row(tm, D), _lay(l, (1, D)), _lay(j, (D, D))],
        out_specs=[_row(tm, D), _row(tm, D)], out_shape=[_sds((t, D), BF), _sds((t, D), F32)],
        compiler_params=_cp(1),
    )(h, nm, w_q)


def _attn_specs(nb, lp):
    cur = lambda c: pl.BlockSpec((QB, c), lambda b, n: (b * nb + n, 0))
    prev = lambda c: pl.BlockSpec((QB, c), lambda b, n: (b * nb + jnp.maximum(n - 1, 0), 0))
    meta = lambda c: pl.BlockSpec((NMETA, c), lambda b, n: (b * (lp // NMETA), 0))
    return cur, prev, meta


def _attn_masks(n):
    qi = lax.broadcasted_iota(jnp.int32, (QB, QB), 0)
    kj = lax.broadcasted_iota(jnp.int32, (QB, QB), 1)
    m_cur = (kj <= qi) & (n * QB + kj >= NMETA)
    m_prev = (kj > qi) & ((n - 1) * QB + kj >= NMETA)
    qm = lax.broadcasted_iota(jnp.int32, (QB, NMETA), 0)
    km = lax.broadcasted_iota(jnp.int32, (QB, NMETA), 1)
    m_meta = km <= n * QB + qm
    return m_cur, m_prev, m_meta


def attn_fwd(q, k, v, qg, sinks, j, bl, lp):
    t = q.shape[0]
    nb = lp // QB
    cur, prev, meta = _attn_specs(nb, lp)

    def body(q_ref, kc_ref, kp_ref, km_ref, vc_ref, vp_ref, vm_ref, qg_ref, sk_ref, o_ref, lse_ref):
        n = pl.program_id(1)
        m_cur, m_prev, m_meta = _attn_masks(n)
        lane = lax.broadcasted_iota(jnp.int32, (QB, NH), 1)
        lse = jnp.zeros((QB, NH), F32)
        for h in range(NH):
            g = h // (NH // NKV)
            hs, gs = slice(HD * h, HD * h + HD), slice(HD * g, HD * g + HD)
            qh = q_ref[:, hs]
            qn = (qh * _rstd(qh) * qg_ref[...]).astype(BF)
            s_c = jnp.where(m_cur, _dot_nt(qn, kc_ref[:, gs]) * SCALE, NEG)
            s_p = jnp.where(m_prev, _dot_nt(qn, kp_ref[:, gs]) * SCALE, NEG)
            s_m = jnp.where(m_meta, _dot_nt(qn, km_ref[:, gs]) * SCALE, NEG)
            sink = sk_ref[:, h:h + 1]
            mx = jnp.maximum(jnp.maximum(jnp.max(s_c, -1, keepdims=True), jnp.max(s_p, -1, keepdims=True)),
                             jnp.maximum(jnp.max(s_m, -1, keepdims=True), sink))
            p_c, p_p, p_m = jnp.exp(s_c - mx), jnp.exp(s_p - mx), jnp.exp(s_m - mx)
            den = (jnp.sum(p_c, -1, keepdims=True) + jnp.sum(p_p, -1, keepdims=True)
                   + jnp.sum(p_m, -1, keepdims=True) + jnp.exp(sink - mx))
            inv = 1.0 / den
            o = _dot(p_c * inv, vc_ref[:, gs]) + _dot(p_p * inv, vp_ref[:, gs]) + _dot(p_m * inv, vm_ref[:, gs])
            o_ref[:, hs] = o.astype(BF)
            lse = jnp.where(lane == h, mx + jnp.log(den), lse)
        lse_ref[...] = lse

    return pl.pallas_call(
        body, name=f"attn_fwd{j}", grid=(bl, nb),
        in_specs=[cur(D), cur(KVD), prev(KVD), meta(KVD), cur(KVD), prev(KVD), meta(KVD),
                  pl.BlockSpec((None, 1, HD), lambda b, n: (j, 0, 0)), pl.BlockSpec((None, 1, NH), lambda b, n: (j, 0, 0))],
        out_specs=[cur(D), cur(NH)], out_shape=[_sds((t, D), BF), _sds((t, NH), F32)],
        compiler_params=_cp(2),
    )(q, k, k, k, v, v, v, qg, sinks)


def loss_fwd(h, tgt):
    bl, lp, _ = h.shape
    seq = tgt.shape[1]
    cb = 256

    def body(h_ref, t_ref, dh_ref, loss_ref):
        _init(loss_ref, (pl.program_id(0) == 0) & (pl.program_id(1) == 0))
        err = h_ref[NMETA:NMETA + seq, :] - t_ref[...]
        dh_ref[...] = jnp.zeros_like(dh_ref)
        dh_ref[NMETA:NMETA + seq, :] = err * (1.0 / D)
        loss_ref[...] += (0.5 / D) * jnp.sum(err * err)

    return pl.pallas_call(
        body, name="loss_fwd", grid=(bl, D // cb),
        in_specs=[pl.BlockSpec((None, lp, cb), lambda b, c: (b, 0, c)), pl.BlockSpec((None, seq, cb), lambda b, c: (b, 0, c))],
        out_specs=[pl.BlockSpec((None, lp, cb), lambda b, c: (b, 0, c)), pl.BlockSpec((8, 128), lambda b, c: (0, 0))],
        out_shape=[_sds((bl, lp, D), F32), _sds((8, 128), F32)],
        compiler_params=_cp(2),
    )(h, tgt)


def ffn_bwd_x(dh2, g, up, h1, nf, l, wd, wg, wu, tm, xch):
    t = dh2.shape[0]

    def body(dh2_ref, g_ref, up_ref, h1_ref, nf_ref, wd_ref, wg_ref, wu_ref, dg_ref, du_ref, dh1_ref, dnf_ref):
        _init(dnf_ref, pl.program_id(0) == 0)
        dh2v = dh2_ref[...]
        dhid = _dot_nt(dh2v, wd_ref[...])
        gv = g_ref[...].astype(F32)
        uv = up_ref[...].astype(F32)
        sg = _sig(gv)
        dgv = (dhid * uv * (sg * (1.0 + gv * (1.0 - sg)))).astype(BF)
        duv = (dhid * (gv * sg)).astype(BF)
        dg_ref[...] = dgv
        du_ref[...] = duv
        dnorm = _dot_nt(dgv, wg_ref[...]) + _dot_nt(duv, wu_ref[...])
        dx, dnf = _rms_bwd(h1_ref[...], nf_ref[...], dnorm)
        dh1_ref[...] = dh2v + dx
        dnf_ref[...] += dnf

    return _call(
        body, f"ffn_bwd_x{l}", (t // tm,),
        [_row(tm, D), _row(tm, DFF), _row(tm, DFF), _row(tm, D), _lay(l, (1, D)),
         _lay(l, (DFF, D)), _lay(l, (D, DFF)), _lay(l, (D, DFF))],
        [_row(tm, DFF), _row(tm, DFF), _row(tm, D), _acc((1, D))],
        [_sds((t, DFF), BF), _sds((t, DFF), BF), _sds((t, D), F32), _sds((1, D), F32)],
        (dh2, g, up, h1, nf, wd, wg, wu), xch=xch)


def mm_tn(x, dy, tm, name, split=False):
    t, kk = x.shape
    nn = dy.shape[1]
    n8 = nn // NDEV
    nsteps = t // tm

    def body(x_ref, dy_ref, o_ref, acc):
        i = pl.program_id(0)
        _init(acc, i == 0)
        acc[...] += _dot_tn(x_ref[...], dy_ref[...])

        @pl.when(i == nsteps - 1)
        def _():
            if split:
                for p in range(NDEV):
                    o_ref[p] = acc[:, p * n8:(p + 1) * n8].astype(BF)
            else:
                o_ref[...] = acc[...].astype(BF)

    oshape = (NDEV, kk, n8) if split else (kk, nn)
    return pl.pallas_call(
        body, name=name, grid=(nsteps,), in_specs=[_row(tm, kk), _row(tm, nn)],
        out_specs=_acc(oshape), out_shape=_sds(oshape, BF), scratch_shapes=[pltpu.VMEM((kk, nn), F32)],
        compiler_params=_cp(1),
    )(x, dy)


def proj_bwd(dy, w, lw, wshape, h, g, lg, dh_in, tm, name):
    t = h.shape[0]
    nn = dy.shape[1]
    wspec = _res(wshape) if lw is None else _lay(lw, wshape)
    gspec = _res((1, D)) if lg is None else _lay(lg, (1, D))

    def body(dy_ref, w_ref, h_ref, g_ref, dhin_ref, dh_ref, dg_ref):
        _init(dg_ref, pl.program_id(0) == 0)
        du = _dot_nt(dy_ref[...], w_ref[...])
        dx, dg = _rms_bwd(h_ref[...], g_ref[...], du)
        dh_ref[...] = dhin_ref[...] + dx
        dg_ref[...] += dg

    return pl.pallas_call(
        body, name=name, grid=(t // tm,),
        in_specs=[_row(tm, nn), wspec, _row(tm, D), gspec, _row(tm, D)],
        out_specs=[_row(tm, D), _acc((1, D))], out_shape=[_sds((t, D), F32), _sds((1, D), F32)],
        compiler_params=_cp(1),
    )(dy, w, h, g, dh_in)


def out_proj_bwd(dh1, w, lw, tm, name):
    t = dh1.shape[0]

    def body(dh1_ref, w_ref, do_ref):
        do_ref[...] = _dot_nt(dh1_ref[...], w_ref[...]).astype(BF)

    return pl.pallas_call(
        body, name=name, grid=(t // tm,), in_specs=[_row(tm, D), _lay(lw, (D, D))],
        out_specs=_row(tm, D), out_shape=_sds((t, D), BF), compiler_params=_cp(1),
    )(dh1, w)


def attn_bwd(q, k, v, do, o, lse, qg, sinks, j, bl, lp, xch):
    t = q.shape[0]
    nb = lp // QB
    cur, prev, meta = _attn_specs(nb, lp)
    rq = NH // NKV

    def body(q_ref, kc_ref, kp_ref, km_ref, vc_ref, vp_ref, vm_ref, do_ref, o_ref, lse_ref, qg_ref, sk_ref,
             dq_ref, dk_ref, dv_ref, dqg_ref, dsk_ref):
        b, n = pl.program_id(0), pl.program_id(1)
        _init(dk_ref, n == 0)
        _init(dv_ref, n == 0)
        _init(dqg_ref, (b == 0) & (n == 0))
        _init(dsk_ref, (b == 0) & (n == 0))
        m_cur, m_prev, m_meta = _attn_masks(n)
        lane = lax.broadcasted_iota(jnp.int32, (1, NH), 1)
        dqg = jnp.zeros((1, HD), F32)
        dsk = jnp.zeros((1, NH), F32)
        dk_c, dk_p, dk_m, dv_c, dv_p, dv_m = [], [], [], [], [], []
        for g in range(NKV):
            gs = slice(HD * g, HD * g + HD)
            kc, kp, km = kc_ref[:, gs], kp_ref[:, gs], km_ref[:, gs]
            vc, vp, vm = vc_ref[:, gs], vp_ref[:, gs], vm_ref[:, gs]
            acc = [0.0] * 6
            for r in range(rq):
                h = g * rq + r
                hs = slice(HD * h, HD * h + HD)
                qh = q_ref[:, hs]
                rs = _rstd(qh)
                qn = (qh * rs * qg_ref[...]).astype(BF)
                ls = lse_ref[:, h:h + 1]
                p_c = jnp.where(m_cur, jnp.exp(_dot_nt(qn, kc) * SCALE - ls), 0.0)
                p_p = jnp.where(m_prev, jnp.exp(_dot_nt(qn, kp) * SCALE - ls), 0.0)
                p_m = jnp.where(m_meta, jnp.exp(_dot_nt(qn, km) * SCALE - ls), 0.0)
                doh = do_ref[:, hs]
                delta = jnp.sum(doh.astype(F32) * o_ref[:, hs].astype(F32), axis=-1, keepdims=True)
                ds_c = (p_c * (_dot_nt(doh, vc) - delta)).astype(BF)
                ds_p = (p_p * (_dot_nt(doh, vp) - delta)).astype(BF)
                ds_m = (p_m * (_dot_nt(doh, vm) - delta)).astype(BF)
                dqn = (_dot(ds_c, kc) + _dot(ds_p, kp) + _dot(ds_m, km)) * SCALE
                new = [_dot_tn(ds_c, qn) * SCALE, _dot_tn(ds_p, qn) * SCALE, _dot_tn(ds_m, qn) * SCALE,
                       _dot_tn(p_c, doh), _dot_tn(p_p, doh), _dot_tn(p_m, doh)]
                acc = [a + x for a, x in zip(acc, new)]
                sink = sk_ref[:, h:h + 1]
                dsk = dsk + jnp.where(lane == h, -jnp.sum(jnp.exp(sink - ls) * delta), 0.0)
                z = dqn * qg_ref[...]
                dq_ref[:, hs] = rs * z - qh * (rs * rs * rs * jnp.mean(z * qh, axis=-1, keepdims=True))
                dqg = dqg + jnp.sum(dqn * qh * rs, axis=0, keepdims=True)
            for lst, x in zip((dk_c, dk_p, dk_m, dv_c, dv_p, dv_m), acc):
                lst.append(x)
        cat = lambda xs: jnp.concatenate(xs, axis=1)
        r_cur = pl.ds(pl.multiple_of(n * QB, QB), QB)
        r_prev = pl.ds(pl.multiple_of(jnp.maximum(n - 1, 0) * QB, QB), QB)
        dk_ref[r_cur, :] += cat(dk_c)
        dv_ref[r_cur, :] += cat(dv_c)
        dk_ref[r_prev, :] += cat(dk_p)
        dv_ref[r_prev, :] += cat(dv_p)
        dk_ref[0:NMETA, :] += cat(dk_m)
        dv_ref[0:NMETA, :] += cat(dv_m)
        dqg_ref[...] += dqg
        dsk_ref[...] += dsk

    kvspec = pl.BlockSpec((None, lp, KVD), lambda b, n: (b, 0, 0))
    return _call(
        body, f"attn_bwd{j}", (bl, nb),
        [cur(D), cur(KVD), prev(KVD), meta(KVD), cur(KVD), prev(KVD), meta(KVD), cur(D), cur(D), cur(NH),
         pl.BlockSpec((None, 1, HD), lambda b, n: (j, 0, 0)), pl.BlockSpec((None, 1, NH), lambda b, n: (j, 0, 0))],
        [cur(D), kvspec, kvspec, pl.BlockSpec((1, HD), lambda b, n: (0, 0)), pl.BlockSpec((1, NH), lambda b, n: (0, 0))],
        [_sds((t, D), F32), _sds((bl, lp, KVD), F32), _sds((bl, lp, KVD), F32), _sds((1, HD), F32), _sds((1, NH), F32)],
        (q, k, k, k, v, v, v, do, o, lse, qg, sinks), xch=xch)


def kv_bwd_pre(dk0, dk1, dv0, dv1, kv, kng, tm):
    t = kv.shape[0]

    def body(dk0_ref, dk1_ref, dv0_ref, dv1_ref, kv_ref, g_ref, dkv_ref, dg_ref):
        _init(dg_ref, pl.program_id(0) == 0)
        dk = dk0_ref[...] + dk1_ref[...]
        dg = jnp.zeros((1, HD), F32)
        outs = []
        for s in range(NKV):
            sl = slice(HD * s, HD * s + HD)
            dx, dgs = _rms_bwd(kv_ref[:, sl], g_ref[...], dk[:, sl])
            outs.append(dx)
            dg = dg + dgs
        dkv_ref[:, :KVD] = jnp.concatenate(outs, axis=1).astype(BF)
        dkv_ref[:, KVD:] = (dv0_ref[...] + dv1_ref[...]).astype(BF)
        dg_ref[...] += dg

    return pl.pallas_call(
        body, name="kv_bwd_pre", grid=(t // tm,),
        in_specs=[_row(tm, KVD)] * 4 + [_row(tm, 2 * KVD), _res((1, HD))],
        out_specs=[_row(tm, 2 * KVD), _acc((1, HD))], out_shape=[_sds((t, 2 * KVD), BF), _sds((1, HD), F32)],
        compiler_params=_cp(1),
    )(dk0, dk1, dv0, dv1, kv, kng)


def conv_out_bwd(dh1, c, ln_g, ln_b, w_out, i, tm):
    t = dh1.shape[0]

    def body(dh1_ref, c_ref, g_ref, b_ref, w_ref, dc_ref, dg_ref, db_ref, dbo_ref):
        first = pl.program_id(0) == 0
        _init(dg_ref, first)
        _init(db_ref, first)
        _init(dbo_ref, first)
        dh1v = dh1_ref[...]
        ds = _dot_nt(dh1v, w_ref[...])
        cv = c_ref[...]
        xc = cv - jnp.mean(cv, axis=-1, keepdims=True)
        rstd = lax.rsqrt(jnp.mean(xc * xc, axis=-1, keepdims=True) + EPS)
        xh = xc * rstd
        n = xh * g_ref[...] + b_ref[...]
        sg = _sig(n)
        dn = ds * (sg * (1.0 + n * (1.0 - sg)))
        dxh = dn * g_ref[...]
        dc_ref[...] = rstd * (dxh - jnp.mean(dxh, axis=-1, keepdims=True) - xh * jnp.mean(dxh * xh, axis=-1, keepdims=True))
        dg_ref[...] += jnp.sum(dn * xh, axis=0, keepdims=True)
        db_ref[...] += jnp.sum(dn, axis=0, keepdims=True)
        dbo_ref[...] += jnp.sum(dh1v, axis=0, keepdims=True)

    return pl.pallas_call(
        body, name=f"conv_out_bwd{i}", grid=(t // tm,),
        in_specs=[_row(tm, D), _row(tm, D), _lay(i, (1, D)), _lay(i, (1, D)), _lay(i, (D, D))],
        out_specs=[_row(tm, D), _acc((1, D)), _acc((1, D)), _acc((1, D))],
        out_shape=[_sds((t, D), F32)] + [_sds((1, D), F32)] * 3,
        compiler_params=_cp(1),
    )(dh1, c, ln_g, ln_b, w_out)


def conv_mid_bwd(dc, a, big, dw, i, tm, tpb, xch):
    t = dc.shape[0]
    nsteps = t // tm

    def body(dc_ref, nxt_ref, a_ref, prv_ref, big_ref, dw_ref, da_ref, dbin_ref, ddw_ref, dce, ae, wacc, bacc):
        i_ = pl.program_id(0)
        _init(wacc, i_ == 0)
        _init(bacc, i_ == 0)
        dce[0:tm] = dc_ref[...]
        dce[tm:] = jnp.where(i_ % tpb == tpb - 1, 0.0, nxt_ref[...])
        ae[0:HALO] = jnp.where(i_ % tpb == 0, 0.0, prv_ref[...])
        ae[HALO:] = a_ref[...]

        def chunk(k, carry):
            r0 = pl.multiple_of(k * CHUNK, CHUNK)
            wdc = dce[pl.ds(r0, 2 * CHUNK), :]
            wa = ae[pl.ds(r0, 2 * CHUNK), :]
            dcc = wdc[0:CHUNK]
            da = jnp.zeros((CHUNK, D), F32)
            for j in range(CW):
                da = da + dw_ref[j:j + 1, :] * wdc[CW - 1 - j:CW - 1 - j + CHUNK]
                wacc[j] += _fold8(dcc * wa[j + 2:j + 2 + CHUNK])
            bv = big_ref[pl.ds(r0, CHUNK), :]
            a1, sg = bv[:, :D], _sig(bv[:, D:])
            d1 = da * sg
            d2 = da * a1 * sg * (1.0 - sg)
            da_ref[pl.ds(r0, CHUNK), 0:D] = d1.astype(BF)
            da_ref[pl.ds(r0, CHUNK), D:2 * D] = d2.astype(BF)
            bacc[:, 0:D] += _fold8(d1)
            bacc[:, D:2 * D] += _fold8(d2)
            return carry

        lax.fori_loop(0, tm // CHUNK, chunk, 0)

        @pl.when(i_ == nsteps - 1)
        def _():
            dbin_ref[...] = jnp.sum(bacc[...], axis=0, keepdims=True)
            ddw_ref[...] = jnp.sum(wacc[...], axis=1)

    return _call(
        body, f"conv_mid_bwd{i}", (nsteps,),
        [_row(tm, D), _next_halo(tm, t), _row(tm, D), _prev_halo(tm), _row(tm, 2 * D), _lay(i, (CW, D))],
        [_row(tm, 2 * D), _acc((1, 2 * D)), _acc((CW + 1, D))],
        [_sds((t, 2 * D), BF), _sds((1, 2 * D), F32), _sds((CW + 1, D), F32)],
        (dc, dc, a, a, big, dw),
        scratch=[pltpu.VMEM((tm + HALO, D), F32), pltpu.VMEM((tm + HALO, D), F32),
                 pltpu.VMEM((CW + 1, 8, D), F32), pltpu.VMEM((8, 2 * D), F32)], xch=xch)


def input_grads(dh0, seq):
    bl, lp, _ = dh0.shape
    cb = 256

    def body(dh_ref, gx_ref, gm_ref):
        _init(gm_ref, pl.program_id(1) == 0)
        gx_ref[...] = dh_ref[NMETA:NMETA + seq, :]
        gm_ref[...] += dh_ref[0:NMETA, :]

    return pl.pallas_call(
        body, name="input_grads", grid=(D // cb, bl),
        in_specs=[pl.BlockSpec((None, lp, cb), lambda c, b: (b, 0, c))],
        out_specs=[pl.BlockSpec((None, seq, cb), lambda c, b: (b, 0, c)), pl.BlockSpec((NMETA, cb), lambda c, b: (0, c))],
        out_shape=[_sds((bl, seq, D), F32), _sds((NMETA, D), F32)],
        compiler_params=_cp(2),
    )(dh0)


def local_step(x, tgt, meta8, w):
    bl, seq, _ = x.shape
    lp = -(-(NMETA + seq) // QB) * QB
    tpb = 4
    tm = lp // tpb
    t = bl * lp
    na, nbl = 2, 2
    flat = lambda a: a.reshape(t, D)

    h = flat(embed(x, meta8, lp))
    saved = []
    kvs = None
    for l in range(4):
        rec = {"h": h}
        if l < na:
            rec["u"], rec["big"], rec["a"] = conv_in_fwd(h, w["norm_mix"], l, w["conv_w_in"], w["conv_b_in"], l, tm)
            rec["c"], rec["s"] = conv_mid_fwd(rec["a"], w["conv_dw"], w["conv_ln_g"], w["conv_ln_b"], l, tm, tpb)
            rec["h1"], rec["u2"] = mixer_out_fwd(h, rec["s"], w["conv_w_out"], l, w["conv_b_out"], w["norm_ffn"], l, tm,
                                                 f"conv_out_fwd{l}")
        else:
            j = l - na
            if kvs is None:
                kvs = dict(zip(("kn", "kv", "k", "v"), kv_fwd(h, w["kv_norm"], w["w_kv"], w["k_norm"], tm)))
                kvs["h"] = h
            rec["u"], rec["q"] = q_fwd(h, w["norm_mix"], l, w["w_q"], j, tm)
            rec["o"], rec["lse"] = attn_fwd(rec["q"], kvs["k"], kvs["v"], w["q_norm"], w["attn_sinks"], j, bl, lp)
            rec["h1"], rec["u2"] = mixer_out_fwd(h, rec["o"], w["w_o"], j, None, w["norm_ffn"], l, tm, f"attn_out_fwd{j}")
        rec["g"], rec["up"], rec["hid"] = ffn_up_fwd(rec["u2"], w["ffn_w_gate"], w["ffn_w_up"], l, tm // 2)
        h = ffn_down_fwd(rec["hid"], rec["h1"], w["ffn_w_down"], l, tm)
        saved.append(rec)

    dh3, loss_blk = loss_fwd(h.reshape(bl, lp, D), tgt)
    dh = flat(dh3)

    big, small, arrived = {}, {}, {}
    dks, dvs = [], []
    pending = []

    def carried(names, arrivals):
        arrived.update(zip([nm for nm, _ in names], arrivals))

    for l in reversed(range(4)):
        rec = saved[l]
        riders, pending = pending, []
        (dg, du, dh1, small[f"norm_ffn{l}"]), got = ffn_bwd_x(
            dh, rec["g"], rec["up"], rec["h1"], w["norm_ffn"], l, w["ffn_w_down"], w["ffn_w_gate"], w["ffn_w_up"], tm // 2,
            [(big[nm], kind) for nm, kind in riders])
        carried(riders, got)
        big[f"ffn_w_down{l}"] = mm_tn(rec["hid"], dh, tm, f"dw_down{l}")
        big[f"ffn_w_gate{l}"] = mm_tn(rec["u2"], dg, tm, f"dw_gate{l}", split=True)
        big[f"ffn_w_up{l}"] = mm_tn(rec["u2"], du, tm, f"dw_up{l}", split=True)
        riders = [(f"ffn_w_down{l}", "rows"), (f"ffn_w_gate{l}", "pieces"), (f"ffn_w_up{l}", "pieces")]
        xch = [(big[nm], kind) for nm, kind in riders]
        if l >= na:
            j = l - na
            do = out_proj_bwd(dh1, w["w_o"], j, tm, f"attn_out_bwd{j}")
            big[f"w_o{j}"] = mm_tn(rec["o"], dh1, tm, f"dw_o{j}")
            (dq, dk, dv, small[f"q_norm{j}"], small[f"attn_sinks{j}"]), got = attn_bwd(
                rec["q"], kvs["k"], kvs["v"], do, rec["o"], rec["lse"], w["q_norm"], w["attn_sinks"], j, bl, lp, xch)
            carried(riders, got)
            dks.append(dk.reshape(t, KVD))
            dvs.append(dv.reshape(t, KVD))
            big[f"w_q{j}"] = mm_tn(rec["u"], dq, tm, f"dw_q{j}")
            dh, small[f"norm_mix{l}"] = proj_bwd(dq, w["w_q"], j, (D, D), rec["h"], w["norm_mix"], l, dh1, tm, f"q_bwd{j}")
            pending = [(f"w_o{j}", "rows"), (f"w_q{j}", "rows")]
            if l == na:
                dkv, small["k_norm"] = kv_bwd_pre(dks[0], dks[1], dvs[0], dvs[1], kvs["kv"], w["k_norm"], tm)
                big["w_kv"] = mm_tn(kvs["kn"], dkv, tm, "dw_kv")
                dh, small["kv_norm"] = proj_bwd(dkv, w["w_kv"], None, (D, 2 * KVD), kvs["h"], w["kv_norm"], None, dh, tm, "kv_bwd")
                pending.append(("w_kv", "rows"))
        else:
            dc, small[f"conv_ln_g{l}"], small[f"conv_ln_b{l}"], small[f"conv_b_out{l}"] = conv_out_bwd(
                dh1, rec["c"], w["conv_ln_g"], w["conv_ln_b"], w["conv_w_out"], l, tm)
            big[f"conv_w_out{l}"] = mm_tn(rec["s"], dh1, tm, f"dw_conv_out{l}")
            (da, small[f"conv_b_in{l}"], ddw), got = conv_mid_bwd(dc, rec["a"], rec["big"], w["conv_dw"], l, tm, tpb, xch)
            carried(riders, got)
            small[f"conv_dw{l}"] = ddw[:CW]
            big[f"conv_w_in{l}"] = mm_tn(rec["u"], da, tm, f"dw_conv_in{l}", split=True)
            dh, small[f"norm_mix{l}"] = proj_bwd(da, w["conv_w_in"], l, (D, 2 * D), rec["h"], w["norm_mix"], l, dh1, tm,
                                                 f"conv_in_bwd{l}")
            pending = [(f"conv_w_out{l}", "rows"), (f"conv_w_in{l}", "pieces")]
    carried(pending, exchange_grads([(big[nm], kind) for nm, kind in pending]))
    grad_x, small["meta_tokens"] = input_grads(dh.reshape(bl, lp, D), seq)
    return loss_blk, grad_x, big, arrived, small


def all_gather_weights(shards, kinds):
    n = len(shards)

    def out_shape(s, kind):
        if kind == "rows":
            return _sds(s.shape[:-2] + (NDEV * s.shape[-2], s.shape[-1]), s.dtype)
        return _sds((NDEV,) + s.shape, s.dtype)

    def body(*refs):
        srcs, outs = refs[:n], refs[n:2 * n]
        send_sems, recv_sems, local_sems = refs[2 * n:]
        x, y, c = lax.axis_index("x"), lax.axis_index("y"), lax.axis_index("c")
        me, sibling = (x, y, c), (x, y, 1 - c)
        chips = [(1 - x, y), (x, 1 - y), (1 - x, 1 - y)]

        def slot(a, owner):
            idx = 4 * owner[0] + 2 * owner[1] + owner[2]
            if kinds[a] == "rows":
                r = srcs[a].shape[-2]
                if len(srcs[a].shape) == 3:
                    return outs[a].at[:, pl.ds(idx * r, r), :]
                return outs[a].at[pl.ds(idx * r, r), :]
            return outs[a].at[idx]

        def copy(a, k, block, to, src=None):
            return pltpu.make_async_remote_copy(
                src_ref=slot(a, block) if src is None else src, dst_ref=slot(a, block),
                send_sem=send_sems.at[a, k], recv_sem=recv_sems.at[a, k], device_id=to, device_id_type=MESH)

        mine = [pltpu.make_async_copy(srcs[a], slot(a, me), local_sems.at[a]) for a in range(n)]
        for cp in mine:
            cp.start()
        first = []
        for a in range(n):
            first.append(copy(a, 0, me, sibling, src=srcs[a]))
            first += [copy(a, 1 + j, me, (*chip, c), src=srcs[a]) for j, chip in enumerate(chips)]
        for cp in first:
            cp.start()
        passed = []
        for a in range(n):
            for j, chip in enumerate(chips):
                copy(a, 1 + j, (*chip, c), me).wait_recv()
                cp = copy(a, 4 + j, (*chip, c), sibling)
                cp.start()
                passed.append(cp)
        for a in range(n):
            copy(a, 0, sibling, me).wait_recv()
            for j, chip in enumerate(chips):
                copy(a, 4 + j, (*chip, 1 - c), me).wait_recv()
        for cp in first + passed:
            cp.wait_send()
        for cp in mine:
            cp.wait()

    any_spec = pl.BlockSpec(memory_space=pl.ANY)
    return pl.pallas_call(
        body, name="all_gather_weights", out_shape=[out_shape(s, k) for s, k in zip(shards, kinds)],
        in_specs=[any_spec] * n, out_specs=[any_spec] * n,
        scratch_shapes=[pltpu.SemaphoreType.DMA((n, 7)), pltpu.SemaphoreType.DMA((n, 7)), pltpu.SemaphoreType.DMA((n,))],
    )(*shards)


def all_reduce_small(buf):
    rows = buf.shape[0]

    def body(x_ref, o_ref, g_ref, send_sems, recv_sems):
        me = _my_index()
        sends = []
        for k in range(1, NDEV):
            peer = me ^ k
            cp = pltpu.make_async_remote_copy(
                src_ref=x_ref, dst_ref=g_ref.at[me], send_sem=send_sems.at[k - 1], recv_sem=recv_sems.at[k - 1],
                device_id=_coords(peer), device_id_type=MESH)
            cp.start()
            sends.append(cp)
        g_ref[me] = x_ref[...]
        for k in range(1, NDEV):
            peer = me ^ k
            pltpu.make_async_remote_copy(
                src_ref=x_ref, dst_ref=g_ref.at[peer], send_sem=send_sems.at[k - 1], recv_sem=recv_sems.at[k - 1],
                device_id=_coords(peer), device_id_type=MESH).wait_recv()
        for cp in sends:
            cp.wait_send()
        acc = g_ref[0]
        for p in range(1, NDEV):
            acc = acc + g_ref[p]
        o_ref[...] = acc

    return pl.pallas_call(
        body, name="all_reduce_small", out_shape=_sds((rows, D), F32),
        in_specs=[pl.BlockSpec(memory_space=pltpu.VMEM)], out_specs=pl.BlockSpec(memory_space=pltpu.VMEM),
        scratch_shapes=[pltpu.VMEM((NDEV, rows, D), F32), pltpu.SemaphoreType.DMA((7,)), pltpu.SemaphoreType.DMA((7,))],
    )(buf)


def cast_bf16(ws):
    n = len(ws)

    def body(*refs):
        for a in range(n):
            refs[n + a][...] = refs[a][...].astype(BF)

    return pl.pallas_call(
        body, name="cast_bf16", out_shape=[_sds(x.shape, BF) for x in ws],
        compiler_params=pltpu.CompilerParams(vmem_limit_bytes=VMEM_LIMIT),
    )(*ws)


def join_columns(w8, name):
    _, lyr, kk, n8 = w8.shape

    def body(x_ref, o_ref):
        o_ref[...] = jnp.concatenate([x_ref[p] for p in range(NDEV)], axis=1)

    return pl.pallas_call(
        body, name=name, grid=(lyr,),
        in_specs=[pl.BlockSpec((NDEV, None, kk, n8), lambda l: (0, l, 0, 0))],
        out_specs=pl.BlockSpec((None, kk, NDEV * n8), lambda l: (l, 0, 0)),
        out_shape=_sds((lyr, kk, NDEV * n8), w8.dtype), compiler_params=_cp(1),
    )(w8)


def _adamw_math(w, m, v, g):
    m2 = B1 * m + (1.0 - B1) * g
    v2 = B2 * v + (1.0 - B2) * (g * g)
    mh = m2 / (1.0 - B1 ** STEP)
    vh = v2 / (1.0 - B2 ** STEP)
    return -LR * (mh / (jnp.sqrt(vh) + AEPS) + WD * w), m2, v2


def adamw_big(w, m, v, parts, name):
    lyr, r, c = w.shape
    by_cols = c >= 512
    blk = (lyr, r, 256) if by_cols else (lyr, 256 if r % 256 == 0 else r, c)
    imap = (lambda i: (0, 0, i)) if by_cols else (lambda i: (0, i, 0))

    def body(w_ref, m_ref, v_ref, *rest):
        p_refs, (g_ref, d_ref, m2_ref, v2_ref) = rest[:lyr], rest[lyr:]
        for l in range(lyr):
            g = p_refs[l][0].astype(F32)
            for q in range(1, NDEV):
                g = g + p_refs[l][q].astype(F32)
            g_ref[l] = g
            d_ref[l], m2_ref[l], v2_ref[l] = _adamw_math(w_ref[l], m_ref[l], v_ref[l], g)

    spec = pl.BlockSpec(blk, imap)
    pspec = pl.BlockSpec((NDEV,) + blk[1:], imap)
    return pl.pallas_call(
        body, name=name, grid=((c // 256) if by_cols else (r // blk[1]),),
        in_specs=[spec, spec, spec] + [pspec] * lyr,
        out_specs=[spec] * 4, out_shape=[_sds((lyr, r, c), F32)] * 4, compiler_params=_cp(1),
    )(w, m, v, *parts)


def adamw_small(w, m, v, g, name):
    def body(w_ref, m_ref, v_ref, g_ref, d_ref, m2_ref, v2_ref):
        d_ref[...], m2_ref[...], v2_ref[...] = _adamw_math(w_ref[...], m_ref[...], v_ref[...], g_ref[...])

    return pl.pallas_call(body, name=name, out_shape=[_sds(w.shape, F32)] * 3)(w, m, v, g)


NAMES = ["meta_tokens", "norm_mix", "norm_ffn", "conv_w_in", "conv_b_in", "conv_dw", "conv_ln_g", "conv_ln_b", "conv_w_out",
         "conv_b_out", "kv_norm", "w_kv", "k_norm", "w_q", "q_norm", "attn_sinks", "w_o", "ffn_w_gate", "ffn_w_up", "ffn_w_down"]
BIG = {"conv_w_in": "pieces", "conv_w_out": "rows", "w_kv": "rows", "w_q": "rows", "w_o": "rows",
       "ffn_w_gate": "pieces", "ffn_w_up": "pieces", "ffn_w_down": "rows"}
REP_ROWS = 16


def _pad_cols(a, width):
    return jnp.pad(a, ((0, 0), (0, width - a.shape[1])))


def _pack_rep(p):
    rows = [p["norm_mix"], p["norm_ffn"], p["kv_norm"].reshape(1, D), _pad_cols(p["k_norm"].reshape(1, HD), D),
            _pad_cols(p["q_norm"], D), _pad_cols(p["attn_sinks"], D), jnp.zeros((2, D), F32)]
    return jnp.concatenate(rows, axis=0)


def _unpack_rep(a):
    return {"norm_mix": a[0:4], "norm_ffn": a[4:8], "kv_norm": a[8], "k_norm": a[9, :HD], "q_norm": a[10:12, :HD],
            "attn_sinks": a[12:14, :NH]}


SH_NAMES = ["meta_tokens", "conv_b_in", "conv_dw", "conv_ln_g", "conv_ln_b", "conv_b_out"]


def _pack_sh(p):
    c = D // NDEV
    rows = [p["meta_tokens"], p["conv_b_in"].reshape(4, c), p["conv_dw"].reshape(2 * CW, c), p["conv_ln_g"], p["conv_ln_b"],
            p["conv_b_out"]]
    return jnp.concatenate(rows, axis=0)


def _unpack_sh(a):
    c = D // NDEV
    return {"meta_tokens": a[0:16], "conv_b_in": a[16:20].reshape(2, 2 * c), "conv_dw": a[20:82].reshape(2, CW, c),
            "conv_ln_g": a[82:84], "conv_ln_b": a[84:86], "conv_b_out": a[86:88]}


def kernel(x, meta_tokens, norm_mix, norm_ffn, conv_w_in, conv_b_in, conv_dw, conv_ln_g, conv_ln_b, conv_w_out, conv_b_out, kv_norm, w_kv, k_norm, w_q, q_norm, attn_sinks, w_o, ffn_w_gate, ffn_w_up, ffn_w_down, loss_target, m_meta_tokens, m_norm_mix, m_norm_ffn, m_conv_w_in, m_conv_b_in, m_conv_dw, m_conv_ln_g, m_conv_ln_b, m_conv_w_out, m_conv_b_out, m_kv_norm, m_w_kv, m_k_norm, m_w_q, m_q_norm, m_attn_sinks, m_w_o, m_ffn_w_gate, m_ffn_w_up, m_ffn_w_down, v_meta_tokens, v_norm_mix, v_norm_ffn, v_conv_w_in, v_conv_b_in, v_conv_dw, v_conv_ln_g, v_conv_ln_b, v_conv_w_out, v_conv_b_out, v_kv_norm, v_w_kv, v_k_norm, v_w_q, v_q_norm, v_attn_sinks, v_w_o, v_ffn_w_gate, v_ffn_w_up, v_ffn_w_down):
    wts = dict(zip(NAMES, (meta_tokens, norm_mix, norm_ffn, conv_w_in, conv_b_in, conv_dw, conv_ln_g, conv_ln_b, conv_w_out,
                           conv_b_out, kv_norm, w_kv, k_norm, w_q, q_norm, attn_sinks, w_o, ffn_w_gate, ffn_w_up, ffn_w_down)))
    mom = dict(zip(NAMES, (m_meta_tokens, m_norm_mix, m_norm_ffn, m_conv_w_in, m_conv_b_in, m_conv_dw, m_conv_ln_g, m_conv_ln_b,
                           m_conv_w_out, m_conv_b_out, m_kv_norm, m_w_kv, m_k_norm, m_w_q, m_q_norm, m_attn_sinks, m_w_o,
                           m_ffn_w_gate, m_ffn_w_up, m_ffn_w_down)))
    var = dict(zip(NAMES, (v_meta_tokens, v_norm_mix, v_norm_ffn, v_conv_w_in, v_conv_b_in, v_conv_dw, v_conv_ln_g, v_conv_ln_b,
                           v_conv_w_out, v_conv_b_out, v_kv_norm, v_w_kv, v_k_norm, v_w_q, v_q_norm, v_attn_sinks, v_w_o,
                           v_ffn_w_gate, v_ffn_w_up, v_ffn_w_down)))
    me = _my_index()
    c8 = D // NDEV

    big_names = list(BIG)
    shards = cast_bf16([wts[k] for k in big_names])
    vec_names = ["meta_tokens", "conv_b_in", "conv_dw", "conv_ln_g", "conv_ln_b", "conv_b_out"]
    gathered = all_gather_weights(list(shards) + [wts[k] for k in vec_names],
                                  [BIG[k] for k in big_names] + ["flat"] * len(vec_names))
    full = dict(zip(big_names + vec_names, gathered))
    w = {}
    for k in big_names:
        w[k] = join_columns(full[k] if full[k].ndim == 4 else full[k][:, None], "join_" + k) if BIG[k] == "pieces" else full[k]
    join_vec = lambda a: jnp.moveaxis(a, 0, -2).reshape(a.shape[1:-1] + (NDEV * a.shape[-1],))
    w["conv_b_in"] = join_vec(full["conv_b_in"]).reshape(2, 1, 2 * D)
    w["conv_dw"] = join_vec(full["conv_dw"])
    for k in ("conv_ln_g", "conv_ln_b", "conv_b_out"):
        w[k] = join_vec(full[k]).reshape(2, 1, D)
    w["norm_mix"] = norm_mix.reshape(4, 1, D)
    w["norm_ffn"] = norm_ffn.reshape(4, 1, D)
    w["kv_norm"] = kv_norm.reshape(1, D)
    w["k_norm"] = k_norm.reshape(1, HD)
    w["q_norm"] = q_norm.reshape(2, 1, HD)
    w["attn_sinks"] = attn_sinks.reshape(2, 1, NH)

    loss_blk, grad_x, _, arrived, gs = local_step(x, loss_target, full["meta_tokens"], w)

    stack = lambda k, n: jnp.concatenate([gs[f"{k}{i}"] for i in range(n)], axis=0)
    rep = {"norm_mix": stack("norm_mix", 4), "norm_ffn": stack("norm_ffn", 4), "kv_norm": gs["kv_norm"], "k_norm": gs["k_norm"],
           "q_norm": stack("q_norm", 2), "attn_sinks": stack("attn_sinks", 2)}
    loss_row = _pad_cols(loss_blk[0:1, 0:1], D)
    packed = jnp.concatenate(
        [_pack_rep(rep)[:14], loss_row, jnp.zeros((1, D), F32), gs["meta_tokens"], stack("conv_b_in", 2).reshape(4, D),
         stack("conv_dw", 2), stack("conv_ln_g", 2), stack("conv_ln_b", 2), stack("conv_b_out", 2)], axis=0)
    red = all_reduce_small(packed)
    loss = red[14, 0]
    cols = lambda a, width: lax.dynamic_slice_in_dim(a, me * width, width, axis=1)
    g_sh = jnp.concatenate(
        [cols(red[16:32], c8), cols(red[32:36].reshape(2, 2 * D), 2 * c8).reshape(4, c8), cols(red[36:98], c8),
         cols(red[98:100], c8), cols(red[100:102], c8), cols(red[102:104], c8)], axis=0)
    g_rep = red[0:REP_ROWS].at[14:].set(0.0)

    grads, delta, new_m, new_v = {}, {}, {}, {}
    for k in big_names:
        flat2 = wts[k].ndim == 2
        parts = [arrived[k]] if flat2 else [arrived[f"{k}{i}"] for i in range(wts[k].shape[0])]
        as3 = (lambda a: a[None]) if flat2 else (lambda a: a)
        outs = adamw_big(as3(wts[k]), as3(mom[k]), as3(var[k]), parts, "adamw_" + k)
        grads[k], delta[k], new_m[k], new_v[k] = [o[0] if flat2 else o for o in outs]
    d_rep, m_rep, v_rep = adamw_small(_pack_rep(wts), _pack_rep(mom), _pack_rep(var), g_rep, "adamw_rep")
    d_sh, m_sh, v_sh = adamw_small(_pack_sh(wts), _pack_sh(mom), _pack_sh(var), g_sh, "adamw_sh")
    for dst, a_rep, a_sh in ((grads, g_rep, g_sh), (delta, d_rep, d_sh), (new_m, m_rep, m_sh), (new_v, v_rep, v_sh)):
        dst.update(_unpack_rep(a_rep))
        dst.update(_unpack_sh(a_sh))
    return (loss, grad_x, *[grads[k] for k in NAMES], *[delta[k] for k in NAMES], *[new_m[k] for k in NAMES],
            *[new_v[k] for k in NAMES])
```

```python
import functools

import jax
import jax.numpy as jnp
from jax import lax
from jax.experimental import pallas as pl
from jax.experimental.pallas import tpu as pltpu

F32 = jnp.float32
BF = jnp.bfloat16

D = 1024
DFF = 2816
NH = 16
NKV = 4
HD = 64
KVD = NKV * HD
NMETA = 16
CW = 31
HALO = 32
CHUNK = 32
QB = 128
EPS = 1e-6
NEG = -1e30
NDEV = 8
SCALE = HD ** -0.5

LR, B1, B2, AEPS, WD, STEP = 0.001, 0.9, 0.999, 1e-08, 0.01, 10

VMEM_LIMIT = 56 * 2 ** 20
MESH = pl.DeviceIdType.MESH


def _cp(n):
    return pltpu.CompilerParams(dimension_semantics=("arbitrary",) * n, vmem_limit_bytes=VMEM_LIMIT)


def _row(tm, c):
    return pl.BlockSpec((tm, c), lambda i: (i, 0))


def _res(shape):
    return pl.BlockSpec(shape, lambda i: (0,) * len(shape), pipeline_mode=pl.Buffered(1))


def _lay(l, shape):
    return pl.BlockSpec((None,) + tuple(shape), lambda i: (l,) + (0,) * len(shape), pipeline_mode=pl.Buffered(1))


def _acc(shape):
    return pl.BlockSpec(shape, lambda i: (0,) * len(shape))


def _sds(shape, dt):
    return jax.ShapeDtypeStruct(tuple(shape), dt)


def _dot(a, b):
    return jnp.dot(a.astype(BF), b.astype(BF), preferred_element_type=F32)


def _dot_nt(a, b):
    return lax.dot_general(a.astype(BF), b.astype(BF), (((1,), (1,)), ((), ())), preferred_element_type=F32)


def _dot_tn(a, b):
    return lax.dot_general(a.astype(BF), b.astype(BF), (((0,), (0,)), ((), ())), preferred_element_type=F32)


def _rstd(x):
    return lax.rsqrt(jnp.mean(x * x, axis=-1, keepdims=True) + EPS)


def _rms_bwd(x, g, dy):
    r = _rstd(x)
    z = dy * g
    dx = r * z - x * (r * r * r * jnp.mean(z * x, axis=-1, keepdims=True))
    return dx, jnp.sum(dy * x * r, axis=0, keepdims=True)


def _sig(x):
    return jax.nn.sigmoid(x)


def _fold8(x):
    out = x[0:8]
    for k in range(1, x.shape[0] // 8):
        out = out + x[8 * k:8 * k + 8]
    return out


def _init(ref, first):
    @pl.when(first)
    def _():
        ref[...] = jnp.zeros_like(ref)


def _my_index():
    return 4 * lax.axis_index("x") + 2 * lax.axis_index("y") + lax.axis_index("c")


def _coords(idx):
    return (idx // 4, (idx // 2) % 2, idx % 2)


def _xch_shapes(xch):
    return [_sds((NDEV,) + ((a.shape[0] // NDEV, a.shape[1]) if k == "rows" else tuple(a.shape[1:])), a.dtype) for a, k in xch]


def _xch_scratch(n):
    return [pltpu.SemaphoreType.DMA((n, NDEV - 1)), pltpu.SemaphoreType.DMA((n, NDEV - 1)), pltpu.SemaphoreType.DMA((n,))]


def _xch_copies(kinds, srcs, outs, sems, arrivals):
    send_sems, recv_sems, local_sems = sems
    me = _my_index()

    def piece(a, p):
        if kinds[a] == "rows":
            r = srcs[a].shape[0] // NDEV
            return srcs[a].at[pl.ds(p * r, r), :]
        return srcs[a].at[p]

    def remote(a, k, src, slot):
        return pltpu.make_async_remote_copy(
            src_ref=src, dst_ref=outs[a].at[slot], send_sem=send_sems.at[a, k - 1], recv_sem=recv_sems.at[a, k - 1],
            device_id=_coords(me ^ k), device_id_type=MESH)

    n = len(kinds)
    local = [pltpu.make_async_copy(piece(a, me), outs[a].at[me], local_sems.at[a]) for a in range(n)]
    sends = [remote(a, k, piece(a, me ^ k), me) for a in range(n) for k in range(1, NDEV)]
    recvs = [remote(a, k, piece(a, me), me ^ k) for a in range(n) for k in range(1, NDEV)] if arrivals else []
    return local, sends, recvs


def _xch_start(kinds, srcs, outs, sems):
    local, sends, _ = _xch_copies(kinds, srcs, outs, sems, False)
    for cp in local + sends:
        cp.start()


def _xch_wait(kinds, srcs, outs, sems):
    local, sends, recvs = _xch_copies(kinds, srcs, outs, sems, True)
    for cp in recvs:
        cp.wait_recv()
    for cp in sends:
        cp.wait_send()
    for cp in local:
        cp.wait()


def _call(body, name, grid, in_specs, out_specs, out_shape, args, scratch=(), xch=()):
    n_in, n_out, n_x, n_s = len(in_specs), len(out_specs), len(xch), len(scratch)
    kinds = [k for _, k in xch]

    def wrapped(*refs):
        ins, srcs = refs[:n_in], refs[n_in:n_in + n_x]
        outs, arrivals = refs[n_in + n_x:n_in + n_x + n_out], refs[n_in + n_x + n_out:n_in + 2 * n_x + n_out]
        rest = refs[n_in + 2 * n_x + n_out:]
        if n_x:
            ids = [pl.program_id(d) for d in range(len(grid))]
            first, last = ids[0] == 0, ids[0] == grid[0] - 1
            for d in range(1, len(grid)):
                first, last = first & (ids[d] == 0), last & (ids[d] == grid[d] - 1)

            @pl.when(first)
            def _():
                _xch_start(kinds, srcs, arrivals, rest[n_s:])

        body(*ins, *outs, *rest[:n_s])
        if n_x:
            @pl.when(last)
            def _():
                _xch_wait(kinds, srcs, arrivals, rest[n_s:])

    any_spec = pl.BlockSpec(memory_space=pl.ANY)
    res = pl.pallas_call(
        wrapped, name=name, grid=grid, in_specs=list(in_specs) + [any_spec] * n_x, out_specs=list(out_specs) + [any_spec] * n_x,
        out_shape=list(out_shape) + _xch_shapes(xch), scratch_shapes=list(scratch) + (_xch_scratch(n_x) if n_x else []),
        compiler_params=_cp(len(grid)),
    )(*args, *[a for a, _ in xch])
    return res[:n_out], res[n_out:]


def exchange_grads(xch):
    kinds = [k for _, k in xch]
    n = len(xch)

    def body(*refs):
        _xch_start(kinds, refs[:n], refs[n:2 * n], refs[2 * n:])
        _xch_wait(kinds, refs[:n], refs[n:2 * n], refs[2 * n:])

    any_spec = pl.BlockSpec(memory_space=pl.ANY)
    return pl.pallas_call(
        body, name="exchange_grads", out_shape=_xch_shapes(xch), in_specs=[any_spec] * n, out_specs=[any_spec] * n,
        scratch_shapes=_xch_scratch(n),
    )(*[a for a, _ in xch])


def embed(x, meta8, lp):
    bl, seq, _ = x.shape
    c8 = D // NDEV
    cb = 2 * c8

    def body(x_ref, m_ref, h_ref):
        h_ref[0:NMETA, :] = jnp.concatenate([m_ref[0], m_ref[1]], axis=1)
        h_ref[NMETA:NMETA + seq, :] = x_ref[...]
        h_ref[NMETA + seq:, :] = jnp.zeros((lp - NMETA - seq, cb), F32)

    return pl.pallas_call(
        body, name="embed", grid=(bl, D // cb),
        in_specs=[pl.BlockSpec((None, seq, cb), lambda b, c: (b, 0, c)), pl.BlockSpec((2, NMETA, c8), lambda b, c: (c, 0, 0))],
        out_specs=pl.BlockSpec((None, lp, cb), lambda b, c: (b, 0, c)), out_shape=_sds((bl, lp, D), F32),
        compiler_params=_cp(2),
    )(x, meta8)


def conv_in_fwd(h, nm, l, w_in, b_in, i, tm):
    t = h.shape[0]

    def body(h_ref, g_ref, w_ref, b_ref, u_ref, big_ref, a_ref):
        x = h_ref[...]
        ub = (x * _rstd(x) * g_ref[...]).astype(BF)
        u_ref[...] = ub
        big = jnp.dot(ub, w_ref[...], preferred_element_type=F32) + b_ref[...]
        big_ref[...] = big
        a_ref[...] = big[:, :D] * _sig(big[:, D:])

    return pl.pallas_call(
        body, name=f"conv_in_fwd{i}", grid=(t // tm,),
        in_specs=[_row(tm, D), _lay(l, (1, D)), _lay(i, (D, 2 * D)), _lay(i, (1, 2 * D))],
        out_specs=[_row(tm, D), _row(tm, 2 * D), _row(tm, D)],
        out_shape=[_sds((t, D), BF), _sds((t, 2 * D), F32), _sds((t, D), F32)],
        compiler_params=_cp(1),
    )(h, nm, w_in, b_in)


def _prev_halo(tm):
    return pl.BlockSpec((HALO, D), lambda i: (jnp.maximum(i * (tm // HALO) - 1, 0), 0))


def _next_halo(tm, t):
    return pl.BlockSpec((HALO, D), lambda i: (jnp.minimum((i + 1) * (tm // HALO), t // HALO - 1), 0))


def conv_mid_fwd(a, dw, ln_g, ln_b, i, tm, tpb):
    t = a.shape[0]

    def body(a_ref, halo_ref, dw_ref, g_ref, b_ref, c_ref, s_ref, ext):
        first = pl.program_id(0) % tpb == 0
        ext[0:HALO] = jnp.where(first, 0.0, halo_ref[...])
        ext[HALO:] = a_ref[...]

        def chunk(k, carry):
            r0 = pl.multiple_of(k * CHUNK, CHUNK)
            win = ext[pl.ds(r0, 2 * CHUNK), :]
            c = jnp.zeros((CHUNK, D), F32)
            for j in range(CW):
                c = c + dw_ref[j:j + 1, :] * win[j + 2:j + 2 + CHUNK]
            c_ref[pl.ds(r0, CHUNK), :] = c
            mu = jnp.mean(c, axis=-1, keepdims=True)
            xc = c - mu
            n = xc * lax.rsqrt(jnp.mean(xc * xc, axis=-1, keepdims=True) + EPS) * g_ref[...] + b_ref[...]
            s_ref[pl.ds(r0, CHUNK), :] = (n * _sig(n)).astype(BF)
            return carry

        lax.fori_loop(0, tm // CHUNK, chunk, 0)

    return pl.pallas_call(
        body, name=f"conv_mid_fwd{i}", grid=(t // tm,),
        in_specs=[_row(tm, D), _prev_halo(tm), _lay(i, (CW, D)), _lay(i, (1, D)), _lay(i, (1, D))],
        out_specs=[_row(tm, D), _row(tm, D)],
        out_shape=[_sds((t, D), F32), _sds((t, D), BF)],
        scratch_shapes=[pltpu.VMEM((tm + HALO, D), F32)],
        compiler_params=_cp(1),
    )(a, a, dw, ln_g, ln_b)


def mixer_out_fwd(h, s, w, lw, bias, nf, l, tm, name):
    t = h.shape[0]

    def body(*refs):
        if bias is None:
            h_ref, s_ref, w_ref, g_ref, h1_ref, u_ref = refs
            y = 0.0
        else:
            h_ref, s_ref, w_ref, b_ref, g_ref, h1_ref, u_ref = refs
            y = b_ref[...]
        h1 = h_ref[...] + (jnp.dot(s_ref[...], w_ref[...], preferred_element_type=F32) + y)
        h1_ref[...] = h1
        u_ref[...] = (h1 * _rstd(h1) * g_ref[...]).astype(BF)

    ins = [h, s, w] + ([] if bias is None else [bias]) + [nf]
    specs = [_row(tm, D), _row(tm, D), _lay(lw, (D, D))] + ([] if bias is None else [_lay(lw, (1, D))]) + [_lay(l, (1, D))]
    return pl.pallas_call(
        body, name=name, grid=(t // tm,), in_specs=specs,
        out_specs=[_row(tm, D), _row(tm, D)], out_shape=[_sds((t, D), F32), _sds((t, D), BF)],
        compiler_params=_cp(1),
    )(*ins)


def ffn_up_fwd(u, wg, wu, l, tm):
    t = u.shape[0]

    def body(u_ref, wg_ref, wu_ref, g_ref, up_ref, hid_ref):
        ub = u_ref[...]
        g = jnp.dot(ub, wg_ref[...], preferred_element_type=F32)
        up = jnp.dot(ub, wu_ref[...], preferred_element_type=F32)
        g_ref[...] = g.astype(BF)
        up_ref[...] = up.astype(BF)
        hid_ref[...] = (g * _sig(g) * up).astype(BF)

    return pl.pallas_call(
        body, name=f"ffn_up_fwd{l}", grid=(t // tm,),
        in_specs=[_row(tm, D), _lay(l, (D, DFF)), _lay(l, (D, DFF))],
        out_specs=[_row(tm, DFF)] * 3, out_shape=[_sds((t, DFF), BF)] * 3,
        compiler_params=_cp(1),
    )(u, wg, wu)


def ffn_down_fwd(hid, h1, wd, l, tm):
    t = h1.shape[0]

    def body(hid_ref, h1_ref, w_ref, h2_ref):
        h2_ref[...] = h1_ref[...] + jnp.dot(hid_ref[...], w_ref[...], preferred_element_type=F32)

    return pl.pallas_call(
        body, name=f"ffn_down_fwd{l}", grid=(t // tm,),
        in_specs=[_row(tm, DFF), _row(tm, D), _lay(l, (DFF, D))],
        out_specs=_row(tm, D), out_shape=_sds((t, D), F32),
        compiler_params=_cp(1),
    )(hid, h1, wd)


def _seg_rms(x, g, nseg):
    outs = []
    for s in range(nseg):
        xs = x[:, HD * s:HD * s + HD]
        outs.append(xs * _rstd(xs) * g)
    return jnp.concatenate(outs, axis=1)


def kv_fwd(h, kvn, w_kv, kng, tm):
    t = h.shape[0]

    def body(h_ref, g_ref, w_ref, kg_ref, kn_ref, kv_ref, k_ref, v_ref):
        x = h_ref[...]
        kn = (x * _rstd(x) * g_ref[...]).astype(BF)
        kn_ref[...] = kn
        kv = jnp.dot(kn, w_ref[...], preferred_element_type=F32)
        kv_ref[...] = kv
        k_ref[...] = _seg_rms(kv[:, :KVD], kg_ref[...], NKV).astype(BF)
        v_ref[...] = kv[:, KVD:].astype(BF)

    return pl.pallas_call(
        body, name="kv_fwd", grid=(t // tm,),
        in_specs=[_row(tm, D), _res((1, D)), _res((D, 2 * KVD)), _res((1, HD))],
        out_specs=[_row(tm, D), _row(tm, 2 * KVD), _row(tm, KVD), _row(tm, KVD)],
        out_shape=[_sds((t, D), BF), _sds((t, 2 * KVD), F32), _sds((t, KVD), BF), _sds((t, KVD), BF)],
        compiler_params=_cp(1),
    )(h, kvn, w_kv, kng)


def q_fwd(h, nm, l, w_q, j, tm):
    t = h.shape[0]

    def body(h_ref, g_ref, w_ref, u_ref, q_ref):
        x = h_ref[...]
        ub = (x * _rstd(x) * g_ref[...]).astype(BF)
        u_ref[...] = ub
        q_ref[...] = jnp.dot(ub, w_ref[...], preferred_element_type=F32)

    return pl.pallas_call(
        body, name=f"q_fwd{j}", grid=(t // tm,),
        in_specs=[_row(tm, D), _lay(l, (1, D)), _lay(j, (D, D))],
        out_specs=[_row(tm, D), _row(tm, D)], out_shape=[_sds((t, D), BF), _sds((t, D), F32)],
        compiler_params=_cp(1),
    )(h, nm, w_q)


RQ = NH // NKV


def _attn_specs(nb, lp):
    cur = lambda c: pl.BlockSpec((QB, c), lambda b, n: (b * nb + n, 0))
    seq = pl.BlockSpec((None, lp, KVD), lambda b, n: (b, 0, 0))
    return cur, seq


def _attn_masks(n, start):
    rows = RQ * QB
    qpos = n * QB + (lax.broadcasted_iota(jnp.int32, (rows, 2 * QB), 0) & (QB - 1))
    kpos = start + lax.broadcasted_iota(jnp.int32, (rows, 2 * QB), 1)
    band = (kpos <= qpos) & (qpos - kpos < QB) & (kpos >= NMETA)
    qm = n * QB + (lax.broadcasted_iota(jnp.int32, (rows, NMETA), 0) & (QB - 1))
    meta = lax.broadcasted_iota(jnp.int32, (rows, NMETA), 1) <= qm
    return band, meta


def _stack_heads(ref, g, fn):
    return jnp.concatenate([fn(ref[:, HD * (g * RQ + r):HD * (g * RQ + r) + HD]) for r in range(RQ)], axis=0)


def _stack_cols(ref, g):
    return jnp.concatenate([ref[:, g * RQ + r:g * RQ + r + 1] for r in range(RQ)], axis=0)


def _stack_sinks(sk_ref, g):
    return jnp.concatenate([jnp.broadcast_to(sk_ref[:, g * RQ + r:g * RQ + r + 1], (QB, 1)) for r in range(RQ)], axis=0)


def attn_fwd(q, k, v, qg, sinks, j, bl, lp):
    t = q.shape[0]
    nb = lp // QB
    cur, seq = _attn_specs(nb, lp)

    def body(q_ref, k_ref, v_ref, qg_ref, sk_ref, o_ref, lse_ref):
        n = pl.program_id(1)
        start = pl.multiple_of(jnp.maximum(n - 1, 0) * QB, QB)
        m_band, m_meta = _attn_masks(n, start)
        band = pl.ds(start, 2 * QB)
        lane = lax.broadcasted_iota(jnp.int32, (QB, NH), 1)
        lse = jnp.zeros((QB, NH), F32)
        for g in range(NKV):
            gs = slice(HD * g, HD * g + HD)
            qn = _stack_heads(q_ref, g, lambda x: (x * _rstd(x) * qg_ref[...]).astype(BF))
            sink = _stack_sinks(sk_ref, g)
            s_b = jnp.where(m_band, _dot_nt(qn, k_ref[band, gs]) * SCALE, NEG)
            s_m = jnp.where(m_meta, _dot_nt(qn, k_ref[0:NMETA, gs]) * SCALE, NEG)
            mx = jnp.maximum(jnp.maximum(jnp.max(s_b, -1, keepdims=True), jnp.max(s_m, -1, keepdims=True)), sink)
            p_b, p_m = jnp.exp(s_b - mx), jnp.exp(s_m - mx)
            den = jnp.sum(p_b, -1, keepdims=True) + jnp.sum(p_m, -1, keepdims=True) + jnp.exp(sink - mx)
            inv = 1.0 / den
            o = _dot(p_b * inv, v_ref[band, gs]) + _dot(p_m * inv, v_ref[0:NMETA, gs])
            l = mx + jnp.log(den)
            for r in range(RQ):
                h = g * RQ + r
                o_ref[:, HD * h:HD * h + HD] = o[r * QB:(r + 1) * QB].astype(BF)
                lse = jnp.where(lane == h, l[r * QB:(r + 1) * QB], lse)
        lse_ref[...] = lse

    return pl.pallas_call(
        body, name=f"attn_fwd{j}", grid=(bl, nb),
        in_specs=[cur(D), seq, seq,
                  pl.BlockSpec((None, 1, HD), lambda b, n: (j, 0, 0)), pl.BlockSpec((None, 1, NH), lambda b, n: (j, 0, 0))],
        out_specs=[cur(D), cur(NH)], out_shape=[_sds((t, D), BF), _sds((t, NH), F32)],
        compiler_params=_cp(2),
    )(q, k, v, qg, sinks)


def loss_fwd(h, tgt):
    bl, lp, _ = h.shape
    seq = tgt.shape[1]
    cb = 256

    def body(h_ref, t_ref, dh_ref, loss_ref):
        _init(loss_ref, (pl.program_id(0) == 0) & (pl.program_id(1) == 0))
        err = h_ref[NMETA:NMETA + seq, :] - t_ref[...]
        dh_ref[...] = jnp.zeros_like(dh_ref)
        dh_ref[NMETA:NMETA + seq, :] = err * (1.0 / D)
        loss_ref[...] += (0.5 / D) * jnp.sum(err * err)

    return pl.pallas_call(
        body, name="loss_fwd", grid=(bl, D // cb),
        in_specs=[pl.BlockSpec((None, lp, cb), lambda b, c: (b, 0, c)), pl.BlockSpec((None, seq, cb), lambda b, c: (b, 0, c))],
        out_specs=[pl.BlockSpec((None, lp, cb), lambda b, c: (b, 0, c)), pl.BlockSpec((8, 128), lambda b, c: (0, 0))],
        out_shape=[_sds((bl, lp, D), F32), _sds((8, 128), F32)],
        compiler_params=_cp(2),
    )(h, tgt)


def ffn_bwd_x(dh2, g, up, h1, nf, l, wd, wg, wu, tm, xch):
    t = dh2.shape[0]

    def body(dh2_ref, g_ref, up_ref, h1_ref, nf_ref, wd_ref, wg_ref, wu_ref, dg_ref, du_ref, dh1_ref, dnf_ref):
        _init(dnf_ref, pl.program_id(0) == 0)
        dh2v = dh2_ref[...]
        dhid = _dot_nt(dh2v, wd_ref[...])
        gv = g_ref[...].astype(F32)
        uv = up_ref[...].astype(F32)
        sg = _sig(gv)
        dgv = (dhid * uv * (sg * (1.0 + gv * (1.0 - sg)))).astype(BF)
        duv = (dhid * (gv * sg)).astype(BF)
        dg_ref[...] = dgv
        du_ref[...] = duv
        dnorm = _dot_nt(dgv, wg_ref[...]) + _dot_nt(duv, wu_ref[...])
        dx, dnf = _rms_bwd(h1_ref[...], nf_ref[...], dnorm)
        dh1_ref[...] = dh2v + dx
        dnf_ref[...] += dnf

    return _call(
        body, f"ffn_bwd_x{l}", (t // tm,),
        [_row(tm, D), _row(tm, DFF), _row(tm, DFF), _row(tm, D), _lay(l, (1, D)),
         _lay(l, (DFF, D)), _lay(l, (D, DFF)), _lay(l, (D, DFF))],
        [_row(tm, DFF), _row(tm, DFF), _row(tm, D), _acc((1, D))],
        [_sds((t, DFF), BF), _sds((t, DFF), BF), _sds((t, D), F32), _sds((1, D), F32)],
        (dh2, g, up, h1, nf, wd, wg, wu), xch=xch)


def mm_tn(x, dy, tm, name, split=False):
    t, kk = x.shape
    nn = dy.shape[1]
    n8 = nn // NDEV
    nsteps = t // tm

    def body(x_ref, dy_ref, o_ref, acc):
        i = pl.program_id(0)
        _init(acc, i == 0)
        acc[...] += _dot_tn(x_ref[...], dy_ref[...])

        @pl.when(i == nsteps - 1)
        def _():
            if split:
                for p in range(NDEV):
                    o_ref[p] = acc[:, p * n8:(p + 1) * n8].astype(BF)
            else:
                o_ref[...] = acc[...].astype(BF)

    oshape = (NDEV, kk, n8) if split else (kk, nn)
    return pl.pallas_call(
        body, name=name, grid=(nsteps,), in_specs=[_row(tm, kk), _row(tm, nn)],
        out_specs=_acc(oshape), out_shape=_sds(oshape, BF), scratch_shapes=[pltpu.VMEM((kk, nn), F32)],
        compiler_params=_cp(1),
    )(x, dy)


def proj_bwd(dy, w, lw, wshape, h, g, lg, dh_in, tm, name):
    t = h.shape[0]
    nn = dy.shape[1]
    wspec = _res(wshape) if lw is None else _lay(lw, wshape)
    gspec = _res((1, D)) if lg is None else _lay(lg, (1, D))

    def body(dy_ref, w_ref, h_ref, g_ref, dhin_ref, dh_ref, dg_ref):
        _init(dg_ref, pl.program_id(0) == 0)
        du = _dot_nt(dy_ref[...], w_ref[...])
        dx, dg = _rms_bwd(h_ref[...], g_ref[...], du)
        dh_ref[...] = dhin_ref[...] + dx
        dg_ref[...] += dg

    return pl.pallas_call(
        body, name=name, grid=(t // tm,),
        in_specs=[_row(tm, nn), wspec, _row(tm, D), gspec, _row(tm, D)],
        out_specs=[_row(tm, D), _acc((1, D))], out_shape=[_sds((t, D), F32), _sds((1, D), F32)],
        compiler_params=_cp(1),
    )(dy, w, h, g, dh_in)


def out_proj_bwd(dh1, w, lw, tm, name):
    t = dh1.shape[0]

    def body(dh1_ref, w_ref, do_ref):
        do_ref[...] = _dot_nt(dh1_ref[...], w_ref[...]).astype(BF)

    return pl.pallas_call(
        body, name=name, grid=(t // tm,), in_specs=[_row(tm, D), _lay(lw, (D, D))],
        out_specs=_row(tm, D), out_shape=_sds((t, D), BF), compiler_params=_cp(1),
    )(dh1, w)


def attn_bwd(q, k, v, do, o, lse, qg, sinks, j, bl, lp, xch):
    t = q.shape[0]
    nb = lp // QB
    cur, seq = _attn_specs(nb, lp)

    def body(q_ref, k_ref, v_ref, do_ref, o_ref, lse_ref, qg_ref, sk_ref, dq_ref, dk_ref, dv_ref, dqg_ref, dsk_ref):
        b, n = pl.program_id(0), pl.program_id(1)
        _init(dk_ref, n == 0)
        _init(dv_ref, n == 0)
        _init(dqg_ref, (b == 0) & (n == 0))
        _init(dsk_ref, (b == 0) & (n == 0))
        start = pl.multiple_of(jnp.maximum(n - 1, 0) * QB, QB)
        m_band, m_meta = _attn_masks(n, start)
        band = pl.ds(start, 2 * QB)
        lane = lax.broadcasted_iota(jnp.int32, (1, NH), 1)
        dqg = jnp.zeros((1, HD), F32)
        dsk = jnp.zeros((1, NH), F32)
        dk_b, dk_m, dv_b, dv_m = [], [], [], []
        for g in range(NKV):
            gs = slice(HD * g, HD * g + HD)
            kb, km, vb, vm = k_ref[band, gs], k_ref[0:NMETA, gs], v_ref[band, gs], v_ref[0:NMETA, gs]
            qh = _stack_heads(q_ref, g, lambda x: x)
            rs = _rstd(qh)
            qn = (qh * rs * qg_ref[...]).astype(BF)
            ls = _stack_cols(lse_ref, g)
            p_b = jnp.where(m_band, jnp.exp(_dot_nt(qn, kb) * SCALE - ls), 0.0)
            p_m = jnp.where(m_meta, jnp.exp(_dot_nt(qn, km) * SCALE - ls), 0.0)
            doh = _stack_heads(do_ref, g, lambda x: x)
            delta = jnp.sum(doh.astype(F32) * _stack_heads(o_ref, g, lambda x: x).astype(F32), axis=-1, keepdims=True)
            ds_b = (p_b * (_dot_nt(doh, vb) - delta)).astype(BF)
            ds_m = (p_m * (_dot_nt(doh, vm) - delta)).astype(BF)
            dqn = (_dot(ds_b, kb) + _dot(ds_m, km)) * SCALE
            dk_b.append(_dot_tn(qn, ds_b).T * SCALE)
            dk_m.append(_dot_tn(qn, ds_m).T * SCALE)
            dv_b.append(_dot_tn(doh, p_b).T)
            dv_m.append(_dot_tn(doh, p_m).T)
            dsink = jnp.exp(_stack_sinks(sk_ref, g) - ls) * delta
            z = dqn * qg_ref[...]
            dq = rs * z - qh * (rs * rs * rs * jnp.mean(z * qh, axis=-1, keepdims=True))
            dqg = dqg + jnp.sum(dqn * qh * rs, axis=0, keepdims=True)
            for r in range(RQ):
                h = g * RQ + r
                dq_ref[:, HD * h:HD * h + HD] = dq[r * QB:(r + 1) * QB]
                dsk = dsk + jnp.where(lane == h, -jnp.sum(dsink[r * QB:(r + 1) * QB]), 0.0)
        cat = lambda xs: jnp.concatenate(xs, axis=1)
        dk_ref[band, :] += cat(dk_b)
        dv_ref[band, :] += cat(dv_b)
        dk_ref[0:NMETA, :] += cat(dk_m)
        dv_ref[0:NMETA, :] += cat(dv_m)
        dqg_ref[...] += dqg
        dsk_ref[...] += dsk

    return _call(
        body, f"attn_bwd{j}", (bl, nb),
        [cur(D), seq, seq, cur(D), cur(D), cur(NH),
         pl.BlockSpec((None, 1, HD), lambda b, n: (j, 0, 0)), pl.BlockSpec((None, 1, NH), lambda b, n: (j, 0, 0))],
        [cur(D), seq, seq, pl.BlockSpec((1, HD), lambda b, n: (0, 0)), pl.BlockSpec((1, NH), lambda b, n: (0, 0))],
        [_sds((t, D), F32), _sds((bl, lp, KVD), F32), _sds((bl, lp, KVD), F32), _sds((1, HD), F32), _sds((1, NH), F32)],
        (q, k, v, do, o, lse, qg, sinks), xch=xch)


def kv_bwd_pre(dk0, dk1, dv0, dv1, kv, kng, tm):
    t = kv.shape[0]

    def body(dk0_ref, dk1_ref, dv0_ref, dv1_ref, kv_ref, g_ref, dkv_ref, dg_ref):
        _init(dg_ref, pl.program_id(0) == 0)
        dk = dk0_ref[...] + dk1_ref[...]
        dg = jnp.zeros((1, HD), F32)
        outs = []
        for s in range(NKV):
            sl = slice(HD * s, HD * s + HD)
            dx, dgs = _rms_bwd(kv_ref[:, sl], g_ref[...], dk[:, sl])
            outs.append(dx)
            dg = dg + dgs
        dkv_ref[:, :KVD] = jnp.concatenate(outs, axis=1).astype(BF)
        dkv_ref[:, KVD:] = (dv0_ref[...] + dv1_ref[...]).astype(BF)
        dg_ref[...] += dg

    return pl.pallas_call(
        body, name="kv_bwd_pre", grid=(t // tm,),
        in_specs=[_row(tm, KVD)] * 4 + [_row(tm, 2 * KVD), _res((1, HD))],
        out_specs=[_row(tm, 2 * KVD), _acc((1, HD))], out_shape=[_sds((t, 2 * KVD), BF), _sds((1, HD), F32)],
        compiler_params=_cp(1),
    )(dk0, dk1, dv0, dv1, kv, kng)


def conv_out_bwd(dh1, c, ln_g, ln_b, w_out, i, tm):
    t = dh1.shape[0]

    def body(dh1_ref, c_ref, g_ref, b_ref, w_ref, dc_ref, dg_ref, db_ref, dbo_ref):
        first = pl.program_id(0) == 0
        _init(dg_ref, first)
        _init(db_ref, first)
        _init(dbo_ref, first)
        dh1v = dh1_ref[...]
        ds = _dot_nt(dh1v, w_ref[...])
        cv = c_ref[...]
        xc = cv - jnp.mean(cv, axis=-1, keepdims=True)
        rstd = lax.rsqrt(jnp.mean(xc * xc, axis=-1, keepdims=True) + EPS)
        xh = xc * rstd
        n = xh * g_ref[...] + b_ref[...]
        sg = _sig(n)
        dn = ds * (sg * (1.0 + n * (1.0 - sg)))
        dxh = dn * g_ref[...]
        dc_ref[...] = rstd * (dxh - jnp.mean(dxh, axis=-1, keepdims=True) - xh * jnp.mean(dxh * xh, axis=-1, keepdims=True))
        dg_ref[...] += jnp.sum(dn * xh, axis=0, keepdims=True)
        db_ref[...] += jnp.sum(dn, axis=0, keepdims=True)
        dbo_ref[...] += jnp.sum(dh1v, axis=0, keepdims=True)

    return pl.pallas_call(
        body, name=f"conv_out_bwd{i}", grid=(t // tm,),
        in_specs=[_row(tm, D), _row(tm, D), _lay(i, (1, D)), _lay(i, (1, D)), _lay(i, (D, D))],
        out_specs=[_row(tm, D), _acc((1, D)), _acc((1, D)), _acc((1, D))],
        out_shape=[_sds((t, D), F32)] + [_sds((1, D), F32)] * 3,
        compiler_params=_cp(1),
    )(dh1, c, ln_g, ln_b, w_out)


def conv_mid_bwd(dc, a, big, dw, i, tm, tpb, xch):
    t = dc.shape[0]
    nsteps = t // tm

    def body(dc_ref, nxt_ref, a_ref, prv_ref, big_ref, dw_ref, da_ref, dbin_ref, ddw_ref, dce, ae, wacc, bacc):
        i_ = pl.program_id(0)
        _init(wacc, i_ == 0)
        _init(bacc, i_ == 0)
        dce[0:tm] = dc_ref[...]
        dce[tm:] = jnp.where(i_ % tpb == tpb - 1, 0.0, nxt_ref[...])
        ae[0:HALO] = jnp.where(i_ % tpb == 0, 0.0, prv_ref[...])
        ae[HALO:] = a_ref[...]

        def chunk(k, carry):
            r0 = pl.multiple_of(k * CHUNK, CHUNK)
            wdc = dce[pl.ds(r0, 2 * CHUNK), :]
            wa = ae[pl.ds(r0, 2 * CHUNK), :]
            dcc = wdc[0:CHUNK]
            da = jnp.zeros((CHUNK, D), F32)
            for j in range(CW):
                da = da + dw_ref[j:j + 1, :] * wdc[CW - 1 - j:CW - 1 - j + CHUNK]
                wacc[j] += _fold8(dcc * wa[j + 2:j + 2 + CHUNK])
            bv = big_ref[pl.ds(r0, CHUNK), :]
            a1, sg = bv[:, :D], _sig(bv[:, D:])
            d1 = da * sg
            d2 = da * a1 * sg * (1.0 - sg)
            da_ref[pl.ds(r0, CHUNK), 0:D] = d1.astype(BF)
            da_ref[pl.ds(r0, CHUNK), D:2 * D] = d2.astype(BF)
            bacc[:, 0:D] += _fold8(d1)
            bacc[:, D:2 * D] += _fold8(d2)
            return carry

        lax.fori_loop(0, tm // CHUNK, chunk, 0)

        @pl.when(i_ == nsteps - 1)
        def _():
            dbin_ref[...] = jnp.sum(bacc[...], axis=0, keepdims=True)
            ddw_ref[...] = jnp.sum(wacc[...], axis=1)

    return _call(
        body, f"conv_mid_bwd{i}", (nsteps,),
        [_row(tm, D), _next_halo(tm, t), _row(tm, D), _prev_halo(tm), _row(tm, 2 * D), _lay(i, (CW, D))],
        [_row(tm, 2 * D), _acc((1, 2 * D)), _acc((CW + 1, D))],
        [_sds((t, 2 * D), BF), _sds((1, 2 * D), F32), _sds((CW + 1, D), F32)],
        (dc, dc, a, a, big, dw),
        scratch=[pltpu.VMEM((tm + HALO, D), F32), pltpu.VMEM((tm + HALO, D), F32),
                 pltpu.VMEM((CW + 1, 8, D), F32), pltpu.VMEM((8, 2 * D), F32)], xch=xch)


def input_grads(dh0, seq):
    bl, lp, _ = dh0.shape
    cb = 256

    def body(dh_ref, gx_ref, gm_ref):
        _init(gm_ref, pl.program_id(1) == 0)
        gx_ref[...] = dh_ref[NMETA:NMETA + seq, :]
        gm_ref[...] += dh_ref[0:NMETA, :]

    return pl.pallas_call(
        body, name="input_grads", grid=(D // cb, bl),
        in_specs=[pl.BlockSpec((None, lp, cb), lambda c, b: (b, 0, c))],
        out_specs=[pl.BlockSpec((None, seq, cb), lambda c, b: (b, 0, c)), pl.BlockSpec((NMETA, cb), lambda c, b: (0, c))],
        out_shape=[_sds((bl, seq, D), F32), _sds((NMETA, D), F32)],
        compiler_params=_cp(2),
    )(dh0)


def local_step(x, tgt, meta8, w):
    bl, seq, _ = x.shape
    lp = -(-(NMETA + seq) // QB) * QB
    tpb = 4
    tm = lp // tpb
    t = bl * lp
    na, nbl = 2, 2
    flat = lambda a: a.reshape(t, D)

    h = flat(embed(x, meta8, lp))
    saved = []
    kvs = None
    for l in range(4):
        rec = {"h": h}
        if l < na:
            rec["u"], rec["big"], rec["a"] = conv_in_fwd(h, w["norm_mix"], l, w["conv_w_in"], w["conv_b_in"], l, tm)
            rec["c"], rec["s"] = conv_mid_fwd(rec["a"], w["conv_dw"], w["conv_ln_g"], w["conv_ln_b"], l, tm, tpb)
            rec["h1"], rec["u2"] = mixer_out_fwd(h, rec["s"], w["conv_w_out"], l, w["conv_b_out"], w["norm_ffn"], l, tm,
                                                 f"conv_out_fwd{l}")
        else:
            j = l - na
            if kvs is None:
                kvs = dict(zip(("kn", "kv", "k", "v"), kv_fwd(h, w["kv_norm"], w["w_kv"], w["k_norm"], tm)))
                kvs["h"] = h
                kvs["k3"], kvs["v3"] = kvs["k"].reshape(bl, lp, KVD), kvs["v"].reshape(bl, lp, KVD)
            rec["u"], rec["q"] = q_fwd(h, w["norm_mix"], l, w["w_q"], j, tm)
            rec["o"], rec["lse"] = attn_fwd(rec["q"], kvs["k3"], kvs["v3"], w["q_norm"], w["attn_sinks"], j, bl, lp)
            rec["h1"], rec["u2"] = mixer_out_fwd(h, rec["o"], w["w_o"], j, None, w["norm_ffn"], l, tm, f"attn_out_fwd{j}")
        rec["g"], rec["up"], rec["hid"] = ffn_up_fwd(rec["u2"], w["ffn_w_gate"], w["ffn_w_up"], l, tm // 2)
        h = ffn_down_fwd(rec["hid"], rec["h1"], w["ffn_w_down"], l, tm)
        saved.append(rec)

    dh3, loss_blk = loss_fwd(h.reshape(bl, lp, D), tgt)
    dh = flat(dh3)

    big, small, arrived = {}, {}, {}
    dks, dvs = [], []
    pending = []

    def carried(names, arrivals):
        arrived.update(zip([nm for nm, _ in names], arrivals))

    for l in reversed(range(4)):
        rec = saved[l]
        riders, pending = pending, []
        (dg, du, dh1, small[f"norm_ffn{l}"]), got = ffn_bwd_x(
            dh, rec["g"], rec["up"], rec["h1"], w["norm_ffn"], l, w["ffn_w_down"], w["ffn_w_gate"], w["ffn_w_up"], tm // 2,
            [(big[nm], kind) for nm, kind in riders])
        carried(riders, got)
        big[f"ffn_w_down{l}"] = mm_tn(rec["hid"], dh, tm, f"dw_down{l}")
        big[f"ffn_w_gate{l}"] = mm_tn(rec["u2"], dg, tm, f"dw_gate{l}", split=True)
        big[f"ffn_w_up{l}"] = mm_tn(rec["u2"], du, tm, f"dw_up{l}", split=True)
        riders = [(f"ffn_w_down{l}", "rows"), (f"ffn_w_gate{l}", "pieces"), (f"ffn_w_up{l}", "pieces")]
        xch = [(big[nm], kind) for nm, kind in riders]
        if l >= na:
            j = l - na
            do = out_proj_bwd(dh1, w["w_o"], j, tm, f"attn_out_bwd{j}")
            big[f"w_o{j}"] = mm_tn(rec["o"], dh1, tm, f"dw_o{j}")
            (dq, dk, dv, small[f"q_norm{j}"], small[f"attn_sinks{j}"]), got = attn_bwd(
                rec["q"], kvs["k3"], kvs["v3"], do, rec["o"], rec["lse"], w["q_norm"], w["attn_sinks"], j, bl, lp, xch)
            carried(riders, got)
            dks.append(dk.reshape(t, KVD))
            dvs.append(dv.reshape(t, KVD))
            big[f"w_q{j}"] = mm_tn(rec["u"], dq, tm, f"dw_q{j}")
            dh, small[f"norm_mix{l}"] = proj_bwd(dq, w["w_q"], j, (D, D), rec["h"], w["norm_mix"], l, dh1, tm, f"q_bwd{j}")
            pending = [(f"w_o{j}", "rows"), (f"w_q{j}", "rows")]
            if l == na:
                dkv, small["k_norm"] = kv_bwd_pre(dks[0], dks[1], dvs[0], dvs[1], kvs["kv"], w["k_norm"], tm)
                big["w_kv"] = mm_tn(kvs["kn"], dkv, tm, "dw_kv")
                dh, small["kv_norm"] = proj_bwd(dkv, w["w_kv"], None, (D, 2 * KVD), kvs["h"], w["kv_norm"], None, dh, tm, "kv_bwd")
                pending.append(("w_kv", "rows"))
        else:
            dc, small[f"conv_ln_g{l}"], small[f"conv_ln_b{l}"], small[f"conv_b_out{l}"] = conv_out_bwd(
                dh1, rec["c"], w["conv_ln_g"], w["conv_ln_b"], w["conv_w_out"], l, tm)
            big[f"conv_w_out{l}"] = mm_tn(rec["s"], dh1, tm, f"dw_conv_out{l}")
            (da, small[f"conv_b_in{l}"], ddw), got = conv_mid_bwd(dc, rec["a"], rec["big"], w["conv_dw"], l, tm, tpb, xch)
            carried(riders, got)
            small[f"conv_dw{l}"] = ddw[:CW]
            big[f"conv_w_in{l}"] = mm_tn(rec["u"], da, tm, f"dw_conv_in{l}", split=True)
            dh, small[f"norm_mix{l}"] = proj_bwd(da, w["conv_w_in"], l, (D, 2 * D), rec["h"], w["norm_mix"], l, dh1, tm,
                                                 f"conv_in_bwd{l}")
            pending = [(f"conv_w_out{l}", "rows"), (f"conv_w_in{l}", "pieces")]
    carried(pending, exchange_grads([(big[nm], kind) for nm, kind in pending]))
    grad_x, small["meta_tokens"] = input_grads(dh.reshape(bl, lp, D), seq)
    return loss_blk, grad_x, big, arrived, small


def all_gather_weights(shards, kinds):
    n = len(shards)

    def out_shape(s, kind):
        if kind == "rows":
            return _sds(s.shape[:-2] + (NDEV * s.shape[-2], s.shape[-1]), s.dtype)
        return _sds((NDEV,) + s.shape, s.dtype)

    def body(*refs):
        srcs, outs = refs[:n], refs[n:2 * n]
        send_sems, recv_sems, local_sems = refs[2 * n:]
        x, y, c = lax.axis_index("x"), lax.axis_index("y"), lax.axis_index("c")
        me, sibling = (x, y, c), (x, y, 1 - c)
        chips = [(1 - x, y), (x, 1 - y), (1 - x, 1 - y)]

        def slot(a, owner):
            idx = 4 * owner[0] + 2 * owner[1] + owner[2]
            if kinds[a] == "rows":
                r = srcs[a].shape[-2]
                if len(srcs[a].shape) == 3:
                    return outs[a].at[:, pl.ds(idx * r, r), :]
                return outs[a].at[pl.ds(idx * r, r), :]
            return outs[a].at[idx]

        def copy(a, k, block, to, src=None):
            return pltpu.make_async_remote_copy(
                src_ref=slot(a, block) if src is None else src, dst_ref=slot(a, block),
                send_sem=send_sems.at[a, k], recv_sem=recv_sems.at[a, k], device_id=to, device_id_type=MESH)

        mine = [pltpu.make_async_copy(srcs[a], slot(a, me), local_sems.at[a]) for a in range(n)]
        for cp in mine:
            cp.start()
        first = []
        for a in range(n):
            first.append(copy(a, 0, me, sibling, src=srcs[a]))
            first += [copy(a, 1 + j, me, (*chip, c), src=srcs[a]) for j, chip in enumerate(chips)]
        for cp in first:
            cp.start()
        passed = []
        for a in range(n):
            for j, chip in enumerate(chips):
                copy(a, 1 + j, (*chip, c), me).wait_recv()
                cp = copy(a, 4 + j, (*chip, c), sibling)
                cp.start()
                passed.append(cp)
        for a in range(n):
            copy(a, 0, sibling, me).wait_recv()
            for j, chip in enumerate(chips):
                copy(a, 4 + j, (*chip, 1 - c), me).wait_recv()
        for cp in first + passed:
            cp.wait_send()
        for cp in mine:
            cp.wait()

    any_spec = pl.BlockSpec(memory_space=pl.ANY)
    return pl.pallas_call(
        body, name="all_gather_weights", out_shape=[out_shape(s, k) for s, k in zip(shards, kinds)],
        in_specs=[any_spec] * n, out_specs=[any_spec] * n,
        scratch_shapes=[pltpu.SemaphoreType.DMA((n, 7)), pltpu.SemaphoreType.DMA((n, 7)), pltpu.SemaphoreType.DMA((n,))],
    )(*shards)


def all_reduce_small(buf):
    rows = buf.shape[0]

    def body(x_ref, o_ref, g_ref, send_sems, recv_sems):
        me = _my_index()
        sends = []
        for k in range(1, NDEV):
            peer = me ^ k
            cp = pltpu.make_async_remote_copy(
                src_ref=x_ref, dst_ref=g_ref.at[me], send_sem=send_sems.at[k - 1], recv_sem=recv_sems.at[k - 1],
                device_id=_coords(peer), device_id_type=MESH)
            cp.start()
            sends.append(cp)
        g_ref[me] = x_ref[...]
        for k in range(1, NDEV):
            peer = me ^ k
            pltpu.make_async_remote_copy(
                src_ref=x_ref, dst_ref=g_ref.at[peer], send_sem=send_sems.at[k - 1], recv_sem=recv_sems.at[k - 1],
                device_id=_coords(peer), device_id_type=MESH).wait_recv()
        for cp in sends:
            cp.wait_send()
        acc = g_ref[0]
        for p in range(1, NDEV):
            acc = acc + g_ref[p]
        o_ref[...] = acc

    return pl.pallas_call(
        body, name="all_reduce_small", out_shape=_sds((rows, D), F32),
        in_specs=[pl.BlockSpec(memory_space=pltpu.VMEM)], out_specs=pl.BlockSpec(memory_space=pltpu.VMEM),
        scratch_shapes=[pltpu.VMEM((NDEV, rows, D), F32), pltpu.SemaphoreType.DMA((7,)), pltpu.SemaphoreType.DMA((7,))],
    )(buf)


def cast_bf16(ws):
    n = len(ws)

    def body(*refs):
        for a in range(n):
            refs[n + a][...] = refs[a][...].astype(BF)

    return pl.pallas_call(
        body, name="cast_bf16", out_shape=[_sds(x.shape, BF) for x in ws],
        compiler_params=pltpu.CompilerParams(vmem_limit_bytes=VMEM_LIMIT),
    )(*ws)


def join_columns(w8, name):
    _, lyr, kk, n8 = w8.shape

    def body(x_ref, o_ref):
        o_ref[...] = jnp.concatenate([x_ref[p] for p in range(NDEV)], axis=1)

    return pl.pallas_call(
        body, name=name, grid=(lyr,),
        in_specs=[pl.BlockSpec((NDEV, None, kk, n8), lambda l: (0, l, 0, 0))],
        out_specs=pl.BlockSpec((None, kk, NDEV * n8), lambda l: (l, 0, 0)),
        out_shape=_sds((lyr, kk, NDEV * n8), w8.dtype), compiler_params=_cp(1),
    )(w8)


def _adamw_math(w, m, v, g):
    m2 = B1 * m + (1.0 - B1) * g
    v2 = B2 * v + (1.0 - B2) * (g * g)
    mh = m2 / (1.0 - B1 ** STEP)
    vh = v2 / (1.0 - B2 ** STEP)
    return -LR * (mh / (jnp.sqrt(vh) + AEPS) + WD * w), m2, v2


def adamw_big(w, m, v, parts, name):
    lyr, r, c = w.shape
    by_cols = c >= 512
    blk = (lyr, r, 256) if by_cols else (lyr, 256 if r % 256 == 0 else r, c)
    imap = (lambda i: (0, 0, i)) if by_cols else (lambda i: (0, i, 0))

    def body(w_ref, m_ref, v_ref, *rest):
        p_refs, (g_ref, d_ref, m2_ref, v2_ref) = rest[:lyr], rest[lyr:]
        for l in range(lyr):
            g = p_refs[l][0].astype(F32)
            for q in range(1, NDEV):
                g = g + p_refs[l][q].astype(F32)
            g_ref[l] = g
            d_ref[l], m2_ref[l], v2_ref[l] = _adamw_math(w_ref[l], m_ref[l], v_ref[l], g)

    spec = pl.BlockSpec(blk, imap)
    pspec = pl.BlockSpec((NDEV,) + blk[1:], imap)
    return pl.pallas_call(
        body, name=name, grid=((c // 256) if by_cols else (r // blk[1]),),
        in_specs=[spec, spec, spec] + [pspec] * lyr,
        out_specs=[spec] * 4, out_shape=[_sds((lyr, r, c), F32)] * 4, compiler_params=_cp(1),
    )(w, m, v, *parts)


def adamw_small(w, m, v, g, name):
    def body(w_ref, m_ref, v_ref, g_ref, d_ref, m2_ref, v2_ref):
        d_ref[...], m2_ref[...], v2_ref[...] = _adamw_math(w_ref[...], m_ref[...], v_ref[...], g_ref[...])

    return pl.pallas_call(body, name=name, out_shape=[_sds(w.shape, F32)] * 3)(w, m, v, g)


NAMES = ["meta_tokens", "norm_mix", "norm_ffn", "conv_w_in", "conv_b_in", "conv_dw", "conv_ln_g", "conv_ln_b", "conv_w_out",
         "conv_b_out", "kv_norm", "w_kv", "k_norm", "w_q", "q_norm", "attn_sinks", "w_o", "ffn_w_gate", "ffn_w_up", "ffn_w_down"]
BIG = {"conv_w_in": "pieces", "conv_w_out": "rows", "w_kv": "rows", "w_q": "rows", "w_o": "rows",
       "ffn_w_gate": "pieces", "ffn_w_up": "pieces", "ffn_w_down": "rows"}
REP_ROWS = 16


def _pad_cols(a, width):
    return jnp.pad(a, ((0, 0), (0, width - a.shape[1])))


def _pack_rep(p):
    rows = [p["norm_mix"], p["norm_ffn"], p["kv_norm"].reshape(1, D), _pad_cols(p["k_norm"].reshape(1, HD), D),
            _pad_cols(p["q_norm"], D), _pad_cols(p["attn_sinks"], D), jnp.zeros((2, D), F32)]
    return jnp.concatenate(rows, axis=0)


def _unpack_rep(a):
    return {"norm_mix": a[0:4], "norm_ffn": a[4:8], "kv_norm": a[8], "k_norm": a[9, :HD], "q_norm": a[10:12, :HD],
            "attn_sinks": a[12:14, :NH]}


SH_NAMES = ["meta_tokens", "conv_b_in", "conv_dw", "conv_ln_g", "conv_ln_b", "conv_b_out"]


def _pack_sh(p):
    c = D // NDEV
    rows = [p["meta_tokens"], p["conv_b_in"].reshape(4, c), p["conv_dw"].reshape(2 * CW, c), p["conv_ln_g"], p["conv_ln_b"],
            p["conv_b_out"]]
    return jnp.concatenate(rows, axis=0)


def _unpack_sh(a):
    c = D // NDEV
    return {"meta_tokens": a[0:16], "conv_b_in": a[16:20].reshape(2, 2 * c), "conv_dw": a[20:82].reshape(2, CW, c),
            "conv_ln_g": a[82:84], "conv_ln_b": a[84:86], "conv_b_out": a[86:88]}


def kernel(x, meta_tokens, norm_mix, norm_ffn, conv_w_in, conv_b_in, conv_dw, conv_ln_g, conv_ln_b, conv_w_out, conv_b_out, kv_norm, w_kv, k_norm, w_q, q_norm, attn_sinks, w_o, ffn_w_gate, ffn_w_up, ffn_w_down, loss_target, m_meta_tokens, m_norm_mix, m_norm_ffn, m_conv_w_in, m_conv_b_in, m_conv_dw, m_conv_ln_g, m_conv_ln_b, m_conv_w_out, m_conv_b_out, m_kv_norm, m_w_kv, m_k_norm, m_w_q, m_q_norm, m_attn_sinks, m_w_o, m_ffn_w_gate, m_ffn_w_up, m_ffn_w_down, v_meta_tokens, v_norm_mix, v_norm_ffn, v_conv_w_in, v_conv_b_in, v_conv_dw, v_conv_ln_g, v_conv_ln_b, v_conv_w_out, v_conv_b_out, v_kv_norm, v_w_kv, v_k_norm, v_w_q, v_q_norm, v_attn_sinks, v_w_o, v_ffn_w_gate, v_ffn_w_up, v_ffn_w_down):
    wts = dict(zip(NAMES, (meta_tokens, norm_mix, norm_ffn, conv_w_in, conv_b_in, conv_dw, conv_ln_g, conv_ln_b, conv_w_out,
                           conv_b_out, kv_norm, w_kv, k_norm, w_q, q_norm, attn_sinks, w_o, ffn_w_gate, ffn_w_up, ffn_w_down)))
    mom = dict(zip(NAMES, (m_meta_tokens, m_norm_mix, m_norm_ffn, m_conv_w_in, m_conv_b_in, m_conv_dw, m_conv_ln_g, m_conv_ln_b,
                           m_conv_w_out, m_conv_b_out, m_kv_norm, m_w_kv, m_k_norm, m_w_q, m_q_norm, m_attn_sinks, m_w_o,
                           m_ffn_w_gate, m_ffn_w_up, m_ffn_w_down)))
    var = dict(zip(NAMES, (v_meta_tokens, v_norm_mix, v_norm_ffn, v_conv_w_in, v_conv_b_in, v_conv_dw, v_conv_ln_g, v_conv_ln_b,
                           v_conv_w_out, v_conv_b_out, v_kv_norm, v_w_kv, v_k_norm, v_w_q, v_q_norm, v_attn_sinks, v_w_o,
                           v_ffn_w_gate, v_ffn_w_up, v_ffn_w_down)))
    me = _my_index()
    c8 = D // NDEV

    big_names = list(BIG)
    shards = cast_bf16([wts[k] for k in big_names])
    vec_names = ["meta_tokens", "conv_b_in", "conv_dw", "conv_ln_g", "conv_ln_b", "conv_b_out"]
    gathered = all_gather_weights(list(shards) + [wts[k] for k in vec_names],
                                  [BIG[k] for k in big_names] + ["flat"] * len(vec_names))
    full = dict(zip(big_names + vec_names, gathered))
    w = {}
    for k in big_names:
        w[k] = join_columns(full[k] if full[k].ndim == 4 else full[k][:, None], "join_" + k) if BIG[k] == "pieces" else full[k]
    join_vec = lambda a: jnp.moveaxis(a, 0, -2).reshape(a.shape[1:-1] + (NDEV * a.shape[-1],))
    w["conv_b_in"] = join_vec(full["conv_b_in"]).reshape(2, 1, 2 * D)
    w["conv_dw"] = join_vec(full["conv_dw"])
    for k in ("conv_ln_g", "conv_ln_b", "conv_b_out"):
        w[k] = join_vec(full[k]).reshape(2, 1, D)
    w["norm_mix"] = norm_mix.reshape(4, 1, D)
    w["norm_ffn"] = norm_ffn.reshape(4, 1, D)
    w["kv_norm"] = kv_norm.reshape(1, D)
    w["k_norm"] = k_norm.reshape(1, HD)
    w["q_norm"] = q_norm.reshape(2, 1, HD)
    w["attn_sinks"] = attn_sinks.reshape(2, 1, NH)

    loss_blk, grad_x, _, arrived, gs = local_step(x, loss_target, full["meta_tokens"], w)

    stack = lambda k, n: jnp.concatenate([gs[f"{k}{i}"] for i in range(n)], axis=0)
    rep = {"norm_mix": stack("norm_mix", 4), "norm_ffn": stack("norm_ffn", 4), "kv_norm": gs["kv_norm"], "k_norm": gs["k_norm"],
           "q_norm": stack("q_norm", 2), "attn_sinks": stack("attn_sinks", 2)}
    loss_row = _pad_cols(loss_blk[0:1, 0:1], D)
    packed = jnp.concatenate(
        [_pack_rep(rep)[:14], loss_row, jnp.zeros((1, D), F32), gs["meta_tokens"], stack("conv_b_in", 2).reshape(4, D),
         stack("conv_dw", 2), stack("conv_ln_g", 2), stack("conv_ln_b", 2), stack("conv_b_out", 2)], axis=0)
    red = all_reduce_small(packed)
    loss = red[14, 0]
    cols = lambda a, width: lax.dynamic_slice_in_dim(a, me * width, width, axis=1)
    g_sh = jnp.concatenate(
        [cols(red[16:32], c8), cols(red[32:36].reshape(2, 2 * D), 2 * c8).reshape(4, c8), cols(red[36:98], c8),
         cols(red[98:100], c8), cols(red[100:102], c8), cols(red[102:104], c8)], axis=0)
    g_rep = red[0:REP_ROWS].at[14:].set(0.0)

    grads, delta, new_m, new_v = {}, {}, {}, {}
    for k in big_names:
        flat2 = wts[k].ndim == 2
        parts = [arrived[k]] if flat2 else [arrived[f"{k}{i}"] for i in range(wts[k].shape[0])]
        as3 = (lambda a: a[None]) if flat2 else (lambda a: a)
        outs = adamw_big(as3(wts[k]), as3(mom[k]), as3(var[k]), parts, "adamw_" + k)
        grads[k], delta[k], new_m[k], new_v[k] = [o[0] if flat2 else o for o in outs]
    d_rep, m_rep, v_rep = adamw_small(_pack_rep(wts), _pack_rep(mom), _pack_rep(var), g_rep, "adamw_rep")
    d_sh, m_sh, v_sh = adamw_small(_pack_sh(wts), _pack_sh(mom), _pack_sh(var), g_sh, "adamw_sh")
    for dst, a_rep, a_sh in ((grads, g_rep, g_sh), (delta, d_rep, d_sh), (new_m, m_rep, m_sh), (new_v, v_rep, v_sh)):
        dst.update(_unpack_rep(a_rep))
        dst.update(_unpack_sh(a_sh))
    return (loss, grad_x, *[grads[k] for k in NAMES], *[delta[k] for k in NAMES], *[new_m[k] for k in NAMES],
            *[new_v[k] for k in NAMES])
```

```python
import functools

import jax
import jax.numpy as jnp
from jax import lax
from jax.experimental import pallas as pl
from jax.experimental.pallas import tpu as pltpu

F32 = jnp.float32
BF = jnp.bfloat16

D = 1024
DFF = 2816
NH = 16
NKV = 4
HD = 64
KVD = NKV * HD
NMETA = 16
CW = 31
HALO = 32
CHUNK = 32
QB = 128
EPS = 1e-6
NEG = -1e30
NDEV = 8
SCALE = HD ** -0.5

LR, B1, B2, AEPS, WD, STEP = 0.001, 0.9, 0.999, 1e-08, 0.01, 10

VMEM_LIMIT = 56 * 2 ** 20
MESH = pl.DeviceIdType.MESH


def _cp(n):
    return pltpu.CompilerParams(dimension_semantics=("arbitrary",) * n, vmem_limit_bytes=VMEM_LIMIT)


def _row(tm, c):
    return pl.BlockSpec((tm, c), lambda i: (i, 0))


def _res(shape):
    return pl.BlockSpec(shape, lambda i: (0,) * len(shape), pipeline_mode=pl.Buffered(1))


def _lay(l, shape):
    return pl.BlockSpec((None,) + tuple(shape), lambda i: (l,) + (0,) * len(shape), pipeline_mode=pl.Buffered(1))


def _acc(shape):
    return pl.BlockSpec(shape, lambda i: (0,) * len(shape))


def _sds(shape, dt):
    return jax.ShapeDtypeStruct(tuple(shape), dt)


def _dot(a, b):
    return jnp.dot(a.astype(BF), b.astype(BF), preferred_element_type=F32)


def _dot_nt(a, b):
    return lax.dot_general(a.astype(BF), b.astype(BF), (((1,), (1,)), ((), ())), preferred_element_type=F32)


def _dot_tn(a, b):
    return lax.dot_general(a.astype(BF), b.astype(BF), (((0,), (0,)), ((), ())), preferred_element_type=F32)


def _rstd(x):
    return lax.rsqrt(jnp.mean(x * x, axis=-1, keepdims=True) + EPS)


def _rms_bwd(x, g, dy):
    r = _rstd(x)
    z = dy * g
    dx = r * z - x * (r * r * r * jnp.mean(z * x, axis=-1, keepdims=True))
    return dx, jnp.sum(dy * x * r, axis=0, keepdims=True)


def _sig(x):
    return jax.nn.sigmoid(x)


def _fold8(x):
    out = x[0:8]
    for k in range(1, x.shape[0] // 8):
        out = out + x[8 * k:8 * k + 8]
    return out


def _shifted(win):
    return [win] + [pltpu.roll(win, 2 * CHUNK - rho, 0) for rho in range(1, 8)]


def _tap(phases, o):
    return phases[o % 8][8 * (o // 8):8 * (o // 8) + CHUNK]


def _init(ref, first):
    @pl.when(first)
    def _():
        ref[...] = jnp.zeros_like(ref)


def _my_index():
    return 4 * lax.axis_index("x") + 2 * lax.axis_index("y") + lax.axis_index("c")


def _coords(idx):
    return (idx // 4, (idx // 2) % 2, idx % 2)


def _xch_shapes(xch):
    return [_sds((NDEV,) + ((a.shape[0] // NDEV, a.shape[1]) if k == "rows" else tuple(a.shape[1:])), a.dtype) for a, k in xch]


def _xch_scratch(n):
    return [pltpu.SemaphoreType.DMA((n, NDEV - 1)), pltpu.SemaphoreType.DMA((n, NDEV - 1)), pltpu.SemaphoreType.DMA((n,))]


def _xch_copies(kinds, srcs, outs, sems, arrivals):
    send_sems, recv_sems, local_sems = sems
    me = _my_index()

    def piece(a, p):
        if kinds[a] == "rows":
            r = srcs[a].shape[0] // NDEV
            return srcs[a].at[pl.ds(p * r, r), :]
        return srcs[a].at[p]

    def remote(a, k, src, slot):
        return pltpu.make_async_remote_copy(
            src_ref=src, dst_ref=outs[a].at[slot], send_sem=send_sems.at[a, k - 1], recv_sem=recv_sems.at[a, k - 1],
            device_id=_coords(me ^ k), device_id_type=MESH)

    n = len(kinds)
    local = [pltpu.make_async_copy(piece(a, me), outs[a].at[me], local_sems.at[a]) for a in range(n)]
    sends = [remote(a, k, piece(a, me ^ k), me) for a in range(n) for k in range(1, NDEV)]
    recvs = [remote(a, k, piece(a, me), me ^ k) for a in range(n) for k in range(1, NDEV)] if arrivals else []
    return local, sends, recvs


def _xch_start(kinds, srcs, outs, sems):
    local, sends, _ = _xch_copies(kinds, srcs, outs, sems, False)
    for cp in local + sends:
        cp.start()


def _xch_wait(kinds, srcs, outs, sems):
    local, sends, recvs = _xch_copies(kinds, srcs, outs, sems, True)
    for cp in recvs:
        cp.wait_recv()
    for cp in sends:
        cp.wait_send()
    for cp in local:
        cp.wait()


def _call(body, name, grid, in_specs, out_specs, out_shape, args, scratch=(), xch=()):
    n_in, n_out, n_x, n_s = len(in_specs), len(out_specs), len(xch), len(scratch)
    kinds = [k for _, k in xch]

    def wrapped(*refs):
        ins, srcs = refs[:n_in], refs[n_in:n_in + n_x]
        outs, arrivals = refs[n_in + n_x:n_in + n_x + n_out], refs[n_in + n_x + n_out:n_in + 2 * n_x + n_out]
        rest = refs[n_in + 2 * n_x + n_out:]
        if n_x:
            ids = [pl.program_id(d) for d in range(len(grid))]
            first, last = ids[0] == 0, ids[0] == grid[0] - 1
            for d in range(1, len(grid)):
                first, last = first & (ids[d] == 0), last & (ids[d] == grid[d] - 1)

            @pl.when(first)
            def _():
                _xch_start(kinds, srcs, arrivals, rest[n_s:])

        body(*ins, *outs, *rest[:n_s])
        if n_x:
            @pl.when(last)
            def _():
                _xch_wait(kinds, srcs, arrivals, rest[n_s:])

    any_spec = pl.BlockSpec(memory_space=pl.ANY)
    res = pl.pallas_call(
        wrapped, name=name, grid=grid, in_specs=list(in_specs) + [any_spec] * n_x, out_specs=list(out_specs) + [any_spec] * n_x,
        out_shape=list(out_shape) + _xch_shapes(xch), scratch_shapes=list(scratch) + (_xch_scratch(n_x) if n_x else []),
        compiler_params=_cp(len(grid)),
    )(*args, *[a for a, _ in xch])
    return res[:n_out], res[n_out:]


def exchange_grads(xch):
    kinds = [k for _, k in xch]
    n = len(xch)

    def body(*refs):
        _xch_start(kinds, refs[:n], refs[n:2 * n], refs[2 * n:])
        _xch_wait(kinds, refs[:n], refs[n:2 * n], refs[2 * n:])

    any_spec = pl.BlockSpec(memory_space=pl.ANY)
    return pl.pallas_call(
        body, name="exchange_grads", out_shape=_xch_shapes(xch), in_specs=[any_spec] * n, out_specs=[any_spec] * n,
        scratch_shapes=_xch_scratch(n),
    )(*[a for a, _ in xch])


def embed(x, meta8, lp):
    bl, seq, _ = x.shape
    c8 = D // NDEV
    cb = 2 * c8

    def body(x_ref, m_ref, h_ref):
        h_ref[0:NMETA, :] = jnp.concatenate([m_ref[0], m_ref[1]], axis=1)
        h_ref[NMETA:NMETA + seq, :] = x_ref[...]
        h_ref[NMETA + seq:, :] = jnp.zeros((lp - NMETA - seq, cb), F32)

    return pl.pallas_call(
        body, name="embed", grid=(bl, D // cb),
        in_specs=[pl.BlockSpec((None, seq, cb), lambda b, c: (b, 0, c)), pl.BlockSpec((2, NMETA, c8), lambda b, c: (c, 0, 0))],
        out_specs=pl.BlockSpec((None, lp, cb), lambda b, c: (b, 0, c)), out_shape=_sds((bl, lp, D), F32),
        compiler_params=_cp(2),
    )(x, meta8)


def conv_in_fwd(h, nm, l, w_in, b_in, i, tm):
    t = h.shape[0]

    def body(h_ref, g_ref, w_ref, b_ref, u_ref, big_ref, a_ref):
        x = h_ref[...]
        ub = (x * _rstd(x) * g_ref[...]).astype(BF)
        u_ref[...] = ub
        big = jnp.dot(ub, w_ref[...], preferred_element_type=F32) + b_ref[...]
        big_ref[...] = big
        a_ref[...] = big[:, :D] * _sig(big[:, D:])

    return pl.pallas_call(
        body, name=f"conv_in_fwd{i}", grid=(t // tm,),
        in_specs=[_row(tm, D), _lay(l, (1, D)), _lay(i, (D, 2 * D)), _lay(i, (1, 2 * D))],
        out_specs=[_row(tm, D), _row(tm, 2 * D), _row(tm, D)],
        out_shape=[_sds((t, D), BF), _sds((t, 2 * D), F32), _sds((t, D), F32)],
        compiler_params=_cp(1),
    )(h, nm, w_in, b_in)


def _prev_halo(tm):
    return pl.BlockSpec((HALO, D), lambda i: (jnp.maximum(i * (tm // HALO) - 1, 0), 0))


def _next_halo(tm, t):
    return pl.BlockSpec((HALO, D), lambda i: (jnp.minimum((i + 1) * (tm // HALO), t // HALO - 1), 0))


def conv_mid_fwd(a, dw, ln_g, ln_b, i, tm, tpb):
    t = a.shape[0]

    def body(a_ref, halo_ref, dw_ref, g_ref, b_ref, c_ref, s_ref, ext):
        first = pl.program_id(0) % tpb == 0
        ext[0:HALO] = jnp.where(first, 0.0, halo_ref[...])
        ext[HALO:] = a_ref[...]

        def chunk(k, carry):
            r0 = pl.multiple_of(k * CHUNK, CHUNK)
            win = _shifted(ext[pl.ds(r0, 2 * CHUNK), :])
            c = jnp.zeros((CHUNK, D), F32)
            for j in range(CW):
                c = c + dw_ref[j:j + 1, :] * _tap(win, j + 2)
            c_ref[pl.ds(r0, CHUNK), :] = c
            mu = jnp.mean(c, axis=-1, keepdims=True)
            xc = c - mu
            n = xc * lax.rsqrt(jnp.mean(xc * xc, axis=-1, keepdims=True) + EPS) * g_ref[...] + b_ref[...]
            s_ref[pl.ds(r0, CHUNK), :] = (n * _sig(n)).astype(BF)
            return carry

        lax.fori_loop(0, tm // CHUNK, chunk, 0)

    return pl.pallas_call(
        body, name=f"conv_mid_fwd{i}", grid=(t // tm,),
        in_specs=[_row(tm, D), _prev_halo(tm), _lay(i, (CW, D)), _lay(i, (1, D)), _lay(i, (1, D))],
        out_specs=[_row(tm, D), _row(tm, D)],
        out_shape=[_sds((t, D), F32), _sds((t, D), BF)],
        scratch_shapes=[pltpu.VMEM((tm + HALO, D), F32)],
        compiler_params=_cp(1),
    )(a, a, dw, ln_g, ln_b)


def mixer_out_fwd(h, s, w, lw, bias, nf, l, tm, name):
    t = h.shape[0]

    def body(*refs):
        if bias is None:
            h_ref, s_ref, w_ref, g_ref, h1_ref, u_ref = refs
            y = 0.0
        else:
            h_ref, s_ref, w_ref, b_ref, g_ref, h1_ref, u_ref = refs
            y = b_ref[...]
        h1 = h_ref[...] + (jnp.dot(s_ref[...], w_ref[...], preferred_element_type=F32) + y)
        h1_ref[...] = h1
        u_ref[...] = (h1 * _rstd(h1) * g_ref[...]).astype(BF)

    ins = [h, s, w] + ([] if bias is None else [bias]) + [nf]
    specs = [_row(tm, D), _row(tm, D), _lay(lw, (D, D))] + ([] if bias is None else [_lay(lw, (1, D))]) + [_lay(l, (1, D))]
    return pl.pallas_call(
        body, name=name, grid=(t // tm,), in_specs=specs,
        out_specs=[_row(tm, D), _row(tm, D)], out_shape=[_sds((t, D), F32), _sds((t, D), BF)],
        compiler_params=_cp(1),
    )(*ins)


def ffn_up_fwd(u, wg, wu, l, tm):
    t = u.shape[0]

    def body(u_ref, wg_ref, wu_ref, g_ref, up_ref, hid_ref):
        ub = u_ref[...]
        g = jnp.dot(ub, wg_ref[...], preferred_element_type=F32)
        up = jnp.dot(ub, wu_ref[...], preferred_element_type=F32)
        g_ref[...] = g.astype(BF)
        up_ref[...] = up.astype(BF)
        hid_ref[...] = (g * _sig(g) * up).astype(BF)

    return pl.pallas_call(
        body, name=f"ffn_up_fwd{l}", grid=(t // tm,),
        in_specs=[_row(tm, D), _lay(l, (D, DFF)), _lay(l, (D, DFF))],
        out_specs=[_row(tm, DFF)] * 3, out_shape=[_sds((t, DFF), BF)] * 3,
        compiler_params=_cp(1),
    )(u, wg, wu)


def ffn_down_fwd(hid, h1, wd, l, tm):
    t = h1.shape[0]

    def body(hid_ref, h1_ref, w_ref, h2_ref):
        h2_ref[...] = h1_ref[...] + jnp.dot(hid_ref[...], w_ref[...], preferred_element_type=F32)

    return pl.pallas_call(
        body, name=f"ffn_down_fwd{l}", grid=(t // tm,),
        in_specs=[_row(tm, DFF), _row(tm, D), _lay(l, (DFF, D))],
        out_specs=_row(tm, D), out_shape=_sds((t, D), F32),
        compiler_params=_cp(1),
    )(hid, h1, wd)


def _seg_rms(x, g, nseg):
    outs = []
    for s in range(nseg):
        xs = x[:, HD * s:HD * s + HD]
        outs.append(xs * _rstd(xs) * g)
    return jnp.concatenate(outs, axis=1)


def kv_fwd(h, kvn, w_kv, kng, tm):
    t = h.shape[0]

    def body(h_ref, g_ref, w_ref, kg_ref, kn_ref, kv_ref, k_ref, v_ref):
        x = h_ref[...]
        kn = (x * _rstd(x) * g_ref[...]).astype(BF)
        kn_ref[...] = kn
        kv = jnp.dot(kn, w_ref[...], preferred_element_type=F32)
        kv_ref[...] = kv
        k_ref[...] = _seg_rms(kv[:, :KVD], kg_ref[...], NKV).astype(BF)
        v_ref[...] = kv[:, KVD:].astype(BF)

    return pl.pallas_call(
        body, name="kv_fwd", grid=(t // tm,),
        in_specs=[_row(tm, D), _res((1, D)), _res((D, 2 * KVD)), _res((1, HD))],
        out_specs=[_row(tm, D), _row(tm, 2 * KVD), _row(tm, KVD), _row(tm, KVD)],
        out_shape=[_sds((t, D), BF), _sds((t, 2 * KVD), F32), _sds((t, KVD), BF), _sds((t, KVD), BF)],
        compiler_params=_cp(1),
    )(h, kvn, w_kv, kng)


def q_fwd(h, nm, l, w_q, j, tm):
    t = h.shape[0]

    def body(h_ref, g_ref, w_ref, u_ref, q_ref):
        x = h_ref[...]
        ub = (x * _rstd(x) * g_ref[...]).astype(BF)
        u_ref[...] = ub
        q_ref[...] = jnp.dot(ub, w_ref[...], preferred_element_type=F32)

    return pl.pallas_call(
        body, name=f"q_fwd{j}", grid=(t // tm,),
        in_specs=[_row(tm, D), _lay(l, (1, D)), _lay(j, (D, D))],
        out_specs=[_row(tm, D), _row(tm, D)], out_shape=[_sds((t, D), BF), _sds((t, D), F32)],
        compiler_params=_cp(1),
    )(h, nm, w_q)


RQ = NH // NKV


def _attn_specs(nb, lp):
    cur = lambda c: pl.BlockSpec((QB, c), lambda b, n: (b * nb + n, 0))
    seq = pl.BlockSpec((None, lp, KVD), lambda b, n: (b, 0, 0))
    return cur, seq


def _attn_masks(n, start):
    rows = RQ * QB
    qpos = n * QB + (lax.broadcasted_iota(jnp.int32, (rows, 2 * QB), 0) & (QB - 1))
    kpos = start + lax.broadcasted_iota(jnp.int32, (rows, 2 * QB), 1)
    band = (kpos <= qpos) & (qpos - kpos < QB) & (kpos >= NMETA)
    qm = n * QB + (lax.broadcasted_iota(jnp.int32, (rows, NMETA), 0) & (QB - 1))
    meta = lax.broadcasted_iota(jnp.int32, (rows, NMETA), 1) <= qm
    return band, meta


def _stack_heads(ref, g, fn):
    return jnp.concatenate([fn(ref[:, HD * (g * RQ + r):HD * (g * RQ + r) + HD]) for r in range(RQ)], axis=0)


def _stack_cols(ref, g):
    return jnp.concatenate([ref[:, g * RQ + r:g * RQ + r + 1] for r in range(RQ)], axis=0)


def _stack_sinks(sk_ref, g):
    return jnp.concatenate([jnp.broadcast_to(sk_ref[:, g * RQ + r:g * RQ + r + 1], (QB, 1)) for r in range(RQ)], axis=0)


def attn_fwd(q, k, v, qg, sinks, j, bl, lp):
    t = q.shape[0]
    nb = lp // QB
    cur, seq = _attn_specs(nb, lp)

    def body(q_ref, k_ref, v_ref, qg_ref, sk_ref, o_ref, lse_ref):
        n = pl.program_id(1)
        start = pl.multiple_of(jnp.maximum(n - 1, 0) * QB, QB)
        m_band, m_meta = _attn_masks(n, start)
        band = pl.ds(start, 2 * QB)
        lane = lax.broadcasted_iota(jnp.int32, (QB, NH), 1)
        lse = jnp.zeros((QB, NH), F32)
        for g in range(NKV):
            gs = slice(HD * g, HD * g + HD)
            qn = _stack_heads(q_ref, g, lambda x: (x * _rstd(x) * qg_ref[...]).astype(BF))
            sink = _stack_sinks(sk_ref, g)
            s_b = jnp.where(m_band, _dot_nt(qn, k_ref[band, gs]) * SCALE, NEG)
            s_m = jnp.where(m_meta, _dot_nt(qn, k_ref[0:NMETA, gs]) * SCALE, NEG)
            mx = jnp.maximum(jnp.maximum(jnp.max(s_b, -1, keepdims=True), jnp.max(s_m, -1, keepdims=True)), sink)
            p_b, p_m = jnp.exp(s_b - mx), jnp.exp(s_m - mx)
            den = jnp.sum(p_b, -1, keepdims=True) + jnp.sum(p_m, -1, keepdims=True) + jnp.exp(sink - mx)
            inv = 1.0 / den
            o = _dot(p_b * inv, v_ref[band, gs]) + _dot(p_m * inv, v_ref[0:NMETA, gs])
            l = mx + jnp.log(den)
            for r in range(RQ):
                h = g * RQ + r
                o_ref[:, HD * h:HD * h + HD] = o[r * QB:(r + 1) * QB].astype(BF)
                lse = jnp.where(lane == h, l[r * QB:(r + 1) * QB], lse)
        lse_ref[...] = lse

    return pl.pallas_call(
        body, name=f"attn_fwd{j}", grid=(bl, nb),
        in_specs=[cur(D), seq, seq,
                  pl.BlockSpec((None, 1, HD), lambda b, n: (j, 0, 0)), pl.BlockSpec((None, 1, NH), lambda b, n: (j, 0, 0))],
        out_specs=[cur(D), cur(NH)], out_shape=[_sds((t, D), BF), _sds((t, NH), F32)],
        compiler_params=_cp(2),
    )(q, k, v, qg, sinks)


def loss_fwd(h, tgt):
    bl, lp, _ = h.shape
    seq = tgt.shape[1]
    cb = 256

    def body(h_ref, t_ref, dh_ref, loss_ref):
        _init(loss_ref, (pl.program_id(0) == 0) & (pl.program_id(1) == 0))
        err = h_ref[NMETA:NMETA + seq, :] - t_ref[...]
        dh_ref[...] = jnp.zeros_like(dh_ref)
        dh_ref[NMETA:NMETA + seq, :] = err * (1.0 / D)
        loss_ref[...] += (0.5 / D) * jnp.sum(err * err)

    return pl.pallas_call(
        body, name="loss_fwd", grid=(bl, D // cb),
        in_specs=[pl.BlockSpec((None, lp, cb), lambda b, c: (b, 0, c)), pl.BlockSpec((None, seq, cb), lambda b, c: (b, 0, c))],
        out_specs=[pl.BlockSpec((None, lp, cb), lambda b, c: (b, 0, c)), pl.BlockSpec((8, 128), lambda b, c: (0, 0))],
        out_shape=[_sds((bl, lp, D), F32), _sds((8, 128), F32)],
        compiler_params=_cp(2),
    )(h, tgt)


def ffn_bwd_x(dh2, g, up, h1, nf, l, wd, wg, wu, tm, xch):
    t = dh2.shape[0]

    def body(dh2_ref, g_ref, up_ref, h1_ref, nf_ref, wd_ref, wg_ref, wu_ref, dg_ref, du_ref, dh1_ref, dnf_ref):
        _init(dnf_ref, pl.program_id(0) == 0)
        dh2v = dh2_ref[...]
        dhid = _dot_nt(dh2v, wd_ref[...])
        gv = g_ref[...].astype(F32)
        uv = up_ref[...].astype(F32)
        sg = _sig(gv)
        dgv = (dhid * uv * (sg * (1.0 + gv * (1.0 - sg)))).astype(BF)
        duv = (dhid * (gv * sg)).astype(BF)
        dg_ref[...] = dgv
        du_ref[...] = duv
        dnorm = _dot_nt(dgv, wg_ref[...]) + _dot_nt(duv, wu_ref[...])
        dx, dnf = _rms_bwd(h1_ref[...], nf_ref[...], dnorm)
        dh1_ref[...] = dh2v + dx
        dnf_ref[...] += dnf

    return _call(
        body, f"ffn_bwd_x{l}", (t // tm,),
        [_row(tm, D), _row(tm, DFF), _row(tm, DFF), _row(tm, D), _lay(l, (1, D)),
         _lay(l, (DFF, D)), _lay(l, (D, DFF)), _lay(l, (D, DFF))],
        [_row(tm, DFF), _row(tm, DFF), _row(tm, D), _acc((1, D))],
        [_sds((t, DFF), BF), _sds((t, DFF), BF), _sds((t, D), F32), _sds((1, D), F32)],
        (dh2, g, up, h1, nf, wd, wg, wu), xch=xch)


def mm_tn(x, dy, tm, name, split=False):
    t, kk = x.shape
    nn = dy.shape[1]
    n8 = nn // NDEV
    nsteps = t // tm

    def body(x_ref, dy_ref, o_ref, acc):
        i = pl.program_id(0)
        _init(acc, i == 0)
        acc[...] += _dot_tn(x_ref[...], dy_ref[...])

        @pl.when(i == nsteps - 1)
        def _():
            if split:
                for p in range(NDEV):
                    o_ref[p] = acc[:, p * n8:(p + 1) * n8].astype(BF)
            else:
                o_ref[...] = acc[...].astype(BF)

    oshape = (NDEV, kk, n8) if split else (kk, nn)
    return pl.pallas_call(
        body, name=name, grid=(nsteps,), in_specs=[_row(tm, kk), _row(tm, nn)],
        out_specs=_acc(oshape), out_shape=_sds(oshape, BF), scratch_shapes=[pltpu.VMEM((kk, nn), F32)],
        compiler_params=_cp(1),
    )(x, dy)


def proj_bwd(dy, w, lw, wshape, h, g, lg, dh_in, tm, name):
    t = h.shape[0]
    nn = dy.shape[1]
    wspec = _res(wshape) if lw is None else _lay(lw, wshape)
    gspec = _res((1, D)) if lg is None else _lay(lg, (1, D))

    def body(dy_ref, w_ref, h_ref, g_ref, dhin_ref, dh_ref, dg_ref):
        _init(dg_ref, pl.program_id(0) == 0)
        du = _dot_nt(dy_ref[...], w_ref[...])
        dx, dg = _rms_bwd(h_ref[...], g_ref[...], du)
        dh_ref[...] = dhin_ref[...] + dx
        dg_ref[...] += dg

    return pl.pallas_call(
        body, name=name, grid=(t // tm,),
        in_specs=[_row(tm, nn), wspec, _row(tm, D), gspec, _row(tm, D)],
        out_specs=[_row(tm, D), _acc((1, D))], out_shape=[_sds((t, D), F32), _sds((1, D), F32)],
        compiler_params=_cp(1),
    )(dy, w, h, g, dh_in)


def out_proj_bwd(dh1, w, lw, tm, name):
    t = dh1.shape[0]

    def body(dh1_ref, w_ref, do_ref):
        do_ref[...] = _dot_nt(dh1_ref[...], w_ref[...]).astype(BF)

    return pl.pallas_call(
        body, name=name, grid=(t // tm,), in_specs=[_row(tm, D), _lay(lw, (D, D))],
        out_specs=_row(tm, D), out_shape=_sds((t, D), BF), compiler_params=_cp(1),
    )(dh1, w)


def attn_bwd(q, k, v, do, o, lse, qg, sinks, j, bl, lp, xch):
    t = q.shape[0]
    nb = lp // QB
    cur, seq = _attn_specs(nb, lp)

    def body(q_ref, k_ref, v_ref, do_ref, o_ref, lse_ref, qg_ref, sk_ref, dq_ref, dk_ref, dv_ref, dqg_ref, dsk_ref):
        b, n = pl.program_id(0), pl.program_id(1)
        _init(dk_ref, n == 0)
        _init(dv_ref, n == 0)
        _init(dqg_ref, (b == 0) & (n == 0))
        _init(dsk_ref, (b == 0) & (n == 0))
        start = pl.multiple_of(jnp.maximum(n - 1, 0) * QB, QB)
        m_band, m_meta = _attn_masks(n, start)
        band = pl.ds(start, 2 * QB)
        lane = lax.broadcasted_iota(jnp.int32, (1, NH), 1)
        dqg = jnp.zeros((1, HD), F32)
        dsk = jnp.zeros((1, NH), F32)
        dk_b, dk_m, dv_b, dv_m = [], [], [], []
        for g in range(NKV):
            gs = slice(HD * g, HD * g + HD)
            kb, km, vb, vm = k_ref[band, gs], k_ref[0:NMETA, gs], v_ref[band, gs], v_ref[0:NMETA, gs]
            qh = _stack_heads(q_ref, g, lambda x: x)
            rs = _rstd(qh)
            qn = (qh * rs * qg_ref[...]).astype(BF)
            ls = _stack_cols(lse_ref, g)
            p_b = jnp.where(m_band, jnp.exp(_dot_nt(qn, kb) * SCALE - ls), 0.0)
            p_m = jnp.where(m_meta, jnp.exp(_dot_nt(qn, km) * SCALE - ls), 0.0)
            doh = _stack_heads(do_ref, g, lambda x: x)
            delta = jnp.sum(doh.astype(F32) * _stack_heads(o_ref, g, lambda x: x).astype(F32), axis=-1, keepdims=True)
            ds_b = (p_b * (_dot_nt(doh, vb) - delta)).astype(BF)
            ds_m = (p_m * (_dot_nt(doh, vm) - delta)).astype(BF)
            dqn = (_dot(ds_b, kb) + _dot(ds_m, km)) * SCALE
            dk_b.append(_dot_tn(qn, ds_b).T * SCALE)
            dk_m.append(_dot_tn(qn, ds_m).T * SCALE)
            dv_b.append(_dot_tn(doh, p_b).T)
            dv_m.append(_dot_tn(doh, p_m).T)
            dsink = jnp.exp(_stack_sinks(sk_ref, g) - ls) * delta
            z = dqn * qg_ref[...]
            dq = rs * z - qh * (rs * rs * rs * jnp.mean(z * qh, axis=-1, keepdims=True))
            dqg = dqg + jnp.sum(dqn * qh * rs, axis=0, keepdims=True)
            for r in range(RQ):
                h = g * RQ + r
                dq_ref[:, HD * h:HD * h + HD] = dq[r * QB:(r + 1) * QB]
                dsk = dsk + jnp.where(lane == h, -jnp.sum(dsink[r * QB:(r + 1) * QB]), 0.0)
        cat = lambda xs: jnp.concatenate(xs, axis=1)
        dk_ref[band, :] += cat(dk_b)
        dv_ref[band, :] += cat(dv_b)
        dk_ref[0:NMETA, :] += cat(dk_m)
        dv_ref[0:NMETA, :] += cat(dv_m)
        dqg_ref[...] += dqg
        dsk_ref[...] += dsk

    return _call(
        body, f"attn_bwd{j}", (bl, nb),
        [cur(D), seq, seq, cur(D), cur(D), cur(NH),
         pl.BlockSpec((None, 1, HD), lambda b, n: (j, 0, 0)), pl.BlockSpec((None, 1, NH), lambda b, n: (j, 0, 0))],
        [cur(D), seq, seq, pl.BlockSpec((1, HD), lambda b, n: (0, 0)), pl.BlockSpec((1, NH), lambda b, n: (0, 0))],
        [_sds((t, D), F32), _sds((bl, lp, KVD), F32), _sds((bl, lp, KVD), F32), _sds((1, HD), F32), _sds((1, NH), F32)],
        (q, k, v, do, o, lse, qg, sinks), xch=xch)


def kv_bwd_pre(dk0, dk1, dv0, dv1, kv, kng, tm):
    t = kv.shape[0]

    def body(dk0_ref, dk1_ref, dv0_ref, dv1_ref, kv_ref, g_ref, dkv_ref, dg_ref):
        _init(dg_ref, pl.program_id(0) == 0)
        dk = dk0_ref[...] + dk1_ref[...]
        dg = jnp.zeros((1, HD), F32)
        outs = []
        for s in range(NKV):
            sl = slice(HD * s, HD * s + HD)
            dx, dgs = _rms_bwd(kv_ref[:, sl], g_ref[...], dk[:, sl])
            outs.append(dx)
            dg = dg + dgs
        dkv_ref[:, :KVD] = jnp.concatenate(outs, axis=1).astype(BF)
        dkv_ref[:, KVD:] = (dv0_ref[...] + dv1_ref[...]).astype(BF)
        dg_ref[...] += dg

    return pl.pallas_call(
        body, name="kv_bwd_pre", grid=(t // tm,),
        in_specs=[_row(tm, KVD)] * 4 + [_row(tm, 2 * KVD), _res((1, HD))],
        out_specs=[_row(tm, 2 * KVD), _acc((1, HD))], out_shape=[_sds((t, 2 * KVD), BF), _sds((1, HD), F32)],
        compiler_params=_cp(1),
    )(dk0, dk1, dv0, dv1, kv, kng)


def conv_out_bwd(dh1, c, ln_g, ln_b, w_out, i, tm):
    t = dh1.shape[0]

    def body(dh1_ref, c_ref, g_ref, b_ref, w_ref, dc_ref, dg_ref, db_ref, dbo_ref):
        first = pl.program_id(0) == 0
        _init(dg_ref, first)
        _init(db_ref, first)
        _init(dbo_ref, first)
        dh1v = dh1_ref[...]
        ds = _dot_nt(dh1v, w_ref[...])
        cv = c_ref[...]
        xc = cv - jnp.mean(cv, axis=-1, keepdims=True)
        rstd = lax.rsqrt(jnp.mean(xc * xc, axis=-1, keepdims=True) + EPS)
        xh = xc * rstd
        n = xh * g_ref[...] + b_ref[...]
        sg = _sig(n)
        dn = ds * (sg * (1.0 + n * (1.0 - sg)))
        dxh = dn * g_ref[...]
        dc_ref[...] = rstd * (dxh - jnp.mean(dxh, axis=-1, keepdims=True) - xh * jnp.mean(dxh * xh, axis=-1, keepdims=True))
        dg_ref[...] += jnp.sum(dn * xh, axis=0, keepdims=True)
        db_ref[...] += jnp.sum(dn, axis=0, keepdims=True)
        dbo_ref[...] += jnp.sum(dh1v, axis=0, keepdims=True)

    return pl.pallas_call(
        body, name=f"conv_out_bwd{i}", grid=(t // tm,),
        in_specs=[_row(tm, D), _row(tm, D), _lay(i, (1, D)), _lay(i, (1, D)), _lay(i, (D, D))],
        out_specs=[_row(tm, D), _acc((1, D)), _acc((1, D)), _acc((1, D))],
        out_shape=[_sds((t, D), F32)] + [_sds((1, D), F32)] * 3,
        compiler_params=_cp(1),
    )(dh1, c, ln_g, ln_b, w_out)


def conv_mid_bwd(dc, a, big, dw, i, tm, tpb, xch):
    t = dc.shape[0]
    nsteps = t // tm

    def body(dc_ref, nxt_ref, a_ref, prv_ref, big_ref, dw_ref, da_ref, dbin_ref, ddw_ref, dce, ae, wacc, bacc):
        i_ = pl.program_id(0)
        _init(wacc, i_ == 0)
        _init(bacc, i_ == 0)
        dce[0:tm] = dc_ref[...]
        dce[tm:] = jnp.where(i_ % tpb == tpb - 1, 0.0, nxt_ref[...])
        ae[0:HALO] = jnp.where(i_ % tpb == 0, 0.0, prv_ref[...])
        ae[HALO:] = a_ref[...]

        def chunk(k, carry):
            r0 = pl.multiple_of(k * CHUNK, CHUNK)
            wdc = _shifted(dce[pl.ds(r0, 2 * CHUNK), :])
            wa = _shifted(ae[pl.ds(r0, 2 * CHUNK), :])
            dcc = wdc[0][0:CHUNK]
            da = jnp.zeros((CHUNK, D), F32)
            for j in range(CW):
                da = da + dw_ref[j:j + 1, :] * _tap(wdc, CW - 1 - j)
                wacc[j] += _fold8(dcc * _tap(wa, j + 2))
            bv = big_ref[pl.ds(r0, CHUNK), :]
            a1, sg = bv[:, :D], _sig(bv[:, D:])
            d1 = da * sg
            d2 = da * a1 * sg * (1.0 - sg)
            da_ref[pl.ds(r0, CHUNK), 0:D] = d1.astype(BF)
            da_ref[pl.ds(r0, CHUNK), D:2 * D] = d2.astype(BF)
            bacc[:, 0:D] += _fold8(d1)
            bacc[:, D:2 * D] += _fold8(d2)
            return carry

        lax.fori_loop(0, tm // CHUNK, chunk, 0)

        @pl.when(i_ == nsteps - 1)
        def _():
            dbin_ref[...] = jnp.sum(bacc[...], axis=0, keepdims=True)
            ddw_ref[...] = jnp.sum(wacc[...], axis=1)

    return _call(
        body, f"conv_mid_bwd{i}", (nsteps,),
        [_row(tm, D), _next_halo(tm, t), _row(tm, D), _prev_halo(tm), _row(tm, 2 * D), _lay(i, (CW, D))],
        [_row(tm, 2 * D), _acc((1, 2 * D)), _acc((CW + 1, D))],
        [_sds((t, 2 * D), BF), _sds((1, 2 * D), F32), _sds((CW + 1, D), F32)],
        (dc, dc, a, a, big, dw),
        scratch=[pltpu.VMEM((tm + HALO, D), F32), pltpu.VMEM((tm + HALO, D), F32),
                 pltpu.VMEM((CW + 1, 8, D), F32), pltpu.VMEM((8, 2 * D), F32)], xch=xch)


def input_grads(dh0, seq):
    bl, lp, _ = dh0.shape
    cb = 256

    def body(dh_ref, gx_ref, gm_ref):
        _init(gm_ref, pl.program_id(1) == 0)
        gx_ref[...] = dh_ref[NMETA:NMETA + seq, :]
        gm_ref[...] += dh_ref[0:NMETA, :]

    return pl.pallas_call(
        body, name="input_grads", grid=(D // cb, bl),
        in_specs=[pl.BlockSpec((None, lp, cb), lambda c, b: (b, 0, c))],
        out_specs=[pl.BlockSpec((None, seq, cb), lambda c, b: (b, 0, c)), pl.BlockSpec((NMETA, cb), lambda c, b: (0, c))],
        out_shape=[_sds((bl, seq, D), F32), _sds((NMETA, D), F32)],
        compiler_params=_cp(2),
    )(dh0)


def local_step(x, tgt, meta8, w):
    bl, seq, _ = x.shape
    lp = -(-(NMETA + seq) // QB) * QB
    tpb = 4
    tm = lp // tpb
    t = bl * lp
    na, nbl = 2, 2
    flat = lambda a: a.reshape(t, D)

    h = flat(embed(x, meta8, lp))
    saved = []
    kvs = None
    for l in range(4):
        rec = {"h": h}
        if l < na:
            rec["u"], rec["big"], rec["a"] = conv_in_fwd(h, w["norm_mix"], l, w["conv_w_in"], w["conv_b_in"], l, tm)
            rec["c"], rec["s"] = conv_mid_fwd(rec["a"], w["conv_dw"], w["conv_ln_g"], w["conv_ln_b"], l, tm, tpb)
            rec["h1"], rec["u2"] = mixer_out_fwd(h, rec["s"], w["conv_w_out"], l, w["conv_b_out"], w["norm_ffn"], l, tm,
                                                 f"conv_out_fwd{l}")
        else:
            j = l - na
            if kvs is None:
                kvs = dict(zip(("kn", "kv", "k", "v"), kv_fwd(h, w["kv_norm"], w["w_kv"], w["k_norm"], tm)))
                kvs["h"] = h
                kvs["k3"], kvs["v3"] = kvs["k"].reshape(bl, lp, KVD), kvs["v"].reshape(bl, lp, KVD)
            rec["u"], rec["q"] = q_fwd(h, w["norm_mix"], l, w["w_q"], j, tm)
            rec["o"], rec["lse"] = attn_fwd(rec["q"], kvs["k3"], kvs["v3"], w["q_norm"], w["attn_sinks"], j, bl, lp)
            rec["h1"], rec["u2"] = mixer_out_fwd(h, rec["o"], w["w_o"], j, None, w["norm_ffn"], l, tm, f"attn_out_fwd{j}")
        rec["g"], rec["up"], rec["hid"] = ffn_up_fwd(rec["u2"], w["ffn_w_gate"], w["ffn_w_up"], l, tm // 2)
        h = ffn_down_fwd(rec["hid"], rec["h1"], w["ffn_w_down"], l, tm)
        saved.append(rec)

    dh3, loss_blk = loss_fwd(h.reshape(bl, lp, D), tgt)
    dh = flat(dh3)

    big, small, arrived = {}, {}, {}
    dks, dvs = [], []
    pending = []

    def carried(names, arrivals):
        arrived.update(zip([nm for nm, _ in names], arrivals))

    for l in reversed(range(4)):
        rec = saved[l]
        riders, pending = pending, []
        (dg, du, dh1, small[f"norm_ffn{l}"]), got = ffn_bwd_x(
            dh, rec["g"], rec["up"], rec["h1"], w["norm_ffn"], l, w["ffn_w_down"], w["ffn_w_gate"], w["ffn_w_up"], tm // 2,
            [(big[nm], kind) for nm, kind in riders])
        carried(riders, got)
        big[f"ffn_w_down{l}"] = mm_tn(rec["hid"], dh, 2 * tm, f"dw_down{l}")
        big[f"ffn_w_gate{l}"] = mm_tn(rec["u2"], dg, 2 * tm, f"dw_gate{l}", split=True)
        big[f"ffn_w_up{l}"] = mm_tn(rec["u2"], du, 2 * tm, f"dw_up{l}", split=True)
        riders = [(f"ffn_w_down{l}", "rows"), (f"ffn_w_gate{l}", "pieces"), (f"ffn_w_up{l}", "pieces")]
        xch = [(big[nm], kind) for nm, kind in riders]
        if l >= na:
            j = l - na
            do = out_proj_bwd(dh1, w["w_o"], j, tm, f"attn_out_bwd{j}")
            big[f"w_o{j}"] = mm_tn(rec["o"], dh1, 2 * tm, f"dw_o{j}")
            (dq, dk, dv, small[f"q_norm{j}"], small[f"attn_sinks{j}"]), got = attn_bwd(
                rec["q"], kvs["k3"], kvs["v3"], do, rec["o"], rec["lse"], w["q_norm"], w["attn_sinks"], j, bl, lp, xch)
            carried(riders, got)
            dks.append(dk.reshape(t, KVD))
            dvs.append(dv.reshape(t, KVD))
            big[f"w_q{j}"] = mm_tn(rec["u"], dq, 2 * tm, f"dw_q{j}")
            dh, small[f"norm_mix{l}"] = proj_bwd(dq, w["w_q"], j, (D, D), rec["h"], w["norm_mix"], l, dh1, tm, f"q_bwd{j}")
            pending = [(f"w_o{j}", "rows"), (f"w_q{j}", "rows")]
            if l == na:
                dkv, small["k_norm"] = kv_bwd_pre(dks[0], dks[1], dvs[0], dvs[1], kvs["kv"], w["k_norm"], tm)
                big["w_kv"] = mm_tn(kvs["kn"], dkv, 2 * tm, "dw_kv")
                dh, small["kv_norm"] = proj_bwd(dkv, w["w_kv"], None, (D, 2 * KVD), kvs["h"], w["kv_norm"], None, dh, tm, "kv_bwd")
                pending.append(("w_kv", "rows"))
        else:
            dc, small[f"conv_ln_g{l}"], small[f"conv_ln_b{l}"], small[f"conv_b_out{l}"] = conv_out_bwd(
                dh1, rec["c"], w["conv_ln_g"], w["conv_ln_b"], w["conv_w_out"], l, tm)
            big[f"conv_w_out{l}"] = mm_tn(rec["s"], dh1, 2 * tm, f"dw_conv_out{l}")
            (da, small[f"conv_b_in{l}"], ddw), got = conv_mid_bwd(dc, rec["a"], rec["big"], w["conv_dw"], l, tm, tpb, xch)
            carried(riders, got)
            small[f"conv_dw{l}"] = ddw[:CW]
            big[f"conv_w_in{l}"] = mm_tn(rec["u"], da, 2 * tm, f"dw_conv_in{l}", split=True)
            dh, small[f"norm_mix{l}"] = proj_bwd(da, w["conv_w_in"], l, (D, 2 * D), rec["h"], w["norm_mix"], l, dh1, tm,
                                                 f"conv_in_bwd{l}")
            pending = [(f"conv_w_out{l}", "rows"), (f"conv_w_in{l}", "pieces")]
    carried(pending, exchange_grads([(big[nm], kind) for nm, kind in pending]))
    grad_x, small["meta_tokens"] = input_grads(dh.reshape(bl, lp, D), seq)
    return loss_blk, grad_x, big, arrived, small


def all_gather_weights(shards, kinds):
    n = len(shards)

    def out_shape(s, kind):
        if kind == "rows":
            return _sds(s.shape[:-2] + (NDEV * s.shape[-2], s.shape[-1]), s.dtype)
        return _sds((NDEV,) + s.shape, s.dtype)

    def body(*refs):
        srcs, outs = refs[:n], refs[n:2 * n]
        send_sems, recv_sems, local_sems = refs[2 * n:]
        x, y, c = lax.axis_index("x"), lax.axis_index("y"), lax.axis_index("c")
        me, sibling = (x, y, c), (x, y, 1 - c)
        chips = [(1 - x, y), (x, 1 - y), (1 - x, 1 - y)]

        def slot(a, owner):
            idx = 4 * owner[0] + 2 * owner[1] + owner[2]
            if kinds[a] == "rows":
                r = srcs[a].shape[-2]
                if len(srcs[a].shape) == 3:
                    return outs[a].at[:, pl.ds(idx * r, r), :]
                return outs[a].at[pl.ds(idx * r, r), :]
            return outs[a].at[idx]

        def copy(a, k, block, to, src=None):
            return pltpu.make_async_remote_copy(
                src_ref=slot(a, block) if src is None else src, dst_ref=slot(a, block),
                send_sem=send_sems.at[a, k], recv_sem=recv_sems.at[a, k], device_id=to, device_id_type=MESH)

        mine = [pltpu.make_async_copy(srcs[a], slot(a, me), local_sems.at[a]) for a in range(n)]
        for cp in mine:
            cp.start()
        first = []
        for a in range(n):
            first.append(copy(a, 0, me, sibling, src=srcs[a]))
            first += [copy(a, 1 + j, me, (*chip, c), src=srcs[a]) for j, chip in enumerate(chips)]
        for cp in first:
            cp.start()
        passed = []
        for a in range(n):
            for j, chip in enumerate(chips):
                copy(a, 1 + j, (*chip, c), me).wait_recv()
                cp = copy(a, 4 + j, (*chip, c), sibling)
                cp.start()
                passed.append(cp)
        for a in range(n):
            copy(a, 0, sibling, me).wait_recv()
            for j, chip in enumerate(chips):
                copy(a, 4 + j, (*chip, 1 - c), me).wait_recv()
        for cp in first + passed:
            cp.wait_send()
        for cp in mine:
            cp.wait()

    any_spec = pl.BlockSpec(memory_space=pl.ANY)
    return pl.pallas_call(
        body, name="all_gather_weights", out_shape=[out_shape(s, k) for s, k in zip(shards, kinds)],
        in_specs=[any_spec] * n, out_specs=[any_spec] * n,
        scratch_shapes=[pltpu.SemaphoreType.DMA((n, 7)), pltpu.SemaphoreType.DMA((n, 7)), pltpu.SemaphoreType.DMA((n,))],
    )(*shards)


def all_reduce_small(buf):
    rows = buf.shape[0]

    def body(x_ref, o_ref, g_ref, send_sems, recv_sems):
        me = _my_index()
        sends = []
        for k in range(1, NDEV):
            peer = me ^ k
            cp = pltpu.make_async_remote_copy(
                src_ref=x_ref, dst_ref=g_ref.at[me], send_sem=send_sems.at[k - 1], recv_sem=recv_sems.at[k - 1],
                device_id=_coords(peer), device_id_type=MESH)
            cp.start()
            sends.append(cp)
        g_ref[me] = x_ref[...]
        for k in range(1, NDEV):
            peer = me ^ k
            pltpu.make_async_remote_copy(
                src_ref=x_ref, dst_ref=g_ref.at[peer], send_sem=send_sems.at[k - 1], recv_sem=recv_sems.at[k - 1],
                device_id=_coords(peer), device_id_type=MESH).wait_recv()
        for cp in sends:
            cp.wait_send()
        acc = g_ref[0]
        for p in range(1, NDEV):
            acc = acc + g_ref[p]
        o_ref[...] = acc

    return pl.pallas_call(
        body, name="all_reduce_small", out_shape=_sds((rows, D), F32),
        in_specs=[pl.BlockSpec(memory_space=pltpu.VMEM)], out_specs=pl.BlockSpec(memory_space=pltpu.VMEM),
        scratch_shapes=[pltpu.VMEM((NDEV, rows, D), F32), pltpu.SemaphoreType.DMA((7,)), pltpu.SemaphoreType.DMA((7,))],
    )(buf)


def cast_bf16(ws):
    n = len(ws)

    def body(*refs):
        for a in range(n):
            refs[n + a][...] = refs[a][...].astype(BF)

    return pl.pallas_call(
        body, name="cast_bf16", out_shape=[_sds(x.shape, BF) for x in ws],
        compiler_params=pltpu.CompilerParams(vmem_limit_bytes=VMEM_LIMIT),
    )(*ws)


def join_columns(w8, name):
    _, lyr, kk, n8 = w8.shape

    def body(x_ref, o_ref):
        o_ref[...] = jnp.concatenate([x_ref[p] for p in range(NDEV)], axis=1)

    return pl.pallas_call(
        body, name=name, grid=(lyr,),
        in_specs=[pl.BlockSpec((NDEV, None, kk, n8), lambda l: (0, l, 0, 0))],
        out_specs=pl.BlockSpec((None, kk, NDEV * n8), lambda l: (l, 0, 0)),
        out_shape=_sds((lyr, kk, NDEV * n8), w8.dtype), compiler_params=_cp(1),
    )(w8)


def _adamw_math(w, m, v, g):
    m2 = B1 * m + (1.0 - B1) * g
    v2 = B2 * v + (1.0 - B2) * (g * g)
    mh = m2 / (1.0 - B1 ** STEP)
    vh = v2 / (1.0 - B2 ** STEP)
    return -LR * (mh / (jnp.sqrt(vh) + AEPS) + WD * w), m2, v2


def adamw_big(w, m, v, parts, name):
    lyr, r, c = w.shape
    by_cols = c >= 512
    blk = (lyr, r, 256) if by_cols else (lyr, 256 if r % 256 == 0 else r, c)
    imap = (lambda i: (0, 0, i)) if by_cols else (lambda i: (0, i, 0))

    def body(w_ref, m_ref, v_ref, *rest):
        p_refs, (g_ref, d_ref, m2_ref, v2_ref) = rest[:lyr], rest[lyr:]
        for l in range(lyr):
            g = p_refs[l][0].astype(F32)
            for q in range(1, NDEV):
                g = g + p_refs[l][q].astype(F32)
            g_ref[l] = g
            d_ref[l], m2_ref[l], v2_ref[l] = _adamw_math(w_ref[l], m_ref[l], v_ref[l], g)

    spec = pl.BlockSpec(blk, imap)
    pspec = pl.BlockSpec((NDEV,) + blk[1:], imap)
    return pl.pallas_call(
        body, name=name, grid=((c // 256) if by_cols else (r // blk[1]),),
        in_specs=[spec, spec, spec] + [pspec] * lyr,
        out_specs=[spec] * 4, out_shape=[_sds((lyr, r, c), F32)] * 4, compiler_params=_cp(1),
    )(w, m, v, *parts)


def adamw_small(w, m, v, g, name):
    def body(w_ref, m_ref, v_ref, g_ref, d_ref, m2_ref, v2_ref):
        d_ref[...], m2_ref[...], v2_ref[...] = _adamw_math(w_ref[...], m_ref[...], v_ref[...], g_ref[...])

    return pl.pallas_call(body, name=name, out_shape=[_sds(w.shape, F32)] * 3)(w, m, v, g)


NAMES = ["meta_tokens", "norm_mix", "norm_ffn", "conv_w_in", "conv_b_in", "conv_dw", "conv_ln_g", "conv_ln_b", "conv_w_out",
         "conv_b_out", "kv_norm", "w_kv", "k_norm", "w_q", "q_norm", "attn_sinks", "w_o", "ffn_w_gate", "ffn_w_up", "ffn_w_down"]
BIG = {"conv_w_in": "pieces", "conv_w_out": "rows", "w_kv": "rows", "w_q": "rows", "w_o": "rows",
       "ffn_w_gate": "pieces", "ffn_w_up": "pieces", "ffn_w_down": "rows"}
REP_ROWS = 16


def _pad_cols(a, width):
    return jnp.pad(a, ((0, 0), (0, width - a.shape[1])))


def _pack_rep(p):
    rows = [p["norm_mix"], p["norm_ffn"], p["kv_norm"].reshape(1, D), _pad_cols(p["k_norm"].reshape(1, HD), D),
            _pad_cols(p["q_norm"], D), _pad_cols(p["attn_sinks"], D), jnp.zeros((2, D), F32)]
    return jnp.concatenate(rows, axis=0)


def _unpack_rep(a):
    return {"norm_mix": a[0:4], "norm_ffn": a[4:8], "kv_norm": a[8], "k_norm": a[9, :HD], "q_norm": a[10:12, :HD],
            "attn_sinks": a[12:14, :NH]}


SH_NAMES = ["meta_tokens", "conv_b_in", "conv_dw", "conv_ln_g", "conv_ln_b", "conv_b_out"]


def _pack_sh(p):
    c = D // NDEV
    rows = [p["meta_tokens"], p["conv_b_in"].reshape(4, c), p["conv_dw"].reshape(2 * CW, c), p["conv_ln_g"], p["conv_ln_b"],
            p["conv_b_out"]]
    return jnp.concatenate(rows, axis=0)


def _unpack_sh(a):
    c = D // NDEV
    return {"meta_tokens": a[0:16], "conv_b_in": a[16:20].reshape(2, 2 * c), "conv_dw": a[20:82].reshape(2, CW, c),
            "conv_ln_g": a[82:84], "conv_ln_b": a[84:86], "conv_b_out": a[86:88]}


def kernel(x, meta_tokens, norm_mix, norm_ffn, conv_w_in, conv_b_in, conv_dw, conv_ln_g, conv_ln_b, conv_w_out, conv_b_out, kv_norm, w_kv, k_norm, w_q, q_norm, attn_sinks, w_o, ffn_w_gate, ffn_w_up, ffn_w_down, loss_target, m_meta_tokens, m_norm_mix, m_norm_ffn, m_conv_w_in, m_conv_b_in, m_conv_dw, m_conv_ln_g, m_conv_ln_b, m_conv_w_out, m_conv_b_out, m_kv_norm, m_w_kv, m_k_norm, m_w_q, m_q_norm, m_attn_sinks, m_w_o, m_ffn_w_gate, m_ffn_w_up, m_ffn_w_down, v_meta_tokens, v_norm_mix, v_norm_ffn, v_conv_w_in, v_conv_b_in, v_conv_dw, v_conv_ln_g, v_conv_ln_b, v_conv_w_out, v_conv_b_out, v_kv_norm, v_w_kv, v_k_norm, v_w_q, v_q_norm, v_attn_sinks, v_w_o, v_ffn_w_gate, v_ffn_w_up, v_ffn_w_down):
    wts = dict(zip(NAMES, (meta_tokens, norm_mix, norm_ffn, conv_w_in, conv_b_in, conv_dw, conv_ln_g, conv_ln_b, conv_w_out,
                           conv_b_out, kv_norm, w_kv, k_norm, w_q, q_norm, attn_sinks, w_o, ffn_w_gate, ffn_w_up, ffn_w_down)))
    mom = dict(zip(NAMES, (m_meta_tokens, m_norm_mix, m_norm_ffn, m_conv_w_in, m_conv_b_in, m_conv_dw, m_conv_ln_g, m_conv_ln_b,
                           m_conv_w_out, m_conv_b_out, m_kv_norm, m_w_kv, m_k_norm, m_w_q, m_q_norm, m_attn_sinks, m_w_o,
                           m_ffn_w_gate, m_ffn_w_up, m_ffn_w_down)))
    var = dict(zip(NAMES, (v_meta_tokens, v_norm_mix, v_norm_ffn, v_conv_w_in, v_conv_b_in, v_conv_dw, v_conv_ln_g, v_conv_ln_b,
                           v_conv_w_out, v_conv_b_out, v_kv_norm, v_w_kv, v_k_norm, v_w_q, v_q_norm, v_attn_sinks, v_w_o,
                           v_ffn_w_gate, v_ffn_w_up, v_ffn_w_down)))
    me = _my_index()
    c8 = D // NDEV

    big_names = list(BIG)
    shards = cast_bf16([wts[k] for k in big_names])
    vec_names = ["meta_tokens", "conv_b_in", "conv_dw", "conv_ln_g", "conv_ln_b", "conv_b_out"]
    gathered = all_gather_weights(list(shards) + [wts[k] for k in vec_names],
                                  [BIG[k] for k in big_names] + ["flat"] * len(vec_names))
    full = dict(zip(big_names + vec_names, gathered))
    w = {}
    for k in big_names:
        w[k] = join_columns(full[k] if full[k].ndim == 4 else full[k][:, None], "join_" + k) if BIG[k] == "pieces" else full[k]
    join_vec = lambda a: jnp.moveaxis(a, 0, -2).reshape(a.shape[1:-1] + (NDEV * a.shape[-1],))
    w["conv_b_in"] = join_vec(full["conv_b_in"]).reshape(2, 1, 2 * D)
    w["conv_dw"] = join_vec(full["conv_dw"])
    for k in ("conv_ln_g", "conv_ln_b", "conv_b_out"):
        w[k] = join_vec(full[k]).reshape(2, 1, D)
    w["norm_mix"] = norm_mix.reshape(4, 1, D)
    w["norm_ffn"] = norm_ffn.reshape(4, 1, D)
    w["kv_norm"] = kv_norm.reshape(1, D)
    w["k_norm"] = k_norm.reshape(1, HD)
    w["q_norm"] = q_norm.reshape(2, 1, HD)
    w["attn_sinks"] = attn_sinks.reshape(2, 1, NH)

    loss_blk, grad_x, _, arrived, gs = local_step(x, loss_target, full["meta_tokens"], w)

    stack = lambda k, n: jnp.concatenate([gs[f"{k}{i}"] for i in range(n)], axis=0)
    rep = {"norm_mix": stack("norm_mix", 4), "norm_ffn": stack("norm_ffn", 4), "kv_norm": gs["kv_norm"], "k_norm": gs["k_norm"],
           "q_norm": stack("q_norm", 2), "attn_sinks": stack("attn_sinks", 2)}
    loss_row = _pad_cols(loss_blk[0:1, 0:1], D)
    packed = jnp.concatenate(
        [_pack_rep(rep)[:14], loss_row, jnp.zeros((1, D), F32), gs["meta_tokens"], stack("conv_b_in", 2).reshape(4, D),
         stack("conv_dw", 2), stack("conv_ln_g", 2), stack("conv_ln_b", 2), stack("conv_b_out", 2)], axis=0)
    red = all_reduce_small(packed)
    loss = red[14, 0]
    cols = lambda a, width: lax.dynamic_slice_in_dim(a, me * width, width, axis=1)
    g_sh = jnp.concatenate(
        [cols(red[16:32], c8), cols(red[32:36].reshape(2, 2 * D), 2 * c8).reshape(4, c8), cols(red[36:98], c8),
         cols(red[98:100], c8), cols(red[100:102], c8), cols(red[102:104], c8)], axis=0)
    g_rep = red[0:REP_ROWS].at[14:].set(0.0)

    grads, delta, new_m, new_v = {}, {}, {}, {}
    for k in big_names:
        flat2 = wts[k].ndim == 2
        parts = [arrived[k]] if flat2 else [arrived[f"{k}{i}"] for i in range(wts[k].shape[0])]
        as3 = (lambda a: a[None]) if flat2 else (lambda a: a)
        outs = adamw_big(as3(wts[k]), as3(mom[k]), as3(var[k]), parts, "adamw_" + k)
        grads[k], delta[k], new_m[k], new_v[k] = [o[0] if flat2 else o for o in outs]
    d_rep, m_rep, v_rep = adamw_small(_pack_rep(wts), _pack_rep(mom), _pack_rep(var), g_rep, "adamw_rep")
    d_sh, m_sh, v_sh = adamw_small(_pack_sh(wts), _pack_sh(mom), _pack_sh(var), g_sh, "adamw_sh")
    for dst, a_rep, a_sh in ((grads, g_rep, g_sh), (delta, d_rep, d_sh), (new_m, m_rep, m_sh), (new_v, v_rep, v_sh)):
        dst.update(_unpack_rep(a_rep))
        dst.update(_unpack_sh(a_sh))
    return (loss, grad_x, *[grads[k] for k in NAMES], *[delta[k] for k in NAMES], *[new_m[k] for k in NAMES],
            *[new_v[k] for k in NAMES])
```

```python
import functools

import jax
import jax.numpy as jnp
from jax import lax
from jax.experimental import pallas as pl
from jax.experimental.pallas import tpu as pltpu

F32 = jnp.float32
BF = jnp.bfloat16

D = 1024
DFF = 2816
NH = 16
NKV = 4
HD = 64
KVD = NKV * HD
NMETA = 16
CW = 31
HALO = 32
CHUNK = 32
QB = 128
EPS = 1e-6
NEG = -1e30
NDEV = 8
SCALE = HD ** -0.5

LR, B1, B2, AEPS, WD, STEP = 0.001, 0.9, 0.999, 1e-08, 0.01, 10

VMEM_LIMIT = 56 * 2 ** 20
MESH = pl.DeviceIdType.MESH


def _cp(n):
    return pltpu.CompilerParams(dimension_semantics=("arbitrary",) * n, vmem_limit_bytes=VMEM_LIMIT)


def _row(tm, c):
    return pl.BlockSpec((tm, c), lambda i: (i, 0))


def _res(shape):
    return pl.BlockSpec(shape, lambda i: (0,) * len(shape), pipeline_mode=pl.Buffered(1))


def _lay(l, shape):
    return pl.BlockSpec((None,) + tuple(shape), lambda i: (l,) + (0,) * len(shape), pipeline_mode=pl.Buffered(1))


def _acc(shape):
    return pl.BlockSpec(shape, lambda i: (0,) * len(shape))


def _sds(shape, dt):
    return jax.ShapeDtypeStruct(tuple(shape), dt)


def _dot(a, b):
    return jnp.dot(a.astype(BF), b.astype(BF), preferred_element_type=F32)


def _dot_nt(a, b):
    return lax.dot_general(a.astype(BF), b.astype(BF), (((1,), (1,)), ((), ())), preferred_element_type=F32)


def _dot_tn(a, b):
    return lax.dot_general(a.astype(BF), b.astype(BF), (((0,), (0,)), ((), ())), preferred_element_type=F32)


def _rstd(x):
    return lax.rsqrt(jnp.mean(x * x, axis=-1, keepdims=True) + EPS)


def _rms_bwd(x, g, dy):
    r = _rstd(x)
    z = dy * g
    dx = r * z - x * (r * r * r * jnp.mean(z * x, axis=-1, keepdims=True))
    return dx, jnp.sum(dy * x * r, axis=0, keepdims=True)


def _sig(x):
    return jax.nn.sigmoid(x)


def _fold8(x):
    out = x[0:8]
    for k in range(1, x.shape[0] // 8):
        out = out + x[8 * k:8 * k + 8]
    return out


def _shifted(win):
    return [win] + [pltpu.roll(win, 2 * CHUNK - rho, 0) for rho in range(1, 8)]


def _tap(phases, o):
    return phases[o % 8][8 * (o // 8):8 * (o // 8) + CHUNK]


def _init(ref, first):
    @pl.when(first)
    def _():
        ref[...] = jnp.zeros_like(ref)


def _my_index():
    return 4 * lax.axis_index("x") + 2 * lax.axis_index("y") + lax.axis_index("c")


def _coords(idx):
    return (idx // 4, (idx // 2) % 2, idx % 2)


def _xch_shapes(xch):
    return [_sds((NDEV,) + ((a.shape[0] // NDEV, a.shape[1]) if k == "rows" else tuple(a.shape[1:])), a.dtype) for a, k in xch]


def _xch_scratch(n):
    return [pltpu.SemaphoreType.DMA((n, NDEV - 1)), pltpu.SemaphoreType.DMA((n, NDEV - 1)), pltpu.SemaphoreType.DMA((n,))]


def _xch_copies(kinds, srcs, outs, sems, arrivals):
    send_sems, recv_sems, local_sems = sems
    me = _my_index()

    def piece(a, p):
        if kinds[a] == "rows":
            r = srcs[a].shape[0] // NDEV
            return srcs[a].at[pl.ds(p * r, r), :]
        return srcs[a].at[p]

    def remote(a, k, src, slot):
        return pltpu.make_async_remote_copy(
            src_ref=src, dst_ref=outs[a].at[slot], send_sem=send_sems.at[a, k - 1], recv_sem=recv_sems.at[a, k - 1],
            device_id=_coords(me ^ k), device_id_type=MESH)

    n = len(kinds)
    local = [pltpu.make_async_copy(piece(a, me), outs[a].at[me], local_sems.at[a]) for a in range(n)]
    sends = [remote(a, k, piece(a, me ^ k), me) for a in range(n) for k in range(1, NDEV)]
    recvs = [remote(a, k, piece(a, me), me ^ k) for a in range(n) for k in range(1, NDEV)] if arrivals else []
    return local, sends, recvs


def _xch_start(kinds, srcs, outs, sems):
    local, sends, _ = _xch_copies(kinds, srcs, outs, sems, False)
    for cp in local + sends:
        cp.start()


def _xch_wait(kinds, srcs, outs, sems):
    local, sends, recvs = _xch_copies(kinds, srcs, outs, sems, True)
    for cp in recvs:
        cp.wait_recv()
    for cp in sends:
        cp.wait_send()
    for cp in local:
        cp.wait()


def _gat_copies(srcs, outs, sems):
    send_sems, recv_sems, local_sems = sems
    x, y, c = lax.axis_index("x"), lax.axis_index("y"), lax.axis_index("c")
    me, sibling = (x, y, c), (x, y, 1 - c)
    chips = [(1 - x, y), (x, 1 - y), (1 - x, 1 - y)]

    def copy(a, k, owner, to, from_block=False):
        slot = outs[a].at[4 * owner[0] + 2 * owner[1] + owner[2]]
        return pltpu.make_async_remote_copy(
            src_ref=srcs[a] if from_block else slot, dst_ref=slot, send_sem=send_sems.at[a, k], recv_sem=recv_sems.at[a, k],
            device_id=to, device_id_type=MESH)

    n = len(srcs)
    local = lambda: [pltpu.make_async_copy(srcs[a], outs[a].at[4 * x + 2 * y + c], local_sems.at[a]) for a in range(n)]
    first = lambda: [cp for a in range(n) for cp in
                     [copy(a, 0, me, sibling, True)] + [copy(a, 1 + j, me, (*chip, c), True) for j, chip in enumerate(chips)]]
    landed = lambda: [copy(a, 1 + j, (*chip, c), me) for a in range(n) for j, chip in enumerate(chips)]
    passed = lambda: [copy(a, 4 + j, (*chip, c), sibling) for a in range(n) for j, chip in enumerate(chips)]
    final = lambda: [cp for a in range(n) for cp in
                     [copy(a, 0, sibling, me)] + [copy(a, 4 + j, (*chip, 1 - c), me) for j, chip in enumerate(chips)]]
    return local, first, landed, passed, final


def _gat_start(srcs, outs, sems):
    local, first, _, _, _ = _gat_copies(srcs, outs, sems)
    for cp in local() + first():
        cp.start()


def _gat_forward(srcs, outs, sems):
    _, _, landed, passed, _ = _gat_copies(srcs, outs, sems)
    for got, on in zip(landed(), passed()):
        got.wait_recv()
        on.start()


def _gat_wait(srcs, outs, sems):
    local, first, _, passed, final = _gat_copies(srcs, outs, sems)
    for cp in final():
        cp.wait_recv()
    for cp in first() + passed():
        cp.wait_send()
    for cp in local():
        cp.wait()


def _call(body, name, grid, in_specs, out_specs, out_shape, args, scratch=(), xch=(), gat=()):
    n_in, n_out, n_x, n_g, n_s = len(in_specs), len(out_specs), len(xch), len(gat), len(scratch)
    kinds = [k for _, k in xch]
    total = 1
    for g in grid:
        total *= g

    def wrapped(*refs):
        ins, refs = refs[:n_in], refs[n_in:]
        x_src, refs = refs[:n_x], refs[n_x:]
        g_src, refs = refs[:n_g], refs[n_g:]
        outs, refs = refs[:n_out], refs[n_out:]
        x_out, refs = refs[:n_x], refs[n_x:]
        g_out, refs = refs[:n_g], refs[n_g:]
        own, refs = refs[:n_s], refs[n_s:]
        x_sems, g_sems = (refs[:3], refs[3:]) if n_x else ((), refs)
        step = pl.program_id(0)
        for d in range(1, len(grid)):
            step = step * grid[d] + pl.program_id(d)
        if n_x or n_g:
            @pl.when(step == 0)
            def _():
                if n_x:
                    _xch_start(kinds, x_src, x_out, x_sems)
                if n_g:
                    _gat_start(g_src, g_out, g_sems)

        body(*ins, *outs, *own)
        if n_g:
            @pl.when(step == max(total - 2, 0))
            def _():
                _gat_forward(g_src, g_out, g_sems)

        if n_x or n_g:
            @pl.when(step == total - 1)
            def _():
                if n_x:
                    _xch_wait(kinds, x_src, x_out, x_sems)
                if n_g:
                    _gat_wait(g_src, g_out, g_sems)

    any_spec = pl.BlockSpec(memory_space=pl.ANY)
    g_shapes = [_sds((NDEV,) + tuple(a.shape), a.dtype) for a in gat]
    res = pl.pallas_call(
        wrapped, name=name, grid=grid, in_specs=list(in_specs) + [any_spec] * (n_x + n_g),
        out_specs=list(out_specs) + [any_spec] * (n_x + n_g), out_shape=list(out_shape) + _xch_shapes(xch) + g_shapes,
        scratch_shapes=list(scratch) + (_xch_scratch(n_x) if n_x else []) + (_xch_scratch(n_g) if n_g else []),
        compiler_params=_cp(len(grid)),
    )(*args, *[a for a, _ in xch], *gat)
    return res[:n_out], res[n_out:n_out + n_x], res[n_out + n_x:]


def exchange_grads(xch):
    kinds = [k for _, k in xch]
    n = len(xch)

    def body(*refs):
        _xch_start(kinds, refs[:n], refs[n:2 * n], refs[2 * n:])
        _xch_wait(kinds, refs[:n], refs[n:2 * n], refs[2 * n:])

    any_spec = pl.BlockSpec(memory_space=pl.ANY)
    return pl.pallas_call(
        body, name="exchange_grads", out_shape=_xch_shapes(xch), in_specs=[any_spec] * n, out_specs=[any_spec] * n,
        scratch_shapes=_xch_scratch(n),
    )(*[a for a, _ in xch])


def embed(x, meta8, lp):
    bl, seq, _ = x.shape
    c8 = D // NDEV
    cb = 2 * c8

    def body(x_ref, m_ref, h_ref):
        h_ref[0:NMETA, :] = jnp.concatenate([m_ref[0], m_ref[1]], axis=1)
        h_ref[NMETA:NMETA + seq, :] = x_ref[...]
        h_ref[NMETA + seq:, :] = jnp.zeros((lp - NMETA - seq, cb), F32)

    return pl.pallas_call(
        body, name="embed", grid=(bl, D // cb),
        in_specs=[pl.BlockSpec((None, seq, cb), lambda b, c: (b, 0, c)), pl.BlockSpec((2, NMETA, c8), lambda b, c: (c, 0, 0))],
        out_specs=pl.BlockSpec((None, lp, cb), lambda b, c: (b, 0, c)), out_shape=_sds((bl, lp, D), F32),
        compiler_params=_cp(2),
    )(x, meta8)


def conv_in_fwd(h, nm, l, w_in, b_in, i, tm):
    t = h.shape[0]

    def body(h_ref, g_ref, w_ref, b_ref, u_ref, big_ref, a_ref):
        x = h_ref[...]
        ub = (x * _rstd(x) * g_ref[...]).astype(BF)
        u_ref[...] = ub
        big = jnp.dot(ub, w_ref[...], preferred_element_type=F32) + b_ref[...]
        big_ref[...] = big
        a_ref[...] = big[:, :D] * _sig(big[:, D:])

    return pl.pallas_call(
        body, name=f"conv_in_fwd{i}", grid=(t // tm,),
        in_specs=[_row(tm, D), _lay(l, (1, D)), _res((D, 2 * D)), _lay(i, (1, 2 * D))],
        out_specs=[_row(tm, D), _row(tm, 2 * D), _row(tm, D)],
        out_shape=[_sds((t, D), BF), _sds((t, 2 * D), F32), _sds((t, D), F32)],
        compiler_params=_cp(1),
    )(h, nm, w_in, b_in)


def _prev_halo(tm):
    return pl.BlockSpec((HALO, D), lambda i: (jnp.maximum(i * (tm // HALO) - 1, 0), 0))


def _next_halo(tm, t):
    return pl.BlockSpec((HALO, D), lambda i: (jnp.minimum((i + 1) * (tm // HALO), t // HALO - 1), 0))


def conv_mid_fwd(a, dw, ln_g, ln_b, i, tm, tpb, gat):
    t = a.shape[0]

    def body(a_ref, halo_ref, dw_ref, g_ref, b_ref, c_ref, s_ref, ext):
        first = pl.program_id(0) % tpb == 0
        ext[0:HALO] = jnp.where(first, 0.0, halo_ref[...])
        ext[HALO:] = a_ref[...]

        def chunk(k, carry):
            r0 = pl.multiple_of(k * CHUNK, CHUNK)
            win = _shifted(ext[pl.ds(r0, 2 * CHUNK), :])
            c = jnp.zeros((CHUNK, D), F32)
            for j in range(CW):
                c = c + dw_ref[j:j + 1, :] * _tap(win, j + 2)
            c_ref[pl.ds(r0, CHUNK), :] = c
            mu = jnp.mean(c, axis=-1, keepdims=True)
            xc = c - mu
            n = xc * lax.rsqrt(jnp.mean(xc * xc, axis=-1, keepdims=True) + EPS) * g_ref[...] + b_ref[...]
            s_ref[pl.ds(r0, CHUNK), :] = (n * _sig(n)).astype(BF)
            return carry

        lax.fori_loop(0, tm // CHUNK, chunk, 0)

    return _call(
        body, f"conv_mid_fwd{i}", (t // tm,),
        [_row(tm, D), _prev_halo(tm), _lay(i, (CW, D)), _lay(i, (1, D)), _lay(i, (1, D))],
        [_row(tm, D), _row(tm, D)], [_sds((t, D), F32), _sds((t, D), BF)], (a, a, dw, ln_g, ln_b),
        scratch=[pltpu.VMEM((tm + HALO, D), F32)], gat=gat)


def mixer_out_fwd(h, s, w, lw, bias, nf, l, tm, name):
    t = h.shape[0]

    def body(*refs):
        if bias is None:
            h_ref, s_ref, w_ref, g_ref, h1_ref, u_ref = refs
            y = 0.0
        else:
            h_ref, s_ref, w_ref, b_ref, g_ref, h1_ref, u_ref = refs
            y = b_ref[...]
        h1 = h_ref[...] + (jnp.dot(s_ref[...], w_ref[...], preferred_element_type=F32) + y)
        h1_ref[...] = h1
        u_ref[...] = (h1 * _rstd(h1) * g_ref[...]).astype(BF)

    ins = [h, s, w] + ([] if bias is None else [bias]) + [nf]
    specs = [_row(tm, D), _row(tm, D), _res((D, D))] + ([] if bias is None else [_lay(lw, (1, D))]) + [_lay(l, (1, D))]
    return pl.pallas_call(
        body, name=name, grid=(t // tm,), in_specs=specs,
        out_specs=[_row(tm, D), _row(tm, D)], out_shape=[_sds((t, D), F32), _sds((t, D), BF)],
        compiler_params=_cp(1),
    )(*ins)


def ffn_up_fwd(u, wg, wu, l, tm, gat):
    t = u.shape[0]

    def body(u_ref, wg_ref, wu_ref, g_ref, up_ref, hid_ref):
        ub = u_ref[...]
        g = jnp.dot(ub, wg_ref[...], preferred_element_type=F32)
        up = jnp.dot(ub, wu_ref[...], preferred_element_type=F32)
        g_ref[...] = g.astype(BF)
        up_ref[...] = up.astype(BF)
        hid_ref[...] = (g * _sig(g) * up).astype(BF)

    return _call(
        body, f"ffn_up_fwd{l}", (t // tm,), [_row(tm, D), _res((D, DFF)), _res((D, DFF))],
        [_row(tm, DFF)] * 3, [_sds((t, DFF), BF)] * 3, (u, wg, wu), gat=gat)


def ffn_down_fwd(hid, h1, wd, l, tm):
    t = h1.shape[0]

    def body(hid_ref, h1_ref, w_ref, h2_ref):
        h2_ref[...] = h1_ref[...] + jnp.dot(hid_ref[...], w_ref[...], preferred_element_type=F32)

    return pl.pallas_call(
        body, name=f"ffn_down_fwd{l}", grid=(t // tm,),
        in_specs=[_row(tm, DFF), _row(tm, D), _res((DFF, D))],
        out_specs=_row(tm, D), out_shape=_sds((t, D), F32),
        compiler_params=_cp(1),
    )(hid, h1, wd)


def _seg_rms(x, g, nseg):
    outs = []
    for s in range(nseg):
        xs = x[:, HD * s:HD * s + HD]
        outs.append(xs * _rstd(xs) * g)
    return jnp.concatenate(outs, axis=1)


def kv_fwd(h, kvn, w_kv, kng, tm):
    t = h.shape[0]

    def body(h_ref, g_ref, w_ref, kg_ref, kn_ref, kv_ref, k_ref, v_ref):
        x = h_ref[...]
        kn = (x * _rstd(x) * g_ref[...]).astype(BF)
        kn_ref[...] = kn
        kv = jnp.dot(kn, w_ref[...], preferred_element_type=F32)
        kv_ref[...] = kv
        k_ref[...] = _seg_rms(kv[:, :KVD], kg_ref[...], NKV).astype(BF)
        v_ref[...] = kv[:, KVD:].astype(BF)

    return pl.pallas_call(
        body, name="kv_fwd", grid=(t // tm,),
        in_specs=[_row(tm, D), _res((1, D)), _res((D, 2 * KVD)), _res((1, HD))],
        out_specs=[_row(tm, D), _row(tm, 2 * KVD), _row(tm, KVD), _row(tm, KVD)],
        out_shape=[_sds((t, D), BF), _sds((t, 2 * KVD), F32), _sds((t, KVD), BF), _sds((t, KVD), BF)],
        compiler_params=_cp(1),
    )(h, kvn, w_kv, kng)


def q_fwd(h, nm, l, w_q, j, tm):
    t = h.shape[0]

    def body(h_ref, g_ref, w_ref, u_ref, q_ref):
        x = h_ref[...]
        ub = (x * _rstd(x) * g_ref[...]).astype(BF)
        u_ref[...] = ub
        q_ref[...] = jnp.dot(ub, w_ref[...], preferred_element_type=F32)

    return pl.pallas_call(
        body, name=f"q_fwd{j}", grid=(t // tm,),
        in_specs=[_row(tm, D), _lay(l, (1, D)), _res((D, D))],
        out_specs=[_row(tm, D), _row(tm, D)], out_shape=[_sds((t, D), BF), _sds((t, D), F32)],
        compiler_params=_cp(1),
    )(h, nm, w_q)


RQ = NH // NKV


def _attn_specs(nb, lp):
    cur = lambda c: pl.BlockSpec((QB, c), lambda b, n: (b * nb + n, 0))
    seq = pl.BlockSpec((None, lp, KVD), lambda b, n: (b, 0, 0))
    return cur, seq


def _attn_masks(n, start):
    rows = RQ * QB
    qpos = n * QB + (lax.broadcasted_iota(jnp.int32, (rows, 2 * QB), 0) & (QB - 1))
    kpos = start + lax.broadcasted_iota(jnp.int32, (rows, 2 * QB), 1)
    band = (kpos <= qpos) & (qpos - kpos < QB) & (kpos >= NMETA)
    qm = n * QB + (lax.broadcasted_iota(jnp.int32, (rows, NMETA), 0) & (QB - 1))
    meta = lax.broadcasted_iota(jnp.int32, (rows, NMETA), 1) <= qm
    return band, meta


def _stack_heads(ref, g, fn):
    return jnp.concatenate([fn(ref[:, HD * (g * RQ + r):HD * (g * RQ + r) + HD]) for r in range(RQ)], axis=0)


def _stack_cols(ref, g):
    return jnp.concatenate([ref[:, g * RQ + r:g * RQ + r + 1] for r in range(RQ)], axis=0)


def _stack_sinks(sk_ref, g):
    return jnp.concatenate([jnp.broadcast_to(sk_ref[:, g * RQ + r:g * RQ + r + 1], (QB, 1)) for r in range(RQ)], axis=0)


def attn_fwd(q, k, v, qg, sinks, j, bl, lp, gat):
    t = q.shape[0]
    nb = lp // QB
    cur, seq = _attn_specs(nb, lp)

    def body(q_ref, k_ref, v_ref, qg_ref, sk_ref, o_ref, lse_ref):
        n = pl.program_id(1)
        start = pl.multiple_of(jnp.maximum(n - 1, 0) * QB, QB)
        m_band, m_meta = _attn_masks(n, start)
        band = pl.ds(start, 2 * QB)
        lane = lax.broadcasted_iota(jnp.int32, (QB, NH), 1)
        lse = jnp.zeros((QB, NH), F32)
        for g in range(NKV):
            gs = slice(HD * g, HD * g + HD)
            qn = _stack_heads(q_ref, g, lambda x: (x * _rstd(x) * qg_ref[...]).astype(BF))
            sink = _stack_sinks(sk_ref, g)
            s_b = jnp.where(m_band, _dot_nt(qn, k_ref[band, gs]) * SCALE, NEG)
            s_m = jnp.where(m_meta, _dot_nt(qn, k_ref[0:NMETA, gs]) * SCALE, NEG)
            mx = jnp.maximum(jnp.maximum(jnp.max(s_b, -1, keepdims=True), jnp.max(s_m, -1, keepdims=True)), sink)
            p_b, p_m = jnp.exp(s_b - mx), jnp.exp(s_m - mx)
            den = jnp.sum(p_b, -1, keepdims=True) + jnp.sum(p_m, -1, keepdims=True) + jnp.exp(sink - mx)
            inv = 1.0 / den
            o = _dot(p_b * inv, v_ref[band, gs]) + _dot(p_m * inv, v_ref[0:NMETA, gs])
            l = mx + jnp.log(den)
            for r in range(RQ):
                h = g * RQ + r
                o_ref[:, HD * h:HD * h + HD] = o[r * QB:(r + 1) * QB].astype(BF)
                lse = jnp.where(lane == h, l[r * QB:(r + 1) * QB], lse)
        lse_ref[...] = lse

    return _call(
        body, f"attn_fwd{j}", (bl, nb),
        [cur(D), seq, seq, pl.BlockSpec((None, 1, HD), lambda b, n: (j, 0, 0)), pl.BlockSpec((None, 1, NH), lambda b, n: (j, 0, 0))],
        [cur(D), cur(NH)], [_sds((t, D), BF), _sds((t, NH), F32)], (q, k, v, qg, sinks), gat=gat)


def loss_fwd(h, tgt):
    bl, lp, _ = h.shape
    seq = tgt.shape[1]
    cb = 256

    def body(h_ref, t_ref, dh_ref, loss_ref):
        _init(loss_ref, (pl.program_id(0) == 0) & (pl.program_id(1) == 0))
        err = h_ref[NMETA:NMETA + seq, :] - t_ref[...]
        dh_ref[...] = jnp.zeros_like(dh_ref)
        dh_ref[NMETA:NMETA + seq, :] = err * (1.0 / D)
        loss_ref[...] += (0.5 / D) * jnp.sum(err * err)

    return pl.pallas_call(
        body, name="loss_fwd", grid=(bl, D // cb),
        in_specs=[pl.BlockSpec((None, lp, cb), lambda b, c: (b, 0, c)), pl.BlockSpec((None, seq, cb), lambda b, c: (b, 0, c))],
        out_specs=[pl.BlockSpec((None, lp, cb), lambda b, c: (b, 0, c)), pl.BlockSpec((8, 128), lambda b, c: (0, 0))],
        out_shape=[_sds((bl, lp, D), F32), _sds((8, 128), F32)],
        compiler_params=_cp(2),
    )(h, tgt)


def ffn_bwd_x(dh2, g, up, h1, nf, l, wd, wg, wu, tm, xch):
    t = dh2.shape[0]

    def body(dh2_ref, g_ref, up_ref, h1_ref, nf_ref, wd_ref, wg_ref, wu_ref, dg_ref, du_ref, dh1_ref, dnf_ref):
        _init(dnf_ref, pl.program_id(0) == 0)
        dh2v = dh2_ref[...]
        dhid = _dot_nt(dh2v, wd_ref[...])
        gv = g_ref[...].astype(F32)
        uv = up_ref[...].astype(F32)
        sg = _sig(gv)
        dgv = (dhid * uv * (sg * (1.0 + gv * (1.0 - sg)))).astype(BF)
        duv = (dhid * (gv * sg)).astype(BF)
        dg_ref[...] = dgv
        du_ref[...] = duv
        dnorm = _dot_nt(dgv, wg_ref[...]) + _dot_nt(duv, wu_ref[...])
        dx, dnf = _rms_bwd(h1_ref[...], nf_ref[...], dnorm)
        dh1_ref[...] = dh2v + dx
        dnf_ref[...] += dnf

    return _call(
        body, f"ffn_bwd_x{l}", (t // tm,),
        [_row(tm, D), _row(tm, DFF), _row(tm, DFF), _row(tm, D), _lay(l, (1, D)),
         _res((DFF, D)), _res((D, DFF)), _res((D, DFF))],
        [_row(tm, DFF), _row(tm, DFF), _row(tm, D), _acc((1, D))],
        [_sds((t, DFF), BF), _sds((t, DFF), BF), _sds((t, D), F32), _sds((1, D), F32)],
        (dh2, g, up, h1, nf, wd, wg, wu), xch=xch)


def mm_tn(x, dy, tm, name, split=False):
    t, kk = x.shape
    nn = dy.shape[1]
    n8 = nn // NDEV
    nsteps = t // tm

    def body(x_ref, dy_ref, o_ref, acc):
        i = pl.program_id(0)
        _init(acc, i == 0)
        acc[...] += _dot_tn(x_ref[...], dy_ref[...])

        @pl.when(i == nsteps - 1)
        def _():
            if split:
                for p in range(NDEV):
                    o_ref[p] = acc[:, p * n8:(p + 1) * n8].astype(BF)
            else:
                o_ref[...] = acc[...].astype(BF)

    oshape = (NDEV, kk, n8) if split else (kk, nn)
    return pl.pallas_call(
        body, name=name, grid=(nsteps,), in_specs=[_row(tm, kk), _row(tm, nn)],
        out_specs=_acc(oshape), out_shape=_sds(oshape, BF), scratch_shapes=[pltpu.VMEM((kk, nn), F32)],
        compiler_params=_cp(1),
    )(x, dy)


def proj_bwd(dy, w, h, g, lg, dh_in, tm, name):
    t = h.shape[0]
    nn = dy.shape[1]
    wspec = _res(w.shape)
    gspec = _res((1, D)) if lg is None else _lay(lg, (1, D))

    def body(dy_ref, w_ref, h_ref, g_ref, dhin_ref, dh_ref, dg_ref):
        _init(dg_ref, pl.program_id(0) == 0)
        du = _dot_nt(dy_ref[...], w_ref[...])
        dx, dg = _rms_bwd(h_ref[...], g_ref[...], du)
        dh_ref[...] = dhin_ref[...] + dx
        dg_ref[...] += dg

    return pl.pallas_call(
        body, name=name, grid=(t // tm,),
        in_specs=[_row(tm, nn), wspec, _row(tm, D), gspec, _row(tm, D)],
        out_specs=[_row(tm, D), _acc((1, D))], out_shape=[_sds((t, D), F32), _sds((1, D), F32)],
        compiler_params=_cp(1),
    )(dy, w, h, g, dh_in)


def out_proj_bwd(dh1, w, tm, name):
    t = dh1.shape[0]

    def body(dh1_ref, w_ref, do_ref):
        do_ref[...] = _dot_nt(dh1_ref[...], w_ref[...]).astype(BF)

    return pl.pallas_call(
        body, name=name, grid=(t // tm,), in_specs=[_row(tm, D), _res((D, D))],
        out_specs=_row(tm, D), out_shape=_sds((t, D), BF), compiler_params=_cp(1),
    )(dh1, w)


def attn_bwd(q, k, v, do, o, lse, qg, sinks, j, bl, lp, xch):
    t = q.shape[0]
    nb = lp // QB
    cur, seq = _attn_specs(nb, lp)

    def body(q_ref, k_ref, v_ref, do_ref, o_ref, lse_ref, qg_ref, sk_ref, dq_ref, dk_ref, dv_ref, dqg_ref, dsk_ref):
        b, n = pl.program_id(0), pl.program_id(1)
        _init(dk_ref, n == 0)
        _init(dv_ref, n == 0)
        _init(dqg_ref, (b == 0) & (n == 0))
        _init(dsk_ref, (b == 0) & (n == 0))
        start = pl.multiple_of(jnp.maximum(n - 1, 0) * QB, QB)
        m_band, m_meta = _attn_masks(n, start)
        band = pl.ds(start, 2 * QB)
        lane = lax.broadcasted_iota(jnp.int32, (1, NH), 1)
        dqg = jnp.zeros((1, HD), F32)
        dsk = jnp.zeros((1, NH), F32)
        dk_b, dk_m, dv_b, dv_m = [], [], [], []
        for g in range(NKV):
            gs = slice(HD * g, HD * g + HD)
            kb, km, vb, vm = k_ref[band, gs], k_ref[0:NMETA, gs], v_ref[band, gs], v_ref[0:NMETA, gs]
            qh = _stack_heads(q_ref, g, lambda x: x)
            rs = _rstd(qh)
            qn = (qh * rs * qg_ref[...]).astype(BF)
            ls = _stack_cols(lse_ref, g)
            p_b = jnp.where(m_band, jnp.exp(_dot_nt(qn, kb) * SCALE - ls), 0.0)
            p_m = jnp.where(m_meta, jnp.exp(_dot_nt(qn, km) * SCALE - ls), 0.0)
            doh = _stack_heads(do_ref, g, lambda x: x)
            delta = jnp.sum(doh.astype(F32) * _stack_heads(o_ref, g, lambda x: x).astype(F32), axis=-1, keepdims=True)
            ds_b = (p_b * (_dot_nt(doh, vb) - delta)).astype(BF)
            ds_m = (p_m * (_dot_nt(doh, vm) - delta)).astype(BF)
            dqn = (_dot(ds_b, kb) + _dot(ds_m, km)) * SCALE
            dk_b.append(_dot_tn(qn, ds_b).T * SCALE)
            dk_m.append(_dot_tn(qn, ds_m).T * SCALE)
            dv_b.append(_dot_tn(doh, p_b).T)
            dv_m.append(_dot_tn(doh, p_m).T)
            dsink = jnp.exp(_stack_sinks(sk_ref, g) - ls) * delta
            z = dqn * qg_ref[...]
            dq = rs * z - qh * (rs * rs * rs * jnp.mean(z * qh, axis=-1, keepdims=True))
            dqg = dqg + jnp.sum(dqn * qh * rs, axis=0, keepdims=True)
            for r in range(RQ):
                h = g * RQ + r
                dq_ref[:, HD * h:HD * h + HD] = dq[r * QB:(r + 1) * QB]
                dsk = dsk + jnp.where(lane == h, -jnp.sum(dsink[r * QB:(r + 1) * QB]), 0.0)
        cat = lambda xs: jnp.concatenate(xs, axis=1)
        dk_ref[band, :] += cat(dk_b)
        dv_ref[band, :] += cat(dv_b)
        dk_ref[0:NMETA, :] += cat(dk_m)
        dv_ref[0:NMETA, :] += cat(dv_m)
        dqg_ref[...] += dqg
        dsk_ref[...] += dsk

    return _call(
        body, f"attn_bwd{j}", (bl, nb),
        [cur(D), seq, seq, cur(D), cur(D), cur(NH),
         pl.BlockSpec((None, 1, HD), lambda b, n: (j, 0, 0)), pl.BlockSpec((None, 1, NH), lambda b, n: (j, 0, 0))],
        [cur(D), seq, seq, pl.BlockSpec((1, HD), lambda b, n: (0, 0)), pl.BlockSpec((1, NH), lambda b, n: (0, 0))],
        [_sds((t, D), F32), _sds((bl, lp, KVD), F32), _sds((bl, lp, KVD), F32), _sds((1, HD), F32), _sds((1, NH), F32)],
        (q, k, v, do, o, lse, qg, sinks), xch=xch)


def kv_bwd_pre(dk0, dk1, dv0, dv1, kv, kng, tm):
    t = kv.shape[0]

    def body(dk0_ref, dk1_ref, dv0_ref, dv1_ref, kv_ref, g_ref, dkv_ref, dg_ref):
        _init(dg_ref, pl.program_id(0) == 0)
        dk = dk0_ref[...] + dk1_ref[...]
        dg = jnp.zeros((1, HD), F32)
        outs = []
        for s in range(NKV):
            sl = slice(HD * s, HD * s + HD)
            dx, dgs = _rms_bwd(kv_ref[:, sl], g_ref[...], dk[:, sl])
            outs.append(dx)
            dg = dg + dgs
        dkv_ref[:, :KVD] = jnp.concatenate(outs, axis=1).astype(BF)
        dkv_ref[:, KVD:] = (dv0_ref[...] + dv1_ref[...]).astype(BF)
        dg_ref[...] += dg

    return pl.pallas_call(
        body, name="kv_bwd_pre", grid=(t // tm,),
        in_specs=[_row(tm, KVD)] * 4 + [_row(tm, 2 * KVD), _res((1, HD))],
        out_specs=[_row(tm, 2 * KVD), _acc((1, HD))], out_shape=[_sds((t, 2 * KVD), BF), _sds((1, HD), F32)],
        compiler_params=_cp(1),
    )(dk0, dk1, dv0, dv1, kv, kng)


def conv_out_bwd(dh1, c, ln_g, ln_b, w_out, i, tm):
    t = dh1.shape[0]

    def body(dh1_ref, c_ref, g_ref, b_ref, w_ref, dc_ref, dg_ref, db_ref, dbo_ref):
        first = pl.program_id(0) == 0
        _init(dg_ref, first)
        _init(db_ref, first)
        _init(dbo_ref, first)
        dh1v = dh1_ref[...]
        ds = _dot_nt(dh1v, w_ref[...])
        cv = c_ref[...]
        xc = cv - jnp.mean(cv, axis=-1, keepdims=True)
        rstd = lax.rsqrt(jnp.mean(xc * xc, axis=-1, keepdims=True) + EPS)
        xh = xc * rstd
        n = xh * g_ref[...] + b_ref[...]
        sg = _sig(n)
        dn = ds * (sg * (1.0 + n * (1.0 - sg)))
        dxh = dn * g_ref[...]
        dc_ref[...] = rstd * (dxh - jnp.mean(dxh, axis=-1, keepdims=True) - xh * jnp.mean(dxh * xh, axis=-1, keepdims=True))
        dg_ref[...] += jnp.sum(dn * xh, axis=0, keepdims=True)
        db_ref[...] += jnp.sum(dn, axis=0, keepdims=True)
        dbo_ref[...] += jnp.sum(dh1v, axis=0, keepdims=True)

    return pl.pallas_call(
        body, name=f"conv_out_bwd{i}", grid=(t // tm,),
        in_specs=[_row(tm, D), _row(tm, D), _lay(i, (1, D)), _lay(i, (1, D)), _res((D, D))],
        out_specs=[_row(tm, D), _acc((1, D)), _acc((1, D)), _acc((1, D))],
        out_shape=[_sds((t, D), F32)] + [_sds((1, D), F32)] * 3,
        compiler_params=_cp(1),
    )(dh1, c, ln_g, ln_b, w_out)


def conv_mid_bwd(dc, a, big, dw, i, tm, tpb, xch):
    t = dc.shape[0]
    nsteps = t // tm

    def body(dc_ref, nxt_ref, a_ref, prv_ref, big_ref, dw_ref, da_ref, dbin_ref, ddw_ref, dce, ae, wacc, bacc):
        i_ = pl.program_id(0)
        _init(wacc, i_ == 0)
        _init(bacc, i_ == 0)
        dce[0:tm] = dc_ref[...]
        dce[tm:] = jnp.where(i_ % tpb == tpb - 1, 0.0, nxt_ref[...])
        ae[0:HALO] = jnp.where(i_ % tpb == 0, 0.0, prv_ref[...])
        ae[HALO:] = a_ref[...]

        def chunk(k, carry):
            r0 = pl.multiple_of(k * CHUNK, CHUNK)
            wdc = _shifted(dce[pl.ds(r0, 2 * CHUNK), :])
            wa = _shifted(ae[pl.ds(r0, 2 * CHUNK), :])
            dcc = wdc[0][0:CHUNK]
            da = jnp.zeros((CHUNK, D), F32)
            for j in range(CW):
                da = da + dw_ref[j:j + 1, :] * _tap(wdc, CW - 1 - j)
                wacc[j] += _fold8(dcc * _tap(wa, j + 2))
            bv = big_ref[pl.ds(r0, CHUNK), :]
            a1, sg = bv[:, :D], _sig(bv[:, D:])
            d1 = da * sg
            d2 = da * a1 * sg * (1.0 - sg)
            da_ref[pl.ds(r0, CHUNK), 0:D] = d1.astype(BF)
            da_ref[pl.ds(r0, CHUNK), D:2 * D] = d2.astype(BF)
            bacc[:, 0:D] += _fold8(d1)
            bacc[:, D:2 * D] += _fold8(d2)
            return carry

        lax.fori_loop(0, tm // CHUNK, chunk, 0)

        @pl.when(i_ == nsteps - 1)
        def _():
            dbin_ref[...] = jnp.sum(bacc[...], axis=0, keepdims=True)
            ddw_ref[...] = jnp.sum(wacc[...], axis=1)

    return _call(
        body, f"conv_mid_bwd{i}", (nsteps,),
        [_row(tm, D), _next_halo(tm, t), _row(tm, D), _prev_halo(tm), _row(tm, 2 * D), _lay(i, (CW, D))],
        [_row(tm, 2 * D), _acc((1, 2 * D)), _acc((CW + 1, D))],
        [_sds((t, 2 * D), BF), _sds((1, 2 * D), F32), _sds((CW + 1, D), F32)],
        (dc, dc, a, a, big, dw),
        scratch=[pltpu.VMEM((tm + HALO, D), F32), pltpu.VMEM((tm + HALO, D), F32),
                 pltpu.VMEM((CW + 1, 8, D), F32), pltpu.VMEM((8, 2 * D), F32)], xch=xch)


def input_grads(dh0, seq):
    bl, lp, _ = dh0.shape
    cb = 256

    def body(dh_ref, gx_ref, gm_ref):
        _init(gm_ref, pl.program_id(1) == 0)
        gx_ref[...] = dh_ref[NMETA:NMETA + seq, :]
        gm_ref[...] += dh_ref[0:NMETA, :]

    return pl.pallas_call(
        body, name="input_grads", grid=(D // cb, bl),
        in_specs=[pl.BlockSpec((None, lp, cb), lambda c, b: (b, 0, c))],
        out_specs=[pl.BlockSpec((None, seq, cb), lambda c, b: (b, 0, c)), pl.BlockSpec((NMETA, cb), lambda c, b: (0, c))],
        out_shape=[_sds((bl, seq, D), F32), _sds((NMETA, D), F32)],
        compiler_params=_cp(2),
    )(dh0)


GATHER_PLAN = {
    "conv_mid_fwd0": [("ffn_w_gate", 0), ("ffn_w_up", 0), ("ffn_w_down", 0)],
    "ffn_up_fwd0": [("conv_w_in", 1), ("conv_w_out", 1), ("ffn_w_gate", 1)],
    "conv_mid_fwd1": [("ffn_w_up", 1), ("ffn_w_down", 1), ("w_kv", 0), ("w_q", 0)],
    "ffn_up_fwd1": [("w_o", 0), ("ffn_w_down", 2)],
    "attn_fwd0": [("ffn_w_gate", 2), ("ffn_w_up", 2), ("w_q", 1), ("w_o", 1), ("ffn_w_gate", 3), ("ffn_w_up", 3), ("ffn_w_down", 3)],
}
BIG = {"conv_w_in": "pieces", "conv_w_out": "rows", "w_kv": "rows", "w_q": "rows", "w_o": "rows",
       "ffn_w_gate": "pieces", "ffn_w_up": "pieces", "ffn_w_down": "rows"}


def gathered_matrix(name, layer, blocks8):
    if BIG[name] == "rows":
        return blocks8.reshape(NDEV * blocks8.shape[1], blocks8.shape[2])
    return join_columns(blocks8, f"join_{name}{layer}")


def local_step(x, tgt, meta8, w, mats, shards):
    bl, seq, _ = x.shape
    lp = -(-(NMETA + seq) // QB) * QB
    tpb = 4
    tm = lp // tpb
    t = bl * lp
    na = 2
    flat = lambda a: a.reshape(t, D)
    mats = dict(mats)

    def riders(carrier):
        return [shards[key] for key in GATHER_PLAN[carrier]]

    def landed(carrier, blocks):
        for key, b8 in zip(GATHER_PLAN[carrier], blocks):
            mats[key] = gathered_matrix(*key, b8)

    h = flat(embed(x, meta8, lp))
    saved = []
    kvs = None
    for l in range(4):
        rec = {"h": h}
        if l < na:
            rec["u"], rec["big"], rec["a"] = conv_in_fwd(h, w["norm_mix"], l, mats["conv_w_in", l], w["conv_b_in"], l, tm)
            name = f"conv_mid_fwd{l}"
            (rec["c"], rec["s"]), _, got = conv_mid_fwd(rec["a"], w["conv_dw"], w["conv_ln_g"], w["conv_ln_b"], l, tm, tpb,
                                                         riders(name))
            landed(name, got)
            rec["h1"], rec["u2"] = mixer_out_fwd(h, rec["s"], mats["conv_w_out", l], l, w["conv_b_out"], w["norm_ffn"], l, tm,
                                                 f"conv_out_fwd{l}")
        else:
            j = l - na
            if kvs is None:
                kvs = dict(zip(("kn", "kv", "k", "v"), kv_fwd(h, w["kv_norm"], mats["w_kv", 0], w["k_norm"], tm)))
                kvs["h"] = h
                kvs["k3"], kvs["v3"] = kvs["k"].reshape(bl, lp, KVD), kvs["v"].reshape(bl, lp, KVD)
            rec["u"], rec["q"] = q_fwd(h, w["norm_mix"], l, mats["w_q", j], j, tm)
            name = f"attn_fwd{j}"
            (rec["o"], rec["lse"]), _, got = attn_fwd(rec["q"], kvs["k3"], kvs["v3"], w["q_norm"], w["attn_sinks"], j, bl, lp,
                                                      riders(name) if name in GATHER_PLAN else [])
            if name in GATHER_PLAN:
                landed(name, got)
            rec["h1"], rec["u2"] = mixer_out_fwd(h, rec["o"], mats["w_o", j], j, None, w["norm_ffn"], l, tm, f"attn_out_fwd{j}")
        name = f"ffn_up_fwd{l}"
        (rec["g"], rec["up"], rec["hid"]), _, got = ffn_up_fwd(rec["u2"], mats["ffn_w_gate", l], mats["ffn_w_up", l], l, tm // 2,
                                                              riders(name) if name in GATHER_PLAN else [])
        if name in GATHER_PLAN:
            landed(name, got)
        h = ffn_down_fwd(rec["hid"], rec["h1"], mats["ffn_w_down", l], l, tm)
        saved.append(rec)

    dh3, loss_blk = loss_fwd(h.reshape(bl, lp, D), tgt)
    dh = flat(dh3)

    big, small, arrived = {}, {}, {}
    dks, dvs = [], []
    pending = []

    def carried(names, arrivals):
        arrived.update(zip([nm for nm, _ in names], arrivals))

    for l in reversed(range(4)):
        rec = saved[l]
        riders, pending = pending, []
        (dg, du, dh1, small[f"norm_ffn{l}"]), got, _ = ffn_bwd_x(
            dh, rec["g"], rec["up"], rec["h1"], w["norm_ffn"], l, mats["ffn_w_down", l], mats["ffn_w_gate", l], mats["ffn_w_up", l], tm // 2,
            [(big[nm], kind) for nm, kind in riders])
        carried(riders, got)
        big[f"ffn_w_down{l}"] = mm_tn(rec["hid"], dh, 2 * tm, f"dw_down{l}")
        big[f"ffn_w_gate{l}"] = mm_tn(rec["u2"], dg, 2 * tm, f"dw_gate{l}", split=True)
        big[f"ffn_w_up{l}"] = mm_tn(rec["u2"], du, 2 * tm, f"dw_up{l}", split=True)
        riders = [(f"ffn_w_down{l}", "rows"), (f"ffn_w_gate{l}", "pieces"), (f"ffn_w_up{l}", "pieces")]
        xch = [(big[nm], kind) for nm, kind in riders]
        if l >= na:
            j = l - na
            do = out_proj_bwd(dh1, mats["w_o", j], tm, f"attn_out_bwd{j}")
            big[f"w_o{j}"] = mm_tn(rec["o"], dh1, 2 * tm, f"dw_o{j}")
            (dq, dk, dv, small[f"q_norm{j}"], small[f"attn_sinks{j}"]), got, _ = attn_bwd(
                rec["q"], kvs["k3"], kvs["v3"], do, rec["o"], rec["lse"], w["q_norm"], w["attn_sinks"], j, bl, lp, xch)
            carried(riders, got)
            dks.append(dk.reshape(t, KVD))
            dvs.append(dv.reshape(t, KVD))
            big[f"w_q{j}"] = mm_tn(rec["u"], dq, 2 * tm, f"dw_q{j}")
            dh, small[f"norm_mix{l}"] = proj_bwd(dq, mats["w_q", j], rec["h"], w["norm_mix"], l, dh1, tm, f"q_bwd{j}")
            pending = [(f"w_o{j}", "rows"), (f"w_q{j}", "rows")]
            if l == na:
                dkv, small["k_norm"] = kv_bwd_pre(dks[0], dks[1], dvs[0], dvs[1], kvs["kv"], w["k_norm"], tm)
                big["w_kv"] = mm_tn(kvs["kn"], dkv, 2 * tm, "dw_kv")
                dh, small["kv_norm"] = proj_bwd(dkv, mats["w_kv", 0], kvs["h"], w["kv_norm"], None, dh, tm, "kv_bwd")
                pending.append(("w_kv", "rows"))
        else:
            dc, small[f"conv_ln_g{l}"], small[f"conv_ln_b{l}"], small[f"conv_b_out{l}"] = conv_out_bwd(
                dh1, rec["c"], w["conv_ln_g"], w["conv_ln_b"], mats["conv_w_out", l], l, tm)
            big[f"conv_w_out{l}"] = mm_tn(rec["s"], dh1, 2 * tm, f"dw_conv_out{l}")
            (da, small[f"conv_b_in{l}"], ddw), got, _ = conv_mid_bwd(dc, rec["a"], rec["big"], w["conv_dw"], l, tm, tpb, xch)
            carried(riders, got)
            small[f"conv_dw{l}"] = ddw[:CW]
            big[f"conv_w_in{l}"] = mm_tn(rec["u"], da, 2 * tm, f"dw_conv_in{l}", split=True)
            dh, small[f"norm_mix{l}"] = proj_bwd(da, mats["conv_w_in", l], rec["h"], w["norm_mix"], l, dh1, tm, f"conv_in_bwd{l}")
            pending = [(f"conv_w_out{l}", "rows"), (f"conv_w_in{l}", "pieces")]
    carried(pending, exchange_grads([(big[nm], kind) for nm, kind in pending]))
    grad_x, small["meta_tokens"] = input_grads(dh.reshape(bl, lp, D), seq)
    return loss_blk, grad_x, big, arrived, small


def all_gather_blocks(blocks):
    n = len(blocks)

    def body(*refs):
        srcs, outs, sems = refs[:n], refs[n:2 * n], refs[2 * n:]
        _gat_start(srcs, outs, sems)
        _gat_forward(srcs, outs, sems)
        _gat_wait(srcs, outs, sems)

    any_spec = pl.BlockSpec(memory_space=pl.ANY)
    return pl.pallas_call(
        body, name="all_gather_blocks", out_shape=[_sds((NDEV,) + tuple(a.shape), a.dtype) for a in blocks],
        in_specs=[any_spec] * n, out_specs=[any_spec] * n, scratch_shapes=_xch_scratch(n),
    )(*blocks)


def all_reduce_small(buf):
    rows = buf.shape[0]

    def body(x_ref, o_ref, g_ref, send_sems, recv_sems):
        me = _my_index()
        sends = []
        for k in range(1, NDEV):
            peer = me ^ k
            cp = pltpu.make_async_remote_copy(
                src_ref=x_ref, dst_ref=g_ref.at[me], send_sem=send_sems.at[k - 1], recv_sem=recv_sems.at[k - 1],
                device_id=_coords(peer), device_id_type=MESH)
            cp.start()
            sends.append(cp)
        g_ref[me] = x_ref[...]
        for k in range(1, NDEV):
            peer = me ^ k
            pltpu.make_async_remote_copy(
                src_ref=x_ref, dst_ref=g_ref.at[peer], send_sem=send_sems.at[k - 1], recv_sem=recv_sems.at[k - 1],
                device_id=_coords(peer), device_id_type=MESH).wait_recv()
        for cp in sends:
            cp.wait_send()
        acc = g_ref[0]
        for p in range(1, NDEV):
            acc = acc + g_ref[p]
        o_ref[...] = acc

    return pl.pallas_call(
        body, name="all_reduce_small", out_shape=_sds((rows, D), F32),
        in_specs=[pl.BlockSpec(memory_space=pltpu.VMEM)], out_specs=pl.BlockSpec(memory_space=pltpu.VMEM),
        scratch_shapes=[pltpu.VMEM((NDEV, rows, D), F32), pltpu.SemaphoreType.DMA((7,)), pltpu.SemaphoreType.DMA((7,))],
    )(buf)


def cast_bf16(ws):
    n = len(ws)
    counts = [1 if x.ndim == 2 else x.shape[0] for x in ws]

    def body(*refs):
        outs = iter(refs[n:])
        for a in range(n):
            for l in range(counts[a]):
                next(outs)[...] = (refs[a][...] if ws[a].ndim == 2 else refs[a][l]).astype(BF)

    flat = pl.pallas_call(
        body, name="cast_bf16", out_shape=[_sds(x.shape[-2:], BF) for x, k in zip(ws, counts) for _ in range(k)],
        compiler_params=pltpu.CompilerParams(vmem_limit_bytes=VMEM_LIMIT),
    )(*ws)
    it = iter(flat)
    return [[next(it) for _ in range(k)] for k in counts]


def join_columns(w8, name):
    _, kk, n8 = w8.shape

    def body(x_ref, o_ref):
        o_ref[...] = jnp.concatenate([x_ref[p] for p in range(NDEV)], axis=1)

    return pl.pallas_call(body, name=name, out_shape=_sds((kk, NDEV * n8), w8.dtype),
                          compiler_params=pltpu.CompilerParams(vmem_limit_bytes=VMEM_LIMIT))(w8)


def _adamw_math(w, m, v, g):
    m2 = B1 * m + (1.0 - B1) * g
    v2 = B2 * v + (1.0 - B2) * (g * g)
    mh = m2 / (1.0 - B1 ** STEP)
    vh = v2 / (1.0 - B2 ** STEP)
    return -LR * (mh / (jnp.sqrt(vh) + AEPS) + WD * w), m2, v2


def adamw_big(w, m, v, parts, name):
    lyr, r, c = w.shape
    by_cols = c >= 512
    blk = (lyr, r, 256) if by_cols else (lyr, 256 if r % 256 == 0 else r, c)
    imap = (lambda i: (0, 0, i)) if by_cols else (lambda i: (0, i, 0))

    def body(w_ref, m_ref, v_ref, *rest):
        p_refs, (g_ref, d_ref, m2_ref, v2_ref) = rest[:lyr], rest[lyr:]
        for l in range(lyr):
            g = p_refs[l][0].astype(F32)
            for q in range(1, NDEV):
                g = g + p_refs[l][q].astype(F32)
            g_ref[l] = g
            d_ref[l], m2_ref[l], v2_ref[l] = _adamw_math(w_ref[l], m_ref[l], v_ref[l], g)

    spec = pl.BlockSpec(blk, imap)
    pspec = pl.BlockSpec((NDEV,) + blk[1:], imap)
    return pl.pallas_call(
        body, name=name, grid=((c // 256) if by_cols else (r // blk[1]),),
        in_specs=[spec, spec, spec] + [pspec] * lyr,
        out_specs=[spec] * 4, out_shape=[_sds((lyr, r, c), F32)] * 4, compiler_params=_cp(1),
    )(w, m, v, *parts)


def adamw_small(w, m, v, g, name):
    def body(w_ref, m_ref, v_ref, g_ref, d_ref, m2_ref, v2_ref):
        d_ref[...], m2_ref[...], v2_ref[...] = _adamw_math(w_ref[...], m_ref[...], v_ref[...], g_ref[...])

    return pl.pallas_call(body, name=name, out_shape=[_sds(w.shape, F32)] * 3)(w, m, v, g)


NAMES = ["meta_tokens", "norm_mix", "norm_ffn", "conv_w_in", "conv_b_in", "conv_dw", "conv_ln_g", "conv_ln_b", "conv_w_out",
         "conv_b_out", "kv_norm", "w_kv", "k_norm", "w_q", "q_norm", "attn_sinks", "w_o", "ffn_w_gate", "ffn_w_up", "ffn_w_down"]
REP_ROWS = 16


def _pad_cols(a, width):
    return jnp.pad(a, ((0, 0), (0, width - a.shape[1])))


def _pack_rep(p):
    rows = [p["norm_mix"], p["norm_ffn"], p["kv_norm"].reshape(1, D), _pad_cols(p["k_norm"].reshape(1, HD), D),
            _pad_cols(p["q_norm"], D), _pad_cols(p["attn_sinks"], D), jnp.zeros((2, D), F32)]
    return jnp.concatenate(rows, axis=0)


def _unpack_rep(a):
    return {"norm_mix": a[0:4], "norm_ffn": a[4:8], "kv_norm": a[8], "k_norm": a[9, :HD], "q_norm": a[10:12, :HD],
            "attn_sinks": a[12:14, :NH]}


SH_NAMES = ["meta_tokens", "conv_b_in", "conv_dw", "conv_ln_g", "conv_ln_b", "conv_b_out"]


def _pack_sh(p):
    c = D // NDEV
    rows = [p["meta_tokens"], p["conv_b_in"].reshape(4, c), p["conv_dw"].reshape(2 * CW, c), p["conv_ln_g"], p["conv_ln_b"],
            p["conv_b_out"]]
    return jnp.concatenate(rows, axis=0)


def _unpack_sh(a):
    c = D // NDEV
    return {"meta_tokens": a[0:16], "conv_b_in": a[16:20].reshape(2, 2 * c), "conv_dw": a[20:82].reshape(2, CW, c),
            "conv_ln_g": a[82:84], "conv_ln_b": a[84:86], "conv_b_out": a[86:88]}


def kernel(x, meta_tokens, norm_mix, norm_ffn, conv_w_in, conv_b_in, conv_dw, conv_ln_g, conv_ln_b, conv_w_out, conv_b_out, kv_norm, w_kv, k_norm, w_q, q_norm, attn_sinks, w_o, ffn_w_gate, ffn_w_up, ffn_w_down, loss_target, m_meta_tokens, m_norm_mix, m_norm_ffn, m_conv_w_in, m_conv_b_in, m_conv_dw, m_conv_ln_g, m_conv_ln_b, m_conv_w_out, m_conv_b_out, m_kv_norm, m_w_kv, m_k_norm, m_w_q, m_q_norm, m_attn_sinks, m_w_o, m_ffn_w_gate, m_ffn_w_up, m_ffn_w_down, v_meta_tokens, v_norm_mix, v_norm_ffn, v_conv_w_in, v_conv_b_in, v_conv_dw, v_conv_ln_g, v_conv_ln_b, v_conv_w_out, v_conv_b_out, v_kv_norm, v_w_kv, v_k_norm, v_w_q, v_q_norm, v_attn_sinks, v_w_o, v_ffn_w_gate, v_ffn_w_up, v_ffn_w_down):
    wts = dict(zip(NAMES, (meta_tokens, norm_mix, norm_ffn, conv_w_in, conv_b_in, conv_dw, conv_ln_g, conv_ln_b, conv_w_out,
                           conv_b_out, kv_norm, w_kv, k_norm, w_q, q_norm, attn_sinks, w_o, ffn_w_gate, ffn_w_up, ffn_w_down)))
    mom = dict(zip(NAMES, (m_meta_tokens, m_norm_mix, m_norm_ffn, m_conv_w_in, m_conv_b_in, m_conv_dw, m_conv_ln_g, m_conv_ln_b,
                           m_conv_w_out, m_conv_b_out, m_kv_norm, m_w_kv, m_k_norm, m_w_q, m_q_norm, m_attn_sinks, m_w_o,
                           m_ffn_w_gate, m_ffn_w_up, m_ffn_w_down)))
    var = dict(zip(NAMES, (v_meta_tokens, v_norm_mix, v_norm_ffn, v_conv_w_in, v_conv_b_in, v_conv_dw, v_conv_ln_g, v_conv_ln_b,
                           v_conv_w_out, v_conv_b_out, v_kv_norm, v_w_kv, v_k_norm, v_w_q, v_q_norm, v_attn_sinks, v_w_o,
                           v_ffn_w_gate, v_ffn_w_up, v_ffn_w_down)))
    me = _my_index()
    c8 = D // NDEV

    big_names = list(BIG)
    layers = cast_bf16([wts[k] for k in big_names])
    shards = {(k, l): blk for k, per_layer in zip(big_names, layers) for l, blk in enumerate(per_layer)}
    first = [("conv_w_in", 0), ("conv_w_out", 0)]
    vec_names = ["meta_tokens", "conv_b_in", "conv_dw", "conv_ln_g", "conv_ln_b", "conv_b_out"]
    gathered = all_gather_blocks([shards[key] for key in first] + [wts[k] for k in vec_names])
    mats = {key: gathered_matrix(*key, b8) for key, b8 in zip(first, gathered)}
    full = dict(zip(vec_names, gathered[len(first):]))
    join_vec = lambda a: jnp.moveaxis(a, 0, -2).reshape(a.shape[1:-1] + (NDEV * a.shape[-1],))
    w = {}
    w["conv_b_in"] = join_vec(full["conv_b_in"]).reshape(2, 1, 2 * D)
    w["conv_dw"] = join_vec(full["conv_dw"])
    for k in ("conv_ln_g", "conv_ln_b", "conv_b_out"):
        w[k] = join_vec(full[k]).reshape(2, 1, D)
    w["norm_mix"] = norm_mix.reshape(4, 1, D)
    w["norm_ffn"] = norm_ffn.reshape(4, 1, D)
    w["kv_norm"] = kv_norm.reshape(1, D)
    w["k_norm"] = k_norm.reshape(1, HD)
    w["q_norm"] = q_norm.reshape(2, 1, HD)
    w["attn_sinks"] = attn_sinks.reshape(2, 1, NH)

    loss_blk, grad_x, _, arrived, gs = local_step(x, loss_target, full["meta_tokens"], w, mats, shards)

    stack = lambda k, n: jnp.concatenate([gs[f"{k}{i}"] for i in range(n)], axis=0)
    rep = {"norm_mix": stack("norm_mix", 4), "norm_ffn": stack("norm_ffn", 4), "kv_norm": gs["kv_norm"], "k_norm": gs["k_norm"],
           "q_norm": stack("q_norm", 2), "attn_sinks": stack("attn_sinks", 2)}
    loss_row = _pad_cols(loss_blk[0:1, 0:1], D)
    packed = jnp.concatenate(
        [_pack_rep(rep)[:14], loss_row, jnp.zeros((1, D), F32), gs["meta_tokens"], stack("conv_b_in", 2).reshape(4, D),
         stack("conv_dw", 2), stack("conv_ln_g", 2), stack("conv_ln_b", 2), stack("conv_b_out", 2)], axis=0)
    red = all_reduce_small(packed)
    loss = red[14, 0]
    cols = lambda a, width: lax.dynamic_slice_in_dim(a, me * width, width, axis=1)
    g_sh = jnp.concatenate(
        [cols(red[16:32], c8), cols(red[32:36].reshape(2, 2 * D), 2 * c8).reshape(4, c8), cols(red[36:98], c8),
         cols(red[98:100], c8), cols(red[100:102], c8), cols(red[102:104], c8)], axis=0)
    g_rep = red[0:REP_ROWS].at[14:].set(0.0)

    grads, delta, new_m, new_v = {}, {}, {}, {}
    for k in big_names:
        flat2 = wts[k].ndim == 2
        parts = [arrived[k]] if flat2 else [arrived[f"{k}{i}"] for i in range(wts[k].shape[0])]
        as3 = (lambda a: a[None]) if flat2 else (lambda a: a)
        outs = adamw_big(as3(wts[k]), as3(mom[k]), as3(var[k]), parts, "adamw_" + k)
        grads[k], delta[k], new_m[k], new_v[k] = [o[0] if flat2 else o for o in outs]
    d_rep, m_rep, v_rep = adamw_small(_pack_rep(wts), _pack_rep(mom), _pack_rep(var), g_rep, "adamw_rep")
    d_sh, m_sh, v_sh = adamw_small(_pack_sh(wts), _pack_sh(mom), _pack_sh(var), g_sh, "adamw_sh")
    for dst, a_rep, a_sh in ((grads, g_rep, g_sh), (delta, d_rep, d_sh), (new_m, m_rep, m_sh), (new_v, v_rep, v_sh)):
        dst.update(_unpack_rep(a_rep))
        dst.update(_unpack_sh(a_sh))
    return (loss, grad_x, *[grads[k] for k in NAMES], *[delta[k] for k in NAMES], *[new_m[k] for k in NAMES],
            *[new_v[k] for k in NAMES])
```

```python
import functools

import jax
import jax.numpy as jnp
from jax import lax
from jax.experimental import pallas as pl
from jax.experimental.pallas import tpu as pltpu

F32 = jnp.float32
BF = jnp.bfloat16

D = 1024
DFF = 2816
NH = 16
NKV = 4
HD = 64
KVD = NKV * HD
NMETA = 16
CW = 31
HALO = 32
CHUNK = 32
QB = 128
EPS = 1e-6
NEG = -1e30
NDEV = 8
SCALE = HD ** -0.5

LR, B1, B2, AEPS, WD, STEP = 0.001, 0.9, 0.999, 1e-08, 0.01, 10

VMEM_LIMIT = 56 * 2 ** 20
MESH = pl.DeviceIdType.MESH


def _cp(n):
    return pltpu.CompilerParams(dimension_semantics=("arbitrary",) * n, vmem_limit_bytes=VMEM_LIMIT)


def _row(tm, c):
    return pl.BlockSpec((tm, c), lambda i: (i, 0))


def _res(shape):
    return pl.BlockSpec(shape, lambda i: (0,) * len(shape), pipeline_mode=pl.Buffered(1))


def _lay(l, shape):
    return pl.BlockSpec((None,) + tuple(shape), lambda i: (l,) + (0,) * len(shape), pipeline_mode=pl.Buffered(1))


def _acc(shape):
    return pl.BlockSpec(shape, lambda i: (0,) * len(shape))


def _sds(shape, dt):
    return jax.ShapeDtypeStruct(tuple(shape), dt)


def _dot(a, b):
    return jnp.dot(a.astype(BF), b.astype(BF), preferred_element_type=F32)


def _dot_nt(a, b):
    return lax.dot_general(a.astype(BF), b.astype(BF), (((1,), (1,)), ((), ())), preferred_element_type=F32)


def _dot_tn(a, b):
    return lax.dot_general(a.astype(BF), b.astype(BF), (((0,), (0,)), ((), ())), preferred_element_type=F32)


def _rstd(x):
    return lax.rsqrt(jnp.mean(x * x, axis=-1, keepdims=True) + EPS)


def _rms_bwd(x, g, dy):
    r = _rstd(x)
    z = dy * g
    dx = r * z - x * (r * r * r * jnp.mean(z * x, axis=-1, keepdims=True))
    return dx, jnp.sum(dy * x * r, axis=0, keepdims=True)


def _sig(x):
    return jax.nn.sigmoid(x)


def _fold8(x):
    out = x[0:8]
    for k in range(1, x.shape[0] // 8):
        out = out + x[8 * k:8 * k + 8]
    return out


def _shifted(win):
    return [win] + [pltpu.roll(win, 2 * CHUNK - rho, 0) for rho in range(1, 8)]


def _tap(phases, o):
    return phases[o % 8][8 * (o // 8):8 * (o // 8) + CHUNK]


def _init(ref, first):
    @pl.when(first)
    def _():
        ref[...] = jnp.zeros_like(ref)


def _my_index():
    return 4 * lax.axis_index("x") + 2 * lax.axis_index("y") + lax.axis_index("c")


def _coords(idx):
    return (idx // 4, (idx // 2) % 2, idx % 2)


def _xch_shapes(xch):
    return [_sds((NDEV,) + ((a.shape[0] // NDEV, a.shape[1]) if k == "rows" else tuple(a.shape[1:])), a.dtype) for a, k in xch]


def _xch_scratch(n):
    return [pltpu.SemaphoreType.DMA((n, NDEV - 1)), pltpu.SemaphoreType.DMA((n, NDEV - 1)), pltpu.SemaphoreType.DMA((n,))]


def _xch_copies(kinds, srcs, outs, sems, arrivals):
    send_sems, recv_sems, local_sems = sems
    me = _my_index()

    def piece(a, p):
        if kinds[a] == "rows":
            r = srcs[a].shape[0] // NDEV
            return srcs[a].at[pl.ds(p * r, r), :]
        return srcs[a].at[p]

    def remote(a, k, src, slot):
        return pltpu.make_async_remote_copy(
            src_ref=src, dst_ref=outs[a].at[slot], send_sem=send_sems.at[a, k - 1], recv_sem=recv_sems.at[a, k - 1],
            device_id=_coords(me ^ k), device_id_type=MESH)

    n = len(kinds)
    local = [pltpu.make_async_copy(piece(a, me), outs[a].at[me], local_sems.at[a]) for a in range(n)]
    sends = [remote(a, k, piece(a, me ^ k), me) for a in range(n) for k in range(1, NDEV)]
    recvs = [remote(a, k, piece(a, me), me ^ k) for a in range(n) for k in range(1, NDEV)] if arrivals else []
    return local, sends, recvs


def _xch_start(kinds, srcs, outs, sems):
    local, sends, _ = _xch_copies(kinds, srcs, outs, sems, False)
    for cp in local + sends:
        cp.start()


def _xch_wait(kinds, srcs, outs, sems):
    local, sends, recvs = _xch_copies(kinds, srcs, outs, sems, True)
    for cp in recvs:
        cp.wait_recv()
    for cp in sends:
        cp.wait_send()
    for cp in local:
        cp.wait()


def _gat_copies(srcs, outs, sems):
    send_sems, recv_sems, local_sems = sems
    x, y, c = lax.axis_index("x"), lax.axis_index("y"), lax.axis_index("c")
    me, sibling = (x, y, c), (x, y, 1 - c)
    chips = [(1 - x, y), (x, 1 - y), (1 - x, 1 - y)]

    def copy(a, k, owner, to, from_block=False):
        slot = outs[a].at[4 * owner[0] + 2 * owner[1] + owner[2]]
        return pltpu.make_async_remote_copy(
            src_ref=srcs[a] if from_block else slot, dst_ref=slot, send_sem=send_sems.at[a, k], recv_sem=recv_sems.at[a, k],
            device_id=to, device_id_type=MESH)

    n = len(srcs)
    local = lambda: [pltpu.make_async_copy(srcs[a], outs[a].at[4 * x + 2 * y + c], local_sems.at[a]) for a in range(n)]
    first = lambda: [cp for a in range(n) for cp in
                     [copy(a, 0, me, sibling, True)] + [copy(a, 1 + j, me, (*chip, c), True) for j, chip in enumerate(chips)]]
    landed = lambda: [copy(a, 1 + j, (*chip, c), me) for a in range(n) for j, chip in enumerate(chips)]
    passed = lambda: [copy(a, 4 + j, (*chip, c), sibling) for a in range(n) for j, chip in enumerate(chips)]
    final = lambda: [cp for a in range(n) for cp in
                     [copy(a, 0, sibling, me)] + [copy(a, 4 + j, (*chip, 1 - c), me) for j, chip in enumerate(chips)]]
    return local, first, landed, passed, final


def _gat_start(srcs, outs, sems):
    local, first, _, _, _ = _gat_copies(srcs, outs, sems)
    for cp in local() + first():
        cp.start()


def _gat_forward(srcs, outs, sems):
    _, _, landed, passed, _ = _gat_copies(srcs, outs, sems)
    for got, on in zip(landed(), passed()):
        got.wait_recv()
        on.start()


def _gat_wait(srcs, outs, sems):
    local, first, _, passed, final = _gat_copies(srcs, outs, sems)
    for cp in final():
        cp.wait_recv()
    for cp in first() + passed():
        cp.wait_send()
    for cp in local():
        cp.wait()


def _call(body, name, grid, in_specs, out_specs, out_shape, args, scratch=(), xch=(), gat=()):
    n_in, n_out, n_x, n_g, n_s = len(in_specs), len(out_specs), len(xch), len(gat), len(scratch)
    kinds = [k for _, k in xch]
    total = 1
    for g in grid:
        total *= g

    def wrapped(*refs):
        ins, refs = refs[:n_in], refs[n_in:]
        x_src, refs = refs[:n_x], refs[n_x:]
        g_src, refs = refs[:n_g], refs[n_g:]
        outs, refs = refs[:n_out], refs[n_out:]
        x_out, refs = refs[:n_x], refs[n_x:]
        g_out, refs = refs[:n_g], refs[n_g:]
        own, refs = refs[:n_s], refs[n_s:]
        x_sems, g_sems = (refs[:3], refs[3:]) if n_x else ((), refs)
        step = pl.program_id(0)
        for d in range(1, len(grid)):
            step = step * grid[d] + pl.program_id(d)
        if n_x or n_g:
            @pl.when(step == 0)
            def _():
                if n_x:
                    _xch_start(kinds, x_src, x_out, x_sems)
                if n_g:
                    _gat_start(g_src, g_out, g_sems)

        body(*ins, *outs, *own)
        if n_g:
            @pl.when(step == max(total - 2, 0))
            def _():
                _gat_forward(g_src, g_out, g_sems)

        if n_x or n_g:
            @pl.when(step == total - 1)
            def _():
                if n_x:
                    _xch_wait(kinds, x_src, x_out, x_sems)
                if n_g:
                    _gat_wait(g_src, g_out, g_sems)

    any_spec = pl.BlockSpec(memory_space=pl.ANY)
    g_shapes = [_sds((NDEV,) + tuple(a.shape), a.dtype) for a in gat]
    res = pl.pallas_call(
        wrapped, name=name, grid=grid, in_specs=list(in_specs) + [any_spec] * (n_x + n_g),
        out_specs=list(out_specs) + [any_spec] * (n_x + n_g), out_shape=list(out_shape) + _xch_shapes(xch) + g_shapes,
        scratch_shapes=list(scratch) + (_xch_scratch(n_x) if n_x else []) + (_xch_scratch(n_g) if n_g else []),
        compiler_params=_cp(len(grid)),
    )(*args, *[a for a, _ in xch], *gat)
    return res[:n_out], res[n_out:n_out + n_x], res[n_out + n_x:]


def exchange_grads(xch):
    kinds = [k for _, k in xch]
    n = len(xch)

    def body(*refs):
        _xch_start(kinds, refs[:n], refs[n:2 * n], refs[2 * n:])
        _xch_wait(kinds, refs[:n], refs[n:2 * n], refs[2 * n:])

    any_spec = pl.BlockSpec(memory_space=pl.ANY)
    return pl.pallas_call(
        body, name="exchange_grads", out_shape=_xch_shapes(xch), in_specs=[any_spec] * n, out_specs=[any_spec] * n,
        scratch_shapes=_xch_scratch(n),
    )(*[a for a, _ in xch])


def embed(x, meta8, lp):
    bl, seq, _ = x.shape
    c8 = D // NDEV
    cb = 2 * c8

    def body(x_ref, m_ref, h_ref):
        h_ref[0:NMETA, :] = jnp.concatenate([m_ref[0], m_ref[1]], axis=1)
        h_ref[NMETA:NMETA + seq, :] = x_ref[...]
        h_ref[NMETA + seq:, :] = jnp.zeros((lp - NMETA - seq, cb), F32)

    return pl.pallas_call(
        body, name="embed", grid=(bl, D // cb),
        in_specs=[pl.BlockSpec((None, seq, cb), lambda b, c: (b, 0, c)), pl.BlockSpec((2, NMETA, c8), lambda b, c: (c, 0, 0))],
        out_specs=pl.BlockSpec((None, lp, cb), lambda b, c: (b, 0, c)), out_shape=_sds((bl, lp, D), F32),
        compiler_params=_cp(2),
    )(x, meta8)


def conv_in_fwd(h, nm, l, w_in, b_in, i, tm):
    t = h.shape[0]

    def body(h_ref, g_ref, w_ref, b_ref, u_ref, big_ref, a_ref):
        x = h_ref[...]
        ub = (x * _rstd(x) * g_ref[...]).astype(BF)
        u_ref[...] = ub
        big = jnp.dot(ub, w_ref[...], preferred_element_type=F32) + b_ref[...]
        big_ref[...] = big
        a_ref[...] = big[:, :D] * _sig(big[:, D:])

    return pl.pallas_call(
        body, name=f"conv_in_fwd{i}", grid=(t // tm,),
        in_specs=[_row(tm, D), _lay(l, (1, D)), _res((D, 2 * D)), _lay(i, (1, 2 * D))],
        out_specs=[_row(tm, D), _row(tm, 2 * D), _row(tm, D)],
        out_shape=[_sds((t, D), BF), _sds((t, 2 * D), F32), _sds((t, D), F32)],
        compiler_params=_cp(1),
    )(h, nm, w_in, b_in)


def _prev_halo(tm):
    return pl.BlockSpec((HALO, D), lambda i: (jnp.maximum(i * (tm // HALO) - 1, 0), 0))


def _next_halo(tm, t):
    return pl.BlockSpec((HALO, D), lambda i: (jnp.minimum((i + 1) * (tm // HALO), t // HALO - 1), 0))


def conv_mid_fwd(a, dw, ln_g, ln_b, i, tm, tpb, gat):
    t = a.shape[0]

    def body(a_ref, halo_ref, dw_ref, g_ref, b_ref, c_ref, s_ref, ext):
        first = pl.program_id(0) % tpb == 0
        ext[0:HALO] = jnp.where(first, 0.0, halo_ref[...])
        ext[HALO:] = a_ref[...]

        def chunk(k, carry):
            r0 = pl.multiple_of(k * CHUNK, CHUNK)
            win = _shifted(ext[pl.ds(r0, 2 * CHUNK), :])
            c = jnp.zeros((CHUNK, D), F32)
            for j in range(CW):
                c = c + dw_ref[j:j + 1, :] * _tap(win, j + 2)
            c_ref[pl.ds(r0, CHUNK), :] = c
            mu = jnp.mean(c, axis=-1, keepdims=True)
            xc = c - mu
            n = xc * lax.rsqrt(jnp.mean(xc * xc, axis=-1, keepdims=True) + EPS) * g_ref[...] + b_ref[...]
            s_ref[pl.ds(r0, CHUNK), :] = (n * _sig(n)).astype(BF)
            return carry

        lax.fori_loop(0, tm // CHUNK, chunk, 0)

    return _call(
        body, f"conv_mid_fwd{i}", (t // tm,),
        [_row(tm, D), _prev_halo(tm), _lay(i, (CW, D)), _lay(i, (1, D)), _lay(i, (1, D))],
        [_row(tm, D), _row(tm, D)], [_sds((t, D), F32), _sds((t, D), BF)], (a, a, dw, ln_g, ln_b),
        scratch=[pltpu.VMEM((tm + HALO, D), F32)], gat=gat)


def mixer_out_fwd(h, s, w, lw, bias, nf, l, tm, name):
    t = h.shape[0]

    def body(*refs):
        if bias is None:
            h_ref, s_ref, w_ref, g_ref, h1_ref, u_ref = refs
            y = 0.0
        else:
            h_ref, s_ref, w_ref, b_ref, g_ref, h1_ref, u_ref = refs
            y = b_ref[...]
        h1 = h_ref[...] + (jnp.dot(s_ref[...], w_ref[...], preferred_element_type=F32) + y)
        h1_ref[...] = h1
        u_ref[...] = (h1 * _rstd(h1) * g_ref[...]).astype(BF)

    ins = [h, s, w] + ([] if bias is None else [bias]) + [nf]
    specs = [_row(tm, D), _row(tm, D), _res((D, D))] + ([] if bias is None else [_lay(lw, (1, D))]) + [_lay(l, (1, D))]
    return pl.pallas_call(
        body, name=name, grid=(t // tm,), in_specs=specs,
        out_specs=[_row(tm, D), _row(tm, D)], out_shape=[_sds((t, D), F32), _sds((t, D), BF)],
        compiler_params=_cp(1),
    )(*ins)


def ffn_up_fwd(u, wg, wu, l, tm, gat):
    t = u.shape[0]

    def body(u_ref, wg_ref, wu_ref, g_ref, up_ref, hid_ref):
        ub = u_ref[...]
        g = _dot_nt(ub, wg_ref[...])
        up = _dot_nt(ub, wu_ref[...])
        g_ref[...] = g.astype(BF)
        up_ref[...] = up.astype(BF)
        hid_ref[...] = (g * _sig(g) * up).astype(BF)

    return _call(
        body, f"ffn_up_fwd{l}", (t // tm,), [_row(tm, D), _res((DFF, D)), _res((DFF, D))],
        [_row(tm, DFF)] * 3, [_sds((t, DFF), BF)] * 3, (u, wg, wu), gat=gat)


def ffn_down_fwd(hid, h1, wd, l, tm):
    t = h1.shape[0]

    def body(hid_ref, h1_ref, w_ref, h2_ref):
        h2_ref[...] = h1_ref[...] + jnp.dot(hid_ref[...], w_ref[...], preferred_element_type=F32)

    return pl.pallas_call(
        body, name=f"ffn_down_fwd{l}", grid=(t // tm,),
        in_specs=[_row(tm, DFF), _row(tm, D), _res((DFF, D))],
        out_specs=_row(tm, D), out_shape=_sds((t, D), F32),
        compiler_params=_cp(1),
    )(hid, h1, wd)


def _seg_rms(x, g, nseg):
    outs = []
    for s in range(nseg):
        xs = x[:, HD * s:HD * s + HD]
        outs.append(xs * _rstd(xs) * g)
    return jnp.concatenate(outs, axis=1)


def kv_fwd(h, kvn, w_kv, kng, tm):
    t = h.shape[0]

    def body(h_ref, g_ref, w_ref, kg_ref, kn_ref, kv_ref, k_ref, v_ref):
        x = h_ref[...]
        kn = (x * _rstd(x) * g_ref[...]).astype(BF)
        kn_ref[...] = kn
        kv = jnp.dot(kn, w_ref[...], preferred_element_type=F32)
        kv_ref[...] = kv
        k_ref[...] = _seg_rms(kv[:, :KVD], kg_ref[...], NKV).astype(BF)
        v_ref[...] = kv[:, KVD:].astype(BF)

    return pl.pallas_call(
        body, name="kv_fwd", grid=(t // tm,),
        in_specs=[_row(tm, D), _res((1, D)), _res((D, 2 * KVD)), _res((1, HD))],
        out_specs=[_row(tm, D), _row(tm, 2 * KVD), _row(tm, KVD), _row(tm, KVD)],
        out_shape=[_sds((t, D), BF), _sds((t, 2 * KVD), F32), _sds((t, KVD), BF), _sds((t, KVD), BF)],
        compiler_params=_cp(1),
    )(h, kvn, w_kv, kng)


def q_fwd(h, nm, l, w_q, j, tm):
    t = h.shape[0]

    def body(h_ref, g_ref, w_ref, u_ref, q_ref):
        x = h_ref[...]
        ub = (x * _rstd(x) * g_ref[...]).astype(BF)
        u_ref[...] = ub
        q_ref[...] = jnp.dot(ub, w_ref[...], preferred_element_type=F32)

    return pl.pallas_call(
        body, name=f"q_fwd{j}", grid=(t // tm,),
        in_specs=[_row(tm, D), _lay(l, (1, D)), _res((D, D))],
        out_specs=[_row(tm, D), _row(tm, D)], out_shape=[_sds((t, D), BF), _sds((t, D), F32)],
        compiler_params=_cp(1),
    )(h, nm, w_q)


RQ = NH // NKV


def _attn_specs(nb, lp):
    cur = lambda c: pl.BlockSpec((QB, c), lambda b, n: (b * nb + n, 0))
    seq = pl.BlockSpec((None, lp, KVD), lambda b, n: (b, 0, 0))
    return cur, seq


NKEYS = 2 * QB + NMETA


def _attn_mask(n, start):
    shape = (RQ * QB, NKEYS)
    qpos = n * QB + (lax.broadcasted_iota(jnp.int32, shape, 0) & (QB - 1))
    col = lax.broadcasted_iota(jnp.int32, shape, 1)
    in_band = col < 2 * QB
    kpos = jnp.where(in_band, start + col, col - 2 * QB)
    return (kpos <= qpos) & ((col >= 2 * QB) | ((qpos - kpos < QB) & (kpos >= NMETA)))


def _keys(ref, band, gs):
    return jnp.concatenate([ref[band, gs], ref[0:NMETA, gs]], axis=0)


def _stack_heads(ref, g, fn):
    return jnp.concatenate([fn(ref[:, HD * (g * RQ + r):HD * (g * RQ + r) + HD]) for r in range(RQ)], axis=0)


def _stack_cols(ref, g):
    return jnp.concatenate([ref[:, g * RQ + r:g * RQ + r + 1] for r in range(RQ)], axis=0)


def _stack_sinks(sk_ref, g):
    return jnp.concatenate([jnp.broadcast_to(sk_ref[:, g * RQ + r:g * RQ + r + 1], (QB, 1)) for r in range(RQ)], axis=0)


def attn_fwd(q, k, v, qg, sinks, j, bl, lp, gat):
    t = q.shape[0]
    nb = lp // QB
    cur, seq = _attn_specs(nb, lp)

    def body(q_ref, k_ref, v_ref, qg_ref, sk_ref, o_ref, lse_ref):
        n = pl.program_id(1)
        start = pl.multiple_of(jnp.maximum(n - 1, 0) * QB, QB)
        mask = _attn_mask(n, start)
        band = pl.ds(start, 2 * QB)
        lane = lax.broadcasted_iota(jnp.int32, (QB, NH), 1)
        ones = jnp.ones((NKEYS, HD), BF)
        lse = jnp.zeros((QB, NH), F32)
        for g in range(NKV):
            gs = slice(HD * g, HD * g + HD)
            qn = _stack_heads(q_ref, g, lambda x: (x * _rstd(x) * qg_ref[...]).astype(BF))
            sink = _stack_sinks(sk_ref, g)
            s = jnp.where(mask, _dot_nt(qn, _keys(k_ref, band, gs)) * SCALE, NEG)
            mx = jnp.maximum(jnp.max(s, -1, keepdims=True), sink)
            oa = _dot(jnp.exp(s - mx), jnp.concatenate([_keys(v_ref, band, gs), ones], axis=1))
            den = oa[:, HD:HD + 1] + jnp.exp(sink - mx)
            o = oa[:, :HD] * (1.0 / den)
            l = mx + jnp.log(den)
            for r in range(RQ):
                h = g * RQ + r
                o_ref[:, HD * h:HD * h + HD] = o[r * QB:(r + 1) * QB].astype(BF)
                lse = jnp.where(lane == h, l[r * QB:(r + 1) * QB], lse)
        lse_ref[...] = lse

    return _call(
        body, f"attn_fwd{j}", (bl, nb),
        [cur(D), seq, seq, pl.BlockSpec((None, 1, HD), lambda b, n: (j, 0, 0)), pl.BlockSpec((None, 1, NH), lambda b, n: (j, 0, 0))],
        [cur(D), cur(NH)], [_sds((t, D), BF), _sds((t, NH), F32)], (q, k, v, qg, sinks), gat=gat)


def loss_fwd(h, tgt):
    bl, lp, _ = h.shape
    seq = tgt.shape[1]
    cb = 256

    def body(h_ref, t_ref, dh_ref, loss_ref):
        _init(loss_ref, (pl.program_id(0) == 0) & (pl.program_id(1) == 0))
        err = h_ref[NMETA:NMETA + seq, :] - t_ref[...]
        dh_ref[...] = jnp.zeros_like(dh_ref)
        dh_ref[NMETA:NMETA + seq, :] = err * (1.0 / D)
        loss_ref[...] += (0.5 / D) * jnp.sum(err * err)

    return pl.pallas_call(
        body, name="loss_fwd", grid=(bl, D // cb),
        in_specs=[pl.BlockSpec((None, lp, cb), lambda b, c: (b, 0, c)), pl.BlockSpec((None, seq, cb), lambda b, c: (b, 0, c))],
        out_specs=[pl.BlockSpec((None, lp, cb), lambda b, c: (b, 0, c)), pl.BlockSpec((8, 128), lambda b, c: (0, 0))],
        out_shape=[_sds((bl, lp, D), F32), _sds((8, 128), F32)],
        compiler_params=_cp(2),
    )(h, tgt)


def ffn_bwd_x(dh2, g, up, h1, nf, l, wd, wg, wu, tm, xch):
    t = dh2.shape[0]

    def body(dh2_ref, g_ref, up_ref, h1_ref, nf_ref, wd_ref, wg_ref, wu_ref, dg_ref, du_ref, dh1_ref, dnf_ref):
        _init(dnf_ref, pl.program_id(0) == 0)
        dh2v = dh2_ref[...]
        dhid = _dot_nt(dh2v, wd_ref[...])
        gv = g_ref[...].astype(F32)
        uv = up_ref[...].astype(F32)
        sg = _sig(gv)
        dgv = (dhid * uv * (sg * (1.0 + gv * (1.0 - sg)))).astype(BF)
        duv = (dhid * (gv * sg)).astype(BF)
        dg_ref[...] = dgv
        du_ref[...] = duv
        dnorm = _dot(dgv, wg_ref[...]) + _dot(duv, wu_ref[...])
        dx, dnf = _rms_bwd(h1_ref[...], nf_ref[...], dnorm)
        dh1_ref[...] = dh2v + dx
        dnf_ref[...] += dnf

    return _call(
        body, f"ffn_bwd_x{l}", (t // tm,),
        [_row(tm, D), _row(tm, DFF), _row(tm, DFF), _row(tm, D), _lay(l, (1, D)),
         _res((DFF, D)), _res((DFF, D)), _res((DFF, D))],
        [_row(tm, DFF), _row(tm, DFF), _row(tm, D), _acc((1, D))],
        [_sds((t, DFF), BF), _sds((t, DFF), BF), _sds((t, D), F32), _sds((1, D), F32)],
        (dh2, g, up, h1, nf, wd, wg, wu), xch=xch)


def mm_tn(x, dy, tm, name, split=False, transposed=False, xch=()):
    t, kk = x.shape
    nn = dy.shape[1]
    n8 = nn // NDEV
    nsteps = t // tm

    def body(x_ref, dy_ref, o_ref, acc):
        i = pl.program_id(0)
        _init(acc, i == 0)
        acc[...] += _dot_tn(x_ref[...], dy_ref[...])

        @pl.when(i == nsteps - 1)
        def _():
            if split:
                for p in range(NDEV):
                    o_ref[p] = acc[:, p * n8:(p + 1) * n8].astype(BF)
            elif transposed:
                o_ref[...] = acc[...].T.astype(BF)
            else:
                o_ref[...] = acc[...].astype(BF)

    oshape = (NDEV, kk, n8) if split else ((nn, kk) if transposed else (kk, nn))
    (out,), got, _ = _call(body, name, (nsteps,), [_row(tm, kk), _row(tm, nn)], [_acc(oshape)], [_sds(oshape, BF)], (x, dy),
                           scratch=[pltpu.VMEM((kk, nn), F32)], xch=xch)
    return (out, got) if xch else out


def proj_bwd(dy, w, h, g, lg, dh_in, tm, name):
    t = h.shape[0]
    nn = dy.shape[1]
    wspec = _res(w.shape)
    gspec = _res((1, D)) if lg is None else _lay(lg, (1, D))

    def body(dy_ref, w_ref, h_ref, g_ref, dhin_ref, dh_ref, dg_ref):
        _init(dg_ref, pl.program_id(0) == 0)
        du = _dot_nt(dy_ref[...], w_ref[...])
        dx, dg = _rms_bwd(h_ref[...], g_ref[...], du)
        dh_ref[...] = dhin_ref[...] + dx
        dg_ref[...] += dg

    return pl.pallas_call(
        body, name=name, grid=(t // tm,),
        in_specs=[_row(tm, nn), wspec, _row(tm, D), gspec, _row(tm, D)],
        out_specs=[_row(tm, D), _acc((1, D))], out_shape=[_sds((t, D), F32), _sds((1, D), F32)],
        compiler_params=_cp(1),
    )(dy, w, h, g, dh_in)


def out_proj_bwd(dh1, w, tm, name):
    t = dh1.shape[0]

    def body(dh1_ref, w_ref, do_ref):
        do_ref[...] = _dot_nt(dh1_ref[...], w_ref[...]).astype(BF)

    return pl.pallas_call(
        body, name=name, grid=(t // tm,), in_specs=[_row(tm, D), _res((D, D))],
        out_specs=_row(tm, D), out_shape=_sds((t, D), BF), compiler_params=_cp(1),
    )(dh1, w)


def attn_bwd(q, k, v, do, o, lse, qg, sinks, j, bl, lp, xch):
    t = q.shape[0]
    nb = lp // QB
    cur, seq = _attn_specs(nb, lp)

    def body(q_ref, k_ref, v_ref, do_ref, o_ref, lse_ref, qg_ref, sk_ref, dq_ref, dk_ref, dv_ref, dqg_ref, dsk_ref):
        b, n = pl.program_id(0), pl.program_id(1)
        _init(dk_ref, n == 0)
        _init(dv_ref, n == 0)
        _init(dqg_ref, (b == 0) & (n == 0))
        _init(dsk_ref, (b == 0) & (n == 0))
        start = pl.multiple_of(jnp.maximum(n - 1, 0) * QB, QB)
        mask = _attn_mask(n, start)
        band = pl.ds(start, 2 * QB)
        lane = lax.broadcasted_iota(jnp.int32, (1, NH), 1)
        dqg = jnp.zeros((1, HD), F32)
        dsk = jnp.zeros((1, NH), F32)
        dks, dvs = [], []
        for g in range(NKV):
            gs = slice(HD * g, HD * g + HD)
            kk, vv = _keys(k_ref, band, gs), _keys(v_ref, band, gs)
            qh = _stack_heads(q_ref, g, lambda x: x)
            rs = _rstd(qh)
            qn = (qh * rs * qg_ref[...]).astype(BF)
            ls = _stack_cols(lse_ref, g)
            pr = jnp.where(mask, jnp.exp(_dot_nt(qn, kk) * SCALE - ls), 0.0)
            doh = _stack_heads(do_ref, g, lambda x: x)
            delta = jnp.sum(doh.astype(F32) * _stack_heads(o_ref, g, lambda x: x).astype(F32), axis=-1, keepdims=True)
            ds = (pr * (_dot_nt(doh, vv) - delta)).astype(BF)
            dqn = _dot(ds, kk) * SCALE
            dks.append(_dot_tn(qn, ds).T * SCALE)
            dvs.append(_dot_tn(doh, pr).T)
            dsink = jnp.exp(_stack_sinks(sk_ref, g) - ls) * delta
            z = dqn * qg_ref[...]
            dq = rs * z - qh * (rs * rs * rs * jnp.mean(z * qh, axis=-1, keepdims=True))
            dqg = dqg + jnp.sum(dqn * qh * rs, axis=0, keepdims=True)
            for r in range(RQ):
                h = g * RQ + r
                dq_ref[:, HD * h:HD * h + HD] = dq[r * QB:(r + 1) * QB]
                dsk = dsk + jnp.where(lane == h, -jnp.sum(dsink[r * QB:(r + 1) * QB]), 0.0)
        dk = jnp.concatenate(dks, axis=1)
        dv = jnp.concatenate(dvs, axis=1)
        dk_ref[band, :] += dk[:2 * QB]
        dv_ref[band, :] += dv[:2 * QB]
        dk_ref[0:NMETA, :] += dk[2 * QB:]
        dv_ref[0:NMETA, :] += dv[2 * QB:]
        dqg_ref[...] += dqg
        dsk_ref[...] += dsk

    return _call(
        body, f"attn_bwd{j}", (bl, nb),
        [cur(D), seq, seq, cur(D), cur(D), cur(NH),
         pl.BlockSpec((None, 1, HD), lambda b, n: (j, 0, 0)), pl.BlockSpec((None, 1, NH), lambda b, n: (j, 0, 0))],
        [cur(D), seq, seq, pl.BlockSpec((1, HD), lambda b, n: (0, 0)), pl.BlockSpec((1, NH), lambda b, n: (0, 0))],
        [_sds((t, D), F32), _sds((bl, lp, KVD), F32), _sds((bl, lp, KVD), F32), _sds((1, HD), F32), _sds((1, NH), F32)],
        (q, k, v, do, o, lse, qg, sinks), xch=xch)


def kv_bwd_pre(dk0, dk1, dv0, dv1, kv, kng, tm):
    t = kv.shape[0]

    def body(dk0_ref, dk1_ref, dv0_ref, dv1_ref, kv_ref, g_ref, dkv_ref, dg_ref):
        _init(dg_ref, pl.program_id(0) == 0)
        dk = dk0_ref[...] + dk1_ref[...]
        dg = jnp.zeros((1, HD), F32)
        outs = []
        for s in range(NKV):
            sl = slice(HD * s, HD * s + HD)
            dx, dgs = _rms_bwd(kv_ref[:, sl], g_ref[...], dk[:, sl])
            outs.append(dx)
            dg = dg + dgs
        dkv_ref[:, :KVD] = jnp.concatenate(outs, axis=1).astype(BF)
        dkv_ref[:, KVD:] = (dv0_ref[...] + dv1_ref[...]).astype(BF)
        dg_ref[...] += dg

    return pl.pallas_call(
        body, name="kv_bwd_pre", grid=(t // tm,),
        in_specs=[_row(tm, KVD)] * 4 + [_row(tm, 2 * KVD), _res((1, HD))],
        out_specs=[_row(tm, 2 * KVD), _acc((1, HD))], out_shape=[_sds((t, 2 * KVD), BF), _sds((1, HD), F32)],
        compiler_params=_cp(1),
    )(dk0, dk1, dv0, dv1, kv, kng)


def conv_out_bwd(dh1, c, ln_g, ln_b, w_out, i, tm):
    t = dh1.shape[0]

    def body(dh1_ref, c_ref, g_ref, b_ref, w_ref, dc_ref, dg_ref, db_ref, dbo_ref):
        first = pl.program_id(0) == 0
        _init(dg_ref, first)
        _init(db_ref, first)
        _init(dbo_ref, first)
        dh1v = dh1_ref[...]
        ds = _dot_nt(dh1v, w_ref[...])
        cv = c_ref[...]
        xc = cv - jnp.mean(cv, axis=-1, keepdims=True)
        rstd = lax.rsqrt(jnp.mean(xc * xc, axis=-1, keepdims=True) + EPS)
        xh = xc * rstd
        n = xh * g_ref[...] + b_ref[...]
        sg = _sig(n)
        dn = ds * (sg * (1.0 + n * (1.0 - sg)))
        dxh = dn * g_ref[...]
        dc_ref[...] = rstd * (dxh - jnp.mean(dxh, axis=-1, keepdims=True) - xh * jnp.mean(dxh * xh, axis=-1, keepdims=True))
        dg_ref[...] += jnp.sum(dn * xh, axis=0, keepdims=True)
        db_ref[...] += jnp.sum(dn, axis=0, keepdims=True)
        dbo_ref[...] += jnp.sum(dh1v, axis=0, keepdims=True)

    return pl.pallas_call(
        body, name=f"conv_out_bwd{i}", grid=(t // tm,),
        in_specs=[_row(tm, D), _row(tm, D), _lay(i, (1, D)), _lay(i, (1, D)), _res((D, D))],
        out_specs=[_row(tm, D), _acc((1, D)), _acc((1, D)), _acc((1, D))],
        out_shape=[_sds((t, D), F32)] + [_sds((1, D), F32)] * 3,
        compiler_params=_cp(1),
    )(dh1, c, ln_g, ln_b, w_out)


def conv_mid_bwd(dc, a, big, dw, i, tm, tpb, xch):
    t = dc.shape[0]
    nsteps = t // tm

    def body(dc_ref, nxt_ref, a_ref, prv_ref, big_ref, dw_ref, da_ref, dbin_ref, ddw_ref, dce, ae, wacc, bacc):
        i_ = pl.program_id(0)
        _init(wacc, i_ == 0)
        _init(bacc, i_ == 0)
        dce[0:tm] = dc_ref[...]
        dce[tm:] = jnp.where(i_ % tpb == tpb - 1, 0.0, nxt_ref[...])
        ae[0:HALO] = jnp.where(i_ % tpb == 0, 0.0, prv_ref[...])
        ae[HALO:] = a_ref[...]

        def chunk(k, carry):
            r0 = pl.multiple_of(k * CHUNK, CHUNK)
            wdc = _shifted(dce[pl.ds(r0, 2 * CHUNK), :])
            wa = _shifted(ae[pl.ds(r0, 2 * CHUNK), :])
            dcc = wdc[0][0:CHUNK]
            da = jnp.zeros((CHUNK, D), F32)
            for j in range(CW):
                da = da + dw_ref[j:j + 1, :] * _tap(wdc, CW - 1 - j)
                wacc[j] += _fold8(dcc * _tap(wa, j + 2))
            bv = big_ref[pl.ds(r0, CHUNK), :]
            a1, sg = bv[:, :D], _sig(bv[:, D:])
            d1 = da * sg
            d2 = da * a1 * sg * (1.0 - sg)
            da_ref[pl.ds(r0, CHUNK), 0:D] = d1.astype(BF)
            da_ref[pl.ds(r0, CHUNK), D:2 * D] = d2.astype(BF)
            bacc[:, 0:D] += _fold8(d1)
            bacc[:, D:2 * D] += _fold8(d2)
            return carry

        lax.fori_loop(0, tm // CHUNK, chunk, 0)

        @pl.when(i_ == nsteps - 1)
        def _():
            dbin_ref[...] = jnp.sum(bacc[...], axis=0, keepdims=True)
            ddw_ref[...] = jnp.sum(wacc[...], axis=1)

    return _call(
        body, f"conv_mid_bwd{i}", (nsteps,),
        [_row(tm, D), _next_halo(tm, t), _row(tm, D), _prev_halo(tm), _row(tm, 2 * D), _lay(i, (CW, D))],
        [_row(tm, 2 * D), _acc((1, 2 * D)), _acc((CW + 1, D))],
        [_sds((t, 2 * D), BF), _sds((1, 2 * D), F32), _sds((CW + 1, D), F32)],
        (dc, dc, a, a, big, dw),
        scratch=[pltpu.VMEM((tm + HALO, D), F32), pltpu.VMEM((tm + HALO, D), F32),
                 pltpu.VMEM((CW + 1, 8, D), F32), pltpu.VMEM((8, 2 * D), F32)], xch=xch)


def input_grads(dh0, seq):
    bl, lp, _ = dh0.shape
    cb = 256

    def body(dh_ref, gx_ref, gm_ref):
        _init(gm_ref, pl.program_id(1) == 0)
        gx_ref[...] = dh_ref[NMETA:NMETA + seq, :]
        gm_ref[...] += dh_ref[0:NMETA, :]

    return pl.pallas_call(
        body, name="input_grads", grid=(D // cb, bl),
        in_specs=[pl.BlockSpec((None, lp, cb), lambda c, b: (b, 0, c))],
        out_specs=[pl.BlockSpec((None, seq, cb), lambda c, b: (b, 0, c)), pl.BlockSpec((NMETA, cb), lambda c, b: (0, c))],
        out_shape=[_sds((bl, seq, D), F32), _sds((NMETA, D), F32)],
        compiler_params=_cp(2),
    )(dh0)


GATHER_PLAN = {
    "conv_mid_fwd0": [("ffn_w_gate", 0), ("ffn_w_up", 0), ("ffn_w_down", 0)],
    "ffn_up_fwd0": [("conv_w_in", 1), ("conv_w_out", 1), ("ffn_w_gate", 1)],
    "conv_mid_fwd1": [("ffn_w_up", 1), ("ffn_w_down", 1), ("w_kv", 0), ("w_q", 0)],
    "ffn_up_fwd1": [("w_o", 0), ("ffn_w_down", 2)],
    "attn_fwd0": [("ffn_w_gate", 2), ("ffn_w_up", 2), ("w_q", 1), ("w_o", 1), ("ffn_w_gate", 3), ("ffn_w_up", 3), ("ffn_w_down", 3)],
}
BIG = {"conv_w_in": "pieces", "conv_w_out": "rows", "w_kv": "rows", "w_q": "rows", "w_o": "rows",
       "ffn_w_gate": "rows", "ffn_w_up": "rows", "ffn_w_down": "rows"}
TRANSPOSED = ("ffn_w_gate", "ffn_w_up")


def gathered_matrix(name, layer, blocks8):
    if BIG[name] == "rows":
        return blocks8.reshape(NDEV * blocks8.shape[1], blocks8.shape[2])
    return join_columns(blocks8, f"join_{name}{layer}")


def local_step(x, tgt, meta8, w, mats, shards):
    bl, seq, _ = x.shape
    lp = -(-(NMETA + seq) // QB) * QB
    tpb = 4
    tm = lp // tpb
    t = bl * lp
    na = 2
    flat = lambda a: a.reshape(t, D)
    mats = dict(mats)

    def riders(carrier):
        return [shards[key] for key in GATHER_PLAN[carrier]]

    def landed(carrier, blocks):
        for key, b8 in zip(GATHER_PLAN[carrier], blocks):
            mats[key] = gathered_matrix(*key, b8)

    h = flat(embed(x, meta8, lp))
    saved = []
    kvs = None
    for l in range(4):
        rec = {"h": h}
        if l < na:
            rec["u"], rec["big"], rec["a"] = conv_in_fwd(h, w["norm_mix"], l, mats["conv_w_in", l], w["conv_b_in"], l, tm)
            name = f"conv_mid_fwd{l}"
            (rec["c"], rec["s"]), _, got = conv_mid_fwd(rec["a"], w["conv_dw"], w["conv_ln_g"], w["conv_ln_b"], l, tm, tpb,
                                                         riders(name))
            landed(name, got)
            rec["h1"], rec["u2"] = mixer_out_fwd(h, rec["s"], mats["conv_w_out", l], l, w["conv_b_out"], w["norm_ffn"], l, tm,
                                                 f"conv_out_fwd{l}")
        else:
            j = l - na
            if kvs is None:
                kvs = dict(zip(("kn", "kv", "k", "v"), kv_fwd(h, w["kv_norm"], mats["w_kv", 0], w["k_norm"], tm)))
                kvs["h"] = h
                kvs["k3"], kvs["v3"] = kvs["k"].reshape(bl, lp, KVD), kvs["v"].reshape(bl, lp, KVD)
            rec["u"], rec["q"] = q_fwd(h, w["norm_mix"], l, mats["w_q", j], j, tm)
            name = f"attn_fwd{j}"
            (rec["o"], rec["lse"]), _, got = attn_fwd(rec["q"], kvs["k3"], kvs["v3"], w["q_norm"], w["attn_sinks"], j, bl, lp,
                                                      riders(name) if name in GATHER_PLAN else [])
            if name in GATHER_PLAN:
                landed(name, got)
            rec["h1"], rec["u2"] = mixer_out_fwd(h, rec["o"], mats["w_o", j], j, None, w["norm_ffn"], l, tm, f"attn_out_fwd{j}")
        name = f"ffn_up_fwd{l}"
        (rec["g"], rec["up"], rec["hid"]), _, got = ffn_up_fwd(rec["u2"], mats["ffn_w_gate", l], mats["ffn_w_up", l], l, tm // 2,
                                                              riders(name) if name in GATHER_PLAN else [])
        if name in GATHER_PLAN:
            landed(name, got)
        h = ffn_down_fwd(rec["hid"], rec["h1"], mats["ffn_w_down", l], l, tm)
        saved.append(rec)

    dh3, loss_blk = loss_fwd(h.reshape(bl, lp, D), tgt)
    dh = flat(dh3)

    big, small, arrived = {}, {}, {}
    dks, dvs = [], []
    pending = []

    def carried(names, arrivals):
        arrived.update(zip([nm for nm, _ in names], arrivals))

    for l in reversed(range(4)):
        rec = saved[l]
        riders, pending = pending, []
        (dg, du, dh1, small[f"norm_ffn{l}"]), got, _ = ffn_bwd_x(
            dh, rec["g"], rec["up"], rec["h1"], w["norm_ffn"], l, mats["ffn_w_down", l], mats["ffn_w_gate", l], mats["ffn_w_up", l], tm // 2,
            [(big[nm], kind) for nm, kind in riders])
        carried(riders, got)
        big[f"ffn_w_down{l}"] = mm_tn(rec["hid"], dh, 2 * tm, f"dw_down{l}")
        if l >= na:
            big[f"ffn_w_gate{l}"] = mm_tn(rec["u2"], dg, 2 * tm, f"dw_gate{l}", transposed=True)
            big[f"ffn_w_up{l}"] = mm_tn(rec["u2"], du, 2 * tm, f"dw_up{l}", transposed=True)
            riders = [(f"ffn_w_down{l}", "rows"), (f"ffn_w_gate{l}", "rows"), (f"ffn_w_up{l}", "rows")]
        else:
            big[f"ffn_w_gate{l}"], got = mm_tn(rec["u2"], dg, 2 * tm, f"dw_gate{l}", transposed=True,
                                              xch=[(big[f"ffn_w_down{l}"], "rows")])
            carried([(f"ffn_w_down{l}", "rows")], got)
            big[f"ffn_w_up{l}"], got = mm_tn(rec["u2"], du, 2 * tm, f"dw_up{l}", transposed=True,
                                            xch=[(big[f"ffn_w_gate{l}"], "rows")])
            carried([(f"ffn_w_gate{l}", "rows")], got)
            riders = [(f"ffn_w_up{l}", "rows")]
        xch = [(big[nm], kind) for nm, kind in riders]
        if l >= na:
            j = l - na
            do = out_proj_bwd(dh1, mats["w_o", j], tm, f"attn_out_bwd{j}")
            big[f"w_o{j}"] = mm_tn(rec["o"], dh1, 2 * tm, f"dw_o{j}")
            (dq, dk, dv, small[f"q_norm{j}"], small[f"attn_sinks{j}"]), got, _ = attn_bwd(
                rec["q"], kvs["k3"], kvs["v3"], do, rec["o"], rec["lse"], w["q_norm"], w["attn_sinks"], j, bl, lp, xch)
            carried(riders, got)
            dks.append(dk.reshape(t, KVD))
            dvs.append(dv.reshape(t, KVD))
            big[f"w_q{j}"] = mm_tn(rec["u"], dq, 2 * tm, f"dw_q{j}")
            dh, small[f"norm_mix{l}"] = proj_bwd(dq, mats["w_q", j], rec["h"], w["norm_mix"], l, dh1, tm, f"q_bwd{j}")
            pending = [(f"w_o{j}", "rows"), (f"w_q{j}", "rows")]
            if l == na:
                dkv, small["k_norm"] = kv_bwd_pre(dks[0], dks[1], dvs[0], dvs[1], kvs["kv"], w["k_norm"], tm)
                big["w_kv"] = mm_tn(kvs["kn"], dkv, 2 * tm, "dw_kv")
                dh, small["kv_norm"] = proj_bwd(dkv, mats["w_kv", 0], kvs["h"], w["kv_norm"], None, dh, tm, "kv_bwd")
                pending.append(("w_kv", "rows"))
        else:
            dc, small[f"conv_ln_g{l}"], small[f"conv_ln_b{l}"], small[f"conv_b_out{l}"] = conv_out_bwd(
                dh1, rec["c"], w["conv_ln_g"], w["conv_ln_b"], mats["conv_w_out", l], l, tm)
            big[f"conv_w_out{l}"] = mm_tn(rec["s"], dh1, 2 * tm, f"dw_conv_out{l}")
            (da, small[f"conv_b_in{l}"], ddw), got, _ = conv_mid_bwd(dc, rec["a"], rec["big"], w["conv_dw"], l, tm, tpb, xch)
            carried(riders, got)
            small[f"conv_dw{l}"] = ddw[:CW]
            big[f"conv_w_in{l}"] = mm_tn(rec["u"], da, 2 * tm, f"dw_conv_in{l}", split=True)
            dh, small[f"norm_mix{l}"] = proj_bwd(da, mats["conv_w_in", l], rec["h"], w["norm_mix"], l, dh1, tm, f"conv_in_bwd{l}")
            pending = [(f"conv_w_out{l}", "rows"), (f"conv_w_in{l}", "pieces")]
    carried(pending, exchange_grads([(big[nm], kind) for nm, kind in pending]))
    grad_x, small["meta_tokens"] = input_grads(dh.reshape(bl, lp, D), seq)
    return loss_blk, grad_x, big, arrived, small


def all_gather_blocks(blocks):
    n = len(blocks)

    def body(*refs):
        srcs, outs, sems = refs[:n], refs[n:2 * n], refs[2 * n:]
        _gat_start(srcs, outs, sems)
        _gat_forward(srcs, outs, sems)
        _gat_wait(srcs, outs, sems)

    any_spec = pl.BlockSpec(memory_space=pl.ANY)
    return pl.pallas_call(
        body, name="all_gather_blocks", out_shape=[_sds((NDEV,) + tuple(a.shape), a.dtype) for a in blocks],
        in_specs=[any_spec] * n, out_specs=[any_spec] * n, scratch_shapes=_xch_scratch(n),
    )(*blocks)


def all_reduce_small(buf):
    rows = buf.shape[0]

    def body(x_ref, o_ref, g_ref, send_sems, recv_sems):
        me = _my_index()
        sends = []
        for k in range(1, NDEV):
            peer = me ^ k
            cp = pltpu.make_async_remote_copy(
                src_ref=x_ref, dst_ref=g_ref.at[me], send_sem=send_sems.at[k - 1], recv_sem=recv_sems.at[k - 1],
                device_id=_coords(peer), device_id_type=MESH)
            cp.start()
            sends.append(cp)
        g_ref[me] = x_ref[...]
        for k in range(1, NDEV):
            peer = me ^ k
            pltpu.make_async_remote_copy(
                src_ref=x_ref, dst_ref=g_ref.at[peer], send_sem=send_sems.at[k - 1], recv_sem=recv_sems.at[k - 1],
                device_id=_coords(peer), device_id_type=MESH).wait_recv()
        for cp in sends:
            cp.wait_send()
        acc = g_ref[0]
        for p in range(1, NDEV):
            acc = acc + g_ref[p]
        o_ref[...] = acc

    return pl.pallas_call(
        body, name="all_reduce_small", out_shape=_sds((rows, D), F32),
        in_specs=[pl.BlockSpec(memory_space=pltpu.VMEM)], out_specs=pl.BlockSpec(memory_space=pltpu.VMEM),
        scratch_shapes=[pltpu.VMEM((NDEV, rows, D), F32), pltpu.SemaphoreType.DMA((7,)), pltpu.SemaphoreType.DMA((7,))],
    )(buf)


def cast_bf16(ws):
    n = len(ws)
    counts = [1 if x.ndim == 2 else x.shape[0] for x in ws]

    def body(*refs):
        outs = iter(refs[n:])
        for a in range(n):
            for l in range(counts[a]):
                next(outs)[...] = (refs[a][...] if ws[a].ndim == 2 else refs[a][l]).astype(BF)

    flat = pl.pallas_call(
        body, name="cast_bf16", out_shape=[_sds(x.shape[-2:], BF) for x, k in zip(ws, counts) for _ in range(k)],
        compiler_params=pltpu.CompilerParams(vmem_limit_bytes=VMEM_LIMIT),
    )(*ws)
    it = iter(flat)
    return [[next(it) for _ in range(k)] for k in counts]


def join_columns(w8, name):
    _, kk, n8 = w8.shape

    def body(x_ref, o_ref):
        o_ref[...] = jnp.concatenate([x_ref[p] for p in range(NDEV)], axis=1)

    return pl.pallas_call(body, name=name, out_shape=_sds((kk, NDEV * n8), w8.dtype),
                          compiler_params=pltpu.CompilerParams(vmem_limit_bytes=VMEM_LIMIT))(w8)


def _adamw_math(w, m, v, g):
    m2 = B1 * m + (1.0 - B1) * g
    v2 = B2 * v + (1.0 - B2) * (g * g)
    mh = m2 / (1.0 - B1 ** STEP)
    vh = v2 / (1.0 - B2 ** STEP)
    return -LR * (mh / (jnp.sqrt(vh) + AEPS) + WD * w), m2, v2


def adamw_big(w, m, v, parts, name):
    lyr, r, c = w.shape
    by_cols = c >= 512
    blk = (lyr, r, 256) if by_cols else (lyr, 256 if r % 256 == 0 else r, c)
    imap = (lambda i: (0, 0, i)) if by_cols else (lambda i: (0, i, 0))

    def body(w_ref, m_ref, v_ref, *rest):
        p_refs, (g_ref, d_ref, m2_ref, v2_ref) = rest[:lyr], rest[lyr:]
        for l in range(lyr):
            g = p_refs[l][0].astype(F32)
            for q in range(1, NDEV):
                g = g + p_refs[l][q].astype(F32)
            g_ref[l] = g
            d_ref[l], m2_ref[l], v2_ref[l] = _adamw_math(w_ref[l], m_ref[l], v_ref[l], g)

    spec = pl.BlockSpec(blk, imap)
    pspec = pl.BlockSpec((NDEV,) + blk[1:], imap)
    return pl.pallas_call(
        body, name=name, grid=((c // 256) if by_cols else (r // blk[1]),),
        in_specs=[spec, spec, spec] + [pspec] * lyr,
        out_specs=[spec] * 4, out_shape=[_sds((lyr, r, c), F32)] * 4, compiler_params=_cp(1),
    )(w, m, v, *parts)


def adamw_small(w, m, v, g, name):
    def body(w_ref, m_ref, v_ref, g_ref, d_ref, m2_ref, v2_ref):
        d_ref[...], m2_ref[...], v2_ref[...] = _adamw_math(w_ref[...], m_ref[...], v_ref[...], g_ref[...])

    return pl.pallas_call(body, name=name, out_shape=[_sds(w.shape, F32)] * 3)(w, m, v, g)


NAMES = ["meta_tokens", "norm_mix", "norm_ffn", "conv_w_in", "conv_b_in", "conv_dw", "conv_ln_g", "conv_ln_b", "conv_w_out",
         "conv_b_out", "kv_norm", "w_kv", "k_norm", "w_q", "q_norm", "attn_sinks", "w_o", "ffn_w_gate", "ffn_w_up", "ffn_w_down"]
REP_ROWS = 16


def _pad_cols(a, width):
    return jnp.pad(a, ((0, 0), (0, width - a.shape[1])))


def _pack_rep(p):
    rows = [p["norm_mix"], p["norm_ffn"], p["kv_norm"].reshape(1, D), _pad_cols(p["k_norm"].reshape(1, HD), D),
            _pad_cols(p["q_norm"], D), _pad_cols(p["attn_sinks"], D), jnp.zeros((2, D), F32)]
    return jnp.concatenate(rows, axis=0)


def _unpack_rep(a):
    return {"norm_mix": a[0:4], "norm_ffn": a[4:8], "kv_norm": a[8], "k_norm": a[9, :HD], "q_norm": a[10:12, :HD],
            "attn_sinks": a[12:14, :NH]}


SH_NAMES = ["meta_tokens", "conv_b_in", "conv_dw", "conv_ln_g", "conv_ln_b", "conv_b_out"]


def _pack_sh(p):
    c = D // NDEV
    rows = [p["meta_tokens"], p["conv_b_in"].reshape(4, c), p["conv_dw"].reshape(2 * CW, c), p["conv_ln_g"], p["conv_ln_b"],
            p["conv_b_out"]]
    return jnp.concatenate(rows, axis=0)


def _unpack_sh(a):
    c = D // NDEV
    return {"meta_tokens": a[0:16], "conv_b_in": a[16:20].reshape(2, 2 * c), "conv_dw": a[20:82].reshape(2, CW, c),
            "conv_ln_g": a[82:84], "conv_ln_b": a[84:86], "conv_b_out": a[86:88]}


def kernel(x, meta_tokens, norm_mix, norm_ffn, conv_w_in, conv_b_in, conv_dw, conv_ln_g, conv_ln_b, conv_w_out, conv_b_out, kv_norm, w_kv, k_norm, w_q, q_norm, attn_sinks, w_o, ffn_w_gate, ffn_w_up, ffn_w_down, loss_target, m_meta_tokens, m_norm_mix, m_norm_ffn, m_conv_w_in, m_conv_b_in, m_conv_dw, m_conv_ln_g, m_conv_ln_b, m_conv_w_out, m_conv_b_out, m_kv_norm, m_w_kv, m_k_norm, m_w_q, m_q_norm, m_attn_sinks, m_w_o, m_ffn_w_gate, m_ffn_w_up, m_ffn_w_down, v_meta_tokens, v_norm_mix, v_norm_ffn, v_conv_w_in, v_conv_b_in, v_conv_dw, v_conv_ln_g, v_conv_ln_b, v_conv_w_out, v_conv_b_out, v_kv_norm, v_w_kv, v_k_norm, v_w_q, v_q_norm, v_attn_sinks, v_w_o, v_ffn_w_gate, v_ffn_w_up, v_ffn_w_down):
    wts = dict(zip(NAMES, (meta_tokens, norm_mix, norm_ffn, conv_w_in, conv_b_in, conv_dw, conv_ln_g, conv_ln_b, conv_w_out,
                           conv_b_out, kv_norm, w_kv, k_norm, w_q, q_norm, attn_sinks, w_o, ffn_w_gate, ffn_w_up, ffn_w_down)))
    mom = dict(zip(NAMES, (m_meta_tokens, m_norm_mix, m_norm_ffn, m_conv_w_in, m_conv_b_in, m_conv_dw, m_conv_ln_g, m_conv_ln_b,
                           m_conv_w_out, m_conv_b_out, m_kv_norm, m_w_kv, m_k_norm, m_w_q, m_q_norm, m_attn_sinks, m_w_o,
                           m_ffn_w_gate, m_ffn_w_up, m_ffn_w_down)))
    var = dict(zip(NAMES, (v_meta_tokens, v_norm_mix, v_norm_ffn, v_conv_w_in, v_conv_b_in, v_conv_dw, v_conv_ln_g, v_conv_ln_b,
                           v_conv_w_out, v_conv_b_out, v_kv_norm, v_w_kv, v_k_norm, v_w_q, v_q_norm, v_attn_sinks, v_w_o,
                           v_ffn_w_gate, v_ffn_w_up, v_ffn_w_down)))
    me = _my_index()
    c8 = D // NDEV
    for k in TRANSPOSED:
        wts[k], mom[k], var[k] = (jnp.swapaxes(a, 1, 2) for a in (wts[k], mom[k], var[k]))

    big_names = list(BIG)
    layers = cast_bf16([wts[k] for k in big_names])
    shards = {(k, l): blk for k, per_layer in zip(big_names, layers) for l, blk in enumerate(per_layer)}
    first = [("conv_w_in", 0), ("conv_w_out", 0)]
    vec_names = ["meta_tokens", "conv_b_in", "conv_dw", "conv_ln_g", "conv_ln_b", "conv_b_out"]
    gathered = all_gather_blocks([shards[key] for key in first] + [wts[k] for k in vec_names])
    mats = {key: gathered_matrix(*key, b8) for key, b8 in zip(first, gathered)}
    full = dict(zip(vec_names, gathered[len(first):]))
    join_vec = lambda a: jnp.moveaxis(a, 0, -2).reshape(a.shape[1:-1] + (NDEV * a.shape[-1],))
    w = {}
    w["conv_b_in"] = join_vec(full["conv_b_in"]).reshape(2, 1, 2 * D)
    w["conv_dw"] = join_vec(full["conv_dw"])
    for k in ("conv_ln_g", "conv_ln_b", "conv_b_out"):
        w[k] = join_vec(full[k]).reshape(2, 1, D)
    w["norm_mix"] = norm_mix.reshape(4, 1, D)
    w["norm_ffn"] = norm_ffn.reshape(4, 1, D)
    w["kv_norm"] = kv_norm.reshape(1, D)
    w["k_norm"] = k_norm.reshape(1, HD)
    w["q_norm"] = q_norm.reshape(2, 1, HD)
    w["attn_sinks"] = attn_sinks.reshape(2, 1, NH)

    loss_blk, grad_x, _, arrived, gs = local_step(x, loss_target, full["meta_tokens"], w, mats, shards)

    stack = lambda k, n: jnp.concatenate([gs[f"{k}{i}"] for i in range(n)], axis=0)
    rep = {"norm_mix": stack("norm_mix", 4), "norm_ffn": stack("norm_ffn", 4), "kv_norm": gs["kv_norm"], "k_norm": gs["k_norm"],
           "q_norm": stack("q_norm", 2), "attn_sinks": stack("attn_sinks", 2)}
    loss_row = _pad_cols(loss_blk[0:1, 0:1], D)
    packed = jnp.concatenate(
        [_pack_rep(rep)[:14], loss_row, jnp.zeros((1, D), F32), gs["meta_tokens"], stack("conv_b_in", 2).reshape(4, D),
         stack("conv_dw", 2), stack("conv_ln_g", 2), stack("conv_ln_b", 2), stack("conv_b_out", 2)], axis=0)
    red = all_reduce_small(packed)
    loss = red[14, 0]
    cols = lambda a, width: lax.dynamic_slice_in_dim(a, me * width, width, axis=1)
    g_sh = jnp.concatenate(
        [cols(red[16:32], c8), cols(red[32:36].reshape(2, 2 * D), 2 * c8).reshape(4, c8), cols(red[36:98], c8),
         cols(red[98:100], c8), cols(red[100:102], c8), cols(red[102:104], c8)], axis=0)
    g_rep = red[0:REP_ROWS].at[14:].set(0.0)

    grads, delta, new_m, new_v = {}, {}, {}, {}
    for k in big_names:
        flat2 = wts[k].ndim == 2
        parts = [arrived[k]] if flat2 else [arrived[f"{k}{i}"] for i in range(wts[k].shape[0])]
        as3 = (lambda a: a[None]) if flat2 else (lambda a: a)
        outs = adamw_big(as3(wts[k]), as3(mom[k]), as3(var[k]), parts, "adamw_" + k)
        grads[k], delta[k], new_m[k], new_v[k] = [o[0] if flat2 else (jnp.swapaxes(o, 1, 2) if k in TRANSPOSED else o) for o in outs]
    d_rep, m_rep, v_rep = adamw_small(_pack_rep(wts), _pack_rep(mom), _pack_rep(var), g_rep, "adamw_rep")
    d_sh, m_sh, v_sh = adamw_small(_pack_sh(wts), _pack_sh(mom), _pack_sh(var), g_sh, "adamw_sh")
    for dst, a_rep, a_sh in ((grads, g_rep, g_sh), (delta, d_rep, d_sh), (new_m, m_rep, m_sh), (new_v, v_rep, v_sh)):
        dst.update(_unpack_rep(a_rep))
        dst.update(_unpack_sh(a_sh))
    return (loss, grad_x, *[grads[k] for k in NAMES], *[delta[k] for k in NAMES], *[new_m[k] for k in NAMES],
            *[new_v[k] for k in NAMES])
```

```python
import functools

import jax
import jax.numpy as jnp
from jax import lax
from jax.experimental import pallas as pl
from jax.experimental.pallas import tpu as pltpu

F32 = jnp.float32
BF = jnp.bfloat16

D = 1024
DFF = 2816
NH = 16
NKV = 4
HD = 64
KVD = NKV * HD
NMETA = 16
CW = 31
HALO = 32
CHUNK = 32
QB = 128
EPS = 1e-6
NEG = -1e30
NDEV = 8
SCALE = HD ** -0.5

LR, B1, B2, AEPS, WD, STEP = 0.001, 0.9, 0.999, 1e-08, 0.01, 10

VMEM_LIMIT = 56 * 2 ** 20
MESH = pl.DeviceIdType.MESH


def _cp(n):
    return pltpu.CompilerParams(dimension_semantics=("arbitrary",) * n, vmem_limit_bytes=VMEM_LIMIT)


def _row(tm, c):
    return pl.BlockSpec((tm, c), lambda i: (i, 0))


def _res(shape):
    return pl.BlockSpec(shape, lambda i: (0,) * len(shape), pipeline_mode=pl.Buffered(1))


def _lay(l, shape):
    return pl.BlockSpec((None,) + tuple(shape), lambda i: (l,) + (0,) * len(shape), pipeline_mode=pl.Buffered(1))


def _acc(shape):
    return pl.BlockSpec(shape, lambda i: (0,) * len(shape))


def _sds(shape, dt):
    return jax.ShapeDtypeStruct(tuple(shape), dt)


def _dot(a, b):
    return jnp.dot(a.astype(BF), b.astype(BF), preferred_element_type=F32)


def _dot_nt(a, b):
    return lax.dot_general(a.astype(BF), b.astype(BF), (((1,), (1,)), ((), ())), preferred_element_type=F32)


def _dot_tn(a, b):
    return lax.dot_general(a.astype(BF), b.astype(BF), (((0,), (0,)), ((), ())), preferred_element_type=F32)


def _rstd(x):
    return lax.rsqrt(jnp.mean(x * x, axis=-1, keepdims=True) + EPS)


def _rms_bwd(x, g, dy):
    r = _rstd(x)
    z = dy * g
    dx = r * z - x * (r * r * r * jnp.mean(z * x, axis=-1, keepdims=True))
    return dx, jnp.sum(dy * x * r, axis=0, keepdims=True)


def _sig(x):
    return jax.nn.sigmoid(x)


def _fold8(x):
    out = x[0:8]
    for k in range(1, x.shape[0] // 8):
        out = out + x[8 * k:8 * k + 8]
    return out


def _shifted(win):
    return [win] + [pltpu.roll(win, 2 * CHUNK - rho, 0) for rho in range(1, 8)]


def _tap(phases, o):
    return phases[o % 8][8 * (o // 8):8 * (o // 8) + CHUNK]


def _init(ref, first):
    @pl.when(first)
    def _():
        ref[...] = jnp.zeros_like(ref)


def _my_index():
    return 4 * lax.axis_index("x") + 2 * lax.axis_index("y") + lax.axis_index("c")


def _coords(idx):
    return (idx // 4, (idx // 2) % 2, idx % 2)


def _xch_shapes(xch):
    return [_sds((NDEV,) + ((a.shape[0] // NDEV, a.shape[1]) if k == "rows" else tuple(a.shape[1:])), a.dtype) for a, k in xch]


def _xch_scratch(n):
    return [pltpu.SemaphoreType.DMA((n, NDEV - 1)), pltpu.SemaphoreType.DMA((n, NDEV - 1)), pltpu.SemaphoreType.DMA((n,))]


def _xch_copies(kinds, srcs, outs, sems, arrivals):
    send_sems, recv_sems, local_sems = sems
    me = _my_index()

    def piece(a, p):
        if kinds[a] == "rows":
            r = srcs[a].shape[0] // NDEV
            return srcs[a].at[pl.ds(p * r, r), :]
        return srcs[a].at[p]

    def remote(a, k, src, slot):
        return pltpu.make_async_remote_copy(
            src_ref=src, dst_ref=outs[a].at[slot], send_sem=send_sems.at[a, k - 1], recv_sem=recv_sems.at[a, k - 1],
            device_id=_coords(me ^ k), device_id_type=MESH)

    n = len(kinds)
    local = [pltpu.make_async_copy(piece(a, me), outs[a].at[me], local_sems.at[a]) for a in range(n)]
    sends = [remote(a, k, piece(a, me ^ k), me) for a in range(n) for k in range(1, NDEV)]
    recvs = [remote(a, k, piece(a, me), me ^ k) for a in range(n) for k in range(1, NDEV)] if arrivals else []
    return local, sends, recvs


def _xch_start(kinds, srcs, outs, sems):
    local, sends, _ = _xch_copies(kinds, srcs, outs, sems, False)
    for cp in local + sends:
        cp.start()


def _xch_wait(kinds, srcs, outs, sems):
    local, sends, recvs = _xch_copies(kinds, srcs, outs, sems, True)
    for cp in recvs:
        cp.wait_recv()
    for cp in sends:
        cp.wait_send()
    for cp in local:
        cp.wait()


def _gat_copies(srcs, outs, sems):
    send_sems, recv_sems, local_sems = sems
    x, y, c = lax.axis_index("x"), lax.axis_index("y"), lax.axis_index("c")
    me, sibling = (x, y, c), (x, y, 1 - c)
    chips = [(1 - x, y), (x, 1 - y), (1 - x, 1 - y)]

    def copy(a, k, owner, to, from_block=False):
        slot = outs[a].at[4 * owner[0] + 2 * owner[1] + owner[2]]
        return pltpu.make_async_remote_copy(
            src_ref=srcs[a] if from_block else slot, dst_ref=slot, send_sem=send_sems.at[a, k], recv_sem=recv_sems.at[a, k],
            device_id=to, device_id_type=MESH)

    n = len(srcs)
    local = lambda: [pltpu.make_async_copy(srcs[a], outs[a].at[4 * x + 2 * y + c], local_sems.at[a]) for a in range(n)]
    first = lambda: [cp for a in range(n) for cp in
                     [copy(a, 0, me, sibling, True)] + [copy(a, 1 + j, me, (*chip, c), True) for j, chip in enumerate(chips)]]
    landed = lambda: [copy(a, 1 + j, (*chip, c), me) for a in range(n) for j, chip in enumerate(chips)]
    passed = lambda: [copy(a, 4 + j, (*chip, c), sibling) for a in range(n) for j, chip in enumerate(chips)]
    final = lambda: [cp for a in range(n) for cp in
                     [copy(a, 0, sibling, me)] + [copy(a, 4 + j, (*chip, 1 - c), me) for j, chip in enumerate(chips)]]
    return local, first, landed, passed, final


def _gat_start(srcs, outs, sems):
    local, first, _, _, _ = _gat_copies(srcs, outs, sems)
    for cp in local() + first():
        cp.start()


def _gat_forward(srcs, outs, sems):
    _, _, landed, passed, _ = _gat_copies(srcs, outs, sems)
    for got, on in zip(landed(), passed()):
        got.wait_recv()
        on.start()


def _gat_wait(srcs, outs, sems):
    local, first, _, passed, final = _gat_copies(srcs, outs, sems)
    for cp in final():
        cp.wait_recv()
    for cp in first() + passed():
        cp.wait_send()
    for cp in local():
        cp.wait()


def _call(body, name, grid, in_specs, out_specs, out_shape, args, scratch=(), xch=(), gat=()):
    n_in, n_out, n_x, n_g, n_s = len(in_specs), len(out_specs), len(xch), len(gat), len(scratch)
    kinds = [k for _, k in xch]
    total = 1
    for g in grid:
        total *= g

    def wrapped(*refs):
        ins, refs = refs[:n_in], refs[n_in:]
        x_src, refs = refs[:n_x], refs[n_x:]
        g_src, refs = refs[:n_g], refs[n_g:]
        outs, refs = refs[:n_out], refs[n_out:]
        x_out, refs = refs[:n_x], refs[n_x:]
        g_out, refs = refs[:n_g], refs[n_g:]
        own, refs = refs[:n_s], refs[n_s:]
        x_sems, g_sems = (refs[:3], refs[3:]) if n_x else ((), refs)
        step = pl.program_id(0)
        for d in range(1, len(grid)):
            step = step * grid[d] + pl.program_id(d)
        if n_x or n_g:
            @pl.when(step == 0)
            def _():
                if n_x:
                    _xch_start(kinds, x_src, x_out, x_sems)
                if n_g:
                    _gat_start(g_src, g_out, g_sems)

        body(*ins, *outs, *own)
        if n_g:
            @pl.when(step == max(total - 2, 0))
            def _():
                _gat_forward(g_src, g_out, g_sems)

        if n_x or n_g:
            @pl.when(step == total - 1)
            def _():
                if n_x:
                    _xch_wait(kinds, x_src, x_out, x_sems)
                if n_g:
                    _gat_wait(g_src, g_out, g_sems)

    any_spec = pl.BlockSpec(memory_space=pl.ANY)
    g_shapes = [_sds((NDEV,) + tuple(a.shape), a.dtype) for a in gat]
    res = pl.pallas_call(
        wrapped, name=name, grid=grid, in_specs=list(in_specs) + [any_spec] * (n_x + n_g),
        out_specs=list(out_specs) + [any_spec] * (n_x + n_g), out_shape=list(out_shape) + _xch_shapes(xch) + g_shapes,
        scratch_shapes=list(scratch) + (_xch_scratch(n_x) if n_x else []) + (_xch_scratch(n_g) if n_g else []),
        compiler_params=_cp(len(grid)),
    )(*args, *[a for a, _ in xch], *gat)
    return res[:n_out], res[n_out:n_out + n_x], res[n_out + n_x:]


def embed(x, meta8, lp):
    bl, seq, _ = x.shape
    c8 = D // NDEV
    cb = 2 * c8

    def body(x_ref, m_ref, h_ref):
        h_ref[0:NMETA, :] = jnp.concatenate([m_ref[0], m_ref[1]], axis=1)
        h_ref[NMETA:NMETA + seq, :] = x_ref[...]
        h_ref[NMETA + seq:, :] = jnp.zeros((lp - NMETA - seq, cb), F32)

    return pl.pallas_call(
        body, name="embed", grid=(bl, D // cb),
        in_specs=[pl.BlockSpec((None, seq, cb), lambda b, c: (b, 0, c)), pl.BlockSpec((2, NMETA, c8), lambda b, c: (c, 0, 0))],
        out_specs=pl.BlockSpec((None, lp, cb), lambda b, c: (b, 0, c)), out_shape=_sds((bl, lp, D), F32),
        compiler_params=_cp(2),
    )(x, meta8)


def conv_in_fwd(h, nm, l, w_in, b_in, i, tm):
    t = h.shape[0]

    def body(h_ref, g_ref, w_ref, b_ref, u_ref, big_ref, a_ref):
        x = h_ref[...]
        ub = (x * _rstd(x) * g_ref[...]).astype(BF)
        u_ref[...] = ub
        big = jnp.dot(ub, w_ref[...], preferred_element_type=F32) + b_ref[...]
        big_ref[...] = big
        a_ref[...] = big[:, :D] * _sig(big[:, D:])

    return pl.pallas_call(
        body, name=f"conv_in_fwd{i}", grid=(t // tm,),
        in_specs=[_row(tm, D), _lay(l, (1, D)), _res((D, 2 * D)), _lay(i, (1, 2 * D))],
        out_specs=[_row(tm, D), _row(tm, 2 * D), _row(tm, D)],
        out_shape=[_sds((t, D), BF), _sds((t, 2 * D), F32), _sds((t, D), F32)],
        compiler_params=_cp(1),
    )(h, nm, w_in, b_in)


def _prev_halo(tm):
    return pl.BlockSpec((HALO, D), lambda i: (jnp.maximum(i * (tm // HALO) - 1, 0), 0))


def _next_halo(tm, t):
    return pl.BlockSpec((HALO, D), lambda i: (jnp.minimum((i + 1) * (tm // HALO), t // HALO - 1), 0))


def conv_mid_fwd(a, dw, ln_g, ln_b, i, tm, tpb, gat):
    t = a.shape[0]

    def body(a_ref, halo_ref, dw_ref, g_ref, b_ref, c_ref, s_ref, ext):
        first = pl.program_id(0) % tpb == 0
        ext[0:HALO] = jnp.where(first, 0.0, halo_ref[...])
        ext[HALO:] = a_ref[...]

        def chunk(k, carry):
            r0 = pl.multiple_of(k * CHUNK, CHUNK)
            win = _shifted(ext[pl.ds(r0, 2 * CHUNK), :])
            c = jnp.zeros((CHUNK, D), F32)
            for j in range(CW):
                c = c + dw_ref[j:j + 1, :] * _tap(win, j + 2)
            c_ref[pl.ds(r0, CHUNK), :] = c
            mu = jnp.mean(c, axis=-1, keepdims=True)
            xc = c - mu
            n = xc * lax.rsqrt(jnp.mean(xc * xc, axis=-1, keepdims=True) + EPS) * g_ref[...] + b_ref[...]
            s_ref[pl.ds(r0, CHUNK), :] = (n * _sig(n)).astype(BF)
            return carry

        lax.fori_loop(0, tm // CHUNK, chunk, 0)

    return _call(
        body, f"conv_mid_fwd{i}", (t // tm,),
        [_row(tm, D), _prev_halo(tm), _lay(i, (CW, D)), _lay(i, (1, D)), _lay(i, (1, D))],
        [_row(tm, D), _row(tm, D)], [_sds((t, D), F32), _sds((t, D), BF)], (a, a, dw, ln_g, ln_b),
        scratch=[pltpu.VMEM((tm + HALO, D), F32)], gat=gat)


def mixer_out_fwd(h, s, w, lw, bias, nf, l, tm, name):
    t = h.shape[0]

    def body(*refs):
        if bias is None:
            h_ref, s_ref, w_ref, g_ref, h1_ref, u_ref = refs
            y = 0.0
        else:
            h_ref, s_ref, w_ref, b_ref, g_ref, h1_ref, u_ref = refs
            y = b_ref[...]
        h1 = h_ref[...] + (jnp.dot(s_ref[...], w_ref[...], preferred_element_type=F32) + y)
        h1_ref[...] = h1
        u_ref[...] = (h1 * _rstd(h1) * g_ref[...]).astype(BF)

    ins = [h, s, w] + ([] if bias is None else [bias]) + [nf]
    specs = [_row(tm, D), _row(tm, D), _res((D, D))] + ([] if bias is None else [_lay(lw, (1, D))]) + [_lay(l, (1, D))]
    return pl.pallas_call(
        body, name=name, grid=(t // tm,), in_specs=specs,
        out_specs=[_row(tm, D), _row(tm, D)], out_shape=[_sds((t, D), F32), _sds((t, D), BF)],
        compiler_params=_cp(1),
    )(*ins)


def ffn_up_fwd(u, wg, wu, l, tm, gat):
    t = u.shape[0]

    def body(u_ref, wg_ref, wu_ref, g_ref, up_ref, hid_ref):
        ub = u_ref[...]
        g = _dot_nt(ub, wg_ref[...])
        up = _dot_nt(ub, wu_ref[...])
        g_ref[...] = g.astype(BF)
        up_ref[...] = up.astype(BF)
        hid_ref[...] = (g * _sig(g) * up).astype(BF)

    return _call(
        body, f"ffn_up_fwd{l}", (t // tm,), [_row(tm, D), _res((DFF, D)), _res((DFF, D))],
        [_row(tm, DFF)] * 3, [_sds((t, DFF), BF)] * 3, (u, wg, wu), gat=gat)


def ffn_down_fwd(hid, h1, wd, l, tm):
    t = h1.shape[0]

    def body(hid_ref, h1_ref, w_ref, h2_ref):
        h2_ref[...] = h1_ref[...] + jnp.dot(hid_ref[...], w_ref[...], preferred_element_type=F32)

    return pl.pallas_call(
        body, name=f"ffn_down_fwd{l}", grid=(t // tm,),
        in_specs=[_row(tm, DFF), _row(tm, D), _res((DFF, D))],
        out_specs=_row(tm, D), out_shape=_sds((t, D), F32),
        compiler_params=_cp(1),
    )(hid, h1, wd)


def _seg_rms(x, g, nseg):
    outs = []
    for s in range(nseg):
        xs = x[:, HD * s:HD * s + HD]
        outs.append(xs * _rstd(xs) * g)
    return jnp.concatenate(outs, axis=1)


def kv_fwd(h, kvn, w_kv, kng, tm):
    t = h.shape[0]

    def body(h_ref, g_ref, w_ref, kg_ref, kn_ref, kv_ref, k_ref, v_ref):
        x = h_ref[...]
        kn = (x * _rstd(x) * g_ref[...]).astype(BF)
        kn_ref[...] = kn
        kv = jnp.dot(kn, w_ref[...], preferred_element_type=F32)
        kv_ref[...] = kv
        k_ref[...] = _seg_rms(kv[:, :KVD], kg_ref[...], NKV).astype(BF)
        v_ref[...] = kv[:, KVD:].astype(BF)

    return pl.pallas_call(
        body, name="kv_fwd", grid=(t // tm,),
        in_specs=[_row(tm, D), _res((1, D)), _res((D, 2 * KVD)), _res((1, HD))],
        out_specs=[_row(tm, D), _row(tm, 2 * KVD), _row(tm, KVD), _row(tm, KVD)],
        out_shape=[_sds((t, D), BF), _sds((t, 2 * KVD), F32), _sds((t, KVD), BF), _sds((t, KVD), BF)],
        compiler_params=_cp(1),
    )(h, kvn, w_kv, kng)


def q_fwd(h, nm, l, w_q, j, tm):
    t = h.shape[0]

    def body(h_ref, g_ref, w_ref, u_ref, q_ref):
        x = h_ref[...]
        ub = (x * _rstd(x) * g_ref[...]).astype(BF)
        u_ref[...] = ub
        q_ref[...] = jnp.dot(ub, w_ref[...], preferred_element_type=F32)

    return pl.pallas_call(
        body, name=f"q_fwd{j}", grid=(t // tm,),
        in_specs=[_row(tm, D), _lay(l, (1, D)), _res((D, D))],
        out_specs=[_row(tm, D), _row(tm, D)], out_shape=[_sds((t, D), BF), _sds((t, D), F32)],
        compiler_params=_cp(1),
    )(h, nm, w_q)


RQ = NH // NKV


def _attn_specs(nb, lp):
    cur = lambda c: pl.BlockSpec((QB, c), lambda b, n: (b * nb + n, 0))
    seq = pl.BlockSpec((None, lp, KVD), lambda b, n: (b, 0, 0))
    return cur, seq


NKEYS = 2 * QB + NMETA


def _attn_mask(n, start):
    shape = (RQ * QB, NKEYS)
    qpos = n * QB + (lax.broadcasted_iota(jnp.int32, shape, 0) & (QB - 1))
    col = lax.broadcasted_iota(jnp.int32, shape, 1)
    in_band = col < 2 * QB
    kpos = jnp.where(in_band, start + col, col - 2 * QB)
    return (kpos <= qpos) & ((col >= 2 * QB) | ((qpos - kpos < QB) & (kpos >= NMETA)))


def _keys(ref, band, gs):
    return jnp.concatenate([ref[band, gs], ref[0:NMETA, gs]], axis=0)


def _stack_heads(ref, g, fn):
    return jnp.concatenate([fn(ref[:, HD * (g * RQ + r):HD * (g * RQ + r) + HD]) for r in range(RQ)], axis=0)


def _stack_cols(ref, g):
    return jnp.concatenate([ref[:, g * RQ + r:g * RQ + r + 1] for r in range(RQ)], axis=0)


def _stack_sinks(sk_ref, g):
    return jnp.concatenate([jnp.broadcast_to(sk_ref[:, g * RQ + r:g * RQ + r + 1], (QB, 1)) for r in range(RQ)], axis=0)


def attn_fwd(q, k, v, qg, sinks, j, bl, lp, gat):
    t = q.shape[0]
    nb = lp // QB
    cur, seq = _attn_specs(nb, lp)

    def body(q_ref, k_ref, v_ref, qg_ref, sk_ref, o_ref, lse_ref):
        n = pl.program_id(1)
        start = pl.multiple_of(jnp.maximum(n - 1, 0) * QB, QB)
        mask = _attn_mask(n, start)
        band = pl.ds(start, 2 * QB)
        lane = lax.broadcasted_iota(jnp.int32, (QB, NH), 1)
        ones = jnp.ones((NKEYS, HD), BF)
        lse = jnp.zeros((QB, NH), F32)
        for g in range(NKV):
            gs = slice(HD * g, HD * g + HD)
            qn = _stack_heads(q_ref, g, lambda x: (x * _rstd(x) * qg_ref[...]).astype(BF))
            sink = _stack_sinks(sk_ref, g)
            s = jnp.where(mask, _dot_nt(qn, _keys(k_ref, band, gs)) * SCALE, NEG)
            mx = jnp.maximum(jnp.max(s, -1, keepdims=True), sink)
            oa = _dot(jnp.exp(s - mx), jnp.concatenate([_keys(v_ref, band, gs), ones], axis=1))
            den = oa[:, HD:HD + 1] + jnp.exp(sink - mx)
            o = oa[:, :HD] * (1.0 / den)
            l = mx + jnp.log(den)
            for r in range(RQ):
                h = g * RQ + r
                o_ref[:, HD * h:HD * h + HD] = o[r * QB:(r + 1) * QB].astype(BF)
                lse = jnp.where(lane == h, l[r * QB:(r + 1) * QB], lse)
        lse_ref[...] = lse

    return _call(
        body, f"attn_fwd{j}", (bl, nb),
        [cur(D), seq, seq, pl.BlockSpec((None, 1, HD), lambda b, n: (j, 0, 0)), pl.BlockSpec((None, 1, NH), lambda b, n: (j, 0, 0))],
        [cur(D), cur(NH)], [_sds((t, D), BF), _sds((t, NH), F32)], (q, k, v, qg, sinks), gat=gat)


def loss_fwd(h, tgt):
    bl, lp, _ = h.shape
    seq = tgt.shape[1]
    cb = 256

    def body(h_ref, t_ref, dh_ref, loss_ref):
        _init(loss_ref, (pl.program_id(0) == 0) & (pl.program_id(1) == 0))
        err = h_ref[NMETA:NMETA + seq, :] - t_ref[...]
        dh_ref[...] = jnp.zeros_like(dh_ref)
        dh_ref[NMETA:NMETA + seq, :] = err * (1.0 / D)
        loss_ref[...] += (0.5 / D) * jnp.sum(err * err)

    return pl.pallas_call(
        body, name="loss_fwd", grid=(bl, D // cb),
        in_specs=[pl.BlockSpec((None, lp, cb), lambda b, c: (b, 0, c)), pl.BlockSpec((None, seq, cb), lambda b, c: (b, 0, c))],
        out_specs=[pl.BlockSpec((None, lp, cb), lambda b, c: (b, 0, c)), pl.BlockSpec((8, 128), lambda b, c: (0, 0))],
        out_shape=[_sds((bl, lp, D), F32), _sds((8, 128), F32)],
        compiler_params=_cp(2),
    )(h, tgt)


def ffn_bwd_x(dh2, g, up, h1, nf, l, wd, wg, wu, tm, xch):
    t = dh2.shape[0]

    def body(dh2_ref, g_ref, up_ref, h1_ref, nf_ref, wd_ref, wg_ref, wu_ref, dg_ref, du_ref, dh1_ref, dnf_ref):
        _init(dnf_ref, pl.program_id(0) == 0)
        dh2v = dh2_ref[...]
        dhid = _dot_nt(dh2v, wd_ref[...])
        gv = g_ref[...].astype(F32)
        uv = up_ref[...].astype(F32)
        sg = _sig(gv)
        dgv = (dhid * uv * (sg * (1.0 + gv * (1.0 - sg)))).astype(BF)
        duv = (dhid * (gv * sg)).astype(BF)
        dg_ref[...] = dgv
        du_ref[...] = duv
        dnorm = _dot(dgv, wg_ref[...]) + _dot(duv, wu_ref[...])
        dx, dnf = _rms_bwd(h1_ref[...], nf_ref[...], dnorm)
        dh1_ref[...] = dh2v + dx
        dnf_ref[...] += dnf

    return _call(
        body, f"ffn_bwd_x{l}", (t // tm,),
        [_row(tm, D), _row(tm, DFF), _row(tm, DFF), _row(tm, D), _lay(l, (1, D)),
         _res((DFF, D)), _res((DFF, D)), _res((DFF, D))],
        [_row(tm, DFF), _row(tm, DFF), _row(tm, D), _acc((1, D))],
        [_sds((t, DFF), BF), _sds((t, DFF), BF), _sds((t, D), F32), _sds((1, D), F32)],
        (dh2, g, up, h1, nf, wd, wg, wu), xch=xch)


def mm_tn(x, dy, tm, name, split=False, transposed=False, xch=()):
    t, kk = x.shape
    nn = dy.shape[1]
    n8 = nn // NDEV
    nsteps = t // tm

    def body(x_ref, dy_ref, o_ref, acc):
        i = pl.program_id(0)
        _init(acc, i == 0)
        acc[...] += _dot_tn(x_ref[...], dy_ref[...])

        @pl.when(i == nsteps - 1)
        def _():
            if split:
                for p in range(NDEV):
                    o_ref[p] = acc[:, p * n8:(p + 1) * n8].astype(BF)
            elif transposed:
                o_ref[...] = acc[...].T.astype(BF)
            else:
                o_ref[...] = acc[...].astype(BF)

    oshape = (NDEV, kk, n8) if split else ((nn, kk) if transposed else (kk, nn))
    (out,), got, _ = _call(body, name, (nsteps,), [_row(tm, kk), _row(tm, nn)], [_acc(oshape)], [_sds(oshape, BF)], (x, dy),
                           scratch=[pltpu.VMEM((kk, nn), F32)], xch=xch)
    return (out, got) if xch else out


def proj_bwd(dy, w, h, g, lg, dh_in, tm, name):
    t = h.shape[0]
    nn = dy.shape[1]
    wspec = _res(w.shape)
    gspec = _res((1, D)) if lg is None else _lay(lg, (1, D))

    def body(dy_ref, w_ref, h_ref, g_ref, dhin_ref, dh_ref, dg_ref):
        _init(dg_ref, pl.program_id(0) == 0)
        du = _dot_nt(dy_ref[...], w_ref[...])
        dx, dg = _rms_bwd(h_ref[...], g_ref[...], du)
        dh_ref[...] = dhin_ref[...] + dx
        dg_ref[...] += dg

    return pl.pallas_call(
        body, name=name, grid=(t // tm,),
        in_specs=[_row(tm, nn), wspec, _row(tm, D), gspec, _row(tm, D)],
        out_specs=[_row(tm, D), _acc((1, D))], out_shape=[_sds((t, D), F32), _sds((1, D), F32)],
        compiler_params=_cp(1),
    )(dy, w, h, g, dh_in)


def out_proj_bwd(dh1, w, tm, name):
    t = dh1.shape[0]

    def body(dh1_ref, w_ref, do_ref):
        do_ref[...] = _dot_nt(dh1_ref[...], w_ref[...]).astype(BF)

    return pl.pallas_call(
        body, name=name, grid=(t // tm,), in_specs=[_row(tm, D), _res((D, D))],
        out_specs=_row(tm, D), out_shape=_sds((t, D), BF), compiler_params=_cp(1),
    )(dh1, w)


def attn_bwd(q, k, v, do, o, lse, qg, sinks, j, bl, lp, xch):
    t = q.shape[0]
    nb = lp // QB
    cur, seq = _attn_specs(nb, lp)

    def body(q_ref, k_ref, v_ref, do_ref, o_ref, lse_ref, qg_ref, sk_ref, dq_ref, dk_ref, dv_ref, dqg_ref, dsk_ref):
        b, n = pl.program_id(0), pl.program_id(1)
        _init(dk_ref, n == 0)
        _init(dv_ref, n == 0)
        _init(dqg_ref, (b == 0) & (n == 0))
        _init(dsk_ref, (b == 0) & (n == 0))
        start = pl.multiple_of(jnp.maximum(n - 1, 0) * QB, QB)
        mask = _attn_mask(n, start)
        band = pl.ds(start, 2 * QB)
        lane = lax.broadcasted_iota(jnp.int32, (1, NH), 1)
        dqg = jnp.zeros((1, HD), F32)
        dsk = jnp.zeros((1, NH), F32)
        dks, dvs = [], []
        for g in range(NKV):
            gs = slice(HD * g, HD * g + HD)
            kk, vv = _keys(k_ref, band, gs), _keys(v_ref, band, gs)
            qh = _stack_heads(q_ref, g, lambda x: x)
            rs = _rstd(qh)
            qn = (qh * rs * qg_ref[...]).astype(BF)
            ls = _stack_cols(lse_ref, g)
            pr = jnp.where(mask, jnp.exp(_dot_nt(qn, kk) * SCALE - ls), 0.0)
            doh = _stack_heads(do_ref, g, lambda x: x)
            delta = jnp.sum(doh.astype(F32) * _stack_heads(o_ref, g, lambda x: x).astype(F32), axis=-1, keepdims=True)
            ds = (pr * (_dot_nt(doh, vv) - delta)).astype(BF)
            dqn = _dot(ds, kk) * SCALE
            dks.append(_dot_tn(qn, ds).T * SCALE)
            dvs.append(_dot_tn(doh, pr).T)
            dsink = jnp.exp(_stack_sinks(sk_ref, g) - ls) * delta
            z = dqn * qg_ref[...]
            dq = rs * z - qh * (rs * rs * rs * jnp.mean(z * qh, axis=-1, keepdims=True))
            dqg = dqg + jnp.sum(dqn * qh * rs, axis=0, keepdims=True)
            for r in range(RQ):
                h = g * RQ + r
                dq_ref[:, HD * h:HD * h + HD] = dq[r * QB:(r + 1) * QB]
                dsk = dsk + jnp.where(lane == h, -jnp.sum(dsink[r * QB:(r + 1) * QB]), 0.0)
        dk = jnp.concatenate(dks, axis=1)
        dv = jnp.concatenate(dvs, axis=1)
        dk_ref[band, :] += dk[:2 * QB]
        dv_ref[band, :] += dv[:2 * QB]
        dk_ref[0:NMETA, :] += dk[2 * QB:]
        dv_ref[0:NMETA, :] += dv[2 * QB:]
        dqg_ref[...] += dqg
        dsk_ref[...] += dsk

    return _call(
        body, f"attn_bwd{j}", (bl, nb),
        [cur(D), seq, seq, cur(D), cur(D), cur(NH),
         pl.BlockSpec((None, 1, HD), lambda b, n: (j, 0, 0)), pl.BlockSpec((None, 1, NH), lambda b, n: (j, 0, 0))],
        [cur(D), seq, seq, pl.BlockSpec((1, HD), lambda b, n: (0, 0)), pl.BlockSpec((1, NH), lambda b, n: (0, 0))],
        [_sds((t, D), F32), _sds((bl, lp, KVD), F32), _sds((bl, lp, KVD), F32), _sds((1, HD), F32), _sds((1, NH), F32)],
        (q, k, v, do, o, lse, qg, sinks), xch=xch)


def kv_bwd_pre(dk0, dk1, dv0, dv1, kv, kng, tm):
    t = kv.shape[0]

    def body(dk0_ref, dk1_ref, dv0_ref, dv1_ref, kv_ref, g_ref, dkv_ref, dg_ref):
        _init(dg_ref, pl.program_id(0) == 0)
        dk = dk0_ref[...] + dk1_ref[...]
        dg = jnp.zeros((1, HD), F32)
        outs = []
        for s in range(NKV):
            sl = slice(HD * s, HD * s + HD)
            dx, dgs = _rms_bwd(kv_ref[:, sl], g_ref[...], dk[:, sl])
            outs.append(dx)
            dg = dg + dgs
        dkv_ref[:, :KVD] = jnp.concatenate(outs, axis=1).astype(BF)
        dkv_ref[:, KVD:] = (dv0_ref[...] + dv1_ref[...]).astype(BF)
        dg_ref[...] += dg

    return pl.pallas_call(
        body, name="kv_bwd_pre", grid=(t // tm,),
        in_specs=[_row(tm, KVD)] * 4 + [_row(tm, 2 * KVD), _res((1, HD))],
        out_specs=[_row(tm, 2 * KVD), _acc((1, HD))], out_shape=[_sds((t, 2 * KVD), BF), _sds((1, HD), F32)],
        compiler_params=_cp(1),
    )(dk0, dk1, dv0, dv1, kv, kng)


def conv_out_bwd(dh1, c, ln_g, ln_b, w_out, i, tm):
    t = dh1.shape[0]

    def body(dh1_ref, c_ref, g_ref, b_ref, w_ref, dc_ref, dg_ref, db_ref, dbo_ref):
        first = pl.program_id(0) == 0
        _init(dg_ref, first)
        _init(db_ref, first)
        _init(dbo_ref, first)
        dh1v = dh1_ref[...]
        ds = _dot_nt(dh1v, w_ref[...])
        cv = c_ref[...]
        xc = cv - jnp.mean(cv, axis=-1, keepdims=True)
        rstd = lax.rsqrt(jnp.mean(xc * xc, axis=-1, keepdims=True) + EPS)
        xh = xc * rstd
        n = xh * g_ref[...] + b_ref[...]
        sg = _sig(n)
        dn = ds * (sg * (1.0 + n * (1.0 - sg)))
        dxh = dn * g_ref[...]
        dc_ref[...] = rstd * (dxh - jnp.mean(dxh, axis=-1, keepdims=True) - xh * jnp.mean(dxh * xh, axis=-1, keepdims=True))
        dg_ref[...] += jnp.sum(dn * xh, axis=0, keepdims=True)
        db_ref[...] += jnp.sum(dn, axis=0, keepdims=True)
        dbo_ref[...] += jnp.sum(dh1v, axis=0, keepdims=True)

    return pl.pallas_call(
        body, name=f"conv_out_bwd{i}", grid=(t // tm,),
        in_specs=[_row(tm, D), _row(tm, D), _lay(i, (1, D)), _lay(i, (1, D)), _res((D, D))],
        out_specs=[_row(tm, D), _acc((1, D)), _acc((1, D)), _acc((1, D))],
        out_shape=[_sds((t, D), F32)] + [_sds((1, D), F32)] * 3,
        compiler_params=_cp(1),
    )(dh1, c, ln_g, ln_b, w_out)


def conv_mid_bwd(dc, a, big, dw, i, tm, tpb, xch):
    t = dc.shape[0]
    nsteps = t // tm

    def body(dc_ref, nxt_ref, a_ref, prv_ref, big_ref, dw_ref, da_ref, dbin_ref, ddw_ref, dce, ae, wacc, bacc):
        i_ = pl.program_id(0)
        _init(wacc, i_ == 0)
        _init(bacc, i_ == 0)
        dce[0:tm] = dc_ref[...]
        dce[tm:] = jnp.where(i_ % tpb == tpb - 1, 0.0, nxt_ref[...])
        ae[0:HALO] = jnp.where(i_ % tpb == 0, 0.0, prv_ref[...])
        ae[HALO:] = a_ref[...]

        def chunk(k, carry):
            r0 = pl.multiple_of(k * CHUNK, CHUNK)
            wdc = _shifted(dce[pl.ds(r0, 2 * CHUNK), :])
            wa = _shifted(ae[pl.ds(r0, 2 * CHUNK), :])
            dcc = wdc[0][0:CHUNK]
            da = jnp.zeros((CHUNK, D), F32)
            for j in range(CW):
                da = da + dw_ref[j:j + 1, :] * _tap(wdc, CW - 1 - j)
                wacc[j] += _fold8(dcc * _tap(wa, j + 2))
            bv = big_ref[pl.ds(r0, CHUNK), :]
            a1, sg = bv[:, :D], _sig(bv[:, D:])
            d1 = da * sg
            d2 = da * a1 * sg * (1.0 - sg)
            da_ref[pl.ds(r0, CHUNK), 0:D] = d1.astype(BF)
            da_ref[pl.ds(r0, CHUNK), D:2 * D] = d2.astype(BF)
            bacc[:, 0:D] += _fold8(d1)
            bacc[:, D:2 * D] += _fold8(d2)
            return carry

        lax.fori_loop(0, tm // CHUNK, chunk, 0)

        @pl.when(i_ == nsteps - 1)
        def _():
            dbin_ref[...] = jnp.sum(bacc[...], axis=0, keepdims=True)
            ddw_ref[...] = jnp.sum(wacc[...], axis=1)

    return _call(
        body, f"conv_mid_bwd{i}", (nsteps,),
        [_row(tm, D), _next_halo(tm, t), _row(tm, D), _prev_halo(tm), _row(tm, 2 * D), _lay(i, (CW, D))],
        [_row(tm, 2 * D), _acc((1, 2 * D)), _acc((CW + 1, D))],
        [_sds((t, 2 * D), BF), _sds((1, 2 * D), F32), _sds((CW + 1, D), F32)],
        (dc, dc, a, a, big, dw),
        scratch=[pltpu.VMEM((tm + HALO, D), F32), pltpu.VMEM((tm + HALO, D), F32),
                 pltpu.VMEM((CW + 1, 8, D), F32), pltpu.VMEM((8, 2 * D), F32)], xch=xch)


def input_grads(dh0, seq):
    bl, lp, _ = dh0.shape
    cb = 256

    def body(dh_ref, gx_ref, gm_ref):
        _init(gm_ref, pl.program_id(1) == 0)
        gx_ref[...] = dh_ref[NMETA:NMETA + seq, :]
        gm_ref[...] += dh_ref[0:NMETA, :]

    return pl.pallas_call(
        body, name="input_grads", grid=(D // cb, bl),
        in_specs=[pl.BlockSpec((None, lp, cb), lambda c, b: (b, 0, c))],
        out_specs=[pl.BlockSpec((None, seq, cb), lambda c, b: (b, 0, c)), pl.BlockSpec((NMETA, cb), lambda c, b: (0, c))],
        out_shape=[_sds((bl, seq, D), F32), _sds((NMETA, D), F32)],
        compiler_params=_cp(2),
    )(dh0)


GATHER_PLAN = {
    "conv_mid_fwd0": [("ffn_w_gate", 0), ("ffn_w_up", 0), ("ffn_w_down", 0)],
    "ffn_up_fwd0": [("conv_w_in", 1), ("conv_w_out", 1), ("ffn_w_gate", 1)],
    "conv_mid_fwd1": [("ffn_w_up", 1), ("ffn_w_down", 1), ("w_kv", 0), ("w_q", 0)],
    "ffn_up_fwd1": [("w_o", 0), ("ffn_w_down", 2)],
    "attn_fwd0": [("ffn_w_gate", 2), ("ffn_w_up", 2), ("w_q", 1), ("w_o", 1), ("ffn_w_gate", 3), ("ffn_w_up", 3), ("ffn_w_down", 3)],
}
BIG = {"conv_w_in": "pieces", "conv_w_out": "rows", "w_kv": "rows", "w_q": "rows", "w_o": "rows",
       "ffn_w_gate": "rows", "ffn_w_up": "rows", "ffn_w_down": "rows"}
TRANSPOSED = ("ffn_w_gate", "ffn_w_up")


def gathered_matrix(name, layer, blocks8):
    if BIG[name] == "rows":
        return blocks8.reshape(NDEV * blocks8.shape[1], blocks8.shape[2])
    return join_columns(blocks8, f"join_{name}{layer}")


def local_step(x, tgt, meta8, w, mats, shards):
    bl, seq, _ = x.shape
    lp = -(-(NMETA + seq) // QB) * QB
    tpb = 4
    tm = lp // tpb
    t = bl * lp
    na = 2
    flat = lambda a: a.reshape(t, D)
    mats = dict(mats)

    def riders(carrier):
        return [shards[key] for key in GATHER_PLAN[carrier]]

    def landed(carrier, blocks):
        for key, b8 in zip(GATHER_PLAN[carrier], blocks):
            mats[key] = gathered_matrix(*key, b8)

    h = flat(embed(x, meta8, lp))
    saved = []
    kvs = None
    for l in range(4):
        rec = {"h": h}
        if l < na:
            rec["u"], rec["big"], rec["a"] = conv_in_fwd(h, w["norm_mix"], l, mats["conv_w_in", l], w["conv_b_in"], l, tm)
            name = f"conv_mid_fwd{l}"
            (rec["c"], rec["s"]), _, got = conv_mid_fwd(rec["a"], w["conv_dw"], w["conv_ln_g"], w["conv_ln_b"], l, tm, tpb,
                                                         riders(name))
            landed(name, got)
            rec["h1"], rec["u2"] = mixer_out_fwd(h, rec["s"], mats["conv_w_out", l], l, w["conv_b_out"], w["norm_ffn"], l, tm,
                                                 f"conv_out_fwd{l}")
        else:
            j = l - na
            if kvs is None:
                kvs = dict(zip(("kn", "kv", "k", "v"), kv_fwd(h, w["kv_norm"], mats["w_kv", 0], w["k_norm"], tm)))
                kvs["h"] = h
                kvs["k3"], kvs["v3"] = kvs["k"].reshape(bl, lp, KVD), kvs["v"].reshape(bl, lp, KVD)
            rec["u"], rec["q"] = q_fwd(h, w["norm_mix"], l, mats["w_q", j], j, tm)
            name = f"attn_fwd{j}"
            (rec["o"], rec["lse"]), _, got = attn_fwd(rec["q"], kvs["k3"], kvs["v3"], w["q_norm"], w["attn_sinks"], j, bl, lp,
                                                      riders(name) if name in GATHER_PLAN else [])
            if name in GATHER_PLAN:
                landed(name, got)
            rec["h1"], rec["u2"] = mixer_out_fwd(h, rec["o"], mats["w_o", j], j, None, w["norm_ffn"], l, tm, f"attn_out_fwd{j}")
        name = f"ffn_up_fwd{l}"
        (rec["g"], rec["up"], rec["hid"]), _, got = ffn_up_fwd(rec["u2"], mats["ffn_w_gate", l], mats["ffn_w_up", l], l, tm // 2,
                                                              riders(name) if name in GATHER_PLAN else [])
        if name in GATHER_PLAN:
            landed(name, got)
        h = ffn_down_fwd(rec["hid"], rec["h1"], mats["ffn_w_down", l], l, tm)
        saved.append(rec)

    dh3, loss_blk = loss_fwd(h.reshape(bl, lp, D), tgt)
    dh = flat(dh3)

    big, small, arrived = {}, {}, {}
    dks, dvs = [], []
    pending = []

    def carried(names, arrivals):
        arrived.update(zip([nm for nm, _ in names], arrivals))

    behind_dw_down = []
    for l in reversed(range(4)):
        rec = saved[l]
        riders, pending = pending, []
        (dg, du, dh1, small[f"norm_ffn{l}"]), got, _ = ffn_bwd_x(
            dh, rec["g"], rec["up"], rec["h1"], w["norm_ffn"], l, mats["ffn_w_down", l], mats["ffn_w_gate", l], mats["ffn_w_up", l], tm // 2,
            [(big[nm], kind) for nm, kind in riders])
        carried(riders, got)
        riders, behind_dw_down = behind_dw_down, []
        res = mm_tn(rec["hid"], dh, 2 * tm, f"dw_down{l}", xch=[(big[nm], kind) for nm, kind in riders])
        big[f"ffn_w_down{l}"] = res[0] if riders else res
        if riders:
            carried(riders, res[1])
        big[f"ffn_w_gate{l}"] = mm_tn(rec["u2"], dg, 2 * tm, f"dw_gate{l}", transposed=True)
        big[f"ffn_w_up{l}"] = mm_tn(rec["u2"], du, 2 * tm, f"dw_up{l}", transposed=True)
        ffn = [(f"ffn_w_down{l}", "rows"), (f"ffn_w_gate{l}", "rows"), (f"ffn_w_up{l}", "rows")]
        riders = ffn if l >= na else ffn[:2]
        xch = [(big[nm], kind) for nm, kind in riders]
        if l >= na:
            j = l - na
            do = out_proj_bwd(dh1, mats["w_o", j], tm, f"attn_out_bwd{j}")
            big[f"w_o{j}"] = mm_tn(rec["o"], dh1, 2 * tm, f"dw_o{j}")
            (dq, dk, dv, small[f"q_norm{j}"], small[f"attn_sinks{j}"]), got, _ = attn_bwd(
                rec["q"], kvs["k3"], kvs["v3"], do, rec["o"], rec["lse"], w["q_norm"], w["attn_sinks"], j, bl, lp, xch)
            carried(riders, got)
            dks.append(dk.reshape(t, KVD))
            dvs.append(dv.reshape(t, KVD))
            big[f"w_q{j}"] = mm_tn(rec["u"], dq, 2 * tm, f"dw_q{j}")
            dh, small[f"norm_mix{l}"] = proj_bwd(dq, mats["w_q", j], rec["h"], w["norm_mix"], l, dh1, tm, f"q_bwd{j}")
            pending = [(f"w_o{j}", "rows"), (f"w_q{j}", "rows")]
            if l == na:
                dkv, small["k_norm"] = kv_bwd_pre(dks[0], dks[1], dvs[0], dvs[1], kvs["kv"], w["k_norm"], tm)
                big["w_kv"] = mm_tn(kvs["kn"], dkv, 2 * tm, "dw_kv")
                dh, small["kv_norm"] = proj_bwd(dkv, mats["w_kv", 0], kvs["h"], w["kv_norm"], None, dh, tm, "kv_bwd")
                pending.append(("w_kv", "rows"))
        else:
            dc, small[f"conv_ln_g{l}"], small[f"conv_ln_b{l}"], small[f"conv_b_out{l}"] = conv_out_bwd(
                dh1, rec["c"], w["conv_ln_g"], w["conv_ln_b"], mats["conv_w_out", l], l, tm)
            big[f"conv_w_out{l}"] = mm_tn(rec["s"], dh1, 2 * tm, f"dw_conv_out{l}")
            (da, small[f"conv_b_in{l}"], ddw), got, _ = conv_mid_bwd(dc, rec["a"], rec["big"], w["conv_dw"], l, tm, tpb, xch)
            carried(riders, got)
            small[f"conv_dw{l}"] = ddw[:CW]
            big[f"conv_w_in{l}"] = mm_tn(rec["u"], da, 2 * tm, f"dw_conv_in{l}", split=True)
            dh, small[f"norm_mix{l}"] = proj_bwd(da, mats["conv_w_in", l], rec["h"], w["norm_mix"], l, dh1, tm, f"conv_in_bwd{l}")
            pending = [ffn[2], (f"conv_w_out{l}", "rows")]
            behind_dw_down = [(f"conv_w_in{l}", "pieces")]
    pending += behind_dw_down
    grad_x, small["meta_tokens"] = input_grads(dh.reshape(bl, lp, D), seq)
    return loss_blk, grad_x, big, arrived, pending, small


def all_gather_blocks(blocks):
    n = len(blocks)

    def body(*refs):
        srcs, outs, sems = refs[:n], refs[n:2 * n], refs[2 * n:]
        _gat_start(srcs, outs, sems)
        _gat_forward(srcs, outs, sems)
        _gat_wait(srcs, outs, sems)

    any_spec = pl.BlockSpec(memory_space=pl.ANY)
    return pl.pallas_call(
        body, name="all_gather_blocks", out_shape=[_sds((NDEV,) + tuple(a.shape), a.dtype) for a in blocks],
        in_specs=[any_spec] * n, out_specs=[any_spec] * n, scratch_shapes=_xch_scratch(n),
    )(*blocks)


def cast_bf16(ws):
    n = len(ws)
    counts = [1 if x.ndim == 2 else x.shape[0] for x in ws]

    def body(*refs):
        outs = iter(refs[n:])
        for a in range(n):
            for l in range(counts[a]):
                next(outs)[...] = (refs[a][...] if ws[a].ndim == 2 else refs[a][l]).astype(BF)

    flat = pl.pallas_call(
        body, name="cast_bf16", out_shape=[_sds(x.shape[-2:], BF) for x, k in zip(ws, counts) for _ in range(k)],
        compiler_params=pltpu.CompilerParams(vmem_limit_bytes=VMEM_LIMIT),
    )(*ws)
    it = iter(flat)
    return [[next(it) for _ in range(k)] for k in counts]


def join_columns(w8, name):
    _, kk, n8 = w8.shape

    def body(x_ref, o_ref):
        o_ref[...] = jnp.concatenate([x_ref[p] for p in range(NDEV)], axis=1)

    return pl.pallas_call(body, name=name, out_shape=_sds((kk, NDEV * n8), w8.dtype),
                          compiler_params=pltpu.CompilerParams(vmem_limit_bytes=VMEM_LIMIT))(w8)


def _adamw_math(w, m, v, g):
    m2 = B1 * m + (1.0 - B1) * g
    v2 = B2 * v + (1.0 - B2) * (g * g)
    mh = m2 / (1.0 - B1 ** STEP)
    vh = v2 / (1.0 - B2 ** STEP)
    return -LR * (mh / (jnp.sqrt(vh) + AEPS) + WD * w), m2, v2


def adamw_big(w, m, v, parts, name, xch=(), gat=()):
    lyr, r, c = w.shape
    by_cols = c >= 512
    blk = (lyr, r, 256) if by_cols else (lyr, 256 if r % 256 == 0 else r, c)
    imap = (lambda i: (0, 0, i)) if by_cols else (lambda i: (0, i, 0))

    def body(w_ref, m_ref, v_ref, *rest):
        p_refs, (g_ref, d_ref, m2_ref, v2_ref) = rest[:lyr], rest[lyr:]
        for l in range(lyr):
            g = p_refs[l][0].astype(F32)
            for q in range(1, NDEV):
                g = g + p_refs[l][q].astype(F32)
            g_ref[l] = g
            d_ref[l], m2_ref[l], v2_ref[l] = _adamw_math(w_ref[l], m_ref[l], v_ref[l], g)

    spec = pl.BlockSpec(blk, imap)
    pspec = pl.BlockSpec((NDEV,) + blk[1:], imap)
    return _call(body, name, ((c // 256) if by_cols else (r // blk[1]),), [spec, spec, spec] + [pspec] * lyr,
                 [spec] * 4, [_sds((lyr, r, c), F32)] * 4, (w, m, v, *parts), xch=xch, gat=gat)


def sum_slots(g8):
    def body(g_ref, o_ref):
        acc = g_ref[0]
        for q in range(1, NDEV):
            acc = acc + g_ref[q]
        o_ref[...] = acc

    return pl.pallas_call(body, name="sum_slots", out_shape=_sds(g8.shape[1:], F32))(g8)


def adamw_small(w, m, v, g, name):
    def body(w_ref, m_ref, v_ref, g_ref, d_ref, m2_ref, v2_ref):
        d_ref[...], m2_ref[...], v2_ref[...] = _adamw_math(w_ref[...], m_ref[...], v_ref[...], g_ref[...])

    return pl.pallas_call(body, name=name, out_shape=[_sds(w.shape, F32)] * 3)(w, m, v, g)


NAMES = ["meta_tokens", "norm_mix", "norm_ffn", "conv_w_in", "conv_b_in", "conv_dw", "conv_ln_g", "conv_ln_b", "conv_w_out",
         "conv_b_out", "kv_norm", "w_kv", "k_norm", "w_q", "q_norm", "attn_sinks", "w_o", "ffn_w_gate", "ffn_w_up", "ffn_w_down"]
REP_ROWS = 16


def _pad_cols(a, width):
    return jnp.pad(a, ((0, 0), (0, width - a.shape[1])))


def _pack_rep(p):
    rows = [p["norm_mix"], p["norm_ffn"], p["kv_norm"].reshape(1, D), _pad_cols(p["k_norm"].reshape(1, HD), D),
            _pad_cols(p["q_norm"], D), _pad_cols(p["attn_sinks"], D), jnp.zeros((2, D), F32)]
    return jnp.concatenate(rows, axis=0)


def _unpack_rep(a):
    return {"norm_mix": a[0:4], "norm_ffn": a[4:8], "kv_norm": a[8], "k_norm": a[9, :HD], "q_norm": a[10:12, :HD],
            "attn_sinks": a[12:14, :NH]}


SH_NAMES = ["meta_tokens", "conv_b_in", "conv_dw", "conv_ln_g", "conv_ln_b", "conv_b_out"]


def _pack_sh(p):
    c = D // NDEV
    rows = [p["meta_tokens"], p["conv_b_in"].reshape(4, c), p["conv_dw"].reshape(2 * CW, c), p["conv_ln_g"], p["conv_ln_b"],
            p["conv_b_out"]]
    return jnp.concatenate(rows, axis=0)


def _unpack_sh(a):
    c = D // NDEV
    return {"meta_tokens": a[0:16], "conv_b_in": a[16:20].reshape(2, 2 * c), "conv_dw": a[20:82].reshape(2, CW, c),
            "conv_ln_g": a[82:84], "conv_ln_b": a[84:86], "conv_b_out": a[86:88]}


def kernel(x, meta_tokens, norm_mix, norm_ffn, conv_w_in, conv_b_in, conv_dw, conv_ln_g, conv_ln_b, conv_w_out, conv_b_out, kv_norm, w_kv, k_norm, w_q, q_norm, attn_sinks, w_o, ffn_w_gate, ffn_w_up, ffn_w_down, loss_target, m_meta_tokens, m_norm_mix, m_norm_ffn, m_conv_w_in, m_conv_b_in, m_conv_dw, m_conv_ln_g, m_conv_ln_b, m_conv_w_out, m_conv_b_out, m_kv_norm, m_w_kv, m_k_norm, m_w_q, m_q_norm, m_attn_sinks, m_w_o, m_ffn_w_gate, m_ffn_w_up, m_ffn_w_down, v_meta_tokens, v_norm_mix, v_norm_ffn, v_conv_w_in, v_conv_b_in, v_conv_dw, v_conv_ln_g, v_conv_ln_b, v_conv_w_out, v_conv_b_out, v_kv_norm, v_w_kv, v_k_norm, v_w_q, v_q_norm, v_attn_sinks, v_w_o, v_ffn_w_gate, v_ffn_w_up, v_ffn_w_down):
    wts = dict(zip(NAMES, (meta_tokens, norm_mix, norm_ffn, conv_w_in, conv_b_in, conv_dw, conv_ln_g, conv_ln_b, conv_w_out,
                           conv_b_out, kv_norm, w_kv, k_norm, w_q, q_norm, attn_sinks, w_o, ffn_w_gate, ffn_w_up, ffn_w_down)))
    mom = dict(zip(NAMES, (m_meta_tokens, m_norm_mix, m_norm_ffn, m_conv_w_in, m_conv_b_in, m_conv_dw, m_conv_ln_g, m_conv_ln_b,
                           m_conv_w_out, m_conv_b_out, m_kv_norm, m_w_kv, m_k_norm, m_w_q, m_q_norm, m_attn_sinks, m_w_o,
                           m_ffn_w_gate, m_ffn_w_up, m_ffn_w_down)))
    var = dict(zip(NAMES, (v_meta_tokens, v_norm_mix, v_norm_ffn, v_conv_w_in, v_conv_b_in, v_conv_dw, v_conv_ln_g, v_conv_ln_b,
                           v_conv_w_out, v_conv_b_out, v_kv_norm, v_w_kv, v_k_norm, v_w_q, v_q_norm, v_attn_sinks, v_w_o,
                           v_ffn_w_gate, v_ffn_w_up, v_ffn_w_down)))
    me = _my_index()
    c8 = D // NDEV
    for k in TRANSPOSED:
        wts[k], mom[k], var[k] = (jnp.swapaxes(a, 1, 2) for a in (wts[k], mom[k], var[k]))

    big_names = list(BIG)
    layers = cast_bf16([wts[k] for k in big_names])
    shards = {(k, l): blk for k, per_layer in zip(big_names, layers) for l, blk in enumerate(per_layer)}
    first = [("conv_w_in", 0), ("conv_w_out", 0)]
    vec_names = ["meta_tokens", "conv_b_in", "conv_dw", "conv_ln_g", "conv_ln_b", "conv_b_out"]
    gathered = all_gather_blocks([shards[key] for key in first] + [wts[k] for k in vec_names])
    mats = {key: gathered_matrix(*key, b8) for key, b8 in zip(first, gathered)}
    full = dict(zip(vec_names, gathered[len(first):]))
    join_vec = lambda a: jnp.moveaxis(a, 0, -2).reshape(a.shape[1:-1] + (NDEV * a.shape[-1],))
    w = {}
    w["conv_b_in"] = join_vec(full["conv_b_in"]).reshape(2, 1, 2 * D)
    w["conv_dw"] = join_vec(full["conv_dw"])
    for k in ("conv_ln_g", "conv_ln_b", "conv_b_out"):
        w[k] = join_vec(full[k]).reshape(2, 1, D)
    w["norm_mix"] = norm_mix.reshape(4, 1, D)
    w["norm_ffn"] = norm_ffn.reshape(4, 1, D)
    w["kv_norm"] = kv_norm.reshape(1, D)
    w["k_norm"] = k_norm.reshape(1, HD)
    w["q_norm"] = q_norm.reshape(2, 1, HD)
    w["attn_sinks"] = attn_sinks.reshape(2, 1, NH)

    loss_blk, grad_x, gbig, arrived, leftover, gs = local_step(x, loss_target, full["meta_tokens"], w, mats, shards)

    stack = lambda k, n: jnp.concatenate([gs[f"{k}{i}"] for i in range(n)], axis=0)
    rep = {"norm_mix": stack("norm_mix", 4), "norm_ffn": stack("norm_ffn", 4), "kv_norm": gs["kv_norm"], "k_norm": gs["k_norm"],
           "q_norm": stack("q_norm", 2), "attn_sinks": stack("attn_sinks", 2)}
    loss_row = _pad_cols(loss_blk[0:1, 0:1], D)
    packed = jnp.concatenate(
        [_pack_rep(rep)[:14], loss_row, jnp.zeros((1, D), F32), gs["meta_tokens"], stack("conv_b_in", 2).reshape(4, D),
         stack("conv_dw", 2), stack("conv_ln_g", 2), stack("conv_ln_b", 2), stack("conv_b_out", 2)], axis=0)

    grads, delta, new_m, new_v = {}, {}, {}, {}
    waiting = {nm.rstrip("0123456789") for nm, _ in leftover}
    leftover = sorted(leftover, key=lambda rider: -gbig[rider[0]].size)
    order = sorted([k for k in big_names if k not in waiting], key=lambda k: -wts[k].size) + [k for k in big_names if k in waiting]
    small8 = None
    for k in order:
        flat2 = wts[k].ndim == 2
        as3 = (lambda a: a[None]) if flat2 else (lambda a: a)
        riders = [leftover.pop(0)] if leftover and k not in waiting else []
        gat = [packed] if small8 is None and not riders else []
        parts = [arrived[k]] if flat2 else [arrived[f"{k}{i}"] for i in range(wts[k].shape[0])]
        outs, got_x, got_g = adamw_big(as3(wts[k]), as3(mom[k]), as3(var[k]), parts, "adamw_" + k,
                                       xch=[(gbig[nm], kind) for nm, kind in riders], gat=gat)
        arrived.update(zip([nm for nm, _ in riders], got_x))
        if gat:
            small8 = got_g[0]
        grads[k], delta[k], new_m[k], new_v[k] = [o[0] if flat2 else (jnp.swapaxes(o, 1, 2) if k in TRANSPOSED else o) for o in outs]
    assert not leftover and small8 is not None
    red = sum_slots(small8)
    loss = red[14, 0]
    cols = lambda a, width: lax.dynamic_slice_in_dim(a, me * width, width, axis=1)
    g_sh = jnp.concatenate(
        [cols(red[16:32], c8), cols(red[32:36].reshape(2, 2 * D), 2 * c8).reshape(4, c8), cols(red[36:98], c8),
         cols(red[98:100], c8), cols(red[100:102], c8), cols(red[102:104], c8)], axis=0)
    g_rep = red[0:REP_ROWS].at[14:].set(0.0)
    d_rep, m_rep, v_rep = adamw_small(_pack_rep(wts), _pack_rep(mom), _pack_rep(var), g_rep, "adamw_rep")
    d_sh, m_sh, v_sh = adamw_small(_pack_sh(wts), _pack_sh(mom), _pack_sh(var), g_sh, "adamw_sh")
    for dst, a_rep, a_sh in ((grads, g_rep, g_sh), (delta, d_rep, d_sh), (new_m, m_rep, m_sh), (new_v, v_rep, v_sh)):
        dst.update(_unpack_rep(a_rep))
        dst.update(_unpack_sh(a_sh))
    return (loss, grad_x, *[grads[k] for k in NAMES], *[delta[k] for k in NAMES], *[new_m[k] for k in NAMES],
            *[new_v[k] for k in NAMES])
```

```python
import functools

import jax
import jax.numpy as jnp
from jax import lax
from jax.experimental import pallas as pl
from jax.experimental.pallas import tpu as pltpu

F32 = jnp.float32
BF = jnp.bfloat16

D = 1024
DFF = 2816
NH = 16
NKV = 4
HD = 64
KVD = NKV * HD
NMETA = 16
CW = 31
HALO = 32
CHUNK = 32
QB = 128
EPS = 1e-6
NEG = -1e30
NDEV = 8
SCALE = HD ** -0.5

LR, B1, B2, AEPS, WD, STEP = 0.001, 0.9, 0.999, 1e-08, 0.01, 10

VMEM_LIMIT = 56 * 2 ** 20
MESH = pl.DeviceIdType.MESH


def _cp(n):
    return pltpu.CompilerParams(dimension_semantics=("arbitrary",) * n, vmem_limit_bytes=VMEM_LIMIT)


def _row(tm, c):
    return pl.BlockSpec((tm, c), lambda i: (i, 0))


def _res(shape):
    return pl.BlockSpec(shape, lambda i: (0,) * len(shape), pipeline_mode=pl.Buffered(1))


def _lay(l, shape):
    return pl.BlockSpec((None,) + tuple(shape), lambda i: (l,) + (0,) * len(shape), pipeline_mode=pl.Buffered(1))


def _acc(shape):
    return pl.BlockSpec(shape, lambda i: (0,) * len(shape))


def _sds(shape, dt):
    return jax.ShapeDtypeStruct(tuple(shape), dt)


def _dot(a, b):
    return jnp.dot(a.astype(BF), b.astype(BF), preferred_element_type=F32)


def _dot_nt(a, b):
    return lax.dot_general(a.astype(BF), b.astype(BF), (((1,), (1,)), ((), ())), preferred_element_type=F32)


def _dot_tn(a, b):
    return lax.dot_general(a.astype(BF), b.astype(BF), (((0,), (0,)), ((), ())), preferred_element_type=F32)


def _rstd(x):
    return lax.rsqrt(jnp.mean(x * x, axis=-1, keepdims=True) + EPS)


def _rms_bwd(x, g, dy):
    r = _rstd(x)
    z = dy * g
    dx = r * z - x * (r * r * r * jnp.mean(z * x, axis=-1, keepdims=True))
    return dx, jnp.sum(dy * x * r, axis=0, keepdims=True)


def _sig(x):
    return jax.nn.sigmoid(x)


def _fold8(x):
    out = x[0:8]
    for k in range(1, x.shape[0] // 8):
        out = out + x[8 * k:8 * k + 8]
    return out


def _shifted(win):
    return [win] + [pltpu.roll(win, 2 * CHUNK - rho, 0) for rho in range(1, 8)]


def _tap(phases, o):
    return phases[o % 8][8 * (o // 8):8 * (o // 8) + CHUNK]


def _init(ref, first):
    @pl.when(first)
    def _():
        ref[...] = jnp.zeros_like(ref)


def _my_index():
    return 4 * lax.axis_index("x") + 2 * lax.axis_index("y") + lax.axis_index("c")


def _coords(idx):
    return (idx // 4, (idx // 2) % 2, idx % 2)


def _xch_shapes(xch):
    return [_sds((NDEV,) + ((a.shape[0] // NDEV, a.shape[1]) if k == "rows" else tuple(a.shape[1:])), a.dtype) for a, k in xch]


def _xch_scratch(n):
    return [pltpu.SemaphoreType.DMA((n, NDEV - 1)), pltpu.SemaphoreType.DMA((n, NDEV - 1)), pltpu.SemaphoreType.DMA((n,))]


def _xch_copies(kinds, srcs, outs, sems, arrivals):
    send_sems, recv_sems, local_sems = sems
    me = _my_index()

    def piece(a, p):
        if kinds[a] == "rows":
            r = srcs[a].shape[0] // NDEV
            return srcs[a].at[pl.ds(p * r, r), :]
        return srcs[a].at[p]

    def remote(a, k, src, slot):
        return pltpu.make_async_remote_copy(
            src_ref=src, dst_ref=outs[a].at[slot], send_sem=send_sems.at[a, k - 1], recv_sem=recv_sems.at[a, k - 1],
            device_id=_coords(me ^ k), device_id_type=MESH)

    n = len(kinds)
    local = [pltpu.make_async_copy(piece(a, me), outs[a].at[me], local_sems.at[a]) for a in range(n)]
    sends = [remote(a, k, piece(a, me ^ k), me) for a in range(n) for k in range(1, NDEV)]
    recvs = [remote(a, k, piece(a, me), me ^ k) for a in range(n) for k in range(1, NDEV)] if arrivals else []
    return local, sends, recvs


def _xch_start(kinds, srcs, outs, sems):
    local, sends, _ = _xch_copies(kinds, srcs, outs, sems, False)
    for cp in local + sends:
        cp.start()


def _xch_wait(kinds, srcs, outs, sems):
    local, sends, recvs = _xch_copies(kinds, srcs, outs, sems, True)
    for cp in recvs:
        cp.wait_recv()
    for cp in sends:
        cp.wait_send()
    for cp in local:
        cp.wait()


def _gat_copies(srcs, outs, sems):
    send_sems, recv_sems, local_sems = sems
    x, y, c = lax.axis_index("x"), lax.axis_index("y"), lax.axis_index("c")
    me, sibling = (x, y, c), (x, y, 1 - c)
    chips = [(1 - x, y), (x, 1 - y), (1 - x, 1 - y)]

    def copy(a, k, owner, to, from_block=False):
        slot = outs[a].at[4 * owner[0] + 2 * owner[1] + owner[2]]
        return pltpu.make_async_remote_copy(
            src_ref=srcs[a] if from_block else slot, dst_ref=slot, send_sem=send_sems.at[a, k], recv_sem=recv_sems.at[a, k],
            device_id=to, device_id_type=MESH)

    n = len(srcs)
    local = lambda: [pltpu.make_async_copy(srcs[a], outs[a].at[4 * x + 2 * y + c], local_sems.at[a]) for a in range(n)]
    first = lambda: [cp for a in range(n) for cp in
                     [copy(a, 0, me, sibling, True)] + [copy(a, 1 + j, me, (*chip, c), True) for j, chip in enumerate(chips)]]
    landed = lambda: [copy(a, 1 + j, (*chip, c), me) for a in range(n) for j, chip in enumerate(chips)]
    passed = lambda: [copy(a, 4 + j, (*chip, c), sibling) for a in range(n) for j, chip in enumerate(chips)]
    final = lambda: [cp for a in range(n) for cp in
                     [copy(a, 0, sibling, me)] + [copy(a, 4 + j, (*chip, 1 - c), me) for j, chip in enumerate(chips)]]
    return local, first, landed, passed, final


def _gat_start(srcs, outs, sems):
    local, first, _, _, _ = _gat_copies(srcs, outs, sems)
    for cp in local() + first():
        cp.start()


def _gat_forward(srcs, outs, sems):
    _, _, landed, passed, _ = _gat_copies(srcs, outs, sems)
    for got, on in zip(landed(), passed()):
        got.wait_recv()
        on.start()


def _gat_wait(srcs, outs, sems):
    local, first, _, passed, final = _gat_copies(srcs, outs, sems)
    for cp in final():
        cp.wait_recv()
    for cp in first() + passed():
        cp.wait_send()
    for cp in local():
        cp.wait()


def _call(body, name, grid, in_specs, out_specs, out_shape, args, scratch=(), xch=(), gat=()):
    n_in, n_out, n_x, n_g, n_s = len(in_specs), len(out_specs), len(xch), len(gat), len(scratch)
    kinds = [k for _, k in xch]
    total = 1
    for g in grid:
        total *= g

    def wrapped(*refs):
        ins, refs = refs[:n_in], refs[n_in:]
        x_src, refs = refs[:n_x], refs[n_x:]
        g_src, refs = refs[:n_g], refs[n_g:]
        outs, refs = refs[:n_out], refs[n_out:]
        x_out, refs = refs[:n_x], refs[n_x:]
        g_out, refs = refs[:n_g], refs[n_g:]
        own, refs = refs[:n_s], refs[n_s:]
        x_sems, g_sems = (refs[:3], refs[3:]) if n_x else ((), refs)
        step = pl.program_id(0)
        for d in range(1, len(grid)):
            step = step * grid[d] + pl.program_id(d)
        if n_x or n_g:
            @pl.when(step == 0)
            def _():
                if n_x:
                    _xch_start(kinds, x_src, x_out, x_sems)
                if n_g:
                    _gat_start(g_src, g_out, g_sems)

        body(*ins, *outs, *own)
        if n_g:
            @pl.when(step == max(total - 2, 0))
            def _():
                _gat_forward(g_src, g_out, g_sems)

        if n_x or n_g:
            @pl.when(step == total - 1)
            def _():
                if n_x:
                    _xch_wait(kinds, x_src, x_out, x_sems)
                if n_g:
                    _gat_wait(g_src, g_out, g_sems)

    any_spec = pl.BlockSpec(memory_space=pl.ANY)
    g_shapes = [_sds((NDEV,) + tuple(a.shape), a.dtype) for a in gat]
    res = pl.pallas_call(
        wrapped, name=name, grid=grid, in_specs=list(in_specs) + [any_spec] * (n_x + n_g),
        out_specs=list(out_specs) + [any_spec] * (n_x + n_g), out_shape=list(out_shape) + _xch_shapes(xch) + g_shapes,
        scratch_shapes=list(scratch) + (_xch_scratch(n_x) if n_x else []) + (_xch_scratch(n_g) if n_g else []),
        compiler_params=_cp(len(grid)),
    )(*args, *[a for a, _ in xch], *gat)
    return res[:n_out], res[n_out:n_out + n_x], res[n_out + n_x:]


def embed(x, meta8, lp):
    bl, seq, _ = x.shape
    c8 = D // NDEV
    cb = 2 * c8

    def body(x_ref, m_ref, h_ref):
        h_ref[0:NMETA, :] = jnp.concatenate([m_ref[0], m_ref[1]], axis=1)
        h_ref[NMETA:NMETA + seq, :] = x_ref[...]
        h_ref[NMETA + seq:, :] = jnp.zeros((lp - NMETA - seq, cb), F32)

    return pl.pallas_call(
        body, name="embed", grid=(bl, D // cb),
        in_specs=[pl.BlockSpec((None, seq, cb), lambda b, c: (b, 0, c)), pl.BlockSpec((2, NMETA, c8), lambda b, c: (c, 0, 0))],
        out_specs=pl.BlockSpec((None, lp, cb), lambda b, c: (b, 0, c)), out_shape=_sds((bl, lp, D), F32),
        compiler_params=_cp(2),
    )(x, meta8)


def conv_in_fwd(h, nm, l, w_in, b_in, i, tm):
    t = h.shape[0]

    def body(h_ref, g_ref, w_ref, b_ref, u_ref, big_ref, a_ref):
        x = h_ref[...]
        ub = (x * _rstd(x) * g_ref[...]).astype(BF)
        u_ref[...] = ub
        big = jnp.dot(ub, w_ref[...], preferred_element_type=F32) + b_ref[...]
        big_ref[...] = big
        a_ref[...] = big[:, :D] * _sig(big[:, D:])

    return pl.pallas_call(
        body, name=f"conv_in_fwd{i}", grid=(t // tm,),
        in_specs=[_row(tm, D), _lay(l, (1, D)), _res((D, 2 * D)), _lay(i, (1, 2 * D))],
        out_specs=[_row(tm, D), _row(tm, 2 * D), _row(tm, D)],
        out_shape=[_sds((t, D), BF), _sds((t, 2 * D), F32), _sds((t, D), F32)],
        compiler_params=_cp(1),
    )(h, nm, w_in, b_in)


def _prev_halo(tm):
    return pl.BlockSpec((HALO, D), lambda i: (jnp.maximum(i * (tm // HALO) - 1, 0), 0))


def _next_halo(tm, t):
    return pl.BlockSpec((HALO, D), lambda i: (jnp.minimum((i + 1) * (tm // HALO), t // HALO - 1), 0))


def conv_mid_fwd(a, dw, ln_g, ln_b, i, tm, tpb, gat):
    t = a.shape[0]

    def body(a_ref, halo_ref, dw_ref, g_ref, b_ref, c_ref, s_ref, ext):
        first = pl.program_id(0) % tpb == 0
        ext[0:HALO] = jnp.where(first, 0.0, halo_ref[...])
        ext[HALO:] = a_ref[...]

        def chunk(k, carry):
            r0 = pl.multiple_of(k * CHUNK, CHUNK)
            win = _shifted(ext[pl.ds(r0, 2 * CHUNK), :])
            c = jnp.zeros((CHUNK, D), F32)
            for j in range(CW):
                c = c + dw_ref[j:j + 1, :] * _tap(win, j + 2)
            c_ref[pl.ds(r0, CHUNK), :] = c
            mu = jnp.mean(c, axis=-1, keepdims=True)
            xc = c - mu
            n = xc * lax.rsqrt(jnp.mean(xc * xc, axis=-1, keepdims=True) + EPS) * g_ref[...] + b_ref[...]
            s_ref[pl.ds(r0, CHUNK), :] = (n * _sig(n)).astype(BF)
            return carry

        lax.fori_loop(0, tm // CHUNK, chunk, 0)

    return _call(
        body, f"conv_mid_fwd{i}", (t // tm,),
        [_row(tm, D), _prev_halo(tm), _lay(i, (CW, D)), _lay(i, (1, D)), _lay(i, (1, D))],
        [_row(tm, D), _row(tm, D)], [_sds((t, D), F32), _sds((t, D), BF)], (a, a, dw, ln_g, ln_b),
        scratch=[pltpu.VMEM((tm + HALO, D), F32)], gat=gat)


def mixer_ffn_fwd(h, s, w_out, lw, bias, nf, l, wg, wu, wd, tm, gat):
    t = h.shape[0]

    def body(*refs):
        if bias is None:
            h_ref, s_ref, w_ref, nf_ref, wg_ref, wu_ref, wd_ref, h1_ref, u_ref, g_ref, up_ref, hid_ref, h2_ref = refs
            y = 0.0
        else:
            h_ref, s_ref, w_ref, b_ref, nf_ref, wg_ref, wu_ref, wd_ref, h1_ref, u_ref, g_ref, up_ref, hid_ref, h2_ref = refs
            y = b_ref[...]
        h1 = h_ref[...] + (jnp.dot(s_ref[...], w_ref[...], preferred_element_type=F32) + y)
        h1_ref[...] = h1
        ub = (h1 * _rstd(h1) * nf_ref[...]).astype(BF)
        u_ref[...] = ub
        g = _dot_nt(ub, wg_ref[...])
        up = _dot_nt(ub, wu_ref[...])
        g_ref[...] = g.astype(BF)
        up_ref[...] = up.astype(BF)
        hid = (g * _sig(g) * up).astype(BF)
        hid_ref[...] = hid
        h2_ref[...] = h1 + jnp.dot(hid, wd_ref[...], preferred_element_type=F32)

    ins = [h, s, w_out] + ([] if bias is None else [bias]) + [nf, wg, wu, wd]
    specs = ([_row(tm, D), _row(tm, D), _res((D, D))] + ([] if bias is None else [_lay(lw, (1, D))])
             + [_lay(l, (1, D)), _res((DFF, D)), _res((DFF, D)), _res((DFF, D))])
    return _call(
        body, f"mixer_ffn_fwd{l}", (t // tm,), specs,
        [_row(tm, D), _row(tm, D), _row(tm, DFF), _row(tm, DFF), _row(tm, DFF), _row(tm, D)],
        [_sds((t, D), F32), _sds((t, D), BF), _sds((t, DFF), BF), _sds((t, DFF), BF), _sds((t, DFF), BF), _sds((t, D), F32)],
        ins, gat=gat)


def _seg_rms(x, g, nseg):
    outs = []
    for s in range(nseg):
        xs = x[:, HD * s:HD * s + HD]
        outs.append(xs * _rstd(xs) * g)
    return jnp.concatenate(outs, axis=1)


def kv_fwd(h, kvn, w_kv, kng, tm):
    t = h.shape[0]

    def body(h_ref, g_ref, w_ref, kg_ref, kn_ref, kv_ref, k_ref, v_ref):
        x = h_ref[...]
        kn = (x * _rstd(x) * g_ref[...]).astype(BF)
        kn_ref[...] = kn
        kv = jnp.dot(kn, w_ref[...], preferred_element_type=F32)
        kv_ref[...] = kv
        k_ref[...] = _seg_rms(kv[:, :KVD], kg_ref[...], NKV).astype(BF)
        v_ref[...] = kv[:, KVD:].astype(BF)

    return pl.pallas_call(
        body, name="kv_fwd", grid=(t // tm,),
        in_specs=[_row(tm, D), _res((1, D)), _res((D, 2 * KVD)), _res((1, HD))],
        out_specs=[_row(tm, D), _row(tm, 2 * KVD), _row(tm, KVD), _row(tm, KVD)],
        out_shape=[_sds((t, D), BF), _sds((t, 2 * KVD), F32), _sds((t, KVD), BF), _sds((t, KVD), BF)],
        compiler_params=_cp(1),
    )(h, kvn, w_kv, kng)


def q_fwd(h, nm, l, w_q, j, tm):
    t = h.shape[0]

    def body(h_ref, g_ref, w_ref, u_ref, q_ref):
        x = h_ref[...]
        ub = (x * _rstd(x) * g_ref[...]).astype(BF)
        u_ref[...] = ub
        q_ref[...] = jnp.dot(ub, w_ref[...], preferred_element_type=F32)

    return pl.pallas_call(
        body, name=f"q_fwd{j}", grid=(t // tm,),
        in_specs=[_row(tm, D), _lay(l, (1, D)), _res((D, D))],
        out_specs=[_row(tm, D), _row(tm, D)], out_shape=[_sds((t, D), BF), _sds((t, D), F32)],
        compiler_params=_cp(1),
    )(h, nm, w_q)


RQ = NH // NKV


def _attn_specs(nb, lp):
    cur = lambda c: pl.BlockSpec((QB, c), lambda b, n: (b * nb + n, 0))
    seq = pl.BlockSpec((None, lp, KVD), lambda b, n: (b, 0, 0))
    seq_t = pl.BlockSpec((None, KVD, lp), lambda b, n: (b, 0, 0))
    return cur, seq, seq_t


NKEYS = 2 * QB + NMETA


def _attn_mask(n, start):
    shape = (RQ * QB, NKEYS)
    qpos = n * QB + (lax.broadcasted_iota(jnp.int32, shape, 0) & (QB - 1))
    col = lax.broadcasted_iota(jnp.int32, shape, 1)
    in_band = col < 2 * QB
    kpos = jnp.where(in_band, start + col, col - 2 * QB)
    return (kpos <= qpos) & ((col >= 2 * QB) | ((qpos - kpos < QB) & (kpos >= NMETA)))


def _keys(ref, band, gs):
    return jnp.concatenate([ref[band, gs], ref[0:NMETA, gs]], axis=0)


def _keys_t(ref, band, gs):
    return jnp.concatenate([ref[gs, band], ref[gs, 0:NMETA]], axis=1)


def transpose_seq(a, name):
    bl, r, c = a.shape

    def body(a_ref, o_ref):
        o_ref[...] = a_ref[...].T

    return pl.pallas_call(
        body, name=name, grid=(bl,), in_specs=[pl.BlockSpec((None, r, c), lambda b: (b, 0, 0))],
        out_specs=pl.BlockSpec((None, c, r), lambda b: (b, 0, 0)), out_shape=_sds((bl, c, r), a.dtype), compiler_params=_cp(1),
    )(a)


def sum_transposed(a0, a1):
    bl, c, r = a0.shape

    def body(a0_ref, a1_ref, o_ref):
        o_ref[...] = (a0_ref[...] + a1_ref[...]).T

    spec = pl.BlockSpec((None, c, r), lambda b: (b, 0, 0))
    return pl.pallas_call(
        body, name="sum_transposed", grid=(bl,), in_specs=[spec, spec],
        out_specs=pl.BlockSpec((r, c), lambda b: (b, 0)), out_shape=_sds((bl * r, c), a0.dtype), compiler_params=_cp(1),
    )(a0, a1)


def _stack_heads(ref, g, fn):
    return jnp.concatenate([fn(ref[:, HD * (g * RQ + r):HD * (g * RQ + r) + HD]) for r in range(RQ)], axis=0)


def _stack_cols(ref, g):
    return jnp.concatenate([ref[:, g * RQ + r:g * RQ + r + 1] for r in range(RQ)], axis=0)


def _stack_sinks(sk_ref, g):
    return jnp.concatenate([jnp.broadcast_to(sk_ref[:, g * RQ + r:g * RQ + r + 1], (QB, 1)) for r in range(RQ)], axis=0)


def attn_fwd(q, kt, v, qg, sinks, j, bl, lp, gat):
    t = q.shape[0]
    nb = lp // QB
    cur, seq, seq_t = _attn_specs(nb, lp)

    def body(q_ref, kt_ref, v_ref, qg_ref, sk_ref, o_ref, lse_ref):
        n = pl.program_id(1)
        start = pl.multiple_of(jnp.maximum(n - 1, 0) * QB, QB)
        mask = _attn_mask(n, start)
        band = pl.ds(start, 2 * QB)
        lane = lax.broadcasted_iota(jnp.int32, (QB, NH), 1)
        ones = jnp.ones((NKEYS, HD), BF)
        lse = jnp.zeros((QB, NH), F32)
        for g in range(NKV):
            gs = slice(HD * g, HD * g + HD)
            qn = _stack_heads(q_ref, g, lambda x: (x * _rstd(x) * qg_ref[...]).astype(BF))
            sink = _stack_sinks(sk_ref, g)
            s = jnp.where(mask, _dot(qn, _keys_t(kt_ref, band, gs)) * SCALE, NEG)
            mx = jnp.maximum(jnp.max(s, -1, keepdims=True), sink)
            oa = _dot(jnp.exp(s - mx), jnp.concatenate([_keys(v_ref, band, gs), ones], axis=1))
            den = oa[:, HD:HD + 1] + jnp.exp(sink - mx)
            o = oa[:, :HD] * (1.0 / den)
            l = mx + jnp.log(den)
            for r in range(RQ):
                h = g * RQ + r
                o_ref[:, HD * h:HD * h + HD] = o[r * QB:(r + 1) * QB].astype(BF)
                lse = jnp.where(lane == h, l[r * QB:(r + 1) * QB], lse)
        lse_ref[...] = lse

    return _call(
        body, f"attn_fwd{j}", (bl, nb),
        [cur(D), seq_t, seq, pl.BlockSpec((None, 1, HD), lambda b, n: (j, 0, 0)), pl.BlockSpec((None, 1, NH), lambda b, n: (j, 0, 0))],
        [cur(D), cur(NH)], [_sds((t, D), BF), _sds((t, NH), F32)], (q, kt, v, qg, sinks), gat=gat)


def loss_fwd(h, tgt):
    bl, lp, _ = h.shape
    seq = tgt.shape[1]
    cb = 256

    def body(h_ref, t_ref, dh_ref, loss_ref):
        _init(loss_ref, (pl.program_id(0) == 0) & (pl.program_id(1) == 0))
        err = h_ref[NMETA:NMETA + seq, :] - t_ref[...]
        dh_ref[...] = jnp.zeros_like(dh_ref)
        dh_ref[NMETA:NMETA + seq, :] = err * (1.0 / D)
        loss_ref[...] += (0.5 / D) * jnp.sum(err * err)

    return pl.pallas_call(
        body, name="loss_fwd", grid=(bl, D // cb),
        in_specs=[pl.BlockSpec((None, lp, cb), lambda b, c: (b, 0, c)), pl.BlockSpec((None, seq, cb), lambda b, c: (b, 0, c))],
        out_specs=[pl.BlockSpec((None, lp, cb), lambda b, c: (b, 0, c)), pl.BlockSpec((8, 128), lambda b, c: (0, 0))],
        out_shape=[_sds((bl, lp, D), F32), _sds((8, 128), F32)],
        compiler_params=_cp(2),
    )(h, tgt)


def ffn_bwd_x(dh2, g, up, h1, nf, l, wd, wg, wu, tm, xch):
    t = dh2.shape[0]

    def body(dh2_ref, g_ref, up_ref, h1_ref, nf_ref, wd_ref, wg_ref, wu_ref, dg_ref, du_ref, dh1_ref, dnf_ref):
        _init(dnf_ref, pl.program_id(0) == 0)
        dh2v = dh2_ref[...]
        dhid = _dot_nt(dh2v, wd_ref[...])
        gv = g_ref[...].astype(F32)
        uv = up_ref[...].astype(F32)
        sg = _sig(gv)
        dgv = (dhid * uv * (sg * (1.0 + gv * (1.0 - sg)))).astype(BF)
        duv = (dhid * (gv * sg)).astype(BF)
        dg_ref[...] = dgv
        du_ref[...] = duv
        dnorm = _dot(dgv, wg_ref[...]) + _dot(duv, wu_ref[...])
        dx, dnf = _rms_bwd(h1_ref[...], nf_ref[...], dnorm)
        dh1_ref[...] = dh2v + dx
        dnf_ref[...] += dnf

    return _call(
        body, f"ffn_bwd_x{l}", (t // tm,),
        [_row(tm, D), _row(tm, DFF), _row(tm, DFF), _row(tm, D), _lay(l, (1, D)),
         _res((DFF, D)), _res((DFF, D)), _res((DFF, D))],
        [_row(tm, DFF), _row(tm, DFF), _row(tm, D), _acc((1, D))],
        [_sds((t, DFF), BF), _sds((t, DFF), BF), _sds((t, D), F32), _sds((1, D), F32)],
        (dh2, g, up, h1, nf, wd, wg, wu), xch=xch)


def mm_tn(x, dy, tm, name, split=False, transposed=False, xch=()):
    t, kk = x.shape
    nn = dy.shape[1]
    n8 = nn // NDEV
    nsteps = t // tm

    def body(x_ref, dy_ref, o_ref, acc):
        i = pl.program_id(0)
        _init(acc, i == 0)
        acc[...] += _dot_tn(x_ref[...], dy_ref[...])

        @pl.when(i == nsteps - 1)
        def _():
            if split:
                for p in range(NDEV):
                    o_ref[p] = acc[:, p * n8:(p + 1) * n8].astype(BF)
            elif transposed:
                o_ref[...] = acc[...].T.astype(BF)
            else:
                o_ref[...] = acc[...].astype(BF)

    oshape = (NDEV, kk, n8) if split else ((nn, kk) if transposed else (kk, nn))
    (out,), got, _ = _call(body, name, (nsteps,), [_row(tm, kk), _row(tm, nn)], [_acc(oshape)], [_sds(oshape, BF)], (x, dy),
                           scratch=[pltpu.VMEM((kk, nn), F32)], xch=xch)
    return (out, got) if xch else out


def proj_bwd(dy, w, h, g, lg, dh_in, tm, name):
    t = h.shape[0]
    nn = dy.shape[1]
    wspec = _res(w.shape)
    gspec = _res((1, D)) if lg is None else _lay(lg, (1, D))

    def body(dy_ref, w_ref, h_ref, g_ref, dhin_ref, dh_ref, dg_ref):
        _init(dg_ref, pl.program_id(0) == 0)
        du = _dot_nt(dy_ref[...], w_ref[...])
        dx, dg = _rms_bwd(h_ref[...], g_ref[...], du)
        dh_ref[...] = dhin_ref[...] + dx
        dg_ref[...] += dg

    return pl.pallas_call(
        body, name=name, grid=(t // tm,),
        in_specs=[_row(tm, nn), wspec, _row(tm, D), gspec, _row(tm, D)],
        out_specs=[_row(tm, D), _acc((1, D))], out_shape=[_sds((t, D), F32), _sds((1, D), F32)],
        compiler_params=_cp(1),
    )(dy, w, h, g, dh_in)


def out_proj_bwd(dh1, w, tm, name):
    t = dh1.shape[0]

    def body(dh1_ref, w_ref, do_ref):
        do_ref[...] = _dot_nt(dh1_ref[...], w_ref[...]).astype(BF)

    return pl.pallas_call(
        body, name=name, grid=(t // tm,), in_specs=[_row(tm, D), _res((D, D))],
        out_specs=_row(tm, D), out_shape=_sds((t, D), BF), compiler_params=_cp(1),
    )(dh1, w)


def attn_bwd(q, k, kt, vt, do, o, lse, qg, sinks, j, bl, lp, xch):
    t = q.shape[0]
    nb = lp // QB
    cur, seq, seq_t = _attn_specs(nb, lp)

    def body(q_ref, k_ref, kt_ref, vt_ref, do_ref, o_ref, lse_ref, qg_ref, sk_ref, dq_ref, dk_ref, dv_ref, dqg_ref, dsk_ref):
        b, n = pl.program_id(0), pl.program_id(1)
        _init(dk_ref, n == 0)
        _init(dv_ref, n == 0)
        _init(dqg_ref, (b == 0) & (n == 0))
        _init(dsk_ref, (b == 0) & (n == 0))
        start = pl.multiple_of(jnp.maximum(n - 1, 0) * QB, QB)
        mask = _attn_mask(n, start)
        band = pl.ds(start, 2 * QB)
        lane = lax.broadcasted_iota(jnp.int32, (1, NH), 1)
        dqg = jnp.zeros((1, HD), F32)
        dsk = jnp.zeros((1, NH), F32)
        for g in range(NKV):
            gs = slice(HD * g, HD * g + HD)
            kk = _keys(k_ref, band, gs)
            qh = _stack_heads(q_ref, g, lambda x: x)
            rs = _rstd(qh)
            qn = (qh * rs * qg_ref[...]).astype(BF)
            ls = _stack_cols(lse_ref, g)
            pr = jnp.where(mask, jnp.exp(_dot(qn, _keys_t(kt_ref, band, gs)) * SCALE - ls), 0.0)
            doh = _stack_heads(do_ref, g, lambda x: x)
            delta = jnp.sum(doh.astype(F32) * _stack_heads(o_ref, g, lambda x: x).astype(F32), axis=-1, keepdims=True)
            ds = (pr * (_dot(doh, _keys_t(vt_ref, band, gs)) - delta)).astype(BF)
            dqn = _dot(ds, kk) * SCALE
            dkt = _dot_tn(qn, ds) * SCALE
            dvt = _dot_tn(doh, pr)
            dk_ref[gs, band] += dkt[:, :2 * QB]
            dv_ref[gs, band] += dvt[:, :2 * QB]
            dk_ref[gs, 0:NMETA] += dkt[:, 2 * QB:]
            dv_ref[gs, 0:NMETA] += dvt[:, 2 * QB:]
            dsink = jnp.exp(_stack_sinks(sk_ref, g) - ls) * delta
            z = dqn * qg_ref[...]
            dq = rs * z - qh * (rs * rs * rs * jnp.mean(z * qh, axis=-1, keepdims=True))
            dqg = dqg + jnp.sum(dqn * qh * rs, axis=0, keepdims=True)
            for r in range(RQ):
                h = g * RQ + r
                dq_ref[:, HD * h:HD * h + HD] = dq[r * QB:(r + 1) * QB]
                dsk = dsk + jnp.where(lane == h, -jnp.sum(dsink[r * QB:(r + 1) * QB]), 0.0)
        dqg_ref[...] += dqg
        dsk_ref[...] += dsk

    return _call(
        body, f"attn_bwd{j}", (bl, nb),
        [cur(D), seq, seq_t, seq_t, cur(D), cur(D), cur(NH),
         pl.BlockSpec((None, 1, HD), lambda b, n: (j, 0, 0)), pl.BlockSpec((None, 1, NH), lambda b, n: (j, 0, 0))],
        [cur(D), seq_t, seq_t, pl.BlockSpec((1, HD), lambda b, n: (0, 0)), pl.BlockSpec((1, NH), lambda b, n: (0, 0))],
        [_sds((t, D), F32), _sds((bl, KVD, lp), F32), _sds((bl, KVD, lp), F32), _sds((1, HD), F32), _sds((1, NH), F32)],
        (q, k, kt, vt, do, o, lse, qg, sinks), xch=xch)


def kv_bwd_pre(dk, dv, kv, kng, tm):
    t = kv.shape[0]

    def body(dk_ref, dv_ref, kv_ref, g_ref, dkv_ref, dg_ref):
        _init(dg_ref, pl.program_id(0) == 0)
        dg = jnp.zeros((1, HD), F32)
        outs = []
        for s in range(NKV):
            sl = slice(HD * s, HD * s + HD)
            dx, dgs = _rms_bwd(kv_ref[:, sl], g_ref[...], dk_ref[:, sl])
            outs.append(dx)
            dg = dg + dgs
        dkv_ref[:, :KVD] = jnp.concatenate(outs, axis=1).astype(BF)
        dkv_ref[:, KVD:] = dv_ref[...].astype(BF)
        dg_ref[...] += dg

    return pl.pallas_call(
        body, name="kv_bwd_pre", grid=(t // tm,),
        in_specs=[_row(tm, KVD)] * 2 + [_row(tm, 2 * KVD), _res((1, HD))],
        out_specs=[_row(tm, 2 * KVD), _acc((1, HD))], out_shape=[_sds((t, 2 * KVD), BF), _sds((1, HD), F32)],
        compiler_params=_cp(1),
    )(dk, dv, kv, kng)


def conv_out_bwd(dh1, c, ln_g, ln_b, w_out, i, tm):
    t = dh1.shape[0]

    def body(dh1_ref, c_ref, g_ref, b_ref, w_ref, dc_ref, dg_ref, db_ref, dbo_ref):
        first = pl.program_id(0) == 0
        _init(dg_ref, first)
        _init(db_ref, first)
        _init(dbo_ref, first)
        dh1v = dh1_ref[...]
        ds = _dot_nt(dh1v, w_ref[...])
        cv = c_ref[...]
        xc = cv - jnp.mean(cv, axis=-1, keepdims=True)
        rstd = lax.rsqrt(jnp.mean(xc * xc, axis=-1, keepdims=True) + EPS)
        xh = xc * rstd
        n = xh * g_ref[...] + b_ref[...]
        sg = _sig(n)
        dn = ds * (sg * (1.0 + n * (1.0 - sg)))
        dxh = dn * g_ref[...]
        dc_ref[...] = rstd * (dxh - jnp.mean(dxh, axis=-1, keepdims=True) - xh * jnp.mean(dxh * xh, axis=-1, keepdims=True))
        dg_ref[...] += jnp.sum(dn * xh, axis=0, keepdims=True)
        db_ref[...] += jnp.sum(dn, axis=0, keepdims=True)
        dbo_ref[...] += jnp.sum(dh1v, axis=0, keepdims=True)

    return pl.pallas_call(
        body, name=f"conv_out_bwd{i}", grid=(t // tm,),
        in_specs=[_row(tm, D), _row(tm, D), _lay(i, (1, D)), _lay(i, (1, D)), _res((D, D))],
        out_specs=[_row(tm, D), _acc((1, D)), _acc((1, D)), _acc((1, D))],
        out_shape=[_sds((t, D), F32)] + [_sds((1, D), F32)] * 3,
        compiler_params=_cp(1),
    )(dh1, c, ln_g, ln_b, w_out)


def conv_mid_bwd(dc, a, big, dw, i, tm, tpb, xch):
    t = dc.shape[0]
    nsteps = t // tm

    def body(dc_ref, nxt_ref, a_ref, prv_ref, big_ref, dw_ref, da_ref, dbin_ref, ddw_ref, dce, ae, wacc, bacc):
        i_ = pl.program_id(0)
        _init(wacc, i_ == 0)
        _init(bacc, i_ == 0)
        dce[0:tm] = dc_ref[...]
        dce[tm:] = jnp.where(i_ % tpb == tpb - 1, 0.0, nxt_ref[...])
        ae[0:HALO] = jnp.where(i_ % tpb == 0, 0.0, prv_ref[...])
        ae[HALO:] = a_ref[...]

        def chunk(k, carry):
            r0 = pl.multiple_of(k * CHUNK, CHUNK)
            wdc = _shifted(dce[pl.ds(r0, 2 * CHUNK), :])
            wa = _shifted(ae[pl.ds(r0, 2 * CHUNK), :])
            dcc = wdc[0][0:CHUNK]
            da = jnp.zeros((CHUNK, D), F32)
            for j in range(CW):
                da = da + dw_ref[j:j + 1, :] * _tap(wdc, CW - 1 - j)
                wacc[j] += _fold8(dcc * _tap(wa, j + 2))
            bv = big_ref[pl.ds(r0, CHUNK), :]
            a1, sg = bv[:, :D], _sig(bv[:, D:])
            d1 = da * sg
            d2 = da * a1 * sg * (1.0 - sg)
            da_ref[pl.ds(r0, CHUNK), 0:D] = d1.astype(BF)
            da_ref[pl.ds(r0, CHUNK), D:2 * D] = d2.astype(BF)
            bacc[:, 0:D] += _fold8(d1)
            bacc[:, D:2 * D] += _fold8(d2)
            return carry

        lax.fori_loop(0, tm // CHUNK, chunk, 0)

        @pl.when(i_ == nsteps - 1)
        def _():
            dbin_ref[...] = jnp.sum(bacc[...], axis=0, keepdims=True)
            ddw_ref[...] = jnp.sum(wacc[...], axis=1)

    return _call(
        body, f"conv_mid_bwd{i}", (nsteps,),
        [_row(tm, D), _next_halo(tm, t), _row(tm, D), _prev_halo(tm), _row(tm, 2 * D), _lay(i, (CW, D))],
        [_row(tm, 2 * D), _acc((1, 2 * D)), _acc((CW + 1, D))],
        [_sds((t, 2 * D), BF), _sds((1, 2 * D), F32), _sds((CW + 1, D), F32)],
        (dc, dc, a, a, big, dw),
        scratch=[pltpu.VMEM((tm + HALO, D), F32), pltpu.VMEM((tm + HALO, D), F32),
                 pltpu.VMEM((CW + 1, 8, D), F32), pltpu.VMEM((8, 2 * D), F32)], xch=xch)


def input_grads(dh0, seq):
    bl, lp, _ = dh0.shape
    cb = 256

    def body(dh_ref, gx_ref, gm_ref):
        _init(gm_ref, pl.program_id(1) == 0)
        gx_ref[...] = dh_ref[NMETA:NMETA + seq, :]
        gm_ref[...] += dh_ref[0:NMETA, :]

    return pl.pallas_call(
        body, name="input_grads", grid=(D // cb, bl),
        in_specs=[pl.BlockSpec((None, lp, cb), lambda c, b: (b, 0, c))],
        out_specs=[pl.BlockSpec((None, seq, cb), lambda c, b: (b, 0, c)), pl.BlockSpec((NMETA, cb), lambda c, b: (0, c))],
        out_shape=[_sds((bl, seq, D), F32), _sds((NMETA, D), F32)],
        compiler_params=_cp(2),
    )(dh0)


GATHER_PLAN = {
    "conv_mid_fwd0": [("ffn_w_gate", 0), ("ffn_w_up", 0), ("ffn_w_down", 0)],
    "mixer_ffn_fwd0": [("conv_w_in", 1), ("conv_w_out", 1), ("ffn_w_gate", 1)],
    "conv_mid_fwd1": [("ffn_w_up", 1), ("ffn_w_down", 1), ("w_kv", 0), ("w_q", 0)],
    "mixer_ffn_fwd1": [("w_o", 0), ("ffn_w_down", 2)],
    "attn_fwd0": [("ffn_w_gate", 2), ("ffn_w_up", 2), ("w_q", 1), ("w_o", 1), ("ffn_w_gate", 3), ("ffn_w_up", 3), ("ffn_w_down", 3)],
}
BIG = {"conv_w_in": "pieces", "conv_w_out": "rows", "w_kv": "rows", "w_q": "rows", "w_o": "rows",
       "ffn_w_gate": "rows", "ffn_w_up": "rows", "ffn_w_down": "rows"}
TRANSPOSED = ("ffn_w_gate", "ffn_w_up")


def gathered_matrix(name, layer, blocks8):
    if BIG[name] == "rows":
        return blocks8.reshape(NDEV * blocks8.shape[1], blocks8.shape[2])
    return join_columns(blocks8, f"join_{name}{layer}")


def local_step(x, tgt, meta8, w, mats, shards):
    bl, seq, _ = x.shape
    lp = -(-(NMETA + seq) // QB) * QB
    tpb = 4
    tm = lp // tpb
    t = bl * lp
    na = 2
    flat = lambda a: a.reshape(t, D)
    mats = dict(mats)

    def riders(carrier):
        return [shards[key] for key in GATHER_PLAN[carrier]]

    def landed(carrier, blocks):
        for key, b8 in zip(GATHER_PLAN[carrier], blocks):
            mats[key] = gathered_matrix(*key, b8)

    h = flat(embed(x, meta8, lp))
    saved = []
    kvs = None
    for l in range(4):
        rec = {"h": h}
        if l < na:
            rec["u"], rec["big"], rec["a"] = conv_in_fwd(h, w["norm_mix"], l, mats["conv_w_in", l], w["conv_b_in"], l, tm)
            name = f"conv_mid_fwd{l}"
            (rec["c"], rec["s"]), _, got = conv_mid_fwd(rec["a"], w["conv_dw"], w["conv_ln_g"], w["conv_ln_b"], l, tm, tpb,
                                                         riders(name))
            landed(name, got)
            mixed, w_out, lw, bias = rec["s"], mats["conv_w_out", l], l, w["conv_b_out"]
        else:
            j = l - na
            if kvs is None:
                kvs = dict(zip(("kn", "kv", "k", "v"), kv_fwd(h, w["kv_norm"], mats["w_kv", 0], w["k_norm"], tm)))
                kvs["h"] = h
                kvs["k3"], kvs["v3"] = kvs["k"].reshape(bl, lp, KVD), kvs["v"].reshape(bl, lp, KVD)
                kvs["kt"], kvs["vt"] = transpose_seq(kvs["k3"], "transpose_k"), transpose_seq(kvs["v3"], "transpose_v")
            rec["u"], rec["q"] = q_fwd(h, w["norm_mix"], l, mats["w_q", j], j, tm)
            name = f"attn_fwd{j}"
            (rec["o"], rec["lse"]), _, got = attn_fwd(rec["q"], kvs["kt"], kvs["v3"], w["q_norm"], w["attn_sinks"], j, bl, lp,
                                                      riders(name) if name in GATHER_PLAN else [])
            if name in GATHER_PLAN:
                landed(name, got)
            mixed, w_out, lw, bias = rec["o"], mats["w_o", j], j, None
        name = f"mixer_ffn_fwd{l}"
        (rec["h1"], rec["u2"], rec["g"], rec["up"], rec["hid"], h), _, got = mixer_ffn_fwd(
            h, mixed, w_out, lw, bias, w["norm_ffn"], l, mats["ffn_w_gate", l], mats["ffn_w_up", l], mats["ffn_w_down", l], tm // 2,
            riders(name) if name in GATHER_PLAN else [])
        if name in GATHER_PLAN:
            landed(name, got)
        saved.append(rec)

    dh3, loss_blk = loss_fwd(h.reshape(bl, lp, D), tgt)
    dh = flat(dh3)

    big, small, arrived = {}, {}, {}
    dks, dvs = [], []
    pending = []

    def carried(names, arrivals):
        arrived.update(zip([nm for nm, _ in names], arrivals))

    behind_dw_down = []
    for l in reversed(range(4)):
        rec = saved[l]
        riders, pending = pending, []
        (dg, du, dh1, small[f"norm_ffn{l}"]), got, _ = ffn_bwd_x(
            dh, rec["g"], rec["up"], rec["h1"], w["norm_ffn"], l, mats["ffn_w_down", l], mats["ffn_w_gate", l], mats["ffn_w_up", l], tm // 2,
            [(big[nm], kind) for nm, kind in riders])
        carried(riders, got)
        riders, behind_dw_down = behind_dw_down, []
        res = mm_tn(rec["hid"], dh, 2 * tm, f"dw_down{l}", xch=[(big[nm], kind) for nm, kind in riders])
        big[f"ffn_w_down{l}"] = res[0] if riders else res
        if riders:
            carried(riders, res[1])
        big[f"ffn_w_gate{l}"] = mm_tn(rec["u2"], dg, 2 * tm, f"dw_gate{l}", transposed=True)
        big[f"ffn_w_up{l}"] = mm_tn(rec["u2"], du, 2 * tm, f"dw_up{l}", transposed=True)
        ffn = [(f"ffn_w_down{l}", "rows"), (f"ffn_w_gate{l}", "rows"), (f"ffn_w_up{l}", "rows")]
        riders = ffn if l >= na else ffn[:2]
        xch = [(big[nm], kind) for nm, kind in riders]
        if l >= na:
            j = l - na
            do = out_proj_bwd(dh1, mats["w_o", j], tm, f"attn_out_bwd{j}")
            big[f"w_o{j}"] = mm_tn(rec["o"], dh1, 2 * tm, f"dw_o{j}")
            (dq, dk, dv, small[f"q_norm{j}"], small[f"attn_sinks{j}"]), got, _ = attn_bwd(
                rec["q"], kvs["k3"], kvs["kt"], kvs["vt"], do, rec["o"], rec["lse"], w["q_norm"], w["attn_sinks"], j, bl, lp, xch)
            carried(riders, got)
            dks.append(dk)
            dvs.append(dv)
            big[f"w_q{j}"] = mm_tn(rec["u"], dq, 2 * tm, f"dw_q{j}")
            dh, small[f"norm_mix{l}"] = proj_bwd(dq, mats["w_q", j], rec["h"], w["norm_mix"], l, dh1, tm, f"q_bwd{j}")
            pending = [(f"w_o{j}", "rows"), (f"w_q{j}", "rows")]
            if l == na:
                dkv, small["k_norm"] = kv_bwd_pre(sum_transposed(*dks), sum_transposed(*dvs), kvs["kv"], w["k_norm"], tm)
                big["w_kv"] = mm_tn(kvs["kn"], dkv, 2 * tm, "dw_kv")
                dh, small["kv_norm"] = proj_bwd(dkv, mats["w_kv", 0], kvs["h"], w["kv_norm"], None, dh, tm, "kv_bwd")
                pending.append(("w_kv", "rows"))
        else:
            dc, small[f"conv_ln_g{l}"], small[f"conv_ln_b{l}"], small[f"conv_b_out{l}"] = conv_out_bwd(
                dh1, rec["c"], w["conv_ln_g"], w["conv_ln_b"], mats["conv_w_out", l], l, tm)
            big[f"conv_w_out{l}"] = mm_tn(rec["s"], dh1, 2 * tm, f"dw_conv_out{l}")
            (da, small[f"conv_b_in{l}"], ddw), got, _ = conv_mid_bwd(dc, rec["a"], rec["big"], w["conv_dw"], l, tm, tpb, xch)
            carried(riders, got)
            small[f"conv_dw{l}"] = ddw[:CW]
            big[f"conv_w_in{l}"] = mm_tn(rec["u"], da, 2 * tm, f"dw_conv_in{l}", split=True)
            dh, small[f"norm_mix{l}"] = proj_bwd(da, mats["conv_w_in", l], rec["h"], w["norm_mix"], l, dh1, tm, f"conv_in_bwd{l}")
            pending = [ffn[2], (f"conv_w_out{l}", "rows")]
            behind_dw_down = [(f"conv_w_in{l}", "pieces")]
    pending += behind_dw_down
    grad_x, small["meta_tokens"] = input_grads(dh.reshape(bl, lp, D), seq)
    return loss_blk, grad_x, big, arrived, pending, small


def all_gather_blocks(blocks):
    n = len(blocks)

    def body(*refs):
        srcs, outs, sems = refs[:n], refs[n:2 * n], refs[2 * n:]
        _gat_start(srcs, outs, sems)
        _gat_forward(srcs, outs, sems)
        _gat_wait(srcs, outs, sems)

    any_spec = pl.BlockSpec(memory_space=pl.ANY)
    return pl.pallas_call(
        body, name="all_gather_blocks", out_shape=[_sds((NDEV,) + tuple(a.shape), a.dtype) for a in blocks],
        in_specs=[any_spec] * n, out_specs=[any_spec] * n, scratch_shapes=_xch_scratch(n),
    )(*blocks)


def cast_bf16(ws):
    n = len(ws)
    counts = [1 if x.ndim == 2 else x.shape[0] for x in ws]

    def body(*refs):
        outs = iter(refs[n:])
        for a in range(n):
            for l in range(counts[a]):
                next(outs)[...] = (refs[a][...] if ws[a].ndim == 2 else refs[a][l]).astype(BF)

    flat = pl.pallas_call(
        body, name="cast_bf16", out_shape=[_sds(x.shape[-2:], BF) for x, k in zip(ws, counts) for _ in range(k)],
        compiler_params=pltpu.CompilerParams(vmem_limit_bytes=VMEM_LIMIT),
    )(*ws)
    it = iter(flat)
    return [[next(it) for _ in range(k)] for k in counts]


def join_columns(w8, name):
    _, kk, n8 = w8.shape

    def body(x_ref, o_ref):
        o_ref[...] = jnp.concatenate([x_ref[p] for p in range(NDEV)], axis=1)

    return pl.pallas_call(body, name=name, out_shape=_sds((kk, NDEV * n8), w8.dtype),
                          compiler_params=pltpu.CompilerParams(vmem_limit_bytes=VMEM_LIMIT))(w8)


def _adamw_math(w, m, v, g):
    m2 = B1 * m + (1.0 - B1) * g
    v2 = B2 * v + (1.0 - B2) * (g * g)
    mh = m2 / (1.0 - B1 ** STEP)
    vh = v2 / (1.0 - B2 ** STEP)
    return -LR * (mh / (jnp.sqrt(vh) + AEPS) + WD * w), m2, v2


def adamw_big(w, m, v, parts, name, xch=(), gat=()):
    lyr, r, c = w.shape
    by_cols = c >= 512
    blk = (lyr, r, 256) if by_cols else (lyr, 256 if r % 256 == 0 else r, c)
    imap = (lambda i: (0, 0, i)) if by_cols else (lambda i: (0, i, 0))

    def body(w_ref, m_ref, v_ref, *rest):
        p_refs, (g_ref, d_ref, m2_ref, v2_ref) = rest[:lyr], rest[lyr:]
        for l in range(lyr):
            g = p_refs[l][0].astype(F32)
            for q in range(1, NDEV):
                g = g + p_refs[l][q].astype(F32)
            g_ref[l] = g
            d_ref[l], m2_ref[l], v2_ref[l] = _adamw_math(w_ref[l], m_ref[l], v_ref[l], g)

    spec = pl.BlockSpec(blk, imap)
    pspec = pl.BlockSpec((NDEV,) + blk[1:], imap)
    return _call(body, name, ((c // 256) if by_cols else (r // blk[1]),), [spec, spec, spec] + [pspec] * lyr,
                 [spec] * 4, [_sds((lyr, r, c), F32)] * 4, (w, m, v, *parts), xch=xch, gat=gat)


def sum_slots(g8):
    def body(g_ref, o_ref):
        acc = g_ref[0]
        for q in range(1, NDEV):
            acc = acc + g_ref[q]
        o_ref[...] = acc

    return pl.pallas_call(body, name="sum_slots", out_shape=_sds(g8.shape[1:], F32))(g8)


def adamw_small(w, m, v, g, name):
    def body(w_ref, m_ref, v_ref, g_ref, d_ref, m2_ref, v2_ref):
        d_ref[...], m2_ref[...], v2_ref[...] = _adamw_math(w_ref[...], m_ref[...], v_ref[...], g_ref[...])

    return pl.pallas_call(body, name=name, out_shape=[_sds(w.shape, F32)] * 3)(w, m, v, g)


NAMES = ["meta_tokens", "norm_mix", "norm_ffn", "conv_w_in", "conv_b_in", "conv_dw", "conv_ln_g", "conv_ln_b", "conv_w_out",
         "conv_b_out", "kv_norm", "w_kv", "k_norm", "w_q", "q_norm", "attn_sinks", "w_o", "ffn_w_gate", "ffn_w_up", "ffn_w_down"]
REP_ROWS = 16


def _pad_cols(a, width):
    return jnp.pad(a, ((0, 0), (0, width - a.shape[1])))


def _pack_rep(p):
    rows = [p["norm_mix"], p["norm_ffn"], p["kv_norm"].reshape(1, D), _pad_cols(p["k_norm"].reshape(1, HD), D),
            _pad_cols(p["q_norm"], D), _pad_cols(p["attn_sinks"], D), jnp.zeros((2, D), F32)]
    return jnp.concatenate(rows, axis=0)


def _unpack_rep(a):
    return {"norm_mix": a[0:4], "norm_ffn": a[4:8], "kv_norm": a[8], "k_norm": a[9, :HD], "q_norm": a[10:12, :HD],
            "attn_sinks": a[12:14, :NH]}


SH_NAMES = ["meta_tokens", "conv_b_in", "conv_dw", "conv_ln_g", "conv_ln_b", "conv_b_out"]


def _pack_sh(p):
    c = D // NDEV
    rows = [p["meta_tokens"], p["conv_b_in"].reshape(4, c), p["conv_dw"].reshape(2 * CW, c), p["conv_ln_g"], p["conv_ln_b"],
            p["conv_b_out"]]
    return jnp.concatenate(rows, axis=0)


def _unpack_sh(a):
    c = D // NDEV
    return {"meta_tokens": a[0:16], "conv_b_in": a[16:20].reshape(2, 2 * c), "conv_dw": a[20:82].reshape(2, CW, c),
            "conv_ln_g": a[82:84], "conv_ln_b": a[84:86], "conv_b_out": a[86:88]}


def kernel(x, meta_tokens, norm_mix, norm_ffn, conv_w_in, conv_b_in, conv_dw, conv_ln_g, conv_ln_b, conv_w_out, conv_b_out, kv_norm, w_kv, k_norm, w_q, q_norm, attn_sinks, w_o, ffn_w_gate, ffn_w_up, ffn_w_down, loss_target, m_meta_tokens, m_norm_mix, m_norm_ffn, m_conv_w_in, m_conv_b_in, m_conv_dw, m_conv_ln_g, m_conv_ln_b, m_conv_w_out, m_conv_b_out, m_kv_norm, m_w_kv, m_k_norm, m_w_q, m_q_norm, m_attn_sinks, m_w_o, m_ffn_w_gate, m_ffn_w_up, m_ffn_w_down, v_meta_tokens, v_norm_mix, v_norm_ffn, v_conv_w_in, v_conv_b_in, v_conv_dw, v_conv_ln_g, v_conv_ln_b, v_conv_w_out, v_conv_b_out, v_kv_norm, v_w_kv, v_k_norm, v_w_q, v_q_norm, v_attn_sinks, v_w_o, v_ffn_w_gate, v_ffn_w_up, v_ffn_w_down):
    wts = dict(zip(NAMES, (meta_tokens, norm_mix, norm_ffn, conv_w_in, conv_b_in, conv_dw, conv_ln_g, conv_ln_b, conv_w_out,
                           conv_b_out, kv_norm, w_kv, k_norm, w_q, q_norm, attn_sinks, w_o, ffn_w_gate, ffn_w_up, ffn_w_down)))
    mom = dict(zip(NAMES, (m_meta_tokens, m_norm_mix, m_norm_ffn, m_conv_w_in, m_conv_b_in, m_conv_dw, m_conv_ln_g, m_conv_ln_b,
                           m_conv_w_out, m_conv_b_out, m_kv_norm, m_w_kv, m_k_norm, m_w_q, m_q_norm, m_attn_sinks, m_w_o,
                           m_ffn_w_gate, m_ffn_w_up, m_ffn_w_down)))
    var = dict(zip(NAMES, (v_meta_tokens, v_norm_mix, v_norm_ffn, v_conv_w_in, v_conv_b_in, v_conv_dw, v_conv_ln_g, v_conv_ln_b,
                           v_conv_w_out, v_conv_b_out, v_kv_norm, v_w_kv, v_k_norm, v_w_q, v_q_norm, v_attn_sinks, v_w_o,
                           v_ffn_w_gate, v_ffn_w_up, v_ffn_w_down)))
    me = _my_index()
    c8 = D // NDEV
    for k in TRANSPOSED:
        wts[k], mom[k], var[k] = (jnp.swapaxes(a, 1, 2) for a in (wts[k], mom[k], var[k]))

    big_names = list(BIG)
    layers = cast_bf16([wts[k] for k in big_names])
    shards = {(k, l): blk for k, per_layer in zip(big_names, layers) for l, blk in enumerate(per_layer)}
    first = [("conv_w_in", 0), ("conv_w_out", 0)]
    vec_names = ["meta_tokens", "conv_b_in", "conv_dw", "conv_ln_g", "conv_ln_b", "conv_b_out"]
    gathered = all_gather_blocks([shards[key] for key in first] + [wts[k] for k in vec_names])
    mats = {key: gathered_matrix(*key, b8) for key, b8 in zip(first, gathered)}
    full = dict(zip(vec_names, gathered[len(first):]))
    join_vec = lambda a: jnp.moveaxis(a, 0, -2).reshape(a.shape[1:-1] + (NDEV * a.shape[-1],))
    w = {}
    w["conv_b_in"] = join_vec(full["conv_b_in"]).reshape(2, 1, 2 * D)
    w["conv_dw"] = join_vec(full["conv_dw"])
    for k in ("conv_ln_g", "conv_ln_b", "conv_b_out"):
        w[k] = join_vec(full[k]).reshape(2, 1, D)
    w["norm_mix"] = norm_mix.reshape(4, 1, D)
    w["norm_ffn"] = norm_ffn.reshape(4, 1, D)
    w["kv_norm"] = kv_norm.reshape(1, D)
    w["k_norm"] = k_norm.reshape(1, HD)
    w["q_norm"] = q_norm.reshape(2, 1, HD)
    w["attn_sinks"] = attn_sinks.reshape(2, 1, NH)

    loss_blk, grad_x, gbig, arrived, leftover, gs = local_step(x, loss_target, full["meta_tokens"], w, mats, shards)

    stack = lambda k, n: jnp.concatenate([gs[f"{k}{i}"] for i in range(n)], axis=0)
    rep = {"norm_mix": stack("norm_mix", 4), "norm_ffn": stack("norm_ffn", 4), "kv_norm": gs["kv_norm"], "k_norm": gs["k_norm"],
           "q_norm": stack("q_norm", 2), "attn_sinks": stack("attn_sinks", 2)}
    loss_row = _pad_cols(loss_blk[0:1, 0:1], D)
    packed = jnp.concatenate(
        [_pack_rep(rep)[:14], loss_row, jnp.zeros((1, D), F32), gs["meta_tokens"], stack("conv_b_in", 2).reshape(4, D),
         stack("conv_dw", 2), stack("conv_ln_g", 2), stack("conv_ln_b", 2), stack("conv_b_out", 2)], axis=0)

    grads, delta, new_m, new_v = {}, {}, {}, {}
    waiting = {nm.rstrip("0123456789") for nm, _ in leftover}
    leftover = sorted(leftover, key=lambda rider: -gbig[rider[0]].size)
    order = sorted([k for k in big_names if k not in waiting], key=lambda k: -wts[k].size) + [k for k in big_names if k in waiting]
    small8 = None
    for k in order:
        flat2 = wts[k].ndim == 2
        as3 = (lambda a: a[None]) if flat2 else (lambda a: a)
        riders = [leftover.pop(0)] if leftover and k not in waiting else []
        gat = [packed] if small8 is None and not riders else []
        parts = [arrived[k]] if flat2 else [arrived[f"{k}{i}"] for i in range(wts[k].shape[0])]
        outs, got_x, got_g = adamw_big(as3(wts[k]), as3(mom[k]), as3(var[k]), parts, "adamw_" + k,
                                       xch=[(gbig[nm], kind) for nm, kind in riders], gat=gat)
        arrived.update(zip([nm for nm, _ in riders], got_x))
        if gat:
            small8 = got_g[0]
        grads[k], delta[k], new_m[k], new_v[k] = [o[0] if flat2 else (jnp.swapaxes(o, 1, 2) if k in TRANSPOSED else o) for o in outs]
    assert not leftover and small8 is not None
    red = sum_slots(small8)
    loss = red[14, 0]
    cols = lambda a, width: lax.dynamic_slice_in_dim(a, me * width, width, axis=1)
    g_sh = jnp.concatenate(
        [cols(red[16:32], c8), cols(red[32:36].reshape(2, 2 * D), 2 * c8).reshape(4, c8), cols(red[36:98], c8),
         cols(red[98:100], c8), cols(red[100:102], c8), cols(red[102:104], c8)], axis=0)
    g_rep = red[0:REP_ROWS].at[14:].set(0.0)
    d_rep, m_rep, v_rep = adamw_small(_pack_rep(wts), _pack_rep(mom), _pack_rep(var), g_rep, "adamw_rep")
    d_sh, m_sh, v_sh = adamw_small(_pack_sh(wts), _pack_sh(mom), _pack_sh(var), g_sh, "adamw_sh")
    for dst, a_rep, a_sh in ((grads, g_rep, g_sh), (delta, d_rep, d_sh), (new_m, m_rep, m_sh), (new_v, v_rep, v_sh)):
        dst.update(_unpack_rep(a_rep))
        dst.update(_unpack_sh(a_sh))
    return (loss, grad_x, *[grads[k] for k in NAMES], *[delta[k] for k in NAMES], *[new_m[k] for k in NAMES],
            *[new_v[k] for k in NAMES])
```

```python
import functools

import jax
import jax.numpy as jnp
from jax import lax
from jax.experimental import pallas as pl
from jax.experimental.pallas import tpu as pltpu

F32 = jnp.float32
BF = jnp.bfloat16

D = 1024
DFF = 2816
NH = 16
NKV = 4
HD = 64
KVD = NKV * HD
NMETA = 16
CW = 31
HALO = 32
CHUNK = 32
QB = 128
EPS = 1e-6
NEG = -1e30
NDEV = 8
SCALE = HD ** -0.5

LR, B1, B2, AEPS, WD, STEP = 0.001, 0.9, 0.999, 1e-08, 0.01, 10

VMEM_LIMIT = 56 * 2 ** 20
MESH = pl.DeviceIdType.MESH


def _cp(n):
    return pltpu.CompilerParams(dimension_semantics=("arbitrary",) * n, vmem_limit_bytes=VMEM_LIMIT)


def _row(tm, c):
    return pl.BlockSpec((tm, c), lambda i: (i, 0))


def _res(shape):
    return pl.BlockSpec(shape, lambda i: (0,) * len(shape), pipeline_mode=pl.Buffered(1))


def _lay(l, shape):
    return pl.BlockSpec((None,) + tuple(shape), lambda i: (l,) + (0,) * len(shape), pipeline_mode=pl.Buffered(1))


def _acc(shape):
    return pl.BlockSpec(shape, lambda i: (0,) * len(shape))


def _sds(shape, dt):
    return jax.ShapeDtypeStruct(tuple(shape), dt)


def _dot(a, b):
    return jnp.dot(a.astype(BF), b.astype(BF), preferred_element_type=F32)


def _dot_nt(a, b):
    return lax.dot_general(a.astype(BF), b.astype(BF), (((1,), (1,)), ((), ())), preferred_element_type=F32)


def _dot_tn(a, b):
    return lax.dot_general(a.astype(BF), b.astype(BF), (((0,), (0,)), ((), ())), preferred_element_type=F32)


def _rstd(x):
    return lax.rsqrt(jnp.mean(x * x, axis=-1, keepdims=True) + EPS)


def _rms_bwd(x, g, dy):
    r = _rstd(x)
    z = dy * g
    dx = r * z - x * (r * r * r * jnp.mean(z * x, axis=-1, keepdims=True))
    return dx, jnp.sum(dy * x * r, axis=0, keepdims=True)


def _sig(x):
    return jax.nn.sigmoid(x)


def _fold8(x):
    out = x[0:8]
    for k in range(1, x.shape[0] // 8):
        out = out + x[8 * k:8 * k + 8]
    return out


def _shifted(win):
    return [win] + [pltpu.roll(win, 2 * CHUNK - rho, 0) for rho in range(1, 8)]


def _tap(phases, o):
    return phases[o % 8][8 * (o // 8):8 * (o // 8) + CHUNK]


def _init(ref, first):
    @pl.when(first)
    def _():
        ref[...] = jnp.zeros_like(ref)


def _my_index():
    return 4 * lax.axis_index("x") + 2 * lax.axis_index("y") + lax.axis_index("c")


def _coords(idx):
    return (idx // 4, (idx // 2) % 2, idx % 2)


def _xch_shapes(xch):
    return [_sds((NDEV,) + ((a.shape[0] // NDEV, a.shape[1]) if k == "rows" else tuple(a.shape[1:])), a.dtype) for a, k in xch]


def _xch_scratch(n):
    return [pltpu.SemaphoreType.DMA((n, NDEV - 1)), pltpu.SemaphoreType.DMA((n, NDEV - 1)), pltpu.SemaphoreType.DMA((n,))]


def _xch_copies(kinds, srcs, outs, sems, arrivals):
    send_sems, recv_sems, local_sems = sems
    me = _my_index()

    def piece(a, p):
        if kinds[a] == "rows":
            r = srcs[a].shape[0] // NDEV
            return srcs[a].at[pl.ds(p * r, r), :]
        return srcs[a].at[p]

    def remote(a, k, src, slot):
        return pltpu.make_async_remote_copy(
            src_ref=src, dst_ref=outs[a].at[slot], send_sem=send_sems.at[a, k - 1], recv_sem=recv_sems.at[a, k - 1],
            device_id=_coords(me ^ k), device_id_type=MESH)

    n = len(kinds)
    local = [pltpu.make_async_copy(piece(a, me), outs[a].at[me], local_sems.at[a]) for a in range(n)]
    sends = [remote(a, k, piece(a, me ^ k), me) for a in range(n) for k in range(1, NDEV)]
    recvs = [remote(a, k, piece(a, me), me ^ k) for a in range(n) for k in range(1, NDEV)] if arrivals else []
    return local, sends, recvs


def _xch_start(kinds, srcs, outs, sems):
    local, sends, _ = _xch_copies(kinds, srcs, outs, sems, False)
    for cp in local + sends:
        cp.start()


def _xch_wait(kinds, srcs, outs, sems):
    local, sends, recvs = _xch_copies(kinds, srcs, outs, sems, True)
    for cp in recvs:
        cp.wait_recv()
    for cp in sends:
        cp.wait_send()
    for cp in local:
        cp.wait()


def _gat_copies(srcs, outs, sems):
    send_sems, recv_sems, local_sems = sems
    x, y, c = lax.axis_index("x"), lax.axis_index("y"), lax.axis_index("c")
    me, sibling = (x, y, c), (x, y, 1 - c)
    chips = [(1 - x, y), (x, 1 - y), (1 - x, 1 - y)]

    def copy(a, k, owner, to, from_block=False):
        slot = outs[a].at[4 * owner[0] + 2 * owner[1] + owner[2]]
        return pltpu.make_async_remote_copy(
            src_ref=srcs[a] if from_block else slot, dst_ref=slot, send_sem=send_sems.at[a, k], recv_sem=recv_sems.at[a, k],
            device_id=to, device_id_type=MESH)

    n = len(srcs)
    local = lambda: [pltpu.make_async_copy(srcs[a], outs[a].at[4 * x + 2 * y + c], local_sems.at[a]) for a in range(n)]
    first = lambda: [cp for a in range(n) for cp in
                     [copy(a, 0, me, sibling, True)] + [copy(a, 1 + j, me, (*chip, c), True) for j, chip in enumerate(chips)]]
    landed = lambda: [copy(a, 1 + j, (*chip, c), me) for a in range(n) for j, chip in enumerate(chips)]
    passed = lambda: [copy(a, 4 + j, (*chip, c), sibling) for a in range(n) for j, chip in enumerate(chips)]
    final = lambda: [cp for a in range(n) for cp in
                     [copy(a, 0, sibling, me)] + [copy(a, 4 + j, (*chip, 1 - c), me) for j, chip in enumerate(chips)]]
    return local, first, landed, passed, final


def _gat_start(srcs, outs, sems):
    local, first, _, _, _ = _gat_copies(srcs, outs, sems)
    for cp in local() + first():
        cp.start()


def _gat_forward(srcs, outs, sems):
    _, _, landed, passed, _ = _gat_copies(srcs, outs, sems)
    for got, on in zip(landed(), passed()):
        got.wait_recv()
        on.start()


def _gat_wait(srcs, outs, sems):
    local, first, _, passed, final = _gat_copies(srcs, outs, sems)
    for cp in final():
        cp.wait_recv()
    for cp in first() + passed():
        cp.wait_send()
    for cp in local():
        cp.wait()


def _call(body, name, grid, in_specs, out_specs, out_shape, args, scratch=(), xch=(), gat=()):
    n_in, n_out, n_x, n_g, n_s = len(in_specs), len(out_specs), len(xch), len(gat), len(scratch)
    kinds = [k for _, k in xch]
    total = 1
    for g in grid:
        total *= g

    def wrapped(*refs):
        ins, refs = refs[:n_in], refs[n_in:]
        x_src, refs = refs[:n_x], refs[n_x:]
        g_src, refs = refs[:n_g], refs[n_g:]
        outs, refs = refs[:n_out], refs[n_out:]
        x_out, refs = refs[:n_x], refs[n_x:]
        g_out, refs = refs[:n_g], refs[n_g:]
        own, refs = refs[:n_s], refs[n_s:]
        x_sems, g_sems = (refs[:3], refs[3:]) if n_x else ((), refs)
        step = pl.program_id(0)
        for d in range(1, len(grid)):
            step = step * grid[d] + pl.program_id(d)
        if n_x or n_g:
            @pl.when(step == 0)
            def _():
                if n_x:
                    _xch_start(kinds, x_src, x_out, x_sems)
                if n_g:
                    _gat_start(g_src, g_out, g_sems)

        body(*ins, *outs, *own)
        if n_g:
            @pl.when(step == max(total - 2, 0))
            def _():
                _gat_forward(g_src, g_out, g_sems)

        if n_x or n_g:
            @pl.when(step == total - 1)
            def _():
                if n_x:
                    _xch_wait(kinds, x_src, x_out, x_sems)
                if n_g:
                    _gat_wait(g_src, g_out, g_sems)

    any_spec = pl.BlockSpec(memory_space=pl.ANY)
    g_shapes = [_sds((NDEV,) + tuple(a.shape), a.dtype) for a in gat]
    res = pl.pallas_call(
        wrapped, name=name, grid=grid, in_specs=list(in_specs) + [any_spec] * (n_x + n_g),
        out_specs=list(out_specs) + [any_spec] * (n_x + n_g), out_shape=list(out_shape) + _xch_shapes(xch) + g_shapes,
        scratch_shapes=list(scratch) + (_xch_scratch(n_x) if n_x else []) + (_xch_scratch(n_g) if n_g else []),
        compiler_params=_cp(len(grid)),
    )(*args, *[a for a, _ in xch], *gat)
    return res[:n_out], res[n_out:n_out + n_x], res[n_out + n_x:]


def embed(x, meta8, lp):
    bl, seq, _ = x.shape
    c8 = D // NDEV
    cb = 2 * c8

    def body(x_ref, m_ref, h_ref):
        h_ref[0:NMETA, :] = jnp.concatenate([m_ref[0], m_ref[1]], axis=1)
        h_ref[NMETA:NMETA + seq, :] = x_ref[...]
        h_ref[NMETA + seq:, :] = jnp.zeros((lp - NMETA - seq, cb), F32)

    return pl.pallas_call(
        body, name="embed", grid=(bl, D // cb),
        in_specs=[pl.BlockSpec((None, seq, cb), lambda b, c: (b, 0, c)), pl.BlockSpec((2, NMETA, c8), lambda b, c: (c, 0, 0))],
        out_specs=pl.BlockSpec((None, lp, cb), lambda b, c: (b, 0, c)), out_shape=_sds((bl, lp, D), F32),
        compiler_params=_cp(2),
    )(x, meta8)


def conv_in_fwd(h, nm, l, w_in, b_in, i, tm):
    t = h.shape[0]

    def body(h_ref, g_ref, w_ref, b_ref, u_ref, big_ref, a_ref):
        x = h_ref[...]
        ub = (x * _rstd(x) * g_ref[...]).astype(BF)
        u_ref[...] = ub
        big = jnp.dot(ub, w_ref[...], preferred_element_type=F32) + b_ref[...]
        big_ref[...] = big
        a_ref[...] = big[:, :D] * _sig(big[:, D:])

    return pl.pallas_call(
        body, name=f"conv_in_fwd{i}", grid=(t // tm,),
        in_specs=[_row(tm, D), _lay(l, (1, D)), _res((D, 2 * D)), _lay(i, (1, 2 * D))],
        out_specs=[_row(tm, D), _row(tm, 2 * D), _row(tm, D)],
        out_shape=[_sds((t, D), BF), _sds((t, 2 * D), F32), _sds((t, D), F32)],
        compiler_params=_cp(1),
    )(h, nm, w_in, b_in)


def _prev_halo(tm):
    return pl.BlockSpec((HALO, D), lambda i: (jnp.maximum(i * (tm // HALO) - 1, 0), 0))


def _next_halo(tm, t):
    return pl.BlockSpec((HALO, D), lambda i: (jnp.minimum((i + 1) * (tm // HALO), t // HALO - 1), 0))


def conv_mid_fwd(a, dw, ln_g, ln_b, i, tm, tpb, gat):
    t = a.shape[0]

    def body(a_ref, halo_ref, dw_ref, g_ref, b_ref, c_ref, s_ref, ext):
        first = pl.program_id(0) % tpb == 0
        ext[0:HALO] = jnp.where(first, 0.0, halo_ref[...])
        ext[HALO:] = a_ref[...]

        def chunk(k, carry):
            r0 = pl.multiple_of(k * CHUNK, CHUNK)
            win = _shifted(ext[pl.ds(r0, 2 * CHUNK), :])
            c = jnp.zeros((CHUNK, D), F32)
            for j in range(CW):
                c = c + dw_ref[j:j + 1, :] * _tap(win, j + 2)
            c_ref[pl.ds(r0, CHUNK), :] = c
            mu = jnp.mean(c, axis=-1, keepdims=True)
            xc = c - mu
            n = xc * lax.rsqrt(jnp.mean(xc * xc, axis=-1, keepdims=True) + EPS) * g_ref[...] + b_ref[...]
            s_ref[pl.ds(r0, CHUNK), :] = (n * _sig(n)).astype(BF)
            return carry

        lax.fori_loop(0, tm // CHUNK, chunk, 0)

    return _call(
        body, f"conv_mid_fwd{i}", (t // tm,),
        [_row(tm, D), _prev_halo(tm), _lay(i, (CW, D)), _lay(i, (1, D)), _lay(i, (1, D))],
        [_row(tm, D), _row(tm, D)], [_sds((t, D), F32), _sds((t, D), BF)], (a, a, dw, ln_g, ln_b),
        scratch=[pltpu.VMEM((tm + HALO, D), F32)], gat=gat)


def mixer_ffn_fwd(h, s, w_out, lw, bias, nf, l, wg, wu, wd, tm, gat):
    t = h.shape[0]

    def body(*refs):
        if bias is None:
            h_ref, s_ref, w_ref, nf_ref, wg_ref, wu_ref, wd_ref, h1_ref, u_ref, g_ref, up_ref, hid_ref, h2_ref = refs
            y = 0.0
        else:
            h_ref, s_ref, w_ref, b_ref, nf_ref, wg_ref, wu_ref, wd_ref, h1_ref, u_ref, g_ref, up_ref, hid_ref, h2_ref = refs
            y = b_ref[...]
        h1 = h_ref[...] + (jnp.dot(s_ref[...], w_ref[...], preferred_element_type=F32) + y)
        h1_ref[...] = h1
        ub = (h1 * _rstd(h1) * nf_ref[...]).astype(BF)
        u_ref[...] = ub
        g = _dot_nt(ub, wg_ref[...])
        up = _dot_nt(ub, wu_ref[...])
        g_ref[...] = g.astype(BF)
        up_ref[...] = up.astype(BF)
        hid = (g * _sig(g) * up).astype(BF)
        hid_ref[...] = hid
        h2_ref[...] = h1 + jnp.dot(hid, wd_ref[...], preferred_element_type=F32)

    ins = [h, s, w_out] + ([] if bias is None else [bias]) + [nf, wg, wu, wd]
    specs = ([_row(tm, D), _row(tm, D), _res((D, D))] + ([] if bias is None else [_lay(lw, (1, D))])
             + [_lay(l, (1, D)), _res((DFF, D)), _res((DFF, D)), _res((DFF, D))])
    return _call(
        body, f"mixer_ffn_fwd{l}", (t // tm,), specs,
        [_row(tm, D), _row(tm, D), _row(tm, DFF), _row(tm, DFF), _row(tm, DFF), _row(tm, D)],
        [_sds((t, D), F32), _sds((t, D), BF), _sds((t, DFF), BF), _sds((t, DFF), BF), _sds((t, DFF), BF), _sds((t, D), F32)],
        ins, gat=gat)


def _seg_rms(x, g, nseg):
    outs = []
    for s in range(nseg):
        xs = x[:, HD * s:HD * s + HD]
        outs.append(xs * _rstd(xs) * g)
    return jnp.concatenate(outs, axis=1)


def kv_fwd(h, kvn, w_kv, kng, tm):
    t = h.shape[0]

    def body(h_ref, g_ref, w_ref, kg_ref, kn_ref, kv_ref, k_ref, v_ref):
        x = h_ref[...]
        kn = (x * _rstd(x) * g_ref[...]).astype(BF)
        kn_ref[...] = kn
        kv = jnp.dot(kn, w_ref[...], preferred_element_type=F32)
        kv_ref[...] = kv
        k_ref[...] = _seg_rms(kv[:, :KVD], kg_ref[...], NKV).astype(BF)
        v_ref[...] = kv[:, KVD:].astype(BF)

    return pl.pallas_call(
        body, name="kv_fwd", grid=(t // tm,),
        in_specs=[_row(tm, D), _res((1, D)), _res((D, 2 * KVD)), _res((1, HD))],
        out_specs=[_row(tm, D), _row(tm, 2 * KVD), _row(tm, KVD), _row(tm, KVD)],
        out_shape=[_sds((t, D), BF), _sds((t, 2 * KVD), F32), _sds((t, KVD), BF), _sds((t, KVD), BF)],
        compiler_params=_cp(1),
    )(h, kvn, w_kv, kng)


def q_fwd(h, nm, l, w_q, j, tm):
    t = h.shape[0]

    def body(h_ref, g_ref, w_ref, u_ref, q_ref):
        x = h_ref[...]
        ub = (x * _rstd(x) * g_ref[...]).astype(BF)
        u_ref[...] = ub
        q_ref[...] = jnp.dot(ub, w_ref[...], preferred_element_type=F32)

    return pl.pallas_call(
        body, name=f"q_fwd{j}", grid=(t // tm,),
        in_specs=[_row(tm, D), _lay(l, (1, D)), _res((D, D))],
        out_specs=[_row(tm, D), _row(tm, D)], out_shape=[_sds((t, D), BF), _sds((t, D), F32)],
        compiler_params=_cp(1),
    )(h, nm, w_q)


RQ = NH // NKV


def _attn_specs(nb, lp):
    cur = lambda c: pl.BlockSpec((QB, c), lambda b, n: (b * nb + n, 0))
    seq = pl.BlockSpec((None, lp, KVD), lambda b, n: (b, 0, 0))
    seq_t = pl.BlockSpec((None, KVD, lp), lambda b, n: (b, 0, 0))
    return cur, seq, seq_t


NKEYS = 2 * QB + NMETA


def _attn_mask(n, start):
    shape = (RQ * QB, NKEYS)
    qpos = n * QB + (lax.broadcasted_iota(jnp.int32, shape, 0) & (QB - 1))
    col = lax.broadcasted_iota(jnp.int32, shape, 1)
    in_band = col < 2 * QB
    kpos = jnp.where(in_band, start + col, col - 2 * QB)
    return (kpos <= qpos) & ((col >= 2 * QB) | ((qpos - kpos < QB) & (kpos >= NMETA)))


def _keys(ref, band, gs):
    return jnp.concatenate([ref[band, gs], ref[0:NMETA, gs]], axis=0)


def _keys_t(ref, band, gs):
    return jnp.concatenate([ref[gs, band], ref[gs, 0:NMETA]], axis=1)


def transpose_seq(a, name):
    bl, r, c = a.shape

    def body(a_ref, o_ref):
        o_ref[...] = a_ref[...].T

    return pl.pallas_call(
        body, name=name, grid=(bl,), in_specs=[pl.BlockSpec((None, r, c), lambda b: (b, 0, 0))],
        out_specs=pl.BlockSpec((None, c, r), lambda b: (b, 0, 0)), out_shape=_sds((bl, c, r), a.dtype), compiler_params=_cp(1),
    )(a)


def sum_transposed(a0, a1):
    bl, c, r = a0.shape

    def body(a0_ref, a1_ref, o_ref):
        o_ref[...] = (a0_ref[...] + a1_ref[...]).T

    spec = pl.BlockSpec((None, c, r), lambda b: (b, 0, 0))
    return pl.pallas_call(
        body, name="sum_transposed", grid=(bl,), in_specs=[spec, spec],
        out_specs=pl.BlockSpec((r, c), lambda b: (b, 0)), out_shape=_sds((bl * r, c), a0.dtype), compiler_params=_cp(1),
    )(a0, a1)


def _stack_heads(ref, g, fn):
    return jnp.concatenate([fn(ref[:, HD * (g * RQ + r):HD * (g * RQ + r) + HD]) for r in range(RQ)], axis=0)


def _stack_cols(ref, g):
    return jnp.concatenate([ref[:, g * RQ + r:g * RQ + r + 1] for r in range(RQ)], axis=0)


def _stack_sinks(sk_ref, g):
    return jnp.concatenate([jnp.broadcast_to(sk_ref[:, g * RQ + r:g * RQ + r + 1], (QB, 1)) for r in range(RQ)], axis=0)


def attn_fwd(q, kt, v, qg, sinks, j, bl, lp, gat):
    t = q.shape[0]
    nb = lp // QB
    cur, seq, seq_t = _attn_specs(nb, lp)

    def body(q_ref, kt_ref, v_ref, qg_ref, sk_ref, o_ref, lse_ref):
        n = pl.program_id(1)
        start = pl.multiple_of(jnp.maximum(n - 1, 0) * QB, QB)
        mask = _attn_mask(n, start)
        band = pl.ds(start, 2 * QB)
        lane = lax.broadcasted_iota(jnp.int32, (QB, NH), 1)
        ones = jnp.ones((NKEYS, HD), BF)
        lse = jnp.zeros((QB, NH), F32)
        for g in range(NKV):
            gs = slice(HD * g, HD * g + HD)
            qn = _stack_heads(q_ref, g, lambda x: (x * _rstd(x) * qg_ref[...]).astype(BF))
            sink = _stack_sinks(sk_ref, g)
            s = jnp.where(mask, _dot(qn, _keys_t(kt_ref, band, gs)) * SCALE, NEG)
            mx = jnp.maximum(jnp.max(s, -1, keepdims=True), sink)
            oa = _dot(jnp.exp(s - mx), jnp.concatenate([_keys(v_ref, band, gs), ones], axis=1))
            den = oa[:, HD:HD + 1] + jnp.exp(sink - mx)
            o = oa[:, :HD] * (1.0 / den)
            l = mx + jnp.log(den)
            for r in range(RQ):
                h = g * RQ + r
                o_ref[:, HD * h:HD * h + HD] = o[r * QB:(r + 1) * QB].astype(BF)
                lse = jnp.where(lane == h, l[r * QB:(r + 1) * QB], lse)
        lse_ref[...] = lse

    return _call(
        body, f"attn_fwd{j}", (bl, nb),
        [cur(D), seq_t, seq, pl.BlockSpec((None, 1, HD), lambda b, n: (j, 0, 0)), pl.BlockSpec((None, 1, NH), lambda b, n: (j, 0, 0))],
        [cur(D), cur(NH)], [_sds((t, D), BF), _sds((t, NH), F32)], (q, kt, v, qg, sinks), gat=gat)


def loss_fwd(h, tgt):
    bl, lp, _ = h.shape
    seq = tgt.shape[1]
    cb = 256

    def body(h_ref, t_ref, dh_ref, loss_ref):
        _init(loss_ref, (pl.program_id(0) == 0) & (pl.program_id(1) == 0))
        err = h_ref[NMETA:NMETA + seq, :] - t_ref[...]
        dh_ref[...] = jnp.zeros_like(dh_ref)
        dh_ref[NMETA:NMETA + seq, :] = err * (1.0 / D)
        loss_ref[...] += (0.5 / D) * jnp.sum(err * err)

    return pl.pallas_call(
        body, name="loss_fwd", grid=(bl, D // cb),
        in_specs=[pl.BlockSpec((None, lp, cb), lambda b, c: (b, 0, c)), pl.BlockSpec((None, seq, cb), lambda b, c: (b, 0, c))],
        out_specs=[pl.BlockSpec((None, lp, cb), lambda b, c: (b, 0, c)), pl.BlockSpec((8, 128), lambda b, c: (0, 0))],
        out_shape=[_sds((bl, lp, D), F32), _sds((8, 128), F32)],
        compiler_params=_cp(2),
    )(h, tgt)


def ffn_bwd_x(dh2, g, up, h1, nf, l, wd, wg, wu, w_o, tm, xch):
    t = dh2.shape[0]

    def body(dh2_ref, g_ref, up_ref, h1_ref, nf_ref, wd_ref, wg_ref, wu_ref, *rest):
        if w_o is None:
            dg_ref, du_ref, dh1_ref, dnf_ref = rest
        else:
            wo_ref, dg_ref, du_ref, dh1_ref, dnf_ref, do_ref = rest
        _init(dnf_ref, pl.program_id(0) == 0)
        dh2v = dh2_ref[...]
        dhid = _dot_nt(dh2v, wd_ref[...])
        gv = g_ref[...].astype(F32)
        uv = up_ref[...].astype(F32)
        sg = _sig(gv)
        dgv = (dhid * uv * (sg * (1.0 + gv * (1.0 - sg)))).astype(BF)
        duv = (dhid * (gv * sg)).astype(BF)
        dg_ref[...] = dgv
        du_ref[...] = duv
        dnorm = _dot(dgv, wg_ref[...]) + _dot(duv, wu_ref[...])
        dx, dnf = _rms_bwd(h1_ref[...], nf_ref[...], dnorm)
        dh1 = dh2v + dx
        dh1_ref[...] = dh1
        dnf_ref[...] += dnf
        if w_o is not None:
            do_ref[...] = _dot_nt(dh1, wo_ref[...]).astype(BF)

    attn = w_o is not None
    return _call(
        body, f"ffn_bwd_x{l}", (t // tm,),
        [_row(tm, D), _row(tm, DFF), _row(tm, DFF), _row(tm, D), _lay(l, (1, D)),
         _res((DFF, D)), _res((DFF, D)), _res((DFF, D))] + ([_res((D, D))] if attn else []),
        [_row(tm, DFF), _row(tm, DFF), _row(tm, D), _acc((1, D))] + ([_row(tm, D)] if attn else []),
        [_sds((t, DFF), BF), _sds((t, DFF), BF), _sds((t, D), F32), _sds((1, D), F32)] + ([_sds((t, D), BF)] if attn else []),
        (dh2, g, up, h1, nf, wd, wg, wu) + ((w_o,) if attn else ()), xch=xch)


def mm_tn(x, dy, tm, name, split=False, transposed=False, xch=()):
    t, kk = x.shape
    nn = dy.shape[1]
    n8 = nn // NDEV
    nsteps = t // tm

    def body(x_ref, dy_ref, o_ref, acc):
        i = pl.program_id(0)
        _init(acc, i == 0)
        acc[...] += _dot_tn(x_ref[...], dy_ref[...])

        @pl.when(i == nsteps - 1)
        def _():
            if split:
                for p in range(NDEV):
                    o_ref[p] = acc[:, p * n8:(p + 1) * n8].astype(BF)
            elif transposed:
                o_ref[...] = acc[...].T.astype(BF)
            else:
                o_ref[...] = acc[...].astype(BF)

    oshape = (NDEV, kk, n8) if split else ((nn, kk) if transposed else (kk, nn))
    (out,), got, _ = _call(body, name, (nsteps,), [_row(tm, kk), _row(tm, nn)], [_acc(oshape)], [_sds(oshape, BF)], (x, dy),
                           scratch=[pltpu.VMEM((kk, nn), F32)], xch=xch)
    return (out, got) if xch else out


def proj_bwd(dy, w, h, g, lg, dh_in, tm, name):
    t = h.shape[0]
    nn = dy.shape[1]
    wspec = _res(w.shape)
    gspec = _res((1, D)) if lg is None else _lay(lg, (1, D))

    def body(dy_ref, w_ref, h_ref, g_ref, dhin_ref, dh_ref, dg_ref):
        _init(dg_ref, pl.program_id(0) == 0)
        du = _dot_nt(dy_ref[...], w_ref[...])
        dx, dg = _rms_bwd(h_ref[...], g_ref[...], du)
        dh_ref[...] = dhin_ref[...] + dx
        dg_ref[...] += dg

    return pl.pallas_call(
        body, name=name, grid=(t // tm,),
        in_specs=[_row(tm, nn), wspec, _row(tm, D), gspec, _row(tm, D)],
        out_specs=[_row(tm, D), _acc((1, D))], out_shape=[_sds((t, D), F32), _sds((1, D), F32)],
        compiler_params=_cp(1),
    )(dy, w, h, g, dh_in)


def attn_bwd(q, k, kt, vt, do, o, lse, qg, sinks, j, bl, lp, xch):
    t = q.shape[0]
    nb = lp // QB
    cur, seq, seq_t = _attn_specs(nb, lp)

    def body(q_ref, k_ref, kt_ref, vt_ref, do_ref, o_ref, lse_ref, qg_ref, sk_ref, dq_ref, dk_ref, dv_ref, dqg_ref, dsk_ref):
        b, n = pl.program_id(0), pl.program_id(1)
        _init(dk_ref, n == 0)
        _init(dv_ref, n == 0)
        _init(dqg_ref, (b == 0) & (n == 0))
        _init(dsk_ref, (b == 0) & (n == 0))
        start = pl.multiple_of(jnp.maximum(n - 1, 0) * QB, QB)
        mask = _attn_mask(n, start)
        band = pl.ds(start, 2 * QB)
        lane = lax.broadcasted_iota(jnp.int32, (1, NH), 1)
        dqg = jnp.zeros((1, HD), F32)
        dsk = jnp.zeros((1, NH), F32)
        for g in range(NKV):
            gs = slice(HD * g, HD * g + HD)
            kk = _keys(k_ref, band, gs)
            qh = _stack_heads(q_ref, g, lambda x: x)
            rs = _rstd(qh)
            qn = (qh * rs * qg_ref[...]).astype(BF)
            ls = _stack_cols(lse_ref, g)
            pr = jnp.where(mask, jnp.exp(_dot(qn, _keys_t(kt_ref, band, gs)) * SCALE - ls), 0.0)
            doh = _stack_heads(do_ref, g, lambda x: x)
            delta = jnp.sum(doh.astype(F32) * _stack_heads(o_ref, g, lambda x: x).astype(F32), axis=-1, keepdims=True)
            ds = (pr * (_dot(doh, _keys_t(vt_ref, band, gs)) - delta)).astype(BF)
            dqn = _dot(ds, kk) * SCALE
            dkt = _dot_tn(qn, ds) * SCALE
            dvt = _dot_tn(doh, pr)
            dk_ref[gs, band] += dkt[:, :2 * QB]
            dv_ref[gs, band] += dvt[:, :2 * QB]
            dk_ref[gs, 0:NMETA] += dkt[:, 2 * QB:]
            dv_ref[gs, 0:NMETA] += dvt[:, 2 * QB:]
            dsink = jnp.exp(_stack_sinks(sk_ref, g) - ls) * delta
            z = dqn * qg_ref[...]
            dq = rs * z - qh * (rs * rs * rs * jnp.mean(z * qh, axis=-1, keepdims=True))
            dqg = dqg + jnp.sum(dqn * qh * rs, axis=0, keepdims=True)
            for r in range(RQ):
                h = g * RQ + r
                dq_ref[:, HD * h:HD * h + HD] = dq[r * QB:(r + 1) * QB]
                dsk = dsk + jnp.where(lane == h, -jnp.sum(dsink[r * QB:(r + 1) * QB]), 0.0)
        dqg_ref[...] += dqg
        dsk_ref[...] += dsk

    return _call(
        body, f"attn_bwd{j}", (bl, nb),
        [cur(D), seq, seq_t, seq_t, cur(D), cur(D), cur(NH),
         pl.BlockSpec((None, 1, HD), lambda b, n: (j, 0, 0)), pl.BlockSpec((None, 1, NH), lambda b, n: (j, 0, 0))],
        [cur(D), seq_t, seq_t, pl.BlockSpec((1, HD), lambda b, n: (0, 0)), pl.BlockSpec((1, NH), lambda b, n: (0, 0))],
        [_sds((t, D), F32), _sds((bl, KVD, lp), F32), _sds((bl, KVD, lp), F32), _sds((1, HD), F32), _sds((1, NH), F32)],
        (q, k, kt, vt, do, o, lse, qg, sinks), xch=xch)


def kv_bwd_pre(dk, dv, kv, kng, tm):
    t = kv.shape[0]

    def body(dk_ref, dv_ref, kv_ref, g_ref, dkv_ref, dg_ref):
        _init(dg_ref, pl.program_id(0) == 0)
        dg = jnp.zeros((1, HD), F32)
        outs = []
        for s in range(NKV):
            sl = slice(HD * s, HD * s + HD)
            dx, dgs = _rms_bwd(kv_ref[:, sl], g_ref[...], dk_ref[:, sl])
            outs.append(dx)
            dg = dg + dgs
        dkv_ref[:, :KVD] = jnp.concatenate(outs, axis=1).astype(BF)
        dkv_ref[:, KVD:] = dv_ref[...].astype(BF)
        dg_ref[...] += dg

    return pl.pallas_call(
        body, name="kv_bwd_pre", grid=(t // tm,),
        in_specs=[_row(tm, KVD)] * 2 + [_row(tm, 2 * KVD), _res((1, HD))],
        out_specs=[_row(tm, 2 * KVD), _acc((1, HD))], out_shape=[_sds((t, 2 * KVD), BF), _sds((1, HD), F32)],
        compiler_params=_cp(1),
    )(dk, dv, kv, kng)


def conv_out_bwd(dh1, c, ln_g, ln_b, w_out, i, tm):
    t = dh1.shape[0]

    def body(dh1_ref, c_ref, g_ref, b_ref, w_ref, dc_ref, dg_ref, db_ref, dbo_ref):
        first = pl.program_id(0) == 0
        _init(dg_ref, first)
        _init(db_ref, first)
        _init(dbo_ref, first)
        dh1v = dh1_ref[...]
        ds = _dot_nt(dh1v, w_ref[...])
        cv = c_ref[...]
        xc = cv - jnp.mean(cv, axis=-1, keepdims=True)
        rstd = lax.rsqrt(jnp.mean(xc * xc, axis=-1, keepdims=True) + EPS)
        xh = xc * rstd
        n = xh * g_ref[...] + b_ref[...]
        sg = _sig(n)
        dn = ds * (sg * (1.0 + n * (1.0 - sg)))
        dxh = dn * g_ref[...]
        dc_ref[...] = rstd * (dxh - jnp.mean(dxh, axis=-1, keepdims=True) - xh * jnp.mean(dxh * xh, axis=-1, keepdims=True))
        dg_ref[...] += jnp.sum(dn * xh, axis=0, keepdims=True)
        db_ref[...] += jnp.sum(dn, axis=0, keepdims=True)
        dbo_ref[...] += jnp.sum(dh1v, axis=0, keepdims=True)

    return pl.pallas_call(
        body, name=f"conv_out_bwd{i}", grid=(t // tm,),
        in_specs=[_row(tm, D), _row(tm, D), _lay(i, (1, D)), _lay(i, (1, D)), _res((D, D))],
        out_specs=[_row(tm, D), _acc((1, D)), _acc((1, D)), _acc((1, D))],
        out_shape=[_sds((t, D), F32)] + [_sds((1, D), F32)] * 3,
        compiler_params=_cp(1),
    )(dh1, c, ln_g, ln_b, w_out)


def conv_mid_bwd(dc, a, big, dw, i, tm, tpb, xch):
    t = dc.shape[0]
    nsteps = t // tm

    def body(dc_ref, nxt_ref, a_ref, prv_ref, big_ref, dw_ref, da_ref, dbin_ref, ddw_ref, dce, ae, wacc, bacc):
        i_ = pl.program_id(0)
        _init(wacc, i_ == 0)
        _init(bacc, i_ == 0)
        dce[0:tm] = dc_ref[...]
        dce[tm:] = jnp.where(i_ % tpb == tpb - 1, 0.0, nxt_ref[...])
        ae[0:HALO] = jnp.where(i_ % tpb == 0, 0.0, prv_ref[...])
        ae[HALO:] = a_ref[...]

        def chunk(k, carry):
            r0 = pl.multiple_of(k * CHUNK, CHUNK)
            wdc = _shifted(dce[pl.ds(r0, 2 * CHUNK), :])
            wa = _shifted(ae[pl.ds(r0, 2 * CHUNK), :])
            dcc = wdc[0][0:CHUNK]
            da = jnp.zeros((CHUNK, D), F32)
            for j in range(CW):
                da = da + dw_ref[j:j + 1, :] * _tap(wdc, CW - 1 - j)
                wacc[j] += _fold8(dcc * _tap(wa, j + 2))
            bv = big_ref[pl.ds(r0, CHUNK), :]
            a1, sg = bv[:, :D], _sig(bv[:, D:])
            d1 = da * sg
            d2 = da * a1 * sg * (1.0 - sg)
            da_ref[pl.ds(r0, CHUNK), 0:D] = d1.astype(BF)
            da_ref[pl.ds(r0, CHUNK), D:2 * D] = d2.astype(BF)
            bacc[:, 0:D] += _fold8(d1)
            bacc[:, D:2 * D] += _fold8(d2)
            return carry

        lax.fori_loop(0, tm // CHUNK, chunk, 0)

        @pl.when(i_ == nsteps - 1)
        def _():
            dbin_ref[...] = jnp.sum(bacc[...], axis=0, keepdims=True)
            ddw_ref[...] = jnp.sum(wacc[...], axis=1)

    return _call(
        body, f"conv_mid_bwd{i}", (nsteps,),
        [_row(tm, D), _next_halo(tm, t), _row(tm, D), _prev_halo(tm), _row(tm, 2 * D), _lay(i, (CW, D))],
        [_row(tm, 2 * D), _acc((1, 2 * D)), _acc((CW + 1, D))],
        [_sds((t, 2 * D), BF), _sds((1, 2 * D), F32), _sds((CW + 1, D), F32)],
        (dc, dc, a, a, big, dw),
        scratch=[pltpu.VMEM((tm + HALO, D), F32), pltpu.VMEM((tm + HALO, D), F32),
                 pltpu.VMEM((CW + 1, 8, D), F32), pltpu.VMEM((8, 2 * D), F32)], xch=xch)


def input_grads(dh0, seq):
    bl, lp, _ = dh0.shape
    cb = 256

    def body(dh_ref, gx_ref, gm_ref):
        _init(gm_ref, pl.program_id(1) == 0)
        gx_ref[...] = dh_ref[NMETA:NMETA + seq, :]
        gm_ref[...] += dh_ref[0:NMETA, :]

    return pl.pallas_call(
        body, name="input_grads", grid=(D // cb, bl),
        in_specs=[pl.BlockSpec((None, lp, cb), lambda c, b: (b, 0, c))],
        out_specs=[pl.BlockSpec((None, seq, cb), lambda c, b: (b, 0, c)), pl.BlockSpec((NMETA, cb), lambda c, b: (0, c))],
        out_shape=[_sds((bl, seq, D), F32), _sds((NMETA, D), F32)],
        compiler_params=_cp(2),
    )(dh0)


GATHER_PLAN = {
    "conv_mid_fwd0": [("ffn_w_gate", 0), ("ffn_w_up", 0), ("ffn_w_down", 0)],
    "mixer_ffn_fwd0": [("conv_w_in", 1), ("conv_w_out", 1), ("ffn_w_gate", 1)],
    "conv_mid_fwd1": [("ffn_w_up", 1), ("ffn_w_down", 1), ("w_kv", 0), ("w_q", 0)],
    "mixer_ffn_fwd1": [("w_o", 0), ("ffn_w_down", 2)],
    "attn_fwd0": [("ffn_w_gate", 2), ("ffn_w_up", 2), ("w_q", 1), ("w_o", 1), ("ffn_w_gate", 3), ("ffn_w_up", 3), ("ffn_w_down", 3)],
}
BIG = {"conv_w_in": "pieces", "conv_w_out": "rows", "w_kv": "rows", "w_q": "rows", "w_o": "rows",
       "ffn_w_gate": "rows", "ffn_w_up": "rows", "ffn_w_down": "rows"}
TRANSPOSED = ("ffn_w_gate", "ffn_w_up")


def gathered_matrix(name, layer, blocks8):
    if BIG[name] == "rows":
        return blocks8.reshape(NDEV * blocks8.shape[1], blocks8.shape[2])
    return join_columns(blocks8, f"join_{name}{layer}")


def local_step(x, tgt, meta8, w, mats, shards):
    bl, seq, _ = x.shape
    lp = -(-(NMETA + seq) // QB) * QB
    tpb = 4
    tm = lp // tpb
    t = bl * lp
    na = 2
    flat = lambda a: a.reshape(t, D)
    mats = dict(mats)

    def riders(carrier):
        return [shards[key] for key in GATHER_PLAN[carrier]]

    def landed(carrier, blocks):
        for key, b8 in zip(GATHER_PLAN[carrier], blocks):
            mats[key] = gathered_matrix(*key, b8)

    h = flat(embed(x, meta8, lp))
    saved = []
    kvs = None
    for l in range(4):
        rec = {"h": h}
        if l < na:
            rec["u"], rec["big"], rec["a"] = conv_in_fwd(h, w["norm_mix"], l, mats["conv_w_in", l], w["conv_b_in"], l, tm)
            name = f"conv_mid_fwd{l}"
            (rec["c"], rec["s"]), _, got = conv_mid_fwd(rec["a"], w["conv_dw"], w["conv_ln_g"], w["conv_ln_b"], l, tm, tpb,
                                                         riders(name))
            landed(name, got)
            mixed, w_out, lw, bias = rec["s"], mats["conv_w_out", l], l, w["conv_b_out"]
        else:
            j = l - na
            if kvs is None:
                kvs = dict(zip(("kn", "kv", "k", "v"), kv_fwd(h, w["kv_norm"], mats["w_kv", 0], w["k_norm"], tm)))
                kvs["h"] = h
                kvs["k3"], kvs["v3"] = kvs["k"].reshape(bl, lp, KVD), kvs["v"].reshape(bl, lp, KVD)
                kvs["kt"], kvs["vt"] = transpose_seq(kvs["k3"], "transpose_k"), transpose_seq(kvs["v3"], "transpose_v")
            rec["u"], rec["q"] = q_fwd(h, w["norm_mix"], l, mats["w_q", j], j, tm)
            name = f"attn_fwd{j}"
            (rec["o"], rec["lse"]), _, got = attn_fwd(rec["q"], kvs["kt"], kvs["v3"], w["q_norm"], w["attn_sinks"], j, bl, lp,
                                                      riders(name) if name in GATHER_PLAN else [])
            if name in GATHER_PLAN:
                landed(name, got)
            mixed, w_out, lw, bias = rec["o"], mats["w_o", j], j, None
        name = f"mixer_ffn_fwd{l}"
        (rec["h1"], rec["u2"], rec["g"], rec["up"], rec["hid"], h), _, got = mixer_ffn_fwd(
            h, mixed, w_out, lw, bias, w["norm_ffn"], l, mats["ffn_w_gate", l], mats["ffn_w_up", l], mats["ffn_w_down", l], tm // 2,
            riders(name) if name in GATHER_PLAN else [])
        if name in GATHER_PLAN:
            landed(name, got)
        saved.append(rec)

    dh3, loss_blk = loss_fwd(h.reshape(bl, lp, D), tgt)
    dh = flat(dh3)

    big, small, arrived = {}, {}, {}
    dks, dvs = [], []
    pending = []

    def carried(names, arrivals):
        arrived.update(zip([nm for nm, _ in names], arrivals))

    behind_dw_down = []
    for l in reversed(range(4)):
        rec = saved[l]
        riders, pending = pending, []
        outs, got, _ = ffn_bwd_x(
            dh, rec["g"], rec["up"], rec["h1"], w["norm_ffn"], l, mats["ffn_w_down", l], mats["ffn_w_gate", l], mats["ffn_w_up", l],
            mats["w_o", l - na] if l >= na else None, tm // 2, [(big[nm], kind) for nm, kind in riders])
        dg, du, dh1, small[f"norm_ffn{l}"] = outs[:4]
        carried(riders, got)
        riders, behind_dw_down = behind_dw_down, []
        res = mm_tn(rec["hid"], dh, 2 * tm, f"dw_down{l}", xch=[(big[nm], kind) for nm, kind in riders])
        big[f"ffn_w_down{l}"] = res[0] if riders else res
        if riders:
            carried(riders, res[1])
        big[f"ffn_w_gate{l}"] = mm_tn(rec["u2"], dg, 2 * tm, f"dw_gate{l}", transposed=True)
        big[f"ffn_w_up{l}"] = mm_tn(rec["u2"], du, 2 * tm, f"dw_up{l}", transposed=True)
        ffn = [(f"ffn_w_down{l}", "rows"), (f"ffn_w_gate{l}", "rows"), (f"ffn_w_up{l}", "rows")]
        riders = ffn if l >= na else ffn[:2]
        xch = [(big[nm], kind) for nm, kind in riders]
        if l >= na:
            j = l - na
            do = outs[4]
            big[f"w_o{j}"] = mm_tn(rec["o"], dh1, 2 * tm, f"dw_o{j}")
            (dq, dk, dv, small[f"q_norm{j}"], small[f"attn_sinks{j}"]), got, _ = attn_bwd(
                rec["q"], kvs["k3"], kvs["kt"], kvs["vt"], do, rec["o"], rec["lse"], w["q_norm"], w["attn_sinks"], j, bl, lp, xch)
            carried(riders, got)
            dks.append(dk)
            dvs.append(dv)
            big[f"w_q{j}"] = mm_tn(rec["u"], dq, 2 * tm, f"dw_q{j}")
            dh, small[f"norm_mix{l}"] = proj_bwd(dq, mats["w_q", j], rec["h"], w["norm_mix"], l, dh1, tm, f"q_bwd{j}")
            pending = [(f"w_o{j}", "rows"), (f"w_q{j}", "rows")]
            if l == na:
                dkv, small["k_norm"] = kv_bwd_pre(sum_transposed(*dks), sum_transposed(*dvs), kvs["kv"], w["k_norm"], tm)
                big["w_kv"] = mm_tn(kvs["kn"], dkv, 2 * tm, "dw_kv")
                dh, small["kv_norm"] = proj_bwd(dkv, mats["w_kv", 0], kvs["h"], w["kv_norm"], None, dh, tm, "kv_bwd")
                pending.append(("w_kv", "rows"))
        else:
            dc, small[f"conv_ln_g{l}"], small[f"conv_ln_b{l}"], small[f"conv_b_out{l}"] = conv_out_bwd(
                dh1, rec["c"], w["conv_ln_g"], w["conv_ln_b"], mats["conv_w_out", l], l, tm)
            big[f"conv_w_out{l}"] = mm_tn(rec["s"], dh1, 2 * tm, f"dw_conv_out{l}")
            (da, small[f"conv_b_in{l}"], ddw), got, _ = conv_mid_bwd(dc, rec["a"], rec["big"], w["conv_dw"], l, tm, tpb, xch)
            carried(riders, got)
            small[f"conv_dw{l}"] = ddw
            big[f"conv_w_in{l}"] = mm_tn(rec["u"], da, 2 * tm, f"dw_conv_in{l}", split=True)
            dh, small[f"norm_mix{l}"] = proj_bwd(da, mats["conv_w_in", l], rec["h"], w["norm_mix"], l, dh1, tm, f"conv_in_bwd{l}")
            pending = [ffn[2], (f"conv_w_out{l}", "rows")]
            behind_dw_down = [(f"conv_w_in{l}", "pieces")]
    pending += behind_dw_down
    grad_x, small["meta_tokens"] = input_grads(dh.reshape(bl, lp, D), seq)
    return loss_blk, grad_x, big, arrived, pending, small


def all_gather_blocks(blocks):
    n = len(blocks)

    def body(*refs):
        srcs, outs, sems = refs[:n], refs[n:2 * n], refs[2 * n:]
        _gat_start(srcs, outs, sems)
        _gat_forward(srcs, outs, sems)
        _gat_wait(srcs, outs, sems)

    any_spec = pl.BlockSpec(memory_space=pl.ANY)
    return pl.pallas_call(
        body, name="all_gather_blocks", out_shape=[_sds((NDEV,) + tuple(a.shape), a.dtype) for a in blocks],
        in_specs=[any_spec] * n, out_specs=[any_spec] * n, scratch_shapes=_xch_scratch(n),
    )(*blocks)


def cast_bf16(ws):
    n = len(ws)
    counts = [1 if x.ndim == 2 else x.shape[0] for x in ws]

    def body(*refs):
        outs = iter(refs[n:])
        for a in range(n):
            for l in range(counts[a]):
                next(outs)[...] = (refs[a][...] if ws[a].ndim == 2 else refs[a][l]).astype(BF)

    flat = pl.pallas_call(
        body, name="cast_bf16", out_shape=[_sds(x.shape[-2:], BF) for x, k in zip(ws, counts) for _ in range(k)],
        compiler_params=pltpu.CompilerParams(vmem_limit_bytes=VMEM_LIMIT),
    )(*ws)
    it = iter(flat)
    return [[next(it) for _ in range(k)] for k in counts]


def join_columns(w8, name):
    _, kk, n8 = w8.shape

    def body(x_ref, o_ref):
        o_ref[...] = jnp.concatenate([x_ref[p] for p in range(NDEV)], axis=1)

    return pl.pallas_call(body, name=name, out_shape=_sds((kk, NDEV * n8), w8.dtype),
                          compiler_params=pltpu.CompilerParams(vmem_limit_bytes=VMEM_LIMIT))(w8)


def _adamw_math(w, m, v, g):
    m2 = B1 * m + (1.0 - B1) * g
    v2 = B2 * v + (1.0 - B2) * (g * g)
    mh = m2 / (1.0 - B1 ** STEP)
    vh = v2 / (1.0 - B2 ** STEP)
    return -LR * (mh / (jnp.sqrt(vh) + AEPS) + WD * w), m2, v2


def adamw_big(w, m, v, parts, name, xch=(), gat=()):
    lyr, r, c = w.shape
    by_cols = c >= 512
    blk = (lyr, r, 256) if by_cols else (lyr, 256 if r % 256 == 0 else r, c)
    imap = (lambda i: (0, 0, i)) if by_cols else (lambda i: (0, i, 0))

    def body(w_ref, m_ref, v_ref, *rest):
        p_refs, (g_ref, d_ref, m2_ref, v2_ref) = rest[:lyr], rest[lyr:]
        for l in range(lyr):
            g = p_refs[l][0].astype(F32)
            for q in range(1, NDEV):
                g = g + p_refs[l][q].astype(F32)
            g_ref[l] = g
            d_ref[l], m2_ref[l], v2_ref[l] = _adamw_math(w_ref[l], m_ref[l], v_ref[l], g)

    spec = pl.BlockSpec(blk, imap)
    pspec = pl.BlockSpec((NDEV,) + blk[1:], imap)
    return _call(body, name, ((c // 256) if by_cols else (r // blk[1]),), [spec, spec, spec] + [pspec] * lyr,
                 [spec] * 4, [_sds((lyr, r, c), F32)] * 4, (w, m, v, *parts), xch=xch, gat=gat)


SMALL_ROWS = 104
REPLICATED = {"norm_mix": (0, 4, D), "norm_ffn": (4, 4, D), "kv_norm": (8, 1, D), "k_norm": (9, 1, HD), "q_norm": (10, 2, HD),
              "attn_sinks": (12, 2, NH)}
LOSS_ROW = 14
SHARDED = {"meta_tokens": (16, NMETA), "conv_b_in": (32, 4), "conv_dw": (36, 2 * CW), "conv_ln_g": (98, 2), "conv_ln_b": (100, 2),
           "conv_b_out": (102, 2)}


def pack_small(gs, loss_blk):
    order = ([f"norm_mix{l}" for l in range(4)] + [f"norm_ffn{l}" for l in range(4)] + ["kv_norm", "k_norm", "q_norm0", "q_norm1",
             "attn_sinks0", "attn_sinks1", "meta_tokens", "conv_b_in0", "conv_b_in1", "conv_dw0", "conv_dw1", "conv_ln_g0",
             "conv_ln_g1", "conv_ln_b0", "conv_ln_b1", "conv_b_out0", "conv_b_out1"])

    def body(*refs):
        r = dict(zip(order, refs))
        loss_ref, o_ref = refs[len(order)], refs[len(order) + 1]
        o_ref[...] = jnp.zeros_like(o_ref)
        for l in range(4):
            o_ref[l:l + 1, :] = r[f"norm_mix{l}"][...]
            o_ref[4 + l:5 + l, :] = r[f"norm_ffn{l}"][...]
        o_ref[8:9, :] = r["kv_norm"][...]
        o_ref[9:10, 0:HD] = r["k_norm"][...]
        for j in range(2):
            o_ref[10 + j:11 + j, 0:HD] = r[f"q_norm{j}"][...]
            o_ref[12 + j:13 + j, 0:NH] = r[f"attn_sinks{j}"][...]
            o_ref[32 + 2 * j:33 + 2 * j, :] = r[f"conv_b_in{j}"][:, 0:D]
            o_ref[33 + 2 * j:34 + 2 * j, :] = r[f"conv_b_in{j}"][:, D:2 * D]
            o_ref[36 + CW * j:36 + CW * (j + 1), :] = r[f"conv_dw{j}"][0:CW, :]
            o_ref[98 + j:99 + j, :] = r[f"conv_ln_g{j}"][...]
            o_ref[100 + j:101 + j, :] = r[f"conv_ln_b{j}"][...]
            o_ref[102 + j:103 + j, :] = r[f"conv_b_out{j}"][...]
        o_ref[LOSS_ROW:LOSS_ROW + 1, 0:1] = loss_ref[0:1, 0:1]
        o_ref[16:16 + NMETA, :] = r["meta_tokens"][...]

    return pl.pallas_call(body, name="pack_small", out_shape=_sds((SMALL_ROWS, D), F32))(*[gs[k] for k in order], loss_blk)


def adamw_small(g8, wts, mom, var):
    names = list(REPLICATED) + list(SHARDED)
    shape2 = {"kv_norm": (1, D), "k_norm": (1, HD)}
    ins = [a[k].reshape(shape2.get(k, a[k].shape)) for a in (wts, mom, var) for k in names]
    n = len(names)

    def body(*refs):
        g8_ref, w_refs, m_refs, v_refs = refs[0], refs[1:1 + n], refs[1 + n:1 + 2 * n], refs[1 + 2 * n:1 + 3 * n]
        loss_ref, outs, red_ref = refs[1 + 3 * n], refs[2 + 3 * n:-1], refs[-1]
        me = _my_index()
        acc = g8_ref[0]
        for q in range(1, NDEV):
            acc = acc + g8_ref[q]
        red_ref[...] = acc
        loss_ref[...] = red_ref[LOSS_ROW:LOSS_ROW + 1, 0:1]

        def mine(rows, width):
            acc = jnp.zeros((rows.stop - rows.start, width), F32)
            for p_ in range(NDEV):
                acc = acc + jnp.where(me == p_, red_ref[rows, p_ * width:(p_ + 1) * width], 0.0)
            return acc

        for i, k in enumerate(names):
            if k in REPLICATED:
                r0, nr, width = REPLICATED[k]
                g = red_ref[r0:r0 + nr, 0:width]
            elif k == "conv_b_in":
                half = D // (2 * D // NDEV)
                acc = jnp.zeros((2, 2 * D // NDEV), F32)
                for p_ in range(NDEV):
                    c0 = (p_ % half) * (2 * D // NDEV)
                    part = jnp.concatenate([red_ref[32 + 2 * j + p_ // half:33 + 2 * j + p_ // half, c0:c0 + 2 * D // NDEV]
                                            for j in range(2)], axis=0)
                    acc = acc + jnp.where(me == p_, part, 0.0)
                g = acc
            else:
                r0, nr = SHARDED[k]
                g = mine(slice(r0, r0 + nr), D // NDEV)
            w_, m_, v_ = w_refs[i], m_refs[i], v_refs[i]
            g_out, d_out, m_out, v_out = outs[4 * i:4 * i + 4]
            if k == "conv_dw":
                for j in range(2):
                    gj = g[CW * j:CW * (j + 1)]
                    g_out[j] = gj
                    d_out[j], m_out[j], v_out[j] = _adamw_math(w_[j], m_[j], v_[j], gj)
            else:
                g_out[...] = g
                d_out[...], m_out[...], v_out[...] = _adamw_math(w_[...], m_[...], v_[...], g)

    out_shape = [_sds((1, 1), F32)] + [_sds(ins[i].shape, F32) for i in range(n) for _ in range(4)]
    res = pl.pallas_call(body, name="adamw_small", out_shape=out_shape, scratch_shapes=[pltpu.VMEM((SMALL_ROWS, D), F32)])(g8, *ins)
    out = {k: tuple(o.reshape(wts[k].shape) for o in res[1 + 4 * i:5 + 4 * i]) for i, k in enumerate(names)}
    return res[0], out


NAMES = ["meta_tokens", "norm_mix", "norm_ffn", "conv_w_in", "conv_b_in", "conv_dw", "conv_ln_g", "conv_ln_b", "conv_w_out",
         "conv_b_out", "kv_norm", "w_kv", "k_norm", "w_q", "q_norm", "attn_sinks", "w_o", "ffn_w_gate", "ffn_w_up", "ffn_w_down"]


def kernel(x, meta_tokens, norm_mix, norm_ffn, conv_w_in, conv_b_in, conv_dw, conv_ln_g, conv_ln_b, conv_w_out, conv_b_out, kv_norm, w_kv, k_norm, w_q, q_norm, attn_sinks, w_o, ffn_w_gate, ffn_w_up, ffn_w_down, loss_target, m_meta_tokens, m_norm_mix, m_norm_ffn, m_conv_w_in, m_conv_b_in, m_conv_dw, m_conv_ln_g, m_conv_ln_b, m_conv_w_out, m_conv_b_out, m_kv_norm, m_w_kv, m_k_norm, m_w_q, m_q_norm, m_attn_sinks, m_w_o, m_ffn_w_gate, m_ffn_w_up, m_ffn_w_down, v_meta_tokens, v_norm_mix, v_norm_ffn, v_conv_w_in, v_conv_b_in, v_conv_dw, v_conv_ln_g, v_conv_ln_b, v_conv_w_out, v_conv_b_out, v_kv_norm, v_w_kv, v_k_norm, v_w_q, v_q_norm, v_attn_sinks, v_w_o, v_ffn_w_gate, v_ffn_w_up, v_ffn_w_down):
    wts = dict(zip(NAMES, (meta_tokens, norm_mix, norm_ffn, conv_w_in, conv_b_in, conv_dw, conv_ln_g, conv_ln_b, conv_w_out,
                           conv_b_out, kv_norm, w_kv, k_norm, w_q, q_norm, attn_sinks, w_o, ffn_w_gate, ffn_w_up, ffn_w_down)))
    mom = dict(zip(NAMES, (m_meta_tokens, m_norm_mix, m_norm_ffn, m_conv_w_in, m_conv_b_in, m_conv_dw, m_conv_ln_g, m_conv_ln_b,
                           m_conv_w_out, m_conv_b_out, m_kv_norm, m_w_kv, m_k_norm, m_w_q, m_q_norm, m_attn_sinks, m_w_o,
                           m_ffn_w_gate, m_ffn_w_up, m_ffn_w_down)))
    var = dict(zip(NAMES, (v_meta_tokens, v_norm_mix, v_norm_ffn, v_conv_w_in, v_conv_b_in, v_conv_dw, v_conv_ln_g, v_conv_ln_b,
                           v_conv_w_out, v_conv_b_out, v_kv_norm, v_w_kv, v_k_norm, v_w_q, v_q_norm, v_attn_sinks, v_w_o,
                           v_ffn_w_gate, v_ffn_w_up, v_ffn_w_down)))
    for k in TRANSPOSED:
        wts[k], mom[k], var[k] = (jnp.swapaxes(a, 1, 2) for a in (wts[k], mom[k], var[k]))

    big_names = list(BIG)
    layers = cast_bf16([wts[k] for k in big_names])
    shards = {(k, l): blk for k, per_layer in zip(big_names, layers) for l, blk in enumerate(per_layer)}
    first = [("conv_w_in", 0), ("conv_w_out", 0)]
    vec_names = ["meta_tokens", "conv_b_in", "conv_dw", "conv_ln_g", "conv_ln_b", "conv_b_out"]
    gathered = all_gather_blocks([shards[key] for key in first] + [wts[k] for k in vec_names])
    mats = {key: gathered_matrix(*key, b8) for key, b8 in zip(first, gathered)}
    full = dict(zip(vec_names, gathered[len(first):]))
    join_vec = lambda a: jnp.moveaxis(a, 0, -2).reshape(a.shape[1:-1] + (NDEV * a.shape[-1],))
    w = {}
    w["conv_b_in"] = join_vec(full["conv_b_in"]).reshape(2, 1, 2 * D)
    w["conv_dw"] = join_vec(full["conv_dw"])
    for k in ("conv_ln_g", "conv_ln_b", "conv_b_out"):
        w[k] = join_vec(full[k]).reshape(2, 1, D)
    w["norm_mix"] = norm_mix.reshape(4, 1, D)
    w["norm_ffn"] = norm_ffn.reshape(4, 1, D)
    w["kv_norm"] = kv_norm.reshape(1, D)
    w["k_norm"] = k_norm.reshape(1, HD)
    w["q_norm"] = q_norm.reshape(2, 1, HD)
    w["attn_sinks"] = attn_sinks.reshape(2, 1, NH)

    loss_blk, grad_x, gbig, arrived, leftover, gs = local_step(x, loss_target, full["meta_tokens"], w, mats, shards)

    packed = pack_small(gs, loss_blk)

    grads, delta, new_m, new_v = {}, {}, {}, {}
    waiting = {nm.rstrip("0123456789") for nm, _ in leftover}
    leftover = sorted(leftover, key=lambda rider: -gbig[rider[0]].size)
    order = sorted([k for k in big_names if k not in waiting], key=lambda k: -wts[k].size) + [k for k in big_names if k in waiting]
    small8 = None
    for k in order:
        flat2 = wts[k].ndim == 2
        as3 = (lambda a: a[None]) if flat2 else (lambda a: a)
        riders = [leftover.pop(0)] if leftover and k not in waiting else []
        gat = [packed] if small8 is None and not riders else []
        parts = [arrived[k]] if flat2 else [arrived[f"{k}{i}"] for i in range(wts[k].shape[0])]
        outs, got_x, got_g = adamw_big(as3(wts[k]), as3(mom[k]), as3(var[k]), parts, "adamw_" + k,
                                       xch=[(gbig[nm], kind) for nm, kind in riders], gat=gat)
        arrived.update(zip([nm for nm, _ in riders], got_x))
        if gat:
            small8 = got_g[0]
        grads[k], delta[k], new_m[k], new_v[k] = [o[0] if flat2 else (jnp.swapaxes(o, 1, 2) if k in TRANSPOSED else o) for o in outs]
    assert not leftover and small8 is not None
    loss, small = adamw_small(small8, wts, mom, var)
    for k, (g_, d_, m_, v_) in small.items():
        grads[k], delta[k], new_m[k], new_v[k] = g_, d_, m_, v_
    return (loss.reshape(()), grad_x, *[grads[k] for k in NAMES], *[delta[k] for k in NAMES], *[new_m[k] for k in NAMES],
            *[new_v[k] for k in NAMES])
```

```python
import functools

import jax
import jax.numpy as jnp
from jax import lax
from jax.experimental import pallas as pl
from jax.experimental.pallas import tpu as pltpu

F32 = jnp.float32
BF = jnp.bfloat16

D = 1024
DFF = 2816
NH = 16
NKV = 4
HD = 64
KVD = NKV * HD
NMETA = 16
CW = 31
HALO = 32
CHUNK = 32
QB = 128
EPS = 1e-6
NEG = -1e30
NDEV = 8
SCALE = HD ** -0.5

LR, B1, B2, AEPS, WD, STEP = 0.001, 0.9, 0.999, 1e-08, 0.01, 10

VMEM_LIMIT = 56 * 2 ** 20
MESH = pl.DeviceIdType.MESH


def _cp(n):
    return pltpu.CompilerParams(dimension_semantics=("arbitrary",) * n, vmem_limit_bytes=VMEM_LIMIT)


def _row(tm, c):
    return pl.BlockSpec((tm, c), lambda i: (i, 0))


def _res(shape):
    return pl.BlockSpec(shape, lambda i: (0,) * len(shape), pipeline_mode=pl.Buffered(1))


def _lay(l, shape):
    return pl.BlockSpec((None,) + tuple(shape), lambda i: (l,) + (0,) * len(shape), pipeline_mode=pl.Buffered(1))


def _acc(shape):
    return pl.BlockSpec(shape, lambda i: (0,) * len(shape))


def _sds(shape, dt):
    return jax.ShapeDtypeStruct(tuple(shape), dt)


def _dot(a, b):
    return jnp.dot(a.astype(BF), b.astype(BF), preferred_element_type=F32)


def _dot_nt(a, b):
    return lax.dot_general(a.astype(BF), b.astype(BF), (((1,), (1,)), ((), ())), preferred_element_type=F32)


def _dot_tn(a, b):
    return lax.dot_general(a.astype(BF), b.astype(BF), (((0,), (0,)), ((), ())), preferred_element_type=F32)


def _rstd(x):
    return lax.rsqrt(jnp.mean(x * x, axis=-1, keepdims=True) + EPS)


def _rms_bwd(x, g, dy):
    r = _rstd(x)
    z = dy * g
    dx = r * z - x * (r * r * r * jnp.mean(z * x, axis=-1, keepdims=True))
    return dx, jnp.sum(dy * x * r, axis=0, keepdims=True)


def _sig(x):
    return jax.nn.sigmoid(x)


def _fold8(x):
    out = x[0:8]
    for k in range(1, x.shape[0] // 8):
        out = out + x[8 * k:8 * k + 8]
    return out


def _shifted(win):
    return [win] + [pltpu.roll(win, 2 * CHUNK - rho, 0) for rho in range(1, 8)]


def _tap(phases, o):
    return phases[o % 8][8 * (o // 8):8 * (o // 8) + CHUNK]


def _init(ref, first):
    @pl.when(first)
    def _():
        ref[...] = jnp.zeros_like(ref)


def _my_index():
    return 4 * lax.axis_index("x") + 2 * lax.axis_index("y") + lax.axis_index("c")


def _coords(idx):
    return (idx // 4, (idx // 2) % 2, idx % 2)


ALL = tuple(range(NDEV))
H1, H2 = (0, 1, 2, 4, 6), (3, 5, 7)


def _xch_shapes(xch):
    return [_sds((len(ks),) + ((a.shape[0] // NDEV, a.shape[1]) if k == "rows" else tuple(a.shape[1:])), a.dtype) for a, k, ks in xch]


def _xch_scratch(n):
    return [pltpu.SemaphoreType.DMA((n, NDEV)), pltpu.SemaphoreType.DMA((n, NDEV)), pltpu.SemaphoreType.DMA((n,))]


def _xch_copies(meta, srcs, outs, sems, arrivals):
    send_sems, recv_sems, local_sems = sems
    me = _my_index()

    def piece(a, p):
        if meta[a][0] == "rows":
            r = srcs[a].shape[0] // NDEV
            return srcs[a].at[pl.ds(p * r, r), :]
        return srcs[a].at[p]

    def remote(a, i, k, src):
        return pltpu.make_async_remote_copy(
            src_ref=src, dst_ref=outs[a].at[i], send_sem=send_sems.at[a, k], recv_sem=recv_sems.at[a, k],
            device_id=_coords(me ^ k), device_id_type=MESH)

    local, sends, recvs = [], [], []
    for a, (_, ks) in enumerate(meta):
        for i, k in enumerate(ks):
            if k == 0:
                local.append(pltpu.make_async_copy(piece(a, me), outs[a].at[i], local_sems.at[a]))
            else:
                sends.append(remote(a, i, k, piece(a, me ^ k)))
                if arrivals:
                    recvs.append(remote(a, i, k, piece(a, me)))
    return local, sends, recvs


def _xch_start(meta, srcs, outs, sems):
    local, sends, _ = _xch_copies(meta, srcs, outs, sems, False)
    for cp in local + sends:
        cp.start()


def _xch_wait(meta, srcs, outs, sems):
    local, sends, recvs = _xch_copies(meta, srcs, outs, sems, True)
    for cp in recvs:
        cp.wait_recv()
    for cp in sends:
        cp.wait_send()
    for cp in local:
        cp.wait()


def _gat_copies(srcs, outs, sems):
    send_sems, recv_sems, local_sems = sems
    x, y, c = lax.axis_index("x"), lax.axis_index("y"), lax.axis_index("c")
    me, sibling = (x, y, c), (x, y, 1 - c)
    chips = [(1 - x, y), (x, 1 - y), (1 - x, 1 - y)]

    def copy(a, k, owner, to, from_block=False):
        slot = outs[a].at[4 * owner[0] + 2 * owner[1] + owner[2]]
        return pltpu.make_async_remote_copy(
            src_ref=srcs[a] if from_block else slot, dst_ref=slot, send_sem=send_sems.at[a, k], recv_sem=recv_sems.at[a, k],
            device_id=to, device_id_type=MESH)

    n = len(srcs)
    local = lambda: [pltpu.make_async_copy(srcs[a], outs[a].at[4 * x + 2 * y + c], local_sems.at[a]) for a in range(n)]
    first = lambda: [cp for a in range(n) for cp in
                     [copy(a, 0, me, sibling, True)] + [copy(a, 1 + j, me, (*chip, c), True) for j, chip in enumerate(chips)]]
    landed = lambda: [copy(a, 1 + j, (*chip, c), me) for a in range(n) for j, chip in enumerate(chips)]
    passed = lambda: [copy(a, 4 + j, (*chip, c), sibling) for a in range(n) for j, chip in enumerate(chips)]
    final = lambda: [cp for a in range(n) for cp in
                     [copy(a, 0, sibling, me)] + [copy(a, 4 + j, (*chip, 1 - c), me) for j, chip in enumerate(chips)]]
    return local, first, landed, passed, final


def _gat_start(srcs, outs, sems):
    local, first, _, _, _ = _gat_copies(srcs, outs, sems)
    for cp in local() + first():
        cp.start()


def _gat_forward(srcs, outs, sems):
    _, _, landed, passed, _ = _gat_copies(srcs, outs, sems)
    for got, on in zip(landed(), passed()):
        got.wait_recv()
        on.start()


def _gat_wait(srcs, outs, sems):
    local, first, _, passed, final = _gat_copies(srcs, outs, sems)
    for cp in final():
        cp.wait_recv()
    for cp in first() + passed():
        cp.wait_send()
    for cp in local():
        cp.wait()


def _call(body, name, grid, in_specs, out_specs, out_shape, args, scratch=(), xch=(), gat=()):
    n_in, n_out, n_x, n_g, n_s = len(in_specs), len(out_specs), len(xch), len(gat), len(scratch)
    kinds = [(k, ks) for _, k, ks in xch]
    total = 1
    for g in grid:
        total *= g

    def wrapped(*refs):
        ins, refs = refs[:n_in], refs[n_in:]
        x_src, refs = refs[:n_x], refs[n_x:]
        g_src, refs = refs[:n_g], refs[n_g:]
        outs, refs = refs[:n_out], refs[n_out:]
        x_out, refs = refs[:n_x], refs[n_x:]
        g_out, refs = refs[:n_g], refs[n_g:]
        own, refs = refs[:n_s], refs[n_s:]
        x_sems, g_sems = (refs[:3], refs[3:]) if n_x else ((), refs)
        step = pl.program_id(0)
        for d in range(1, len(grid)):
            step = step * grid[d] + pl.program_id(d)
        if n_x or n_g:
            @pl.when(step == 0)
            def _():
                if n_x:
                    _xch_start(kinds, x_src, x_out, x_sems)
                if n_g:
                    _gat_start(g_src, g_out, g_sems)

        body(*ins, *outs, *own)
        if n_g:
            @pl.when(step == max(total - 2, 0))
            def _():
                _gat_forward(g_src, g_out, g_sems)

        if n_x or n_g:
            @pl.when(step == total - 1)
            def _():
                if n_x:
                    _xch_wait(kinds, x_src, x_out, x_sems)
                if n_g:
                    _gat_wait(g_src, g_out, g_sems)

    any_spec = pl.BlockSpec(memory_space=pl.ANY)
    g_shapes = [_sds((NDEV,) + tuple(a.shape), a.dtype) for a in gat]
    res = pl.pallas_call(
        wrapped, name=name, grid=grid, in_specs=list(in_specs) + [any_spec] * (n_x + n_g),
        out_specs=list(out_specs) + [any_spec] * (n_x + n_g), out_shape=list(out_shape) + _xch_shapes(xch) + g_shapes,
        scratch_shapes=list(scratch) + (_xch_scratch(n_x) if n_x else []) + (_xch_scratch(n_g) if n_g else []),
        compiler_params=_cp(len(grid)),
    )(*args, *[a for a, _, _ in xch], *gat)
    return res[:n_out], res[n_out:n_out + n_x], res[n_out + n_x:]


def embed(x, meta8, lp):
    bl, seq, _ = x.shape
    c8 = D // NDEV
    cb = 2 * c8

    def body(x_ref, m_ref, h_ref):
        h_ref[0:NMETA, :] = jnp.concatenate([m_ref[0], m_ref[1]], axis=1)
        h_ref[NMETA:NMETA + seq, :] = x_ref[...]
        h_ref[NMETA + seq:, :] = jnp.zeros((lp - NMETA - seq, cb), F32)

    return pl.pallas_call(
        body, name="embed", grid=(bl, D // cb),
        in_specs=[pl.BlockSpec((None, seq, cb), lambda b, c: (b, 0, c)), pl.BlockSpec((2, NMETA, c8), lambda b, c: (c, 0, 0))],
        out_specs=pl.BlockSpec((None, lp, cb), lambda b, c: (b, 0, c)), out_shape=_sds((bl, lp, D), F32),
        compiler_params=_cp(2),
    )(x, meta8)


def conv_in_fwd(h, nm, l, w_in, b_in, i, tm):
    t = h.shape[0]

    def body(h_ref, g_ref, w_ref, b_ref, u_ref, big_ref, a_ref):
        x = h_ref[...]
        ub = (x * _rstd(x) * g_ref[...]).astype(BF)
        u_ref[...] = ub
        big = jnp.dot(ub, w_ref[...], preferred_element_type=F32) + b_ref[...]
        big_ref[...] = big
        a_ref[...] = big[:, :D] * _sig(big[:, D:])

    return pl.pallas_call(
        body, name=f"conv_in_fwd{i}", grid=(t // tm,),
        in_specs=[_row(tm, D), _lay(l, (1, D)), _res((D, 2 * D)), _lay(i, (1, 2 * D))],
        out_specs=[_row(tm, D), _row(tm, 2 * D), _row(tm, D)],
        out_shape=[_sds((t, D), BF), _sds((t, 2 * D), F32), _sds((t, D), F32)],
        compiler_params=_cp(1),
    )(h, nm, w_in, b_in)


def _prev_halo(tm):
    return pl.BlockSpec((HALO, D), lambda i: (jnp.maximum(i * (tm // HALO) - 1, 0), 0))


def _next_halo(tm, t):
    return pl.BlockSpec((HALO, D), lambda i: (jnp.minimum((i + 1) * (tm // HALO), t // HALO - 1), 0))


def conv_mid_fwd(a, dw, ln_g, ln_b, i, tm, tpb, gat):
    t = a.shape[0]

    def body(a_ref, halo_ref, dw_ref, g_ref, b_ref, c_ref, s_ref, ext):
        first = pl.program_id(0) % tpb == 0
        ext[0:HALO] = jnp.where(first, 0.0, halo_ref[...])
        ext[HALO:] = a_ref[...]

        def chunk(k, carry):
            r0 = pl.multiple_of(k * CHUNK, CHUNK)
            win = _shifted(ext[pl.ds(r0, 2 * CHUNK), :])
            c = jnp.zeros((CHUNK, D), F32)
            for j in range(CW):
                c = c + dw_ref[j:j + 1, :] * _tap(win, j + 2)
            c_ref[pl.ds(r0, CHUNK), :] = c
            mu = jnp.mean(c, axis=-1, keepdims=True)
            xc = c - mu
            n = xc * lax.rsqrt(jnp.mean(xc * xc, axis=-1, keepdims=True) + EPS) * g_ref[...] + b_ref[...]
            s_ref[pl.ds(r0, CHUNK), :] = (n * _sig(n)).astype(BF)
            return carry

        lax.fori_loop(0, tm // CHUNK, chunk, 0)

    return _call(
        body, f"conv_mid_fwd{i}", (t // tm,),
        [_row(tm, D), _prev_halo(tm), _lay(i, (CW, D)), _lay(i, (1, D)), _lay(i, (1, D))],
        [_row(tm, D), _row(tm, D)], [_sds((t, D), F32), _sds((t, D), BF)], (a, a, dw, ln_g, ln_b),
        scratch=[pltpu.VMEM((tm + HALO, D), F32)], gat=gat)


def mixer_ffn_fwd(h, s, w_out, lw, bias, nf, l, wg, wu, wd, tm, gat):
    t = h.shape[0]

    def body(*refs):
        if bias is None:
            h_ref, s_ref, w_ref, nf_ref, wg_ref, wu_ref, wd_ref, h1_ref, u_ref, g_ref, up_ref, hid_ref, h2_ref = refs
            y = 0.0
        else:
            h_ref, s_ref, w_ref, b_ref, nf_ref, wg_ref, wu_ref, wd_ref, h1_ref, u_ref, g_ref, up_ref, hid_ref, h2_ref = refs
            y = b_ref[...]
        h1 = h_ref[...] + (jnp.dot(s_ref[...], w_ref[...], preferred_element_type=F32) + y)
        h1_ref[...] = h1
        ub = (h1 * _rstd(h1) * nf_ref[...]).astype(BF)
        u_ref[...] = ub
        g = _dot_nt(ub, wg_ref[...])
        up = _dot_nt(ub, wu_ref[...])
        g_ref[...] = g.astype(BF)
        up_ref[...] = up.astype(BF)
        hid = (g * _sig(g) * up).astype(BF)
        hid_ref[...] = hid
        h2_ref[...] = h1 + jnp.dot(hid, wd_ref[...], preferred_element_type=F32)

    ins = [h, s, w_out] + ([] if bias is None else [bias]) + [nf, wg, wu, wd]
    specs = ([_row(tm, D), _row(tm, D), _res((D, D))] + ([] if bias is None else [_lay(lw, (1, D))])
             + [_lay(l, (1, D)), _res((DFF, D)), _res((DFF, D)), _res((DFF, D))])
    return _call(
        body, f"mixer_ffn_fwd{l}", (t // tm,), specs,
        [_row(tm, D), _row(tm, D), _row(tm, DFF), _row(tm, DFF), _row(tm, DFF), _row(tm, D)],
        [_sds((t, D), F32), _sds((t, D), BF), _sds((t, DFF), BF), _sds((t, DFF), BF), _sds((t, DFF), BF), _sds((t, D), F32)],
        ins, gat=gat)


def _seg_rms(x, g, nseg):
    outs = []
    for s in range(nseg):
        xs = x[:, HD * s:HD * s + HD]
        outs.append(xs * _rstd(xs) * g)
    return jnp.concatenate(outs, axis=1)


def kv_fwd(h, kvn, w_kv, kng, tm):
    t = h.shape[0]

    def body(h_ref, g_ref, w_ref, kg_ref, kn_ref, kv_ref, k_ref, v_ref):
        x = h_ref[...]
        kn = (x * _rstd(x) * g_ref[...]).astype(BF)
        kn_ref[...] = kn
        kv = jnp.dot(kn, w_ref[...], preferred_element_type=F32)
        kv_ref[...] = kv
        k_ref[...] = _seg_rms(kv[:, :KVD], kg_ref[...], NKV).astype(BF)
        v_ref[...] = kv[:, KVD:].astype(BF)

    return pl.pallas_call(
        body, name="kv_fwd", grid=(t // tm,),
        in_specs=[_row(tm, D), _res((1, D)), _res((D, 2 * KVD)), _res((1, HD))],
        out_specs=[_row(tm, D), _row(tm, 2 * KVD), _row(tm, KVD), _row(tm, KVD)],
        out_shape=[_sds((t, D), BF), _sds((t, 2 * KVD), F32), _sds((t, KVD), BF), _sds((t, KVD), BF)],
        compiler_params=_cp(1),
    )(h, kvn, w_kv, kng)


def q_fwd(h, nm, l, w_q, j, tm):
    t = h.shape[0]

    def body(h_ref, g_ref, w_ref, u_ref, q_ref):
        x = h_ref[...]
        ub = (x * _rstd(x) * g_ref[...]).astype(BF)
        u_ref[...] = ub
        q_ref[...] = jnp.dot(ub, w_ref[...], preferred_element_type=F32)

    return pl.pallas_call(
        body, name=f"q_fwd{j}", grid=(t // tm,),
        in_specs=[_row(tm, D), _lay(l, (1, D)), _res((D, D))],
        out_specs=[_row(tm, D), _row(tm, D)], out_shape=[_sds((t, D), BF), _sds((t, D), F32)],
        compiler_params=_cp(1),
    )(h, nm, w_q)


RQ = NH // NKV


def _attn_specs(nb, lp):
    cur = lambda c: pl.BlockSpec((QB, c), lambda b, n: (b * nb + n, 0))
    seq = pl.BlockSpec((None, lp, KVD), lambda b, n: (b, 0, 0))
    seq_t = pl.BlockSpec((None, KVD, lp), lambda b, n: (b, 0, 0))
    return cur, seq, seq_t


NKEYS = 2 * QB + NMETA


def _attn_mask(n, start):
    shape = (RQ * QB, NKEYS)
    qpos = n * QB + (lax.broadcasted_iota(jnp.int32, shape, 0) & (QB - 1))
    col = lax.broadcasted_iota(jnp.int32, shape, 1)
    in_band = col < 2 * QB
    kpos = jnp.where(in_band, start + col, col - 2 * QB)
    return (kpos <= qpos) & ((col >= 2 * QB) | ((qpos - kpos < QB) & (kpos >= NMETA)))


def _keys(ref, band, gs):
    return jnp.concatenate([ref[band, gs], ref[0:NMETA, gs]], axis=0)


def _keys_t(ref, band, gs):
    return jnp.concatenate([ref[gs, band], ref[gs, 0:NMETA]], axis=1)


def transpose_seq(a, name):
    bl, r, c = a.shape

    def body(a_ref, o_ref):
        o_ref[...] = a_ref[...].T

    return pl.pallas_call(
        body, name=name, grid=(bl,), in_specs=[pl.BlockSpec((None, r, c), lambda b: (b, 0, 0))],
        out_specs=pl.BlockSpec((None, c, r), lambda b: (b, 0, 0)), out_shape=_sds((bl, c, r), a.dtype), compiler_params=_cp(1),
    )(a)


def sum_transposed(a0, a1):
    bl, c, r = a0.shape

    def body(a0_ref, a1_ref, o_ref):
        o_ref[...] = (a0_ref[...] + a1_ref[...]).T

    spec = pl.BlockSpec((None, c, r), lambda b: (b, 0, 0))
    return pl.pallas_call(
        body, name="sum_transposed", grid=(bl,), in_specs=[spec, spec],
        out_specs=pl.BlockSpec((r, c), lambda b: (b, 0)), out_shape=_sds((bl * r, c), a0.dtype), compiler_params=_cp(1),
    )(a0, a1)


def _stack_heads(ref, g, fn):
    return jnp.concatenate([fn(ref[:, HD * (g * RQ + r):HD * (g * RQ + r) + HD]) for r in range(RQ)], axis=0)


def _stack_cols(ref, g):
    return jnp.concatenate([ref[:, g * RQ + r:g * RQ + r + 1] for r in range(RQ)], axis=0)


def _stack_sinks(sk_ref, g):
    return jnp.concatenate([jnp.broadcast_to(sk_ref[:, g * RQ + r:g * RQ + r + 1], (QB, 1)) for r in range(RQ)], axis=0)


def attn_fwd(q, kt, v, qg, sinks, j, bl, lp, gat):
    t = q.shape[0]
    nb = lp // QB
    cur, seq, seq_t = _attn_specs(nb, lp)

    def body(q_ref, kt_ref, v_ref, qg_ref, sk_ref, o_ref, lse_ref):
        n = pl.program_id(1)
        start = pl.multiple_of(jnp.maximum(n - 1, 0) * QB, QB)
        mask = _attn_mask(n, start)
        band = pl.ds(start, 2 * QB)
        lane = lax.broadcasted_iota(jnp.int32, (QB, NH), 1)
        ones = jnp.ones((NKEYS, HD), BF)
        lse = jnp.zeros((QB, NH), F32)
        for g in range(NKV):
            gs = slice(HD * g, HD * g + HD)
            qn = _stack_heads(q_ref, g, lambda x: (x * _rstd(x) * qg_ref[...]).astype(BF))
            sink = _stack_sinks(sk_ref, g)
            s = jnp.where(mask, _dot(qn, _keys_t(kt_ref, band, gs)) * SCALE, NEG)
            mx = jnp.maximum(jnp.max(s, -1, keepdims=True), sink)
            oa = _dot(jnp.exp(s - mx), jnp.concatenate([_keys(v_ref, band, gs), ones], axis=1))
            den = oa[:, HD:HD + 1] + jnp.exp(sink - mx)
            o = oa[:, :HD] * (1.0 / den)
            l = mx + jnp.log(den)
            for r in range(RQ):
                h = g * RQ + r
                o_ref[:, HD * h:HD * h + HD] = o[r * QB:(r + 1) * QB].astype(BF)
                lse = jnp.where(lane == h, l[r * QB:(r + 1) * QB], lse)
        lse_ref[...] = lse

    return _call(
        body, f"attn_fwd{j}", (bl, nb),
        [cur(D), seq_t, seq, pl.BlockSpec((None, 1, HD), lambda b, n: (j, 0, 0)), pl.BlockSpec((None, 1, NH), lambda b, n: (j, 0, 0))],
        [cur(D), cur(NH)], [_sds((t, D), BF), _sds((t, NH), F32)], (q, kt, v, qg, sinks), gat=gat)


def loss_fwd(h, tgt):
    bl, lp, _ = h.shape
    seq = tgt.shape[1]
    cb = 256

    def body(h_ref, t_ref, dh_ref, loss_ref):
        _init(loss_ref, (pl.program_id(0) == 0) & (pl.program_id(1) == 0))
        err = h_ref[NMETA:NMETA + seq, :] - t_ref[...]
        dh_ref[...] = jnp.zeros_like(dh_ref)
        dh_ref[NMETA:NMETA + seq, :] = err * (1.0 / D)
        loss_ref[...] += (0.5 / D) * jnp.sum(err * err)

    return pl.pallas_call(
        body, name="loss_fwd", grid=(bl, D // cb),
        in_specs=[pl.BlockSpec((None, lp, cb), lambda b, c: (b, 0, c)), pl.BlockSpec((None, seq, cb), lambda b, c: (b, 0, c))],
        out_specs=[pl.BlockSpec((None, lp, cb), lambda b, c: (b, 0, c)), pl.BlockSpec((8, 128), lambda b, c: (0, 0))],
        out_shape=[_sds((bl, lp, D), F32), _sds((8, 128), F32)],
        compiler_params=_cp(2),
    )(h, tgt)


def ffn_bwd_x(dh2, g, up, h1, nf, l, wd, wg, wu, w_o, tm, xch):
    t = dh2.shape[0]

    def body(dh2_ref, g_ref, up_ref, h1_ref, nf_ref, wd_ref, wg_ref, wu_ref, *rest):
        if w_o is None:
            dg_ref, du_ref, dh1_ref, dnf_ref = rest
        else:
            wo_ref, dg_ref, du_ref, dh1_ref, dnf_ref, do_ref = rest
        _init(dnf_ref, pl.program_id(0) == 0)
        dh2v = dh2_ref[...]
        dhid = _dot_nt(dh2v, wd_ref[...])
        gv = g_ref[...].astype(F32)
        uv = up_ref[...].astype(F32)
        sg = _sig(gv)
        dgv = (dhid * uv * (sg * (1.0 + gv * (1.0 - sg)))).astype(BF)
        duv = (dhid * (gv * sg)).astype(BF)
        dg_ref[...] = dgv
        du_ref[...] = duv
        dnorm = _dot(dgv, wg_ref[...]) + _dot(duv, wu_ref[...])
        dx, dnf = _rms_bwd(h1_ref[...], nf_ref[...], dnorm)
        dh1 = dh2v + dx
        dh1_ref[...] = dh1
        dnf_ref[...] += dnf
        if w_o is not None:
            do_ref[...] = _dot_nt(dh1, wo_ref[...]).astype(BF)

    attn = w_o is not None
    return _call(
        body, f"ffn_bwd_x{l}", (t // tm,),
        [_row(tm, D), _row(tm, DFF), _row(tm, DFF), _row(tm, D), _lay(l, (1, D)),
         _res((DFF, D)), _res((DFF, D)), _res((DFF, D))] + ([_res((D, D))] if attn else []),
        [_row(tm, DFF), _row(tm, DFF), _row(tm, D), _acc((1, D))] + ([_row(tm, D)] if attn else []),
        [_sds((t, DFF), BF), _sds((t, DFF), BF), _sds((t, D), F32), _sds((1, D), F32)] + ([_sds((t, D), BF)] if attn else []),
        (dh2, g, up, h1, nf, wd, wg, wu) + ((w_o,) if attn else ()), xch=xch)


def mm_tn(x, dy, tm, name, split=False, transposed=False, xch=()):
    t, kk = x.shape
    nn = dy.shape[1]
    n8 = nn // NDEV
    nsteps = t // tm

    def body(x_ref, dy_ref, o_ref, acc):
        i = pl.program_id(0)
        _init(acc, i == 0)
        acc[...] += _dot_tn(x_ref[...], dy_ref[...])

        @pl.when(i == nsteps - 1)
        def _():
            if split:
                for p in range(NDEV):
                    o_ref[p] = acc[:, p * n8:(p + 1) * n8].astype(BF)
            elif transposed:
                o_ref[...] = acc[...].T.astype(BF)
            else:
                o_ref[...] = acc[...].astype(BF)

    oshape = (NDEV, kk, n8) if split else ((nn, kk) if transposed else (kk, nn))
    (out,), got, _ = _call(body, name, (nsteps,), [_row(tm, kk), _row(tm, nn)], [_acc(oshape)], [_sds(oshape, BF)], (x, dy),
                           scratch=[pltpu.VMEM((kk, nn), F32)], xch=xch)
    return out, got


def proj_bwd(dy, w, h, g, lg, dh_in, tm, name, xch=()):
    t = h.shape[0]
    nn = dy.shape[1]
    wspec = _res(w.shape)
    gspec = _res((1, D)) if lg is None else _lay(lg, (1, D))

    def body(dy_ref, w_ref, h_ref, g_ref, dhin_ref, dh_ref, dg_ref):
        _init(dg_ref, pl.program_id(0) == 0)
        du = _dot_nt(dy_ref[...], w_ref[...])
        dx, dg = _rms_bwd(h_ref[...], g_ref[...], du)
        dh_ref[...] = dhin_ref[...] + dx
        dg_ref[...] += dg

    return _call(body, name, (t // tm,), [_row(tm, nn), wspec, _row(tm, D), gspec, _row(tm, D)],
                 [_row(tm, D), _acc((1, D))], [_sds((t, D), F32), _sds((1, D), F32)], (dy, w, h, g, dh_in), xch=xch)


def attn_bwd(q, k, kt, vt, do, o, lse, qg, sinks, j, bl, lp, xch):
    t = q.shape[0]
    nb = lp // QB
    cur, seq, seq_t = _attn_specs(nb, lp)

    def body(q_ref, k_ref, kt_ref, vt_ref, do_ref, o_ref, lse_ref, qg_ref, sk_ref, dq_ref, dk_ref, dv_ref, dqg_ref, dsk_ref):
        b, n = pl.program_id(0), pl.program_id(1)
        _init(dk_ref, n == 0)
        _init(dv_ref, n == 0)
        _init(dqg_ref, (b == 0) & (n == 0))
        _init(dsk_ref, (b == 0) & (n == 0))
        start = pl.multiple_of(jnp.maximum(n - 1, 0) * QB, QB)
        mask = _attn_mask(n, start)
        band = pl.ds(start, 2 * QB)
        lane = lax.broadcasted_iota(jnp.int32, (1, NH), 1)
        dqg = jnp.zeros((1, HD), F32)
        dsk = jnp.zeros((1, NH), F32)
        for g in range(NKV):
            gs = slice(HD * g, HD * g + HD)
            kk = _keys(k_ref, band, gs)
            qh = _stack_heads(q_ref, g, lambda x: x)
            rs = _rstd(qh)
            qn = (qh * rs * qg_ref[...]).astype(BF)
            ls = _stack_cols(lse_ref, g)
            pr = jnp.where(mask, jnp.exp(_dot(qn, _keys_t(kt_ref, band, gs)) * SCALE - ls), 0.0)
            doh = _stack_heads(do_ref, g, lambda x: x)
            delta = jnp.sum(doh.astype(F32) * _stack_heads(o_ref, g, lambda x: x).astype(F32), axis=-1, keepdims=True)
            ds = (pr * (_dot(doh, _keys_t(vt_ref, band, gs)) - delta)).astype(BF)
            dqn = _dot(ds, kk) * SCALE
            dkt = _dot_tn(qn, ds) * SCALE
            dvt = _dot_tn(doh, pr)
            dk_ref[gs, band] += dkt[:, :2 * QB]
            dv_ref[gs, band] += dvt[:, :2 * QB]
            dk_ref[gs, 0:NMETA] += dkt[:, 2 * QB:]
            dv_ref[gs, 0:NMETA] += dvt[:, 2 * QB:]
            dsink = jnp.exp(_stack_sinks(sk_ref, g) - ls) * delta
            z = dqn * qg_ref[...]
            dq = rs * z - qh * (rs * rs * rs * jnp.mean(z * qh, axis=-1, keepdims=True))
            dqg = dqg + jnp.sum(dqn * qh * rs, axis=0, keepdims=True)
            for r in range(RQ):
                h = g * RQ + r
                dq_ref[:, HD * h:HD * h + HD] = dq[r * QB:(r + 1) * QB]
                dsk = dsk + jnp.where(lane == h, -jnp.sum(dsink[r * QB:(r + 1) * QB]), 0.0)
        dqg_ref[...] += dqg
        dsk_ref[...] += dsk

    return _call(
        body, f"attn_bwd{j}", (bl, nb),
        [cur(D), seq, seq_t, seq_t, cur(D), cur(D), cur(NH),
         pl.BlockSpec((None, 1, HD), lambda b, n: (j, 0, 0)), pl.BlockSpec((None, 1, NH), lambda b, n: (j, 0, 0))],
        [cur(D), seq_t, seq_t, pl.BlockSpec((1, HD), lambda b, n: (0, 0)), pl.BlockSpec((1, NH), lambda b, n: (0, 0))],
        [_sds((t, D), F32), _sds((bl, KVD, lp), F32), _sds((bl, KVD, lp), F32), _sds((1, HD), F32), _sds((1, NH), F32)],
        (q, k, kt, vt, do, o, lse, qg, sinks), xch=xch)


def kv_bwd_pre(dk, dv, kv, kng, tm):
    t = kv.shape[0]

    def body(dk_ref, dv_ref, kv_ref, g_ref, dkv_ref, dg_ref):
        _init(dg_ref, pl.program_id(0) == 0)
        dg = jnp.zeros((1, HD), F32)
        outs = []
        for s in range(NKV):
            sl = slice(HD * s, HD * s + HD)
            dx, dgs = _rms_bwd(kv_ref[:, sl], g_ref[...], dk_ref[:, sl])
            outs.append(dx)
            dg = dg + dgs
        dkv_ref[:, :KVD] = jnp.concatenate(outs, axis=1).astype(BF)
        dkv_ref[:, KVD:] = dv_ref[...].astype(BF)
        dg_ref[...] += dg

    return pl.pallas_call(
        body, name="kv_bwd_pre", grid=(t // tm,),
        in_specs=[_row(tm, KVD)] * 2 + [_row(tm, 2 * KVD), _res((1, HD))],
        out_specs=[_row(tm, 2 * KVD), _acc((1, HD))], out_shape=[_sds((t, 2 * KVD), BF), _sds((1, HD), F32)],
        compiler_params=_cp(1),
    )(dk, dv, kv, kng)


def conv_out_bwd(dh1, c, ln_g, ln_b, w_out, i, tm, xch):
    t = dh1.shape[0]

    def body(dh1_ref, c_ref, g_ref, b_ref, w_ref, dc_ref, dg_ref, db_ref, dbo_ref):
        first = pl.program_id(0) == 0
        _init(dg_ref, first)
        _init(db_ref, first)
        _init(dbo_ref, first)
        dh1v = dh1_ref[...]
        ds = _dot_nt(dh1v, w_ref[...])
        cv = c_ref[...]
        xc = cv - jnp.mean(cv, axis=-1, keepdims=True)
        rstd = lax.rsqrt(jnp.mean(xc * xc, axis=-1, keepdims=True) + EPS)
        xh = xc * rstd
        n = xh * g_ref[...] + b_ref[...]
        sg = _sig(n)
        dn = ds * (sg * (1.0 + n * (1.0 - sg)))
        dxh = dn * g_ref[...]
        dc_ref[...] = rstd * (dxh - jnp.mean(dxh, axis=-1, keepdims=True) - xh * jnp.mean(dxh * xh, axis=-1, keepdims=True))
        dg_ref[...] += jnp.sum(dn * xh, axis=0, keepdims=True)
        db_ref[...] += jnp.sum(dn, axis=0, keepdims=True)
        dbo_ref[...] += jnp.sum(dh1v, axis=0, keepdims=True)

    return _call(
        body, f"conv_out_bwd{i}", (t // tm,), [_row(tm, D), _row(tm, D), _lay(i, (1, D)), _lay(i, (1, D)), _res((D, D))],
        [_row(tm, D), _acc((1, D)), _acc((1, D)), _acc((1, D))], [_sds((t, D), F32)] + [_sds((1, D), F32)] * 3,
        (dh1, c, ln_g, ln_b, w_out), xch=xch)


def conv_mid_bwd(dc, a, big, dw, i, tm, tpb, xch):
    t = dc.shape[0]
    nsteps = t // tm

    def body(dc_ref, nxt_ref, a_ref, prv_ref, big_ref, dw_ref, da_ref, dbin_ref, ddw_ref, dce, ae, wacc, bacc):
        i_ = pl.program_id(0)
        _init(wacc, i_ == 0)
        _init(bacc, i_ == 0)
        dce[0:tm] = dc_ref[...]
        dce[tm:] = jnp.where(i_ % tpb == tpb - 1, 0.0, nxt_ref[...])
        ae[0:HALO] = jnp.where(i_ % tpb == 0, 0.0, prv_ref[...])
        ae[HALO:] = a_ref[...]

        def chunk(k, carry):
            r0 = pl.multiple_of(k * CHUNK, CHUNK)
            wdc = _shifted(dce[pl.ds(r0, 2 * CHUNK), :])
            wa = _shifted(ae[pl.ds(r0, 2 * CHUNK), :])
            dcc = wdc[0][0:CHUNK]
            da = jnp.zeros((CHUNK, D), F32)
            for j in range(CW):
                da = da + dw_ref[j:j + 1, :] * _tap(wdc, CW - 1 - j)
                wacc[j] += _fold8(dcc * _tap(wa, j + 2))
            bv = big_ref[pl.ds(r0, CHUNK), :]
            a1, sg = bv[:, :D], _sig(bv[:, D:])
            d1 = da * sg
            d2 = da * a1 * sg * (1.0 - sg)
            da_ref[pl.ds(r0, CHUNK), 0:D] = d1.astype(BF)
            da_ref[pl.ds(r0, CHUNK), D:2 * D] = d2.astype(BF)
            bacc[:, 0:D] += _fold8(d1)
            bacc[:, D:2 * D] += _fold8(d2)
            return carry

        lax.fori_loop(0, tm // CHUNK, chunk, 0)

        @pl.when(i_ == nsteps - 1)
        def _():
            dbin_ref[...] = jnp.sum(bacc[...], axis=0, keepdims=True)
            ddw_ref[...] = jnp.sum(wacc[...], axis=1)

    return _call(
        body, f"conv_mid_bwd{i}", (nsteps,),
        [_row(tm, D), _next_halo(tm, t), _row(tm, D), _prev_halo(tm), _row(tm, 2 * D), _lay(i, (CW, D))],
        [_row(tm, 2 * D), _acc((1, 2 * D)), _acc((CW + 1, D))],
        [_sds((t, 2 * D), BF), _sds((1, 2 * D), F32), _sds((CW + 1, D), F32)],
        (dc, dc, a, a, big, dw),
        scratch=[pltpu.VMEM((tm + HALO, D), F32), pltpu.VMEM((tm + HALO, D), F32),
                 pltpu.VMEM((CW + 1, 8, D), F32), pltpu.VMEM((8, 2 * D), F32)], xch=xch)


def input_grads(dh0, seq):
    bl, lp, _ = dh0.shape
    cb = 256

    def body(dh_ref, gx_ref, gm_ref):
        _init(gm_ref, pl.program_id(1) == 0)
        gx_ref[...] = dh_ref[NMETA:NMETA + seq, :]
        gm_ref[...] += dh_ref[0:NMETA, :]

    return pl.pallas_call(
        body, name="input_grads", grid=(D // cb, bl),
        in_specs=[pl.BlockSpec((None, lp, cb), lambda c, b: (b, 0, c))],
        out_specs=[pl.BlockSpec((None, seq, cb), lambda c, b: (b, 0, c)), pl.BlockSpec((NMETA, cb), lambda c, b: (0, c))],
        out_shape=[_sds((bl, seq, D), F32), _sds((NMETA, D), F32)],
        compiler_params=_cp(2),
    )(dh0)


GATHER_PLAN = {
    "conv_mid_fwd0": [("ffn_w_gate", 0), ("ffn_w_up", 0), ("ffn_w_down", 0)],
    "mixer_ffn_fwd0": [("conv_w_in", 1), ("conv_w_out", 1), ("ffn_w_gate", 1)],
    "conv_mid_fwd1": [("ffn_w_up", 1), ("ffn_w_down", 1), ("w_kv", 0), ("w_q", 0)],
    "mixer_ffn_fwd1": [("w_o", 0), ("ffn_w_down", 2)],
    "attn_fwd0": [("ffn_w_gate", 2), ("ffn_w_up", 2), ("w_q", 1), ("w_o", 1), ("ffn_w_gate", 3), ("ffn_w_up", 3), ("ffn_w_down", 3)],
}
EXCHANGE_PLAN = {
    "attn_bwd1": [("ffn_w_down3", ALL), ("ffn_w_gate3", ALL), ("ffn_w_up3", ALL)],
    "dw_down2": [("w_o1", ALL)],
    "ffn_bwd_x2": [("w_q1", ALL)],
    "attn_bwd0": [("ffn_w_down2", ALL), ("ffn_w_gate2", ALL), ("ffn_w_up2", ALL)],
    "dw_down1": [("w_o0", ALL), ("w_q0", H1)],
    "ffn_bwd_x1": [("w_q0", H2), ("w_kv", ALL)],
    "conv_out_bwd1": [("ffn_w_down1", H1)],
    "conv_mid_bwd1": [("ffn_w_down1", H2), ("ffn_w_gate1", ALL)],
    "dw_conv_in1": [("ffn_w_up1", H1)],
    "conv_in_bwd1": [("ffn_w_up1", H2)],
    "dw_down0": [("conv_w_out1", ALL), ("conv_w_in1", H1)],
    "ffn_bwd_x0": [("conv_w_in1", H2)],
    "dw_gate0": [("ffn_w_down0", H1)],
    "dw_up0": [("ffn_w_down0", H2)],
    "conv_out_bwd0": [("ffn_w_gate0", H1)],
    "conv_mid_bwd0": [("ffn_w_gate0", H2), ("ffn_w_up0", ALL)],
    "dw_conv_in0": [("conv_w_out0", ALL)],
    "conv_in_bwd0": [("conv_w_in0", H1)],
    "tail": [("conv_w_in0", H2)],
}
BIG = {"conv_w_in": "pieces", "conv_w_out": "rows", "w_kv": "rows", "w_q": "rows", "w_o": "rows",
       "ffn_w_gate": "rows", "ffn_w_up": "rows", "ffn_w_down": "rows"}
EXCHANGE_KIND = BIG
TRANSPOSED = ("ffn_w_gate", "ffn_w_up")


def gathered_matrix(name, layer, blocks8):
    if BIG[name] == "rows":
        return blocks8.reshape(NDEV * blocks8.shape[1], blocks8.shape[2])
    return join_columns(blocks8, f"join_{name}{layer}")


def local_step(x, tgt, meta8, w, mats, shards):
    bl, seq, _ = x.shape
    lp = -(-(NMETA + seq) // QB) * QB
    tpb = 4
    tm = lp // tpb
    t = bl * lp
    na = 2
    flat = lambda a: a.reshape(t, D)
    mats = dict(mats)

    def riders(carrier):
        return [shards[key] for key in GATHER_PLAN[carrier]]

    def landed(carrier, blocks):
        for key, b8 in zip(GATHER_PLAN[carrier], blocks):
            mats[key] = gathered_matrix(*key, b8)

    h = flat(embed(x, meta8, lp))
    saved = []
    kvs = None
    for l in range(4):
        rec = {"h": h}
        if l < na:
            rec["u"], rec["big"], rec["a"] = conv_in_fwd(h, w["norm_mix"], l, mats["conv_w_in", l], w["conv_b_in"], l, tm)
            name = f"conv_mid_fwd{l}"
            (rec["c"], rec["s"]), _, got = conv_mid_fwd(rec["a"], w["conv_dw"], w["conv_ln_g"], w["conv_ln_b"], l, tm, tpb,
                                                         riders(name))
            landed(name, got)
            mixed, w_out, lw, bias = rec["s"], mats["conv_w_out", l], l, w["conv_b_out"]
        else:
            j = l - na
            if kvs is None:
                kvs = dict(zip(("kn", "kv", "k", "v"), kv_fwd(h, w["kv_norm"], mats["w_kv", 0], w["k_norm"], tm)))
                kvs["h"] = h
                kvs["k3"], kvs["v3"] = kvs["k"].reshape(bl, lp, KVD), kvs["v"].reshape(bl, lp, KVD)
                kvs["kt"], kvs["vt"] = transpose_seq(kvs["k3"], "transpose_k"), transpose_seq(kvs["v3"], "transpose_v")
            rec["u"], rec["q"] = q_fwd(h, w["norm_mix"], l, mats["w_q", j], j, tm)
            name = f"attn_fwd{j}"
            (rec["o"], rec["lse"]), _, got = attn_fwd(rec["q"], kvs["kt"], kvs["v3"], w["q_norm"], w["attn_sinks"], j, bl, lp,
                                                      riders(name) if name in GATHER_PLAN else [])
            if name in GATHER_PLAN:
                landed(name, got)
            mixed, w_out, lw, bias = rec["o"], mats["w_o", j], j, None
        name = f"mixer_ffn_fwd{l}"
        (rec["h1"], rec["u2"], rec["g"], rec["up"], rec["hid"], h), _, got = mixer_ffn_fwd(
            h, mixed, w_out, lw, bias, w["norm_ffn"], l, mats["ffn_w_gate", l], mats["ffn_w_up", l], mats["ffn_w_down", l], tm // 2,
            riders(name) if name in GATHER_PLAN else [])
        if name in GATHER_PLAN:
            landed(name, got)
        saved.append(rec)

    dh3, loss_blk = loss_fwd(h.reshape(bl, lp, D), tgt)
    dh = flat(dh3)

    big, small, arrived = {}, {}, {}
    dks, dvs = [], []

    def ride(kernel_name):
        return [(big[nm], EXCHANGE_KIND[nm.rstrip("0123456789")], ks) for nm, ks in EXCHANGE_PLAN.get(kernel_name, [])]

    def landed_x(kernel_name, arrivals):
        for (nm, _), got in zip(EXCHANGE_PLAN.get(kernel_name, []), arrivals):
            arrived.setdefault(nm, []).append(got)

    def dw(name, grad, x, dy, **kw):
        big[grad], got = mm_tn(x, dy, 2 * tm, name, xch=ride(name), **kw)
        landed_x(name, got)

    for l in reversed(range(4)):
        rec = saved[l]
        dw(f"dw_down{l}", f"ffn_w_down{l}", rec["hid"], dh)
        name = f"ffn_bwd_x{l}"
        outs, got, _ = ffn_bwd_x(
            dh, rec["g"], rec["up"], rec["h1"], w["norm_ffn"], l, mats["ffn_w_down", l], mats["ffn_w_gate", l], mats["ffn_w_up", l],
            mats["w_o", l - na] if l >= na else None, tm // 2, ride(name))
        landed_x(name, got)
        dg, du, dh1, small[f"norm_ffn{l}"] = outs[:4]
        dw(f"dw_gate{l}", f"ffn_w_gate{l}", rec["u2"], dg, transposed=True)
        dw(f"dw_up{l}", f"ffn_w_up{l}", rec["u2"], du, transposed=True)
        if l >= na:
            j = l - na
            dw(f"dw_o{j}", f"w_o{j}", rec["o"], dh1)
            name = f"attn_bwd{j}"
            (dq, dk, dv, small[f"q_norm{j}"], small[f"attn_sinks{j}"]), got, _ = attn_bwd(
                rec["q"], kvs["k3"], kvs["kt"], kvs["vt"], outs[4], rec["o"], rec["lse"], w["q_norm"], w["attn_sinks"], j, bl, lp,
                ride(name))
            landed_x(name, got)
            dks.append(dk)
            dvs.append(dv)
            dw(f"dw_q{j}", f"w_q{j}", rec["u"], dq)
            dh, small[f"norm_mix{l}"] = proj_bwd(dq, mats["w_q", j], rec["h"], w["norm_mix"], l, dh1, tm, f"q_bwd{j}")[0]
            if l == na:
                dkv, small["k_norm"] = kv_bwd_pre(sum_transposed(*dks), sum_transposed(*dvs), kvs["kv"], w["k_norm"], tm)
                dw("dw_kv", "w_kv", kvs["kn"], dkv)
                dh, small["kv_norm"] = proj_bwd(dkv, mats["w_kv", 0], kvs["h"], w["kv_norm"], None, dh, tm, "kv_bwd")[0]
        else:
            name = f"conv_out_bwd{l}"
            (dc, small[f"conv_ln_g{l}"], small[f"conv_ln_b{l}"], small[f"conv_b_out{l}"]), got, _ = conv_out_bwd(
                dh1, rec["c"], w["conv_ln_g"], w["conv_ln_b"], mats["conv_w_out", l], l, tm, ride(name))
            landed_x(name, got)
            dw(f"dw_conv_out{l}", f"conv_w_out{l}", rec["s"], dh1)
            name = f"conv_mid_bwd{l}"
            (da, small[f"conv_b_in{l}"], small[f"conv_dw{l}"]), got, _ = conv_mid_bwd(
                dc, rec["a"], rec["big"], w["conv_dw"], l, tm, tpb, ride(name))
            landed_x(name, got)
            dw(f"dw_conv_in{l}", f"conv_w_in{l}", rec["u"], da, split=True)
            name = f"conv_in_bwd{l}"
            (dh, small[f"norm_mix{l}"]), got, _ = proj_bwd(da, mats["conv_w_in", l], rec["h"], w["norm_mix"], l, dh1, tm, name,
                                                           ride(name))
            landed_x(name, got)
    grad_x, small["meta_tokens"] = input_grads(dh.reshape(bl, lp, D), seq)
    return loss_blk, grad_x, big, arrived, small


def all_gather_blocks(blocks):
    n = len(blocks)

    def body(*refs):
        srcs, outs, sems = refs[:n], refs[n:2 * n], refs[2 * n:]
        _gat_start(srcs, outs, sems)
        _gat_forward(srcs, outs, sems)
        _gat_wait(srcs, outs, sems)

    any_spec = pl.BlockSpec(memory_space=pl.ANY)
    return pl.pallas_call(
        body, name="all_gather_blocks", out_shape=[_sds((NDEV,) + tuple(a.shape), a.dtype) for a in blocks],
        in_specs=[any_spec] * n, out_specs=[any_spec] * n, scratch_shapes=_xch_scratch(n),
    )(*blocks)


def cast_bf16(ws):
    n = len(ws)
    counts = [1 if x.ndim == 2 else x.shape[0] for x in ws]

    def body(*refs):
        outs = iter(refs[n:])
        for a in range(n):
            for l in range(counts[a]):
                next(outs)[...] = (refs[a][...] if ws[a].ndim == 2 else refs[a][l]).astype(BF)

    flat = pl.pallas_call(
        body, name="cast_bf16", out_shape=[_sds(x.shape[-2:], BF) for x, k in zip(ws, counts) for _ in range(k)],
        compiler_params=pltpu.CompilerParams(vmem_limit_bytes=VMEM_LIMIT),
    )(*ws)
    it = iter(flat)
    return [[next(it) for _ in range(k)] for k in counts]


def join_columns(w8, name):
    _, kk, n8 = w8.shape

    def body(x_ref, o_ref):
        o_ref[...] = jnp.concatenate([x_ref[p] for p in range(NDEV)], axis=1)

    return pl.pallas_call(body, name=name, out_shape=_sds((kk, NDEV * n8), w8.dtype),
                          compiler_params=pltpu.CompilerParams(vmem_limit_bytes=VMEM_LIMIT))(w8)


def _adamw_math(w, m, v, g):
    m2 = B1 * m + (1.0 - B1) * g
    v2 = B2 * v + (1.0 - B2) * (g * g)
    mh = m2 / (1.0 - B1 ** STEP)
    vh = v2 / (1.0 - B2 ** STEP)
    return -LR * (mh / (jnp.sqrt(vh) + AEPS) + WD * w), m2, v2


def adamw_big(w, m, v, parts, name, xch=(), gat=()):
    lyr, r, c = w.shape
    by_cols = c >= 512
    blk = (lyr, r, 256) if by_cols else (lyr, 256 if r % 256 == 0 else r, c)
    imap = (lambda i: (0, 0, i)) if by_cols else (lambda i: (0, i, 0))
    counts = [len(per_layer) for per_layer in parts]

    def body(w_ref, m_ref, v_ref, *rest):
        p_refs, (g_ref, d_ref, m2_ref, v2_ref) = iter(rest[:sum(counts)]), rest[sum(counts):]
        for l in range(lyr):
            g = None
            for _ in range(counts[l]):
                ref = next(p_refs)
                for q in range(ref.shape[0]):
                    g = ref[q].astype(F32) if g is None else g + ref[q].astype(F32)
            g_ref[l] = g
            d_ref[l], m2_ref[l], v2_ref[l] = _adamw_math(w_ref[l], m_ref[l], v_ref[l], g)

    spec = pl.BlockSpec(blk, imap)
    flat = [a for per_layer in parts for a in per_layer]
    pspecs = [pl.BlockSpec((a.shape[0],) + blk[1:], imap) for a in flat]
    return _call(body, name, ((c // 256) if by_cols else (r // blk[1]),), [spec, spec, spec] + pspecs,
                 [spec] * 4, [_sds((lyr, r, c), F32)] * 4, (w, m, v, *flat), xch=xch, gat=gat)


SMALL_ROWS = 104
REPLICATED = {"norm_mix": (0, 4, D), "norm_ffn": (4, 4, D), "kv_norm": (8, 1, D), "k_norm": (9, 1, HD), "q_norm": (10, 2, HD),
              "attn_sinks": (12, 2, NH)}
LOSS_ROW = 14
SHARDED = {"meta_tokens": (16, NMETA), "conv_b_in": (32, 4), "conv_dw": (36, 2 * CW), "conv_ln_g": (98, 2), "conv_ln_b": (100, 2),
           "conv_b_out": (102, 2)}


def pack_small(gs, loss_blk):
    order = ([f"norm_mix{l}" for l in range(4)] + [f"norm_ffn{l}" for l in range(4)] + ["kv_norm", "k_norm", "q_norm0", "q_norm1",
             "attn_sinks0", "attn_sinks1", "meta_tokens", "conv_b_in0", "conv_b_in1", "conv_dw0", "conv_dw1", "conv_ln_g0",
             "conv_ln_g1", "conv_ln_b0", "conv_ln_b1", "conv_b_out0", "conv_b_out1"])

    def body(*refs):
        r = dict(zip(order, refs))
        loss_ref, o_ref = refs[len(order)], refs[len(order) + 1]
        o_ref[...] = jnp.zeros_like(o_ref)
        for l in range(4):
            o_ref[l:l + 1, :] = r[f"norm_mix{l}"][...]
            o_ref[4 + l:5 + l, :] = r[f"norm_ffn{l}"][...]
        o_ref[8:9, :] = r["kv_norm"][...]
        o_ref[9:10, 0:HD] = r["k_norm"][...]
        for j in range(2):
            o_ref[10 + j:11 + j, 0:HD] = r[f"q_norm{j}"][...]
            o_ref[12 + j:13 + j, 0:NH] = r[f"attn_sinks{j}"][...]
            o_ref[32 + 2 * j:33 + 2 * j, :] = r[f"conv_b_in{j}"][:, 0:D]
            o_ref[33 + 2 * j:34 + 2 * j, :] = r[f"conv_b_in{j}"][:, D:2 * D]
            o_ref[36 + CW * j:36 + CW * (j + 1), :] = r[f"conv_dw{j}"][0:CW, :]
            o_ref[98 + j:99 + j, :] = r[f"conv_ln_g{j}"][...]
            o_ref[100 + j:101 + j, :] = r[f"conv_ln_b{j}"][...]
            o_ref[102 + j:103 + j, :] = r[f"conv_b_out{j}"][...]
        o_ref[LOSS_ROW:LOSS_ROW + 1, 0:1] = loss_ref[0:1, 0:1]
        o_ref[16:16 + NMETA, :] = r["meta_tokens"][...]

    return pl.pallas_call(body, name="pack_small", out_shape=_sds((SMALL_ROWS, D), F32))(*[gs[k] for k in order], loss_blk)


def adamw_small(g8, wts, mom, var):
    names = list(REPLICATED) + list(SHARDED)
    shape2 = {"kv_norm": (1, D), "k_norm": (1, HD)}
    ins = [a[k].reshape(shape2.get(k, a[k].shape)) for a in (wts, mom, var) for k in names]
    n = len(names)

    def body(*refs):
        g8_ref, w_refs, m_refs, v_refs = refs[0], refs[1:1 + n], refs[1 + n:1 + 2 * n], refs[1 + 2 * n:1 + 3 * n]
        loss_ref, outs, red_ref = refs[1 + 3 * n], refs[2 + 3 * n:-1], refs[-1]
        me = _my_index()
        acc = g8_ref[0]
        for q in range(1, NDEV):
            acc = acc + g8_ref[q]
        red_ref[...] = acc
        loss_ref[...] = red_ref[LOSS_ROW:LOSS_ROW + 1, 0:1]

        def mine(rows, width):
            acc = jnp.zeros((rows.stop - rows.start, width), F32)
            for p_ in range(NDEV):
                acc = acc + jnp.where(me == p_, red_ref[rows, p_ * width:(p_ + 1) * width], 0.0)
            return acc

        for i, k in enumerate(names):
            if k in REPLICATED:
                r0, nr, width = REPLICATED[k]
                g = red_ref[r0:r0 + nr, 0:width]
            elif k == "conv_b_in":
                half = D // (2 * D // NDEV)
                acc = jnp.zeros((2, 2 * D // NDEV), F32)
                for p_ in range(NDEV):
                    c0 = (p_ % half) * (2 * D // NDEV)
                    part = jnp.concatenate([red_ref[32 + 2 * j + p_ // half:33 + 2 * j + p_ // half, c0:c0 + 2 * D // NDEV]
                                            for j in range(2)], axis=0)
                    acc = acc + jnp.where(me == p_, part, 0.0)
                g = acc
            else:
                r0, nr = SHARDED[k]
                g = mine(slice(r0, r0 + nr), D // NDEV)
            w_, m_, v_ = w_refs[i], m_refs[i], v_refs[i]
            g_out, d_out, m_out, v_out = outs[4 * i:4 * i + 4]
            if k == "conv_dw":
                for j in range(2):
                    gj = g[CW * j:CW * (j + 1)]
                    g_out[j] = gj
                    d_out[j], m_out[j], v_out[j] = _adamw_math(w_[j], m_[j], v_[j], gj)
            else:
                g_out[...] = g
                d_out[...], m_out[...], v_out[...] = _adamw_math(w_[...], m_[...], v_[...], g)

    out_shape = [_sds((1, 1), F32)] + [_sds(ins[i].shape, F32) for i in range(n) for _ in range(4)]
    res = pl.pallas_call(body, name="adamw_small", out_shape=out_shape, scratch_shapes=[pltpu.VMEM((SMALL_ROWS, D), F32)])(g8, *ins)
    out = {k: tuple(o.reshape(wts[k].shape) for o in res[1 + 4 * i:5 + 4 * i]) for i, k in enumerate(names)}
    return res[0], out


NAMES = ["meta_tokens", "norm_mix", "norm_ffn", "conv_w_in", "conv_b_in", "conv_dw", "conv_ln_g", "conv_ln_b", "conv_w_out",
         "conv_b_out", "kv_norm", "w_kv", "k_norm", "w_q", "q_norm", "attn_sinks", "w_o", "ffn_w_gate", "ffn_w_up", "ffn_w_down"]


def kernel(x, meta_tokens, norm_mix, norm_ffn, conv_w_in, conv_b_in, conv_dw, conv_ln_g, conv_ln_b, conv_w_out, conv_b_out, kv_norm, w_kv, k_norm, w_q, q_norm, attn_sinks, w_o, ffn_w_gate, ffn_w_up, ffn_w_down, loss_target, m_meta_tokens, m_norm_mix, m_norm_ffn, m_conv_w_in, m_conv_b_in, m_conv_dw, m_conv_ln_g, m_conv_ln_b, m_conv_w_out, m_conv_b_out, m_kv_norm, m_w_kv, m_k_norm, m_w_q, m_q_norm, m_attn_sinks, m_w_o, m_ffn_w_gate, m_ffn_w_up, m_ffn_w_down, v_meta_tokens, v_norm_mix, v_norm_ffn, v_conv_w_in, v_conv_b_in, v_conv_dw, v_conv_ln_g, v_conv_ln_b, v_conv_w_out, v_conv_b_out, v_kv_norm, v_w_kv, v_k_norm, v_w_q, v_q_norm, v_attn_sinks, v_w_o, v_ffn_w_gate, v_ffn_w_up, v_ffn_w_down):
    wts = dict(zip(NAMES, (meta_tokens, norm_mix, norm_ffn, conv_w_in, conv_b_in, conv_dw, conv_ln_g, conv_ln_b, conv_w_out,
                           conv_b_out, kv_norm, w_kv, k_norm, w_q, q_norm, attn_sinks, w_o, ffn_w_gate, ffn_w_up, ffn_w_down)))
    mom = dict(zip(NAMES, (m_meta_tokens, m_norm_mix, m_norm_ffn, m_conv_w_in, m_conv_b_in, m_conv_dw, m_conv_ln_g, m_conv_ln_b,
                           m_conv_w_out, m_conv_b_out, m_kv_norm, m_w_kv, m_k_norm, m_w_q, m_q_norm, m_attn_sinks, m_w_o,
                           m_ffn_w_gate, m_ffn_w_up, m_ffn_w_down)))
    var = dict(zip(NAMES, (v_meta_tokens, v_norm_mix, v_norm_ffn, v_conv_w_in, v_conv_b_in, v_conv_dw, v_conv_ln_g, v_conv_ln_b,
                           v_conv_w_out, v_conv_b_out, v_kv_norm, v_w_kv, v_k_norm, v_w_q, v_q_norm, v_attn_sinks, v_w_o,
                           v_ffn_w_gate, v_ffn_w_up, v_ffn_w_down)))
    for k in TRANSPOSED:
        wts[k], mom[k], var[k] = (jnp.swapaxes(a, 1, 2) for a in (wts[k], mom[k], var[k]))

    big_names = list(BIG)
    layers = cast_bf16([wts[k] for k in big_names])
    shards = {(k, l): blk for k, per_layer in zip(big_names, layers) for l, blk in enumerate(per_layer)}
    first = [("conv_w_in", 0), ("conv_w_out", 0)]
    vec_names = ["meta_tokens", "conv_b_in", "conv_dw", "conv_ln_g", "conv_ln_b", "conv_b_out"]
    gathered = all_gather_blocks([shards[key] for key in first] + [wts[k] for k in vec_names])
    mats = {key: gathered_matrix(*key, b8) for key, b8 in zip(first, gathered)}
    full = dict(zip(vec_names, gathered[len(first):]))
    join_vec = lambda a: jnp.moveaxis(a, 0, -2).reshape(a.shape[1:-1] + (NDEV * a.shape[-1],))
    w = {}
    w["conv_b_in"] = join_vec(full["conv_b_in"]).reshape(2, 1, 2 * D)
    w["conv_dw"] = join_vec(full["conv_dw"])
    for k in ("conv_ln_g", "conv_ln_b", "conv_b_out"):
        w[k] = join_vec(full[k]).reshape(2, 1, D)
    w["norm_mix"] = norm_mix.reshape(4, 1, D)
    w["norm_ffn"] = norm_ffn.reshape(4, 1, D)
    w["kv_norm"] = kv_norm.reshape(1, D)
    w["k_norm"] = k_norm.reshape(1, HD)
    w["q_norm"] = q_norm.reshape(2, 1, HD)
    w["attn_sinks"] = attn_sinks.reshape(2, 1, NH)

    loss_blk, grad_x, gbig, arrived, gs = local_step(x, loss_target, full["meta_tokens"], w, mats, shards)

    packed = pack_small(gs, loss_blk)

    grads, delta, new_m, new_v = {}, {}, {}, {}
    tail = EXCHANGE_PLAN["tail"]
    waiting = {nm.rstrip("0123456789") for nm, _ in tail}
    order = sorted([k for k in big_names if k not in waiting], key=lambda k: -wts[k].size) + [k for k in big_names if k in waiting]
    small8 = None
    for pos, k in enumerate(order):
        flat2 = wts[k].ndim == 2
        as3 = (lambda a: a[None]) if flat2 else (lambda a: a)
        riders = tail if pos == 0 else []
        gat = [packed] if pos == 1 else []
        parts = [arrived[k]] if flat2 else [arrived[f"{k}{i}"] for i in range(wts[k].shape[0])]
        outs, got_x, got_g = adamw_big(as3(wts[k]), as3(mom[k]), as3(var[k]), parts, "adamw_" + k,
                                       xch=[(gbig[nm], EXCHANGE_KIND[nm.rstrip("0123456789")], ks) for nm, ks in riders], gat=gat)
        for (nm, _), got in zip(riders, got_x):
            arrived[nm].append(got)
        if gat:
            small8 = got_g[0]
        grads[k], delta[k], new_m[k], new_v[k] = [o[0] if flat2 else (jnp.swapaxes(o, 1, 2) if k in TRANSPOSED else o) for o in outs]
    loss, small = adamw_small(small8, wts, mom, var)
    for k, (g_, d_, m_, v_) in small.items():
        grads[k], delta[k], new_m[k], new_v[k] = g_, d_, m_, v_
    return (loss.reshape(()), grad_x, *[grads[k] for k in NAMES], *[delta[k] for k in NAMES], *[new_m[k] for k in NAMES],
            *[new_v[k] for k in NAMES])
```

```python
import functools

import jax
import jax.numpy as jnp
from jax import lax
from jax.experimental import pallas as pl
from jax.experimental.pallas import tpu as pltpu

F32 = jnp.float32
BF = jnp.bfloat16

D = 1024
DFF = 2816
NH = 16
NKV = 4
HD = 64
KVD = NKV * HD
NMETA = 16
CW = 31
HALO = 32
CHUNK = 32
QB = 128
EPS = 1e-6
NEG = -1e30
NDEV = 8
SCALE = HD ** -0.5

LR, B1, B2, AEPS, WD, STEP = 0.001, 0.9, 0.999, 1e-08, 0.01, 10

VMEM_LIMIT = 56 * 2 ** 20
MESH = pl.DeviceIdType.MESH


def _cp(n):
    return pltpu.CompilerParams(dimension_semantics=("arbitrary",) * n, vmem_limit_bytes=VMEM_LIMIT)


def _row(tm, c):
    return pl.BlockSpec((tm, c), lambda i: (i, 0))


def _res(shape):
    return pl.BlockSpec(shape, lambda i: (0,) * len(shape), pipeline_mode=pl.Buffered(1))


def _lay(l, shape):
    return pl.BlockSpec((None,) + tuple(shape), lambda i: (l,) + (0,) * len(shape), pipeline_mode=pl.Buffered(1))


def _acc(shape):
    return pl.BlockSpec(shape, lambda i: (0,) * len(shape))


def _sds(shape, dt):
    return jax.ShapeDtypeStruct(tuple(shape), dt)


def _dot(a, b):
    return jnp.dot(a.astype(BF), b.astype(BF), preferred_element_type=F32)


def _dot_nt(a, b):
    return lax.dot_general(a.astype(BF), b.astype(BF), (((1,), (1,)), ((), ())), preferred_element_type=F32)


def _dot_tn(a, b):
    return lax.dot_general(a.astype(BF), b.astype(BF), (((0,), (0,)), ((), ())), preferred_element_type=F32)


def _rstd(x):
    return lax.rsqrt(jnp.mean(x * x, axis=-1, keepdims=True) + EPS)


def _rms_bwd(x, g, dy):
    r = _rstd(x)
    z = dy * g
    dx = r * z - x * (r * r * r * jnp.mean(z * x, axis=-1, keepdims=True))
    return dx, jnp.sum(dy * x * r, axis=0, keepdims=True)


def _sig(x):
    return jax.nn.sigmoid(x)


def _fold8(x):
    out = x[0:8]
    for k in range(1, x.shape[0] // 8):
        out = out + x[8 * k:8 * k + 8]
    return out


def _shifted(win):
    return [win] + [pltpu.roll(win, 2 * CHUNK - rho, 0) for rho in range(1, 8)]


def _tap(phases, o):
    return phases[o % 8][8 * (o // 8):8 * (o // 8) + CHUNK]


def _init(ref, first):
    @pl.when(first)
    def _():
        ref[...] = jnp.zeros_like(ref)


def _my_index():
    return 4 * lax.axis_index("x") + 2 * lax.axis_index("y") + lax.axis_index("c")


def _coords(idx):
    return (idx // 4, (idx // 2) % 2, idx % 2)


ALL = tuple(range(NDEV))
H1, H2 = (0, 1, 2, 4, 6), (3, 5, 7)


def _xch_shapes(xch):
    return [_sds((len(ks),) + ((a.shape[0] // NDEV, a.shape[1]) if k == "rows" else tuple(a.shape[1:])), a.dtype) for a, k, ks in xch]


def _xch_scratch(n):
    return [pltpu.SemaphoreType.DMA((n, NDEV)), pltpu.SemaphoreType.DMA((n, NDEV)), pltpu.SemaphoreType.DMA((n,))]


def _xch_copies(meta, srcs, outs, sems, arrivals):
    send_sems, recv_sems, local_sems = sems
    me = _my_index()

    def piece(a, p):
        if meta[a][0] == "rows":
            r = srcs[a].shape[0] // NDEV
            return srcs[a].at[pl.ds(p * r, r), :]
        return srcs[a].at[p]

    def remote(a, i, k, src):
        return pltpu.make_async_remote_copy(
            src_ref=src, dst_ref=outs[a].at[i], send_sem=send_sems.at[a, k], recv_sem=recv_sems.at[a, k],
            device_id=_coords(me ^ k), device_id_type=MESH)

    local, sends, recvs = [], [], []
    for a, (_, ks) in enumerate(meta):
        for i, k in enumerate(ks):
            if k == 0:
                local.append(pltpu.make_async_copy(piece(a, me), outs[a].at[i], local_sems.at[a]))
            else:
                sends.append(remote(a, i, k, piece(a, me ^ k)))
                if arrivals:
                    recvs.append(remote(a, i, k, piece(a, me)))
    return local, sends, recvs


def _xch_start(meta, srcs, outs, sems):
    local, sends, _ = _xch_copies(meta, srcs, outs, sems, False)
    for cp in local + sends:
        cp.start()


def _xch_wait(meta, srcs, outs, sems):
    local, sends, recvs = _xch_copies(meta, srcs, outs, sems, True)
    for cp in recvs:
        cp.wait_recv()
    for cp in sends:
        cp.wait_send()
    for cp in local:
        cp.wait()


def _gat_copies(srcs, outs, sems):
    send_sems, recv_sems, local_sems = sems
    x, y, c = lax.axis_index("x"), lax.axis_index("y"), lax.axis_index("c")
    me, sibling = (x, y, c), (x, y, 1 - c)
    chips = [(1 - x, y), (x, 1 - y), (1 - x, 1 - y)]

    def copy(a, k, owner, to, from_block=False):
        slot = outs[a].at[4 * owner[0] + 2 * owner[1] + owner[2]]
        return pltpu.make_async_remote_copy(
            src_ref=srcs[a] if from_block else slot, dst_ref=slot, send_sem=send_sems.at[a, k], recv_sem=recv_sems.at[a, k],
            device_id=to, device_id_type=MESH)

    n = len(srcs)
    local = lambda: [pltpu.make_async_copy(srcs[a], outs[a].at[4 * x + 2 * y + c], local_sems.at[a]) for a in range(n)]
    first = lambda: [cp for a in range(n) for cp in
                     [copy(a, 0, me, sibling, True)] + [copy(a, 1 + j, me, (*chip, c), True) for j, chip in enumerate(chips)]]
    landed = lambda: [copy(a, 1 + j, (*chip, c), me) for a in range(n) for j, chip in enumerate(chips)]
    passed = lambda: [copy(a, 4 + j, (*chip, c), sibling) for a in range(n) for j, chip in enumerate(chips)]
    final = lambda: [cp for a in range(n) for cp in
                     [copy(a, 0, sibling, me)] + [copy(a, 4 + j, (*chip, 1 - c), me) for j, chip in enumerate(chips)]]
    return local, first, landed, passed, final


def _gat_start(srcs, outs, sems):
    local, first, _, _, _ = _gat_copies(srcs, outs, sems)
    for cp in local() + first():
        cp.start()


def _gat_forward(srcs, outs, sems):
    _, _, landed, passed, _ = _gat_copies(srcs, outs, sems)
    for got, on in zip(landed(), passed()):
        got.wait_recv()
        on.start()


def _gat_wait(srcs, outs, sems):
    local, first, _, passed, final = _gat_copies(srcs, outs, sems)
    for cp in final():
        cp.wait_recv()
    for cp in first() + passed():
        cp.wait_send()
    for cp in local():
        cp.wait()


def _call(body, name, grid, in_specs, out_specs, out_shape, args, scratch=(), xch=(), gat=()):
    n_in, n_out, n_x, n_g, n_s = len(in_specs), len(out_specs), len(xch), len(gat), len(scratch)
    kinds = [(k, ks) for _, k, ks in xch]
    total = 1
    for g in grid:
        total *= g

    def wrapped(*refs):
        ins, refs = refs[:n_in], refs[n_in:]
        x_src, refs = refs[:n_x], refs[n_x:]
        g_src, refs = refs[:n_g], refs[n_g:]
        outs, refs = refs[:n_out], refs[n_out:]
        x_out, refs = refs[:n_x], refs[n_x:]
        g_out, refs = refs[:n_g], refs[n_g:]
        own, refs = refs[:n_s], refs[n_s:]
        x_sems, g_sems = (refs[:3], refs[3:]) if n_x else ((), refs)
        step = pl.program_id(0)
        for d in range(1, len(grid)):
            step = step * grid[d] + pl.program_id(d)
        if n_x or n_g:
            @pl.when(step == 0)
            def _():
                if n_x:
                    _xch_start(kinds, x_src, x_out, x_sems)
                if n_g:
                    _gat_start(g_src, g_out, g_sems)

        body(*ins, *outs, *own)
        if n_g:
            @pl.when(step == max(total - 2, 0))
            def _():
                _gat_forward(g_src, g_out, g_sems)

        if n_x or n_g:
            @pl.when(step == total - 1)
            def _():
                if n_x:
                    _xch_wait(kinds, x_src, x_out, x_sems)
                if n_g:
                    _gat_wait(g_src, g_out, g_sems)

    any_spec = pl.BlockSpec(memory_space=pl.ANY)
    g_shapes = [_sds((NDEV,) + tuple(a.shape), a.dtype) for a in gat]
    res = pl.pallas_call(
        wrapped, name=name, grid=grid, in_specs=list(in_specs) + [any_spec] * (n_x + n_g),
        out_specs=list(out_specs) + [any_spec] * (n_x + n_g), out_shape=list(out_shape) + _xch_shapes(xch) + g_shapes,
        scratch_shapes=list(scratch) + (_xch_scratch(n_x) if n_x else []) + (_xch_scratch(n_g) if n_g else []),
        compiler_params=_cp(len(grid)),
    )(*args, *[a for a, _, _ in xch], *gat)
    return res[:n_out], res[n_out:n_out + n_x], res[n_out + n_x:]


def embed(x, meta8, lp):
    bl, seq, _ = x.shape
    c8 = D // NDEV
    cb = 2 * c8

    def body(x_ref, m_ref, h_ref):
        h_ref[0:NMETA, :] = jnp.concatenate([m_ref[0], m_ref[1]], axis=1)
        h_ref[NMETA:NMETA + seq, :] = x_ref[...]
        h_ref[NMETA + seq:, :] = jnp.zeros((lp - NMETA - seq, cb), F32)

    return pl.pallas_call(
        body, name="embed", grid=(bl, D // cb),
        in_specs=[pl.BlockSpec((None, seq, cb), lambda b, c: (b, 0, c)), pl.BlockSpec((2, NMETA, c8), lambda b, c: (c, 0, 0))],
        out_specs=pl.BlockSpec((None, lp, cb), lambda b, c: (b, 0, c)), out_shape=_sds((bl, lp, D), F32),
        compiler_params=_cp(2),
    )(x, meta8)


def conv_in_fwd(h, nm, l, w_in, b_in, i, tm):
    t = h.shape[0]

    def body(h_ref, g_ref, w_ref, b_ref, u_ref, big_ref, a_ref):
        x = h_ref[...]
        ub = (x * _rstd(x) * g_ref[...]).astype(BF)
        u_ref[...] = ub
        big = jnp.dot(ub, w_ref[...], preferred_element_type=F32) + b_ref[...]
        big_ref[...] = big.astype(BF)
        a_ref[...] = big[:, :D] * _sig(big[:, D:])

    return pl.pallas_call(
        body, name=f"conv_in_fwd{i}", grid=(t // tm,),
        in_specs=[_row(tm, D), _lay(l, (1, D)), _res((D, 2 * D)), _lay(i, (1, 2 * D))],
        out_specs=[_row(tm, D), _row(tm, 2 * D), _row(tm, D)],
        out_shape=[_sds((t, D), BF), _sds((t, 2 * D), BF), _sds((t, D), F32)],
        compiler_params=_cp(1),
    )(h, nm, w_in, b_in)


def _prev_halo(tm):
    return pl.BlockSpec((HALO, D), lambda i: (jnp.maximum(i * (tm // HALO) - 1, 0), 0))


def _next_halo(tm, t):
    return pl.BlockSpec((HALO, D), lambda i: (jnp.minimum((i + 1) * (tm // HALO), t // HALO - 1), 0))


def conv_mid_fwd(a, dw, ln_g, ln_b, i, tm, tpb, gat):
    t = a.shape[0]

    def body(a_ref, halo_ref, dw_ref, g_ref, b_ref, c_ref, s_ref, ext):
        first = pl.program_id(0) % tpb == 0
        ext[0:HALO] = jnp.where(first, 0.0, halo_ref[...])
        ext[HALO:] = a_ref[...]

        def chunk(k, carry):
            r0 = pl.multiple_of(k * CHUNK, CHUNK)
            win = _shifted(ext[pl.ds(r0, 2 * CHUNK), :])
            c = jnp.zeros((CHUNK, D), F32)
            for j in range(CW):
                c = c + dw_ref[j:j + 1, :] * _tap(win, j + 2)
            c_ref[pl.ds(r0, CHUNK), :] = c
            mu = jnp.mean(c, axis=-1, keepdims=True)
            xc = c - mu
            n = xc * lax.rsqrt(jnp.mean(xc * xc, axis=-1, keepdims=True) + EPS) * g_ref[...] + b_ref[...]
            s_ref[pl.ds(r0, CHUNK), :] = (n * _sig(n)).astype(BF)
            return carry

        lax.fori_loop(0, tm // CHUNK, chunk, 0)

    return _call(
        body, f"conv_mid_fwd{i}", (t // tm,),
        [_row(tm, D), _prev_halo(tm), _lay(i, (CW, D)), _lay(i, (1, D)), _lay(i, (1, D))],
        [_row(tm, D), _row(tm, D)], [_sds((t, D), F32), _sds((t, D), BF)], (a, a, dw, ln_g, ln_b),
        scratch=[pltpu.VMEM((tm + HALO, D), F32)], gat=gat)


def mixer_ffn_fwd(h, s, w_out, lw, bias, nf, l, wg, wu, wd, tm, gat):
    t = h.shape[0]

    def body(*refs):
        if bias is None:
            h_ref, s_ref, w_ref, nf_ref, wg_ref, wu_ref, wd_ref, h1_ref, u_ref, g_ref, up_ref, hid_ref, h2_ref = refs
            y = 0.0
        else:
            h_ref, s_ref, w_ref, b_ref, nf_ref, wg_ref, wu_ref, wd_ref, h1_ref, u_ref, g_ref, up_ref, hid_ref, h2_ref = refs
            y = b_ref[...]
        h1 = h_ref[...] + (jnp.dot(s_ref[...], w_ref[...], preferred_element_type=F32) + y)
        h1_ref[...] = h1
        ub = (h1 * _rstd(h1) * nf_ref[...]).astype(BF)
        u_ref[...] = ub
        g = _dot_nt(ub, wg_ref[...])
        up = _dot_nt(ub, wu_ref[...])
        g_ref[...] = g.astype(BF)
        up_ref[...] = up.astype(BF)
        hid = (g * _sig(g) * up).astype(BF)
        hid_ref[...] = hid
        h2_ref[...] = h1 + jnp.dot(hid, wd_ref[...], preferred_element_type=F32)

    ins = [h, s, w_out] + ([] if bias is None else [bias]) + [nf, wg, wu, wd]
    specs = ([_row(tm, D), _row(tm, D), _res((D, D))] + ([] if bias is None else [_lay(lw, (1, D))])
             + [_lay(l, (1, D)), _res((DFF, D)), _res((DFF, D)), _res((DFF, D))])
    return _call(
        body, f"mixer_ffn_fwd{l}", (t // tm,), specs,
        [_row(tm, D), _row(tm, D), _row(tm, DFF), _row(tm, DFF), _row(tm, DFF), _row(tm, D)],
        [_sds((t, D), F32), _sds((t, D), BF), _sds((t, DFF), BF), _sds((t, DFF), BF), _sds((t, DFF), BF), _sds((t, D), F32)],
        ins, gat=gat)


def _seg_rms(x, g, nseg):
    outs = []
    for s in range(nseg):
        xs = x[:, HD * s:HD * s + HD]
        outs.append(xs * _rstd(xs) * g)
    return jnp.concatenate(outs, axis=1)


def kv_fwd(h, kvn, w_kv, kng, tm):
    t = h.shape[0]

    def body(h_ref, g_ref, w_ref, kg_ref, kn_ref, kv_ref, k_ref, v_ref):
        x = h_ref[...]
        kn = (x * _rstd(x) * g_ref[...]).astype(BF)
        kn_ref[...] = kn
        kv = jnp.dot(kn, w_ref[...], preferred_element_type=F32)
        kv_ref[...] = kv
        k_ref[...] = _seg_rms(kv[:, :KVD], kg_ref[...], NKV).astype(BF)
        v_ref[...] = kv[:, KVD:].astype(BF)

    return pl.pallas_call(
        body, name="kv_fwd", grid=(t // tm,),
        in_specs=[_row(tm, D), _res((1, D)), _res((D, 2 * KVD)), _res((1, HD))],
        out_specs=[_row(tm, D), _row(tm, 2 * KVD), _row(tm, KVD), _row(tm, KVD)],
        out_shape=[_sds((t, D), BF), _sds((t, 2 * KVD), F32), _sds((t, KVD), BF), _sds((t, KVD), BF)],
        compiler_params=_cp(1),
    )(h, kvn, w_kv, kng)


def q_fwd(h, nm, l, w_q, j, tm):
    t = h.shape[0]

    def body(h_ref, g_ref, w_ref, u_ref, q_ref):
        x = h_ref[...]
        ub = (x * _rstd(x) * g_ref[...]).astype(BF)
        u_ref[...] = ub
        q_ref[...] = jnp.dot(ub, w_ref[...], preferred_element_type=F32)

    return pl.pallas_call(
        body, name=f"q_fwd{j}", grid=(t // tm,),
        in_specs=[_row(tm, D), _lay(l, (1, D)), _res((D, D))],
        out_specs=[_row(tm, D), _row(tm, D)], out_shape=[_sds((t, D), BF), _sds((t, D), F32)],
        compiler_params=_cp(1),
    )(h, nm, w_q)


RQ = NH // NKV


def _attn_specs(nb, lp):
    cur = lambda c: pl.BlockSpec((QB, c), lambda b, n: (b * nb + n, 0))
    seq = pl.BlockSpec((None, lp, KVD), lambda b, n: (b, 0, 0))
    seq_t = pl.BlockSpec((None, KVD, lp), lambda b, n: (b, 0, 0))
    return cur, seq, seq_t


NKEYS = 2 * QB + NMETA


def _attn_mask(n, start):
    shape = (RQ * QB, NKEYS)
    qpos = n * QB + (lax.broadcasted_iota(jnp.int32, shape, 0) & (QB - 1))
    col = lax.broadcasted_iota(jnp.int32, shape, 1)
    in_band = col < 2 * QB
    kpos = jnp.where(in_band, start + col, col - 2 * QB)
    return (kpos <= qpos) & ((col >= 2 * QB) | ((qpos - kpos < QB) & (kpos >= NMETA)))


def _keys(ref, band, gs):
    return jnp.concatenate([ref[band, gs], ref[0:NMETA, gs]], axis=0)


def _keys_t(ref, band, gs):
    return jnp.concatenate([ref[gs, band], ref[gs, 0:NMETA]], axis=1)


def transpose_seq(a, name):
    bl, r, c = a.shape

    def body(a_ref, o_ref):
        o_ref[...] = a_ref[...].T

    return pl.pallas_call(
        body, name=name, grid=(bl,), in_specs=[pl.BlockSpec((None, r, c), lambda b: (b, 0, 0))],
        out_specs=pl.BlockSpec((None, c, r), lambda b: (b, 0, 0)), out_shape=_sds((bl, c, r), a.dtype), compiler_params=_cp(1),
    )(a)


def sum_transposed(a0, a1):
    bl, c, r = a0.shape

    def body(a0_ref, a1_ref, o_ref):
        o_ref[...] = (a0_ref[...] + a1_ref[...]).T

    spec = pl.BlockSpec((None, c, r), lambda b: (b, 0, 0))
    return pl.pallas_call(
        body, name="sum_transposed", grid=(bl,), in_specs=[spec, spec],
        out_specs=pl.BlockSpec((r, c), lambda b: (b, 0)), out_shape=_sds((bl * r, c), a0.dtype), compiler_params=_cp(1),
    )(a0, a1)


def _stack_heads(ref, g, fn):
    return jnp.concatenate([fn(ref[:, HD * (g * RQ + r):HD * (g * RQ + r) + HD]) for r in range(RQ)], axis=0)


def _stack_cols(ref, g):
    return jnp.concatenate([ref[:, g * RQ + r:g * RQ + r + 1] for r in range(RQ)], axis=0)


def _stack_sinks(sk_ref, g):
    return jnp.concatenate([jnp.broadcast_to(sk_ref[:, g * RQ + r:g * RQ + r + 1], (QB, 1)) for r in range(RQ)], axis=0)


def attn_fwd(q, kt, v, qg, sinks, j, bl, lp, gat):
    t = q.shape[0]
    nb = lp // QB
    cur, seq, seq_t = _attn_specs(nb, lp)

    def body(q_ref, kt_ref, v_ref, qg_ref, sk_ref, o_ref, lse_ref):
        n = pl.program_id(1)
        start = pl.multiple_of(jnp.maximum(n - 1, 0) * QB, QB)
        mask = _attn_mask(n, start)
        band = pl.ds(start, 2 * QB)
        lane = lax.broadcasted_iota(jnp.int32, (QB, NH), 1)
        ones = jnp.ones((NKEYS, HD), BF)
        lse = jnp.zeros((QB, NH), F32)
        for g in range(NKV):
            gs = slice(HD * g, HD * g + HD)
            qn = _stack_heads(q_ref, g, lambda x: (x * _rstd(x) * (qg_ref[...] * SCALE)).astype(BF))
            sink = _stack_sinks(sk_ref, g)
            s = jnp.where(mask, _dot(qn, _keys_t(kt_ref, band, gs)), NEG)
            mx = jnp.maximum(jnp.max(s, -1, keepdims=True), sink)
            oa = _dot(jnp.exp(s - mx), jnp.concatenate([_keys(v_ref, band, gs), ones], axis=1))
            den = oa[:, HD:HD + 1] + jnp.exp(sink - mx)
            o = oa[:, :HD] * (1.0 / den)
            l = mx + jnp.log(den)
            for r in range(RQ):
                h = g * RQ + r
                o_ref[:, HD * h:HD * h + HD] = o[r * QB:(r + 1) * QB].astype(BF)
                lse = jnp.where(lane == h, l[r * QB:(r + 1) * QB], lse)
        lse_ref[...] = lse

    return _call(
        body, f"attn_fwd{j}", (bl, nb),
        [cur(D), seq_t, seq, pl.BlockSpec((None, 1, HD), lambda b, n: (j, 0, 0)), pl.BlockSpec((None, 1, NH), lambda b, n: (j, 0, 0))],
        [cur(D), cur(NH)], [_sds((t, D), BF), _sds((t, NH), F32)], (q, kt, v, qg, sinks), gat=gat)


def loss_fwd(h, tgt):
    bl, lp, _ = h.shape
    seq = tgt.shape[1]
    cb = 256

    def body(h_ref, t_ref, dh_ref, loss_ref):
        _init(loss_ref, (pl.program_id(0) == 0) & (pl.program_id(1) == 0))
        err = h_ref[NMETA:NMETA + seq, :] - t_ref[...]
        dh_ref[...] = jnp.zeros_like(dh_ref)
        dh_ref[NMETA:NMETA + seq, :] = err * (1.0 / D)
        loss_ref[...] += (0.5 / D) * jnp.sum(err * err)

    return pl.pallas_call(
        body, name="loss_fwd", grid=(bl, D // cb),
        in_specs=[pl.BlockSpec((None, lp, cb), lambda b, c: (b, 0, c)), pl.BlockSpec((None, seq, cb), lambda b, c: (b, 0, c))],
        out_specs=[pl.BlockSpec((None, lp, cb), lambda b, c: (b, 0, c)), pl.BlockSpec((8, 128), lambda b, c: (0, 0))],
        out_shape=[_sds((bl, lp, D), F32), _sds((8, 128), F32)],
        compiler_params=_cp(2),
    )(h, tgt)


def ffn_bwd_x(dh2, g, up, h1, nf, l, wd, wg, wu, w_o, tm, xch):
    t = dh2.shape[0]

    def body(dh2_ref, g_ref, up_ref, h1_ref, nf_ref, wd_ref, wg_ref, wu_ref, *rest):
        if w_o is None:
            dg_ref, du_ref, dh1_ref, dnf_ref = rest
        else:
            wo_ref, dg_ref, du_ref, dh1_ref, dnf_ref, do_ref = rest
        _init(dnf_ref, pl.program_id(0) == 0)
        dh2v = dh2_ref[...]
        dhid = _dot_nt(dh2v, wd_ref[...])
        gv = g_ref[...].astype(F32)
        uv = up_ref[...].astype(F32)
        sg = _sig(gv)
        dgv = (dhid * uv * (sg * (1.0 + gv * (1.0 - sg)))).astype(BF)
        duv = (dhid * (gv * sg)).astype(BF)
        dg_ref[...] = dgv
        du_ref[...] = duv
        dnorm = _dot(dgv, wg_ref[...]) + _dot(duv, wu_ref[...])
        dx, dnf = _rms_bwd(h1_ref[...], nf_ref[...], dnorm)
        dh1 = dh2v + dx
        dh1_ref[...] = dh1
        dnf_ref[...] += dnf
        if w_o is not None:
            do_ref[...] = _dot_nt(dh1, wo_ref[...]).astype(BF)

    attn = w_o is not None
    return _call(
        body, f"ffn_bwd_x{l}", (t // tm,),
        [_row(tm, D), _row(tm, DFF), _row(tm, DFF), _row(tm, D), _lay(l, (1, D)),
         _res((DFF, D)), _res((DFF, D)), _res((DFF, D))] + ([_res((D, D))] if attn else []),
        [_row(tm, DFF), _row(tm, DFF), _row(tm, D), _acc((1, D))] + ([_row(tm, D)] if attn else []),
        [_sds((t, DFF), BF), _sds((t, DFF), BF), _sds((t, D), F32), _sds((1, D), F32)] + ([_sds((t, D), BF)] if attn else []),
        (dh2, g, up, h1, nf, wd, wg, wu) + ((w_o,) if attn else ()), xch=xch)


def mm_tn(x, dy, tm, name, split=False, transposed=False, xch=()):
    t, kk = x.shape
    nn = dy.shape[1]
    n8 = nn // NDEV
    nsteps = t // tm

    def body(x_ref, dy_ref, o_ref, acc):
        i = pl.program_id(0)
        _init(acc, i == 0)
        acc[...] += _dot_tn(x_ref[...], dy_ref[...])

        @pl.when(i == nsteps - 1)
        def _():
            if split:
                for p in range(NDEV):
                    o_ref[p] = acc[:, p * n8:(p + 1) * n8].astype(BF)
            elif transposed:
                o_ref[...] = acc[...].T.astype(BF)
            else:
                o_ref[...] = acc[...].astype(BF)

    oshape = (NDEV, kk, n8) if split else ((nn, kk) if transposed else (kk, nn))
    (out,), got, _ = _call(body, name, (nsteps,), [_row(tm, kk), _row(tm, nn)], [_acc(oshape)], [_sds(oshape, BF)], (x, dy),
                           scratch=[pltpu.VMEM((kk, nn), F32)], xch=xch)
    return out, got


def proj_bwd(dy, w, h, g, lg, dh_in, tm, name, xch=()):
    t = h.shape[0]
    nn = dy.shape[1]
    wspec = _res(w.shape)
    gspec = _res((1, D)) if lg is None else _lay(lg, (1, D))

    def body(dy_ref, w_ref, h_ref, g_ref, dhin_ref, dh_ref, dg_ref):
        _init(dg_ref, pl.program_id(0) == 0)
        du = _dot_nt(dy_ref[...], w_ref[...])
        dx, dg = _rms_bwd(h_ref[...], g_ref[...], du)
        dh_ref[...] = dhin_ref[...] + dx
        dg_ref[...] += dg

    return _call(body, name, (t // tm,), [_row(tm, nn), wspec, _row(tm, D), gspec, _row(tm, D)],
                 [_row(tm, D), _acc((1, D))], [_sds((t, D), F32), _sds((1, D), F32)], (dy, w, h, g, dh_in), xch=xch)


def attn_bwd(q, k, kt, vt, do, o, lse, qg, sinks, j, bl, lp, xch):
    t = q.shape[0]
    nb = lp // QB
    cur, seq, seq_t = _attn_specs(nb, lp)

    def body(q_ref, k_ref, kt_ref, vt_ref, do_ref, o_ref, lse_ref, qg_ref, sk_ref, dq_ref, dk_ref, dv_ref, dqg_ref, dsk_ref):
        b, n = pl.program_id(0), pl.program_id(1)
        _init(dk_ref, n == 0)
        _init(dv_ref, n == 0)
        _init(dqg_ref, (b == 0) & (n == 0))
        _init(dsk_ref, (b == 0) & (n == 0))
        start = pl.multiple_of(jnp.maximum(n - 1, 0) * QB, QB)
        mask = _attn_mask(n, start)
        band = pl.ds(start, 2 * QB)
        lane = lax.broadcasted_iota(jnp.int32, (1, NH), 1)
        dqg = jnp.zeros((1, HD), F32)
        dsk = jnp.zeros((1, NH), F32)
        for g in range(NKV):
            gs = slice(HD * g, HD * g + HD)
            kk = _keys(k_ref, band, gs)
            qh = _stack_heads(q_ref, g, lambda x: x)
            rs = _rstd(qh)
            qn = (qh * rs * (qg_ref[...] * SCALE)).astype(BF)
            ls = _stack_cols(lse_ref, g)
            pr = jnp.where(mask, jnp.exp(_dot(qn, _keys_t(kt_ref, band, gs)) - ls), 0.0)
            doh = _stack_heads(do_ref, g, lambda x: x)
            delta = jnp.sum(doh.astype(F32) * _stack_heads(o_ref, g, lambda x: x).astype(F32), axis=-1, keepdims=True)
            ds = (pr * (_dot(doh, _keys_t(vt_ref, band, gs)) - delta)).astype(BF)
            dqn = _dot(ds, kk) * SCALE
            dkt = _dot_tn(qn, ds)
            dvt = _dot_tn(doh, pr)
            dk_ref[gs, band] += dkt[:, :2 * QB]
            dv_ref[gs, band] += dvt[:, :2 * QB]
            dk_ref[gs, 0:NMETA] += dkt[:, 2 * QB:]
            dv_ref[gs, 0:NMETA] += dvt[:, 2 * QB:]
            dsink = jnp.exp(_stack_sinks(sk_ref, g) - ls) * delta
            z = dqn * qg_ref[...]
            dq = rs * z - qh * (rs * rs * rs * jnp.mean(z * qh, axis=-1, keepdims=True))
            dqg = dqg + jnp.sum(dqn * qh * rs, axis=0, keepdims=True)
            for r in range(RQ):
                h = g * RQ + r
                dq_ref[:, HD * h:HD * h + HD] = dq[r * QB:(r + 1) * QB]
                dsk = dsk + jnp.where(lane == h, -jnp.sum(dsink[r * QB:(r + 1) * QB]), 0.0)
        dqg_ref[...] += dqg
        dsk_ref[...] += dsk

    return _call(
        body, f"attn_bwd{j}", (bl, nb),
        [cur(D), seq, seq_t, seq_t, cur(D), cur(D), cur(NH),
         pl.BlockSpec((None, 1, HD), lambda b, n: (j, 0, 0)), pl.BlockSpec((None, 1, NH), lambda b, n: (j, 0, 0))],
        [cur(D), seq_t, seq_t, pl.BlockSpec((1, HD), lambda b, n: (0, 0)), pl.BlockSpec((1, NH), lambda b, n: (0, 0))],
        [_sds((t, D), F32), _sds((bl, KVD, lp), F32), _sds((bl, KVD, lp), F32), _sds((1, HD), F32), _sds((1, NH), F32)],
        (q, k, kt, vt, do, o, lse, qg, sinks), xch=xch)


def kv_bwd_pre(dk, dv, kv, kng, tm):
    t = kv.shape[0]

    def body(dk_ref, dv_ref, kv_ref, g_ref, dkv_ref, dg_ref):
        _init(dg_ref, pl.program_id(0) == 0)
        dg = jnp.zeros((1, HD), F32)
        outs = []
        for s in range(NKV):
            sl = slice(HD * s, HD * s + HD)
            dx, dgs = _rms_bwd(kv_ref[:, sl], g_ref[...], dk_ref[:, sl])
            outs.append(dx)
            dg = dg + dgs
        dkv_ref[:, :KVD] = jnp.concatenate(outs, axis=1).astype(BF)
        dkv_ref[:, KVD:] = dv_ref[...].astype(BF)
        dg_ref[...] += dg

    return pl.pallas_call(
        body, name="kv_bwd_pre", grid=(t // tm,),
        in_specs=[_row(tm, KVD)] * 2 + [_row(tm, 2 * KVD), _res((1, HD))],
        out_specs=[_row(tm, 2 * KVD), _acc((1, HD))], out_shape=[_sds((t, 2 * KVD), BF), _sds((1, HD), F32)],
        compiler_params=_cp(1),
    )(dk, dv, kv, kng)


def conv_out_bwd(dh1, c, ln_g, ln_b, w_out, i, tm, xch):
    t = dh1.shape[0]

    def body(dh1_ref, c_ref, g_ref, b_ref, w_ref, dc_ref, dg_ref, db_ref, dbo_ref):
        first = pl.program_id(0) == 0
        _init(dg_ref, first)
        _init(db_ref, first)
        _init(dbo_ref, first)
        dh1v = dh1_ref[...]
        ds = _dot_nt(dh1v, w_ref[...])
        cv = c_ref[...]
        xc = cv - jnp.mean(cv, axis=-1, keepdims=True)
        rstd = lax.rsqrt(jnp.mean(xc * xc, axis=-1, keepdims=True) + EPS)
        xh = xc * rstd
        n = xh * g_ref[...] + b_ref[...]
        sg = _sig(n)
        dn = ds * (sg * (1.0 + n * (1.0 - sg)))
        dxh = dn * g_ref[...]
        dc_ref[...] = rstd * (dxh - jnp.mean(dxh, axis=-1, keepdims=True) - xh * jnp.mean(dxh * xh, axis=-1, keepdims=True))
        dg_ref[...] += jnp.sum(dn * xh, axis=0, keepdims=True)
        db_ref[...] += jnp.sum(dn, axis=0, keepdims=True)
        dbo_ref[...] += jnp.sum(dh1v, axis=0, keepdims=True)

    return _call(
        body, f"conv_out_bwd{i}", (t // tm,), [_row(tm, D), _row(tm, D), _lay(i, (1, D)), _lay(i, (1, D)), _res((D, D))],
        [_row(tm, D), _acc((1, D)), _acc((1, D)), _acc((1, D))], [_sds((t, D), F32)] + [_sds((1, D), F32)] * 3,
        (dh1, c, ln_g, ln_b, w_out), xch=xch)


def conv_mid_bwd(dc, a, big, dw, i, tm, tpb, xch):
    t = dc.shape[0]
    nsteps = t // tm

    def body(dc_ref, nxt_ref, a_ref, prv_ref, big_ref, dw_ref, da_ref, dbin_ref, ddw_ref, dce, ae, wacc, bacc):
        i_ = pl.program_id(0)
        _init(wacc, i_ == 0)
        _init(bacc, i_ == 0)
        dce[0:tm] = dc_ref[...]
        dce[tm:] = jnp.where(i_ % tpb == tpb - 1, 0.0, nxt_ref[...])
        ae[0:HALO] = jnp.where(i_ % tpb == 0, 0.0, prv_ref[...])
        ae[HALO:] = a_ref[...]

        def chunk(k, carry):
            r0 = pl.multiple_of(k * CHUNK, CHUNK)
            wdc = _shifted(dce[pl.ds(r0, 2 * CHUNK), :])
            wa = _shifted(ae[pl.ds(r0, 2 * CHUNK), :])
            dcc = wdc[0][0:CHUNK]
            da = jnp.zeros((CHUNK, D), F32)
            for j in range(CW):
                da = da + dw_ref[j:j + 1, :] * _tap(wdc, CW - 1 - j)
                wacc[j] += _fold8(dcc * _tap(wa, j + 2))
            bv = big_ref[pl.ds(r0, CHUNK), :].astype(F32)
            a1, sg = bv[:, :D], _sig(bv[:, D:])
            d1 = da * sg
            d2 = da * a1 * sg * (1.0 - sg)
            da_ref[pl.ds(r0, CHUNK), 0:D] = d1.astype(BF)
            da_ref[pl.ds(r0, CHUNK), D:2 * D] = d2.astype(BF)
            bacc[:, 0:D] += _fold8(d1)
            bacc[:, D:2 * D] += _fold8(d2)
            return carry

        lax.fori_loop(0, tm // CHUNK, chunk, 0)

        @pl.when(i_ == nsteps - 1)
        def _():
            dbin_ref[...] = jnp.sum(bacc[...], axis=0, keepdims=True)
            ddw_ref[...] = jnp.sum(wacc[...], axis=1)

    return _call(
        body, f"conv_mid_bwd{i}", (nsteps,),
        [_row(tm, D), _next_halo(tm, t), _row(tm, D), _prev_halo(tm), _row(tm, 2 * D), _lay(i, (CW, D))],
        [_row(tm, 2 * D), _acc((1, 2 * D)), _acc((CW + 1, D))],
        [_sds((t, 2 * D), BF), _sds((1, 2 * D), F32), _sds((CW + 1, D), F32)],
        (dc, dc, a, a, big, dw),
        scratch=[pltpu.VMEM((tm + HALO, D), F32), pltpu.VMEM((tm + HALO, D), F32),
                 pltpu.VMEM((CW + 1, 8, D), F32), pltpu.VMEM((8, 2 * D), F32)], xch=xch)


def input_grads(dh0, seq):
    bl, lp, _ = dh0.shape
    cb = 256

    def body(dh_ref, gx_ref, gm_ref):
        _init(gm_ref, pl.program_id(1) == 0)
        gx_ref[...] = dh_ref[NMETA:NMETA + seq, :]
        gm_ref[...] += dh_ref[0:NMETA, :]

    return pl.pallas_call(
        body, name="input_grads", grid=(D // cb, bl),
        in_specs=[pl.BlockSpec((None, lp, cb), lambda c, b: (b, 0, c))],
        out_specs=[pl.BlockSpec((None, seq, cb), lambda c, b: (b, 0, c)), pl.BlockSpec((NMETA, cb), lambda c, b: (0, c))],
        out_shape=[_sds((bl, seq, D), F32), _sds((NMETA, D), F32)],
        compiler_params=_cp(2),
    )(dh0)


GATHER_PLAN = {
    "conv_mid_fwd0": [("ffn_w_gate", 0), ("ffn_w_up", 0), ("ffn_w_down", 0)],
    "mixer_ffn_fwd0": [("conv_w_in", 1), ("conv_w_out", 1), ("ffn_w_gate", 1)],
    "conv_mid_fwd1": [("ffn_w_up", 1), ("ffn_w_down", 1), ("w_kv", 0), ("w_q", 0)],
    "mixer_ffn_fwd1": [("w_o", 0), ("ffn_w_down", 2)],
    "attn_fwd0": [("ffn_w_gate", 2), ("ffn_w_up", 2), ("w_q", 1), ("w_o", 1), ("ffn_w_gate", 3), ("ffn_w_up", 3), ("ffn_w_down", 3)],
}
EXCHANGE_PLAN = {
    "attn_bwd1": [("ffn_w_down3", ALL), ("ffn_w_gate3", ALL), ("ffn_w_up3", ALL)],
    "dw_down2": [("w_o1", ALL)],
    "ffn_bwd_x2": [("w_q1", ALL)],
    "attn_bwd0": [("ffn_w_down2", ALL), ("ffn_w_gate2", ALL), ("ffn_w_up2", ALL)],
    "dw_down1": [("w_o0", ALL), ("w_q0", H1)],
    "ffn_bwd_x1": [("w_q0", H2), ("w_kv", ALL)],
    "dw_gate1": [("ffn_w_down1", H1)],
    "dw_up1": [("ffn_w_down1", H2)],
    "conv_mid_bwd1": [("ffn_w_gate1", ALL), ("ffn_w_up1", H1)],
    "conv_in_bwd1": [("ffn_w_up1", H2)],
    "dw_down0": [("conv_w_out1", ALL)],
    "ffn_bwd_x0": [("conv_w_in1", ALL)],
    "dw_gate0": [("ffn_w_down0", H1)],
    "dw_up0": [("ffn_w_down0", H2)],
    "conv_out_bwd0": [("ffn_w_gate0", H1)],
    "conv_mid_bwd0": [("ffn_w_gate0", H2), ("ffn_w_up0", H1), ("conv_w_out0", ALL)],
    "dw_conv_in0": [("ffn_w_up0", H2)],
    "conv_in_bwd0": [("conv_w_in0", H1)],
    "tail": [("conv_w_in0", H2)],
}
BIG = {"conv_w_in": "pieces", "conv_w_out": "rows", "w_kv": "rows", "w_q": "rows", "w_o": "rows",
       "ffn_w_gate": "rows", "ffn_w_up": "rows", "ffn_w_down": "rows"}
EXCHANGE_KIND = BIG
TRANSPOSED = ("ffn_w_gate", "ffn_w_up")


def gathered_matrix(name, layer, blocks8):
    if BIG[name] == "rows":
        return blocks8.reshape(NDEV * blocks8.shape[1], blocks8.shape[2])
    return join_columns(blocks8, f"join_{name}{layer}")


def local_step(x, tgt, meta8, w, mats, shards):
    bl, seq, _ = x.shape
    lp = -(-(NMETA + seq) // QB) * QB
    tpb = 4
    tm = lp // tpb
    t = bl * lp
    na = 2
    flat = lambda a: a.reshape(t, D)
    mats = dict(mats)

    def riders(carrier):
        return [shards[key] for key in GATHER_PLAN[carrier]]

    def landed(carrier, blocks):
        for key, b8 in zip(GATHER_PLAN[carrier], blocks):
            mats[key] = gathered_matrix(*key, b8)

    h = flat(embed(x, meta8, lp))
    saved = []
    kvs = None
    for l in range(4):
        rec = {"h": h}
        if l < na:
            rec["u"], rec["big"], rec["a"] = conv_in_fwd(h, w["norm_mix"], l, mats["conv_w_in", l], w["conv_b_in"], l, tm)
            name = f"conv_mid_fwd{l}"
            (rec["c"], rec["s"]), _, got = conv_mid_fwd(rec["a"], w["conv_dw"], w["conv_ln_g"], w["conv_ln_b"], l, tm, tpb,
                                                         riders(name))
            landed(name, got)
            mixed, w_out, lw, bias = rec["s"], mats["conv_w_out", l], l, w["conv_b_out"]
        else:
            j = l - na
            if kvs is None:
                kvs = dict(zip(("kn", "kv", "k", "v"), kv_fwd(h, w["kv_norm"], mats["w_kv", 0], w["k_norm"], tm)))
                kvs["h"] = h
                kvs["k3"], kvs["v3"] = kvs["k"].reshape(bl, lp, KVD), kvs["v"].reshape(bl, lp, KVD)
                kvs["kt"], kvs["vt"] = transpose_seq(kvs["k3"], "transpose_k"), transpose_seq(kvs["v3"], "transpose_v")
            rec["u"], rec["q"] = q_fwd(h, w["norm_mix"], l, mats["w_q", j], j, tm)
            name = f"attn_fwd{j}"
            (rec["o"], rec["lse"]), _, got = attn_fwd(rec["q"], kvs["kt"], kvs["v3"], w["q_norm"], w["attn_sinks"], j, bl, lp,
                                                      riders(name) if name in GATHER_PLAN else [])
            if name in GATHER_PLAN:
                landed(name, got)
            mixed, w_out, lw, bias = rec["o"], mats["w_o", j], j, None
        name = f"mixer_ffn_fwd{l}"
        (rec["h1"], rec["u2"], rec["g"], rec["up"], rec["hid"], h), _, got = mixer_ffn_fwd(
            h, mixed, w_out, lw, bias, w["norm_ffn"], l, mats["ffn_w_gate", l], mats["ffn_w_up", l], mats["ffn_w_down", l], tm // 2,
            riders(name) if name in GATHER_PLAN else [])
        if name in GATHER_PLAN:
            landed(name, got)
        saved.append(rec)

    dh3, loss_blk = loss_fwd(h.reshape(bl, lp, D), tgt)
    dh = flat(dh3)

    big, small, arrived = {}, {}, {}
    dks, dvs = [], []

    def ride(kernel_name):
        return [(big[nm], EXCHANGE_KIND[nm.rstrip("0123456789")], ks) for nm, ks in EXCHANGE_PLAN.get(kernel_name, [])]

    def landed_x(kernel_name, arrivals):
        for (nm, _), got in zip(EXCHANGE_PLAN.get(kernel_name, []), arrivals):
            arrived.setdefault(nm, []).append(got)

    def dw(name, grad, x, dy, **kw):
        big[grad], got = mm_tn(x, dy, 2 * tm, name, xch=ride(name), **kw)
        landed_x(name, got)

    for l in reversed(range(4)):
        rec = saved[l]
        dw(f"dw_down{l}", f"ffn_w_down{l}", rec["hid"], dh)
        name = f"ffn_bwd_x{l}"
        outs, got, _ = ffn_bwd_x(
            dh, rec["g"], rec["up"], rec["h1"], w["norm_ffn"], l, mats["ffn_w_down", l], mats["ffn_w_gate", l], mats["ffn_w_up", l],
            mats["w_o", l - na] if l >= na else None, tm // 2, ride(name))
        landed_x(name, got)
        dg, du, dh1, small[f"norm_ffn{l}"] = outs[:4]
        dw(f"dw_gate{l}", f"ffn_w_gate{l}", rec["u2"], dg, transposed=True)
        dw(f"dw_up{l}", f"ffn_w_up{l}", rec["u2"], du, transposed=True)
        if l >= na:
            j = l - na
            dw(f"dw_o{j}", f"w_o{j}", rec["o"], dh1)
            name = f"attn_bwd{j}"
            (dq, dk, dv, small[f"q_norm{j}"], small[f"attn_sinks{j}"]), got, _ = attn_bwd(
                rec["q"], kvs["k3"], kvs["kt"], kvs["vt"], outs[4], rec["o"], rec["lse"], w["q_norm"], w["attn_sinks"], j, bl, lp,
                ride(name))
            landed_x(name, got)
            dks.append(dk)
            dvs.append(dv)
            dw(f"dw_q{j}", f"w_q{j}", rec["u"], dq)
            dh, small[f"norm_mix{l}"] = proj_bwd(dq, mats["w_q", j], rec["h"], w["norm_mix"], l, dh1, tm, f"q_bwd{j}")[0]
            if l == na:
                dkv, small["k_norm"] = kv_bwd_pre(sum_transposed(*dks), sum_transposed(*dvs), kvs["kv"], w["k_norm"], tm)
                dw("dw_kv", "w_kv", kvs["kn"], dkv)
                dh, small["kv_norm"] = proj_bwd(dkv, mats["w_kv", 0], kvs["h"], w["kv_norm"], None, dh, tm, "kv_bwd")[0]
        else:
            name = f"conv_out_bwd{l}"
            (dc, small[f"conv_ln_g{l}"], small[f"conv_ln_b{l}"], small[f"conv_b_out{l}"]), got, _ = conv_out_bwd(
                dh1, rec["c"], w["conv_ln_g"], w["conv_ln_b"], mats["conv_w_out", l], l, tm, ride(name))
            landed_x(name, got)
            dw(f"dw_conv_out{l}", f"conv_w_out{l}", rec["s"], dh1)
            name = f"conv_mid_bwd{l}"
            (da, small[f"conv_b_in{l}"], small[f"conv_dw{l}"]), got, _ = conv_mid_bwd(
                dc, rec["a"], rec["big"], w["conv_dw"], l, tm, tpb, ride(name))
            landed_x(name, got)
            dw(f"dw_conv_in{l}", f"conv_w_in{l}", rec["u"], da, split=True)
            name = f"conv_in_bwd{l}"
            (dh, small[f"norm_mix{l}"]), got, _ = proj_bwd(da, mats["conv_w_in", l], rec["h"], w["norm_mix"], l, dh1, tm, name,
                                                           ride(name))
            landed_x(name, got)
    grad_x, small["meta_tokens"] = input_grads(dh.reshape(bl, lp, D), seq)
    return loss_blk, grad_x, big, arrived, small


def all_gather_blocks(blocks):
    n = len(blocks)

    def body(*refs):
        srcs, outs, sems = refs[:n], refs[n:2 * n], refs[2 * n:]
        _gat_start(srcs, outs, sems)
        _gat_forward(srcs, outs, sems)
        _gat_wait(srcs, outs, sems)

    any_spec = pl.BlockSpec(memory_space=pl.ANY)
    return pl.pallas_call(
        body, name="all_gather_blocks", out_shape=[_sds((NDEV,) + tuple(a.shape), a.dtype) for a in blocks],
        in_specs=[any_spec] * n, out_specs=[any_spec] * n, scratch_shapes=_xch_scratch(n),
    )(*blocks)


def cast_bf16(ws):
    n = len(ws)
    counts = [1 if x.ndim == 2 else x.shape[0] for x in ws]

    def body(*refs):
        outs = iter(refs[n:])
        for a in range(n):
            for l in range(counts[a]):
                next(outs)[...] = (refs[a][...] if ws[a].ndim == 2 else refs[a][l]).astype(BF)

    flat = pl.pallas_call(
        body, name="cast_bf16", out_shape=[_sds(x.shape[-2:], BF) for x, k in zip(ws, counts) for _ in range(k)],
        compiler_params=pltpu.CompilerParams(vmem_limit_bytes=VMEM_LIMIT),
    )(*ws)
    it = iter(flat)
    return [[next(it) for _ in range(k)] for k in counts]


def join_columns(w8, name):
    _, kk, n8 = w8.shape

    def body(x_ref, o_ref):
        o_ref[...] = jnp.concatenate([x_ref[p] for p in range(NDEV)], axis=1)

    return pl.pallas_call(body, name=name, out_shape=_sds((kk, NDEV * n8), w8.dtype),
                          compiler_params=pltpu.CompilerParams(vmem_limit_bytes=VMEM_LIMIT))(w8)


def _adamw_math(w, m, v, g):
    m2 = B1 * m + (1.0 - B1) * g
    v2 = B2 * v + (1.0 - B2) * (g * g)
    mh = m2 / (1.0 - B1 ** STEP)
    vh = v2 / (1.0 - B2 ** STEP)
    return -LR * (mh / (jnp.sqrt(vh) + AEPS) + WD * w), m2, v2


def adamw_big(w, m, v, parts, name, xch=(), gat=()):
    lyr, r, c = w.shape
    by_cols = c >= 512
    blk = (lyr, r, 256) if by_cols else (lyr, 256 if r % 256 == 0 else r, c)
    imap = (lambda i: (0, 0, i)) if by_cols else (lambda i: (0, i, 0))
    counts = [len(per_layer) for per_layer in parts]

    def body(w_ref, m_ref, v_ref, *rest):
        p_refs, (g_ref, d_ref, m2_ref, v2_ref) = iter(rest[:sum(counts)]), rest[sum(counts):]
        for l in range(lyr):
            g = None
            for _ in range(counts[l]):
                ref = next(p_refs)
                for q in range(ref.shape[0]):
                    g = ref[q].astype(F32) if g is None else g + ref[q].astype(F32)
            g_ref[l] = g
            d_ref[l], m2_ref[l], v2_ref[l] = _adamw_math(w_ref[l], m_ref[l], v_ref[l], g)

    spec = pl.BlockSpec(blk, imap)
    flat = [a for per_layer in parts for a in per_layer]
    pspecs = [pl.BlockSpec((a.shape[0],) + blk[1:], imap) for a in flat]
    return _call(body, name, ((c // 256) if by_cols else (r // blk[1]),), [spec, spec, spec] + pspecs,
                 [spec] * 4, [_sds((lyr, r, c), F32)] * 4, (w, m, v, *flat), xch=xch, gat=gat)


SMALL_ROWS = 104
REPLICATED = {"norm_mix": (0, 4, D), "norm_ffn": (4, 4, D), "kv_norm": (8, 1, D), "k_norm": (9, 1, HD), "q_norm": (10, 2, HD),
              "attn_sinks": (12, 2, NH)}
LOSS_ROW = 14
SHARDED = {"meta_tokens": (16, NMETA), "conv_b_in": (32, 4), "conv_dw": (36, 2 * CW), "conv_ln_g": (98, 2), "conv_ln_b": (100, 2),
           "conv_b_out": (102, 2)}


def pack_small(gs, loss_blk):
    order = ([f"norm_mix{l}" for l in range(4)] + [f"norm_ffn{l}" for l in range(4)] + ["kv_norm", "k_norm", "q_norm0", "q_norm1",
             "attn_sinks0", "attn_sinks1", "meta_tokens", "conv_b_in0", "conv_b_in1", "conv_dw0", "conv_dw1", "conv_ln_g0",
             "conv_ln_g1", "conv_ln_b0", "conv_ln_b1", "conv_b_out0", "conv_b_out1"])

    def body(*refs):
        r = dict(zip(order, refs))
        loss_ref, o_ref = refs[len(order)], refs[len(order) + 1]
        o_ref[...] = jnp.zeros_like(o_ref)
        for l in range(4):
            o_ref[l:l + 1, :] = r[f"norm_mix{l}"][...]
            o_ref[4 + l:5 + l, :] = r[f"norm_ffn{l}"][...]
        o_ref[8:9, :] = r["kv_norm"][...]
        o_ref[9:10, 0:HD] = r["k_norm"][...]
        for j in range(2):
            o_ref[10 + j:11 + j, 0:HD] = r[f"q_norm{j}"][...]
            o_ref[12 + j:13 + j, 0:NH] = r[f"attn_sinks{j}"][...]
            o_ref[32 + 2 * j:33 + 2 * j, :] = r[f"conv_b_in{j}"][:, 0:D]
            o_ref[33 + 2 * j:34 + 2 * j, :] = r[f"conv_b_in{j}"][:, D:2 * D]
            o_ref[36 + CW * j:36 + CW * (j + 1), :] = r[f"conv_dw{j}"][0:CW, :]
            o_ref[98 + j:99 + j, :] = r[f"conv_ln_g{j}"][...]
            o_ref[100 + j:101 + j, :] = r[f"conv_ln_b{j}"][...]
            o_ref[102 + j:103 + j, :] = r[f"conv_b_out{j}"][...]
        o_ref[LOSS_ROW:LOSS_ROW + 1, 0:1] = loss_ref[0:1, 0:1]
        o_ref[16:16 + NMETA, :] = r["meta_tokens"][...]

    return pl.pallas_call(body, name="pack_small", out_shape=_sds((SMALL_ROWS, D), F32))(*[gs[k] for k in order], loss_blk)


def adamw_small(g8, wts, mom, var):
    names = list(REPLICATED) + list(SHARDED)
    shape2 = {"kv_norm": (1, D), "k_norm": (1, HD)}
    ins = [a[k].reshape(shape2.get(k, a[k].shape)) for a in (wts, mom, var) for k in names]
    n = len(names)

    def body(*refs):
        g8_ref, w_refs, m_refs, v_refs = refs[0], refs[1:1 + n], refs[1 + n:1 + 2 * n], refs[1 + 2 * n:1 + 3 * n]
        loss_ref, outs, red_ref = refs[1 + 3 * n], refs[2 + 3 * n:-1], refs[-1]
        me = _my_index()
        acc = g8_ref[0]
        for q in range(1, NDEV):
            acc = acc + g8_ref[q]
        red_ref[...] = acc
        loss_ref[...] = red_ref[LOSS_ROW:LOSS_ROW + 1, 0:1]

        def mine(rows, width):
            acc = jnp.zeros((rows.stop - rows.start, width), F32)
            for p_ in range(NDEV):
                acc = acc + jnp.where(me == p_, red_ref[rows, p_ * width:(p_ + 1) * width], 0.0)
            return acc

        for i, k in enumerate(names):
            if k in REPLICATED:
                r0, nr, width = REPLICATED[k]
                g = red_ref[r0:r0 + nr, 0:width]
            elif k == "conv_b_in":
                half = D // (2 * D // NDEV)
                acc = jnp.zeros((2, 2 * D // NDEV), F32)
                for p_ in range(NDEV):
                    c0 = (p_ % half) * (2 * D // NDEV)
                    part = jnp.concatenate([red_ref[32 + 2 * j + p_ // half:33 + 2 * j + p_ // half, c0:c0 + 2 * D // NDEV]
                                            for j in range(2)], axis=0)
                    acc = acc + jnp.where(me == p_, part, 0.0)
                g = acc
            else:
                r0, nr = SHARDED[k]
                g = mine(slice(r0, r0 + nr), D // NDEV)
            w_, m_, v_ = w_refs[i], m_refs[i], v_refs[i]
            g_out, d_out, m_out, v_out = outs[4 * i:4 * i + 4]
            if k == "conv_dw":
                for j in range(2):
                    gj = g[CW * j:CW * (j + 1)]
                    g_out[j] = gj
                    d_out[j], m_out[j], v_out[j] = _adamw_math(w_[j], m_[j], v_[j], gj)
            else:
                g_out[...] = g
                d_out[...], m_out[...], v_out[...] = _adamw_math(w_[...], m_[...], v_[...], g)

    out_shape = [_sds((1, 1), F32)] + [_sds(ins[i].shape, F32) for i in range(n) for _ in range(4)]
    res = pl.pallas_call(body, name="adamw_small", out_shape=out_shape, scratch_shapes=[pltpu.VMEM((SMALL_ROWS, D), F32)])(g8, *ins)
    out = {k: tuple(o.reshape(wts[k].shape) for o in res[1 + 4 * i:5 + 4 * i]) for i, k in enumerate(names)}
    return res[0], out


NAMES = ["meta_tokens", "norm_mix", "norm_ffn", "conv_w_in", "conv_b_in", "conv_dw", "conv_ln_g", "conv_ln_b", "conv_w_out",
         "conv_b_out", "kv_norm", "w_kv", "k_norm", "w_q", "q_norm", "attn_sinks", "w_o", "ffn_w_gate", "ffn_w_up", "ffn_w_down"]


def kernel(x, meta_tokens, norm_mix, norm_ffn, conv_w_in, conv_b_in, conv_dw, conv_ln_g, conv_ln_b, conv_w_out, conv_b_out, kv_norm, w_kv, k_norm, w_q, q_norm, attn_sinks, w_o, ffn_w_gate, ffn_w_up, ffn_w_down, loss_target, m_meta_tokens, m_norm_mix, m_norm_ffn, m_conv_w_in, m_conv_b_in, m_conv_dw, m_conv_ln_g, m_conv_ln_b, m_conv_w_out, m_conv_b_out, m_kv_norm, m_w_kv, m_k_norm, m_w_q, m_q_norm, m_attn_sinks, m_w_o, m_ffn_w_gate, m_ffn_w_up, m_ffn_w_down, v_meta_tokens, v_norm_mix, v_norm_ffn, v_conv_w_in, v_conv_b_in, v_conv_dw, v_conv_ln_g, v_conv_ln_b, v_conv_w_out, v_conv_b_out, v_kv_norm, v_w_kv, v_k_norm, v_w_q, v_q_norm, v_attn_sinks, v_w_o, v_ffn_w_gate, v_ffn_w_up, v_ffn_w_down):
    wts = dict(zip(NAMES, (meta_tokens, norm_mix, norm_ffn, conv_w_in, conv_b_in, conv_dw, conv_ln_g, conv_ln_b, conv_w_out,
                           conv_b_out, kv_norm, w_kv, k_norm, w_q, q_norm, attn_sinks, w_o, ffn_w_gate, ffn_w_up, ffn_w_down)))
    mom = dict(zip(NAMES, (m_meta_tokens, m_norm_mix, m_norm_ffn, m_conv_w_in, m_conv_b_in, m_conv_dw, m_conv_ln_g, m_conv_ln_b,
                           m_conv_w_out, m_conv_b_out, m_kv_norm, m_w_kv, m_k_norm, m_w_q, m_q_norm, m_attn_sinks, m_w_o,
                           m_ffn_w_gate, m_ffn_w_up, m_ffn_w_down)))
    var = dict(zip(NAMES, (v_meta_tokens, v_norm_mix, v_norm_ffn, v_conv_w_in, v_conv_b_in, v_conv_dw, v_conv_ln_g, v_conv_ln_b,
                           v_conv_w_out, v_conv_b_out, v_kv_norm, v_w_kv, v_k_norm, v_w_q, v_q_norm, v_attn_sinks, v_w_o,
                           v_ffn_w_gate, v_ffn_w_up, v_ffn_w_down)))
    for k in TRANSPOSED:
        wts[k], mom[k], var[k] = (jnp.swapaxes(a, 1, 2) for a in (wts[k], mom[k], var[k]))

    big_names = list(BIG)
    layers = cast_bf16([wts[k] for k in big_names])
    shards = {(k, l): blk for k, per_layer in zip(big_names, layers) for l, blk in enumerate(per_layer)}
    first = [("conv_w_in", 0), ("conv_w_out", 0)]
    vec_names = ["meta_tokens", "conv_b_in", "conv_dw", "conv_ln_g", "conv_ln_b", "conv_b_out"]
    gathered = all_gather_blocks([shards[key] for key in first] + [wts[k] for k in vec_names])
    mats = {key: gathered_matrix(*key, b8) for key, b8 in zip(first, gathered)}
    full = dict(zip(vec_names, gathered[len(first):]))
    join_vec = lambda a: jnp.moveaxis(a, 0, -2).reshape(a.shape[1:-1] + (NDEV * a.shape[-1],))
    w = {}
    w["conv_b_in"] = join_vec(full["conv_b_in"]).reshape(2, 1, 2 * D)
    w["conv_dw"] = join_vec(full["conv_dw"])
    for k in ("conv_ln_g", "conv_ln_b", "conv_b_out"):
        w[k] = join_vec(full[k]).reshape(2, 1, D)
    w["norm_mix"] = norm_mix.reshape(4, 1, D)
    w["norm_ffn"] = norm_ffn.reshape(4, 1, D)
    w["kv_norm"] = kv_norm.reshape(1, D)
    w["k_norm"] = k_norm.reshape(1, HD)
    w["q_norm"] = q_norm.reshape(2, 1, HD)
    w["attn_sinks"] = attn_sinks.reshape(2, 1, NH)

    loss_blk, grad_x, gbig, arrived, gs = local_step(x, loss_target, full["meta_tokens"], w, mats, shards)

    packed = pack_small(gs, loss_blk)

    grads, delta, new_m, new_v = {}, {}, {}, {}
    tail = EXCHANGE_PLAN["tail"]
    waiting = {nm.rstrip("0123456789") for nm, _ in tail}
    order = sorted([k for k in big_names if k not in waiting], key=lambda k: -wts[k].size) + [k for k in big_names if k in waiting]
    small8 = None
    for pos, k in enumerate(order):
        flat2 = wts[k].ndim == 2
        as3 = (lambda a: a[None]) if flat2 else (lambda a: a)
        riders = tail if pos == 0 else []
        gat = [packed] if pos == 1 else []
        parts = [arrived[k]] if flat2 else [arrived[f"{k}{i}"] for i in range(wts[k].shape[0])]
        outs, got_x, got_g = adamw_big(as3(wts[k]), as3(mom[k]), as3(var[k]), parts, "adamw_" + k,
                                       xch=[(gbig[nm], EXCHANGE_KIND[nm.rstrip("0123456789")], ks) for nm, ks in riders], gat=gat)
        for (nm, _), got in zip(riders, got_x):
            arrived[nm].append(got)
        if gat:
            small8 = got_g[0]
        grads[k], delta[k], new_m[k], new_v[k] = [o[0] if flat2 else (jnp.swapaxes(o, 1, 2) if k in TRANSPOSED else o) for o in outs]
    loss, small = adamw_small(small8, wts, mom, var)
    for k, (g_, d_, m_, v_) in small.items():
        grads[k], delta[k], new_m[k], new_v[k] = g_, d_, m_, v_
    return (loss.reshape(()), grad_x, *[grads[k] for k in NAMES], *[delta[k] for k in NAMES], *[new_m[k] for k in NAMES],
            *[new_v[k] for k in NAMES])
```

```python
import functools

import jax
import jax.numpy as jnp
from jax import lax
from jax.experimental import pallas as pl
from jax.experimental.pallas import tpu as pltpu

F32 = jnp.float32
BF = jnp.bfloat16

D = 1024
DFF = 2816
NH = 16
NKV = 4
HD = 64
KVD = NKV * HD
NMETA = 16
CW = 31
HALO = 32
CHUNK = 32
QB = 128
EPS = 1e-6
NEG = -1e30
NDEV = 8
SCALE = HD ** -0.5

LR, B1, B2, AEPS, WD, STEP = 0.001, 0.9, 0.999, 1e-08, 0.01, 10

VMEM_LIMIT = 56 * 2 ** 20
MESH = pl.DeviceIdType.MESH


def _cp(n):
    return pltpu.CompilerParams(dimension_semantics=("arbitrary",) * n, vmem_limit_bytes=VMEM_LIMIT)


def _row(tm, c):
    return pl.BlockSpec((tm, c), lambda i: (i, 0))


def _res(shape):
    return pl.BlockSpec(shape, lambda i: (0,) * len(shape), pipeline_mode=pl.Buffered(1))


def _lay(l, shape):
    return pl.BlockSpec((None,) + tuple(shape), lambda i: (l,) + (0,) * len(shape), pipeline_mode=pl.Buffered(1))


def _acc(shape):
    return pl.BlockSpec(shape, lambda i: (0,) * len(shape))


def _sds(shape, dt):
    return jax.ShapeDtypeStruct(tuple(shape), dt)


def _dot(a, b):
    return jnp.dot(a.astype(BF), b.astype(BF), preferred_element_type=F32)


def _dot_nt(a, b):
    return lax.dot_general(a.astype(BF), b.astype(BF), (((1,), (1,)), ((), ())), preferred_element_type=F32)


def _dot_tn(a, b):
    return lax.dot_general(a.astype(BF), b.astype(BF), (((0,), (0,)), ((), ())), preferred_element_type=F32)


def _rstd(x):
    return lax.rsqrt(jnp.mean(x * x, axis=-1, keepdims=True) + EPS)


def _rms_bwd(x, g, dy):
    r = _rstd(x)
    z = dy * g
    dx = r * z - x * (r * r * r * jnp.mean(z * x, axis=-1, keepdims=True))
    return dx, jnp.sum(dy * x * r, axis=0, keepdims=True)


def _sig(x):
    return jax.nn.sigmoid(x)


def _fold8(x):
    out = x[0:8]
    for k in range(1, x.shape[0] // 8):
        out = out + x[8 * k:8 * k + 8]
    return out


def _shifted(win):
    return [win] + [pltpu.roll(win, 2 * CHUNK - rho, 0) for rho in range(1, 8)]


def _tap(phases, o):
    return phases[o % 8][8 * (o // 8):8 * (o // 8) + CHUNK]


def _init(ref, first):
    @pl.when(first)
    def _():
        ref[...] = jnp.zeros_like(ref)


def _my_index():
    return 4 * lax.axis_index("x") + 2 * lax.axis_index("y") + lax.axis_index("c")


def _coords(idx):
    return (idx // 4, (idx // 2) % 2, idx % 2)


ALL = tuple(range(NDEV))
H1, H2 = (0, 1, 2, 4, 6), (3, 5, 7)


def _xch_shapes(xch):
    return [_sds((len(ks),) + ((a.shape[0] // NDEV, a.shape[1]) if k == "rows" else tuple(a.shape[1:])), a.dtype) for a, k, ks in xch]


def _xch_scratch(n):
    return [pltpu.SemaphoreType.DMA((n, NDEV)), pltpu.SemaphoreType.DMA((n, NDEV)), pltpu.SemaphoreType.DMA((n,))]


def _xch_copies(meta, srcs, outs, sems, arrivals):
    send_sems, recv_sems, local_sems = sems
    me = _my_index()

    def piece(a, p):
        if meta[a][0] == "rows":
            r = srcs[a].shape[0] // NDEV
            return srcs[a].at[pl.ds(p * r, r), :]
        return srcs[a].at[p]

    def remote(a, i, k, src):
        return pltpu.make_async_remote_copy(
            src_ref=src, dst_ref=outs[a].at[i], send_sem=send_sems.at[a, k], recv_sem=recv_sems.at[a, k],
            device_id=_coords(me ^ k), device_id_type=MESH)

    local, sends, recvs = [], [], []
    for a, (_, ks) in enumerate(meta):
        for i, k in enumerate(ks):
            if k == 0:
                local.append(pltpu.make_async_copy(piece(a, me), outs[a].at[i], local_sems.at[a]))
            else:
                sends.append(remote(a, i, k, piece(a, me ^ k)))
                if arrivals:
                    recvs.append(remote(a, i, k, piece(a, me)))
    return local, sends, recvs


def _xch_start(meta, srcs, outs, sems):
    local, sends, _ = _xch_copies(meta, srcs, outs, sems, False)
    for cp in local + sends:
        cp.start()


def _xch_wait(meta, srcs, outs, sems):
    local, sends, recvs = _xch_copies(meta, srcs, outs, sems, True)
    for cp in recvs:
        cp.wait_recv()
    for cp in sends:
        cp.wait_send()
    for cp in local:
        cp.wait()


def _gat_copies(srcs, outs, sems):
    send_sems, recv_sems, local_sems = sems
    x, y, c = lax.axis_index("x"), lax.axis_index("y"), lax.axis_index("c")
    me, sibling = (x, y, c), (x, y, 1 - c)
    chips = [(1 - x, y), (x, 1 - y), (1 - x, 1 - y)]

    def copy(a, k, owner, to, from_block=False):
        slot = outs[a].at[4 * owner[0] + 2 * owner[1] + owner[2]]
        return pltpu.make_async_remote_copy(
            src_ref=srcs[a] if from_block else slot, dst_ref=slot, send_sem=send_sems.at[a, k], recv_sem=recv_sems.at[a, k],
            device_id=to, device_id_type=MESH)

    n = len(srcs)
    local = lambda: [pltpu.make_async_copy(srcs[a], outs[a].at[4 * x + 2 * y + c], local_sems.at[a]) for a in range(n)]
    first = lambda: [cp for a in range(n) for cp in
                     [copy(a, 0, me, sibling, True)] + [copy(a, 1 + j, me, (*chip, c), True) for j, chip in enumerate(chips)]]
    landed = lambda: [copy(a, 1 + j, (*chip, c), me) for a in range(n) for j, chip in enumerate(chips)]
    passed = lambda: [copy(a, 4 + j, (*chip, c), sibling) for a in range(n) for j, chip in enumerate(chips)]
    final = lambda: [cp for a in range(n) for cp in
                     [copy(a, 0, sibling, me)] + [copy(a, 4 + j, (*chip, 1 - c), me) for j, chip in enumerate(chips)]]
    return local, first, landed, passed, final


def _gat_start(srcs, outs, sems):
    local, first, _, _, _ = _gat_copies(srcs, outs, sems)
    for cp in local() + first():
        cp.start()


def _gat_forward(srcs, outs, sems):
    _, _, landed, passed, _ = _gat_copies(srcs, outs, sems)
    for got, on in zip(landed(), passed()):
        got.wait_recv()
        on.start()


def _gat_wait(srcs, outs, sems):
    local, first, _, passed, final = _gat_copies(srcs, outs, sems)
    for cp in final():
        cp.wait_recv()
    for cp in first() + passed():
        cp.wait_send()
    for cp in local():
        cp.wait()


def _call(body, name, grid, in_specs, out_specs, out_shape, args, scratch=(), xch=(), gat=()):
    n_in, n_out, n_x, n_g, n_s = len(in_specs), len(out_specs), len(xch), len(gat), len(scratch)
    kinds = [(k, ks) for _, k, ks in xch]
    total = 1
    for g in grid:
        total *= g

    def wrapped(*refs):
        ins, refs = refs[:n_in], refs[n_in:]
        x_src, refs = refs[:n_x], refs[n_x:]
        g_src, refs = refs[:n_g], refs[n_g:]
        outs, refs = refs[:n_out], refs[n_out:]
        x_out, refs = refs[:n_x], refs[n_x:]
        g_out, refs = refs[:n_g], refs[n_g:]
        own, refs = refs[:n_s], refs[n_s:]
        x_sems, g_sems = (refs[:3], refs[3:]) if n_x else ((), refs)
        step = pl.program_id(0)
        for d in range(1, len(grid)):
            step = step * grid[d] + pl.program_id(d)
        if n_x or n_g:
            @pl.when(step == 0)
            def _():
                if n_x:
                    _xch_start(kinds, x_src, x_out, x_sems)
                if n_g:
                    _gat_start(g_src, g_out, g_sems)

        body(*ins, *outs, *own)
        if n_g:
            @pl.when(step == max(total - 2, 0))
            def _():
                _gat_forward(g_src, g_out, g_sems)

        if n_x or n_g:
            @pl.when(step == total - 1)
            def _():
                if n_x:
                    _xch_wait(kinds, x_src, x_out, x_sems)
                if n_g:
                    _gat_wait(g_src, g_out, g_sems)

    any_spec = pl.BlockSpec(memory_space=pl.ANY)
    g_shapes = [_sds((NDEV,) + tuple(a.shape), a.dtype) for a in gat]
    res = pl.pallas_call(
        wrapped, name=name, grid=grid, in_specs=list(in_specs) + [any_spec] * (n_x + n_g),
        out_specs=list(out_specs) + [any_spec] * (n_x + n_g), out_shape=list(out_shape) + _xch_shapes(xch) + g_shapes,
        scratch_shapes=list(scratch) + (_xch_scratch(n_x) if n_x else []) + (_xch_scratch(n_g) if n_g else []),
        compiler_params=_cp(len(grid)),
    )(*args, *[a for a, _, _ in xch], *gat)
    return res[:n_out], res[n_out:n_out + n_x], res[n_out + n_x:]


def embed(x, meta8, lp, gat):
    bl, seq, _ = x.shape
    c8 = D // NDEV
    cb = 2 * c8

    def body(x_ref, m_ref, h_ref):
        h_ref[0:NMETA, :] = jnp.concatenate([m_ref[0], m_ref[1]], axis=1)
        h_ref[NMETA:NMETA + seq, :] = x_ref[...]
        h_ref[NMETA + seq:, :] = jnp.zeros((lp - NMETA - seq, cb), F32)

    (h0,), _, got = _call(
        body, "embed", (bl, D // cb),
        [pl.BlockSpec((None, seq, cb), lambda b, c: (b, 0, c)), pl.BlockSpec((2, NMETA, c8), lambda b, c: (c, 0, 0))],
        [pl.BlockSpec((None, lp, cb), lambda b, c: (b, 0, c))], [_sds((bl, lp, D), F32)], (x, meta8), gat=gat)
    return h0, got


def conv_in_fwd(h, nm, l, w_in, b_in, i, tm):
    t = h.shape[0]

    def body(h_ref, g_ref, w_ref, b_ref, u_ref, big_ref, a_ref):
        x = h_ref[...]
        ub = (x * _rstd(x) * g_ref[...]).astype(BF)
        u_ref[...] = ub
        big = jnp.dot(ub, w_ref[...], preferred_element_type=F32) + b_ref[...]
        big_ref[...] = big.astype(BF)
        a_ref[...] = big[:, :D] * _sig(big[:, D:])

    return pl.pallas_call(
        body, name=f"conv_in_fwd{i}", grid=(t // tm,),
        in_specs=[_row(tm, D), _lay(l, (1, D)), _res((D, 2 * D)), _lay(i, (1, 2 * D))],
        out_specs=[_row(tm, D), _row(tm, 2 * D), _row(tm, D)],
        out_shape=[_sds((t, D), BF), _sds((t, 2 * D), BF), _sds((t, D), F32)],
        compiler_params=_cp(1),
    )(h, nm, w_in, b_in)


def _prev_halo(tm):
    return pl.BlockSpec((HALO, D), lambda i: (jnp.maximum(i * (tm // HALO) - 1, 0), 0))


def _next_halo(tm, t):
    return pl.BlockSpec((HALO, D), lambda i: (jnp.minimum((i + 1) * (tm // HALO), t // HALO - 1), 0))


def conv_mid_fwd(a, dw, ln_g, ln_b, i, tm, tpb, gat):
    t = a.shape[0]

    def body(a_ref, halo_ref, dw_ref, g_ref, b_ref, c_ref, s_ref, ext):
        first = pl.program_id(0) % tpb == 0
        ext[0:HALO] = jnp.where(first, 0.0, halo_ref[...])
        ext[HALO:] = a_ref[...]

        def chunk(k, carry):
            r0 = pl.multiple_of(k * CHUNK, CHUNK)
            win = _shifted(ext[pl.ds(r0, 2 * CHUNK), :])
            c = jnp.zeros((CHUNK, D), F32)
            for j in range(CW):
                c = c + dw_ref[j:j + 1, :] * _tap(win, j + 2)
            c_ref[pl.ds(r0, CHUNK), :] = c
            mu = jnp.mean(c, axis=-1, keepdims=True)
            xc = c - mu
            n = xc * lax.rsqrt(jnp.mean(xc * xc, axis=-1, keepdims=True) + EPS) * g_ref[...] + b_ref[...]
            s_ref[pl.ds(r0, CHUNK), :] = (n * _sig(n)).astype(BF)
            return carry

        lax.fori_loop(0, tm // CHUNK, chunk, 0)

    return _call(
        body, f"conv_mid_fwd{i}", (t // tm,),
        [_row(tm, D), _prev_halo(tm), _lay(i, (CW, D)), _lay(i, (1, D)), _lay(i, (1, D))],
        [_row(tm, D), _row(tm, D)], [_sds((t, D), F32), _sds((t, D), BF)], (a, a, dw, ln_g, ln_b),
        scratch=[pltpu.VMEM((tm + HALO, D), F32)], gat=gat)


def mixer_ffn_fwd(h, s, w_out, lw, bias, nf, l, wg, wu, wd, tm, gat):
    t = h.shape[0]

    def body(*refs):
        if bias is None:
            h_ref, s_ref, w_ref, nf_ref, wg_ref, wu_ref, wd_ref, h1_ref, u_ref, g_ref, up_ref, hid_ref, h2_ref = refs
            y = 0.0
        else:
            h_ref, s_ref, w_ref, b_ref, nf_ref, wg_ref, wu_ref, wd_ref, h1_ref, u_ref, g_ref, up_ref, hid_ref, h2_ref = refs
            y = b_ref[...]
        h1 = h_ref[...] + (jnp.dot(s_ref[...], w_ref[...], preferred_element_type=F32) + y)
        h1_ref[...] = h1
        ub = (h1 * _rstd(h1) * nf_ref[...]).astype(BF)
        u_ref[...] = ub
        g = _dot_nt(ub, wg_ref[...])
        up = _dot_nt(ub, wu_ref[...])
        g_ref[...] = g.astype(BF)
        up_ref[...] = up.astype(BF)
        hid = (g * _sig(g) * up).astype(BF)
        hid_ref[...] = hid
        h2_ref[...] = h1 + jnp.dot(hid, wd_ref[...], preferred_element_type=F32)

    ins = [h, s, w_out] + ([] if bias is None else [bias]) + [nf, wg, wu, wd]
    specs = ([_row(tm, D), _row(tm, D), _res((D, D))] + ([] if bias is None else [_lay(lw, (1, D))])
             + [_lay(l, (1, D)), _res((DFF, D)), _res((DFF, D)), _res((DFF, D))])
    return _call(
        body, f"mixer_ffn_fwd{l}", (t // tm,), specs,
        [_row(tm, D), _row(tm, D), _row(tm, DFF), _row(tm, DFF), _row(tm, DFF), _row(tm, D)],
        [_sds((t, D), F32), _sds((t, D), BF), _sds((t, DFF), BF), _sds((t, DFF), BF), _sds((t, DFF), BF), _sds((t, D), F32)],
        ins, gat=gat)


def _seg_rms(x, g, nseg):
    outs = []
    for s in range(nseg):
        xs = x[:, HD * s:HD * s + HD]
        outs.append(xs * _rstd(xs) * g)
    return jnp.concatenate(outs, axis=1)


def kv_fwd(h, kvn, w_kv, kng, tm):
    t = h.shape[0]

    def body(h_ref, g_ref, w_ref, kg_ref, kn_ref, kv_ref, k_ref, v_ref):
        x = h_ref[...]
        kn = (x * _rstd(x) * g_ref[...]).astype(BF)
        kn_ref[...] = kn
        kv = jnp.dot(kn, w_ref[...], preferred_element_type=F32)
        kv_ref[...] = kv
        k_ref[...] = _seg_rms(kv[:, :KVD], kg_ref[...], NKV).astype(BF)
        v_ref[...] = kv[:, KVD:].astype(BF)

    return pl.pallas_call(
        body, name="kv_fwd", grid=(t // tm,),
        in_specs=[_row(tm, D), _res((1, D)), _res((D, 2 * KVD)), _res((1, HD))],
        out_specs=[_row(tm, D), _row(tm, 2 * KVD), _row(tm, KVD), _row(tm, KVD)],
        out_shape=[_sds((t, D), BF), _sds((t, 2 * KVD), F32), _sds((t, KVD), BF), _sds((t, KVD), BF)],
        compiler_params=_cp(1),
    )(h, kvn, w_kv, kng)


def q_fwd(h, nm, l, w_q, j, tm):
    t = h.shape[0]

    def body(h_ref, g_ref, w_ref, u_ref, q_ref):
        x = h_ref[...]
        ub = (x * _rstd(x) * g_ref[...]).astype(BF)
        u_ref[...] = ub
        q_ref[...] = jnp.dot(ub, w_ref[...], preferred_element_type=F32)

    return pl.pallas_call(
        body, name=f"q_fwd{j}", grid=(t // tm,),
        in_specs=[_row(tm, D), _lay(l, (1, D)), _res((D, D))],
        out_specs=[_row(tm, D), _row(tm, D)], out_shape=[_sds((t, D), BF), _sds((t, D), F32)],
        compiler_params=_cp(1),
    )(h, nm, w_q)


RQ = NH // NKV


def _attn_specs(nb, lp):
    cur = lambda c: pl.BlockSpec((QB, c), lambda b, n: (b * nb + n, 0))
    seq = pl.BlockSpec((None, lp, KVD), lambda b, n: (b, 0, 0))
    seq_t = pl.BlockSpec((None, KVD, lp), lambda b, n: (b, 0, 0))
    return cur, seq, seq_t


NKEYS = 2 * QB + NMETA


def _attn_mask(n, start):
    shape = (RQ * QB, NKEYS)
    qpos = n * QB + (lax.broadcasted_iota(jnp.int32, shape, 0) & (QB - 1))
    col = lax.broadcasted_iota(jnp.int32, shape, 1)
    in_band = col < 2 * QB
    kpos = jnp.where(in_band, start + col, col - 2 * QB)
    return (kpos <= qpos) & ((col >= 2 * QB) | ((qpos - kpos < QB) & (kpos >= NMETA)))


def _keys(ref, band, gs):
    return jnp.concatenate([ref[band, gs], ref[0:NMETA, gs]], axis=0)


def _keys_t(ref, band, gs):
    return jnp.concatenate([ref[gs, band], ref[gs, 0:NMETA]], axis=1)


def transpose_seq(a, name):
    bl, r, c = a.shape

    def body(a_ref, o_ref):
        o_ref[...] = a_ref[...].T

    return pl.pallas_call(
        body, name=name, grid=(bl,), in_specs=[pl.BlockSpec((None, r, c), lambda b: (b, 0, 0))],
        out_specs=pl.BlockSpec((None, c, r), lambda b: (b, 0, 0)), out_shape=_sds((bl, c, r), a.dtype), compiler_params=_cp(1),
    )(a)


def sum_transposed(a0, a1):
    bl, c, r = a0.shape

    def body(a0_ref, a1_ref, o_ref):
        o_ref[...] = (a0_ref[...] + a1_ref[...]).T

    spec = pl.BlockSpec((None, c, r), lambda b: (b, 0, 0))
    return pl.pallas_call(
        body, name="sum_transposed", grid=(bl,), in_specs=[spec, spec],
        out_specs=pl.BlockSpec((r, c), lambda b: (b, 0)), out_shape=_sds((bl * r, c), a0.dtype), compiler_params=_cp(1),
    )(a0, a1)


def _stack_heads(ref, g, fn):
    return jnp.concatenate([fn(ref[:, HD * (g * RQ + r):HD * (g * RQ + r) + HD]) for r in range(RQ)], axis=0)


def _stack_cols(ref, g):
    return jnp.concatenate([ref[:, g * RQ + r:g * RQ + r + 1] for r in range(RQ)], axis=0)


def _stack_sinks(sk_ref, g):
    return jnp.concatenate([jnp.broadcast_to(sk_ref[:, g * RQ + r:g * RQ + r + 1], (QB, 1)) for r in range(RQ)], axis=0)


def attn_fwd(q, kt, v, qg, sinks, j, bl, lp, gat):
    t = q.shape[0]
    nb = lp // QB
    cur, seq, seq_t = _attn_specs(nb, lp)

    def body(q_ref, kt_ref, v_ref, qg_ref, sk_ref, o_ref, lse_ref):
        n = pl.program_id(1)
        start = pl.multiple_of(jnp.maximum(n - 1, 0) * QB, QB)
        mask = _attn_mask(n, start)
        band = pl.ds(start, 2 * QB)
        lane = lax.broadcasted_iota(jnp.int32, (QB, NH), 1)
        ones = jnp.ones((NKEYS, HD), BF)
        lse = jnp.zeros((QB, NH), F32)
        for g in range(NKV):
            gs = slice(HD * g, HD * g + HD)
            qn = _stack_heads(q_ref, g, lambda x: (x * _rstd(x) * (qg_ref[...] * SCALE)).astype(BF))
            sink = _stack_sinks(sk_ref, g)
            s = jnp.where(mask, _dot(qn, _keys_t(kt_ref, band, gs)), NEG)
            mx = jnp.maximum(jnp.max(s, -1, keepdims=True), sink)
            oa = _dot(jnp.exp(s - mx), jnp.concatenate([_keys(v_ref, band, gs), ones], axis=1))
            den = oa[:, HD:HD + 1] + jnp.exp(sink - mx)
            o = oa[:, :HD] * (1.0 / den)
            l = mx + jnp.log(den)
            for r in range(RQ):
                h = g * RQ + r
                o_ref[:, HD * h:HD * h + HD] = o[r * QB:(r + 1) * QB].astype(BF)
                lse = jnp.where(lane == h, l[r * QB:(r + 1) * QB], lse)
        lse_ref[...] = lse

    return _call(
        body, f"attn_fwd{j}", (bl, nb),
        [cur(D), seq_t, seq, pl.BlockSpec((None, 1, HD), lambda b, n: (j, 0, 0)), pl.BlockSpec((None, 1, NH), lambda b, n: (j, 0, 0))],
        [cur(D), cur(NH)], [_sds((t, D), BF), _sds((t, NH), F32)], (q, kt, v, qg, sinks), gat=gat)


def loss_fwd(h, tgt):
    bl, lp, _ = h.shape
    seq = tgt.shape[1]
    cb = 256

    def body(h_ref, t_ref, dh_ref, loss_ref):
        _init(loss_ref, (pl.program_id(0) == 0) & (pl.program_id(1) == 0))
        err = h_ref[NMETA:NMETA + seq, :] - t_ref[...]
        dh_ref[...] = jnp.zeros_like(dh_ref)
        dh_ref[NMETA:NMETA + seq, :] = err * (1.0 / D)
        loss_ref[...] += (0.5 / D) * jnp.sum(err * err)

    return pl.pallas_call(
        body, name="loss_fwd", grid=(bl, D // cb),
        in_specs=[pl.BlockSpec((None, lp, cb), lambda b, c: (b, 0, c)), pl.BlockSpec((None, seq, cb), lambda b, c: (b, 0, c))],
        out_specs=[pl.BlockSpec((None, lp, cb), lambda b, c: (b, 0, c)), pl.BlockSpec((8, 128), lambda b, c: (0, 0))],
        out_shape=[_sds((bl, lp, D), F32), _sds((8, 128), F32)],
        compiler_params=_cp(2),
    )(h, tgt)


def ffn_bwd_x(dh2, g, up, h1, nf, l, wd, wg, wu, w_o, tm, xch):
    t = dh2.shape[0]

    def body(dh2_ref, g_ref, up_ref, h1_ref, nf_ref, wd_ref, wg_ref, wu_ref, *rest):
        if w_o is None:
            dg_ref, du_ref, dh1_ref, dnf_ref = rest
        else:
            wo_ref, dg_ref, du_ref, dh1_ref, dnf_ref, do_ref = rest
        _init(dnf_ref, pl.program_id(0) == 0)
        dh2v = dh2_ref[...]
        dhid = _dot_nt(dh2v, wd_ref[...])
        gv = g_ref[...].astype(F32)
        uv = up_ref[...].astype(F32)
        sg = _sig(gv)
        dgv = (dhid * uv * (sg * (1.0 + gv * (1.0 - sg)))).astype(BF)
        duv = (dhid * (gv * sg)).astype(BF)
        dg_ref[...] = dgv
        du_ref[...] = duv
        dnorm = _dot(dgv, wg_ref[...]) + _dot(duv, wu_ref[...])
        dx, dnf = _rms_bwd(h1_ref[...], nf_ref[...], dnorm)
        dh1 = dh2v + dx
        dh1_ref[...] = dh1
        dnf_ref[...] += dnf
        if w_o is not None:
            do_ref[...] = _dot_nt(dh1, wo_ref[...]).astype(BF)

    attn = w_o is not None
    return _call(
        body, f"ffn_bwd_x{l}", (t // tm,),
        [_row(tm, D), _row(tm, DFF), _row(tm, DFF), _row(tm, D), _lay(l, (1, D)),
         _res((DFF, D)), _res((DFF, D)), _res((DFF, D))] + ([_res((D, D))] if attn else []),
        [_row(tm, DFF), _row(tm, DFF), _row(tm, D), _acc((1, D))] + ([_row(tm, D)] if attn else []),
        [_sds((t, DFF), BF), _sds((t, DFF), BF), _sds((t, D), F32), _sds((1, D), F32)] + ([_sds((t, D), BF)] if attn else []),
        (dh2, g, up, h1, nf, wd, wg, wu) + ((w_o,) if attn else ()), xch=xch)


def mm_tn(x, dy, tm, name, split=False, transposed=False, xch=()):
    t, kk = x.shape
    nn = dy.shape[1]
    n8 = nn // NDEV
    nsteps = t // tm

    def body(x_ref, dy_ref, o_ref, acc):
        i = pl.program_id(0)
        _init(acc, i == 0)
        acc[...] += _dot_tn(x_ref[...], dy_ref[...])

        @pl.when(i == nsteps - 1)
        def _():
            if split:
                for p in range(NDEV):
                    o_ref[p] = acc[:, p * n8:(p + 1) * n8].astype(BF)
            elif transposed:
                o_ref[...] = acc[...].T.astype(BF)
            else:
                o_ref[...] = acc[...].astype(BF)

    oshape = (NDEV, kk, n8) if split else ((nn, kk) if transposed else (kk, nn))
    (out,), got, _ = _call(body, name, (nsteps,), [_row(tm, kk), _row(tm, nn)], [_acc(oshape)], [_sds(oshape, BF)], (x, dy),
                           scratch=[pltpu.VMEM((kk, nn), F32)], xch=xch)
    return out, got


def proj_bwd(dy, w, h, g, lg, dh_in, tm, name, xch=()):
    t = h.shape[0]
    nn = dy.shape[1]
    wspec = _res(w.shape)
    gspec = _res((1, D)) if lg is None else _lay(lg, (1, D))

    def body(dy_ref, w_ref, h_ref, g_ref, dhin_ref, dh_ref, dg_ref):
        _init(dg_ref, pl.program_id(0) == 0)
        du = _dot_nt(dy_ref[...], w_ref[...])
        dx, dg = _rms_bwd(h_ref[...], g_ref[...], du)
        dh_ref[...] = dhin_ref[...] + dx
        dg_ref[...] += dg

    return _call(body, name, (t // tm,), [_row(tm, nn), wspec, _row(tm, D), gspec, _row(tm, D)],
                 [_row(tm, D), _acc((1, D))], [_sds((t, D), F32), _sds((1, D), F32)], (dy, w, h, g, dh_in), xch=xch)


def attn_bwd(q, k, kt, vt, do, o, lse, qg, sinks, j, bl, lp, xch):
    t = q.shape[0]
    nb = lp // QB
    cur, seq, seq_t = _attn_specs(nb, lp)

    def body(q_ref, k_ref, kt_ref, vt_ref, do_ref, o_ref, lse_ref, qg_ref, sk_ref, dq_ref, dk_ref, dv_ref, dqg_ref, dsk_ref):
        b, n = pl.program_id(0), pl.program_id(1)
        _init(dk_ref, n == 0)
        _init(dv_ref, n == 0)
        _init(dqg_ref, (b == 0) & (n == 0))
        _init(dsk_ref, (b == 0) & (n == 0))
        start = pl.multiple_of(jnp.maximum(n - 1, 0) * QB, QB)
        mask = _attn_mask(n, start)
        band = pl.ds(start, 2 * QB)
        lane = lax.broadcasted_iota(jnp.int32, (1, NH), 1)
        dqg = jnp.zeros((1, HD), F32)
        dsk = jnp.zeros((1, NH), F32)
        for g in range(NKV):
            gs = slice(HD * g, HD * g + HD)
            kk = _keys(k_ref, band, gs)
            qh = _stack_heads(q_ref, g, lambda x: x)
            rs = _rstd(qh)
            qn = (qh * rs * (qg_ref[...] * SCALE)).astype(BF)
            ls = _stack_cols(lse_ref, g)
            pr = jnp.where(mask, jnp.exp(_dot(qn, _keys_t(kt_ref, band, gs)) - ls), 0.0)
            doh = _stack_heads(do_ref, g, lambda x: x)
            delta = jnp.sum(doh.astype(F32) * _stack_heads(o_ref, g, lambda x: x).astype(F32), axis=-1, keepdims=True)
            ds = (pr * (_dot(doh, _keys_t(vt_ref, band, gs)) - delta)).astype(BF)
            dqn = _dot(ds, kk) * SCALE
            dkt = _dot_tn(qn, ds)
            dvt = _dot_tn(doh, pr)
            dk_ref[gs, band] += dkt[:, :2 * QB]
            dv_ref[gs, band] += dvt[:, :2 * QB]
            dk_ref[gs, 0:NMETA] += dkt[:, 2 * QB:]
            dv_ref[gs, 0:NMETA] += dvt[:, 2 * QB:]
            dsink = jnp.exp(_stack_sinks(sk_ref, g) - ls) * delta
            z = dqn * qg_ref[...]
            dq = rs * z - qh * (rs * rs * rs * jnp.mean(z * qh, axis=-1, keepdims=True))
            dqg = dqg + jnp.sum(dqn * qh * rs, axis=0, keepdims=True)
            for r in range(RQ):
                h = g * RQ + r
                dq_ref[:, HD * h:HD * h + HD] = dq[r * QB:(r + 1) * QB]
                dsk = dsk + jnp.where(lane == h, -jnp.sum(dsink[r * QB:(r + 1) * QB]), 0.0)
        dqg_ref[...] += dqg
        dsk_ref[...] += dsk

    return _call(
        body, f"attn_bwd{j}", (bl, nb),
        [cur(D), seq, seq_t, seq_t, cur(D), cur(D), cur(NH),
         pl.BlockSpec((None, 1, HD), lambda b, n: (j, 0, 0)), pl.BlockSpec((None, 1, NH), lambda b, n: (j, 0, 0))],
        [cur(D), seq_t, seq_t, pl.BlockSpec((1, HD), lambda b, n: (0, 0)), pl.BlockSpec((1, NH), lambda b, n: (0, 0))],
        [_sds((t, D), F32), _sds((bl, KVD, lp), F32), _sds((bl, KVD, lp), F32), _sds((1, HD), F32), _sds((1, NH), F32)],
        (q, k, kt, vt, do, o, lse, qg, sinks), xch=xch)


def kv_bwd_pre(dk, dv, kv, kng, tm):
    t = kv.shape[0]

    def body(dk_ref, dv_ref, kv_ref, g_ref, dkv_ref, dg_ref):
        _init(dg_ref, pl.program_id(0) == 0)
        dg = jnp.zeros((1, HD), F32)
        outs = []
        for s in range(NKV):
            sl = slice(HD * s, HD * s + HD)
            dx, dgs = _rms_bwd(kv_ref[:, sl], g_ref[...], dk_ref[:, sl])
            outs.append(dx)
            dg = dg + dgs
        dkv_ref[:, :KVD] = jnp.concatenate(outs, axis=1).astype(BF)
        dkv_ref[:, KVD:] = dv_ref[...].astype(BF)
        dg_ref[...] += dg

    return pl.pallas_call(
        body, name="kv_bwd_pre", grid=(t // tm,),
        in_specs=[_row(tm, KVD)] * 2 + [_row(tm, 2 * KVD), _res((1, HD))],
        out_specs=[_row(tm, 2 * KVD), _acc((1, HD))], out_shape=[_sds((t, 2 * KVD), BF), _sds((1, HD), F32)],
        compiler_params=_cp(1),
    )(dk, dv, kv, kng)


def conv_out_bwd(dh1, c, ln_g, ln_b, w_out, i, tm, xch):
    t = dh1.shape[0]

    def body(dh1_ref, c_ref, g_ref, b_ref, w_ref, dc_ref, dg_ref, db_ref, dbo_ref):
        first = pl.program_id(0) == 0
        _init(dg_ref, first)
        _init(db_ref, first)
        _init(dbo_ref, first)
        dh1v = dh1_ref[...]
        ds = _dot_nt(dh1v, w_ref[...])
        cv = c_ref[...]
        xc = cv - jnp.mean(cv, axis=-1, keepdims=True)
        rstd = lax.rsqrt(jnp.mean(xc * xc, axis=-1, keepdims=True) + EPS)
        xh = xc * rstd
        n = xh * g_ref[...] + b_ref[...]
        sg = _sig(n)
        dn = ds * (sg * (1.0 + n * (1.0 - sg)))
        dxh = dn * g_ref[...]
        dc_ref[...] = rstd * (dxh - jnp.mean(dxh, axis=-1, keepdims=True) - xh * jnp.mean(dxh * xh, axis=-1, keepdims=True))
        dg_ref[...] += jnp.sum(dn * xh, axis=0, keepdims=True)
        db_ref[...] += jnp.sum(dn, axis=0, keepdims=True)
        dbo_ref[...] += jnp.sum(dh1v, axis=0, keepdims=True)

    return _call(
        body, f"conv_out_bwd{i}", (t // tm,), [_row(tm, D), _row(tm, D), _lay(i, (1, D)), _lay(i, (1, D)), _res((D, D))],
        [_row(tm, D), _acc((1, D)), _acc((1, D)), _acc((1, D))], [_sds((t, D), F32)] + [_sds((1, D), F32)] * 3,
        (dh1, c, ln_g, ln_b, w_out), xch=xch)


def conv_mid_bwd(dc, a, big, dw, i, tm, tpb, xch):
    t = dc.shape[0]
    nsteps = t // tm

    def body(dc_ref, nxt_ref, a_ref, prv_ref, big_ref, dw_ref, da_ref, dbin_ref, ddw_ref, dce, ae, wacc, bacc):
        i_ = pl.program_id(0)
        _init(wacc, i_ == 0)
        _init(bacc, i_ == 0)
        dce[0:tm] = dc_ref[...]
        dce[tm:] = jnp.where(i_ % tpb == tpb - 1, 0.0, nxt_ref[...])
        ae[0:HALO] = jnp.where(i_ % tpb == 0, 0.0, prv_ref[...])
        ae[HALO:] = a_ref[...]

        def chunk(k, carry):
            r0 = pl.multiple_of(k * CHUNK, CHUNK)
            wdc = _shifted(dce[pl.ds(r0, 2 * CHUNK), :])
            wa = _shifted(ae[pl.ds(r0, 2 * CHUNK), :])
            dcc = wdc[0][0:CHUNK]
            da = jnp.zeros((CHUNK, D), F32)
            for j in range(CW):
                da = da + dw_ref[j:j + 1, :] * _tap(wdc, CW - 1 - j)
                wacc[j] += _fold8(dcc * _tap(wa, j + 2))
            bv = big_ref[pl.ds(r0, CHUNK), :].astype(F32)
            a1, sg = bv[:, :D], _sig(bv[:, D:])
            d1 = da * sg
            d2 = da * a1 * sg * (1.0 - sg)
            da_ref[pl.ds(r0, CHUNK), 0:D] = d1.astype(BF)
            da_ref[pl.ds(r0, CHUNK), D:2 * D] = d2.astype(BF)
            bacc[:, 0:D] += _fold8(d1)
            bacc[:, D:2 * D] += _fold8(d2)
            return carry

        lax.fori_loop(0, tm // CHUNK, chunk, 0)

        @pl.when(i_ == nsteps - 1)
        def _():
            dbin_ref[...] = jnp.sum(bacc[...], axis=0, keepdims=True)
            ddw_ref[...] = jnp.sum(wacc[...], axis=1)

    return _call(
        body, f"conv_mid_bwd{i}", (nsteps,),
        [_row(tm, D), _next_halo(tm, t), _row(tm, D), _prev_halo(tm), _row(tm, 2 * D), _lay(i, (CW, D))],
        [_row(tm, 2 * D), _acc((1, 2 * D)), _acc((CW + 1, D))],
        [_sds((t, 2 * D), BF), _sds((1, 2 * D), F32), _sds((CW + 1, D), F32)],
        (dc, dc, a, a, big, dw),
        scratch=[pltpu.VMEM((tm + HALO, D), F32), pltpu.VMEM((tm + HALO, D), F32),
                 pltpu.VMEM((CW + 1, 8, D), F32), pltpu.VMEM((8, 2 * D), F32)], xch=xch)


def input_grads(dh0, seq):
    bl, lp, _ = dh0.shape
    cb = 256

    def body(dh_ref, gx_ref, gm_ref):
        _init(gm_ref, pl.program_id(1) == 0)
        gx_ref[...] = dh_ref[NMETA:NMETA + seq, :]
        gm_ref[...] += dh_ref[0:NMETA, :]

    return pl.pallas_call(
        body, name="input_grads", grid=(D // cb, bl),
        in_specs=[pl.BlockSpec((None, lp, cb), lambda c, b: (b, 0, c))],
        out_specs=[pl.BlockSpec((None, seq, cb), lambda c, b: (b, 0, c)), pl.BlockSpec((NMETA, cb), lambda c, b: (0, c))],
        out_shape=[_sds((bl, seq, D), F32), _sds((NMETA, D), F32)],
        compiler_params=_cp(2),
    )(dh0)


GATHER_PLAN = {
    "embed": [("conv_w_in", 0), ("conv_w_out", 0)],
    "conv_mid_fwd0": [("ffn_w_gate", 0), ("ffn_w_up", 0), ("ffn_w_down", 0)],
    "mixer_ffn_fwd0": [("conv_w_in", 1), ("conv_w_out", 1), ("ffn_w_gate", 1)],
    "conv_mid_fwd1": [("ffn_w_up", 1), ("ffn_w_down", 1), ("w_kv", 0), ("w_q", 0)],
    "mixer_ffn_fwd1": [("w_o", 0), ("ffn_w_down", 2)],
    "attn_fwd0": [("ffn_w_gate", 2), ("ffn_w_up", 2), ("w_q", 1), ("w_o", 1)],
    "attn_fwd1": [("ffn_w_gate", 3), ("ffn_w_up", 3), ("ffn_w_down", 3)],
}
EXCHANGE_PLAN = {
    "attn_bwd1": [("ffn_w_down3", ALL), ("ffn_w_gate3", ALL), ("ffn_w_up3", ALL)],
    "dw_down2": [("w_o1", ALL)],
    "ffn_bwd_x2": [("w_q1", ALL)],
    "attn_bwd0": [("ffn_w_down2", ALL), ("ffn_w_gate2", ALL), ("ffn_w_up2", ALL)],
    "dw_down1": [("w_o0", ALL), ("w_q0", H1)],
    "ffn_bwd_x1": [("w_q0", H2), ("w_kv", ALL)],
    "dw_gate1": [("ffn_w_down1", H1)],
    "dw_up1": [("ffn_w_down1", H2)],
    "conv_mid_bwd1": [("ffn_w_gate1", ALL), ("ffn_w_up1", H1)],
    "conv_in_bwd1": [("ffn_w_up1", H2)],
    "dw_down0": [("conv_w_out1", ALL)],
    "ffn_bwd_x0": [("conv_w_in1", ALL)],
    "dw_gate0": [("ffn_w_down0", H1)],
    "dw_up0": [("ffn_w_down0", H2)],
    "conv_out_bwd0": [("ffn_w_gate0", H1)],
    "conv_mid_bwd0": [("ffn_w_gate0", H2), ("ffn_w_up0", H1), ("conv_w_out0", ALL)],
    "dw_conv_in0": [("ffn_w_up0", H2)],
    "conv_in_bwd0": [("conv_w_in0", H1)],
    "tail": [("conv_w_in0", H2)],
}
BIG = {"conv_w_in": "pieces", "conv_w_out": "rows", "w_kv": "rows", "w_q": "rows", "w_o": "rows",
       "ffn_w_gate": "rows", "ffn_w_up": "rows", "ffn_w_down": "rows"}
EXCHANGE_KIND = BIG
TRANSPOSED = ("ffn_w_gate", "ffn_w_up")


def gathered_matrix(name, layer, blocks8):
    if BIG[name] == "rows":
        return blocks8.reshape(NDEV * blocks8.shape[1], blocks8.shape[2])
    return join_columns(blocks8, f"join_{name}{layer}")


def local_step(x, tgt, meta8, w, shards):
    bl, seq, _ = x.shape
    lp = -(-(NMETA + seq) // QB) * QB
    tpb = 4
    tm = lp // tpb
    t = bl * lp
    na = 2
    flat = lambda a: a.reshape(t, D)
    mats = {}

    def riders(carrier):
        return [shards[key] for key in GATHER_PLAN[carrier]]

    def landed(carrier, blocks):
        for key, b8 in zip(GATHER_PLAN[carrier], blocks):
            mats[key] = gathered_matrix(*key, b8)

    h0, got = embed(x, meta8, lp, riders("embed"))
    landed("embed", got)
    h = flat(h0)
    saved = []
    kvs = None
    for l in range(4):
        rec = {"h": h}
        if l < na:
            rec["u"], rec["big"], rec["a"] = conv_in_fwd(h, w["norm_mix"], l, mats["conv_w_in", l], w["conv_b_in"], l, tm)
            name = f"conv_mid_fwd{l}"
            (rec["c"], rec["s"]), _, got = conv_mid_fwd(rec["a"], w["conv_dw"], w["conv_ln_g"], w["conv_ln_b"], l, tm, tpb,
                                                         riders(name))
            landed(name, got)
            mixed, w_out, lw, bias = rec["s"], mats["conv_w_out", l], l, w["conv_b_out"]
        else:
            j = l - na
            if kvs is None:
                kvs = dict(zip(("kn", "kv", "k", "v"), kv_fwd(h, w["kv_norm"], mats["w_kv", 0], w["k_norm"], tm)))
                kvs["h"] = h
                kvs["k3"], kvs["v3"] = kvs["k"].reshape(bl, lp, KVD), kvs["v"].reshape(bl, lp, KVD)
                kvs["kt"], kvs["vt"] = transpose_seq(kvs["k3"], "transpose_k"), transpose_seq(kvs["v3"], "transpose_v")
            rec["u"], rec["q"] = q_fwd(h, w["norm_mix"], l, mats["w_q", j], j, tm)
            name = f"attn_fwd{j}"
            (rec["o"], rec["lse"]), _, got = attn_fwd(rec["q"], kvs["kt"], kvs["v3"], w["q_norm"], w["attn_sinks"], j, bl, lp,
                                                      riders(name) if name in GATHER_PLAN else [])
            if name in GATHER_PLAN:
                landed(name, got)
            mixed, w_out, lw, bias = rec["o"], mats["w_o", j], j, None
        name = f"mixer_ffn_fwd{l}"
        (rec["h1"], rec["u2"], rec["g"], rec["up"], rec["hid"], h), _, got = mixer_ffn_fwd(
            h, mixed, w_out, lw, bias, w["norm_ffn"], l, mats["ffn_w_gate", l], mats["ffn_w_up", l], mats["ffn_w_down", l], tm // 2,
            riders(name) if name in GATHER_PLAN else [])
        if name in GATHER_PLAN:
            landed(name, got)
        saved.append(rec)

    dh3, loss_blk = loss_fwd(h.reshape(bl, lp, D), tgt)
    dh = flat(dh3)

    big, small, arrived = {}, {}, {}
    dks, dvs = [], []

    def ride(kernel_name):
        return [(big[nm], EXCHANGE_KIND[nm.rstrip("0123456789")], ks) for nm, ks in EXCHANGE_PLAN.get(kernel_name, [])]

    def landed_x(kernel_name, arrivals):
        for (nm, _), got in zip(EXCHANGE_PLAN.get(kernel_name, []), arrivals):
            arrived.setdefault(nm, []).append(got)

    def dw(name, grad, x, dy, **kw):
        big[grad], got = mm_tn(x, dy, 2 * tm, name, xch=ride(name), **kw)
        landed_x(name, got)

    for l in reversed(range(4)):
        rec = saved[l]
        dw(f"dw_down{l}", f"ffn_w_down{l}", rec["hid"], dh)
        name = f"ffn_bwd_x{l}"
        outs, got, _ = ffn_bwd_x(
            dh, rec["g"], rec["up"], rec["h1"], w["norm_ffn"], l, mats["ffn_w_down", l], mats["ffn_w_gate", l], mats["ffn_w_up", l],
            mats["w_o", l - na] if l >= na else None, tm // 2, ride(name))
        landed_x(name, got)
        dg, du, dh1, small[f"norm_ffn{l}"] = outs[:4]
        dw(f"dw_gate{l}", f"ffn_w_gate{l}", rec["u2"], dg, transposed=True)
        dw(f"dw_up{l}", f"ffn_w_up{l}", rec["u2"], du, transposed=True)
        if l >= na:
            j = l - na
            dw(f"dw_o{j}", f"w_o{j}", rec["o"], dh1)
            name = f"attn_bwd{j}"
            (dq, dk, dv, small[f"q_norm{j}"], small[f"attn_sinks{j}"]), got, _ = attn_bwd(
                rec["q"], kvs["k3"], kvs["kt"], kvs["vt"], outs[4], rec["o"], rec["lse"], w["q_norm"], w["attn_sinks"], j, bl, lp,
                ride(name))
            landed_x(name, got)
            dks.append(dk)
            dvs.append(dv)
            dw(f"dw_q{j}", f"w_q{j}", rec["u"], dq)
            dh, small[f"norm_mix{l}"] = proj_bwd(dq, mats["w_q", j], rec["h"], w["norm_mix"], l, dh1, tm, f"q_bwd{j}")[0]
            if l == na:
                dkv, small["k_norm"] = kv_bwd_pre(sum_transposed(*dks), sum_transposed(*dvs), kvs["kv"], w["k_norm"], tm)
                dw("dw_kv", "w_kv", kvs["kn"], dkv)
                dh, small["kv_norm"] = proj_bwd(dkv, mats["w_kv", 0], kvs["h"], w["kv_norm"], None, dh, tm, "kv_bwd")[0]
        else:
            name = f"conv_out_bwd{l}"
            (dc, small[f"conv_ln_g{l}"], small[f"conv_ln_b{l}"], small[f"conv_b_out{l}"]), got, _ = conv_out_bwd(
                dh1, rec["c"], w["conv_ln_g"], w["conv_ln_b"], mats["conv_w_out", l], l, tm, ride(name))
            landed_x(name, got)
            dw(f"dw_conv_out{l}", f"conv_w_out{l}", rec["s"], dh1)
            name = f"conv_mid_bwd{l}"
            (da, small[f"conv_b_in{l}"], small[f"conv_dw{l}"]), got, _ = conv_mid_bwd(
                dc, rec["a"], rec["big"], w["conv_dw"], l, tm, tpb, ride(name))
            landed_x(name, got)
            dw(f"dw_conv_in{l}", f"conv_w_in{l}", rec["u"], da, split=True)
            name = f"conv_in_bwd{l}"
            (dh, small[f"norm_mix{l}"]), got, _ = proj_bwd(da, mats["conv_w_in", l], rec["h"], w["norm_mix"], l, dh1, tm, name,
                                                           ride(name))
            landed_x(name, got)
    grad_x, small["meta_tokens"] = input_grads(dh.reshape(bl, lp, D), seq)
    return loss_blk, grad_x, big, arrived, small


def all_gather_blocks(blocks):
    n = len(blocks)

    def body(*refs):
        srcs, outs, sems = refs[:n], refs[n:2 * n], refs[2 * n:]
        _gat_start(srcs, outs, sems)
        _gat_forward(srcs, outs, sems)
        _gat_wait(srcs, outs, sems)

    any_spec = pl.BlockSpec(memory_space=pl.ANY)
    return pl.pallas_call(
        body, name="all_gather_blocks", out_shape=[_sds((NDEV,) + tuple(a.shape), a.dtype) for a in blocks],
        in_specs=[any_spec] * n, out_specs=[any_spec] * n, scratch_shapes=_xch_scratch(n),
    )(*blocks)


def cast_bf16(ws):
    n = len(ws)
    counts = [1 if x.ndim == 2 else x.shape[0] for x in ws]

    def body(*refs):
        outs = iter(refs[n:])
        for a in range(n):
            for l in range(counts[a]):
                next(outs)[...] = (refs[a][...] if ws[a].ndim == 2 else refs[a][l]).astype(BF)

    flat = pl.pallas_call(
        body, name="cast_bf16", out_shape=[_sds(x.shape[-2:], BF) for x, k in zip(ws, counts) for _ in range(k)],
        compiler_params=pltpu.CompilerParams(vmem_limit_bytes=VMEM_LIMIT),
    )(*ws)
    it = iter(flat)
    return [[next(it) for _ in range(k)] for k in counts]


def join_columns(w8, name):
    _, kk, n8 = w8.shape

    def body(x_ref, o_ref):
        o_ref[...] = jnp.concatenate([x_ref[p] for p in range(NDEV)], axis=1)

    return pl.pallas_call(body, name=name, out_shape=_sds((kk, NDEV * n8), w8.dtype),
                          compiler_params=pltpu.CompilerParams(vmem_limit_bytes=VMEM_LIMIT))(w8)


def _adamw_math(w, m, v, g):
    m2 = B1 * m + (1.0 - B1) * g
    v2 = B2 * v + (1.0 - B2) * (g * g)
    mh = m2 / (1.0 - B1 ** STEP)
    vh = v2 / (1.0 - B2 ** STEP)
    return -LR * (mh / (jnp.sqrt(vh) + AEPS) + WD * w), m2, v2


def adamw_big(w, m, v, parts, name, xch=(), gat=()):
    lyr, r, c = w.shape
    by_cols = c >= 512
    blk = (lyr, r, 256) if by_cols else (lyr, 256 if r % 256 == 0 else r, c)
    imap = (lambda i: (0, 0, i)) if by_cols else (lambda i: (0, i, 0))
    counts = [len(per_layer) for per_layer in parts]

    def body(w_ref, m_ref, v_ref, *rest):
        p_refs, (g_ref, d_ref, m2_ref, v2_ref) = iter(rest[:sum(counts)]), rest[sum(counts):]
        for l in range(lyr):
            g = None
            for _ in range(counts[l]):
                ref = next(p_refs)
                for q in range(ref.shape[0]):
                    g = ref[q].astype(F32) if g is None else g + ref[q].astype(F32)
            g_ref[l] = g
            d_ref[l], m2_ref[l], v2_ref[l] = _adamw_math(w_ref[l], m_ref[l], v_ref[l], g)

    spec = pl.BlockSpec(blk, imap)
    flat = [a for per_layer in parts for a in per_layer]
    pspecs = [pl.BlockSpec((a.shape[0],) + blk[1:], imap) for a in flat]
    return _call(body, name, ((c // 256) if by_cols else (r // blk[1]),), [spec, spec, spec] + pspecs,
                 [spec] * 4, [_sds((lyr, r, c), F32)] * 4, (w, m, v, *flat), xch=xch, gat=gat)


SMALL_ROWS = 104
REPLICATED = {"norm_mix": (0, 4, D), "norm_ffn": (4, 4, D), "kv_norm": (8, 1, D), "k_norm": (9, 1, HD), "q_norm": (10, 2, HD),
              "attn_sinks": (12, 2, NH)}
LOSS_ROW = 14
SHARDED = {"meta_tokens": (16, NMETA), "conv_b_in": (32, 4), "conv_dw": (36, 2 * CW), "conv_ln_g": (98, 2), "conv_ln_b": (100, 2),
           "conv_b_out": (102, 2)}


def pack_small(gs, loss_blk):
    order = ([f"norm_mix{l}" for l in range(4)] + [f"norm_ffn{l}" for l in range(4)] + ["kv_norm", "k_norm", "q_norm0", "q_norm1",
             "attn_sinks0", "attn_sinks1", "meta_tokens", "conv_b_in0", "conv_b_in1", "conv_dw0", "conv_dw1", "conv_ln_g0",
             "conv_ln_g1", "conv_ln_b0", "conv_ln_b1", "conv_b_out0", "conv_b_out1"])

    def body(*refs):
        r = dict(zip(order, refs))
        loss_ref, o_ref = refs[len(order)], refs[len(order) + 1]
        o_ref[...] = jnp.zeros_like(o_ref)
        for l in range(4):
            o_ref[l:l + 1, :] = r[f"norm_mix{l}"][...]
            o_ref[4 + l:5 + l, :] = r[f"norm_ffn{l}"][...]
        o_ref[8:9, :] = r["kv_norm"][...]
        o_ref[9:10, 0:HD] = r["k_norm"][...]
        for j in range(2):
            o_ref[10 + j:11 + j, 0:HD] = r[f"q_norm{j}"][...]
            o_ref[12 + j:13 + j, 0:NH] = r[f"attn_sinks{j}"][...]
            o_ref[32 + 2 * j:33 + 2 * j, :] = r[f"conv_b_in{j}"][:, 0:D]
            o_ref[33 + 2 * j:34 + 2 * j, :] = r[f"conv_b_in{j}"][:, D:2 * D]
            o_ref[36 + CW * j:36 + CW * (j + 1), :] = r[f"conv_dw{j}"][0:CW, :]
            o_ref[98 + j:99 + j, :] = r[f"conv_ln_g{j}"][...]
            o_ref[100 + j:101 + j, :] = r[f"conv_ln_b{j}"][...]
            o_ref[102 + j:103 + j, :] = r[f"conv_b_out{j}"][...]
        o_ref[LOSS_ROW:LOSS_ROW + 1, 0:1] = loss_ref[0:1, 0:1]
        o_ref[16:16 + NMETA, :] = r["meta_tokens"][...]

    return pl.pallas_call(body, name="pack_small", out_shape=_sds((SMALL_ROWS, D), F32))(*[gs[k] for k in order], loss_blk)


def adamw_small(g8, wts, mom, var):
    names = list(REPLICATED) + list(SHARDED)
    shape2 = {"kv_norm": (1, D), "k_norm": (1, HD)}
    ins = [a[k].reshape(shape2.get(k, a[k].shape)) for a in (wts, mom, var) for k in names]
    n = len(names)

    def body(*refs):
        g8_ref, w_refs, m_refs, v_refs = refs[0], refs[1:1 + n], refs[1 + n:1 + 2 * n], refs[1 + 2 * n:1 + 3 * n]
        loss_ref, outs, red_ref = refs[1 + 3 * n], refs[2 + 3 * n:-1], refs[-1]
        me = _my_index()
        acc = g8_ref[0]
        for q in range(1, NDEV):
            acc = acc + g8_ref[q]
        red_ref[...] = acc
        loss_ref[...] = red_ref[LOSS_ROW:LOSS_ROW + 1, 0:1]

        def mine(rows, width):
            acc = jnp.zeros((rows.stop - rows.start, width), F32)
            for p_ in range(NDEV):
                acc = acc + jnp.where(me == p_, red_ref[rows, p_ * width:(p_ + 1) * width], 0.0)
            return acc

        for i, k in enumerate(names):
            if k in REPLICATED:
                r0, nr, width = REPLICATED[k]
                g = red_ref[r0:r0 + nr, 0:width]
            elif k == "conv_b_in":
                half = D // (2 * D // NDEV)
                acc = jnp.zeros((2, 2 * D // NDEV), F32)
                for p_ in range(NDEV):
                    c0 = (p_ % half) * (2 * D // NDEV)
                    part = jnp.concatenate([red_ref[32 + 2 * j + p_ // half:33 + 2 * j + p_ // half, c0:c0 + 2 * D // NDEV]
                                            for j in range(2)], axis=0)
                    acc = acc + jnp.where(me == p_, part, 0.0)
                g = acc
            else:
                r0, nr = SHARDED[k]
                g = mine(slice(r0, r0 + nr), D // NDEV)
            w_, m_, v_ = w_refs[i], m_refs[i], v_refs[i]
            g_out, d_out, m_out, v_out = outs[4 * i:4 * i + 4]
            if k == "conv_dw":
                for j in range(2):
                    gj = g[CW * j:CW * (j + 1)]
                    g_out[j] = gj
                    d_out[j], m_out[j], v_out[j] = _adamw_math(w_[j], m_[j], v_[j], gj)
            else:
                g_out[...] = g
                d_out[...], m_out[...], v_out[...] = _adamw_math(w_[...], m_[...], v_[...], g)

    out_shape = [_sds((1, 1), F32)] + [_sds(ins[i].shape, F32) for i in range(n) for _ in range(4)]
    res = pl.pallas_call(body, name="adamw_small", out_shape=out_shape, scratch_shapes=[pltpu.VMEM((SMALL_ROWS, D), F32)])(g8, *ins)
    out = {k: tuple(o.reshape(wts[k].shape) for o in res[1 + 4 * i:5 + 4 * i]) for i, k in enumerate(names)}
    return res[0], out


NAMES = ["meta_tokens", "norm_mix", "norm_ffn", "conv_w_in", "conv_b_in", "conv_dw", "conv_ln_g", "conv_ln_b", "conv_w_out",
         "conv_b_out", "kv_norm", "w_kv", "k_norm", "w_q", "q_norm", "attn_sinks", "w_o", "ffn_w_gate", "ffn_w_up", "ffn_w_down"]


def kernel(x, meta_tokens, norm_mix, norm_ffn, conv_w_in, conv_b_in, conv_dw, conv_ln_g, conv_ln_b, conv_w_out, conv_b_out, kv_norm, w_kv, k_norm, w_q, q_norm, attn_sinks, w_o, ffn_w_gate, ffn_w_up, ffn_w_down, loss_target, m_meta_tokens, m_norm_mix, m_norm_ffn, m_conv_w_in, m_conv_b_in, m_conv_dw, m_conv_ln_g, m_conv_ln_b, m_conv_w_out, m_conv_b_out, m_kv_norm, m_w_kv, m_k_norm, m_w_q, m_q_norm, m_attn_sinks, m_w_o, m_ffn_w_gate, m_ffn_w_up, m_ffn_w_down, v_meta_tokens, v_norm_mix, v_norm_ffn, v_conv_w_in, v_conv_b_in, v_conv_dw, v_conv_ln_g, v_conv_ln_b, v_conv_w_out, v_conv_b_out, v_kv_norm, v_w_kv, v_k_norm, v_w_q, v_q_norm, v_attn_sinks, v_w_o, v_ffn_w_gate, v_ffn_w_up, v_ffn_w_down):
    wts = dict(zip(NAMES, (meta_tokens, norm_mix, norm_ffn, conv_w_in, conv_b_in, conv_dw, conv_ln_g, conv_ln_b, conv_w_out,
                           conv_b_out, kv_norm, w_kv, k_norm, w_q, q_norm, attn_sinks, w_o, ffn_w_gate, ffn_w_up, ffn_w_down)))
    mom = dict(zip(NAMES, (m_meta_tokens, m_norm_mix, m_norm_ffn, m_conv_w_in, m_conv_b_in, m_conv_dw, m_conv_ln_g, m_conv_ln_b,
                           m_conv_w_out, m_conv_b_out, m_kv_norm, m_w_kv, m_k_norm, m_w_q, m_q_norm, m_attn_sinks, m_w_o,
                           m_ffn_w_gate, m_ffn_w_up, m_ffn_w_down)))
    var = dict(zip(NAMES, (v_meta_tokens, v_norm_mix, v_norm_ffn, v_conv_w_in, v_conv_b_in, v_conv_dw, v_conv_ln_g, v_conv_ln_b,
                           v_conv_w_out, v_conv_b_out, v_kv_norm, v_w_kv, v_k_norm, v_w_q, v_q_norm, v_attn_sinks, v_w_o,
                           v_ffn_w_gate, v_ffn_w_up, v_ffn_w_down)))
    for k in TRANSPOSED:
        wts[k], mom[k], var[k] = (jnp.swapaxes(a, 1, 2) for a in (wts[k], mom[k], var[k]))

    big_names = list(BIG)
    layers = cast_bf16([wts[k] for k in big_names])
    shards = {(k, l): blk for k, per_layer in zip(big_names, layers) for l, blk in enumerate(per_layer)}
    vec_names = ["meta_tokens", "conv_b_in", "conv_dw", "conv_ln_g", "conv_ln_b", "conv_b_out"]
    full = dict(zip(vec_names, all_gather_blocks([wts[k] for k in vec_names])))
    join_vec = lambda a: jnp.moveaxis(a, 0, -2).reshape(a.shape[1:-1] + (NDEV * a.shape[-1],))
    w = {}
    w["conv_b_in"] = join_vec(full["conv_b_in"]).reshape(2, 1, 2 * D)
    w["conv_dw"] = join_vec(full["conv_dw"])
    for k in ("conv_ln_g", "conv_ln_b", "conv_b_out"):
        w[k] = join_vec(full[k]).reshape(2, 1, D)
    w["norm_mix"] = norm_mix.reshape(4, 1, D)
    w["norm_ffn"] = norm_ffn.reshape(4, 1, D)
    w["kv_norm"] = kv_norm.reshape(1, D)
    w["k_norm"] = k_norm.reshape(1, HD)
    w["q_norm"] = q_norm.reshape(2, 1, HD)
    w["attn_sinks"] = attn_sinks.reshape(2, 1, NH)

    loss_blk, grad_x, gbig, arrived, gs = local_step(x, loss_target, full["meta_tokens"], w, shards)

    packed = pack_small(gs, loss_blk)

    grads, delta, new_m, new_v = {}, {}, {}, {}
    tail = EXCHANGE_PLAN["tail"]
    waiting = {nm.rstrip("0123456789") for nm, _ in tail}
    order = sorted([k for k in big_names if k not in waiting], key=lambda k: -wts[k].size) + [k for k in big_names if k in waiting]
    small8 = None
    for pos, k in enumerate(order):
        flat2 = wts[k].ndim == 2
        as3 = (lambda a: a[None]) if flat2 else (lambda a: a)
        riders = tail if pos == 0 else []
        gat = [packed] if pos == 1 else []
        parts = [arrived[k]] if flat2 else [arrived[f"{k}{i}"] for i in range(wts[k].shape[0])]
        outs, got_x, got_g = adamw_big(as3(wts[k]), as3(mom[k]), as3(var[k]), parts, "adamw_" + k,
                                       xch=[(gbig[nm], EXCHANGE_KIND[nm.rstrip("0123456789")], ks) for nm, ks in riders], gat=gat)
        for (nm, _), got in zip(riders, got_x):
            arrived[nm].append(got)
        if gat:
            small8 = got_g[0]
        grads[k], delta[k], new_m[k], new_v[k] = [o[0] if flat2 else (jnp.swapaxes(o, 1, 2) if k in TRANSPOSED else o) for o in outs]
    loss, small = adamw_small(small8, wts, mom, var)
    for k, (g_, d_, m_, v_) in small.items():
        grads[k], delta[k], new_m[k], new_v[k] = g_, d_, m_, v_
    return (loss.reshape(()), grad_x, *[grads[k] for k in NAMES], *[delta[k] for k in NAMES], *[new_m[k] for k in NAMES],
            *[new_v[k] for k in NAMES])
```

```python
import functools

import jax
import jax.numpy as jnp
from jax import lax
from jax.experimental import pallas as pl
from jax.experimental.pallas import tpu as pltpu

F32 = jnp.float32
BF = jnp.bfloat16

D = 1024
DFF = 2816
NH = 16
NKV = 4
HD = 64
KVD = NKV * HD
NMETA = 16
CW = 31
HALO = 32
CHUNK = 32
QB = 128
EPS = 1e-6
NEG = -1e30
NDEV = 8
SCALE = HD ** -0.5

LR, B1, B2, AEPS, WD, STEP = 0.001, 0.9, 0.999, 1e-08, 0.01, 10

VMEM_LIMIT = 56 * 2 ** 20
MESH = pl.DeviceIdType.MESH


def _cp(n):
    return pltpu.CompilerParams(dimension_semantics=("arbitrary",) * n, vmem_limit_bytes=VMEM_LIMIT)


def _row(tm, c):
    return pl.BlockSpec((tm, c), lambda i: (i, 0))


def _res(shape):
    return pl.BlockSpec(shape, lambda i: (0,) * len(shape), pipeline_mode=pl.Buffered(1))


def _lay(l, shape):
    return pl.BlockSpec((None,) + tuple(shape), lambda i: (l,) + (0,) * len(shape), pipeline_mode=pl.Buffered(1))


def _acc(shape):
    return pl.BlockSpec(shape, lambda i: (0,) * len(shape))


def _sds(shape, dt):
    return jax.ShapeDtypeStruct(tuple(shape), dt)


def _dot(a, b):
    return jnp.dot(a.astype(BF), b.astype(BF), preferred_element_type=F32)


def _dot_nt(a, b):
    return lax.dot_general(a.astype(BF), b.astype(BF), (((1,), (1,)), ((), ())), preferred_element_type=F32)


def _dot_tn(a, b):
    return lax.dot_general(a.astype(BF), b.astype(BF), (((0,), (0,)), ((), ())), preferred_element_type=F32)


def _rstd(x):
    return lax.rsqrt(jnp.mean(x * x, axis=-1, keepdims=True) + EPS)


def _rms_bwd(x, g, dy):
    r = _rstd(x)
    z = dy * g
    dx = r * z - x * (r * r * r * jnp.mean(z * x, axis=-1, keepdims=True))
    return dx, jnp.sum(dy * x * r, axis=0, keepdims=True)


def _sig(x):
    return jax.nn.sigmoid(x)


def _fold8(x):
    out = x[0:8]
    for k in range(1, x.shape[0] // 8):
        out = out + x[8 * k:8 * k + 8]
    return out


def _shifted(win):
    return [win] + [pltpu.roll(win, 2 * CHUNK - rho, 0) for rho in range(1, 8)]


def _tap(phases, o):
    return phases[o % 8][8 * (o // 8):8 * (o // 8) + CHUNK]


def _init(ref, first):
    @pl.when(first)
    def _():
        ref[...] = jnp.zeros_like(ref)


def _my_index():
    return 4 * lax.axis_index("x") + 2 * lax.axis_index("y") + lax.axis_index("c")


def _coords(idx):
    return (idx // 4, (idx // 2) % 2, idx % 2)


ALL = tuple(range(NDEV))
H1, H2 = (0, 1, 2, 4, 6), (3, 5, 7)


def _xch_shapes(xch):
    return [_sds((len(ks),) + ((a.shape[0] // NDEV, a.shape[1]) if k == "rows" else tuple(a.shape[1:])), a.dtype) for a, k, ks in xch]


def _xch_scratch(n):
    return [pltpu.SemaphoreType.DMA((n, NDEV)), pltpu.SemaphoreType.DMA((n, NDEV)), pltpu.SemaphoreType.DMA((n,))]


def _xch_copies(meta, srcs, outs, sems, arrivals):
    send_sems, recv_sems, local_sems = sems
    me = _my_index()

    def piece(a, p):
        if meta[a][0] == "rows":
            r = srcs[a].shape[0] // NDEV
            return srcs[a].at[pl.ds(p * r, r), :]
        return srcs[a].at[p]

    def remote(a, i, k, src):
        return pltpu.make_async_remote_copy(
            src_ref=src, dst_ref=outs[a].at[i], send_sem=send_sems.at[a, k], recv_sem=recv_sems.at[a, k],
            device_id=_coords(me ^ k), device_id_type=MESH)

    local, sends, recvs = [], [], []
    for a, (_, ks) in enumerate(meta):
        for i, k in enumerate(ks):
            if k == 0:
                local.append(pltpu.make_async_copy(piece(a, me), outs[a].at[i], local_sems.at[a]))
            else:
                sends.append(remote(a, i, k, piece(a, me ^ k)))
                if arrivals:
                    recvs.append(remote(a, i, k, piece(a, me)))
    return local, sends, recvs


def _xch_start(meta, srcs, outs, sems):
    local, sends, _ = _xch_copies(meta, srcs, outs, sems, False)
    for cp in local + sends:
        cp.start()


def _xch_wait(meta, srcs, outs, sems):
    local, sends, recvs = _xch_copies(meta, srcs, outs, sems, True)
    for cp in recvs:
        cp.wait_recv()
    for cp in sends:
        cp.wait_send()
    for cp in local:
        cp.wait()


def _gat_copies(srcs, outs, sems):
    send_sems, recv_sems, local_sems = sems
    x, y, c = lax.axis_index("x"), lax.axis_index("y"), lax.axis_index("c")
    me, sibling = (x, y, c), (x, y, 1 - c)
    chips = [(1 - x, y), (x, 1 - y), (1 - x, 1 - y)]

    def copy(a, k, owner, to, from_block=False):
        slot = outs[a].at[4 * owner[0] + 2 * owner[1] + owner[2]]
        return pltpu.make_async_remote_copy(
            src_ref=srcs[a] if from_block else slot, dst_ref=slot, send_sem=send_sems.at[a, k], recv_sem=recv_sems.at[a, k],
            device_id=to, device_id_type=MESH)

    n = len(srcs)
    local = lambda: [pltpu.make_async_copy(srcs[a], outs[a].at[4 * x + 2 * y + c], local_sems.at[a]) for a in range(n)]
    first = lambda: [cp for a in range(n) for cp in
                     [copy(a, 0, me, sibling, True)] + [copy(a, 1 + j, me, (*chip, c), True) for j, chip in enumerate(chips)]]
    landed = lambda: [copy(a, 1 + j, (*chip, c), me) for a in range(n) for j, chip in enumerate(chips)]
    passed = lambda: [copy(a, 4 + j, (*chip, c), sibling) for a in range(n) for j, chip in enumerate(chips)]
    final = lambda: [cp for a in range(n) for cp in
                     [copy(a, 0, sibling, me)] + [copy(a, 4 + j, (*chip, 1 - c), me) for j, chip in enumerate(chips)]]
    return local, first, landed, passed, final


def _gat_start(srcs, outs, sems):
    local, first, _, _, _ = _gat_copies(srcs, outs, sems)
    for cp in local() + first():
        cp.start()


def _gat_forward(srcs, outs, sems):
    _, _, landed, passed, _ = _gat_copies(srcs, outs, sems)
    for got, on in zip(landed(), passed()):
        got.wait_recv()
        on.start()


def _gat_wait(srcs, outs, sems):
    local, first, _, passed, final = _gat_copies(srcs, outs, sems)
    for cp in final():
        cp.wait_recv()
    for cp in first() + passed():
        cp.wait_send()
    for cp in local():
        cp.wait()


def _call(body, name, grid, in_specs, out_specs, out_shape, args, scratch=(), xch=(), gat=()):
    n_in, n_out, n_x, n_g, n_s = len(in_specs), len(out_specs), len(xch), len(gat), len(scratch)
    kinds = [(k, ks) for _, k, ks in xch]
    total = 1
    for g in grid:
        total *= g

    def wrapped(*refs):
        ins, refs = refs[:n_in], refs[n_in:]
        x_src, refs = refs[:n_x], refs[n_x:]
        g_src, refs = refs[:n_g], refs[n_g:]
        outs, refs = refs[:n_out], refs[n_out:]
        x_out, refs = refs[:n_x], refs[n_x:]
        g_out, refs = refs[:n_g], refs[n_g:]
        own, refs = refs[:n_s], refs[n_s:]
        x_sems, g_sems = (refs[:3], refs[3:]) if n_x else ((), refs)
        step = pl.program_id(0)
        for d in range(1, len(grid)):
            step = step * grid[d] + pl.program_id(d)
        if n_x or n_g:
            @pl.when(step == 0)
            def _():
                if n_x:
                    _xch_start(kinds, x_src, x_out, x_sems)
                if n_g:
                    _gat_start(g_src, g_out, g_sems)

        body(*ins, *outs, *own)
        if n_g:
            @pl.when(step == max(total - 2, 0))
            def _():
                _gat_forward(g_src, g_out, g_sems)

        if n_x or n_g:
            @pl.when(step == total - 1)
            def _():
                if n_x:
                    _xch_wait(kinds, x_src, x_out, x_sems)
                if n_g:
                    _gat_wait(g_src, g_out, g_sems)

    any_spec = pl.BlockSpec(memory_space=pl.ANY)
    g_shapes = [_sds((NDEV,) + tuple(a.shape), a.dtype) for a in gat]
    res = pl.pallas_call(
        wrapped, name=name, grid=grid, in_specs=list(in_specs) + [any_spec] * (n_x + n_g),
        out_specs=list(out_specs) + [any_spec] * (n_x + n_g), out_shape=list(out_shape) + _xch_shapes(xch) + g_shapes,
        scratch_shapes=list(scratch) + (_xch_scratch(n_x) if n_x else []) + (_xch_scratch(n_g) if n_g else []),
        compiler_params=_cp(len(grid)),
    )(*args, *[a for a, _, _ in xch], *gat)
    return res[:n_out], res[n_out:n_out + n_x], res[n_out + n_x:]


def embed(x, meta8, lp, gat):
    bl, seq, _ = x.shape
    c8 = D // NDEV
    cb = 2 * c8

    def body(x_ref, m_ref, h_ref):
        h_ref[0:NMETA, :] = jnp.concatenate([m_ref[0], m_ref[1]], axis=1)
        h_ref[NMETA:NMETA + seq, :] = x_ref[...]
        h_ref[NMETA + seq:, :] = jnp.zeros((lp - NMETA - seq, cb), F32)

    (h0,), _, got = _call(
        body, "embed", (bl, D // cb),
        [pl.BlockSpec((None, seq, cb), lambda b, c: (b, 0, c)), pl.BlockSpec((2, NMETA, c8), lambda b, c: (c, 0, 0))],
        [pl.BlockSpec((None, lp, cb), lambda b, c: (b, 0, c))], [_sds((bl, lp, D), F32)], (x, meta8), gat=gat)
    return h0, got


def conv_in_fwd(h, nm, l, w_in, b_in, i, tm):
    t = h.shape[0]

    def body(h_ref, g_ref, w_ref, b_ref, u_ref, big_ref, a_ref):
        x = h_ref[...]
        ub = (x * _rstd(x) * g_ref[...]).astype(BF)
        u_ref[...] = ub
        big = jnp.dot(ub, w_ref[...], preferred_element_type=F32) + b_ref[...]
        big_ref[...] = big.astype(BF)
        a_ref[...] = big[:, :D] * _sig(big[:, D:])

    return pl.pallas_call(
        body, name=f"conv_in_fwd{i}", grid=(t // tm,),
        in_specs=[_row(tm, D), _lay(l, (1, D)), _res((D, 2 * D)), _lay(i, (1, 2 * D))],
        out_specs=[_row(tm, D), _row(tm, 2 * D), _row(tm, D)],
        out_shape=[_sds((t, D), BF), _sds((t, 2 * D), BF), _sds((t, D), F32)],
        compiler_params=_cp(1),
    )(h, nm, w_in, b_in)


def _prev_halo(tm):
    return pl.BlockSpec((HALO, D), lambda i: (jnp.maximum(i * (tm // HALO) - 1, 0), 0))


def _next_halo(tm, t):
    return pl.BlockSpec((HALO, D), lambda i: (jnp.minimum((i + 1) * (tm // HALO), t // HALO - 1), 0))


def conv_mid_fwd(a, dw, ln_g, ln_b, i, tm, tpb, gat):
    t = a.shape[0]

    def body(a_ref, halo_ref, dw_ref, g_ref, b_ref, c_ref, s_ref, ext):
        first = pl.program_id(0) % tpb == 0
        ext[0:HALO] = jnp.where(first, 0.0, halo_ref[...])
        ext[HALO:] = a_ref[...]

        def chunk(k, carry):
            r0 = pl.multiple_of(k * CHUNK, CHUNK)
            win = _shifted(ext[pl.ds(r0, 2 * CHUNK), :])
            c = jnp.zeros((CHUNK, D), F32)
            for j in range(CW):
                c = c + dw_ref[j:j + 1, :] * _tap(win, j + 2)
            c_ref[pl.ds(r0, CHUNK), :] = c
            mu = jnp.mean(c, axis=-1, keepdims=True)
            xc = c - mu
            n = xc * lax.rsqrt(jnp.mean(xc * xc, axis=-1, keepdims=True) + EPS) * g_ref[...] + b_ref[...]
            s_ref[pl.ds(r0, CHUNK), :] = (n * _sig(n)).astype(BF)
            return carry

        lax.fori_loop(0, tm // CHUNK, chunk, 0)

    return _call(
        body, f"conv_mid_fwd{i}", (t // tm,),
        [_row(tm, D), _prev_halo(tm), _lay(i, (CW, D)), _lay(i, (1, D)), _lay(i, (1, D))],
        [_row(tm, D), _row(tm, D)], [_sds((t, D), F32), _sds((t, D), BF)], (a, a, dw, ln_g, ln_b),
        scratch=[pltpu.VMEM((tm + HALO, D), F32)], gat=gat)


def mixer_ffn_fwd(h, s, w_out, lw, bias, nf, l, wg, wu, wd, tm, gat):
    t = h.shape[0]

    def body(*refs):
        if bias is None:
            h_ref, s_ref, w_ref, nf_ref, wg_ref, wu_ref, wd_ref, h1_ref, u_ref, g_ref, up_ref, hid_ref, h2_ref = refs
            y = 0.0
        else:
            h_ref, s_ref, w_ref, b_ref, nf_ref, wg_ref, wu_ref, wd_ref, h1_ref, u_ref, g_ref, up_ref, hid_ref, h2_ref = refs
            y = b_ref[...]
        h1 = h_ref[...] + (jnp.dot(s_ref[...], w_ref[...], preferred_element_type=F32) + y)
        h1_ref[...] = h1
        ub = (h1 * _rstd(h1) * nf_ref[...]).astype(BF)
        u_ref[...] = ub
        g = _dot_nt(ub, wg_ref[...])
        up = _dot_nt(ub, wu_ref[...])
        g_ref[...] = g.astype(BF)
        up_ref[...] = up.astype(BF)
        hid = (g * _sig(g) * up).astype(BF)
        hid_ref[...] = hid
        h2_ref[...] = h1 + jnp.dot(hid, wd_ref[...], preferred_element_type=F32)

    ins = [h, s, w_out] + ([] if bias is None else [bias]) + [nf, wg, wu, wd]
    specs = ([_row(tm, D), _row(tm, D), _res((D, D))] + ([] if bias is None else [_lay(lw, (1, D))])
             + [_lay(l, (1, D)), _res((DFF, D)), _res((DFF, D)), _res((DFF, D))])
    return _call(
        body, f"mixer_ffn_fwd{l}", (t // tm,), specs,
        [_row(tm, D), _row(tm, D), _row(tm, DFF), _row(tm, DFF), _row(tm, DFF), _row(tm, D)],
        [_sds((t, D), F32), _sds((t, D), BF), _sds((t, DFF), BF), _sds((t, DFF), BF), _sds((t, DFF), BF), _sds((t, D), F32)],
        ins, gat=gat)


def _seg_rms(x, g, nseg):
    outs = []
    for s in range(nseg):
        xs = x[:, HD * s:HD * s + HD]
        outs.append(xs * _rstd(xs) * g)
    return jnp.concatenate(outs, axis=1)


def kv_fwd(h, kvn, w_kv, kng, tm):
    t = h.shape[0]

    def body(h_ref, g_ref, w_ref, kg_ref, kn_ref, kv_ref, k_ref, v_ref):
        x = h_ref[...]
        kn = (x * _rstd(x) * g_ref[...]).astype(BF)
        kn_ref[...] = kn
        kv = jnp.dot(kn, w_ref[...], preferred_element_type=F32)
        kv_ref[...] = kv
        k_ref[...] = _seg_rms(kv[:, :KVD], kg_ref[...], NKV).astype(BF)
        v_ref[...] = kv[:, KVD:].astype(BF)

    return pl.pallas_call(
        body, name="kv_fwd", grid=(t // tm,),
        in_specs=[_row(tm, D), _res((1, D)), _res((D, 2 * KVD)), _res((1, HD))],
        out_specs=[_row(tm, D), _row(tm, 2 * KVD), _row(tm, KVD), _row(tm, KVD)],
        out_shape=[_sds((t, D), BF), _sds((t, 2 * KVD), F32), _sds((t, KVD), BF), _sds((t, KVD), BF)],
        compiler_params=_cp(1),
    )(h, kvn, w_kv, kng)


def q_fwd(h, nm, l, w_q, j, tm):
    t = h.shape[0]

    def body(h_ref, g_ref, w_ref, u_ref, q_ref):
        x = h_ref[...]
        ub = (x * _rstd(x) * g_ref[...]).astype(BF)
        u_ref[...] = ub
        q_ref[...] = jnp.dot(ub, w_ref[...], preferred_element_type=F32)

    return pl.pallas_call(
        body, name=f"q_fwd{j}", grid=(t // tm,),
        in_specs=[_row(tm, D), _lay(l, (1, D)), _res((D, D))],
        out_specs=[_row(tm, D), _row(tm, D)], out_shape=[_sds((t, D), BF), _sds((t, D), F32)],
        compiler_params=_cp(1),
    )(h, nm, w_q)


RQ = NH // NKV


def _attn_specs(nb, lp):
    cur = lambda c: pl.BlockSpec((QB, c), lambda b, n: (b * nb + n, 0))
    seq = pl.BlockSpec((None, lp, KVD), lambda b, n: (b, 0, 0))
    seq_t = pl.BlockSpec((None, KVD, lp), lambda b, n: (b, 0, 0))
    return cur, seq, seq_t


NKEYS = 2 * QB + NMETA


def _attn_mask(n, start):
    shape = (RQ * QB, NKEYS)
    qpos = n * QB + (lax.broadcasted_iota(jnp.int32, shape, 0) & (QB - 1))
    col = lax.broadcasted_iota(jnp.int32, shape, 1)
    in_band = col < 2 * QB
    kpos = jnp.where(in_band, start + col, col - 2 * QB)
    return (kpos <= qpos) & ((col >= 2 * QB) | ((qpos - kpos < QB) & (kpos >= NMETA)))


def _keys(ref, band, gs):
    return jnp.concatenate([ref[band, gs], ref[0:NMETA, gs]], axis=0)


def _keys_t(ref, band, gs):
    return jnp.concatenate([ref[gs, band], ref[gs, 0:NMETA]], axis=1)


def transpose_seq(a, name):
    bl, r, c = a.shape

    def body(a_ref, o_ref):
        o_ref[...] = a_ref[...].T

    return pl.pallas_call(
        body, name=name, grid=(bl,), in_specs=[pl.BlockSpec((None, r, c), lambda b: (b, 0, 0))],
        out_specs=pl.BlockSpec((None, c, r), lambda b: (b, 0, 0)), out_shape=_sds((bl, c, r), a.dtype), compiler_params=_cp(1),
    )(a)


def sum_transposed(a0, a1):
    bl, c, r = a0.shape

    def body(a0_ref, a1_ref, o_ref):
        o_ref[...] = (a0_ref[...] + a1_ref[...]).T

    spec = pl.BlockSpec((None, c, r), lambda b: (b, 0, 0))
    return pl.pallas_call(
        body, name="sum_transposed", grid=(bl,), in_specs=[spec, spec],
        out_specs=pl.BlockSpec((r, c), lambda b: (b, 0)), out_shape=_sds((bl * r, c), a0.dtype), compiler_params=_cp(1),
    )(a0, a1)


def _stack_heads(ref, g, fn):
    return jnp.concatenate([fn(ref[:, HD * (g * RQ + r):HD * (g * RQ + r) + HD]) for r in range(RQ)], axis=0)


def _stack_cols(ref, g):
    return jnp.concatenate([ref[:, g * RQ + r:g * RQ + r + 1] for r in range(RQ)], axis=0)


def _stack_sinks(sk_ref, g):
    return jnp.concatenate([jnp.broadcast_to(sk_ref[:, g * RQ + r:g * RQ + r + 1], (QB, 1)) for r in range(RQ)], axis=0)


def attn_fwd(q, kt, v, qg, sinks, j, bl, lp, gat):
    t = q.shape[0]
    nb = lp // QB
    cur, seq, seq_t = _attn_specs(nb, lp)

    def body(q_ref, kt_ref, v_ref, qg_ref, sk_ref, o_ref, lse_ref):
        n = pl.program_id(1)
        start = pl.multiple_of(jnp.maximum(n - 1, 0) * QB, QB)
        mask = _attn_mask(n, start)
        band = pl.ds(start, 2 * QB)
        lane = lax.broadcasted_iota(jnp.int32, (QB, NH), 1)
        ones = jnp.ones((NKEYS, HD), BF)
        lse = jnp.zeros((QB, NH), F32)
        groups = range(NKV)
        gsl = [slice(HD * g, HD * g + HD) for g in groups]
        qns = [_stack_heads(q_ref, g, lambda x: (x * _rstd(x) * (qg_ref[...] * SCALE)).astype(BF)) for g in groups]
        ss = [jnp.where(mask, _dot(qns[g], _keys_t(kt_ref, band, gsl[g])), NEG) for g in groups]
        sinks = [_stack_sinks(sk_ref, g) for g in groups]
        mxs = [jnp.maximum(jnp.max(ss[g], -1, keepdims=True), sinks[g]) for g in groups]
        oas = [_dot(jnp.exp(ss[g] - mxs[g]), jnp.concatenate([_keys(v_ref, band, gsl[g]), ones], axis=1)) for g in groups]
        for g in groups:
            den = oas[g][:, HD:HD + 1] + jnp.exp(sinks[g] - mxs[g])
            o = oas[g][:, :HD] * (1.0 / den)
            l = mxs[g] + jnp.log(den)
            for r in range(RQ):
                h = g * RQ + r
                o_ref[:, HD * h:HD * h + HD] = o[r * QB:(r + 1) * QB].astype(BF)
                lse = jnp.where(lane == h, l[r * QB:(r + 1) * QB], lse)
        lse_ref[...] = lse

    return _call(
        body, f"attn_fwd{j}", (bl, nb),
        [cur(D), seq_t, seq, pl.BlockSpec((None, 1, HD), lambda b, n: (j, 0, 0)), pl.BlockSpec((None, 1, NH), lambda b, n: (j, 0, 0))],
        [cur(D), cur(NH)], [_sds((t, D), BF), _sds((t, NH), F32)], (q, kt, v, qg, sinks), gat=gat)


def loss_fwd(h, tgt):
    bl, lp, _ = h.shape
    seq = tgt.shape[1]
    cb = 256

    def body(h_ref, t_ref, dh_ref, loss_ref):
        _init(loss_ref, (pl.program_id(0) == 0) & (pl.program_id(1) == 0))
        err = h_ref[NMETA:NMETA + seq, :] - t_ref[...]
        dh_ref[...] = jnp.zeros_like(dh_ref)
        dh_ref[NMETA:NMETA + seq, :] = err * (1.0 / D)
        loss_ref[...] += (0.5 / D) * jnp.sum(err * err)

    return pl.pallas_call(
        body, name="loss_fwd", grid=(bl, D // cb),
        in_specs=[pl.BlockSpec((None, lp, cb), lambda b, c: (b, 0, c)), pl.BlockSpec((None, seq, cb), lambda b, c: (b, 0, c))],
        out_specs=[pl.BlockSpec((None, lp, cb), lambda b, c: (b, 0, c)), pl.BlockSpec((8, 128), lambda b, c: (0, 0))],
        out_shape=[_sds((bl, lp, D), F32), _sds((8, 128), F32)],
        compiler_params=_cp(2),
    )(h, tgt)


def ffn_bwd_x(dh2, g, up, h1, nf, l, wd, wg, wu, w_o, tm, xch):
    t = dh2.shape[0]

    def body(dh2_ref, g_ref, up_ref, h1_ref, nf_ref, wd_ref, wg_ref, wu_ref, *rest):
        if w_o is None:
            dg_ref, du_ref, dh1_ref, dnf_ref = rest
        else:
            wo_ref, dg_ref, du_ref, dh1_ref, dnf_ref, do_ref = rest
        _init(dnf_ref, pl.program_id(0) == 0)
        dh2v = dh2_ref[...]
        dhid = _dot_nt(dh2v, wd_ref[...])
        gv = g_ref[...].astype(F32)
        uv = up_ref[...].astype(F32)
        sg = _sig(gv)
        dgv = (dhid * uv * (sg * (1.0 + gv * (1.0 - sg)))).astype(BF)
        duv = (dhid * (gv * sg)).astype(BF)
        dg_ref[...] = dgv
        du_ref[...] = duv
        dnorm = _dot(dgv, wg_ref[...]) + _dot(duv, wu_ref[...])
        dx, dnf = _rms_bwd(h1_ref[...], nf_ref[...], dnorm)
        dh1 = dh2v + dx
        dh1_ref[...] = dh1
        dnf_ref[...] += dnf
        if w_o is not None:
            do_ref[...] = _dot_nt(dh1, wo_ref[...]).astype(BF)

    attn = w_o is not None
    return _call(
        body, f"ffn_bwd_x{l}", (t // tm,),
        [_row(tm, D), _row(tm, DFF), _row(tm, DFF), _row(tm, D), _lay(l, (1, D)),
         _res((DFF, D)), _res((DFF, D)), _res((DFF, D))] + ([_res((D, D))] if attn else []),
        [_row(tm, DFF), _row(tm, DFF), _row(tm, D), _acc((1, D))] + ([_row(tm, D)] if attn else []),
        [_sds((t, DFF), BF), _sds((t, DFF), BF), _sds((t, D), F32), _sds((1, D), F32)] + ([_sds((t, D), BF)] if attn else []),
        (dh2, g, up, h1, nf, wd, wg, wu) + ((w_o,) if attn else ()), xch=xch)


def mm_tn(x, dy, tm, name, split=False, transposed=False, xch=()):
    t, kk = x.shape
    nn = dy.shape[1]
    n8 = nn // NDEV
    nsteps = t // tm

    def body(x_ref, dy_ref, o_ref, acc):
        i = pl.program_id(0)
        _init(acc, i == 0)
        acc[...] += _dot_tn(x_ref[...], dy_ref[...])

        @pl.when(i == nsteps - 1)
        def _():
            if split:
                for p in range(NDEV):
                    o_ref[p] = acc[:, p * n8:(p + 1) * n8].astype(BF)
            elif transposed:
                o_ref[...] = acc[...].T.astype(BF)
            else:
                o_ref[...] = acc[...].astype(BF)

    oshape = (NDEV, kk, n8) if split else ((nn, kk) if transposed else (kk, nn))
    (out,), got, _ = _call(body, name, (nsteps,), [_row(tm, kk), _row(tm, nn)], [_acc(oshape)], [_sds(oshape, BF)], (x, dy),
                           scratch=[pltpu.VMEM((kk, nn), F32)], xch=xch)
    return out, got


def proj_bwd(dy, w, h, g, lg, dh_in, tm, name, xch=()):
    t = h.shape[0]
    nn = dy.shape[1]
    wspec = _res(w.shape)
    gspec = _res((1, D)) if lg is None else _lay(lg, (1, D))

    def body(dy_ref, w_ref, h_ref, g_ref, dhin_ref, dh_ref, dg_ref):
        _init(dg_ref, pl.program_id(0) == 0)
        du = _dot_nt(dy_ref[...], w_ref[...])
        dx, dg = _rms_bwd(h_ref[...], g_ref[...], du)
        dh_ref[...] = dhin_ref[...] + dx
        dg_ref[...] += dg

    return _call(body, name, (t // tm,), [_row(tm, nn), wspec, _row(tm, D), gspec, _row(tm, D)],
                 [_row(tm, D), _acc((1, D))], [_sds((t, D), F32), _sds((1, D), F32)], (dy, w, h, g, dh_in), xch=xch)


def attn_bwd(q, k, kt, vt, do, o, lse, qg, sinks, j, bl, lp, xch):
    t = q.shape[0]
    nb = lp // QB
    cur, seq, seq_t = _attn_specs(nb, lp)

    def body(q_ref, k_ref, kt_ref, vt_ref, do_ref, o_ref, lse_ref, qg_ref, sk_ref, dq_ref, dk_ref, dv_ref, dqg_ref, dsk_ref):
        b, n = pl.program_id(0), pl.program_id(1)
        _init(dk_ref, n == 0)
        _init(dv_ref, n == 0)
        _init(dqg_ref, (b == 0) & (n == 0))
        _init(dsk_ref, (b == 0) & (n == 0))
        start = pl.multiple_of(jnp.maximum(n - 1, 0) * QB, QB)
        mask = _attn_mask(n, start)
        band = pl.ds(start, 2 * QB)
        lane = lax.broadcasted_iota(jnp.int32, (1, NH), 1)
        dqg = jnp.zeros((1, HD), F32)
        dsk = jnp.zeros((1, NH), F32)
        groups = range(NKV)
        gsl = [slice(HD * g, HD * g + HD) for g in groups]
        qhs = [_stack_heads(q_ref, g, lambda x: x) for g in groups]
        rss = [_rstd(qhs[g]) for g in groups]
        qns = [(qhs[g] * rss[g] * (qg_ref[...] * SCALE)).astype(BF) for g in groups]
        lss = [_stack_cols(lse_ref, g) for g in groups]
        dohs = [_stack_heads(do_ref, g, lambda x: x) for g in groups]
        deltas = [jnp.sum(dohs[g].astype(F32) * _stack_heads(o_ref, g, lambda x: x).astype(F32), axis=-1, keepdims=True)
                  for g in groups]
        prs = [jnp.where(mask, jnp.exp(_dot(qns[g], _keys_t(kt_ref, band, gsl[g])) - lss[g]), 0.0) for g in groups]
        dss = [(prs[g] * (_dot(dohs[g], _keys_t(vt_ref, band, gsl[g])) - deltas[g])).astype(BF) for g in groups]
        for g in groups:
            gs = gsl[g]
            dkt = _dot_tn(qns[g], dss[g])
            dvt = _dot_tn(dohs[g], prs[g])
            dk_ref[gs, band] += dkt[:, :2 * QB]
            dv_ref[gs, band] += dvt[:, :2 * QB]
            dk_ref[gs, 0:NMETA] += dkt[:, 2 * QB:]
            dv_ref[gs, 0:NMETA] += dvt[:, 2 * QB:]
        dqns = [_dot(dss[g], _keys(k_ref, band, gsl[g])) * SCALE for g in groups]
        for g in groups:
            qh, rs, dqn = qhs[g], rss[g], dqns[g]
            dsink = jnp.exp(_stack_sinks(sk_ref, g) - lss[g]) * deltas[g]
            z = dqn * qg_ref[...]
            dq = rs * z - qh * (rs * rs * rs * jnp.mean(z * qh, axis=-1, keepdims=True))
            dqg = dqg + jnp.sum(dqn * qh * rs, axis=0, keepdims=True)
            for r in range(RQ):
                h = g * RQ + r
                dq_ref[:, HD * h:HD * h + HD] = dq[r * QB:(r + 1) * QB]
                dsk = dsk + jnp.where(lane == h, -jnp.sum(dsink[r * QB:(r + 1) * QB]), 0.0)
        dqg_ref[...] += dqg
        dsk_ref[...] += dsk

    return _call(
        body, f"attn_bwd{j}", (bl, nb),
        [cur(D), seq, seq_t, seq_t, cur(D), cur(D), cur(NH),
         pl.BlockSpec((None, 1, HD), lambda b, n: (j, 0, 0)), pl.BlockSpec((None, 1, NH), lambda b, n: (j, 0, 0))],
        [cur(D), seq_t, seq_t, pl.BlockSpec((1, HD), lambda b, n: (0, 0)), pl.BlockSpec((1, NH), lambda b, n: (0, 0))],
        [_sds((t, D), F32), _sds((bl, KVD, lp), F32), _sds((bl, KVD, lp), F32), _sds((1, HD), F32), _sds((1, NH), F32)],
        (q, k, kt, vt, do, o, lse, qg, sinks), xch=xch)


def kv_bwd_pre(dk, dv, kv, kng, tm):
    t = kv.shape[0]

    def body(dk_ref, dv_ref, kv_ref, g_ref, dkv_ref, dg_ref):
        _init(dg_ref, pl.program_id(0) == 0)
        dg = jnp.zeros((1, HD), F32)
        outs = []
        for s in range(NKV):
            sl = slice(HD * s, HD * s + HD)
            dx, dgs = _rms_bwd(kv_ref[:, sl], g_ref[...], dk_ref[:, sl])
            outs.append(dx)
            dg = dg + dgs
        dkv_ref[:, :KVD] = jnp.concatenate(outs, axis=1).astype(BF)
        dkv_ref[:, KVD:] = dv_ref[...].astype(BF)
        dg_ref[...] += dg

    return pl.pallas_call(
        body, name="kv_bwd_pre", grid=(t // tm,),
        in_specs=[_row(tm, KVD)] * 2 + [_row(tm, 2 * KVD), _res((1, HD))],
        out_specs=[_row(tm, 2 * KVD), _acc((1, HD))], out_shape=[_sds((t, 2 * KVD), BF), _sds((1, HD), F32)],
        compiler_params=_cp(1),
    )(dk, dv, kv, kng)


def conv_out_bwd(dh1, c, ln_g, ln_b, w_out, i, tm, xch):
    t = dh1.shape[0]

    def body(dh1_ref, c_ref, g_ref, b_ref, w_ref, dc_ref, dg_ref, db_ref, dbo_ref):
        first = pl.program_id(0) == 0
        _init(dg_ref, first)
        _init(db_ref, first)
        _init(dbo_ref, first)
        dh1v = dh1_ref[...]
        ds = _dot_nt(dh1v, w_ref[...])
        cv = c_ref[...]
        xc = cv - jnp.mean(cv, axis=-1, keepdims=True)
        rstd = lax.rsqrt(jnp.mean(xc * xc, axis=-1, keepdims=True) + EPS)
        xh = xc * rstd
        n = xh * g_ref[...] + b_ref[...]
        sg = _sig(n)
        dn = ds * (sg * (1.0 + n * (1.0 - sg)))
        dxh = dn * g_ref[...]
        dc_ref[...] = rstd * (dxh - jnp.mean(dxh, axis=-1, keepdims=True) - xh * jnp.mean(dxh * xh, axis=-1, keepdims=True))
        dg_ref[...] += jnp.sum(dn * xh, axis=0, keepdims=True)
        db_ref[...] += jnp.sum(dn, axis=0, keepdims=True)
        dbo_ref[...] += jnp.sum(dh1v, axis=0, keepdims=True)

    return _call(
        body, f"conv_out_bwd{i}", (t // tm,), [_row(tm, D), _row(tm, D), _lay(i, (1, D)), _lay(i, (1, D)), _res((D, D))],
        [_row(tm, D), _acc((1, D)), _acc((1, D)), _acc((1, D))], [_sds((t, D), F32)] + [_sds((1, D), F32)] * 3,
        (dh1, c, ln_g, ln_b, w_out), xch=xch)


def conv_mid_bwd(dc, a, big, dw, i, tm, tpb, xch):
    t = dc.shape[0]
    nsteps = t // tm

    def body(dc_ref, nxt_ref, a_ref, prv_ref, big_ref, dw_ref, da_ref, dbin_ref, ddw_ref, dce, ae, wacc, bacc):
        i_ = pl.program_id(0)
        _init(wacc, i_ == 0)
        _init(bacc, i_ == 0)
        dce[0:tm] = dc_ref[...]
        dce[tm:] = jnp.where(i_ % tpb == tpb - 1, 0.0, nxt_ref[...])
        ae[0:HALO] = jnp.where(i_ % tpb == 0, 0.0, prv_ref[...])
        ae[HALO:] = a_ref[...]

        def chunk(k, carry):
            r0 = pl.multiple_of(k * CHUNK, CHUNK)
            wdc = _shifted(dce[pl.ds(r0, 2 * CHUNK), :])
            wa = _shifted(ae[pl.ds(r0, 2 * CHUNK), :])
            dcc = wdc[0][0:CHUNK]
            da = jnp.zeros((CHUNK, D), F32)
            for j in range(CW):
                da = da + dw_ref[j:j + 1, :] * _tap(wdc, CW - 1 - j)
                wacc[j] += _fold8(dcc * _tap(wa, j + 2))
            bv = big_ref[pl.ds(r0, CHUNK), :].astype(F32)
            a1, sg = bv[:, :D], _sig(bv[:, D:])
            d1 = da * sg
            d2 = da * a1 * sg * (1.0 - sg)
            da_ref[pl.ds(r0, CHUNK), 0:D] = d1.astype(BF)
            da_ref[pl.ds(r0, CHUNK), D:2 * D] = d2.astype(BF)
            bacc[:, 0:D] += _fold8(d1)
            bacc[:, D:2 * D] += _fold8(d2)
            return carry

        lax.fori_loop(0, tm // CHUNK, chunk, 0)

        @pl.when(i_ == nsteps - 1)
        def _():
            dbin_ref[...] = jnp.sum(bacc[...], axis=0, keepdims=True)
            ddw_ref[...] = jnp.sum(wacc[...], axis=1)

    return _call(
        body, f"conv_mid_bwd{i}", (nsteps,),
        [_row(tm, D), _next_halo(tm, t), _row(tm, D), _prev_halo(tm), _row(tm, 2 * D), _lay(i, (CW, D))],
        [_row(tm, 2 * D), _acc((1, 2 * D)), _acc((CW + 1, D))],
        [_sds((t, 2 * D), BF), _sds((1, 2 * D), F32), _sds((CW + 1, D), F32)],
        (dc, dc, a, a, big, dw),
        scratch=[pltpu.VMEM((tm + HALO, D), F32), pltpu.VMEM((tm + HALO, D), F32),
                 pltpu.VMEM((CW + 1, 8, D), F32), pltpu.VMEM((8, 2 * D), F32)], xch=xch)


def input_grads(dh0, seq):
    bl, lp, _ = dh0.shape
    cb = 256

    def body(dh_ref, gx_ref, gm_ref):
        _init(gm_ref, pl.program_id(1) == 0)
        gx_ref[...] = dh_ref[NMETA:NMETA + seq, :]
        gm_ref[...] += dh_ref[0:NMETA, :]

    return pl.pallas_call(
        body, name="input_grads", grid=(D // cb, bl),
        in_specs=[pl.BlockSpec((None, lp, cb), lambda c, b: (b, 0, c))],
        out_specs=[pl.BlockSpec((None, seq, cb), lambda c, b: (b, 0, c)), pl.BlockSpec((NMETA, cb), lambda c, b: (0, c))],
        out_shape=[_sds((bl, seq, D), F32), _sds((NMETA, D), F32)],
        compiler_params=_cp(2),
    )(dh0)


GATHER_PLAN = {
    "embed": [("conv_w_in", 0), ("conv_w_out", 0)],
    "conv_mid_fwd0": [("ffn_w_gate", 0), ("ffn_w_up", 0), ("ffn_w_down", 0)],
    "mixer_ffn_fwd0": [("conv_w_in", 1), ("conv_w_out", 1), ("ffn_w_gate", 1)],
    "conv_mid_fwd1": [("ffn_w_up", 1), ("ffn_w_down", 1), ("w_kv", 0), ("w_q", 0)],
    "mixer_ffn_fwd1": [("w_o", 0), ("ffn_w_down", 2)],
    "attn_fwd0": [("ffn_w_gate", 2), ("ffn_w_up", 2), ("w_q", 1), ("w_o", 1)],
    "attn_fwd1": [("ffn_w_gate", 3), ("ffn_w_up", 3), ("ffn_w_down", 3)],
}
EXCHANGE_PLAN = {
    "attn_bwd1": [("ffn_w_down3", ALL), ("ffn_w_gate3", ALL), ("ffn_w_up3", H1)],
    "dw_down2": [("w_o1", ALL)],
    "ffn_bwd_x2": [("ffn_w_up3", H2), ("w_q1", ALL)],
    "attn_bwd0": [("ffn_w_down2", ALL), ("ffn_w_gate2", ALL), ("ffn_w_up2", H1)],
    "dw_down1": [("w_o0", ALL), ("w_q0", H1)],
    "ffn_bwd_x1": [("ffn_w_up2", H2), ("w_q0", H2), ("w_kv", ALL)],
    "dw_gate1": [("ffn_w_down1", H1)],
    "dw_up1": [("ffn_w_down1", H2)],
    "conv_mid_bwd1": [("ffn_w_gate1", ALL), ("ffn_w_up1", H1)],
    "conv_in_bwd1": [("ffn_w_up1", H2)],
    "dw_down0": [("conv_w_out1", ALL)],
    "ffn_bwd_x0": [("conv_w_in1", ALL)],
    "dw_gate0": [("ffn_w_down0", H1)],
    "dw_up0": [("ffn_w_down0", H2)],
    "conv_out_bwd0": [("ffn_w_gate0", H1)],
    "conv_mid_bwd0": [("ffn_w_gate0", H2), ("ffn_w_up0", H1), ("conv_w_out0", ALL)],
    "dw_conv_in0": [("ffn_w_up0", H2)],
    "conv_in_bwd0": [("conv_w_in0", H1)],
    "tail": [("conv_w_in0", H2)],
}
BIG = {"conv_w_in": "pieces", "conv_w_out": "rows", "w_kv": "rows", "w_q": "rows", "w_o": "rows",
       "ffn_w_gate": "rows", "ffn_w_up": "rows", "ffn_w_down": "rows"}
EXCHANGE_KIND = BIG
TRANSPOSED = ("ffn_w_gate", "ffn_w_up")


def gathered_matrix(name, layer, blocks8):
    if BIG[name] == "rows":
        return blocks8.reshape(NDEV * blocks8.shape[1], blocks8.shape[2])
    return join_columns(blocks8, f"join_{name}{layer}")


def local_step(x, tgt, meta8, w, shards):
    bl, seq, _ = x.shape
    lp = -(-(NMETA + seq) // QB) * QB
    tpb = 4
    tm = lp // tpb
    t = bl * lp
    na = 2
    flat = lambda a: a.reshape(t, D)
    mats = {}

    def riders(carrier):
        return [shards[key] for key in GATHER_PLAN[carrier]]

    def landed(carrier, blocks):
        for key, b8 in zip(GATHER_PLAN[carrier], blocks):
            mats[key] = gathered_matrix(*key, b8)

    h0, got = embed(x, meta8, lp, riders("embed"))
    landed("embed", got)
    h = flat(h0)
    saved = []
    kvs = None
    for l in range(4):
        rec = {"h": h}
        if l < na:
            rec["u"], rec["big"], rec["a"] = conv_in_fwd(h, w["norm_mix"], l, mats["conv_w_in", l], w["conv_b_in"], l, tm)
            name = f"conv_mid_fwd{l}"
            (rec["c"], rec["s"]), _, got = conv_mid_fwd(rec["a"], w["conv_dw"], w["conv_ln_g"], w["conv_ln_b"], l, tm, tpb,
                                                         riders(name))
            landed(name, got)
            mixed, w_out, lw, bias = rec["s"], mats["conv_w_out", l], l, w["conv_b_out"]
        else:
            j = l - na
            if kvs is None:
                kvs = dict(zip(("kn", "kv", "k", "v"), kv_fwd(h, w["kv_norm"], mats["w_kv", 0], w["k_norm"], tm)))
                kvs["h"] = h
                kvs["k3"], kvs["v3"] = kvs["k"].reshape(bl, lp, KVD), kvs["v"].reshape(bl, lp, KVD)
                kvs["kt"], kvs["vt"] = transpose_seq(kvs["k3"], "transpose_k"), transpose_seq(kvs["v3"], "transpose_v")
            rec["u"], rec["q"] = q_fwd(h, w["norm_mix"], l, mats["w_q", j], j, tm)
            name = f"attn_fwd{j}"
            (rec["o"], rec["lse"]), _, got = attn_fwd(rec["q"], kvs["kt"], kvs["v3"], w["q_norm"], w["attn_sinks"], j, bl, lp,
                                                      riders(name) if name in GATHER_PLAN else [])
            if name in GATHER_PLAN:
                landed(name, got)
            mixed, w_out, lw, bias = rec["o"], mats["w_o", j], j, None
        name = f"mixer_ffn_fwd{l}"
        (rec["h1"], rec["u2"], rec["g"], rec["up"], rec["hid"], h), _, got = mixer_ffn_fwd(
            h, mixed, w_out, lw, bias, w["norm_ffn"], l, mats["ffn_w_gate", l], mats["ffn_w_up", l], mats["ffn_w_down", l], tm // 2,
            riders(name) if name in GATHER_PLAN else [])
        if name in GATHER_PLAN:
            landed(name, got)
        saved.append(rec)

    dh3, loss_blk = loss_fwd(h.reshape(bl, lp, D), tgt)
    dh = flat(dh3)

    big, small, arrived = {}, {}, {}
    dks, dvs = [], []

    def ride(kernel_name):
        return [(big[nm], EXCHANGE_KIND[nm.rstrip("0123456789")], ks) for nm, ks in EXCHANGE_PLAN.get(kernel_name, [])]

    def landed_x(kernel_name, arrivals):
        for (nm, _), got in zip(EXCHANGE_PLAN.get(kernel_name, []), arrivals):
            arrived.setdefault(nm, []).append(got)

    def dw(name, grad, x, dy, **kw):
        big[grad], got = mm_tn(x, dy, 2 * tm, name, xch=ride(name), **kw)
        landed_x(name, got)

    for l in reversed(range(4)):
        rec = saved[l]
        dw(f"dw_down{l}", f"ffn_w_down{l}", rec["hid"], dh)
        name = f"ffn_bwd_x{l}"
        outs, got, _ = ffn_bwd_x(
            dh, rec["g"], rec["up"], rec["h1"], w["norm_ffn"], l, mats["ffn_w_down", l], mats["ffn_w_gate", l], mats["ffn_w_up", l],
            mats["w_o", l - na] if l >= na else None, tm // 2, ride(name))
        landed_x(name, got)
        dg, du, dh1, small[f"norm_ffn{l}"] = outs[:4]
        dw(f"dw_gate{l}", f"ffn_w_gate{l}", rec["u2"], dg, transposed=True)
        dw(f"dw_up{l}", f"ffn_w_up{l}", rec["u2"], du, transposed=True)
        if l >= na:
            j = l - na
            dw(f"dw_o{j}", f"w_o{j}", rec["o"], dh1)
            name = f"attn_bwd{j}"
            (dq, dk, dv, small[f"q_norm{j}"], small[f"attn_sinks{j}"]), got, _ = attn_bwd(
                rec["q"], kvs["k3"], kvs["kt"], kvs["vt"], outs[4], rec["o"], rec["lse"], w["q_norm"], w["attn_sinks"], j, bl, lp,
                ride(name))
            landed_x(name, got)
            dks.append(dk)
            dvs.append(dv)
            dw(f"dw_q{j}", f"w_q{j}", rec["u"], dq)
            dh, small[f"norm_mix{l}"] = proj_bwd(dq, mats["w_q", j], rec["h"], w["norm_mix"], l, dh1, tm, f"q_bwd{j}")[0]
            if l == na:
                dkv, small["k_norm"] = kv_bwd_pre(sum_transposed(*dks), sum_transposed(*dvs), kvs["kv"], w["k_norm"], tm)
                dw("dw_kv", "w_kv", kvs["kn"], dkv)
                dh, small["kv_norm"] = proj_bwd(dkv, mats["w_kv", 0], kvs["h"], w["kv_norm"], None, dh, tm, "kv_bwd")[0]
        else:
            name = f"conv_out_bwd{l}"
            (dc, small[f"conv_ln_g{l}"], small[f"conv_ln_b{l}"], small[f"conv_b_out{l}"]), got, _ = conv_out_bwd(
                dh1, rec["c"], w["conv_ln_g"], w["conv_ln_b"], mats["conv_w_out", l], l, tm, ride(name))
            landed_x(name, got)
            dw(f"dw_conv_out{l}", f"conv_w_out{l}", rec["s"], dh1)
            name = f"conv_mid_bwd{l}"
            (da, small[f"conv_b_in{l}"], small[f"conv_dw{l}"]), got, _ = conv_mid_bwd(
                dc, rec["a"], rec["big"], w["conv_dw"], l, tm, tpb, ride(name))
            landed_x(name, got)
            dw(f"dw_conv_in{l}", f"conv_w_in{l}", rec["u"], da, split=True)
            name = f"conv_in_bwd{l}"
            (dh, small[f"norm_mix{l}"]), got, _ = proj_bwd(da, mats["conv_w_in", l], rec["h"], w["norm_mix"], l, dh1, tm, name,
                                                           ride(name))
            landed_x(name, got)
    grad_x, small["meta_tokens"] = input_grads(dh.reshape(bl, lp, D), seq)
    return loss_blk, grad_x, big, arrived, small


def all_gather_blocks(blocks):
    n = len(blocks)

    def body(*refs):
        srcs, outs, sems = refs[:n], refs[n:2 * n], refs[2 * n:]
        _gat_start(srcs, outs, sems)
        _gat_forward(srcs, outs, sems)
        _gat_wait(srcs, outs, sems)

    any_spec = pl.BlockSpec(memory_space=pl.ANY)
    return pl.pallas_call(
        body, name="all_gather_blocks", out_shape=[_sds((NDEV,) + tuple(a.shape), a.dtype) for a in blocks],
        in_specs=[any_spec] * n, out_specs=[any_spec] * n, scratch_shapes=_xch_scratch(n),
    )(*blocks)


def cast_bf16(ws):
    n = len(ws)
    counts = [1 if x.ndim == 2 else x.shape[0] for x in ws]

    def body(*refs):
        outs = iter(refs[n:])
        for a in range(n):
            for l in range(counts[a]):
                next(outs)[...] = (refs[a][...] if ws[a].ndim == 2 else refs[a][l]).astype(BF)

    flat = pl.pallas_call(
        body, name="cast_bf16", out_shape=[_sds(x.shape[-2:], BF) for x, k in zip(ws, counts) for _ in range(k)],
        compiler_params=pltpu.CompilerParams(vmem_limit_bytes=VMEM_LIMIT),
    )(*ws)
    it = iter(flat)
    return [[next(it) for _ in range(k)] for k in counts]


def join_columns(w8, name):
    _, kk, n8 = w8.shape

    def body(x_ref, o_ref):
        o_ref[...] = jnp.concatenate([x_ref[p] for p in range(NDEV)], axis=1)

    return pl.pallas_call(body, name=name, out_shape=_sds((kk, NDEV * n8), w8.dtype),
                          compiler_params=pltpu.CompilerParams(vmem_limit_bytes=VMEM_LIMIT))(w8)


def _adamw_math(w, m, v, g):
    m2 = B1 * m + (1.0 - B1) * g
    v2 = B2 * v + (1.0 - B2) * (g * g)
    mh = m2 / (1.0 - B1 ** STEP)
    vh = v2 / (1.0 - B2 ** STEP)
    return -LR * (mh / (jnp.sqrt(vh) + AEPS) + WD * w), m2, v2


def adamw_big(w, m, v, parts, name, xch=(), gat=()):
    lyr, r, c = w.shape
    by_cols = c >= 512
    blk = (lyr, r, 256) if by_cols else (lyr, 256 if r % 256 == 0 else r, c)
    imap = (lambda i: (0, 0, i)) if by_cols else (lambda i: (0, i, 0))
    counts = [len(per_layer) for per_layer in parts]

    def body(w_ref, m_ref, v_ref, *rest):
        p_refs, (g_ref, d_ref, m2_ref, v2_ref) = iter(rest[:sum(counts)]), rest[sum(counts):]
        for l in range(lyr):
            g = None
            for _ in range(counts[l]):
                ref = next(p_refs)
                for q in range(ref.shape[0]):
                    g = ref[q].astype(F32) if g is None else g + ref[q].astype(F32)
            g_ref[l] = g
            d_ref[l], m2_ref[l], v2_ref[l] = _adamw_math(w_ref[l], m_ref[l], v_ref[l], g)

    spec = pl.BlockSpec(blk, imap)
    flat = [a for per_layer in parts for a in per_layer]
    pspecs = [pl.BlockSpec((a.shape[0],) + blk[1:], imap) for a in flat]
    return _call(body, name, ((c // 256) if by_cols else (r // blk[1]),), [spec, spec, spec] + pspecs,
                 [spec] * 4, [_sds((lyr, r, c), F32)] * 4, (w, m, v, *flat), xch=xch, gat=gat)


SMALL_ROWS = 104
REPLICATED = {"norm_mix": (0, 4, D), "norm_ffn": (4, 4, D), "kv_norm": (8, 1, D), "k_norm": (9, 1, HD), "q_norm": (10, 2, HD),
              "attn_sinks": (12, 2, NH)}
LOSS_ROW = 14
SHARDED = {"meta_tokens": (16, NMETA), "conv_b_in": (32, 4), "conv_dw": (36, 2 * CW), "conv_ln_g": (98, 2), "conv_ln_b": (100, 2),
           "conv_b_out": (102, 2)}


def pack_small(gs, loss_blk):
    order = ([f"norm_mix{l}" for l in range(4)] + [f"norm_ffn{l}" for l in range(4)] + ["kv_norm", "k_norm", "q_norm0", "q_norm1",
             "attn_sinks0", "attn_sinks1", "meta_tokens", "conv_b_in0", "conv_b_in1", "conv_dw0", "conv_dw1", "conv_ln_g0",
             "conv_ln_g1", "conv_ln_b0", "conv_ln_b1", "conv_b_out0", "conv_b_out1"])

    def body(*refs):
        r = dict(zip(order, refs))
        loss_ref, o_ref = refs[len(order)], refs[len(order) + 1]
        o_ref[...] = jnp.zeros_like(o_ref)
        for l in range(4):
            o_ref[l:l + 1, :] = r[f"norm_mix{l}"][...]
            o_ref[4 + l:5 + l, :] = r[f"norm_ffn{l}"][...]
        o_ref[8:9, :] = r["kv_norm"][...]
        o_ref[9:10, 0:HD] = r["k_norm"][...]
        for j in range(2):
            o_ref[10 + j:11 + j, 0:HD] = r[f"q_norm{j}"][...]
            o_ref[12 + j:13 + j, 0:NH] = r[f"attn_sinks{j}"][...]
            o_ref[32 + 2 * j:33 + 2 * j, :] = r[f"conv_b_in{j}"][:, 0:D]
            o_ref[33 + 2 * j:34 + 2 * j, :] = r[f"conv_b_in{j}"][:, D:2 * D]
            o_ref[36 + CW * j:36 + CW * (j + 1), :] = r[f"conv_dw{j}"][0:CW, :]
            o_ref[98 + j:99 + j, :] = r[f"conv_ln_g{j}"][...]
            o_ref[100 + j:101 + j, :] = r[f"conv_ln_b{j}"][...]
            o_ref[102 + j:103 + j, :] = r[f"conv_b_out{j}"][...]
        o_ref[LOSS_ROW:LOSS_ROW + 1, 0:1] = loss_ref[0:1, 0:1]
        o_ref[16:16 + NMETA, :] = r["meta_tokens"][...]

    return pl.pallas_call(body, name="pack_small", out_shape=_sds((SMALL_ROWS, D), F32))(*[gs[k] for k in order], loss_blk)


def adamw_small(g8, wts, mom, var):
    names = list(REPLICATED) + list(SHARDED)
    shape2 = {"kv_norm": (1, D), "k_norm": (1, HD)}
    ins = [a[k].reshape(shape2.get(k, a[k].shape)) for a in (wts, mom, var) for k in names]
    n = len(names)

    def body(*refs):
        g8_ref, w_refs, m_refs, v_refs = refs[0], refs[1:1 + n], refs[1 + n:1 + 2 * n], refs[1 + 2 * n:1 + 3 * n]
        loss_ref, outs, red_ref = refs[1 + 3 * n], refs[2 + 3 * n:-1], refs[-1]
        me = _my_index()
        acc = g8_ref[0]
        for q in range(1, NDEV):
            acc = acc + g8_ref[q]
        red_ref[...] = acc
        loss_ref[...] = red_ref[LOSS_ROW:LOSS_ROW + 1, 0:1]

        def mine(rows, width):
            acc = jnp.zeros((rows.stop - rows.start, width), F32)
            for p_ in range(NDEV):
                acc = acc + jnp.where(me == p_, red_ref[rows, p_ * width:(p_ + 1) * width], 0.0)
            return acc

        for i, k in enumerate(names):
            if k in REPLICATED:
                r0, nr, width = REPLICATED[k]
                g = red_ref[r0:r0 + nr, 0:width]
            elif k == "conv_b_in":
                half = D // (2 * D // NDEV)
                acc = jnp.zeros((2, 2 * D // NDEV), F32)
                for p_ in range(NDEV):
                    c0 = (p_ % half) * (2 * D // NDEV)
                    part = jnp.concatenate([red_ref[32 + 2 * j + p_ // half:33 + 2 * j + p_ // half, c0:c0 + 2 * D // NDEV]
                                            for j in range(2)], axis=0)
                    acc = acc + jnp.where(me == p_, part, 0.0)
                g = acc
            else:
                r0, nr = SHARDED[k]
                g = mine(slice(r0, r0 + nr), D // NDEV)
            w_, m_, v_ = w_refs[i], m_refs[i], v_refs[i]
            g_out, d_out, m_out, v_out = outs[4 * i:4 * i + 4]
            if k == "conv_dw":
                for j in range(2):
                    gj = g[CW * j:CW * (j + 1)]
                    g_out[j] = gj
                    d_out[j], m_out[j], v_out[j] = _adamw_math(w_[j], m_[j], v_[j], gj)
            else:
                g_out[...] = g
                d_out[...], m_out[...], v_out[...] = _adamw_math(w_[...], m_[...], v_[...], g)

    out_shape = [_sds((1, 1), F32)] + [_sds(ins[i].shape, F32) for i in range(n) for _ in range(4)]
    res = pl.pallas_call(body, name="adamw_small", out_shape=out_shape, scratch_shapes=[pltpu.VMEM((SMALL_ROWS, D), F32)])(g8, *ins)
    out = {k: tuple(o.reshape(wts[k].shape) for o in res[1 + 4 * i:5 + 4 * i]) for i, k in enumerate(names)}
    return res[0], out


NAMES = ["meta_tokens", "norm_mix", "norm_ffn", "conv_w_in", "conv_b_in", "conv_dw", "conv_ln_g", "conv_ln_b", "conv_w_out",
         "conv_b_out", "kv_norm", "w_kv", "k_norm", "w_q", "q_norm", "attn_sinks", "w_o", "ffn_w_gate", "ffn_w_up", "ffn_w_down"]


def kernel(x, meta_tokens, norm_mix, norm_ffn, conv_w_in, conv_b_in, conv_dw, conv_ln_g, conv_ln_b, conv_w_out, conv_b_out, kv_norm, w_kv, k_norm, w_q, q_norm, attn_sinks, w_o, ffn_w_gate, ffn_w_up, ffn_w_down, loss_target, m_meta_tokens, m_norm_mix, m_norm_ffn, m_conv_w_in, m_conv_b_in, m_conv_dw, m_conv_ln_g, m_conv_ln_b, m_conv_w_out, m_conv_b_out, m_kv_norm, m_w_kv, m_k_norm, m_w_q, m_q_norm, m_attn_sinks, m_w_o, m_ffn_w_gate, m_ffn_w_up, m_ffn_w_down, v_meta_tokens, v_norm_mix, v_norm_ffn, v_conv_w_in, v_conv_b_in, v_conv_dw, v_conv_ln_g, v_conv_ln_b, v_conv_w_out, v_conv_b_out, v_kv_norm, v_w_kv, v_k_norm, v_w_q, v_q_norm, v_attn_sinks, v_w_o, v_ffn_w_gate, v_ffn_w_up, v_ffn_w_down):
    wts = dict(zip(NAMES, (meta_tokens, norm_mix, norm_ffn, conv_w_in, conv_b_in, conv_dw, conv_ln_g, conv_ln_b, conv_w_out,
                           conv_b_out, kv_norm, w_kv, k_norm, w_q, q_norm, attn_sinks, w_o, ffn_w_gate, ffn_w_up, ffn_w_down)))
    mom = dict(zip(NAMES, (m_meta_tokens, m_norm_mix, m_norm_ffn, m_conv_w_in, m_conv_b_in, m_conv_dw, m_conv_ln_g, m_conv_ln_b,
                           m_conv_w_out, m_conv_b_out, m_kv_norm, m_w_kv, m_k_norm, m_w_q, m_q_norm, m_attn_sinks, m_w_o,
                           m_ffn_w_gate, m_ffn_w_up, m_ffn_w_down)))
    var = dict(zip(NAMES, (v_meta_tokens, v_norm_mix, v_norm_ffn, v_conv_w_in, v_conv_b_in, v_conv_dw, v_conv_ln_g, v_conv_ln_b,
                           v_conv_w_out, v_conv_b_out, v_kv_norm, v_w_kv, v_k_norm, v_w_q, v_q_norm, v_attn_sinks, v_w_o,
                           v_ffn_w_gate, v_ffn_w_up, v_ffn_w_down)))
    for k in TRANSPOSED:
        wts[k], mom[k], var[k] = (jnp.swapaxes(a, 1, 2) for a in (wts[k], mom[k], var[k]))

    big_names = list(BIG)
    layers = cast_bf16([wts[k] for k in big_names])
    shards = {(k, l): blk for k, per_layer in zip(big_names, layers) for l, blk in enumerate(per_layer)}
    vec_names = ["meta_tokens", "conv_b_in", "conv_dw", "conv_ln_g", "conv_ln_b", "conv_b_out"]
    full = dict(zip(vec_names, all_gather_blocks([wts[k] for k in vec_names])))
    join_vec = lambda a: jnp.moveaxis(a, 0, -2).reshape(a.shape[1:-1] + (NDEV * a.shape[-1],))
    w = {}
    w["conv_b_in"] = join_vec(full["conv_b_in"]).reshape(2, 1, 2 * D)
    w["conv_dw"] = join_vec(full["conv_dw"])
    for k in ("conv_ln_g", "conv_ln_b", "conv_b_out"):
        w[k] = join_vec(full[k]).reshape(2, 1, D)
    w["norm_mix"] = norm_mix.reshape(4, 1, D)
    w["norm_ffn"] = norm_ffn.reshape(4, 1, D)
    w["kv_norm"] = kv_norm.reshape(1, D)
    w["k_norm"] = k_norm.reshape(1, HD)
    w["q_norm"] = q_norm.reshape(2, 1, HD)
    w["attn_sinks"] = attn_sinks.reshape(2, 1, NH)

    loss_blk, grad_x, gbig, arrived, gs = local_step(x, loss_target, full["meta_tokens"], w, shards)

    packed = pack_small(gs, loss_blk)

    grads, delta, new_m, new_v = {}, {}, {}, {}
    tail = EXCHANGE_PLAN["tail"]
    waiting = {nm.rstrip("0123456789") for nm, _ in tail}
    order = sorted([k for k in big_names if k not in waiting], key=lambda k: -wts[k].size) + [k for k in big_names if k in waiting]
    small8 = None
    for pos, k in enumerate(order):
        flat2 = wts[k].ndim == 2
        as3 = (lambda a: a[None]) if flat2 else (lambda a: a)
        riders = tail if pos == 0 else []
        gat = [packed] if pos == 1 else []
        parts = [arrived[k]] if flat2 else [arrived[f"{k}{i}"] for i in range(wts[k].shape[0])]
        outs, got_x, got_g = adamw_big(as3(wts[k]), as3(mom[k]), as3(var[k]), parts, "adamw_" + k,
                                       xch=[(gbig[nm], EXCHANGE_KIND[nm.rstrip("0123456789")], ks) for nm, ks in riders], gat=gat)
        for (nm, _), got in zip(riders, got_x):
            arrived[nm].append(got)
        if gat:
            small8 = got_g[0]
        grads[k], delta[k], new_m[k], new_v[k] = [o[0] if flat2 else (jnp.swapaxes(o, 1, 2) if k in TRANSPOSED else o) for o in outs]
    loss, small = adamw_small(small8, wts, mom, var)
    for k, (g_, d_, m_, v_) in small.items():
        grads[k], delta[k], new_m[k], new_v[k] = g_, d_, m_, v_
    return (loss.reshape(()), grad_x, *[grads[k] for k in NAMES], *[delta[k] for k in NAMES], *[new_m[k] for k in NAMES],
            *[new_v[k] for k in NAMES])
```

```python
import functools

import jax
import jax.numpy as jnp
from jax import lax
from jax.experimental import pallas as pl
from jax.experimental.pallas import tpu as pltpu

F32 = jnp.float32
BF = jnp.bfloat16

D = 1024
DFF = 2816
NH = 16
NKV = 4
HD = 64
KVD = NKV * HD
NMETA = 16
CW = 31
HALO = 32
CHUNK = 32
QB = 128
EPS = 1e-6
NEG = -1e30
NDEV = 8
SCALE = HD ** -0.5

LR, B1, B2, AEPS, WD, STEP = 0.001, 0.9, 0.999, 1e-08, 0.01, 10

VMEM_LIMIT = 56 * 2 ** 20
MESH = pl.DeviceIdType.MESH


def _cp(n):
    return pltpu.CompilerParams(dimension_semantics=("arbitrary",) * n, vmem_limit_bytes=VMEM_LIMIT)


def _row(tm, c):
    return pl.BlockSpec((tm, c), lambda i: (i, 0))


def _res(shape):
    return pl.BlockSpec(shape, lambda i: (0,) * len(shape), pipeline_mode=pl.Buffered(1))


def _lay(l, shape):
    return pl.BlockSpec((None,) + tuple(shape), lambda i: (l,) + (0,) * len(shape), pipeline_mode=pl.Buffered(1))


def _acc(shape):
    return pl.BlockSpec(shape, lambda i: (0,) * len(shape))


def _sds(shape, dt):
    return jax.ShapeDtypeStruct(tuple(shape), dt)


def _dot(a, b):
    return jnp.dot(a.astype(BF), b.astype(BF), preferred_element_type=F32)


def _dot_nt(a, b):
    return lax.dot_general(a.astype(BF), b.astype(BF), (((1,), (1,)), ((), ())), preferred_element_type=F32)


def _dot_tn(a, b):
    return lax.dot_general(a.astype(BF), b.astype(BF), (((0,), (0,)), ((), ())), preferred_element_type=F32)


def _rstd(x):
    return lax.rsqrt(jnp.mean(x * x, axis=-1, keepdims=True) + EPS)


def _rms_bwd(x, g, dy):
    r = _rstd(x)
    z = dy * g
    dx = r * z - x * (r * r * r * jnp.mean(z * x, axis=-1, keepdims=True))
    return dx, jnp.sum(dy * x * r, axis=0, keepdims=True)


def _sig(x):
    return jax.nn.sigmoid(x)


def _fold8(x):
    out = x[0:8]
    for k in range(1, x.shape[0] // 8):
        out = out + x[8 * k:8 * k + 8]
    return out


def _shifted(win):
    return [win] + [pltpu.roll(win, 2 * CHUNK - rho, 0) for rho in range(1, 8)]


def _tap(phases, o):
    return phases[o % 8][8 * (o // 8):8 * (o // 8) + CHUNK]


class _lazy_loads:
    def __init__(self, srcs, dsts, sems):
        self.copies = [pltpu.make_async_copy(src, dst, sems.at[i]) for i, (src, dst) in enumerate(zip(srcs, dsts))]
        self.first = pl.program_id(0) == 0

    def start(self):
        @pl.when(self.first)
        def _():
            for cp in self.copies:
                cp.start()

    def wait(self, i):
        @pl.when(self.first)
        def _():
            self.copies[i].wait()


def _init(ref, first):
    @pl.when(first)
    def _():
        ref[...] = jnp.zeros_like(ref)


def _my_index():
    return 4 * lax.axis_index("x") + 2 * lax.axis_index("y") + lax.axis_index("c")


def _coords(idx):
    return (idx // 4, (idx // 2) % 2, idx % 2)


ALL = tuple(range(NDEV))
H1, H2 = (0, 1, 2, 4, 6), (3, 5, 7)


def _xch_shapes(xch):
    return [_sds((len(ks),) + ((a.shape[0] // NDEV, a.shape[1]) if k == "rows" else tuple(a.shape[1:])), a.dtype) for a, k, ks in xch]


def _xch_scratch(n):
    return [pltpu.SemaphoreType.DMA((n, NDEV)), pltpu.SemaphoreType.DMA((n, NDEV)), pltpu.SemaphoreType.DMA((n,))]


def _xch_copies(meta, srcs, outs, sems, arrivals):
    send_sems, recv_sems, local_sems = sems
    me = _my_index()

    def piece(a, p):
        if meta[a][0] == "rows":
            r = srcs[a].shape[0] // NDEV
            return srcs[a].at[pl.ds(p * r, r), :]
        return srcs[a].at[p]

    def remote(a, i, k, src):
        return pltpu.make_async_remote_copy(
            src_ref=src, dst_ref=outs[a].at[i], send_sem=send_sems.at[a, k], recv_sem=recv_sems.at[a, k],
            device_id=_coords(me ^ k), device_id_type=MESH)

    local, sends, recvs = [], [], []
    for a, (_, ks) in enumerate(meta):
        for i, k in enumerate(ks):
            if k == 0:
                local.append(pltpu.make_async_copy(piece(a, me), outs[a].at[i], local_sems.at[a]))
            else:
                sends.append(remote(a, i, k, piece(a, me ^ k)))
                if arrivals:
                    recvs.append(remote(a, i, k, piece(a, me)))
    return local, sends, recvs


def _xch_start(meta, srcs, outs, sems):
    local, sends, _ = _xch_copies(meta, srcs, outs, sems, False)
    for cp in local + sends:
        cp.start()


def _xch_wait(meta, srcs, outs, sems):
    local, sends, recvs = _xch_copies(meta, srcs, outs, sems, True)
    for cp in recvs:
        cp.wait_recv()
    for cp in sends:
        cp.wait_send()
    for cp in local:
        cp.wait()


def _gat_copies(srcs, outs, sems):
    send_sems, recv_sems, local_sems = sems
    x, y, c = lax.axis_index("x"), lax.axis_index("y"), lax.axis_index("c")
    me, sibling = (x, y, c), (x, y, 1 - c)
    chips = [(1 - x, y), (x, 1 - y), (1 - x, 1 - y)]

    def copy(a, k, owner, to, from_block=False):
        slot = outs[a].at[4 * owner[0] + 2 * owner[1] + owner[2]]
        return pltpu.make_async_remote_copy(
            src_ref=srcs[a] if from_block else slot, dst_ref=slot, send_sem=send_sems.at[a, k], recv_sem=recv_sems.at[a, k],
            device_id=to, device_id_type=MESH)

    n = len(srcs)
    local = lambda: [pltpu.make_async_copy(srcs[a], outs[a].at[4 * x + 2 * y + c], local_sems.at[a]) for a in range(n)]
    first = lambda: [cp for a in range(n) for cp in
                     [copy(a, 0, me, sibling, True)] + [copy(a, 1 + j, me, (*chip, c), True) for j, chip in enumerate(chips)]]
    landed = lambda: [copy(a, 1 + j, (*chip, c), me) for a in range(n) for j, chip in enumerate(chips)]
    passed = lambda: [copy(a, 4 + j, (*chip, c), sibling) for a in range(n) for j, chip in enumerate(chips)]
    final = lambda: [cp for a in range(n) for cp in
                     [copy(a, 0, sibling, me)] + [copy(a, 4 + j, (*chip, 1 - c), me) for j, chip in enumerate(chips)]]
    return local, first, landed, passed, final


def _gat_start(srcs, outs, sems):
    local, first, _, _, _ = _gat_copies(srcs, outs, sems)
    for cp in local() + first():
        cp.start()


def _gat_forward(srcs, outs, sems):
    _, _, landed, passed, _ = _gat_copies(srcs, outs, sems)
    for got, on in zip(landed(), passed()):
        got.wait_recv()
        on.start()


def _gat_wait(srcs, outs, sems):
    local, first, _, passed, final = _gat_copies(srcs, outs, sems)
    for cp in final():
        cp.wait_recv()
    for cp in first() + passed():
        cp.wait_send()
    for cp in local():
        cp.wait()


def _call(body, name, grid, in_specs, out_specs, out_shape, args, scratch=(), xch=(), gat=()):
    n_in, n_out, n_x, n_g, n_s = len(in_specs), len(out_specs), len(xch), len(gat), len(scratch)
    kinds = [(k, ks) for _, k, ks in xch]
    total = 1
    for g in grid:
        total *= g

    def wrapped(*refs):
        ins, refs = refs[:n_in], refs[n_in:]
        x_src, refs = refs[:n_x], refs[n_x:]
        g_src, refs = refs[:n_g], refs[n_g:]
        outs, refs = refs[:n_out], refs[n_out:]
        x_out, refs = refs[:n_x], refs[n_x:]
        g_out, refs = refs[:n_g], refs[n_g:]
        own, refs = refs[:n_s], refs[n_s:]
        x_sems, g_sems = (refs[:3], refs[3:]) if n_x else ((), refs)
        step = pl.program_id(0)
        for d in range(1, len(grid)):
            step = step * grid[d] + pl.program_id(d)
        if n_x or n_g:
            @pl.when(step == 0)
            def _():
                if n_x:
                    _xch_start(kinds, x_src, x_out, x_sems)
                if n_g:
                    _gat_start(g_src, g_out, g_sems)

        body(*ins, *outs, *own)
        if n_g:
            @pl.when(step == max(total - 2, 0))
            def _():
                _gat_forward(g_src, g_out, g_sems)

        if n_x or n_g:
            @pl.when(step == total - 1)
            def _():
                if n_x:
                    _xch_wait(kinds, x_src, x_out, x_sems)
                if n_g:
                    _gat_wait(g_src, g_out, g_sems)

    any_spec = pl.BlockSpec(memory_space=pl.ANY)
    g_shapes = [_sds((NDEV,) + tuple(a.shape), a.dtype) for a in gat]
    res = pl.pallas_call(
        wrapped, name=name, grid=grid, in_specs=list(in_specs) + [any_spec] * (n_x + n_g),
        out_specs=list(out_specs) + [any_spec] * (n_x + n_g), out_shape=list(out_shape) + _xch_shapes(xch) + g_shapes,
        scratch_shapes=list(scratch) + (_xch_scratch(n_x) if n_x else []) + (_xch_scratch(n_g) if n_g else []),
        compiler_params=_cp(len(grid)),
    )(*args, *[a for a, _, _ in xch], *gat)
    return res[:n_out], res[n_out:n_out + n_x], res[n_out + n_x:]


def embed(x, meta8, lp, gat):
    bl, seq, _ = x.shape
    c8 = D // NDEV
    cb = 2 * c8

    def body(x_ref, m_ref, h_ref):
        h_ref[0:NMETA, :] = jnp.concatenate([m_ref[0], m_ref[1]], axis=1)
        h_ref[NMETA:NMETA + seq, :] = x_ref[...]
        h_ref[NMETA + seq:, :] = jnp.zeros((lp - NMETA - seq, cb), F32)

    (h0,), _, got = _call(
        body, "embed", (bl, D // cb),
        [pl.BlockSpec((None, seq, cb), lambda b, c: (b, 0, c)), pl.BlockSpec((2, NMETA, c8), lambda b, c: (c, 0, 0))],
        [pl.BlockSpec((None, lp, cb), lambda b, c: (b, 0, c))], [_sds((bl, lp, D), F32)], (x, meta8), gat=gat)
    return h0, got


def conv_in_fwd(h, nm, l, w_in, b_in, i, tm):
    t = h.shape[0]

    def body(h_ref, g_ref, w_ref, b_ref, u_ref, big_ref, a_ref):
        x = h_ref[...]
        ub = (x * _rstd(x) * g_ref[...]).astype(BF)
        u_ref[...] = ub
        big = jnp.dot(ub, w_ref[...], preferred_element_type=F32) + b_ref[...]
        big_ref[...] = big.astype(BF)
        a_ref[...] = big[:, :D] * _sig(big[:, D:])

    return pl.pallas_call(
        body, name=f"conv_in_fwd{i}", grid=(t // tm,),
        in_specs=[_row(tm, D), _lay(l, (1, D)), _res((D, 2 * D)), _lay(i, (1, 2 * D))],
        out_specs=[_row(tm, D), _row(tm, 2 * D), _row(tm, D)],
        out_shape=[_sds((t, D), BF), _sds((t, 2 * D), BF), _sds((t, D), F32)],
        compiler_params=_cp(1),
    )(h, nm, w_in, b_in)


def _prev_halo(tm):
    return pl.BlockSpec((HALO, D), lambda i: (jnp.maximum(i * (tm // HALO) - 1, 0), 0))


def _next_halo(tm, t):
    return pl.BlockSpec((HALO, D), lambda i: (jnp.minimum((i + 1) * (tm // HALO), t // HALO - 1), 0))


def conv_mid_fwd(a, dw, ln_g, ln_b, i, tm, tpb, gat):
    t = a.shape[0]

    def body(a_ref, halo_ref, dw_ref, g_ref, b_ref, c_ref, s_ref, ext):
        first = pl.program_id(0) % tpb == 0
        ext[0:HALO] = jnp.where(first, 0.0, halo_ref[...])
        ext[HALO:] = a_ref[...]

        def chunk(k, carry):
            r0 = pl.multiple_of(k * CHUNK, CHUNK)
            win = _shifted(ext[pl.ds(r0, 2 * CHUNK), :])
            c = jnp.zeros((CHUNK, D), F32)
            for j in range(CW):
                c = c + dw_ref[j:j + 1, :] * _tap(win, j + 2)
            c_ref[pl.ds(r0, CHUNK), :] = c
            mu = jnp.mean(c, axis=-1, keepdims=True)
            xc = c - mu
            n = xc * lax.rsqrt(jnp.mean(xc * xc, axis=-1, keepdims=True) + EPS) * g_ref[...] + b_ref[...]
            s_ref[pl.ds(r0, CHUNK), :] = (n * _sig(n)).astype(BF)
            return carry

        lax.fori_loop(0, tm // CHUNK, chunk, 0)

    return _call(
        body, f"conv_mid_fwd{i}", (t // tm,),
        [_row(tm, D), _prev_halo(tm), _lay(i, (CW, D)), _lay(i, (1, D)), _lay(i, (1, D))],
        [_row(tm, D), _row(tm, D)], [_sds((t, D), F32), _sds((t, D), BF)], (a, a, dw, ln_g, ln_b),
        scratch=[pltpu.VMEM((tm + HALO, D), F32)], gat=gat)


def mixer_ffn_fwd(h, s, w_out, lw, bias, nf, l, wg, wu, wd, tm, gat):
    t = h.shape[0]

    def body(*refs):
        if bias is None:
            h_ref, s_ref, nf_ref, w_hbm, wg_hbm, wu_hbm, wd_hbm, h1_ref, u_ref, g_ref, up_ref, hid_ref, h2_ref = refs[:13]
            y = 0.0
        else:
            h_ref, s_ref, b_ref, nf_ref, w_hbm, wg_hbm, wu_hbm, wd_hbm, h1_ref, u_ref, g_ref, up_ref, hid_ref, h2_ref = refs[:14]
            y = b_ref[...]
        w_ref, wg_ref, wu_ref, wd_ref, sems = refs[-5:]
        loads = _lazy_loads((w_hbm, wg_hbm, wu_hbm, wd_hbm), (w_ref, wg_ref, wu_ref, wd_ref), sems)
        loads.start()
        loads.wait(0)
        h1 = h_ref[...] + (jnp.dot(s_ref[...], w_ref[...], preferred_element_type=F32) + y)
        h1_ref[...] = h1
        ub = (h1 * _rstd(h1) * nf_ref[...]).astype(BF)
        u_ref[...] = ub
        loads.wait(1)
        g = _dot_nt(ub, wg_ref[...])
        loads.wait(2)
        up = _dot_nt(ub, wu_ref[...])
        g_ref[...] = g.astype(BF)
        up_ref[...] = up.astype(BF)
        hid = (g * _sig(g) * up).astype(BF)
        hid_ref[...] = hid
        loads.wait(3)
        h2_ref[...] = h1 + jnp.dot(hid, wd_ref[...], preferred_element_type=F32)

    hbm = pl.BlockSpec(memory_space=pl.ANY)
    ins = [h, s] + ([] if bias is None else [bias]) + [nf, w_out, wg, wu, wd]
    specs = [_row(tm, D), _row(tm, D)] + ([] if bias is None else [_lay(lw, (1, D))]) + [_lay(l, (1, D)), hbm, hbm, hbm, hbm]
    return _call(
        body, f"mixer_ffn_fwd{l}", (t // tm,), specs,
        [_row(tm, D), _row(tm, D), _row(tm, DFF), _row(tm, DFF), _row(tm, DFF), _row(tm, D)],
        [_sds((t, D), F32), _sds((t, D), BF), _sds((t, DFF), BF), _sds((t, DFF), BF), _sds((t, DFF), BF), _sds((t, D), F32)],
        ins, scratch=[pltpu.VMEM((D, D), BF), pltpu.VMEM((DFF, D), BF), pltpu.VMEM((DFF, D), BF), pltpu.VMEM((DFF, D), BF),
                      pltpu.SemaphoreType.DMA((4,))], gat=gat)


def _seg_rms(x, g, nseg):
    outs = []
    for s in range(nseg):
        xs = x[:, HD * s:HD * s + HD]
        outs.append(xs * _rstd(xs) * g)
    return jnp.concatenate(outs, axis=1)


def kv_fwd(h, kvn, w_kv, kng, tm):
    t = h.shape[0]

    def body(h_ref, g_ref, w_ref, kg_ref, kn_ref, kv_ref, k_ref, v_ref):
        x = h_ref[...]
        kn = (x * _rstd(x) * g_ref[...]).astype(BF)
        kn_ref[...] = kn
        kv = jnp.dot(kn, w_ref[...], preferred_element_type=F32)
        kv_ref[...] = kv
        k_ref[...] = _seg_rms(kv[:, :KVD], kg_ref[...], NKV).astype(BF)
        v_ref[...] = kv[:, KVD:].astype(BF)

    return pl.pallas_call(
        body, name="kv_fwd", grid=(t // tm,),
        in_specs=[_row(tm, D), _res((1, D)), _res((D, 2 * KVD)), _res((1, HD))],
        out_specs=[_row(tm, D), _row(tm, 2 * KVD), _row(tm, KVD), _row(tm, KVD)],
        out_shape=[_sds((t, D), BF), _sds((t, 2 * KVD), F32), _sds((t, KVD), BF), _sds((t, KVD), BF)],
        compiler_params=_cp(1),
    )(h, kvn, w_kv, kng)


def q_fwd(h, nm, l, w_q, j, tm):
    t = h.shape[0]

    def body(h_ref, g_ref, w_ref, u_ref, q_ref):
        x = h_ref[...]
        ub = (x * _rstd(x) * g_ref[...]).astype(BF)
        u_ref[...] = ub
        q_ref[...] = jnp.dot(ub, w_ref[...], preferred_element_type=F32)

    return pl.pallas_call(
        body, name=f"q_fwd{j}", grid=(t // tm,),
        in_specs=[_row(tm, D), _lay(l, (1, D)), _res((D, D))],
        out_specs=[_row(tm, D), _row(tm, D)], out_shape=[_sds((t, D), BF), _sds((t, D), F32)],
        compiler_params=_cp(1),
    )(h, nm, w_q)


RQ = NH // NKV


def _attn_specs(nb, lp):
    cur = lambda c: pl.BlockSpec((QB, c), lambda b, n: (b * nb + n, 0))
    seq = pl.BlockSpec((None, lp, KVD), lambda b, n: (b, 0, 0))
    seq_t = pl.BlockSpec((None, KVD, lp), lambda b, n: (b, 0, 0))
    return cur, seq, seq_t


NKEYS = 2 * QB + NMETA


def _attn_mask(n, start):
    shape = (RQ * QB, NKEYS)
    qpos = n * QB + (lax.broadcasted_iota(jnp.int32, shape, 0) & (QB - 1))
    col = lax.broadcasted_iota(jnp.int32, shape, 1)
    in_band = col < 2 * QB
    kpos = jnp.where(in_band, start + col, col - 2 * QB)
    return (kpos <= qpos) & ((col >= 2 * QB) | ((qpos - kpos < QB) & (kpos >= NMETA)))


def _keys(ref, band, gs):
    return jnp.concatenate([ref[band, gs], ref[0:NMETA, gs]], axis=0)


def _keys_t(ref, band, gs):
    return jnp.concatenate([ref[gs, band], ref[gs, 0:NMETA]], axis=1)


def transpose_seq(a, name):
    bl, r, c = a.shape

    def body(a_ref, o_ref):
        o_ref[...] = a_ref[...].T

    return pl.pallas_call(
        body, name=name, grid=(bl,), in_specs=[pl.BlockSpec((None, r, c), lambda b: (b, 0, 0))],
        out_specs=pl.BlockSpec((None, c, r), lambda b: (b, 0, 0)), out_shape=_sds((bl, c, r), a.dtype), compiler_params=_cp(1),
    )(a)


def sum_transposed(a0, a1):
    bl, c, r = a0.shape

    def body(a0_ref, a1_ref, o_ref):
        o_ref[...] = (a0_ref[...] + a1_ref[...]).T

    spec = pl.BlockSpec((None, c, r), lambda b: (b, 0, 0))
    return pl.pallas_call(
        body, name="sum_transposed", grid=(bl,), in_specs=[spec, spec],
        out_specs=pl.BlockSpec((r, c), lambda b: (b, 0)), out_shape=_sds((bl * r, c), a0.dtype), compiler_params=_cp(1),
    )(a0, a1)


def _stack_heads(ref, g, fn):
    return jnp.concatenate([fn(ref[:, HD * (g * RQ + r):HD * (g * RQ + r) + HD]) for r in range(RQ)], axis=0)


def _stack_cols(ref, g):
    return jnp.concatenate([ref[:, g * RQ + r:g * RQ + r + 1] for r in range(RQ)], axis=0)


def _stack_sinks(sk_ref, g):
    return jnp.concatenate([jnp.broadcast_to(sk_ref[:, g * RQ + r:g * RQ + r + 1], (QB, 1)) for r in range(RQ)], axis=0)


def attn_fwd(q, kt, v, qg, sinks, j, bl, lp, gat):
    t = q.shape[0]
    nb = lp // QB
    cur, seq, seq_t = _attn_specs(nb, lp)

    def body(q_ref, kt_ref, v_ref, qg_ref, sk_ref, o_ref, lse_ref):
        n = pl.program_id(1)
        start = pl.multiple_of(jnp.maximum(n - 1, 0) * QB, QB)
        mask = _attn_mask(n, start)
        band = pl.ds(start, 2 * QB)
        lane = lax.broadcasted_iota(jnp.int32, (QB, NH), 1)
        ones = jnp.ones((NKEYS, HD), BF)
        lse = jnp.zeros((QB, NH), F32)
        groups = range(NKV)
        gsl = [slice(HD * g, HD * g + HD) for g in groups]
        qns = [_stack_heads(q_ref, g, lambda x: (x * _rstd(x) * (qg_ref[...] * SCALE)).astype(BF)) for g in groups]
        ss = [jnp.where(mask, _dot(qns[g], _keys_t(kt_ref, band, gsl[g])), NEG) for g in groups]
        sinks = [_stack_sinks(sk_ref, g) for g in groups]
        mxs = [jnp.maximum(jnp.max(ss[g], -1, keepdims=True), sinks[g]) for g in groups]
        oas = [_dot(jnp.exp(ss[g] - mxs[g]), jnp.concatenate([_keys(v_ref, band, gsl[g]), ones], axis=1)) for g in groups]
        for g in groups:
            den = oas[g][:, HD:HD + 1] + jnp.exp(sinks[g] - mxs[g])
            o = oas[g][:, :HD] * (1.0 / den)
            l = mxs[g] + jnp.log(den)
            for r in range(RQ):
                h = g * RQ + r
                o_ref[:, HD * h:HD * h + HD] = o[r * QB:(r + 1) * QB].astype(BF)
                lse = jnp.where(lane == h, l[r * QB:(r + 1) * QB], lse)
        lse_ref[...] = lse

    return _call(
        body, f"attn_fwd{j}", (bl, nb),
        [cur(D), seq_t, seq, pl.BlockSpec((None, 1, HD), lambda b, n: (j, 0, 0)), pl.BlockSpec((None, 1, NH), lambda b, n: (j, 0, 0))],
        [cur(D), cur(NH)], [_sds((t, D), BF), _sds((t, NH), F32)], (q, kt, v, qg, sinks), gat=gat)


def loss_fwd(h, tgt):
    bl, lp, _ = h.shape
    seq = tgt.shape[1]
    cb = 256

    def body(h_ref, t_ref, dh_ref, loss_ref):
        _init(loss_ref, (pl.program_id(0) == 0) & (pl.program_id(1) == 0))
        err = h_ref[NMETA:NMETA + seq, :] - t_ref[...]
        dh_ref[...] = jnp.zeros_like(dh_ref)
        dh_ref[NMETA:NMETA + seq, :] = err * (1.0 / D)
        loss_ref[...] += (0.5 / D) * jnp.sum(err * err)

    return pl.pallas_call(
        body, name="loss_fwd", grid=(bl, D // cb),
        in_specs=[pl.BlockSpec((None, lp, cb), lambda b, c: (b, 0, c)), pl.BlockSpec((None, seq, cb), lambda b, c: (b, 0, c))],
        out_specs=[pl.BlockSpec((None, lp, cb), lambda b, c: (b, 0, c)), pl.BlockSpec((8, 128), lambda b, c: (0, 0))],
        out_shape=[_sds((bl, lp, D), F32), _sds((8, 128), F32)],
        compiler_params=_cp(2),
    )(h, tgt)


def ffn_bwd_x(dh2, g, up, h1, nf, l, wd, wg, wu, w_o, tm, xch):
    t = dh2.shape[0]

    attn = w_o is not None
    nw = 4 if attn else 3

    def body(dh2_ref, g_ref, up_ref, h1_ref, nf_ref, *rest):
        hbm, rest = rest[:nw], rest[nw:]
        if attn:
            dg_ref, du_ref, dh1_ref, dnf_ref, do_ref = rest[:5]
        else:
            dg_ref, du_ref, dh1_ref, dnf_ref = rest[:4]
        vmem, sems = rest[-nw - 1:-1], rest[-1]
        wd_ref, wg_ref, wu_ref = vmem[:3]
        loads = _lazy_loads(hbm, vmem, sems)
        loads.start()
        _init(dnf_ref, pl.program_id(0) == 0)
        dh2v = dh2_ref[...]
        loads.wait(0)
        dhid = _dot_nt(dh2v, wd_ref[...])
        gv = g_ref[...].astype(F32)
        uv = up_ref[...].astype(F32)
        sg = _sig(gv)
        dgv = (dhid * uv * (sg * (1.0 + gv * (1.0 - sg)))).astype(BF)
        duv = (dhid * (gv * sg)).astype(BF)
        dg_ref[...] = dgv
        du_ref[...] = duv
        loads.wait(1)
        loads.wait(2)
        dnorm = _dot(dgv, wg_ref[...]) + _dot(duv, wu_ref[...])
        dx, dnf = _rms_bwd(h1_ref[...], nf_ref[...], dnorm)
        dh1 = dh2v + dx
        dh1_ref[...] = dh1
        dnf_ref[...] += dnf
        if attn:
            loads.wait(3)
            do_ref[...] = _dot_nt(dh1, vmem[3][...]).astype(BF)

    any_spec = pl.BlockSpec(memory_space=pl.ANY)
    return _call(
        body, f"ffn_bwd_x{l}", (t // tm,),
        [_row(tm, D), _row(tm, DFF), _row(tm, DFF), _row(tm, D), _lay(l, (1, D))] + [any_spec] * nw,
        [_row(tm, DFF), _row(tm, DFF), _row(tm, D), _acc((1, D))] + ([_row(tm, D)] if attn else []),
        [_sds((t, DFF), BF), _sds((t, DFF), BF), _sds((t, D), F32), _sds((1, D), F32)] + ([_sds((t, D), BF)] if attn else []),
        (dh2, g, up, h1, nf, wd, wg, wu) + ((w_o,) if attn else ()),
        scratch=[pltpu.VMEM((DFF, D), BF)] * 3 + ([pltpu.VMEM((D, D), BF)] if attn else []) + [pltpu.SemaphoreType.DMA((nw,))],
        xch=xch)


def mm_tn(x, dy, tm, name, split=False, transposed=False, xch=()):
    t, kk = x.shape
    nn = dy.shape[1]
    n8 = nn // NDEV
    nsteps = t // tm

    def body(x_ref, dy_ref, o_ref, acc):
        i = pl.program_id(0)
        _init(acc, i == 0)
        acc[...] += _dot_tn(x_ref[...], dy_ref[...])

        @pl.when(i == nsteps - 1)
        def _():
            if split:
                for p in range(NDEV):
                    o_ref[p] = acc[:, p * n8:(p + 1) * n8].astype(BF)
            elif transposed:
                o_ref[...] = acc[...].T.astype(BF)
            else:
                o_ref[...] = acc[...].astype(BF)

    oshape = (NDEV, kk, n8) if split else ((nn, kk) if transposed else (kk, nn))
    (out,), got, _ = _call(body, name, (nsteps,), [_row(tm, kk), _row(tm, nn)], [_acc(oshape)], [_sds(oshape, BF)], (x, dy),
                           scratch=[pltpu.VMEM((kk, nn), F32)], xch=xch)
    return out, got


def proj_bwd(dy, w, h, g, lg, dh_in, tm, name, xch=()):
    t = h.shape[0]
    nn = dy.shape[1]
    wspec = _res(w.shape)
    gspec = _res((1, D)) if lg is None else _lay(lg, (1, D))

    def body(dy_ref, w_ref, h_ref, g_ref, dhin_ref, dh_ref, dg_ref):
        _init(dg_ref, pl.program_id(0) == 0)
        du = _dot_nt(dy_ref[...], w_ref[...])
        dx, dg = _rms_bwd(h_ref[...], g_ref[...], du)
        dh_ref[...] = dhin_ref[...] + dx
        dg_ref[...] += dg

    return _call(body, name, (t // tm,), [_row(tm, nn), wspec, _row(tm, D), gspec, _row(tm, D)],
                 [_row(tm, D), _acc((1, D))], [_sds((t, D), F32), _sds((1, D), F32)], (dy, w, h, g, dh_in), xch=xch)


def attn_bwd(q, k, kt, vt, do, o, lse, qg, sinks, j, bl, lp, xch):
    t = q.shape[0]
    nb = lp // QB
    cur, seq, seq_t = _attn_specs(nb, lp)

    def body(q_ref, k_ref, kt_ref, vt_ref, do_ref, o_ref, lse_ref, qg_ref, sk_ref, dq_ref, dk_ref, dv_ref, dqg_ref, dsk_ref):
        b, n = pl.program_id(0), pl.program_id(1)
        _init(dk_ref, n == 0)
        _init(dv_ref, n == 0)
        _init(dqg_ref, (b == 0) & (n == 0))
        _init(dsk_ref, (b == 0) & (n == 0))
        start = pl.multiple_of(jnp.maximum(n - 1, 0) * QB, QB)
        mask = _attn_mask(n, start)
        band = pl.ds(start, 2 * QB)
        lane = lax.broadcasted_iota(jnp.int32, (1, NH), 1)
        dqg = jnp.zeros((1, HD), F32)
        dsk = jnp.zeros((1, NH), F32)
        groups = range(NKV)
        gsl = [slice(HD * g, HD * g + HD) for g in groups]
        qhs = [_stack_heads(q_ref, g, lambda x: x) for g in groups]
        rss = [_rstd(qhs[g]) for g in groups]
        qns = [(qhs[g] * rss[g] * (qg_ref[...] * SCALE)).astype(BF) for g in groups]
        lss = [_stack_cols(lse_ref, g) for g in groups]
        dohs = [_stack_heads(do_ref, g, lambda x: x) for g in groups]
        deltas = [jnp.sum(dohs[g].astype(F32) * _stack_heads(o_ref, g, lambda x: x).astype(F32), axis=-1, keepdims=True)
                  for g in groups]
        prs = [jnp.where(mask, jnp.exp(_dot(qns[g], _keys_t(kt_ref, band, gsl[g])) - lss[g]), 0.0) for g in groups]
        dss = [(prs[g] * (_dot(dohs[g], _keys_t(vt_ref, band, gsl[g])) - deltas[g])).astype(BF) for g in groups]
        for g in groups:
            gs = gsl[g]
            dkt = _dot_tn(qns[g], dss[g])
            dvt = _dot_tn(dohs[g], prs[g])
            dk_ref[gs, band] += dkt[:, :2 * QB]
            dv_ref[gs, band] += dvt[:, :2 * QB]
            dk_ref[gs, 0:NMETA] += dkt[:, 2 * QB:]
            dv_ref[gs, 0:NMETA] += dvt[:, 2 * QB:]
        dqns = [_dot(dss[g], _keys(k_ref, band, gsl[g])) * SCALE for g in groups]
        for g in groups:
            qh, rs, dqn = qhs[g], rss[g], dqns[g]
            dsink = jnp.exp(_stack_sinks(sk_ref, g) - lss[g]) * deltas[g]
            z = dqn * qg_ref[...]
            dq = rs * z - qh * (rs * rs * rs * jnp.mean(z * qh, axis=-1, keepdims=True))
            dqg = dqg + jnp.sum(dqn * qh * rs, axis=0, keepdims=True)
            for r in range(RQ):
                h = g * RQ + r
                dq_ref[:, HD * h:HD * h + HD] = dq[r * QB:(r + 1) * QB]
                dsk = dsk + jnp.where(lane == h, -jnp.sum(dsink[r * QB:(r + 1) * QB]), 0.0)
        dqg_ref[...] += dqg
        dsk_ref[...] += dsk

    return _call(
        body, f"attn_bwd{j}", (bl, nb),
        [cur(D), seq, seq_t, seq_t, cur(D), cur(D), cur(NH),
         pl.BlockSpec((None, 1, HD), lambda b, n: (j, 0, 0)), pl.BlockSpec((None, 1, NH), lambda b, n: (j, 0, 0))],
        [cur(D), seq_t, seq_t, pl.BlockSpec((1, HD), lambda b, n: (0, 0)), pl.BlockSpec((1, NH), lambda b, n: (0, 0))],
        [_sds((t, D), F32), _sds((bl, KVD, lp), F32), _sds((bl, KVD, lp), F32), _sds((1, HD), F32), _sds((1, NH), F32)],
        (q, k, kt, vt, do, o, lse, qg, sinks), xch=xch)


def kv_bwd_pre(dk, dv, kv, kng, tm):
    t = kv.shape[0]

    def body(dk_ref, dv_ref, kv_ref, g_ref, dkv_ref, dg_ref):
        _init(dg_ref, pl.program_id(0) == 0)
        dg = jnp.zeros((1, HD), F32)
        outs = []
        for s in range(NKV):
            sl = slice(HD * s, HD * s + HD)
            dx, dgs = _rms_bwd(kv_ref[:, sl], g_ref[...], dk_ref[:, sl])
            outs.append(dx)
            dg = dg + dgs
        dkv_ref[:, :KVD] = jnp.concatenate(outs, axis=1).astype(BF)
        dkv_ref[:, KVD:] = dv_ref[...].astype(BF)
        dg_ref[...] += dg

    return pl.pallas_call(
        body, name="kv_bwd_pre", grid=(t // tm,),
        in_specs=[_row(tm, KVD)] * 2 + [_row(tm, 2 * KVD), _res((1, HD))],
        out_specs=[_row(tm, 2 * KVD), _acc((1, HD))], out_shape=[_sds((t, 2 * KVD), BF), _sds((1, HD), F32)],
        compiler_params=_cp(1),
    )(dk, dv, kv, kng)


def conv_out_bwd(dh1, c, ln_g, ln_b, w_out, i, tm, xch):
    t = dh1.shape[0]

    def body(dh1_ref, c_ref, g_ref, b_ref, w_ref, dc_ref, dg_ref, db_ref, dbo_ref):
        first = pl.program_id(0) == 0
        _init(dg_ref, first)
        _init(db_ref, first)
        _init(dbo_ref, first)
        dh1v = dh1_ref[...]
        ds = _dot_nt(dh1v, w_ref[...])
        cv = c_ref[...]
        xc = cv - jnp.mean(cv, axis=-1, keepdims=True)
        rstd = lax.rsqrt(jnp.mean(xc * xc, axis=-1, keepdims=True) + EPS)
        xh = xc * rstd
        n = xh * g_ref[...] + b_ref[...]
        sg = _sig(n)
        dn = ds * (sg * (1.0 + n * (1.0 - sg)))
        dxh = dn * g_ref[...]
        dc_ref[...] = rstd * (dxh - jnp.mean(dxh, axis=-1, keepdims=True) - xh * jnp.mean(dxh * xh, axis=-1, keepdims=True))
        dg_ref[...] += jnp.sum(dn * xh, axis=0, keepdims=True)
        db_ref[...] += jnp.sum(dn, axis=0, keepdims=True)
        dbo_ref[...] += jnp.sum(dh1v, axis=0, keepdims=True)

    return _call(
        body, f"conv_out_bwd{i}", (t // tm,), [_row(tm, D), _row(tm, D), _lay(i, (1, D)), _lay(i, (1, D)), _res((D, D))],
        [_row(tm, D), _acc((1, D)), _acc((1, D)), _acc((1, D))], [_sds((t, D), F32)] + [_sds((1, D), F32)] * 3,
        (dh1, c, ln_g, ln_b, w_out), xch=xch)


def conv_mid_bwd(dc, a, big, dw, i, tm, tpb, xch):
    t = dc.shape[0]
    nsteps = t // tm

    def body(dc_ref, nxt_ref, a_ref, prv_ref, big_ref, dw_ref, da_ref, dbin_ref, ddw_ref, dce, ae, wacc, bacc):
        i_ = pl.program_id(0)
        _init(wacc, i_ == 0)
        _init(bacc, i_ == 0)
        dce[0:tm] = dc_ref[...]
        dce[tm:] = jnp.where(i_ % tpb == tpb - 1, 0.0, nxt_ref[...])
        ae[0:HALO] = jnp.where(i_ % tpb == 0, 0.0, prv_ref[...])
        ae[HALO:] = a_ref[...]

        def chunk(k, carry):
            r0 = pl.multiple_of(k * CHUNK, CHUNK)
            wdc = _shifted(dce[pl.ds(r0, 2 * CHUNK), :])
            wa = _shifted(ae[pl.ds(r0, 2 * CHUNK), :])
            dcc = wdc[0][0:CHUNK]
            da = jnp.zeros((CHUNK, D), F32)
            for j in range(CW):
                da = da + dw_ref[j:j + 1, :] * _tap(wdc, CW - 1 - j)
                wacc[j] += _fold8(dcc * _tap(wa, j + 2))
            bv = big_ref[pl.ds(r0, CHUNK), :].astype(F32)
            a1, sg = bv[:, :D], _sig(bv[:, D:])
            d1 = da * sg
            d2 = da * a1 * sg * (1.0 - sg)
            da_ref[pl.ds(r0, CHUNK), 0:D] = d1.astype(BF)
            da_ref[pl.ds(r0, CHUNK), D:2 * D] = d2.astype(BF)
            bacc[:, 0:D] += _fold8(d1)
            bacc[:, D:2 * D] += _fold8(d2)
            return carry

        lax.fori_loop(0, tm // CHUNK, chunk, 0)

        @pl.when(i_ == nsteps - 1)
        def _():
            dbin_ref[...] = jnp.sum(bacc[...], axis=0, keepdims=True)
            ddw_ref[...] = jnp.sum(wacc[...], axis=1)

    return _call(
        body, f"conv_mid_bwd{i}", (nsteps,),
        [_row(tm, D), _next_halo(tm, t), _row(tm, D), _prev_halo(tm), _row(tm, 2 * D), _lay(i, (CW, D))],
        [_row(tm, 2 * D), _acc((1, 2 * D)), _acc((CW + 1, D))],
        [_sds((t, 2 * D), BF), _sds((1, 2 * D), F32), _sds((CW + 1, D), F32)],
        (dc, dc, a, a, big, dw),
        scratch=[pltpu.VMEM((tm + HALO, D), F32), pltpu.VMEM((tm + HALO, D), F32),
                 pltpu.VMEM((CW + 1, 8, D), F32), pltpu.VMEM((8, 2 * D), F32)], xch=xch)


def input_grads(dh0, seq):
    bl, lp, _ = dh0.shape
    cb = 256

    def body(dh_ref, gx_ref, gm_ref):
        _init(gm_ref, pl.program_id(1) == 0)
        gx_ref[...] = dh_ref[NMETA:NMETA + seq, :]
        gm_ref[...] += dh_ref[0:NMETA, :]

    return pl.pallas_call(
        body, name="input_grads", grid=(D // cb, bl),
        in_specs=[pl.BlockSpec((None, lp, cb), lambda c, b: (b, 0, c))],
        out_specs=[pl.BlockSpec((None, seq, cb), lambda c, b: (b, 0, c)), pl.BlockSpec((NMETA, cb), lambda c, b: (0, c))],
        out_shape=[_sds((bl, seq, D), F32), _sds((NMETA, D), F32)],
        compiler_params=_cp(2),
    )(dh0)


GATHER_PLAN = {
    "embed": [("conv_w_in", 0), ("conv_w_out", 0)],
    "conv_mid_fwd0": [("ffn_w_gate", 0), ("ffn_w_up", 0), ("ffn_w_down", 0)],
    "mixer_ffn_fwd0": [("conv_w_in", 1), ("conv_w_out", 1), ("ffn_w_gate", 1)],
    "conv_mid_fwd1": [("ffn_w_up", 1), ("ffn_w_down", 1), ("w_kv", 0), ("w_q", 0)],
    "mixer_ffn_fwd1": [("w_o", 0), ("ffn_w_down", 2)],
    "attn_fwd0": [("ffn_w_gate", 2), ("ffn_w_up", 2), ("w_q", 1), ("w_o", 1)],
    "attn_fwd1": [("ffn_w_gate", 3), ("ffn_w_up", 3), ("ffn_w_down", 3)],
}
EXCHANGE_PLAN = {
    "attn_bwd1": [("ffn_w_down3", ALL), ("ffn_w_gate3", ALL), ("ffn_w_up3", H1)],
    "dw_down2": [("w_o1", ALL)],
    "ffn_bwd_x2": [("ffn_w_up3", H2), ("w_q1", ALL)],
    "attn_bwd0": [("ffn_w_down2", ALL), ("ffn_w_gate2", ALL), ("ffn_w_up2", H1)],
    "dw_down1": [("w_o0", ALL), ("w_q0", H1)],
    "ffn_bwd_x1": [("ffn_w_up2", H2), ("w_q0", H2), ("w_kv", ALL)],
    "dw_gate1": [("ffn_w_down1", H1)],
    "dw_up1": [("ffn_w_down1", H2)],
    "conv_mid_bwd1": [("ffn_w_gate1", ALL), ("ffn_w_up1", H1)],
    "conv_in_bwd1": [("ffn_w_up1", H2)],
    "dw_down0": [("conv_w_out1", ALL)],
    "ffn_bwd_x0": [("conv_w_in1", ALL)],
    "dw_gate0": [("ffn_w_down0", H1)],
    "dw_up0": [("ffn_w_down0", H2)],
    "conv_out_bwd0": [("ffn_w_gate0", H1)],
    "conv_mid_bwd0": [("ffn_w_gate0", H2), ("ffn_w_up0", H1), ("conv_w_out0", ALL)],
    "dw_conv_in0": [("ffn_w_up0", H2)],
    "conv_in_bwd0": [("conv_w_in0", H1)],
    "tail": [("conv_w_in0", H2)],
}
BIG = {"conv_w_in": "pieces", "conv_w_out": "rows", "w_kv": "rows", "w_q": "rows", "w_o": "rows",
       "ffn_w_gate": "rows", "ffn_w_up": "rows", "ffn_w_down": "rows"}
EXCHANGE_KIND = BIG
TRANSPOSED = ("ffn_w_gate", "ffn_w_up")


def gathered_matrix(name, layer, blocks8):
    if BIG[name] == "rows":
        return blocks8.reshape(NDEV * blocks8.shape[1], blocks8.shape[2])
    return join_columns(blocks8, f"join_{name}{layer}")


def local_step(x, tgt, meta8, w, shards):
    bl, seq, _ = x.shape
    lp = -(-(NMETA + seq) // QB) * QB
    tpb = 4
    tm = lp // tpb
    t = bl * lp
    na = 2
    flat = lambda a: a.reshape(t, D)
    mats = {}

    def riders(carrier):
        return [shards[key] for key in GATHER_PLAN[carrier]]

    def landed(carrier, blocks):
        for key, b8 in zip(GATHER_PLAN[carrier], blocks):
            mats[key] = gathered_matrix(*key, b8)

    h0, got = embed(x, meta8, lp, riders("embed"))
    landed("embed", got)
    h = flat(h0)
    saved = []
    kvs = None
    for l in range(4):
        rec = {"h": h}
        if l < na:
            rec["u"], rec["big"], rec["a"] = conv_in_fwd(h, w["norm_mix"], l, mats["conv_w_in", l], w["conv_b_in"], l, tm)
            name = f"conv_mid_fwd{l}"
            (rec["c"], rec["s"]), _, got = conv_mid_fwd(rec["a"], w["conv_dw"], w["conv_ln_g"], w["conv_ln_b"], l, tm, tpb,
                                                         riders(name))
            landed(name, got)
            mixed, w_out, lw, bias = rec["s"], mats["conv_w_out", l], l, w["conv_b_out"]
        else:
            j = l - na
            if kvs is None:
                kvs = dict(zip(("kn", "kv", "k", "v"), kv_fwd(h, w["kv_norm"], mats["w_kv", 0], w["k_norm"], tm)))
                kvs["h"] = h
                kvs["k3"], kvs["v3"] = kvs["k"].reshape(bl, lp, KVD), kvs["v"].reshape(bl, lp, KVD)
                kvs["kt"], kvs["vt"] = transpose_seq(kvs["k3"], "transpose_k"), transpose_seq(kvs["v3"], "transpose_v")
            rec["u"], rec["q"] = q_fwd(h, w["norm_mix"], l, mats["w_q", j], j, tm)
            name = f"attn_fwd{j}"
            (rec["o"], rec["lse"]), _, got = attn_fwd(rec["q"], kvs["kt"], kvs["v3"], w["q_norm"], w["attn_sinks"], j, bl, lp,
                                                      riders(name) if name in GATHER_PLAN else [])
            if name in GATHER_PLAN:
                landed(name, got)
            mixed, w_out, lw, bias = rec["o"], mats["w_o", j], j, None
        name = f"mixer_ffn_fwd{l}"
        (rec["h1"], rec["u2"], rec["g"], rec["up"], rec["hid"], h), _, got = mixer_ffn_fwd(
            h, mixed, w_out, lw, bias, w["norm_ffn"], l, mats["ffn_w_gate", l], mats["ffn_w_up", l], mats["ffn_w_down", l], tm // 2,
            riders(name) if name in GATHER_PLAN else [])
        if name in GATHER_PLAN:
            landed(name, got)
        saved.append(rec)

    dh3, loss_blk = loss_fwd(h.reshape(bl, lp, D), tgt)
    dh = flat(dh3)

    big, small, arrived = {}, {}, {}
    dks, dvs = [], []

    def ride(kernel_name):
        return [(big[nm], EXCHANGE_KIND[nm.rstrip("0123456789")], ks) for nm, ks in EXCHANGE_PLAN.get(kernel_name, [])]

    def landed_x(kernel_name, arrivals):
        for (nm, _), got in zip(EXCHANGE_PLAN.get(kernel_name, []), arrivals):
            arrived.setdefault(nm, []).append(got)

    def dw(name, grad, x, dy, **kw):
        big[grad], got = mm_tn(x, dy, 2 * tm, name, xch=ride(name), **kw)
        landed_x(name, got)

    for l in reversed(range(4)):
        rec = saved[l]
        dw(f"dw_down{l}", f"ffn_w_down{l}", rec["hid"], dh)
        name = f"ffn_bwd_x{l}"
        outs, got, _ = ffn_bwd_x(
            dh, rec["g"], rec["up"], rec["h1"], w["norm_ffn"], l, mats["ffn_w_down", l], mats["ffn_w_gate", l], mats["ffn_w_up", l],
            mats["w_o", l - na] if l >= na else None, tm // 2, ride(name))
        landed_x(name, got)
        dg, du, dh1, small[f"norm_ffn{l}"] = outs[:4]
        dw(f"dw_gate{l}", f"ffn_w_gate{l}", rec["u2"], dg, transposed=True)
        dw(f"dw_up{l}", f"ffn_w_up{l}", rec["u2"], du, transposed=True)
        if l >= na:
            j = l - na
            dw(f"dw_o{j}", f"w_o{j}", rec["o"], dh1)
            name = f"attn_bwd{j}"
            (dq, dk, dv, small[f"q_norm{j}"], small[f"attn_sinks{j}"]), got, _ = attn_bwd(
                rec["q"], kvs["k3"], kvs["kt"], kvs["vt"], outs[4], rec["o"], rec["lse"], w["q_norm"], w["attn_sinks"], j, bl, lp,
                ride(name))
            landed_x(name, got)
            dks.append(dk)
            dvs.append(dv)
            dw(f"dw_q{j}", f"w_q{j}", rec["u"], dq)
            dh, small[f"norm_mix{l}"] = proj_bwd(dq, mats["w_q", j], rec["h"], w["norm_mix"], l, dh1, tm, f"q_bwd{j}")[0]
            if l == na:
                dkv, small["k_norm"] = kv_bwd_pre(sum_transposed(*dks), sum_transposed(*dvs), kvs["kv"], w["k_norm"], tm)
                dw("dw_kv", "w_kv", kvs["kn"], dkv)
                dh, small["kv_norm"] = proj_bwd(dkv, mats["w_kv", 0], kvs["h"], w["kv_norm"], None, dh, tm, "kv_bwd")[0]
        else:
            name = f"conv_out_bwd{l}"
            (dc, small[f"conv_ln_g{l}"], small[f"conv_ln_b{l}"], small[f"conv_b_out{l}"]), got, _ = conv_out_bwd(
                dh1, rec["c"], w["conv_ln_g"], w["conv_ln_b"], mats["conv_w_out", l], l, tm, ride(name))
            landed_x(name, got)
            dw(f"dw_conv_out{l}", f"conv_w_out{l}", rec["s"], dh1)
            name = f"conv_mid_bwd{l}"
            (da, small[f"conv_b_in{l}"], small[f"conv_dw{l}"]), got, _ = conv_mid_bwd(
                dc, rec["a"], rec["big"], w["conv_dw"], l, tm, tpb, ride(name))
            landed_x(name, got)
            dw(f"dw_conv_in{l}", f"conv_w_in{l}", rec["u"], da, split=True)
            name = f"conv_in_bwd{l}"
            (dh, small[f"norm_mix{l}"]), got, _ = proj_bwd(da, mats["conv_w_in", l], rec["h"], w["norm_mix"], l, dh1, tm, name,
                                                           ride(name))
            landed_x(name, got)
    grad_x, small["meta_tokens"] = input_grads(dh.reshape(bl, lp, D), seq)
    return loss_blk, grad_x, big, arrived, small


def all_gather_blocks(blocks):
    n = len(blocks)

    def body(*refs):
        srcs, outs, sems = refs[:n], refs[n:2 * n], refs[2 * n:]
        _gat_start(srcs, outs, sems)
        _gat_forward(srcs, outs, sems)
        _gat_wait(srcs, outs, sems)

    any_spec = pl.BlockSpec(memory_space=pl.ANY)
    return pl.pallas_call(
        body, name="all_gather_blocks", out_shape=[_sds((NDEV,) + tuple(a.shape), a.dtype) for a in blocks],
        in_specs=[any_spec] * n, out_specs=[any_spec] * n, scratch_shapes=_xch_scratch(n),
    )(*blocks)


def cast_bf16(ws):
    n = len(ws)
    counts = [1 if x.ndim == 2 else x.shape[0] for x in ws]

    def body(*refs):
        outs = iter(refs[n:])
        for a in range(n):
            for l in range(counts[a]):
                next(outs)[...] = (refs[a][...] if ws[a].ndim == 2 else refs[a][l]).astype(BF)

    flat = pl.pallas_call(
        body, name="cast_bf16", out_shape=[_sds(x.shape[-2:], BF) for x, k in zip(ws, counts) for _ in range(k)],
        compiler_params=pltpu.CompilerParams(vmem_limit_bytes=VMEM_LIMIT),
    )(*ws)
    it = iter(flat)
    return [[next(it) for _ in range(k)] for k in counts]


def join_columns(w8, name):
    _, kk, n8 = w8.shape

    def body(x_ref, o_ref):
        o_ref[...] = jnp.concatenate([x_ref[p] for p in range(NDEV)], axis=1)

    return pl.pallas_call(body, name=name, out_shape=_sds((kk, NDEV * n8), w8.dtype),
                          compiler_params=pltpu.CompilerParams(vmem_limit_bytes=VMEM_LIMIT))(w8)


def _adamw_math(w, m, v, g):
    m2 = B1 * m + (1.0 - B1) * g
    v2 = B2 * v + (1.0 - B2) * (g * g)
    mh = m2 / (1.0 - B1 ** STEP)
    vh = v2 / (1.0 - B2 ** STEP)
    return -LR * (mh / (jnp.sqrt(vh) + AEPS) + WD * w), m2, v2


def adamw_big(w, m, v, parts, name, xch=(), gat=()):
    lyr, r, c = w.shape
    by_cols = c >= 512
    blk = (lyr, r, 256) if by_cols else (lyr, 256 if r % 256 == 0 else r, c)
    imap = (lambda i: (0, 0, i)) if by_cols else (lambda i: (0, i, 0))
    counts = [len(per_layer) for per_layer in parts]

    def body(w_ref, m_ref, v_ref, *rest):
        p_refs, (g_ref, d_ref, m2_ref, v2_ref) = iter(rest[:sum(counts)]), rest[sum(counts):]
        for l in range(lyr):
            g = None
            for _ in range(counts[l]):
                ref = next(p_refs)
                for q in range(ref.shape[0]):
                    g = ref[q].astype(F32) if g is None else g + ref[q].astype(F32)
            g_ref[l] = g
            d_ref[l], m2_ref[l], v2_ref[l] = _adamw_math(w_ref[l], m_ref[l], v_ref[l], g)

    spec = pl.BlockSpec(blk, imap)
    flat = [a for per_layer in parts for a in per_layer]
    pspecs = [pl.BlockSpec((a.shape[0],) + blk[1:], imap) for a in flat]
    return _call(body, name, ((c // 256) if by_cols else (r // blk[1]),), [spec, spec, spec] + pspecs,
                 [spec] * 4, [_sds((lyr, r, c), F32)] * 4, (w, m, v, *flat), xch=xch, gat=gat)


SMALL_ROWS = 104
REPLICATED = {"norm_mix": (0, 4, D), "norm_ffn": (4, 4, D), "kv_norm": (8, 1, D), "k_norm": (9, 1, HD), "q_norm": (10, 2, HD),
              "attn_sinks": (12, 2, NH)}
LOSS_ROW = 14
SHARDED = {"meta_tokens": (16, NMETA), "conv_b_in": (32, 4), "conv_dw": (36, 2 * CW), "conv_ln_g": (98, 2), "conv_ln_b": (100, 2),
           "conv_b_out": (102, 2)}


def pack_small(gs, loss_blk):
    order = ([f"norm_mix{l}" for l in range(4)] + [f"norm_ffn{l}" for l in range(4)] + ["kv_norm", "k_norm", "q_norm0", "q_norm1",
             "attn_sinks0", "attn_sinks1", "meta_tokens", "conv_b_in0", "conv_b_in1", "conv_dw0", "conv_dw1", "conv_ln_g0",
             "conv_ln_g1", "conv_ln_b0", "conv_ln_b1", "conv_b_out0", "conv_b_out1"])

    def body(*refs):
        r = dict(zip(order, refs))
        loss_ref, o_ref = refs[len(order)], refs[len(order) + 1]
        o_ref[...] = jnp.zeros_like(o_ref)
        for l in range(4):
            o_ref[l:l + 1, :] = r[f"norm_mix{l}"][...]
            o_ref[4 + l:5 + l, :] = r[f"norm_ffn{l}"][...]
        o_ref[8:9, :] = r["kv_norm"][...]
        o_ref[9:10, 0:HD] = r["k_norm"][...]
        for j in range(2):
            o_ref[10 + j:11 + j, 0:HD] = r[f"q_norm{j}"][...]
            o_ref[12 + j:13 + j, 0:NH] = r[f"attn_sinks{j}"][...]
            o_ref[32 + 2 * j:33 + 2 * j, :] = r[f"conv_b_in{j}"][:, 0:D]
            o_ref[33 + 2 * j:34 + 2 * j, :] = r[f"conv_b_in{j}"][:, D:2 * D]
            o_ref[36 + CW * j:36 + CW * (j + 1), :] = r[f"conv_dw{j}"][0:CW, :]
            o_ref[98 + j:99 + j, :] = r[f"conv_ln_g{j}"][...]
            o_ref[100 + j:101 + j, :] = r[f"conv_ln_b{j}"][...]
            o_ref[102 + j:103 + j, :] = r[f"conv_b_out{j}"][...]
        o_ref[LOSS_ROW:LOSS_ROW + 1, 0:1] = loss_ref[0:1, 0:1]
        o_ref[16:16 + NMETA, :] = r["meta_tokens"][...]

    return pl.pallas_call(body, name="pack_small", out_shape=_sds((SMALL_ROWS, D), F32))(*[gs[k] for k in order], loss_blk)


def adamw_small(g8, wts, mom, var):
    names = list(REPLICATED) + list(SHARDED)
    shape2 = {"kv_norm": (1, D), "k_norm": (1, HD)}
    ins = [a[k].reshape(shape2.get(k, a[k].shape)) for a in (wts, mom, var) for k in names]
    n = len(names)

    def body(*refs):
        g8_ref, w_refs, m_refs, v_refs = refs[0], refs[1:1 + n], refs[1 + n:1 + 2 * n], refs[1 + 2 * n:1 + 3 * n]
        loss_ref, outs, red_ref = refs[1 + 3 * n], refs[2 + 3 * n:-1], refs[-1]
        me = _my_index()
        acc = g8_ref[0]
        for q in range(1, NDEV):
            acc = acc + g8_ref[q]
        red_ref[...] = acc
        loss_ref[...] = red_ref[LOSS_ROW:LOSS_ROW + 1, 0:1]

        def mine(rows, width):
            acc = jnp.zeros((rows.stop - rows.start, width), F32)
            for p_ in range(NDEV):
                acc = acc + jnp.where(me == p_, red_ref[rows, p_ * width:(p_ + 1) * width], 0.0)
            return acc

        for i, k in enumerate(names):
            if k in REPLICATED:
                r0, nr, width = REPLICATED[k]
                g = red_ref[r0:r0 + nr, 0:width]
            elif k == "conv_b_in":
                half = D // (2 * D // NDEV)
                acc = jnp.zeros((2, 2 * D // NDEV), F32)
                for p_ in range(NDEV):
                    c0 = (p_ % half) * (2 * D // NDEV)
                    part = jnp.concatenate([red_ref[32 + 2 * j + p_ // half:33 + 2 * j + p_ // half, c0:c0 + 2 * D // NDEV]
                                            for j in range(2)], axis=0)
                    acc = acc + jnp.where(me == p_, part, 0.0)
                g = acc
            else:
                r0, nr = SHARDED[k]
                g = mine(slice(r0, r0 + nr), D // NDEV)
            w_, m_, v_ = w_refs[i], m_refs[i], v_refs[i]
            g_out, d_out, m_out, v_out = outs[4 * i:4 * i + 4]
            if k == "conv_dw":
                for j in range(2):
                    gj = g[CW * j:CW * (j + 1)]
                    g_out[j] = gj
                    d_out[j], m_out[j], v_out[j] = _adamw_math(w_[j], m_[j], v_[j], gj)
            else:
                g_out[...] = g
                d_out[...], m_out[...], v_out[...] = _adamw_math(w_[...], m_[...], v_[...], g)

    out_shape = [_sds((1, 1), F32)] + [_sds(ins[i].shape, F32) for i in range(n) for _ in range(4)]
    res = pl.pallas_call(body, name="adamw_small", out_shape=out_shape, scratch_shapes=[pltpu.VMEM((SMALL_ROWS, D), F32)])(g8, *ins)
    out = {k: tuple(o.reshape(wts[k].shape) for o in res[1 + 4 * i:5 + 4 * i]) for i, k in enumerate(names)}
    return res[0], out


NAMES = ["meta_tokens", "norm_mix", "norm_ffn", "conv_w_in", "conv_b_in", "conv_dw", "conv_ln_g", "conv_ln_b", "conv_w_out",
         "conv_b_out", "kv_norm", "w_kv", "k_norm", "w_q", "q_norm", "attn_sinks", "w_o", "ffn_w_gate", "ffn_w_up", "ffn_w_down"]


def kernel(x, meta_tokens, norm_mix, norm_ffn, conv_w_in, conv_b_in, conv_dw, conv_ln_g, conv_ln_b, conv_w_out, conv_b_out, kv_norm, w_kv, k_norm, w_q, q_norm, attn_sinks, w_o, ffn_w_gate, ffn_w_up, ffn_w_down, loss_target, m_meta_tokens, m_norm_mix, m_norm_ffn, m_conv_w_in, m_conv_b_in, m_conv_dw, m_conv_ln_g, m_conv_ln_b, m_conv_w_out, m_conv_b_out, m_kv_norm, m_w_kv, m_k_norm, m_w_q, m_q_norm, m_attn_sinks, m_w_o, m_ffn_w_gate, m_ffn_w_up, m_ffn_w_down, v_meta_tokens, v_norm_mix, v_norm_ffn, v_conv_w_in, v_conv_b_in, v_conv_dw, v_conv_ln_g, v_conv_ln_b, v_conv_w_out, v_conv_b_out, v_kv_norm, v_w_kv, v_k_norm, v_w_q, v_q_norm, v_attn_sinks, v_w_o, v_ffn_w_gate, v_ffn_w_up, v_ffn_w_down):
    wts = dict(zip(NAMES, (meta_tokens, norm_mix, norm_ffn, conv_w_in, conv_b_in, conv_dw, conv_ln_g, conv_ln_b, conv_w_out,
                           conv_b_out, kv_norm, w_kv, k_norm, w_q, q_norm, attn_sinks, w_o, ffn_w_gate, ffn_w_up, ffn_w_down)))
    mom = dict(zip(NAMES, (m_meta_tokens, m_norm_mix, m_norm_ffn, m_conv_w_in, m_conv_b_in, m_conv_dw, m_conv_ln_g, m_conv_ln_b,
                           m_conv_w_out, m_conv_b_out, m_kv_norm, m_w_kv, m_k_norm, m_w_q, m_q_norm, m_attn_sinks, m_w_o,
                           m_ffn_w_gate, m_ffn_w_up, m_ffn_w_down)))
    var = dict(zip(NAMES, (v_meta_tokens, v_norm_mix, v_norm_ffn, v_conv_w_in, v_conv_b_in, v_conv_dw, v_conv_ln_g, v_conv_ln_b,
                           v_conv_w_out, v_conv_b_out, v_kv_norm, v_w_kv, v_k_norm, v_w_q, v_q_norm, v_attn_sinks, v_w_o,
                           v_ffn_w_gate, v_ffn_w_up, v_ffn_w_down)))
    for k in TRANSPOSED:
        wts[k], mom[k], var[k] = (jnp.swapaxes(a, 1, 2) for a in (wts[k], mom[k], var[k]))

    big_names = list(BIG)
    layers = cast_bf16([wts[k] for k in big_names])
    shards = {(k, l): blk for k, per_layer in zip(big_names, layers) for l, blk in enumerate(per_layer)}
    vec_names = ["meta_tokens", "conv_b_in", "conv_dw", "conv_ln_g", "conv_ln_b", "conv_b_out"]
    full = dict(zip(vec_names, all_gather_blocks([wts[k] for k in vec_names])))
    join_vec = lambda a: jnp.moveaxis(a, 0, -2).reshape(a.shape[1:-1] + (NDEV * a.shape[-1],))
    w = {}
    w["conv_b_in"] = join_vec(full["conv_b_in"]).reshape(2, 1, 2 * D)
    w["conv_dw"] = join_vec(full["conv_dw"])
    for k in ("conv_ln_g", "conv_ln_b", "conv_b_out"):
        w[k] = join_vec(full[k]).reshape(2, 1, D)
    w["norm_mix"] = norm_mix.reshape(4, 1, D)
    w["norm_ffn"] = norm_ffn.reshape(4, 1, D)
    w["kv_norm"] = kv_norm.reshape(1, D)
    w["k_norm"] = k_norm.reshape(1, HD)
    w["q_norm"] = q_norm.reshape(2, 1, HD)
    w["attn_sinks"] = attn_sinks.reshape(2, 1, NH)

    loss_blk, grad_x, gbig, arrived, gs = local_step(x, loss_target, full["meta_tokens"], w, shards)

    packed = pack_small(gs, loss_blk)

    grads, delta, new_m, new_v = {}, {}, {}, {}
    tail = EXCHANGE_PLAN["tail"]
    waiting = {nm.rstrip("0123456789") for nm, _ in tail}
    order = sorted([k for k in big_names if k not in waiting], key=lambda k: -wts[k].size) + [k for k in big_names if k in waiting]
    small8 = None
    for pos, k in enumerate(order):
        flat2 = wts[k].ndim == 2
        as3 = (lambda a: a[None]) if flat2 else (lambda a: a)
        riders = tail if pos == 0 else []
        gat = [packed] if pos == 1 else []
        parts = [arrived[k]] if flat2 else [arrived[f"{k}{i}"] for i in range(wts[k].shape[0])]
        outs, got_x, got_g = adamw_big(as3(wts[k]), as3(mom[k]), as3(var[k]), parts, "adamw_" + k,
                                       xch=[(gbig[nm], EXCHANGE_KIND[nm.rstrip("0123456789")], ks) for nm, ks in riders], gat=gat)
        for (nm, _), got in zip(riders, got_x):
            arrived[nm].append(got)
        if gat:
            small8 = got_g[0]
        grads[k], delta[k], new_m[k], new_v[k] = [o[0] if flat2 else (jnp.swapaxes(o, 1, 2) if k in TRANSPOSED else o) for o in outs]
    loss, small = adamw_small(small8, wts, mom, var)
    for k, (g_, d_, m_, v_) in small.items():
        grads[k], delta[k], new_m[k], new_v[k] = g_, d_, m_, v_
    return (loss.reshape(()), grad_x, *[grads[k] for k in NAMES], *[delta[k] for k in NAMES], *[new_m[k] for k in NAMES],
            *[new_v[k] for k in NAMES])
```

```python
import functools

import jax
import jax.numpy as jnp
from jax import lax
from jax.experimental import pallas as pl
from jax.experimental.pallas import tpu as pltpu

F32 = jnp.float32
BF = jnp.bfloat16

D = 1024
DFF = 2816
NH = 16
NKV = 4
HD = 64
KVD = NKV * HD
NMETA = 16
CW = 31
HALO = 32
CHUNK = 32
QB = 128
EPS = 1e-6
NEG = -1e30
NDEV = 8
SCALE = HD ** -0.5

LR, B1, B2, AEPS, WD, STEP = 0.001, 0.9, 0.999, 1e-08, 0.01, 10

VMEM_LIMIT = 56 * 2 ** 20
MESH = pl.DeviceIdType.MESH


def _cp(n):
    return pltpu.CompilerParams(dimension_semantics=("arbitrary",) * n, vmem_limit_bytes=VMEM_LIMIT)


def _row(tm, c):
    return pl.BlockSpec((tm, c), lambda i: (i, 0))


def _res(shape):
    return pl.BlockSpec(shape, lambda i: (0,) * len(shape), pipeline_mode=pl.Buffered(1))


def _lay(l, shape):
    return pl.BlockSpec((None,) + tuple(shape), lambda i: (l,) + (0,) * len(shape), pipeline_mode=pl.Buffered(1))


def _acc(shape):
    return pl.BlockSpec(shape, lambda i: (0,) * len(shape))


def _sds(shape, dt):
    return jax.ShapeDtypeStruct(tuple(shape), dt)


def _dot(a, b):
    return jnp.dot(a.astype(BF), b.astype(BF), preferred_element_type=F32)


def _dot_nt(a, b):
    return lax.dot_general(a.astype(BF), b.astype(BF), (((1,), (1,)), ((), ())), preferred_element_type=F32)


def _dot_tn(a, b):
    return lax.dot_general(a.astype(BF), b.astype(BF), (((0,), (0,)), ((), ())), preferred_element_type=F32)


def _rstd(x):
    return lax.rsqrt(jnp.mean(x * x, axis=-1, keepdims=True) + EPS)


def _rms_bwd(x, g, dy):
    r = _rstd(x)
    z = dy * g
    dx = r * z - x * (r * r * r * jnp.mean(z * x, axis=-1, keepdims=True))
    return dx, jnp.sum(dy * x * r, axis=0, keepdims=True)


def _sig(x):
    return jax.nn.sigmoid(x)


def _fold8(x):
    out = x[0:8]
    for k in range(1, x.shape[0] // 8):
        out = out + x[8 * k:8 * k + 8]
    return out


def _shifted(win):
    return [win] + [pltpu.roll(win, 2 * CHUNK - rho, 0) for rho in range(1, 8)]


def _tap(phases, o):
    return phases[o % 8][8 * (o // 8):8 * (o // 8) + CHUNK]


def _init(ref, first):
    @pl.when(first)
    def _():
        ref[...] = jnp.zeros_like(ref)


def _my_index():
    return 4 * lax.axis_index("x") + 2 * lax.axis_index("y") + lax.axis_index("c")


def _coords(idx):
    return (idx // 4, (idx // 2) % 2, idx % 2)


ALL = tuple(range(NDEV))
H1, H2 = (0, 1, 2, 4, 6), (3, 5, 7)


def _xch_shapes(xch):
    return [_sds((len(ks),) + ((a.shape[0] // NDEV, a.shape[1]) if k == "rows" else tuple(a.shape[1:])), a.dtype) for a, k, ks in xch]


def _xch_scratch(n):
    return [pltpu.SemaphoreType.DMA((n, NDEV)), pltpu.SemaphoreType.DMA((n, NDEV)), pltpu.SemaphoreType.DMA((n,))]


def _xch_copies(meta, srcs, outs, sems, arrivals):
    send_sems, recv_sems, local_sems = sems
    me = _my_index()

    def piece(a, p):
        if meta[a][0] == "rows":
            r = srcs[a].shape[0] // NDEV
            return srcs[a].at[pl.ds(p * r, r), :]
        return srcs[a].at[p]

    def remote(a, i, k, src):
        return pltpu.make_async_remote_copy(
            src_ref=src, dst_ref=outs[a].at[i], send_sem=send_sems.at[a, k], recv_sem=recv_sems.at[a, k],
            device_id=_coords(me ^ k), device_id_type=MESH)

    local, sends, recvs = [], [], []
    for a, (_, ks) in enumerate(meta):
        for i, k in enumerate(ks):
            if k == 0:
                local.append(pltpu.make_async_copy(piece(a, me), outs[a].at[i], local_sems.at[a]))
            else:
                sends.append(remote(a, i, k, piece(a, me ^ k)))
                if arrivals:
                    recvs.append(remote(a, i, k, piece(a, me)))
    return local, sends, recvs


def _xch_start(meta, srcs, outs, sems):
    local, sends, _ = _xch_copies(meta, srcs, outs, sems, False)
    for cp in local + sends:
        cp.start()


def _xch_wait(meta, srcs, outs, sems):
    local, sends, recvs = _xch_copies(meta, srcs, outs, sems, True)
    for cp in recvs:
        cp.wait_recv()
    for cp in sends:
        cp.wait_send()
    for cp in local:
        cp.wait()


def _gat_copies(srcs, outs, sems):
    send_sems, recv_sems, local_sems = sems
    x, y, c = lax.axis_index("x"), lax.axis_index("y"), lax.axis_index("c")
    me, sibling = (x, y, c), (x, y, 1 - c)
    chips = [(1 - x, y), (x, 1 - y), (1 - x, 1 - y)]

    def copy(a, k, owner, to, from_block=False):
        slot = outs[a].at[4 * owner[0] + 2 * owner[1] + owner[2]]
        return pltpu.make_async_remote_copy(
            src_ref=srcs[a] if from_block else slot, dst_ref=slot, send_sem=send_sems.at[a, k], recv_sem=recv_sems.at[a, k],
            device_id=to, device_id_type=MESH)

    n = len(srcs)
    local = lambda: [pltpu.make_async_copy(srcs[a], outs[a].at[4 * x + 2 * y + c], local_sems.at[a]) for a in range(n)]
    first = lambda: [cp for a in range(n) for cp in
                     [copy(a, 0, me, sibling, True)] + [copy(a, 1 + j, me, (*chip, c), True) for j, chip in enumerate(chips)]]
    landed = lambda: [copy(a, 1 + j, (*chip, c), me) for a in range(n) for j, chip in enumerate(chips)]
    passed = lambda: [copy(a, 4 + j, (*chip, c), sibling) for a in range(n) for j, chip in enumerate(chips)]
    final = lambda: [cp for a in range(n) for cp in
                     [copy(a, 0, sibling, me)] + [copy(a, 4 + j, (*chip, 1 - c), me) for j, chip in enumerate(chips)]]
    return local, first, landed, passed, final


def _gat_start(srcs, outs, sems):
    local, first, _, _, _ = _gat_copies(srcs, outs, sems)
    for cp in local() + first():
        cp.start()


def _gat_forward(srcs, outs, sems):
    _, _, landed, passed, _ = _gat_copies(srcs, outs, sems)
    for got, on in zip(landed(), passed()):
        got.wait_recv()
        on.start()


def _gat_wait(srcs, outs, sems):
    local, first, _, passed, final = _gat_copies(srcs, outs, sems)
    for cp in final():
        cp.wait_recv()
    for cp in first() + passed():
        cp.wait_send()
    for cp in local():
        cp.wait()


def _call(body, name, grid, in_specs, out_specs, out_shape, args, scratch=(), xch=(), gat=()):
    n_in, n_out, n_x, n_g, n_s = len(in_specs), len(out_specs), len(xch), len(gat), len(scratch)
    kinds = [(k, ks) for _, k, ks in xch]
    total = 1
    for g in grid:
        total *= g

    def wrapped(*refs):
        ins, refs = refs[:n_in], refs[n_in:]
        x_src, refs = refs[:n_x], refs[n_x:]
        g_src, refs = refs[:n_g], refs[n_g:]
        outs, refs = refs[:n_out], refs[n_out:]
        x_out, refs = refs[:n_x], refs[n_x:]
        g_out, refs = refs[:n_g], refs[n_g:]
        own, refs = refs[:n_s], refs[n_s:]
        x_sems, g_sems = (refs[:3], refs[3:]) if n_x else ((), refs)
        step = pl.program_id(0)
        for d in range(1, len(grid)):
            step = step * grid[d] + pl.program_id(d)
        if n_x or n_g:
            @pl.when(step == 0)
            def _():
                if n_x:
                    _xch_start(kinds, x_src, x_out, x_sems)
                if n_g:
                    _gat_start(g_src, g_out, g_sems)

        body(*ins, *outs, *own)
        if n_g:
            @pl.when(step == max(total - 2, 0))
            def _():
                _gat_forward(g_src, g_out, g_sems)

        if n_x or n_g:
            @pl.when(step == total - 1)
            def _():
                if n_x:
                    _xch_wait(kinds, x_src, x_out, x_sems)
                if n_g:
                    _gat_wait(g_src, g_out, g_sems)

    any_spec = pl.BlockSpec(memory_space=pl.ANY)
    g_shapes = [_sds((NDEV,) + tuple(a.shape), a.dtype) for a in gat]
    res = pl.pallas_call(
        wrapped, name=name, grid=grid, in_specs=list(in_specs) + [any_spec] * (n_x + n_g),
        out_specs=list(out_specs) + [any_spec] * (n_x + n_g), out_shape=list(out_shape) + _xch_shapes(xch) + g_shapes,
        scratch_shapes=list(scratch) + (_xch_scratch(n_x) if n_x else []) + (_xch_scratch(n_g) if n_g else []),
        compiler_params=_cp(len(grid)),
    )(*args, *[a for a, _, _ in xch], *gat)
    return res[:n_out], res[n_out:n_out + n_x], res[n_out + n_x:]


def embed(x, meta8, lp, gat):
    bl, seq, _ = x.shape
    c8 = D // NDEV
    cb = 2 * c8

    def body(x_ref, m_ref, h_ref):
        h_ref[0:NMETA, :] = jnp.concatenate([m_ref[0], m_ref[1]], axis=1)
        h_ref[NMETA:NMETA + seq, :] = x_ref[...]
        h_ref[NMETA + seq:, :] = jnp.zeros((lp - NMETA - seq, cb), F32)

    (h0,), _, got = _call(
        body, "embed", (bl, D // cb),
        [pl.BlockSpec((None, seq, cb), lambda b, c: (b, 0, c)), pl.BlockSpec((2, NMETA, c8), lambda b, c: (c, 0, 0))],
        [pl.BlockSpec((None, lp, cb), lambda b, c: (b, 0, c))], [_sds((bl, lp, D), F32)], (x, meta8), gat=gat)
    return h0, got


def conv_in_fwd(h, nm, l, w_in, b_in, i, tm):
    t = h.shape[0]

    def body(h_ref, g_ref, w_ref, b_ref, u_ref, big_ref, a_ref):
        x = h_ref[...]
        ub = (x * _rstd(x) * g_ref[...]).astype(BF)
        u_ref[...] = ub
        big = jnp.dot(ub, w_ref[...], preferred_element_type=F32) + b_ref[...]
        big_ref[...] = big.astype(BF)
        a_ref[...] = big[:, :D] * _sig(big[:, D:])

    return pl.pallas_call(
        body, name=f"conv_in_fwd{i}", grid=(t // tm,),
        in_specs=[_row(tm, D), _lay(l, (1, D)), _res((D, 2 * D)), _lay(i, (1, 2 * D))],
        out_specs=[_row(tm, D), _row(tm, 2 * D), _row(tm, D)],
        out_shape=[_sds((t, D), BF), _sds((t, 2 * D), BF), _sds((t, D), F32)],
        compiler_params=_cp(1),
    )(h, nm, w_in, b_in)


def _prev_halo(tm):
    return pl.BlockSpec((HALO, D), lambda i: (jnp.maximum(i * (tm // HALO) - 1, 0), 0))


def _next_halo(tm, t):
    return pl.BlockSpec((HALO, D), lambda i: (jnp.minimum((i + 1) * (tm // HALO), t // HALO - 1), 0))


def conv_mid_fwd(a, dw, ln_g, ln_b, i, tm, tpb, gat):
    t = a.shape[0]

    def body(a_ref, halo_ref, dw_ref, g_ref, b_ref, c_ref, s_ref, ext):
        first = pl.program_id(0) % tpb == 0
        ext[0:HALO] = jnp.where(first, 0.0, halo_ref[...])
        ext[HALO:] = a_ref[...]

        def chunk(k, carry):
            r0 = pl.multiple_of(k * CHUNK, CHUNK)
            win = _shifted(ext[pl.ds(r0, 2 * CHUNK), :])
            c = jnp.zeros((CHUNK, D), F32)
            for j in range(CW):
                c = c + dw_ref[j:j + 1, :] * _tap(win, j + 2)
            c_ref[pl.ds(r0, CHUNK), :] = c
            mu = jnp.mean(c, axis=-1, keepdims=True)
            xc = c - mu
            n = xc * lax.rsqrt(jnp.mean(xc * xc, axis=-1, keepdims=True) + EPS) * g_ref[...] + b_ref[...]
            s_ref[pl.ds(r0, CHUNK), :] = (n * _sig(n)).astype(BF)
            return carry

        lax.fori_loop(0, tm // CHUNK, chunk, 0)

    return _call(
        body, f"conv_mid_fwd{i}", (t // tm,),
        [_row(tm, D), _prev_halo(tm), _lay(i, (CW, D)), _lay(i, (1, D)), _lay(i, (1, D))],
        [_row(tm, D), _row(tm, D)], [_sds((t, D), F32), _sds((t, D), BF)], (a, a, dw, ln_g, ln_b),
        scratch=[pltpu.VMEM((tm + HALO, D), F32)], gat=gat)


def mixer_ffn_fwd(h, s, w_out, lw, bias, nf, l, wg, wu, wd, tm, gat):
    t = h.shape[0]

    def body(*refs):
        if bias is None:
            h_ref, s_ref, w_ref, nf_ref, wg_ref, wu_ref, wd_ref, h1_ref, u_ref, g_ref, up_ref, hid_ref, h2_ref = refs
            y = 0.0
        else:
            h_ref, s_ref, w_ref, b_ref, nf_ref, wg_ref, wu_ref, wd_ref, h1_ref, u_ref, g_ref, up_ref, hid_ref, h2_ref = refs
            y = b_ref[...]
        h1 = h_ref[...] + (jnp.dot(s_ref[...], w_ref[...], preferred_element_type=F32) + y)
        h1_ref[...] = h1
        ub = (h1 * _rstd(h1) * nf_ref[...]).astype(BF)
        u_ref[...] = ub
        g = _dot_nt(ub, wg_ref[...])
        up = _dot_nt(ub, wu_ref[...])
        g_ref[...] = g.astype(BF)
        up_ref[...] = up.astype(BF)
        hid = (g * _sig(g) * up).astype(BF)
        hid_ref[...] = hid
        h2_ref[...] = h1 + jnp.dot(hid, wd_ref[...], preferred_element_type=F32)

    ins = [h, s, w_out] + ([] if bias is None else [bias]) + [nf, wg, wu, wd]
    specs = ([_row(tm, D), _row(tm, D), _res((D, D))] + ([] if bias is None else [_lay(lw, (1, D))])
             + [_lay(l, (1, D)), _res((DFF, D)), _res((DFF, D)), _res((DFF, D))])
    return _call(
        body, f"mixer_ffn_fwd{l}", (t // tm,), specs,
        [_row(tm, D), _row(tm, D), _row(tm, DFF), _row(tm, DFF), _row(tm, DFF), _row(tm, D)],
        [_sds((t, D), F32), _sds((t, D), BF), _sds((t, DFF), BF), _sds((t, DFF), BF), _sds((t, DFF), BF), _sds((t, D), F32)],
        ins, gat=gat)


def _seg_rms(x, g, nseg):
    outs = []
    for s in range(nseg):
        xs = x[:, HD * s:HD * s + HD]
        outs.append(xs * _rstd(xs) * g)
    return jnp.concatenate(outs, axis=1)


def kv_fwd(h, kvn, w_kv, kng, tm):
    t = h.shape[0]

    def body(h_ref, g_ref, w_ref, kg_ref, kn_ref, kv_ref, k_ref, v_ref):
        x = h_ref[...]
        kn = (x * _rstd(x) * g_ref[...]).astype(BF)
        kn_ref[...] = kn
        kv = jnp.dot(kn, w_ref[...], preferred_element_type=F32)
        kv_ref[...] = kv
        k_ref[...] = _seg_rms(kv[:, :KVD], kg_ref[...], NKV).astype(BF)
        v_ref[...] = kv[:, KVD:].astype(BF)

    return pl.pallas_call(
        body, name="kv_fwd", grid=(t // tm,),
        in_specs=[_row(tm, D), _res((1, D)), _res((D, 2 * KVD)), _res((1, HD))],
        out_specs=[_row(tm, D), _row(tm, 2 * KVD), _row(tm, KVD), _row(tm, KVD)],
        out_shape=[_sds((t, D), BF), _sds((t, 2 * KVD), F32), _sds((t, KVD), BF), _sds((t, KVD), BF)],
        compiler_params=_cp(1),
    )(h, kvn, w_kv, kng)


def q_fwd(h, nm, l, w_q, j, tm):
    t = h.shape[0]

    def body(h_ref, g_ref, w_ref, u_ref, q_ref):
        x = h_ref[...]
        ub = (x * _rstd(x) * g_ref[...]).astype(BF)
        u_ref[...] = ub
        q_ref[...] = jnp.dot(ub, w_ref[...], preferred_element_type=F32)

    return pl.pallas_call(
        body, name=f"q_fwd{j}", grid=(t // tm,),
        in_specs=[_row(tm, D), _lay(l, (1, D)), _res((D, D))],
        out_specs=[_row(tm, D), _row(tm, D)], out_shape=[_sds((t, D), BF), _sds((t, D), F32)],
        compiler_params=_cp(1),
    )(h, nm, w_q)


RQ = NH // NKV


def _attn_specs(nb, lp):
    cur = lambda c: pl.BlockSpec((QB, c), lambda b, n: (b * nb + n, 0))
    seq = pl.BlockSpec((None, lp, KVD), lambda b, n: (b, 0, 0))
    seq_t = pl.BlockSpec((None, KVD, lp), lambda b, n: (b, 0, 0))
    return cur, seq, seq_t


NKEYS = 2 * QB + NMETA


def _attn_mask(n, start):
    shape = (RQ * QB, NKEYS)
    qpos = n * QB + (lax.broadcasted_iota(jnp.int32, shape, 0) & (QB - 1))
    col = lax.broadcasted_iota(jnp.int32, shape, 1)
    in_band = col < 2 * QB
    kpos = jnp.where(in_band, start + col, col - 2 * QB)
    return (kpos <= qpos) & ((col >= 2 * QB) | ((qpos - kpos < QB) & (kpos >= NMETA)))


def _keys(ref, band, gs):
    return jnp.concatenate([ref[band, gs], ref[0:NMETA, gs]], axis=0)


def _keys_t(ref, band, gs):
    return jnp.concatenate([ref[gs, band], ref[gs, 0:NMETA]], axis=1)


def transpose_seq(a, name):
    bl, r, c = a.shape

    def body(a_ref, o_ref):
        o_ref[...] = a_ref[...].T

    return pl.pallas_call(
        body, name=name, grid=(bl,), in_specs=[pl.BlockSpec((None, r, c), lambda b: (b, 0, 0))],
        out_specs=pl.BlockSpec((None, c, r), lambda b: (b, 0, 0)), out_shape=_sds((bl, c, r), a.dtype), compiler_params=_cp(1),
    )(a)


def sum_transposed(a0, a1):
    bl, c, r = a0.shape

    def body(a0_ref, a1_ref, o_ref):
        o_ref[...] = (a0_ref[...] + a1_ref[...]).T

    spec = pl.BlockSpec((None, c, r), lambda b: (b, 0, 0))
    return pl.pallas_call(
        body, name="sum_transposed", grid=(bl,), in_specs=[spec, spec],
        out_specs=pl.BlockSpec((r, c), lambda b: (b, 0)), out_shape=_sds((bl * r, c), a0.dtype), compiler_params=_cp(1),
    )(a0, a1)


def _stack_heads(ref, g, fn):
    return jnp.concatenate([fn(ref[:, HD * (g * RQ + r):HD * (g * RQ + r) + HD]) for r in range(RQ)], axis=0)


def _stack_cols(ref, g):
    return jnp.concatenate([ref[:, g * RQ + r:g * RQ + r + 1] for r in range(RQ)], axis=0)


def _stack_sinks(sk_ref, g):
    return jnp.concatenate([jnp.broadcast_to(sk_ref[:, g * RQ + r:g * RQ + r + 1], (QB, 1)) for r in range(RQ)], axis=0)


def attn_fwd(q, kt, v, qg, sinks, j, bl, lp, gat):
    t = q.shape[0]
    nb = lp // QB
    cur, seq, seq_t = _attn_specs(nb, lp)

    def body(q_ref, kt_ref, v_ref, qg_ref, sk_ref, o_ref, lse_ref):
        n = pl.program_id(1)
        start = pl.multiple_of(jnp.maximum(n - 1, 0) * QB, QB)
        mask = _attn_mask(n, start)
        band = pl.ds(start, 2 * QB)
        lane = lax.broadcasted_iota(jnp.int32, (QB, NH), 1)
        ones = jnp.ones((NKEYS, HD), BF)
        lse = jnp.zeros((QB, NH), F32)
        groups = range(NKV)
        gsl = [slice(HD * g, HD * g + HD) for g in groups]
        qns = [_stack_heads(q_ref, g, lambda x: (x * _rstd(x) * (qg_ref[...] * SCALE)).astype(BF)) for g in groups]
        ss = [jnp.where(mask, _dot(qns[g], _keys_t(kt_ref, band, gsl[g])), NEG) for g in groups]
        sinks = [_stack_sinks(sk_ref, g) for g in groups]
        mxs = [jnp.maximum(jnp.max(ss[g], -1, keepdims=True), sinks[g]) for g in groups]
        oas = [_dot(jnp.exp(ss[g] - mxs[g]), jnp.concatenate([_keys(v_ref, band, gsl[g]), ones], axis=1)) for g in groups]
        for g in groups:
            den = oas[g][:, HD:HD + 1] + jnp.exp(sinks[g] - mxs[g])
            o = oas[g][:, :HD] * (1.0 / den)
            l = mxs[g] + jnp.log(den)
            for r in range(RQ):
                h = g * RQ + r
                o_ref[:, HD * h:HD * h + HD] = o[r * QB:(r + 1) * QB].astype(BF)
                lse = jnp.where(lane == h, l[r * QB:(r + 1) * QB], lse)
        lse_ref[...] = lse

    return _call(
        body, f"attn_fwd{j}", (bl, nb),
        [cur(D), seq_t, seq, pl.BlockSpec((None, 1, HD), lambda b, n: (j, 0, 0)), pl.BlockSpec((None, 1, NH), lambda b, n: (j, 0, 0))],
        [cur(D), cur(NH)], [_sds((t, D), BF), _sds((t, NH), F32)], (q, kt, v, qg, sinks), gat=gat)


def loss_fwd(h, tgt):
    bl, lp, _ = h.shape
    seq = tgt.shape[1]
    cb = 256

    def body(h_ref, t_ref, dh_ref, loss_ref):
        _init(loss_ref, (pl.program_id(0) == 0) & (pl.program_id(1) == 0))
        err = h_ref[NMETA:NMETA + seq, :] - t_ref[...]
        dh_ref[...] = jnp.zeros_like(dh_ref)
        dh_ref[NMETA:NMETA + seq, :] = err * (1.0 / D)
        loss_ref[...] += (0.5 / D) * jnp.sum(err * err)

    return pl.pallas_call(
        body, name="loss_fwd", grid=(bl, D // cb),
        in_specs=[pl.BlockSpec((None, lp, cb), lambda b, c: (b, 0, c)), pl.BlockSpec((None, seq, cb), lambda b, c: (b, 0, c))],
        out_specs=[pl.BlockSpec((None, lp, cb), lambda b, c: (b, 0, c)), pl.BlockSpec((8, 128), lambda b, c: (0, 0))],
        out_shape=[_sds((bl, lp, D), F32), _sds((8, 128), F32)],
        compiler_params=_cp(2),
    )(h, tgt)


def ffn_bwd_x(dh2, g, up, h1, nf, l, wd, wg, wu, w_o, tm, xch):
    t = dh2.shape[0]

    def body(dh2_ref, g_ref, up_ref, h1_ref, nf_ref, wd_ref, wg_ref, wu_ref, *rest):
        if w_o is None:
            dg_ref, du_ref, dh1_ref, dnf_ref = rest
        else:
            wo_ref, dg_ref, du_ref, dh1_ref, dnf_ref, do_ref = rest
        _init(dnf_ref, pl.program_id(0) == 0)
        dh2v = dh2_ref[...]
        dhid = _dot_nt(dh2v, wd_ref[...])
        gv = g_ref[...].astype(F32)
        uv = up_ref[...].astype(F32)
        sg = _sig(gv)
        dgv = (dhid * uv * (sg * (1.0 + gv * (1.0 - sg)))).astype(BF)
        duv = (dhid * (gv * sg)).astype(BF)
        dg_ref[...] = dgv
        du_ref[...] = duv
        dnorm = _dot(dgv, wg_ref[...]) + _dot(duv, wu_ref[...])
        dx, dnf = _rms_bwd(h1_ref[...], nf_ref[...], dnorm)
        dh1 = dh2v + dx
        dh1_ref[...] = dh1
        dnf_ref[...] += dnf
        if w_o is not None:
            do_ref[...] = _dot_nt(dh1, wo_ref[...]).astype(BF)

    attn = w_o is not None
    return _call(
        body, f"ffn_bwd_x{l}", (t // tm,),
        [_row(tm, D), _row(tm, DFF), _row(tm, DFF), _row(tm, D), _lay(l, (1, D)),
         _res((DFF, D)), _res((DFF, D)), _res((DFF, D))] + ([_res((D, D))] if attn else []),
        [_row(tm, DFF), _row(tm, DFF), _row(tm, D), _acc((1, D))] + ([_row(tm, D)] if attn else []),
        [_sds((t, DFF), BF), _sds((t, DFF), BF), _sds((t, D), F32), _sds((1, D), F32)] + ([_sds((t, D), BF)] if attn else []),
        (dh2, g, up, h1, nf, wd, wg, wu) + ((w_o,) if attn else ()), xch=xch)


def mm_tn(x, dy, tm, name, split=False, transposed=False, xch=()):
    t, kk = x.shape
    nn = dy.shape[1]
    n8 = nn // NDEV
    nsteps = t // tm

    def body(x_ref, dy_ref, o_ref, acc):
        i = pl.program_id(0)
        _init(acc, i == 0)
        acc[...] += _dot_tn(x_ref[...], dy_ref[...])

        @pl.when(i == nsteps - 1)
        def _():
            if split:
                for p in range(NDEV):
                    o_ref[p] = acc[:, p * n8:(p + 1) * n8].astype(BF)
            elif transposed:
                o_ref[...] = acc[...].T.astype(BF)
            else:
                o_ref[...] = acc[...].astype(BF)

    oshape = (NDEV, kk, n8) if split else ((nn, kk) if transposed else (kk, nn))
    ospec = pl.BlockSpec(oshape, lambda i: (0,) * len(oshape), pipeline_mode=pl.Buffered(1))
    (out,), got, _ = _call(body, name, (nsteps,), [_row(tm, kk), _row(tm, nn)], [ospec], [_sds(oshape, BF)], (x, dy),
                           scratch=[pltpu.VMEM((kk, nn), F32)], xch=xch)
    return out, got


def proj_bwd(dy, w, h, g, lg, dh_in, tm, name, xch=()):
    t = h.shape[0]
    nn = dy.shape[1]
    wspec = _res(w.shape)
    gspec = _res((1, D)) if lg is None else _lay(lg, (1, D))

    def body(dy_ref, w_ref, h_ref, g_ref, dhin_ref, dh_ref, dg_ref):
        _init(dg_ref, pl.program_id(0) == 0)
        du = _dot_nt(dy_ref[...], w_ref[...])
        dx, dg = _rms_bwd(h_ref[...], g_ref[...], du)
        dh_ref[...] = dhin_ref[...] + dx
        dg_ref[...] += dg

    return _call(body, name, (t // tm,), [_row(tm, nn), wspec, _row(tm, D), gspec, _row(tm, D)],
                 [_row(tm, D), _acc((1, D))], [_sds((t, D), F32), _sds((1, D), F32)], (dy, w, h, g, dh_in), xch=xch)


def attn_bwd(q, k, kt, vt, do, o, lse, qg, sinks, j, bl, lp, xch):
    t = q.shape[0]
    nb = lp // QB
    cur, seq, seq_t = _attn_specs(nb, lp)

    def body(q_ref, k_ref, kt_ref, vt_ref, do_ref, o_ref, lse_ref, qg_ref, sk_ref, dq_ref, dk_ref, dv_ref, dqg_ref, dsk_ref):
        b, n = pl.program_id(0), pl.program_id(1)
        _init(dk_ref, n == 0)
        _init(dv_ref, n == 0)
        _init(dqg_ref, (b == 0) & (n == 0))
        _init(dsk_ref, (b == 0) & (n == 0))
        start = pl.multiple_of(jnp.maximum(n - 1, 0) * QB, QB)
        mask = _attn_mask(n, start)
        band = pl.ds(start, 2 * QB)
        lane = lax.broadcasted_iota(jnp.int32, (1, NH), 1)
        dqg = jnp.zeros((1, HD), F32)
        dsk = jnp.zeros((1, NH), F32)
        groups = range(NKV)
        gsl = [slice(HD * g, HD * g + HD) for g in groups]
        qhs = [_stack_heads(q_ref, g, lambda x: x) for g in groups]
        rss = [_rstd(qhs[g]) for g in groups]
        qns = [(qhs[g] * rss[g] * (qg_ref[...] * SCALE)).astype(BF) for g in groups]
        lss = [_stack_cols(lse_ref, g) for g in groups]
        dohs = [_stack_heads(do_ref, g, lambda x: x) for g in groups]
        deltas = [jnp.sum(dohs[g].astype(F32) * _stack_heads(o_ref, g, lambda x: x).astype(F32), axis=-1, keepdims=True)
                  for g in groups]
        prs = [jnp.where(mask, jnp.exp(_dot(qns[g], _keys_t(kt_ref, band, gsl[g])) - lss[g]), 0.0) for g in groups]
        dss = [(prs[g] * (_dot(dohs[g], _keys_t(vt_ref, band, gsl[g])) - deltas[g])).astype(BF) for g in groups]
        for g in groups:
            gs = gsl[g]
            dkt = _dot_tn(qns[g], dss[g])
            dvt = _dot_tn(dohs[g], prs[g])
            dk_ref[gs, band] += dkt[:, :2 * QB]
            dv_ref[gs, band] += dvt[:, :2 * QB]
            dk_ref[gs, 0:NMETA] += dkt[:, 2 * QB:]
            dv_ref[gs, 0:NMETA] += dvt[:, 2 * QB:]
        dqns = [_dot(dss[g], _keys(k_ref, band, gsl[g])) * SCALE for g in groups]
        for g in groups:
            qh, rs, dqn = qhs[g], rss[g], dqns[g]
            dsink = jnp.exp(_stack_sinks(sk_ref, g) - lss[g]) * deltas[g]
            z = dqn * qg_ref[...]
            dq = rs * z - qh * (rs * rs * rs * jnp.mean(z * qh, axis=-1, keepdims=True))
            dqg = dqg + jnp.sum(dqn * qh * rs, axis=0, keepdims=True)
            for r in range(RQ):
                h = g * RQ + r
                dq_ref[:, HD * h:HD * h + HD] = dq[r * QB:(r + 1) * QB]
                dsk = dsk + jnp.where(lane == h, -jnp.sum(dsink[r * QB:(r + 1) * QB]), 0.0)
        dqg_ref[...] += dqg
        dsk_ref[...] += dsk

    return _call(
        body, f"attn_bwd{j}", (bl, nb),
        [cur(D), seq, seq_t, seq_t, cur(D), cur(D), cur(NH),
         pl.BlockSpec((None, 1, HD), lambda b, n: (j, 0, 0)), pl.BlockSpec((None, 1, NH), lambda b, n: (j, 0, 0))],
        [cur(D), seq_t, seq_t, pl.BlockSpec((1, HD), lambda b, n: (0, 0)), pl.BlockSpec((1, NH), lambda b, n: (0, 0))],
        [_sds((t, D), F32), _sds((bl, KVD, lp), F32), _sds((bl, KVD, lp), F32), _sds((1, HD), F32), _sds((1, NH), F32)],
        (q, k, kt, vt, do, o, lse, qg, sinks), xch=xch)


def kv_bwd_pre(dk, dv, kv, kng, tm):
    t = kv.shape[0]

    def body(dk_ref, dv_ref, kv_ref, g_ref, dkv_ref, dg_ref):
        _init(dg_ref, pl.program_id(0) == 0)
        dg = jnp.zeros((1, HD), F32)
        outs = []
        for s in range(NKV):
            sl = slice(HD * s, HD * s + HD)
            dx, dgs = _rms_bwd(kv_ref[:, sl], g_ref[...], dk_ref[:, sl])
            outs.append(dx)
            dg = dg + dgs
        dkv_ref[:, :KVD] = jnp.concatenate(outs, axis=1).astype(BF)
        dkv_ref[:, KVD:] = dv_ref[...].astype(BF)
        dg_ref[...] += dg

    return pl.pallas_call(
        body, name="kv_bwd_pre", grid=(t // tm,),
        in_specs=[_row(tm, KVD)] * 2 + [_row(tm, 2 * KVD), _res((1, HD))],
        out_specs=[_row(tm, 2 * KVD), _acc((1, HD))], out_shape=[_sds((t, 2 * KVD), BF), _sds((1, HD), F32)],
        compiler_params=_cp(1),
    )(dk, dv, kv, kng)


def conv_out_bwd(dh1, c, ln_g, ln_b, w_out, i, tm, xch):
    t = dh1.shape[0]

    def body(dh1_ref, c_ref, g_ref, b_ref, w_ref, dc_ref, dg_ref, db_ref, dbo_ref):
        first = pl.program_id(0) == 0
        _init(dg_ref, first)
        _init(db_ref, first)
        _init(dbo_ref, first)
        dh1v = dh1_ref[...]
        ds = _dot_nt(dh1v, w_ref[...])
        cv = c_ref[...]
        xc = cv - jnp.mean(cv, axis=-1, keepdims=True)
        rstd = lax.rsqrt(jnp.mean(xc * xc, axis=-1, keepdims=True) + EPS)
        xh = xc * rstd
        n = xh * g_ref[...] + b_ref[...]
        sg = _sig(n)
        dn = ds * (sg * (1.0 + n * (1.0 - sg)))
        dxh = dn * g_ref[...]
        dc_ref[...] = rstd * (dxh - jnp.mean(dxh, axis=-1, keepdims=True) - xh * jnp.mean(dxh * xh, axis=-1, keepdims=True))
        dg_ref[...] += jnp.sum(dn * xh, axis=0, keepdims=True)
        db_ref[...] += jnp.sum(dn, axis=0, keepdims=True)
        dbo_ref[...] += jnp.sum(dh1v, axis=0, keepdims=True)

    return _call(
        body, f"conv_out_bwd{i}", (t // tm,), [_row(tm, D), _row(tm, D), _lay(i, (1, D)), _lay(i, (1, D)), _res((D, D))],
        [_row(tm, D), _acc((1, D)), _acc((1, D)), _acc((1, D))], [_sds((t, D), F32)] + [_sds((1, D), F32)] * 3,
        (dh1, c, ln_g, ln_b, w_out), xch=xch)


def conv_mid_bwd(dc, a, big, dw, i, tm, tpb, xch):
    t = dc.shape[0]
    nsteps = t // tm

    def body(dc_ref, nxt_ref, a_ref, prv_ref, big_ref, dw_ref, da_ref, dbin_ref, ddw_ref, dce, ae, wacc, bacc):
        i_ = pl.program_id(0)
        _init(wacc, i_ == 0)
        _init(bacc, i_ == 0)
        dce[0:tm] = dc_ref[...]
        dce[tm:] = jnp.where(i_ % tpb == tpb - 1, 0.0, nxt_ref[...])
        ae[0:HALO] = jnp.where(i_ % tpb == 0, 0.0, prv_ref[...])
        ae[HALO:] = a_ref[...]

        def chunk(k, carry):
            r0 = pl.multiple_of(k * CHUNK, CHUNK)
            wdc = _shifted(dce[pl.ds(r0, 2 * CHUNK), :])
            wa = _shifted(ae[pl.ds(r0, 2 * CHUNK), :])
            dcc = wdc[0][0:CHUNK]
            da = jnp.zeros((CHUNK, D), F32)
            for j in range(CW):
                da = da + dw_ref[j:j + 1, :] * _tap(wdc, CW - 1 - j)
                wacc[j] += _fold8(dcc * _tap(wa, j + 2))
            bv = big_ref[pl.ds(r0, CHUNK), :].astype(F32)
            a1, sg = bv[:, :D], _sig(bv[:, D:])
            d1 = da * sg
            d2 = da * a1 * sg * (1.0 - sg)
            da_ref[pl.ds(r0, CHUNK), 0:D] = d1.astype(BF)
            da_ref[pl.ds(r0, CHUNK), D:2 * D] = d2.astype(BF)
            bacc[:, 0:D] += _fold8(d1)
            bacc[:, D:2 * D] += _fold8(d2)
            return carry

        lax.fori_loop(0, tm // CHUNK, chunk, 0)

        @pl.when(i_ == nsteps - 1)
        def _():
            dbin_ref[...] = jnp.sum(bacc[...], axis=0, keepdims=True)
            ddw_ref[...] = jnp.sum(wacc[...], axis=1)

    return _call(
        body, f"conv_mid_bwd{i}", (nsteps,),
        [_row(tm, D), _next_halo(tm, t), _row(tm, D), _prev_halo(tm), _row(tm, 2 * D), _lay(i, (CW, D))],
        [_row(tm, 2 * D), _acc((1, 2 * D)), _acc((CW + 1, D))],
        [_sds((t, 2 * D), BF), _sds((1, 2 * D), F32), _sds((CW + 1, D), F32)],
        (dc, dc, a, a, big, dw),
        scratch=[pltpu.VMEM((tm + HALO, D), F32), pltpu.VMEM((tm + HALO, D), F32),
                 pltpu.VMEM((CW + 1, 8, D), F32), pltpu.VMEM((8, 2 * D), F32)], xch=xch)


def input_grads(dh0, seq):
    bl, lp, _ = dh0.shape
    cb = 256

    def body(dh_ref, gx_ref, gm_ref):
        _init(gm_ref, pl.program_id(1) == 0)
        gx_ref[...] = dh_ref[NMETA:NMETA + seq, :]
        gm_ref[...] += dh_ref[0:NMETA, :]

    return pl.pallas_call(
        body, name="input_grads", grid=(D // cb, bl),
        in_specs=[pl.BlockSpec((None, lp, cb), lambda c, b: (b, 0, c))],
        out_specs=[pl.BlockSpec((None, seq, cb), lambda c, b: (b, 0, c)), pl.BlockSpec((NMETA, cb), lambda c, b: (0, c))],
        out_shape=[_sds((bl, seq, D), F32), _sds((NMETA, D), F32)],
        compiler_params=_cp(2),
    )(dh0)


GATHER_PLAN = {
    "embed": [("conv_w_in", 0), ("conv_w_out", 0)],
    "conv_mid_fwd0": [("ffn_w_gate", 0), ("ffn_w_up", 0), ("ffn_w_down", 0)],
    "mixer_ffn_fwd0": [("conv_w_in", 1), ("conv_w_out", 1), ("ffn_w_gate", 1)],
    "conv_mid_fwd1": [("ffn_w_up", 1), ("ffn_w_down", 1), ("w_kv", 0), ("w_q", 0)],
    "mixer_ffn_fwd1": [("w_o", 0), ("ffn_w_down", 2)],
    "attn_fwd0": [("ffn_w_gate", 2), ("ffn_w_up", 2), ("w_q", 1), ("w_o", 1)],
    "attn_fwd1": [("ffn_w_gate", 3), ("ffn_w_up", 3), ("ffn_w_down", 3)],
}
EXCHANGE_PLAN = {
    "attn_bwd1": [("ffn_w_down3", ALL), ("ffn_w_gate3", ALL), ("ffn_w_up3", H1)],
    "dw_down2": [("w_o1", ALL)],
    "ffn_bwd_x2": [("ffn_w_up3", H2), ("w_q1", ALL)],
    "attn_bwd0": [("ffn_w_down2", ALL), ("ffn_w_gate2", ALL), ("ffn_w_up2", H1)],
    "dw_down1": [("w_o0", ALL), ("w_q0", H1)],
    "ffn_bwd_x1": [("ffn_w_up2", H2), ("w_q0", H2), ("w_kv", ALL)],
    "dw_gate1": [("ffn_w_down1", H1)],
    "dw_up1": [("ffn_w_down1", H2)],
    "conv_mid_bwd1": [("ffn_w_gate1", ALL), ("ffn_w_up1", H1)],
    "conv_in_bwd1": [("ffn_w_up1", H2)],
    "dw_down0": [("conv_w_out1", ALL)],
    "ffn_bwd_x0": [("conv_w_in1", ALL)],
    "dw_gate0": [("ffn_w_down0", H1)],
    "dw_up0": [("ffn_w_down0", H2)],
    "conv_out_bwd0": [("ffn_w_gate0", H1)],
    "conv_mid_bwd0": [("ffn_w_gate0", H2), ("ffn_w_up0", H1), ("conv_w_out0", ALL)],
    "dw_conv_in0": [("ffn_w_up0", H2)],
    "conv_in_bwd0": [("conv_w_in0", H1)],
    "tail": [("conv_w_in0", H2)],
}
BIG = {"conv_w_in": "pieces", "conv_w_out": "rows", "w_kv": "rows", "w_q": "rows", "w_o": "rows",
       "ffn_w_gate": "rows", "ffn_w_up": "rows", "ffn_w_down": "rows"}
EXCHANGE_KIND = BIG
TRANSPOSED = ("ffn_w_gate", "ffn_w_up")


def gathered_matrix(name, layer, blocks8):
    if BIG[name] == "rows":
        return blocks8.reshape(NDEV * blocks8.shape[1], blocks8.shape[2])
    return join_columns(blocks8, f"join_{name}{layer}")


def local_step(x, tgt, meta8, w, shards):
    bl, seq, _ = x.shape
    lp = -(-(NMETA + seq) // QB) * QB
    tpb = 4
    tm = lp // tpb
    t = bl * lp
    na = 2
    flat = lambda a: a.reshape(t, D)
    mats = {}

    def riders(carrier):
        return [shards[key] for key in GATHER_PLAN[carrier]]

    def landed(carrier, blocks):
        for key, b8 in zip(GATHER_PLAN[carrier], blocks):
            mats[key] = gathered_matrix(*key, b8)

    h0, got = embed(x, meta8, lp, riders("embed"))
    landed("embed", got)
    h = flat(h0)
    saved = []
    kvs = None
    for l in range(4):
        rec = {"h": h}
        if l < na:
            rec["u"], rec["big"], rec["a"] = conv_in_fwd(h, w["norm_mix"], l, mats["conv_w_in", l], w["conv_b_in"], l, tm)
            name = f"conv_mid_fwd{l}"
            (rec["c"], rec["s"]), _, got = conv_mid_fwd(rec["a"], w["conv_dw"], w["conv_ln_g"], w["conv_ln_b"], l, tm, tpb,
                                                         riders(name))
            landed(name, got)
            mixed, w_out, lw, bias = rec["s"], mats["conv_w_out", l], l, w["conv_b_out"]
        else:
            j = l - na
            if kvs is None:
                kvs = dict(zip(("kn", "kv", "k", "v"), kv_fwd(h, w["kv_norm"], mats["w_kv", 0], w["k_norm"], tm)))
                kvs["h"] = h
                kvs["k3"], kvs["v3"] = kvs["k"].reshape(bl, lp, KVD), kvs["v"].reshape(bl, lp, KVD)
                kvs["kt"], kvs["vt"] = transpose_seq(kvs["k3"], "transpose_k"), transpose_seq(kvs["v3"], "transpose_v")
            rec["u"], rec["q"] = q_fwd(h, w["norm_mix"], l, mats["w_q", j], j, tm)
            name = f"attn_fwd{j}"
            (rec["o"], rec["lse"]), _, got = attn_fwd(rec["q"], kvs["kt"], kvs["v3"], w["q_norm"], w["attn_sinks"], j, bl, lp,
                                                      riders(name) if name in GATHER_PLAN else [])
            if name in GATHER_PLAN:
                landed(name, got)
            mixed, w_out, lw, bias = rec["o"], mats["w_o", j], j, None
        name = f"mixer_ffn_fwd{l}"
        (rec["h1"], rec["u2"], rec["g"], rec["up"], rec["hid"], h), _, got = mixer_ffn_fwd(
            h, mixed, w_out, lw, bias, w["norm_ffn"], l, mats["ffn_w_gate", l], mats["ffn_w_up", l], mats["ffn_w_down", l], tm // 2,
            riders(name) if name in GATHER_PLAN else [])
        if name in GATHER_PLAN:
            landed(name, got)
        saved.append(rec)

    dh3, loss_blk = loss_fwd(h.reshape(bl, lp, D), tgt)
    dh = flat(dh3)

    big, small, arrived = {}, {}, {}
    dks, dvs = [], []

    def ride(kernel_name):
        return [(big[nm], EXCHANGE_KIND[nm.rstrip("0123456789")], ks) for nm, ks in EXCHANGE_PLAN.get(kernel_name, [])]

    def landed_x(kernel_name, arrivals):
        for (nm, _), got in zip(EXCHANGE_PLAN.get(kernel_name, []), arrivals):
            arrived.setdefault(nm, []).append(got)

    def dw(name, grad, x, dy, rows=lp, **kw):
        big[grad], got = mm_tn(x, dy, rows, name, xch=ride(name), **kw)
        landed_x(name, got)

    for l in reversed(range(4)):
        rec = saved[l]
        dw(f"dw_down{l}", f"ffn_w_down{l}", rec["hid"], dh, rows=lp // 2)
        name = f"ffn_bwd_x{l}"
        outs, got, _ = ffn_bwd_x(
            dh, rec["g"], rec["up"], rec["h1"], w["norm_ffn"], l, mats["ffn_w_down", l], mats["ffn_w_gate", l], mats["ffn_w_up", l],
            mats["w_o", l - na] if l >= na else None, tm // 2, ride(name))
        landed_x(name, got)
        dg, du, dh1, small[f"norm_ffn{l}"] = outs[:4]
        dw(f"dw_gate{l}", f"ffn_w_gate{l}", rec["u2"], dg, transposed=True)
        dw(f"dw_up{l}", f"ffn_w_up{l}", rec["u2"], du, transposed=True)
        if l >= na:
            j = l - na
            dw(f"dw_o{j}", f"w_o{j}", rec["o"], dh1)
            name = f"attn_bwd{j}"
            (dq, dk, dv, small[f"q_norm{j}"], small[f"attn_sinks{j}"]), got, _ = attn_bwd(
                rec["q"], kvs["k3"], kvs["kt"], kvs["vt"], outs[4], rec["o"], rec["lse"], w["q_norm"], w["attn_sinks"], j, bl, lp,
                ride(name))
            landed_x(name, got)
            dks.append(dk)
            dvs.append(dv)
            dw(f"dw_q{j}", f"w_q{j}", rec["u"], dq)
            dh, small[f"norm_mix{l}"] = proj_bwd(dq, mats["w_q", j], rec["h"], w["norm_mix"], l, dh1, tm, f"q_bwd{j}")[0]
            if l == na:
                dkv, small["k_norm"] = kv_bwd_pre(sum_transposed(*dks), sum_transposed(*dvs), kvs["kv"], w["k_norm"], tm)
                dw("dw_kv", "w_kv", kvs["kn"], dkv)
                dh, small["kv_norm"] = proj_bwd(dkv, mats["w_kv", 0], kvs["h"], w["kv_norm"], None, dh, tm, "kv_bwd")[0]
        else:
            name = f"conv_out_bwd{l}"
            (dc, small[f"conv_ln_g{l}"], small[f"conv_ln_b{l}"], small[f"conv_b_out{l}"]), got, _ = conv_out_bwd(
                dh1, rec["c"], w["conv_ln_g"], w["conv_ln_b"], mats["conv_w_out", l], l, tm, ride(name))
            landed_x(name, got)
            dw(f"dw_conv_out{l}", f"conv_w_out{l}", rec["s"], dh1)
            name = f"conv_mid_bwd{l}"
            (da, small[f"conv_b_in{l}"], small[f"conv_dw{l}"]), got, _ = conv_mid_bwd(
                dc, rec["a"], rec["big"], w["conv_dw"], l, tm, tpb, ride(name))
            landed_x(name, got)
            dw(f"dw_conv_in{l}", f"conv_w_in{l}", rec["u"], da, split=True)
            name = f"conv_in_bwd{l}"
            (dh, small[f"norm_mix{l}"]), got, _ = proj_bwd(da, mats["conv_w_in", l], rec["h"], w["norm_mix"], l, dh1, tm, name,
                                                           ride(name))
            landed_x(name, got)
    grad_x, small["meta_tokens"] = input_grads(dh.reshape(bl, lp, D), seq)
    return loss_blk, grad_x, big, arrived, small


def all_gather_blocks(blocks):
    n = len(blocks)

    def body(*refs):
        srcs, outs, sems = refs[:n], refs[n:2 * n], refs[2 * n:]
        _gat_start(srcs, outs, sems)
        _gat_forward(srcs, outs, sems)
        _gat_wait(srcs, outs, sems)

    any_spec = pl.BlockSpec(memory_space=pl.ANY)
    return pl.pallas_call(
        body, name="all_gather_blocks", out_shape=[_sds((NDEV,) + tuple(a.shape), a.dtype) for a in blocks],
        in_specs=[any_spec] * n, out_specs=[any_spec] * n, scratch_shapes=_xch_scratch(n),
    )(*blocks)


def cast_bf16(ws):
    n = len(ws)
    counts = [1 if x.ndim == 2 else x.shape[0] for x in ws]

    def body(*refs):
        outs = iter(refs[n:])
        for a in range(n):
            for l in range(counts[a]):
                next(outs)[...] = (refs[a][...] if ws[a].ndim == 2 else refs[a][l]).astype(BF)

    flat = pl.pallas_call(
        body, name="cast_bf16", out_shape=[_sds(x.shape[-2:], BF) for x, k in zip(ws, counts) for _ in range(k)],
        compiler_params=pltpu.CompilerParams(vmem_limit_bytes=VMEM_LIMIT),
    )(*ws)
    it = iter(flat)
    return [[next(it) for _ in range(k)] for k in counts]


def join_columns(w8, name):
    _, kk, n8 = w8.shape

    def body(x_ref, o_ref):
        o_ref[...] = jnp.concatenate([x_ref[p] for p in range(NDEV)], axis=1)

    return pl.pallas_call(body, name=name, out_shape=_sds((kk, NDEV * n8), w8.dtype),
                          compiler_params=pltpu.CompilerParams(vmem_limit_bytes=VMEM_LIMIT))(w8)


def _adamw_math(w, m, v, g):
    m2 = B1 * m + (1.0 - B1) * g
    v2 = B2 * v + (1.0 - B2) * (g * g)
    mh = m2 / (1.0 - B1 ** STEP)
    vh = v2 / (1.0 - B2 ** STEP)
    return -LR * (mh / (jnp.sqrt(vh) + AEPS) + WD * w), m2, v2


def adamw_big(w, m, v, parts, name, xch=(), gat=()):
    lyr, r, c = w.shape
    by_cols = c >= 512
    blk = (lyr, r, 256) if by_cols else (lyr, 256 if r % 256 == 0 else r, c)
    imap = (lambda i: (0, 0, i)) if by_cols else (lambda i: (0, i, 0))
    counts = [len(per_layer) for per_layer in parts]

    def body(w_ref, m_ref, v_ref, *rest):
        p_refs, (g_ref, d_ref, m2_ref, v2_ref) = iter(rest[:sum(counts)]), rest[sum(counts):]
        for l in range(lyr):
            g = None
            for _ in range(counts[l]):
                ref = next(p_refs)
                for q in range(ref.shape[0]):
                    g = ref[q].astype(F32) if g is None else g + ref[q].astype(F32)
            g_ref[l] = g
            d_ref[l], m2_ref[l], v2_ref[l] = _adamw_math(w_ref[l], m_ref[l], v_ref[l], g)

    spec = pl.BlockSpec(blk, imap)
    flat = [a for per_layer in parts for a in per_layer]
    pspecs = [pl.BlockSpec((a.shape[0],) + blk[1:], imap) for a in flat]
    return _call(body, name, ((c // 256) if by_cols else (r // blk[1]),), [spec, spec, spec] + pspecs,
                 [spec] * 4, [_sds((lyr, r, c), F32)] * 4, (w, m, v, *flat), xch=xch, gat=gat)


SMALL_ROWS = 104
REPLICATED = {"norm_mix": (0, 4, D), "norm_ffn": (4, 4, D), "kv_norm": (8, 1, D), "k_norm": (9, 1, HD), "q_norm": (10, 2, HD),
              "attn_sinks": (12, 2, NH)}
LOSS_ROW = 14
SHARDED = {"meta_tokens": (16, NMETA), "conv_b_in": (32, 4), "conv_dw": (36, 2 * CW), "conv_ln_g": (98, 2), "conv_ln_b": (100, 2),
           "conv_b_out": (102, 2)}


def pack_small(gs, loss_blk):
    order = ([f"norm_mix{l}" for l in range(4)] + [f"norm_ffn{l}" for l in range(4)] + ["kv_norm", "k_norm", "q_norm0", "q_norm1",
             "attn_sinks0", "attn_sinks1", "meta_tokens", "conv_b_in0", "conv_b_in1", "conv_dw0", "conv_dw1", "conv_ln_g0",
             "conv_ln_g1", "conv_ln_b0", "conv_ln_b1", "conv_b_out0", "conv_b_out1"])

    def body(*refs):
        r = dict(zip(order, refs))
        loss_ref, o_ref = refs[len(order)], refs[len(order) + 1]
        o_ref[...] = jnp.zeros_like(o_ref)
        for l in range(4):
            o_ref[l:l + 1, :] = r[f"norm_mix{l}"][...]
            o_ref[4 + l:5 + l, :] = r[f"norm_ffn{l}"][...]
        o_ref[8:9, :] = r["kv_norm"][...]
        o_ref[9:10, 0:HD] = r["k_norm"][...]
        for j in range(2):
            o_ref[10 + j:11 + j, 0:HD] = r[f"q_norm{j}"][...]
            o_ref[12 + j:13 + j, 0:NH] = r[f"attn_sinks{j}"][...]
            o_ref[32 + 2 * j:33 + 2 * j, :] = r[f"conv_b_in{j}"][:, 0:D]
            o_ref[33 + 2 * j:34 + 2 * j, :] = r[f"conv_b_in{j}"][:, D:2 * D]
            o_ref[36 + CW * j:36 + CW * (j + 1), :] = r[f"conv_dw{j}"][0:CW, :]
            o_ref[98 + j:99 + j, :] = r[f"conv_ln_g{j}"][...]
            o_ref[100 + j:101 + j, :] = r[f"conv_ln_b{j}"][...]
            o_ref[102 + j:103 + j, :] = r[f"conv_b_out{j}"][...]
        o_ref[LOSS_ROW:LOSS_ROW + 1, 0:1] = loss_ref[0:1, 0:1]
        o_ref[16:16 + NMETA, :] = r["meta_tokens"][...]

    return pl.pallas_call(body, name="pack_small", out_shape=_sds((SMALL_ROWS, D), F32))(*[gs[k] for k in order], loss_blk)


def adamw_small(g8, wts, mom, var):
    names = list(REPLICATED) + list(SHARDED)
    shape2 = {"kv_norm": (1, D), "k_norm": (1, HD)}
    ins = [a[k].reshape(shape2.get(k, a[k].shape)) for a in (wts, mom, var) for k in names]
    n = len(names)

    def body(*refs):
        g8_ref, w_refs, m_refs, v_refs = refs[0], refs[1:1 + n], refs[1 + n:1 + 2 * n], refs[1 + 2 * n:1 + 3 * n]
        loss_ref, outs, red_ref = refs[1 + 3 * n], refs[2 + 3 * n:-1], refs[-1]
        me = _my_index()
        acc = g8_ref[0]
        for q in range(1, NDEV):
            acc = acc + g8_ref[q]
        red_ref[...] = acc
        loss_ref[...] = red_ref[LOSS_ROW:LOSS_ROW + 1, 0:1]

        def mine(rows, width):
            acc = jnp.zeros((rows.stop - rows.start, width), F32)
            for p_ in range(NDEV):
                acc = acc + jnp.where(me == p_, red_ref[rows, p_ * width:(p_ + 1) * width], 0.0)
            return acc

        for i, k in enumerate(names):
            if k in REPLICATED:
                r0, nr, width = REPLICATED[k]
                g = red_ref[r0:r0 + nr, 0:width]
            elif k == "conv_b_in":
                half = D // (2 * D // NDEV)
                acc = jnp.zeros((2, 2 * D // NDEV), F32)
                for p_ in range(NDEV):
                    c0 = (p_ % half) * (2 * D // NDEV)
                    part = jnp.concatenate([red_ref[32 + 2 * j + p_ // half:33 + 2 * j + p_ // half, c0:c0 + 2 * D // NDEV]
                                            for j in range(2)], axis=0)
                    acc = acc + jnp.where(me == p_, part, 0.0)
                g = acc
            else:
                r0, nr = SHARDED[k]
                g = mine(slice(r0, r0 + nr), D // NDEV)
            w_, m_, v_ = w_refs[i], m_refs[i], v_refs[i]
            g_out, d_out, m_out, v_out = outs[4 * i:4 * i + 4]
            if k == "conv_dw":
                for j in range(2):
                    gj = g[CW * j:CW * (j + 1)]
                    g_out[j] = gj
                    d_out[j], m_out[j], v_out[j] = _adamw_math(w_[j], m_[j], v_[j], gj)
            else:
                g_out[...] = g
                d_out[...], m_out[...], v_out[...] = _adamw_math(w_[...], m_[...], v_[...], g)

    out_shape = [_sds((1, 1), F32)] + [_sds(ins[i].shape, F32) for i in range(n) for _ in range(4)]
    res = pl.pallas_call(body, name="adamw_small", out_shape=out_shape, scratch_shapes=[pltpu.VMEM((SMALL_ROWS, D), F32)])(g8, *ins)
    out = {k: tuple(o.reshape(wts[k].shape) for o in res[1 + 4 * i:5 + 4 * i]) for i, k in enumerate(names)}
    return res[0], out


NAMES = ["meta_tokens", "norm_mix", "norm_ffn", "conv_w_in", "conv_b_in", "conv_dw", "conv_ln_g", "conv_ln_b", "conv_w_out",
         "conv_b_out", "kv_norm", "w_kv", "k_norm", "w_q", "q_norm", "attn_sinks", "w_o", "ffn_w_gate", "ffn_w_up", "ffn_w_down"]


def kernel(x, meta_tokens, norm_mix, norm_ffn, conv_w_in, conv_b_in, conv_dw, conv_ln_g, conv_ln_b, conv_w_out, conv_b_out, kv_norm, w_kv, k_norm, w_q, q_norm, attn_sinks, w_o, ffn_w_gate, ffn_w_up, ffn_w_down, loss_target, m_meta_tokens, m_norm_mix, m_norm_ffn, m_conv_w_in, m_conv_b_in, m_conv_dw, m_conv_ln_g, m_conv_ln_b, m_conv_w_out, m_conv_b_out, m_kv_norm, m_w_kv, m_k_norm, m_w_q, m_q_norm, m_attn_sinks, m_w_o, m_ffn_w_gate, m_ffn_w_up, m_ffn_w_down, v_meta_tokens, v_norm_mix, v_norm_ffn, v_conv_w_in, v_conv_b_in, v_conv_dw, v_conv_ln_g, v_conv_ln_b, v_conv_w_out, v_conv_b_out, v_kv_norm, v_w_kv, v_k_norm, v_w_q, v_q_norm, v_attn_sinks, v_w_o, v_ffn_w_gate, v_ffn_w_up, v_ffn_w_down):
    wts = dict(zip(NAMES, (meta_tokens, norm_mix, norm_ffn, conv_w_in, conv_b_in, conv_dw, conv_ln_g, conv_ln_b, conv_w_out,
                           conv_b_out, kv_norm, w_kv, k_norm, w_q, q_norm, attn_sinks, w_o, ffn_w_gate, ffn_w_up, ffn_w_down)))
    mom = dict(zip(NAMES, (m_meta_tokens, m_norm_mix, m_norm_ffn, m_conv_w_in, m_conv_b_in, m_conv_dw, m_conv_ln_g, m_conv_ln_b,
                           m_conv_w_out, m_conv_b_out, m_kv_norm, m_w_kv, m_k_norm, m_w_q, m_q_norm, m_attn_sinks, m_w_o,
                           m_ffn_w_gate, m_ffn_w_up, m_ffn_w_down)))
    var = dict(zip(NAMES, (v_meta_tokens, v_norm_mix, v_norm_ffn, v_conv_w_in, v_conv_b_in, v_conv_dw, v_conv_ln_g, v_conv_ln_b,
                           v_conv_w_out, v_conv_b_out, v_kv_norm, v_w_kv, v_k_norm, v_w_q, v_q_norm, v_attn_sinks, v_w_o,
                           v_ffn_w_gate, v_ffn_w_up, v_ffn_w_down)))
    for k in TRANSPOSED:
        wts[k], mom[k], var[k] = (jnp.swapaxes(a, 1, 2) for a in (wts[k], mom[k], var[k]))

    big_names = list(BIG)
    layers = cast_bf16([wts[k] for k in big_names])
    shards = {(k, l): blk for k, per_layer in zip(big_names, layers) for l, blk in enumerate(per_layer)}
    vec_names = ["meta_tokens", "conv_b_in", "conv_dw", "conv_ln_g", "conv_ln_b", "conv_b_out"]
    full = dict(zip(vec_names, all_gather_blocks([wts[k] for k in vec_names])))
    join_vec = lambda a: jnp.moveaxis(a, 0, -2).reshape(a.shape[1:-1] + (NDEV * a.shape[-1],))
    w = {}
    w["conv_b_in"] = join_vec(full["conv_b_in"]).reshape(2, 1, 2 * D)
    w["conv_dw"] = join_vec(full["conv_dw"])
    for k in ("conv_ln_g", "conv_ln_b", "conv_b_out"):
        w[k] = join_vec(full[k]).reshape(2, 1, D)
    w["norm_mix"] = norm_mix.reshape(4, 1, D)
    w["norm_ffn"] = norm_ffn.reshape(4, 1, D)
    w["kv_norm"] = kv_norm.reshape(1, D)
    w["k_norm"] = k_norm.reshape(1, HD)
    w["q_norm"] = q_norm.reshape(2, 1, HD)
    w["attn_sinks"] = attn_sinks.reshape(2, 1, NH)

    loss_blk, grad_x, gbig, arrived, gs = local_step(x, loss_target, full["meta_tokens"], w, shards)

    packed = pack_small(gs, loss_blk)

    grads, delta, new_m, new_v = {}, {}, {}, {}
    tail = EXCHANGE_PLAN["tail"]
    waiting = {nm.rstrip("0123456789") for nm, _ in tail}
    order = sorted([k for k in big_names if k not in waiting], key=lambda k: -wts[k].size) + [k for k in big_names if k in waiting]
    small8 = None
    for pos, k in enumerate(order):
        flat2 = wts[k].ndim == 2
        as3 = (lambda a: a[None]) if flat2 else (lambda a: a)
        riders = tail if pos == 0 else []
        gat = [packed] if pos == 1 else []
        parts = [arrived[k]] if flat2 else [arrived[f"{k}{i}"] for i in range(wts[k].shape[0])]
        outs, got_x, got_g = adamw_big(as3(wts[k]), as3(mom[k]), as3(var[k]), parts, "adamw_" + k,
                                       xch=[(gbig[nm], EXCHANGE_KIND[nm.rstrip("0123456789")], ks) for nm, ks in riders], gat=gat)
        for (nm, _), got in zip(riders, got_x):
            arrived[nm].append(got)
        if gat:
            small8 = got_g[0]
        grads[k], delta[k], new_m[k], new_v[k] = [o[0] if flat2 else (jnp.swapaxes(o, 1, 2) if k in TRANSPOSED else o) for o in outs]
    loss, small = adamw_small(small8, wts, mom, var)
    for k, (g_, d_, m_, v_) in small.items():
        grads[k], delta[k], new_m[k], new_v[k] = g_, d_, m_, v_
    return (loss.reshape(()), grad_x, *[grads[k] for k in NAMES], *[delta[k] for k in NAMES], *[new_m[k] for k in NAMES],
            *[new_v[k] for k in NAMES])
```

```python
import functools

import jax
import jax.numpy as jnp
from jax import lax
from jax.experimental import pallas as pl
from jax.experimental.pallas import tpu as pltpu

F32 = jnp.float32
BF = jnp.bfloat16

D = 1024
DFF = 2816
NH = 16
NKV = 4
HD = 64
KVD = NKV * HD
NMETA = 16
CW = 31
HALO = 32
CHUNK = 32
QB = 128
EPS = 1e-6
NEG = -1e30
NDEV = 8
SCALE = HD ** -0.5

LR, B1, B2, AEPS, WD, STEP = 0.001, 0.9, 0.999, 1e-08, 0.01, 10

VMEM_LIMIT = 56 * 2 ** 20
MESH = pl.DeviceIdType.MESH


def _cp(n):
    return pltpu.CompilerParams(dimension_semantics=("arbitrary",) * n, vmem_limit_bytes=VMEM_LIMIT)


def _row(tm, c):
    return pl.BlockSpec((tm, c), lambda i: (i, 0))


def _res(shape):
    return pl.BlockSpec(shape, lambda i: (0,) * len(shape), pipeline_mode=pl.Buffered(1))


def _lay(l, shape):
    return pl.BlockSpec((None,) + tuple(shape), lambda i: (l,) + (0,) * len(shape), pipeline_mode=pl.Buffered(1))


def _acc(shape):
    return pl.BlockSpec(shape, lambda i: (0,) * len(shape))


def _sds(shape, dt):
    return jax.ShapeDtypeStruct(tuple(shape), dt)


def _dot(a, b):
    return jnp.dot(a.astype(BF), b.astype(BF), preferred_element_type=F32)


def _dot_nt(a, b):
    return lax.dot_general(a.astype(BF), b.astype(BF), (((1,), (1,)), ((), ())), preferred_element_type=F32)


def _dot_tn(a, b):
    return lax.dot_general(a.astype(BF), b.astype(BF), (((0,), (0,)), ((), ())), preferred_element_type=F32)


def _rstd(x):
    return lax.rsqrt(jnp.mean(x * x, axis=-1, keepdims=True) + EPS)


def _rms_bwd(x, g, dy):
    r = _rstd(x)
    z = dy * g
    dx = r * z - x * (r * r * r * jnp.mean(z * x, axis=-1, keepdims=True))
    return dx, jnp.sum(dy * x * r, axis=0, keepdims=True)


def _sig(x):
    return jax.nn.sigmoid(x)


def _fold8(x):
    out = x[0:8]
    for k in range(1, x.shape[0] // 8):
        out = out + x[8 * k:8 * k + 8]
    return out


def _shifted(win):
    return [win] + [pltpu.roll(win, 2 * CHUNK - rho, 0) for rho in range(1, 8)]


def _tap(phases, o):
    return phases[o % 8][8 * (o // 8):8 * (o // 8) + CHUNK]


def _init(ref, first):
    @pl.when(first)
    def _():
        ref[...] = jnp.zeros_like(ref)


def _my_index():
    return 4 * lax.axis_index("x") + 2 * lax.axis_index("y") + lax.axis_index("c")


def _coords(idx):
    return (idx // 4, (idx // 2) % 2, idx % 2)


ALL = tuple(range(NDEV))
H1, H2 = (0, 1, 2, 4, 6), (3, 5, 7)


def _xch_shapes(xch):
    return [_sds((len(ks),) + ((a.shape[0] // NDEV, a.shape[1]) if k == "rows" else tuple(a.shape[1:])), a.dtype) for a, k, ks in xch]


def _xch_scratch(n):
    return [pltpu.SemaphoreType.DMA((n, NDEV)), pltpu.SemaphoreType.DMA((n, NDEV)), pltpu.SemaphoreType.DMA((n,))]


def _xch_copies(meta, srcs, outs, sems, arrivals):
    send_sems, recv_sems, local_sems = sems
    me = _my_index()

    def piece(a, p):
        if meta[a][0] == "rows":
            r = srcs[a].shape[0] // NDEV
            return srcs[a].at[pl.ds(p * r, r), :]
        return srcs[a].at[p]

    def remote(a, i, k, src):
        return pltpu.make_async_remote_copy(
            src_ref=src, dst_ref=outs[a].at[i], send_sem=send_sems.at[a, k], recv_sem=recv_sems.at[a, k],
            device_id=_coords(me ^ k), device_id_type=MESH)

    local, sends, recvs = [], [], []
    for a, (_, ks) in enumerate(meta):
        for i, k in enumerate(ks):
            if k == 0:
                local.append(pltpu.make_async_copy(piece(a, me), outs[a].at[i], local_sems.at[a]))
            else:
                sends.append(remote(a, i, k, piece(a, me ^ k)))
                if arrivals:
                    recvs.append(remote(a, i, k, piece(a, me)))
    return local, sends, recvs


def _xch_start(meta, srcs, outs, sems):
    local, sends, _ = _xch_copies(meta, srcs, outs, sems, False)
    for cp in local + sends:
        cp.start()


def _xch_wait(meta, srcs, outs, sems):
    local, sends, recvs = _xch_copies(meta, srcs, outs, sems, True)
    for cp in recvs:
        cp.wait_recv()
    for cp in sends:
        cp.wait_send()
    for cp in local:
        cp.wait()


def _gat_copies(srcs, outs, sems):
    send_sems, recv_sems, local_sems = sems
    x, y, c = lax.axis_index("x"), lax.axis_index("y"), lax.axis_index("c")
    me, sibling = (x, y, c), (x, y, 1 - c)
    chips = [(1 - x, y), (x, 1 - y), (1 - x, 1 - y)]

    def copy(a, k, owner, to, from_block=False):
        slot = outs[a].at[4 * owner[0] + 2 * owner[1] + owner[2]]
        return pltpu.make_async_remote_copy(
            src_ref=srcs[a] if from_block else slot, dst_ref=slot, send_sem=send_sems.at[a, k], recv_sem=recv_sems.at[a, k],
            device_id=to, device_id_type=MESH)

    n = len(srcs)
    local = lambda: [pltpu.make_async_copy(srcs[a], outs[a].at[4 * x + 2 * y + c], local_sems.at[a]) for a in range(n)]
    first = lambda: [cp for a in range(n) for cp in
                     [copy(a, 0, me, sibling, True)] + [copy(a, 1 + j, me, (*chip, c), True) for j, chip in enumerate(chips)]]
    landed = lambda: [copy(a, 1 + j, (*chip, c), me) for a in range(n) for j, chip in enumerate(chips)]
    passed = lambda: [copy(a, 4 + j, (*chip, c), sibling) for a in range(n) for j, chip in enumerate(chips)]
    final = lambda: [cp for a in range(n) for cp in
                     [copy(a, 0, sibling, me)] + [copy(a, 4 + j, (*chip, 1 - c), me) for j, chip in enumerate(chips)]]
    return local, first, landed, passed, final


def _gat_start(srcs, outs, sems):
    local, first, _, _, _ = _gat_copies(srcs, outs, sems)
    for cp in local() + first():
        cp.start()


def _gat_forward(srcs, outs, sems):
    _, _, landed, passed, _ = _gat_copies(srcs, outs, sems)
    for got, on in zip(landed(), passed()):
        got.wait_recv()
        on.start()


def _gat_wait(srcs, outs, sems):
    local, first, _, passed, final = _gat_copies(srcs, outs, sems)
    for cp in final():
        cp.wait_recv()
    for cp in first() + passed():
        cp.wait_send()
    for cp in local():
        cp.wait()


def _call(body, name, grid, in_specs, out_specs, out_shape, args, scratch=(), xch=(), gat=()):
    n_in, n_out, n_x, n_g, n_s = len(in_specs), len(out_specs), len(xch), len(gat), len(scratch)
    kinds = [(k, ks) for _, k, ks in xch]
    total = 1
    for g in grid:
        total *= g

    def wrapped(*refs):
        ins, refs = refs[:n_in], refs[n_in:]
        x_src, refs = refs[:n_x], refs[n_x:]
        g_src, refs = refs[:n_g], refs[n_g:]
        outs, refs = refs[:n_out], refs[n_out:]
        x_out, refs = refs[:n_x], refs[n_x:]
        g_out, refs = refs[:n_g], refs[n_g:]
        own, refs = refs[:n_s], refs[n_s:]
        x_sems, g_sems = (refs[:3], refs[3:]) if n_x else ((), refs)
        step = pl.program_id(0)
        for d in range(1, len(grid)):
            step = step * grid[d] + pl.program_id(d)
        if n_x or n_g:
            @pl.when(step == 0)
            def _():
                if n_x:
                    _xch_start(kinds, x_src, x_out, x_sems)
                if n_g:
                    _gat_start(g_src, g_out, g_sems)

        body(*ins, *outs, *own)
        if n_g:
            @pl.when(step == max(total - 2, 0))
            def _():
                _gat_forward(g_src, g_out, g_sems)

        if n_x or n_g:
            @pl.when(step == total - 1)
            def _():
                if n_x:
                    _xch_wait(kinds, x_src, x_out, x_sems)
                if n_g:
                    _gat_wait(g_src, g_out, g_sems)

    any_spec = pl.BlockSpec(memory_space=pl.ANY)
    g_shapes = [_sds((NDEV,) + tuple(a.shape), a.dtype) for a in gat]
    res = pl.pallas_call(
        wrapped, name=name, grid=grid, in_specs=list(in_specs) + [any_spec] * (n_x + n_g),
        out_specs=list(out_specs) + [any_spec] * (n_x + n_g), out_shape=list(out_shape) + _xch_shapes(xch) + g_shapes,
        scratch_shapes=list(scratch) + (_xch_scratch(n_x) if n_x else []) + (_xch_scratch(n_g) if n_g else []),
        compiler_params=_cp(len(grid)),
    )(*args, *[a for a, _, _ in xch], *gat)
    return res[:n_out], res[n_out:n_out + n_x], res[n_out + n_x:]


def embed(x, meta8, lp, gat):
    bl, seq, _ = x.shape
    c8 = D // NDEV
    cb = 2 * c8

    def body(x_ref, m_ref, h_ref):
        h_ref[0:NMETA, :] = jnp.concatenate([m_ref[0], m_ref[1]], axis=1)
        h_ref[NMETA:NMETA + seq, :] = x_ref[...]
        h_ref[NMETA + seq:, :] = jnp.zeros((lp - NMETA - seq, cb), F32)

    (h0,), _, got = _call(
        body, "embed", (bl, D // cb),
        [pl.BlockSpec((None, seq, cb), lambda b, c: (b, 0, c)), pl.BlockSpec((2, NMETA, c8), lambda b, c: (c, 0, 0))],
        [pl.BlockSpec((None, lp, cb), lambda b, c: (b, 0, c))], [_sds((bl, lp, D), F32)], (x, meta8), gat=gat)
    return h0, got


def conv_in_fwd(h, nm, l, w_in, b_in, i, tm):
    t = h.shape[0]

    def body(h_ref, g_ref, w_ref, b_ref, u_ref, big_ref, a_ref):
        x = h_ref[...]
        ub = (x * _rstd(x) * g_ref[...]).astype(BF)
        u_ref[...] = ub
        big = jnp.dot(ub, w_ref[...], preferred_element_type=F32) + b_ref[...]
        big_ref[...] = big.astype(BF)
        a_ref[...] = big[:, :D] * _sig(big[:, D:])

    return pl.pallas_call(
        body, name=f"conv_in_fwd{i}", grid=(t // tm,),
        in_specs=[_row(tm, D), _lay(l, (1, D)), _res((D, 2 * D)), _lay(i, (1, 2 * D))],
        out_specs=[_row(tm, D), _row(tm, 2 * D), _row(tm, D)],
        out_shape=[_sds((t, D), BF), _sds((t, 2 * D), BF), _sds((t, D), F32)],
        compiler_params=_cp(1),
    )(h, nm, w_in, b_in)


def _prev_halo(tm):
    return pl.BlockSpec((HALO, D), lambda i: (jnp.maximum(i * (tm // HALO) - 1, 0), 0))


def _next_halo(tm, t):
    return pl.BlockSpec((HALO, D), lambda i: (jnp.minimum((i + 1) * (tm // HALO), t // HALO - 1), 0))


def conv_mid_fwd(a, dw, ln_g, ln_b, i, tm, tpb, gat):
    t = a.shape[0]

    def body(a_ref, halo_ref, dw_ref, g_ref, b_ref, c_ref, s_ref, ext):
        first = pl.program_id(0) % tpb == 0
        ext[0:HALO] = jnp.where(first, 0.0, halo_ref[...])
        ext[HALO:] = a_ref[...]

        def chunk(k, carry):
            r0 = pl.multiple_of(k * CHUNK, CHUNK)
            win = _shifted(ext[pl.ds(r0, 2 * CHUNK), :])
            c = jnp.zeros((CHUNK, D), F32)
            for j in range(CW):
                c = c + dw_ref[j:j + 1, :] * _tap(win, j + 2)
            c_ref[pl.ds(r0, CHUNK), :] = c
            mu = jnp.mean(c, axis=-1, keepdims=True)
            xc = c - mu
            n = xc * lax.rsqrt(jnp.mean(xc * xc, axis=-1, keepdims=True) + EPS) * g_ref[...] + b_ref[...]
            s_ref[pl.ds(r0, CHUNK), :] = (n * _sig(n)).astype(BF)
            return carry

        lax.fori_loop(0, tm // CHUNK, chunk, 0)

    return _call(
        body, f"conv_mid_fwd{i}", (t // tm,),
        [_row(tm, D), _prev_halo(tm), _lay(i, (CW, D)), _lay(i, (1, D)), _lay(i, (1, D))],
        [_row(tm, D), _row(tm, D)], [_sds((t, D), F32), _sds((t, D), BF)], (a, a, dw, ln_g, ln_b),
        scratch=[pltpu.VMEM((tm + HALO, D), F32)], gat=gat)


def mixer_ffn_fwd(h, s, w_out, lw, bias, nf, l, wg, wu, wd, tm, gat):
    t = h.shape[0]

    def body(*refs):
        if bias is None:
            h_ref, s_ref, w_ref, nf_ref, wg_ref, wu_ref, wd_ref, h1_ref, u_ref, g_ref, up_ref, hid_ref, h2_ref = refs
            y = 0.0
        else:
            h_ref, s_ref, w_ref, b_ref, nf_ref, wg_ref, wu_ref, wd_ref, h1_ref, u_ref, g_ref, up_ref, hid_ref, h2_ref = refs
            y = b_ref[...]
        h1 = h_ref[...] + (jnp.dot(s_ref[...], w_ref[...], preferred_element_type=F32) + y)
        h1_ref[...] = h1
        ub = (h1 * _rstd(h1) * nf_ref[...]).astype(BF)
        u_ref[...] = ub
        g = _dot_nt(ub, wg_ref[...])
        up = _dot_nt(ub, wu_ref[...])
        g_ref[...] = g.astype(BF)
        up_ref[...] = up.astype(BF)
        hid = (g * _sig(g) * up).astype(BF)
        hid_ref[...] = hid
        h2_ref[...] = h1 + jnp.dot(hid, wd_ref[...], preferred_element_type=F32)

    ins = [h, s, w_out] + ([] if bias is None else [bias]) + [nf, wg, wu, wd]
    specs = ([_row(tm, D), _row(tm, D), _res((D, D))] + ([] if bias is None else [_lay(lw, (1, D))])
             + [_lay(l, (1, D)), _res((DFF, D)), _res((DFF, D)), _res((DFF, D))])
    return _call(
        body, f"mixer_ffn_fwd{l}", (t // tm,), specs,
        [_row(tm, D), _row(tm, D), _row(tm, DFF), _row(tm, DFF), _row(tm, DFF), _row(tm, D)],
        [_sds((t, D), F32), _sds((t, D), BF), _sds((t, DFF), BF), _sds((t, DFF), BF), _sds((t, DFF), BF), _sds((t, D), F32)],
        ins, gat=gat)


def _seg_rms(x, g, nseg):
    outs = []
    for s in range(nseg):
        xs = x[:, HD * s:HD * s + HD]
        outs.append(xs * _rstd(xs) * g)
    return jnp.concatenate(outs, axis=1)


def kv_fwd(h, kvn, w_kv, kng, tm):
    t = h.shape[0]

    def body(h_ref, g_ref, w_ref, kg_ref, kn_ref, kv_ref, k_ref, v_ref):
        x = h_ref[...]
        kn = (x * _rstd(x) * g_ref[...]).astype(BF)
        kn_ref[...] = kn
        kv = jnp.dot(kn, w_ref[...], preferred_element_type=F32)
        kv_ref[...] = kv
        k_ref[...] = _seg_rms(kv[:, :KVD], kg_ref[...], NKV).astype(BF)
        v_ref[...] = kv[:, KVD:].astype(BF)

    return pl.pallas_call(
        body, name="kv_fwd", grid=(t // tm,),
        in_specs=[_row(tm, D), _res((1, D)), _res((D, 2 * KVD)), _res((1, HD))],
        out_specs=[_row(tm, D), _row(tm, 2 * KVD), _row(tm, KVD), _row(tm, KVD)],
        out_shape=[_sds((t, D), BF), _sds((t, 2 * KVD), F32), _sds((t, KVD), BF), _sds((t, KVD), BF)],
        compiler_params=_cp(1),
    )(h, kvn, w_kv, kng)


def q_fwd(h, nm, l, w_q, j, tm):
    t = h.shape[0]

    def body(h_ref, g_ref, w_ref, u_ref, q_ref):
        x = h_ref[...]
        ub = (x * _rstd(x) * g_ref[...]).astype(BF)
        u_ref[...] = ub
        q_ref[...] = jnp.dot(ub, w_ref[...], preferred_element_type=F32)

    return pl.pallas_call(
        body, name=f"q_fwd{j}", grid=(t // tm,),
        in_specs=[_row(tm, D), _lay(l, (1, D)), _res((D, D))],
        out_specs=[_row(tm, D), _row(tm, D)], out_shape=[_sds((t, D), BF), _sds((t, D), F32)],
        compiler_params=_cp(1),
    )(h, nm, w_q)


RQ = NH // NKV


NKEYS = 2 * QB + NMETA


def _attn_mask(n, start):
    shape = (RQ * QB, NKEYS)
    qpos = n * QB + (lax.broadcasted_iota(jnp.int32, shape, 0) & (QB - 1))
    col = lax.broadcasted_iota(jnp.int32, shape, 1)
    in_band = col < 2 * QB
    kpos = jnp.where(in_band, start + col, col - 2 * QB)
    return (kpos <= qpos) & ((col >= 2 * QB) | ((qpos - kpos < QB) & (kpos >= NMETA)))


def _keys(ref, band, gs):
    return jnp.concatenate([ref[band, gs], ref[0:NMETA, gs]], axis=0)


def _keys_t(ref, band, gs):
    return jnp.concatenate([ref[gs, band], ref[gs, 0:NMETA]], axis=1)


def transpose_seq(a, name):
    bl, r, c = a.shape

    def body(a_ref, o_ref):
        o_ref[...] = a_ref[...].T

    return pl.pallas_call(
        body, name=name, grid=(bl,), in_specs=[pl.BlockSpec((None, r, c), lambda b: (b, 0, 0))],
        out_specs=pl.BlockSpec((None, c, r), lambda b: (b, 0, 0)), out_shape=_sds((bl, c, r), a.dtype), compiler_params=_cp(1),
    )(a)


def sum_transposed(a0, a1):
    bl, c, r = a0.shape

    def body(a0_ref, a1_ref, o_ref):
        o_ref[...] = (a0_ref[...] + a1_ref[...]).T

    spec = pl.BlockSpec((None, c, r), lambda b: (b, 0, 0))
    return pl.pallas_call(
        body, name="sum_transposed", grid=(bl,), in_specs=[spec, spec],
        out_specs=pl.BlockSpec((r, c), lambda b: (b, 0)), out_shape=_sds((bl * r, c), a0.dtype), compiler_params=_cp(1),
    )(a0, a1)


def _stack_heads(ref, g, fn):
    return jnp.concatenate([fn(ref[:, HD * (g * RQ + r):HD * (g * RQ + r) + HD]) for r in range(RQ)], axis=0)


def _stack_cols(ref, g):
    return jnp.concatenate([ref[:, g * RQ + r:g * RQ + r + 1] for r in range(RQ)], axis=0)


def _stack_sinks(sk_ref, g):
    return jnp.concatenate([jnp.broadcast_to(sk_ref[:, g * RQ + r:g * RQ + r + 1], (QB, 1)) for r in range(RQ)], axis=0)


def attn_fwd(q, kt, v, qg, sinks, j, bl, lp, gat):
    t = q.shape[0]
    nb = lp // QB

    def body(q_ref, kt_ref, v_ref, qg_ref, sk_ref, o_ref, lse_ref):
        n = pl.program_id(0)
        start = pl.multiple_of(jnp.maximum(n - 1, 0) * QB, QB)
        mask = _attn_mask(n, start)
        band = pl.ds(start, 2 * QB)
        lane = lax.broadcasted_iota(jnp.int32, (QB, NH), 1)
        ones = jnp.ones((NKEYS, HD), BF)
        pairs = [(b, g) for b in range(bl) for g in range(NKV)]
        gsl = [slice(HD * g, HD * g + HD) for g in range(NKV)]
        qns = [_stack_heads(q_ref.at[b], g, lambda x: (x * _rstd(x) * (qg_ref[...] * SCALE)).astype(BF)) for b, g in pairs]
        ss = [jnp.where(mask, _dot(qns[i], _keys_t(kt_ref.at[b], band, gsl[g])), NEG) for i, (b, g) in enumerate(pairs)]
        sinks = [_stack_sinks(sk_ref, g) for g in range(NKV)]
        mxs = [jnp.maximum(jnp.max(ss[i], -1, keepdims=True), sinks[g]) for i, (b, g) in enumerate(pairs)]
        oas = [_dot(jnp.exp(ss[i] - mxs[i]), jnp.concatenate([_keys(v_ref.at[b], band, gsl[g]), ones], axis=1))
               for i, (b, g) in enumerate(pairs)]
        lses = [jnp.zeros((QB, NH), F32) for _ in range(bl)]
        for i, (b, g) in enumerate(pairs):
            den = oas[i][:, HD:HD + 1] + jnp.exp(sinks[g] - mxs[i])
            o = oas[i][:, :HD] * (1.0 / den)
            l = mxs[i] + jnp.log(den)
            for r in range(RQ):
                h = g * RQ + r
                o_ref[b, :, HD * h:HD * h + HD] = o[r * QB:(r + 1) * QB].astype(BF)
                lses[b] = jnp.where(lane == h, l[r * QB:(r + 1) * QB], lses[b])
        for b in range(bl):
            lse_ref[b] = lses[b]

    blk = lambda c: pl.BlockSpec((bl, QB, c), lambda n: (0, n, 0))
    (o, lse), _, got = _call(
        body, f"attn_fwd{j}", (nb,),
        [blk(D), pl.BlockSpec((bl, KVD, lp), lambda n: (0, 0, 0)), pl.BlockSpec((bl, lp, KVD), lambda n: (0, 0, 0)),
         pl.BlockSpec((None, 1, HD), lambda n: (j, 0, 0)), pl.BlockSpec((None, 1, NH), lambda n: (j, 0, 0))],
        [blk(D), blk(NH)], [_sds((bl, lp, D), BF), _sds((bl, lp, NH), F32)], (q.reshape(bl, lp, D), kt, v, qg, sinks), gat=gat)
    return (o.reshape(t, D), lse.reshape(t, NH)), (), got


def loss_fwd(h, tgt):
    bl, lp, _ = h.shape
    seq = tgt.shape[1]
    cb = 256

    def body(h_ref, t_ref, dh_ref, loss_ref):
        _init(loss_ref, (pl.program_id(0) == 0) & (pl.program_id(1) == 0))
        err = h_ref[NMETA:NMETA + seq, :] - t_ref[...]
        dh_ref[...] = jnp.zeros_like(dh_ref)
        dh_ref[NMETA:NMETA + seq, :] = err * (1.0 / D)
        loss_ref[...] += (0.5 / D) * jnp.sum(err * err)

    return pl.pallas_call(
        body, name="loss_fwd", grid=(bl, D // cb),
        in_specs=[pl.BlockSpec((None, lp, cb), lambda b, c: (b, 0, c)), pl.BlockSpec((None, seq, cb), lambda b, c: (b, 0, c))],
        out_specs=[pl.BlockSpec((None, lp, cb), lambda b, c: (b, 0, c)), pl.BlockSpec((8, 128), lambda b, c: (0, 0))],
        out_shape=[_sds((bl, lp, D), F32), _sds((8, 128), F32)],
        compiler_params=_cp(2),
    )(h, tgt)


def ffn_bwd_x(dh2, g, up, h1, nf, l, wd, wg, wu, w_o, tm, xch):
    t = dh2.shape[0]

    def body(dh2_ref, g_ref, up_ref, h1_ref, nf_ref, wd_ref, wg_ref, wu_ref, *rest):
        if w_o is None:
            dg_ref, du_ref, dh1_ref, dnf_ref = rest
        else:
            wo_ref, dg_ref, du_ref, dh1_ref, dnf_ref, do_ref = rest
        _init(dnf_ref, pl.program_id(0) == 0)
        dh2v = dh2_ref[...]
        dhid = _dot_nt(dh2v, wd_ref[...])
        gv = g_ref[...].astype(F32)
        uv = up_ref[...].astype(F32)
        sg = _sig(gv)
        dgv = (dhid * uv * (sg * (1.0 + gv * (1.0 - sg)))).astype(BF)
        duv = (dhid * (gv * sg)).astype(BF)
        dg_ref[...] = dgv
        du_ref[...] = duv
        dnorm = _dot(dgv, wg_ref[...]) + _dot(duv, wu_ref[...])
        dx, dnf = _rms_bwd(h1_ref[...], nf_ref[...], dnorm)
        dh1 = dh2v + dx
        dh1_ref[...] = dh1
        dnf_ref[...] += dnf
        if w_o is not None:
            do_ref[...] = _dot_nt(dh1, wo_ref[...]).astype(BF)

    attn = w_o is not None
    return _call(
        body, f"ffn_bwd_x{l}", (t // tm,),
        [_row(tm, D), _row(tm, DFF), _row(tm, DFF), _row(tm, D), _lay(l, (1, D)),
         _res((DFF, D)), _res((DFF, D)), _res((DFF, D))] + ([_res((D, D))] if attn else []),
        [_row(tm, DFF), _row(tm, DFF), _row(tm, D), _acc((1, D))] + ([_row(tm, D)] if attn else []),
        [_sds((t, DFF), BF), _sds((t, DFF), BF), _sds((t, D), F32), _sds((1, D), F32)] + ([_sds((t, D), BF)] if attn else []),
        (dh2, g, up, h1, nf, wd, wg, wu) + ((w_o,) if attn else ()), xch=xch)


def mm_tn(x, dy, tm, name, split=False, transposed=False, xch=()):
    t, kk = x.shape
    nn = dy.shape[1]
    n8 = nn // NDEV
    nsteps = t // tm

    def body(x_ref, dy_ref, o_ref, acc):
        i = pl.program_id(0)
        _init(acc, i == 0)
        acc[...] += _dot_tn(x_ref[...], dy_ref[...])

        @pl.when(i == nsteps - 1)
        def _():
            if split:
                for p in range(NDEV):
                    o_ref[p] = acc[:, p * n8:(p + 1) * n8].astype(BF)
            elif transposed:
                o_ref[...] = acc[...].T.astype(BF)
            else:
                o_ref[...] = acc[...].astype(BF)

    oshape = (NDEV, kk, n8) if split else ((nn, kk) if transposed else (kk, nn))
    (out,), got, _ = _call(body, name, (nsteps,), [_row(tm, kk), _row(tm, nn)], [_acc(oshape)], [_sds(oshape, BF)], (x, dy),
                           scratch=[pltpu.VMEM((kk, nn), F32)], xch=xch)
    return out, got


def proj_bwd(dy, w, h, g, lg, dh_in, tm, name, xch=()):
    t = h.shape[0]
    nn = dy.shape[1]
    wspec = _res(w.shape)
    gspec = _res((1, D)) if lg is None else _lay(lg, (1, D))

    def body(dy_ref, w_ref, h_ref, g_ref, dhin_ref, dh_ref, dg_ref):
        _init(dg_ref, pl.program_id(0) == 0)
        du = _dot_nt(dy_ref[...], w_ref[...])
        dx, dg = _rms_bwd(h_ref[...], g_ref[...], du)
        dh_ref[...] = dhin_ref[...] + dx
        dg_ref[...] += dg

    return _call(body, name, (t // tm,), [_row(tm, nn), wspec, _row(tm, D), gspec, _row(tm, D)],
                 [_row(tm, D), _acc((1, D))], [_sds((t, D), F32), _sds((1, D), F32)], (dy, w, h, g, dh_in), xch=xch)


def attn_bwd(q, k, kt, vt, do, o, lse, qg, sinks, j, bl, lp, xch):
    t = q.shape[0]
    nb = lp // QB

    def body(q_ref, k_ref, kt_ref, vt_ref, do_ref, o_ref, lse_ref, qg_ref, sk_ref, dq_ref, dk_ref, dv_ref, dqg_ref, dsk_ref):
        n = pl.program_id(0)
        for ref in (dk_ref, dv_ref, dqg_ref, dsk_ref):
            _init(ref, n == 0)
        start = pl.multiple_of(jnp.maximum(n - 1, 0) * QB, QB)
        mask = _attn_mask(n, start)
        band = pl.ds(start, 2 * QB)
        lane = lax.broadcasted_iota(jnp.int32, (1, NH), 1)
        dqg = jnp.zeros((1, HD), F32)
        dsk = jnp.zeros((1, NH), F32)
        pairs = [(b, g) for b in range(bl) for g in range(NKV)]
        idx = range(len(pairs))
        gsl = [slice(HD * g, HD * g + HD) for g in range(NKV)]
        qhs = [_stack_heads(q_ref.at[b], g, lambda x: x) for b, g in pairs]
        rss = [_rstd(qhs[i]) for i in idx]
        qns = [(qhs[i] * rss[i] * (qg_ref[...] * SCALE)).astype(BF) for i in idx]
        lss = [_stack_cols(lse_ref.at[b], g) for b, g in pairs]
        dohs = [_stack_heads(do_ref.at[b], g, lambda x: x) for b, g in pairs]
        deltas = [jnp.sum(dohs[i].astype(F32) * _stack_heads(o_ref.at[b], g, lambda x: x).astype(F32), axis=-1, keepdims=True)
                  for i, (b, g) in enumerate(pairs)]
        prs = [jnp.where(mask, jnp.exp(_dot(qns[i], _keys_t(kt_ref.at[b], band, gsl[g])) - lss[i]), 0.0)
               for i, (b, g) in enumerate(pairs)]
        dss = [(prs[i] * (_dot(dohs[i], _keys_t(vt_ref.at[b], band, gsl[g])) - deltas[i])).astype(BF)
               for i, (b, g) in enumerate(pairs)]
        for i, (b, g) in enumerate(pairs):
            gs = gsl[g]
            dkt = _dot_tn(qns[i], dss[i])
            dvt = _dot_tn(dohs[i], prs[i])
            dk_ref[b, gs, band] += dkt[:, :2 * QB]
            dv_ref[b, gs, band] += dvt[:, :2 * QB]
            dk_ref[b, gs, 0:NMETA] += dkt[:, 2 * QB:]
            dv_ref[b, gs, 0:NMETA] += dvt[:, 2 * QB:]
        dqns = [_dot(dss[i], _keys(k_ref.at[b], band, gsl[g])) * SCALE for i, (b, g) in enumerate(pairs)]
        for i, (b, g) in enumerate(pairs):
            qh, rs, dqn = qhs[i], rss[i], dqns[i]
            dsink = jnp.exp(_stack_sinks(sk_ref, g) - lss[i]) * deltas[i]
            z = dqn * qg_ref[...]
            dq = rs * z - qh * (rs * rs * rs * jnp.mean(z * qh, axis=-1, keepdims=True))
            dqg = dqg + jnp.sum(dqn * qh * rs, axis=0, keepdims=True)
            for r in range(RQ):
                h = g * RQ + r
                dq_ref[b, :, HD * h:HD * h + HD] = dq[r * QB:(r + 1) * QB]
                dsk = dsk + jnp.where(lane == h, -jnp.sum(dsink[r * QB:(r + 1) * QB]), 0.0)
        dqg_ref[...] += dqg
        dsk_ref[...] += dsk

    blk = lambda c: pl.BlockSpec((bl, QB, c), lambda n: (0, n, 0))
    seq = pl.BlockSpec((bl, lp, KVD), lambda n: (0, 0, 0))
    seq_t = pl.BlockSpec((bl, KVD, lp), lambda n: (0, 0, 0))
    as3 = lambda a: a.reshape(bl, lp, a.shape[-1])
    (dq, dk, dv, dqg, dsk), got, _ = _call(
        body, f"attn_bwd{j}", (nb,),
        [blk(D), seq, seq_t, seq_t, blk(D), blk(D), blk(NH),
         pl.BlockSpec((None, 1, HD), lambda n: (j, 0, 0)), pl.BlockSpec((None, 1, NH), lambda n: (j, 0, 0))],
        [blk(D), seq_t, seq_t, pl.BlockSpec((1, HD), lambda n: (0, 0)), pl.BlockSpec((1, NH), lambda n: (0, 0))],
        [_sds((bl, lp, D), F32), _sds((bl, KVD, lp), F32), _sds((bl, KVD, lp), F32), _sds((1, HD), F32), _sds((1, NH), F32)],
        (as3(q), k, kt, vt, as3(do), as3(o), as3(lse), qg, sinks), xch=xch)
    return (dq.reshape(t, D), dk, dv, dqg, dsk), got, ()


def kv_bwd_pre(dk, dv, kv, kng, tm):
    t = kv.shape[0]

    def body(dk_ref, dv_ref, kv_ref, g_ref, dkv_ref, dg_ref):
        _init(dg_ref, pl.program_id(0) == 0)
        dg = jnp.zeros((1, HD), F32)
        outs = []
        for s in range(NKV):
            sl = slice(HD * s, HD * s + HD)
            dx, dgs = _rms_bwd(kv_ref[:, sl], g_ref[...], dk_ref[:, sl])
            outs.append(dx)
            dg = dg + dgs
        dkv_ref[:, :KVD] = jnp.concatenate(outs, axis=1).astype(BF)
        dkv_ref[:, KVD:] = dv_ref[...].astype(BF)
        dg_ref[...] += dg

    return pl.pallas_call(
        body, name="kv_bwd_pre", grid=(t // tm,),
        in_specs=[_row(tm, KVD)] * 2 + [_row(tm, 2 * KVD), _res((1, HD))],
        out_specs=[_row(tm, 2 * KVD), _acc((1, HD))], out_shape=[_sds((t, 2 * KVD), BF), _sds((1, HD), F32)],
        compiler_params=_cp(1),
    )(dk, dv, kv, kng)


def conv_out_bwd(dh1, c, ln_g, ln_b, w_out, i, tm, xch):
    t = dh1.shape[0]

    def body(dh1_ref, c_ref, g_ref, b_ref, w_ref, dc_ref, dg_ref, db_ref, dbo_ref):
        first = pl.program_id(0) == 0
        _init(dg_ref, first)
        _init(db_ref, first)
        _init(dbo_ref, first)
        dh1v = dh1_ref[...]
        ds = _dot_nt(dh1v, w_ref[...])
        cv = c_ref[...]
        xc = cv - jnp.mean(cv, axis=-1, keepdims=True)
        rstd = lax.rsqrt(jnp.mean(xc * xc, axis=-1, keepdims=True) + EPS)
        xh = xc * rstd
        n = xh * g_ref[...] + b_ref[...]
        sg = _sig(n)
        dn = ds * (sg * (1.0 + n * (1.0 - sg)))
        dxh = dn * g_ref[...]
        dc_ref[...] = rstd * (dxh - jnp.mean(dxh, axis=-1, keepdims=True) - xh * jnp.mean(dxh * xh, axis=-1, keepdims=True))
        dg_ref[...] += jnp.sum(dn * xh, axis=0, keepdims=True)
        db_ref[...] += jnp.sum(dn, axis=0, keepdims=True)
        dbo_ref[...] += jnp.sum(dh1v, axis=0, keepdims=True)

    return _call(
        body, f"conv_out_bwd{i}", (t // tm,), [_row(tm, D), _row(tm, D), _lay(i, (1, D)), _lay(i, (1, D)), _res((D, D))],
        [_row(tm, D), _acc((1, D)), _acc((1, D)), _acc((1, D))], [_sds((t, D), F32)] + [_sds((1, D), F32)] * 3,
        (dh1, c, ln_g, ln_b, w_out), xch=xch)


def conv_mid_bwd(dc, a, big, dw, i, tm, tpb, xch):
    t = dc.shape[0]
    nsteps = t // tm

    def body(dc_ref, nxt_ref, a_ref, prv_ref, big_ref, dw_ref, da_ref, dbin_ref, ddw_ref, dce, ae, wacc, bacc):
        i_ = pl.program_id(0)
        _init(wacc, i_ == 0)
        _init(bacc, i_ == 0)
        dce[0:tm] = dc_ref[...]
        dce[tm:] = jnp.where(i_ % tpb == tpb - 1, 0.0, nxt_ref[...])
        ae[0:HALO] = jnp.where(i_ % tpb == 0, 0.0, prv_ref[...])
        ae[HALO:] = a_ref[...]

        def chunk(k, carry):
            r0 = pl.multiple_of(k * CHUNK, CHUNK)
            wdc = _shifted(dce[pl.ds(r0, 2 * CHUNK), :])
            wa = _shifted(ae[pl.ds(r0, 2 * CHUNK), :])
            dcc = wdc[0][0:CHUNK]
            da = jnp.zeros((CHUNK, D), F32)
            for j in range(CW):
                da = da + dw_ref[j:j + 1, :] * _tap(wdc, CW - 1 - j)
                wacc[j] += _fold8(dcc * _tap(wa, j + 2))
            bv = big_ref[pl.ds(r0, CHUNK), :].astype(F32)
            a1, sg = bv[:, :D], _sig(bv[:, D:])
            d1 = da * sg
            d2 = da * a1 * sg * (1.0 - sg)
            da_ref[pl.ds(r0, CHUNK), 0:D] = d1.astype(BF)
            da_ref[pl.ds(r0, CHUNK), D:2 * D] = d2.astype(BF)
            bacc[:, 0:D] += _fold8(d1)
            bacc[:, D:2 * D] += _fold8(d2)
            return carry

        lax.fori_loop(0, tm // CHUNK, chunk, 0)

        @pl.when(i_ == nsteps - 1)
        def _():
            dbin_ref[...] = jnp.sum(bacc[...], axis=0, keepdims=True)
            ddw_ref[...] = jnp.sum(wacc[...], axis=1)

    return _call(
        body, f"conv_mid_bwd{i}", (nsteps,),
        [_row(tm, D), _next_halo(tm, t), _row(tm, D), _prev_halo(tm), _row(tm, 2 * D), _lay(i, (CW, D))],
        [_row(tm, 2 * D), _acc((1, 2 * D)), _acc((CW + 1, D))],
        [_sds((t, 2 * D), BF), _sds((1, 2 * D), F32), _sds((CW + 1, D), F32)],
        (dc, dc, a, a, big, dw),
        scratch=[pltpu.VMEM((tm + HALO, D), F32), pltpu.VMEM((tm + HALO, D), F32),
                 pltpu.VMEM((CW + 1, 8, D), F32), pltpu.VMEM((8, 2 * D), F32)], xch=xch)


def input_grads(dh0, seq):
    bl, lp, _ = dh0.shape
    cb = 256

    def body(dh_ref, gx_ref, gm_ref):
        _init(gm_ref, pl.program_id(1) == 0)
        gx_ref[...] = dh_ref[NMETA:NMETA + seq, :]
        gm_ref[...] += dh_ref[0:NMETA, :]

    return pl.pallas_call(
        body, name="input_grads", grid=(D // cb, bl),
        in_specs=[pl.BlockSpec((None, lp, cb), lambda c, b: (b, 0, c))],
        out_specs=[pl.BlockSpec((None, seq, cb), lambda c, b: (b, 0, c)), pl.BlockSpec((NMETA, cb), lambda c, b: (0, c))],
        out_shape=[_sds((bl, seq, D), F32), _sds((NMETA, D), F32)],
        compiler_params=_cp(2),
    )(dh0)


GATHER_PLAN = {
    "embed": [("conv_w_in", 0), ("conv_w_out", 0)],
    "conv_mid_fwd0": [("ffn_w_gate", 0), ("ffn_w_up", 0), ("ffn_w_down", 0)],
    "mixer_ffn_fwd0": [("conv_w_in", 1), ("conv_w_out", 1), ("ffn_w_gate", 1)],
    "conv_mid_fwd1": [("ffn_w_up", 1), ("ffn_w_down", 1), ("w_kv", 0), ("w_q", 0)],
    "mixer_ffn_fwd1": [("w_o", 0), ("ffn_w_down", 2)],
    "attn_fwd0": [("ffn_w_gate", 2), ("ffn_w_up", 2), ("w_q", 1), ("w_o", 1)],
    "attn_fwd1": [("ffn_w_gate", 3), ("ffn_w_up", 3), ("ffn_w_down", 3)],
}
EXCHANGE_PLAN = {
    "attn_bwd1": [("ffn_w_down3", ALL), ("ffn_w_gate3", ALL), ("ffn_w_up3", H1)],
    "dw_down2": [("w_o1", ALL)],
    "ffn_bwd_x2": [("ffn_w_up3", H2), ("w_q1", ALL)],
    "attn_bwd0": [("ffn_w_down2", ALL), ("ffn_w_gate2", ALL), ("ffn_w_up2", H1)],
    "dw_down1": [("w_o0", ALL), ("w_q0", H1)],
    "ffn_bwd_x1": [("ffn_w_up2", H2), ("w_q0", H2), ("w_kv", ALL)],
    "dw_gate1": [("ffn_w_down1", H1)],
    "dw_up1": [("ffn_w_down1", H2)],
    "conv_mid_bwd1": [("ffn_w_gate1", ALL), ("ffn_w_up1", H1)],
    "conv_in_bwd1": [("ffn_w_up1", H2)],
    "dw_down0": [("conv_w_out1", ALL)],
    "ffn_bwd_x0": [("conv_w_in1", ALL)],
    "dw_gate0": [("ffn_w_down0", H1)],
    "dw_up0": [("ffn_w_down0", H2)],
    "conv_out_bwd0": [("ffn_w_gate0", H1)],
    "conv_mid_bwd0": [("ffn_w_gate0", H2), ("ffn_w_up0", H1), ("conv_w_out0", ALL)],
    "dw_conv_in0": [("ffn_w_up0", H2)],
    "conv_in_bwd0": [("conv_w_in0", H1)],
    "tail": [("conv_w_in0", H2)],
}
BIG = {"conv_w_in": "pieces", "conv_w_out": "rows", "w_kv": "rows", "w_q": "rows", "w_o": "rows",
       "ffn_w_gate": "rows", "ffn_w_up": "rows", "ffn_w_down": "rows"}
EXCHANGE_KIND = BIG
TRANSPOSED = ("ffn_w_gate", "ffn_w_up")


def gathered_matrix(name, layer, blocks8):
    if BIG[name] == "rows":
        return blocks8.reshape(NDEV * blocks8.shape[1], blocks8.shape[2])
    return join_columns(blocks8, f"join_{name}{layer}")


def local_step(x, tgt, meta8, w, shards):
    bl, seq, _ = x.shape
    lp = -(-(NMETA + seq) // QB) * QB
    tpb = 4
    tm = lp // tpb
    t = bl * lp
    na = 2
    flat = lambda a: a.reshape(t, D)
    mats = {}

    def riders(carrier):
        return [shards[key] for key in GATHER_PLAN[carrier]]

    def landed(carrier, blocks):
        for key, b8 in zip(GATHER_PLAN[carrier], blocks):
            mats[key] = gathered_matrix(*key, b8)

    h0, got = embed(x, meta8, lp, riders("embed"))
    landed("embed", got)
    h = flat(h0)
    saved = []
    kvs = None
    for l in range(4):
        rec = {"h": h}
        if l < na:
            rec["u"], rec["big"], rec["a"] = conv_in_fwd(h, w["norm_mix"], l, mats["conv_w_in", l], w["conv_b_in"], l, tm)
            name = f"conv_mid_fwd{l}"
            (rec["c"], rec["s"]), _, got = conv_mid_fwd(rec["a"], w["conv_dw"], w["conv_ln_g"], w["conv_ln_b"], l, tm, tpb,
                                                         riders(name))
            landed(name, got)
            mixed, w_out, lw, bias = rec["s"], mats["conv_w_out", l], l, w["conv_b_out"]
        else:
            j = l - na
            if kvs is None:
                kvs = dict(zip(("kn", "kv", "k", "v"), kv_fwd(h, w["kv_norm"], mats["w_kv", 0], w["k_norm"], tm)))
                kvs["h"] = h
                kvs["k3"], kvs["v3"] = kvs["k"].reshape(bl, lp, KVD), kvs["v"].reshape(bl, lp, KVD)
                kvs["kt"], kvs["vt"] = transpose_seq(kvs["k3"], "transpose_k"), transpose_seq(kvs["v3"], "transpose_v")
            rec["u"], rec["q"] = q_fwd(h, w["norm_mix"], l, mats["w_q", j], j, tm)
            name = f"attn_fwd{j}"
            (rec["o"], rec["lse"]), _, got = attn_fwd(rec["q"], kvs["kt"], kvs["v3"], w["q_norm"], w["attn_sinks"], j, bl, lp,
                                                      riders(name) if name in GATHER_PLAN else [])
            if name in GATHER_PLAN:
                landed(name, got)
            mixed, w_out, lw, bias = rec["o"], mats["w_o", j], j, None
        name = f"mixer_ffn_fwd{l}"
        (rec["h1"], rec["u2"], rec["g"], rec["up"], rec["hid"], h), _, got = mixer_ffn_fwd(
            h, mixed, w_out, lw, bias, w["norm_ffn"], l, mats["ffn_w_gate", l], mats["ffn_w_up", l], mats["ffn_w_down", l], tm // 2,
            riders(name) if name in GATHER_PLAN else [])
        if name in GATHER_PLAN:
            landed(name, got)
        saved.append(rec)

    dh3, loss_blk = loss_fwd(h.reshape(bl, lp, D), tgt)
    dh = flat(dh3)

    big, small, arrived = {}, {}, {}
    dks, dvs = [], []

    def ride(kernel_name):
        return [(big[nm], EXCHANGE_KIND[nm.rstrip("0123456789")], ks) for nm, ks in EXCHANGE_PLAN.get(kernel_name, [])]

    def landed_x(kernel_name, arrivals):
        for (nm, _), got in zip(EXCHANGE_PLAN.get(kernel_name, []), arrivals):
            arrived.setdefault(nm, []).append(got)

    def dw(name, grad, x, dy, **kw):
        big[grad], got = mm_tn(x, dy, 2 * tm, name, xch=ride(name), **kw)
        landed_x(name, got)

    for l in reversed(range(4)):
        rec = saved[l]
        dw(f"dw_down{l}", f"ffn_w_down{l}", rec["hid"], dh)
        name = f"ffn_bwd_x{l}"
        outs, got, _ = ffn_bwd_x(
            dh, rec["g"], rec["up"], rec["h1"], w["norm_ffn"], l, mats["ffn_w_down", l], mats["ffn_w_gate", l], mats["ffn_w_up", l],
            mats["w_o", l - na] if l >= na else None, tm // 2, ride(name))
        landed_x(name, got)
        dg, du, dh1, small[f"norm_ffn{l}"] = outs[:4]
        dw(f"dw_gate{l}", f"ffn_w_gate{l}", rec["u2"], dg, transposed=True)
        dw(f"dw_up{l}", f"ffn_w_up{l}", rec["u2"], du, transposed=True)
        if l >= na:
            j = l - na
            dw(f"dw_o{j}", f"w_o{j}", rec["o"], dh1)
            name = f"attn_bwd{j}"
            (dq, dk, dv, small[f"q_norm{j}"], small[f"attn_sinks{j}"]), got, _ = attn_bwd(
                rec["q"], kvs["k3"], kvs["kt"], kvs["vt"], outs[4], rec["o"], rec["lse"], w["q_norm"], w["attn_sinks"], j, bl, lp,
                ride(name))
            landed_x(name, got)
            dks.append(dk)
            dvs.append(dv)
            dw(f"dw_q{j}", f"w_q{j}", rec["u"], dq)
            dh, small[f"norm_mix{l}"] = proj_bwd(dq, mats["w_q", j], rec["h"], w["norm_mix"], l, dh1, tm, f"q_bwd{j}")[0]
            if l == na:
                dkv, small["k_norm"] = kv_bwd_pre(sum_transposed(*dks), sum_transposed(*dvs), kvs["kv"], w["k_norm"], tm)
                dw("dw_kv", "w_kv", kvs["kn"], dkv)
                dh, small["kv_norm"] = proj_bwd(dkv, mats["w_kv", 0], kvs["h"], w["kv_norm"], None, dh, tm, "kv_bwd")[0]
        else:
            name = f"conv_out_bwd{l}"
            (dc, small[f"conv_ln_g{l}"], small[f"conv_ln_b{l}"], small[f"conv_b_out{l}"]), got, _ = conv_out_bwd(
                dh1, rec["c"], w["conv_ln_g"], w["conv_ln_b"], mats["conv_w_out", l], l, tm, ride(name))
            landed_x(name, got)
            dw(f"dw_conv_out{l}", f"conv_w_out{l}", rec["s"], dh1)
            name = f"conv_mid_bwd{l}"
            (da, small[f"conv_b_in{l}"], small[f"conv_dw{l}"]), got, _ = conv_mid_bwd(
                dc, rec["a"], rec["big"], w["conv_dw"], l, tm, tpb, ride(name))
            landed_x(name, got)
            dw(f"dw_conv_in{l}", f"conv_w_in{l}", rec["u"], da, split=True)
            name = f"conv_in_bwd{l}"
            (dh, small[f"norm_mix{l}"]), got, _ = proj_bwd(da, mats["conv_w_in", l], rec["h"], w["norm_mix"], l, dh1, tm, name,
                                                           ride(name))
            landed_x(name, got)
    grad_x, small["meta_tokens"] = input_grads(dh.reshape(bl, lp, D), seq)
    return loss_blk, grad_x, big, arrived, small


def all_gather_blocks(blocks):
    n = len(blocks)

    def body(*refs):
        srcs, outs, sems = refs[:n], refs[n:2 * n], refs[2 * n:]
        _gat_start(srcs, outs, sems)
        _gat_forward(srcs, outs, sems)
        _gat_wait(srcs, outs, sems)

    any_spec = pl.BlockSpec(memory_space=pl.ANY)
    return pl.pallas_call(
        body, name="all_gather_blocks", out_shape=[_sds((NDEV,) + tuple(a.shape), a.dtype) for a in blocks],
        in_specs=[any_spec] * n, out_specs=[any_spec] * n, scratch_shapes=_xch_scratch(n),
    )(*blocks)


def cast_bf16(ws):
    n = len(ws)
    counts = [1 if x.ndim == 2 else x.shape[0] for x in ws]

    def body(*refs):
        outs = iter(refs[n:])
        for a in range(n):
            for l in range(counts[a]):
                next(outs)[...] = (refs[a][...] if ws[a].ndim == 2 else refs[a][l]).astype(BF)

    flat = pl.pallas_call(
        body, name="cast_bf16", out_shape=[_sds(x.shape[-2:], BF) for x, k in zip(ws, counts) for _ in range(k)],
        compiler_params=pltpu.CompilerParams(vmem_limit_bytes=VMEM_LIMIT),
    )(*ws)
    it = iter(flat)
    return [[next(it) for _ in range(k)] for k in counts]


def join_columns(w8, name):
    _, kk, n8 = w8.shape

    def body(x_ref, o_ref):
        o_ref[...] = jnp.concatenate([x_ref[p] for p in range(NDEV)], axis=1)

    return pl.pallas_call(body, name=name, out_shape=_sds((kk, NDEV * n8), w8.dtype),
                          compiler_params=pltpu.CompilerParams(vmem_limit_bytes=VMEM_LIMIT))(w8)


def _adamw_math(w, m, v, g):
    m2 = B1 * m + (1.0 - B1) * g
    v2 = B2 * v + (1.0 - B2) * (g * g)
    mh = m2 / (1.0 - B1 ** STEP)
    vh = v2 / (1.0 - B2 ** STEP)
    return -LR * (mh / (jnp.sqrt(vh) + AEPS) + WD * w), m2, v2


def adamw_big(w, m, v, parts, name, xch=(), gat=()):
    lyr, r, c = w.shape
    by_cols = c >= 512
    blk = (lyr, r, 256) if by_cols else (lyr, 256 if r % 256 == 0 else r, c)
    imap = (lambda i: (0, 0, i)) if by_cols else (lambda i: (0, i, 0))
    counts = [len(per_layer) for per_layer in parts]

    def body(w_ref, m_ref, v_ref, *rest):
        p_refs, (g_ref, d_ref, m2_ref, v2_ref) = iter(rest[:sum(counts)]), rest[sum(counts):]
        for l in range(lyr):
            g = None
            for _ in range(counts[l]):
                ref = next(p_refs)
                for q in range(ref.shape[0]):
                    g = ref[q].astype(F32) if g is None else g + ref[q].astype(F32)
            g_ref[l] = g
            d_ref[l], m2_ref[l], v2_ref[l] = _adamw_math(w_ref[l], m_ref[l], v_ref[l], g)

    spec = pl.BlockSpec(blk, imap)
    flat = [a for per_layer in parts for a in per_layer]
    pspecs = [pl.BlockSpec((a.shape[0],) + blk[1:], imap) for a in flat]
    return _call(body, name, ((c // 256) if by_cols else (r // blk[1]),), [spec, spec, spec] + pspecs,
                 [spec] * 4, [_sds((lyr, r, c), F32)] * 4, (w, m, v, *flat), xch=xch, gat=gat)


SMALL_ROWS = 104
REPLICATED = {"norm_mix": (0, 4, D), "norm_ffn": (4, 4, D), "kv_norm": (8, 1, D), "k_norm": (9, 1, HD), "q_norm": (10, 2, HD),
              "attn_sinks": (12, 2, NH)}
LOSS_ROW = 14
SHARDED = {"meta_tokens": (16, NMETA), "conv_b_in": (32, 4), "conv_dw": (36, 2 * CW), "conv_ln_g": (98, 2), "conv_ln_b": (100, 2),
           "conv_b_out": (102, 2)}


def pack_small(gs, loss_blk):
    order = ([f"norm_mix{l}" for l in range(4)] + [f"norm_ffn{l}" for l in range(4)] + ["kv_norm", "k_norm", "q_norm0", "q_norm1",
             "attn_sinks0", "attn_sinks1", "meta_tokens", "conv_b_in0", "conv_b_in1", "conv_dw0", "conv_dw1", "conv_ln_g0",
             "conv_ln_g1", "conv_ln_b0", "conv_ln_b1", "conv_b_out0", "conv_b_out1"])

    def body(*refs):
        r = dict(zip(order, refs))
        loss_ref, o_ref = refs[len(order)], refs[len(order) + 1]
        o_ref[...] = jnp.zeros_like(o_ref)
        for l in range(4):
            o_ref[l:l + 1, :] = r[f"norm_mix{l}"][...]
            o_ref[4 + l:5 + l, :] = r[f"norm_ffn{l}"][...]
        o_ref[8:9, :] = r["kv_norm"][...]
        o_ref[9:10, 0:HD] = r["k_norm"][...]
        for j in range(2):
            o_ref[10 + j:11 + j, 0:HD] = r[f"q_norm{j}"][...]
            o_ref[12 + j:13 + j, 0:NH] = r[f"attn_sinks{j}"][...]
            o_ref[32 + 2 * j:33 + 2 * j, :] = r[f"conv_b_in{j}"][:, 0:D]
            o_ref[33 + 2 * j:34 + 2 * j, :] = r[f"conv_b_in{j}"][:, D:2 * D]
            o_ref[36 + CW * j:36 + CW * (j + 1), :] = r[f"conv_dw{j}"][0:CW, :]
            o_ref[98 + j:99 + j, :] = r[f"conv_ln_g{j}"][...]
            o_ref[100 + j:101 + j, :] = r[f"conv_ln_b{j}"][...]
            o_ref[102 + j:103 + j, :] = r[f"conv_b_out{j}"][...]
        o_ref[LOSS_ROW:LOSS_ROW + 1, 0:1] = loss_ref[0:1, 0:1]
        o_ref[16:16 + NMETA, :] = r["meta_tokens"][...]

    return pl.pallas_call(body, name="pack_small", out_shape=_sds((SMALL_ROWS, D), F32))(*[gs[k] for k in order], loss_blk)


def adamw_small(g8, wts, mom, var):
    names = list(REPLICATED) + list(SHARDED)
    shape2 = {"kv_norm": (1, D), "k_norm": (1, HD)}
    ins = [a[k].reshape(shape2.get(k, a[k].shape)) for a in (wts, mom, var) for k in names]
    n = len(names)

    def body(*refs):
        g8_ref, w_refs, m_refs, v_refs = refs[0], refs[1:1 + n], refs[1 + n:1 + 2 * n], refs[1 + 2 * n:1 + 3 * n]
        loss_ref, outs, red_ref = refs[1 + 3 * n], refs[2 + 3 * n:-1], refs[-1]
        me = _my_index()
        acc = g8_ref[0]
        for q in range(1, NDEV):
            acc = acc + g8_ref[q]
        red_ref[...] = acc
        loss_ref[...] = red_ref[LOSS_ROW:LOSS_ROW + 1, 0:1]

        def mine(rows, width):
            acc = jnp.zeros((rows.stop - rows.start, width), F32)
            for p_ in range(NDEV):
                acc = acc + jnp.where(me == p_, red_ref[rows, p_ * width:(p_ + 1) * width], 0.0)
            return acc

        for i, k in enumerate(names):
            if k in REPLICATED:
                r0, nr, width = REPLICATED[k]
                g = red_ref[r0:r0 + nr, 0:width]
            elif k == "conv_b_in":
                half = D // (2 * D // NDEV)
                acc = jnp.zeros((2, 2 * D // NDEV), F32)
                for p_ in range(NDEV):
                    c0 = (p_ % half) * (2 * D // NDEV)
                    part = jnp.concatenate([red_ref[32 + 2 * j + p_ // half:33 + 2 * j + p_ // half, c0:c0 + 2 * D // NDEV]
                                            for j in range(2)], axis=0)
                    acc = acc + jnp.where(me == p_, part, 0.0)
                g = acc
            else:
                r0, nr = SHARDED[k]
                g = mine(slice(r0, r0 + nr), D // NDEV)
            w_, m_, v_ = w_refs[i], m_refs[i], v_refs[i]
            g_out, d_out, m_out, v_out = outs[4 * i:4 * i + 4]
            if k == "conv_dw":
                for j in range(2):
                    gj = g[CW * j:CW * (j + 1)]
                    g_out[j] = gj
                    d_out[j], m_out[j], v_out[j] = _adamw_math(w_[j], m_[j], v_[j], gj)
            else:
                g_out[...] = g
                d_out[...], m_out[...], v_out[...] = _adamw_math(w_[...], m_[...], v_[...], g)

    out_shape = [_sds((1, 1), F32)] + [_sds(ins[i].shape, F32) for i in range(n) for _ in range(4)]
    res = pl.pallas_call(body, name="adamw_small", out_shape=out_shape, scratch_shapes=[pltpu.VMEM((SMALL_ROWS, D), F32)])(g8, *ins)
    out = {k: tuple(o.reshape(wts[k].shape) for o in res[1 + 4 * i:5 + 4 * i]) for i, k in enumerate(names)}
    return res[0], out


NAMES = ["meta_tokens", "norm_mix", "norm_ffn", "conv_w_in", "conv_b_in", "conv_dw", "conv_ln_g", "conv_ln_b", "conv_w_out",
         "conv_b_out", "kv_norm", "w_kv", "k_norm", "w_q", "q_norm", "attn_sinks", "w_o", "ffn_w_gate", "ffn_w_up", "ffn_w_down"]


def kernel(x, meta_tokens, norm_mix, norm_ffn, conv_w_in, conv_b_in, conv_dw, conv_ln_g, conv_ln_b, conv_w_out, conv_b_out, kv_norm, w_kv, k_norm, w_q, q_norm, attn_sinks, w_o, ffn_w_gate, ffn_w_up, ffn_w_down, loss_target, m_meta_tokens, m_norm_mix, m_norm_ffn, m_conv_w_in, m_conv_b_in, m_conv_dw, m_conv_ln_g, m_conv_ln_b, m_conv_w_out, m_conv_b_out, m_kv_norm, m_w_kv, m_k_norm, m_w_q, m_q_norm, m_attn_sinks, m_w_o, m_ffn_w_gate, m_ffn_w_up, m_ffn_w_down, v_meta_tokens, v_norm_mix, v_norm_ffn, v_conv_w_in, v_conv_b_in, v_conv_dw, v_conv_ln_g, v_conv_ln_b, v_conv_w_out, v_conv_b_out, v_kv_norm, v_w_kv, v_k_norm, v_w_q, v_q_norm, v_attn_sinks, v_w_o, v_ffn_w_gate, v_ffn_w_up, v_ffn_w_down):
    wts = dict(zip(NAMES, (meta_tokens, norm_mix, norm_ffn, conv_w_in, conv_b_in, conv_dw, conv_ln_g, conv_ln_b, conv_w_out,
                           conv_b_out, kv_norm, w_kv, k_norm, w_q, q_norm, attn_sinks, w_o, ffn_w_gate, ffn_w_up, ffn_w_down)))
    mom = dict(zip(NAMES, (m_meta_tokens, m_norm_mix, m_norm_ffn, m_conv_w_in, m_conv_b_in, m_conv_dw, m_conv_ln_g, m_conv_ln_b,
                           m_conv_w_out, m_conv_b_out, m_kv_norm, m_w_kv, m_k_norm, m_w_q, m_q_norm, m_attn_sinks, m_w_o,
                           m_ffn_w_gate, m_ffn_w_up, m_ffn_w_down)))
    var = dict(zip(NAMES, (v_meta_tokens, v_norm_mix, v_norm_ffn, v_conv_w_in, v_conv_b_in, v_conv_dw, v_conv_ln_g, v_conv_ln_b,
                           v_conv_w_out, v_conv_b_out, v_kv_norm, v_w_kv, v_k_norm, v_w_q, v_q_norm, v_attn_sinks, v_w_o,
                           v_ffn_w_gate, v_ffn_w_up, v_ffn_w_down)))
    for k in TRANSPOSED:
        wts[k], mom[k], var[k] = (jnp.swapaxes(a, 1, 2) for a in (wts[k], mom[k], var[k]))

    big_names = list(BIG)
    layers = cast_bf16([wts[k] for k in big_names])
    shards = {(k, l): blk for k, per_layer in zip(big_names, layers) for l, blk in enumerate(per_layer)}
    vec_names = ["meta_tokens", "conv_b_in", "conv_dw", "conv_ln_g", "conv_ln_b", "conv_b_out"]
    full = dict(zip(vec_names, all_gather_blocks([wts[k] for k in vec_names])))
    join_vec = lambda a: jnp.moveaxis(a, 0, -2).reshape(a.shape[1:-1] + (NDEV * a.shape[-1],))
    w = {}
    w["conv_b_in"] = join_vec(full["conv_b_in"]).reshape(2, 1, 2 * D)
    w["conv_dw"] = join_vec(full["conv_dw"])
    for k in ("conv_ln_g", "conv_ln_b", "conv_b_out"):
        w[k] = join_vec(full[k]).reshape(2, 1, D)
    w["norm_mix"] = norm_mix.reshape(4, 1, D)
    w["norm_ffn"] = norm_ffn.reshape(4, 1, D)
    w["kv_norm"] = kv_norm.reshape(1, D)
    w["k_norm"] = k_norm.reshape(1, HD)
    w["q_norm"] = q_norm.reshape(2, 1, HD)
    w["attn_sinks"] = attn_sinks.reshape(2, 1, NH)

    loss_blk, grad_x, gbig, arrived, gs = local_step(x, loss_target, full["meta_tokens"], w, shards)

    packed = pack_small(gs, loss_blk)

    grads, delta, new_m, new_v = {}, {}, {}, {}
    tail = EXCHANGE_PLAN["tail"]
    waiting = {nm.rstrip("0123456789") for nm, _ in tail}
    order = sorted([k for k in big_names if k not in waiting], key=lambda k: -wts[k].size) + [k for k in big_names if k in waiting]
    small8 = None
    for pos, k in enumerate(order):
        flat2 = wts[k].ndim == 2
        as3 = (lambda a: a[None]) if flat2 else (lambda a: a)
        riders = tail if pos == 0 else []
        gat = [packed] if pos == 1 else []
        parts = [arrived[k]] if flat2 else [arrived[f"{k}{i}"] for i in range(wts[k].shape[0])]
        outs, got_x, got_g = adamw_big(as3(wts[k]), as3(mom[k]), as3(var[k]), parts, "adamw_" + k,
                                       xch=[(gbig[nm], EXCHANGE_KIND[nm.rstrip("0123456789")], ks) for nm, ks in riders], gat=gat)
        for (nm, _), got in zip(riders, got_x):
            arrived[nm].append(got)
        if gat:
            small8 = got_g[0]
        grads[k], delta[k], new_m[k], new_v[k] = [o[0] if flat2 else (jnp.swapaxes(o, 1, 2) if k in TRANSPOSED else o) for o in outs]
    loss, small = adamw_small(small8, wts, mom, var)
    for k, (g_, d_, m_, v_) in small.items():
        grads[k], delta[k], new_m[k], new_v[k] = g_, d_, m_, v_
    return (loss.reshape(()), grad_x, *[grads[k] for k in NAMES], *[delta[k] for k in NAMES], *[new_m[k] for k in NAMES],
            *[new_v[k] for k in NAMES])
```

```python
import jax
import jax.numpy as jnp
from jax import lax
from jax.experimental import pallas as pl
from jax.experimental.pallas import tpu as pltpu

F32 = jnp.float32
BF = jnp.bfloat16

D = 1024
DFF = 2816
NH = 16
NKV = 4
HD = 64
KVD = NKV * HD
NMETA = 16
CW = 31
HALO = 32
CHUNK = 32
QB = 128
EPS = 1e-6
NEG = -1e30
NDEV = 8
SCALE = HD ** -0.5

LR, B1, B2, AEPS, WD, STEP = 0.001, 0.9, 0.999, 1e-08, 0.01, 10

VMEM_LIMIT = 56 * 2 ** 20
MESH = pl.DeviceIdType.MESH


def _cp(n):
    return pltpu.CompilerParams(dimension_semantics=("arbitrary",) * n, vmem_limit_bytes=VMEM_LIMIT)


def _row(tm, c):
    return pl.BlockSpec((tm, c), lambda i: (i, 0))


def _res(shape):
    return pl.BlockSpec(shape, lambda i: (0,) * len(shape), pipeline_mode=pl.Buffered(1))


def _lay(l, shape):
    return pl.BlockSpec((None,) + tuple(shape), lambda i: (l,) + (0,) * len(shape), pipeline_mode=pl.Buffered(1))


def _acc(shape):
    return pl.BlockSpec(shape, lambda i: (0,) * len(shape))


def _sds(shape, dt):
    return jax.ShapeDtypeStruct(tuple(shape), dt)


def _dot(a, b):
    return jnp.dot(a.astype(BF), b.astype(BF), preferred_element_type=F32)


def _dot_nt(a, b):
    return lax.dot_general(a.astype(BF), b.astype(BF), (((1,), (1,)), ((), ())), preferred_element_type=F32)


def _dot_tn(a, b):
    return lax.dot_general(a.astype(BF), b.astype(BF), (((0,), (0,)), ((), ())), preferred_element_type=F32)


def _rstd(x):
    return lax.rsqrt(jnp.mean(x * x, axis=-1, keepdims=True) + EPS)


def _rms_bwd(x, g, dy):
    r = _rstd(x)
    z = dy * g
    dx = r * z - x * (r * r * r * jnp.mean(z * x, axis=-1, keepdims=True))
    return dx, jnp.sum(dy * x * r, axis=0, keepdims=True)


def _sig(x):
    return jax.nn.sigmoid(x)


def _fold8(x):
    out = x[0:8]
    for k in range(1, x.shape[0] // 8):
        out = out + x[8 * k:8 * k + 8]
    return out


def _shifted(win):
    return [win] + [pltpu.roll(win, 2 * CHUNK - rho, 0) for rho in range(1, 8)]


def _tap(phases, o):
    return phases[o % 8][8 * (o // 8):8 * (o // 8) + CHUNK]


def _spread_taps(dw_ref, taps):
    @pl.when(pl.program_id(0) == 0)
    def _():
        for j in range(CW):
            taps[j] = jnp.broadcast_to(dw_ref[j:j + 1, :], taps.shape[1:])


def _tap_weight(taps, j):
    return jnp.concatenate([taps[j]] * (CHUNK // 8), axis=0)


def _init(ref, first):
    @pl.when(first)
    def _():
        ref[...] = jnp.zeros_like(ref)


def _my_index():
    return 4 * lax.axis_index("x") + 2 * lax.axis_index("y") + lax.axis_index("c")


def _coords(idx):
    return (idx // 4, (idx // 2) % 2, idx % 2)


ALL = tuple(range(NDEV))
H1, H2 = (0, 1, 2, 4, 6), (3, 5, 7)


def _xch_shapes(xch):
    return [_sds((len(ks),) + ((a.shape[0] // NDEV, a.shape[1]) if k == "rows" else tuple(a.shape[1:])), a.dtype) for a, k, ks in xch]


def _xch_scratch(n):
    return [pltpu.SemaphoreType.DMA((n, NDEV)), pltpu.SemaphoreType.DMA((n, NDEV)), pltpu.SemaphoreType.DMA((n,))]


def _xch_copies(meta, srcs, outs, sems, arrivals):
    send_sems, recv_sems, local_sems = sems
    me = _my_index()

    def piece(a, p):
        if meta[a][0] == "rows":
            r = srcs[a].shape[0] // NDEV
            return srcs[a].at[pl.ds(p * r, r), :]
        return srcs[a].at[p]

    def remote(a, i, k, src):
        return pltpu.make_async_remote_copy(
            src_ref=src, dst_ref=outs[a].at[i], send_sem=send_sems.at[a, k], recv_sem=recv_sems.at[a, k],
            device_id=_coords(me ^ k), device_id_type=MESH)

    local, sends, recvs = [], [], []
    for a, (_, ks) in enumerate(meta):
        for i, k in enumerate(ks):
            if k == 0:
                local.append(pltpu.make_async_copy(piece(a, me), outs[a].at[i], local_sems.at[a]))
            else:
                sends.append(remote(a, i, k, piece(a, me ^ k)))
                if arrivals:
                    recvs.append(remote(a, i, k, piece(a, me)))
    return local, sends, recvs


def _xch_start(meta, srcs, outs, sems):
    local, sends, _ = _xch_copies(meta, srcs, outs, sems, False)
    for cp in local + sends:
        cp.start()


def _xch_wait(meta, srcs, outs, sems):
    local, sends, recvs = _xch_copies(meta, srcs, outs, sems, True)
    for cp in recvs:
        cp.wait_recv()
    for cp in sends:
        cp.wait_send()
    for cp in local:
        cp.wait()


def _gat_copies(srcs, outs, sems):
    send_sems, recv_sems, local_sems = sems
    x, y, c = lax.axis_index("x"), lax.axis_index("y"), lax.axis_index("c")
    me, sibling = (x, y, c), (x, y, 1 - c)
    chips = [(1 - x, y), (x, 1 - y), (1 - x, 1 - y)]

    def copy(a, k, owner, to, from_block=False):
        slot = outs[a].at[4 * owner[0] + 2 * owner[1] + owner[2]]
        return pltpu.make_async_remote_copy(
            src_ref=srcs[a] if from_block else slot, dst_ref=slot, send_sem=send_sems.at[a, k], recv_sem=recv_sems.at[a, k],
            device_id=to, device_id_type=MESH)

    n = len(srcs)
    local = lambda: [pltpu.make_async_copy(srcs[a], outs[a].at[4 * x + 2 * y + c], local_sems.at[a]) for a in range(n)]
    first = lambda: [cp for a in range(n) for cp in
                     [copy(a, 0, me, sibling, True)] + [copy(a, 1 + j, me, (*chip, c), True) for j, chip in enumerate(chips)]]
    landed = lambda: [copy(a, 1 + j, (*chip, c), me) for a in range(n) for j, chip in enumerate(chips)]
    passed = lambda: [copy(a, 4 + j, (*chip, c), sibling) for a in range(n) for j, chip in enumerate(chips)]
    final = lambda: [cp for a in range(n) for cp in
                     [copy(a, 0, sibling, me)] + [copy(a, 4 + j, (*chip, 1 - c), me) for j, chip in enumerate(chips)]]
    return local, first, landed, passed, final


def _gat_start(srcs, outs, sems):
    local, first, _, _, _ = _gat_copies(srcs, outs, sems)
    for cp in local() + first():
        cp.start()


def _gat_forward(srcs, outs, sems):
    _, _, landed, passed, _ = _gat_copies(srcs, outs, sems)
    for got, on in zip(landed(), passed()):
        got.wait_recv()
        on.start()


def _gat_wait(srcs, outs, sems):
    local, first, _, passed, final = _gat_copies(srcs, outs, sems)
    for cp in final():
        cp.wait_recv()
    for cp in first() + passed():
        cp.wait_send()
    for cp in local():
        cp.wait()


def _call(body, name, grid, in_specs, out_specs, out_shape, args, scratch=(), xch=(), gat=()):
    n_in, n_out, n_x, n_g, n_s = len(in_specs), len(out_specs), len(xch), len(gat), len(scratch)
    kinds = [(k, ks) for _, k, ks in xch]
    total = 1
    for g in grid:
        total *= g

    def wrapped(*refs):
        ins, refs = refs[:n_in], refs[n_in:]
        x_src, refs = refs[:n_x], refs[n_x:]
        g_src, refs = refs[:n_g], refs[n_g:]
        outs, refs = refs[:n_out], refs[n_out:]
        x_out, refs = refs[:n_x], refs[n_x:]
        g_out, refs = refs[:n_g], refs[n_g:]
        own, refs = refs[:n_s], refs[n_s:]
        x_sems, g_sems = (refs[:3], refs[3:]) if n_x else ((), refs)
        step = pl.program_id(0)
        for d in range(1, len(grid)):
            step = step * grid[d] + pl.program_id(d)
        if n_x or n_g:
            @pl.when(step == 0)
            def _():
                if n_x:
                    _xch_start(kinds, x_src, x_out, x_sems)
                if n_g:
                    _gat_start(g_src, g_out, g_sems)

        body(*ins, *outs, *own)
        if n_g:
            @pl.when(step == max(total - 2, 0))
            def _():
                _gat_forward(g_src, g_out, g_sems)

        if n_x or n_g:
            @pl.when(step == total - 1)
            def _():
                if n_x:
                    _xch_wait(kinds, x_src, x_out, x_sems)
                if n_g:
                    _gat_wait(g_src, g_out, g_sems)

    any_spec = pl.BlockSpec(memory_space=pl.ANY)
    g_shapes = [_sds((NDEV,) + tuple(a.shape), a.dtype) for a in gat]
    res = pl.pallas_call(
        wrapped, name=name, grid=grid, in_specs=list(in_specs) + [any_spec] * (n_x + n_g),
        out_specs=list(out_specs) + [any_spec] * (n_x + n_g), out_shape=list(out_shape) + _xch_shapes(xch) + g_shapes,
        scratch_shapes=list(scratch) + (_xch_scratch(n_x) if n_x else []) + (_xch_scratch(n_g) if n_g else []),
        compiler_params=_cp(len(grid)),
    )(*args, *[a for a, _, _ in xch], *gat)
    return res[:n_out], res[n_out:n_out + n_x], res[n_out + n_x:]


def embed(x, meta8, lp, gat):
    bl, seq, _ = x.shape
    c8 = D // NDEV
    cb = 2 * c8

    def body(x_ref, m_ref, h_ref):
        h_ref[0:NMETA, :] = jnp.concatenate([m_ref[0], m_ref[1]], axis=1)
        h_ref[NMETA:NMETA + seq, :] = x_ref[...]
        h_ref[NMETA + seq:, :] = jnp.zeros((lp - NMETA - seq, cb), F32)

    (h0,), _, got = _call(
        body, "embed", (bl, D // cb),
        [pl.BlockSpec((None, seq, cb), lambda b, c: (b, 0, c)), pl.BlockSpec((2, NMETA, c8), lambda b, c: (c, 0, 0))],
        [pl.BlockSpec((None, lp, cb), lambda b, c: (b, 0, c))], [_sds((bl, lp, D), F32)], (x, meta8), gat=gat)
    return h0, got


def conv_in_fwd(h, nm, l, w_in, b_in, i, tm):
    t = h.shape[0]

    def body(h_ref, g_ref, w_ref, b_ref, u_ref, big_ref, a_ref):
        x = h_ref[...]
        ub = (x * _rstd(x) * g_ref[...]).astype(BF)
        u_ref[...] = ub
        big = jnp.dot(ub, w_ref[...], preferred_element_type=F32) + b_ref[...]
        big_ref[...] = big.astype(BF)
        a_ref[...] = big[:, :D] * _sig(big[:, D:])

    return pl.pallas_call(
        body, name=f"conv_in_fwd{i}", grid=(t // tm,),
        in_specs=[_row(tm, D), _lay(l, (1, D)), _res((D, 2 * D)), _lay(i, (1, 2 * D))],
        out_specs=[_row(tm, D), _row(tm, 2 * D), _row(tm, D)],
        out_shape=[_sds((t, D), BF), _sds((t, 2 * D), BF), _sds((t, D), F32)],
        compiler_params=_cp(1),
    )(h, nm, w_in, b_in)


def _prev_halo(tm):
    return pl.BlockSpec((HALO, D), lambda i: (jnp.maximum(i * (tm // HALO) - 1, 0), 0))


def _next_halo(tm, t):
    return pl.BlockSpec((HALO, D), lambda i: (jnp.minimum((i + 1) * (tm // HALO), t // HALO - 1), 0))


def conv_mid_fwd(a, dw, ln_g, ln_b, i, tm, tpb, gat):
    t = a.shape[0]

    def body(a_ref, halo_ref, dw_ref, g_ref, b_ref, c_ref, s_ref, ext, taps):
        _spread_taps(dw_ref, taps)
        first = pl.program_id(0) % tpb == 0
        ext[0:HALO] = jnp.where(first, 0.0, halo_ref[...])
        ext[HALO:] = a_ref[...]

        def chunk(k, carry):
            r0 = pl.multiple_of(k * CHUNK, CHUNK)
            win = _shifted(ext[pl.ds(r0, 2 * CHUNK), :])
            c = jnp.zeros((CHUNK, D), F32)
            for j in range(CW):
                c = c + _tap_weight(taps, j) * _tap(win, j + 2)
            c_ref[pl.ds(r0, CHUNK), :] = c
            mu = jnp.mean(c, axis=-1, keepdims=True)
            xc = c - mu
            n = xc * lax.rsqrt(jnp.mean(xc * xc, axis=-1, keepdims=True) + EPS) * g_ref[...] + b_ref[...]
            s_ref[pl.ds(r0, CHUNK), :] = (n * _sig(n)).astype(BF)
            return carry

        lax.fori_loop(0, tm // CHUNK, chunk, 0)

    return _call(
        body, f"conv_mid_fwd{i}", (t // tm,),
        [_row(tm, D), _prev_halo(tm), _lay(i, (CW, D)), _lay(i, (1, D)), _lay(i, (1, D))],
        [_row(tm, D), _row(tm, D)], [_sds((t, D), F32), _sds((t, D), BF)], (a, a, dw, ln_g, ln_b),
        scratch=[pltpu.VMEM((tm + HALO, D), F32), pltpu.VMEM((CW, 8, D), F32)], gat=gat)


def mixer_ffn_fwd(h, s, w_out, lw, bias, nf, l, wg, wu, wd, tm, gat):
    t = h.shape[0]

    def body(*refs):
        if bias is None:
            h_ref, s_ref, w_ref, nf_ref, wg_ref, wu_ref, wd_ref, h1_ref, u_ref, g_ref, up_ref, hid_ref, h2_ref = refs
            y = 0.0
        else:
            h_ref, s_ref, w_ref, b_ref, nf_ref, wg_ref, wu_ref, wd_ref, h1_ref, u_ref, g_ref, up_ref, hid_ref, h2_ref = refs
            y = b_ref[...]
        h1 = h_ref[...] + (jnp.dot(s_ref[...], w_ref[...], preferred_element_type=F32) + y)
        h1_ref[...] = h1
        ub = (h1 * _rstd(h1) * nf_ref[...]).astype(BF)
        u_ref[...] = ub
        g = _dot_nt(ub, wg_ref[...])
        up = _dot_nt(ub, wu_ref[...])
        g_ref[...] = g.astype(BF)
        up_ref[...] = up.astype(BF)
        hid = (g * _sig(g) * up).astype(BF)
        hid_ref[...] = hid
        h2_ref[...] = h1 + jnp.dot(hid, wd_ref[...], preferred_element_type=F32)

    ins = [h, s, w_out] + ([] if bias is None else [bias]) + [nf, wg, wu, wd]
    specs = ([_row(tm, D), _row(tm, D), _res((D, D))] + ([] if bias is None else [_lay(lw, (1, D))])
             + [_lay(l, (1, D)), _res((DFF, D)), _res((DFF, D)), _res((DFF, D))])
    return _call(
        body, f"mixer_ffn_fwd{l}", (t // tm,), specs,
        [_row(tm, D), _row(tm, D), _row(tm, DFF), _row(tm, DFF), _row(tm, DFF), _row(tm, D)],
        [_sds((t, D), F32), _sds((t, D), BF), _sds((t, DFF), BF), _sds((t, DFF), BF), _sds((t, DFF), BF), _sds((t, D), F32)],
        ins, gat=gat)


def _seg_rms(x, g, nseg):
    outs = []
    for s in range(nseg):
        xs = x[:, HD * s:HD * s + HD]
        outs.append(xs * _rstd(xs) * g)
    return jnp.concatenate(outs, axis=1)


def kv_fwd(h, kvn, w_kv, kng, tm):
    t = h.shape[0]

    def body(h_ref, g_ref, w_ref, kg_ref, kn_ref, kv_ref, k_ref, v_ref):
        x = h_ref[...]
        kn = (x * _rstd(x) * g_ref[...]).astype(BF)
        kn_ref[...] = kn
        kv = jnp.dot(kn, w_ref[...], preferred_element_type=F32)
        kv_ref[...] = kv
        k_ref[...] = _seg_rms(kv[:, :KVD], kg_ref[...], NKV).astype(BF)
        v_ref[...] = kv[:, KVD:].astype(BF)

    return pl.pallas_call(
        body, name="kv_fwd", grid=(t // tm,),
        in_specs=[_row(tm, D), _res((1, D)), _res((D, 2 * KVD)), _res((1, HD))],
        out_specs=[_row(tm, D), _row(tm, 2 * KVD), _row(tm, KVD), _row(tm, KVD)],
        out_shape=[_sds((t, D), BF), _sds((t, 2 * KVD), F32), _sds((t, KVD), BF), _sds((t, KVD), BF)],
        compiler_params=_cp(1),
    )(h, kvn, w_kv, kng)


def q_fwd(h, nm, l, w_q, j, tm):
    t = h.shape[0]

    def body(h_ref, g_ref, w_ref, u_ref, q_ref):
        x = h_ref[...]
        ub = (x * _rstd(x) * g_ref[...]).astype(BF)
        u_ref[...] = ub
        q_ref[...] = jnp.dot(ub, w_ref[...], preferred_element_type=F32)

    return pl.pallas_call(
        body, name=f"q_fwd{j}", grid=(t // tm,),
        in_specs=[_row(tm, D), _lay(l, (1, D)), _res((D, D))],
        out_specs=[_row(tm, D), _row(tm, D)], out_shape=[_sds((t, D), BF), _sds((t, D), F32)],
        compiler_params=_cp(1),
    )(h, nm, w_q)


RQ = NH // NKV


NKEYS = 2 * QB + NMETA


def _attn_mask(n, start):
    shape = (RQ * QB, NKEYS)
    qpos = n * QB + (lax.broadcasted_iota(jnp.int32, shape, 0) & (QB - 1))
    col = lax.broadcasted_iota(jnp.int32, shape, 1)
    in_band = col < 2 * QB
    kpos = jnp.where(in_band, start + col, col - 2 * QB)
    return (kpos <= qpos) & ((col >= 2 * QB) | ((qpos - kpos < QB) & (kpos >= NMETA)))


def _keys(ref, band, gs):
    return jnp.concatenate([ref[band, gs], ref[0:NMETA, gs]], axis=0)


def _keys_t(ref, band, gs):
    return jnp.concatenate([ref[gs, band], ref[gs, 0:NMETA]], axis=1)


def transpose_seq(a, name):
    bl, r, c = a.shape

    def body(a_ref, o_ref):
        o_ref[...] = a_ref[...].T

    return pl.pallas_call(
        body, name=name, grid=(bl,), in_specs=[pl.BlockSpec((None, r, c), lambda b: (b, 0, 0))],
        out_specs=pl.BlockSpec((None, c, r), lambda b: (b, 0, 0)), out_shape=_sds((bl, c, r), a.dtype), compiler_params=_cp(1),
    )(a)


def sum_transposed(a0, a1):
    bl, c, r = a0.shape

    def body(a0_ref, a1_ref, o_ref):
        o_ref[...] = (a0_ref[...] + a1_ref[...]).T

    spec = pl.BlockSpec((None, c, r), lambda b: (b, 0, 0))
    return pl.pallas_call(
        body, name="sum_transposed", grid=(bl,), in_specs=[spec, spec],
        out_specs=pl.BlockSpec((r, c), lambda b: (b, 0)), out_shape=_sds((bl * r, c), a0.dtype), compiler_params=_cp(1),
    )(a0, a1)


def _stack_heads(ref, g, fn):
    return jnp.concatenate([fn(ref[:, HD * (g * RQ + r):HD * (g * RQ + r) + HD]) for r in range(RQ)], axis=0)


def _stack_cols(ref, g):
    return jnp.concatenate([ref[:, g * RQ + r:g * RQ + r + 1] for r in range(RQ)], axis=0)


def _stack_sinks(sk_ref, g):
    return jnp.concatenate([jnp.broadcast_to(sk_ref[:, g * RQ + r:g * RQ + r + 1], (QB, 1)) for r in range(RQ)], axis=0)


def attn_fwd(q, kt, v, qg, sinks, j, bl, lp, gat):
    t = q.shape[0]
    nb = lp // QB

    def body(q_ref, kt_ref, v_ref, qg_ref, sk_ref, o_ref, lse_ref):
        n = pl.program_id(0)
        start = pl.multiple_of(jnp.maximum(n - 1, 0) * QB, QB)
        mask = _attn_mask(n, start)
        band = pl.ds(start, 2 * QB)
        lane = lax.broadcasted_iota(jnp.int32, (QB, NH), 1)
        ones = jnp.ones((NKEYS, HD), BF)
        pairs = [(b, g) for b in range(bl) for g in range(NKV)]
        gsl = [slice(HD * g, HD * g + HD) for g in range(NKV)]
        qns = [_stack_heads(q_ref.at[b], g, lambda x: (x * _rstd(x) * (qg_ref[...] * SCALE)).astype(BF)) for b, g in pairs]
        ss = [jnp.where(mask, _dot(qns[i], _keys_t(kt_ref.at[b], band, gsl[g])), NEG) for i, (b, g) in enumerate(pairs)]
        sinks = [_stack_sinks(sk_ref, g) for g in range(NKV)]
        mxs = [jnp.maximum(jnp.max(ss[i], -1, keepdims=True), sinks[g]) for i, (b, g) in enumerate(pairs)]
        oas = [_dot(jnp.exp(ss[i] - mxs[i]), jnp.concatenate([_keys(v_ref.at[b], band, gsl[g]), ones], axis=1))
               for i, (b, g) in enumerate(pairs)]
        lses = [jnp.zeros((QB, NH), F32) for _ in range(bl)]
        for i, (b, g) in enumerate(pairs):
            den = oas[i][:, HD:HD + 1] + jnp.exp(sinks[g] - mxs[i])
            o = oas[i][:, :HD] * (1.0 / den)
            l = mxs[i] + jnp.log(den)
            for r in range(RQ):
                h = g * RQ + r
                o_ref[b, :, HD * h:HD * h + HD] = o[r * QB:(r + 1) * QB].astype(BF)
                lses[b] = jnp.where(lane == h, l[r * QB:(r + 1) * QB], lses[b])
        for b in range(bl):
            lse_ref[b] = lses[b]

    blk = lambda c: pl.BlockSpec((bl, QB, c), lambda n: (0, n, 0))
    (o, lse), _, got = _call(
        body, f"attn_fwd{j}", (nb,),
        [blk(D), pl.BlockSpec((bl, KVD, lp), lambda n: (0, 0, 0)), pl.BlockSpec((bl, lp, KVD), lambda n: (0, 0, 0)),
         pl.BlockSpec((None, 1, HD), lambda n: (j, 0, 0)), pl.BlockSpec((None, 1, NH), lambda n: (j, 0, 0))],
        [blk(D), blk(NH)], [_sds((bl, lp, D), BF), _sds((bl, lp, NH), F32)], (q.reshape(bl, lp, D), kt, v, qg, sinks), gat=gat)
    return (o.reshape(t, D), lse.reshape(t, NH)), (), got


def loss_fwd(h, tgt):
    bl, lp, _ = h.shape
    seq = tgt.shape[1]
    cb = 256

    def body(h_ref, t_ref, dh_ref, loss_ref):
        _init(loss_ref, (pl.program_id(0) == 0) & (pl.program_id(1) == 0))
        err = h_ref[NMETA:NMETA + seq, :] - t_ref[...]
        dh_ref[...] = jnp.zeros_like(dh_ref)
        dh_ref[NMETA:NMETA + seq, :] = err * (1.0 / D)
        loss_ref[...] += (0.5 / D) * jnp.sum(err * err)

    return pl.pallas_call(
        body, name="loss_fwd", grid=(bl, D // cb),
        in_specs=[pl.BlockSpec((None, lp, cb), lambda b, c: (b, 0, c)), pl.BlockSpec((None, seq, cb), lambda b, c: (b, 0, c))],
        out_specs=[pl.BlockSpec((None, lp, cb), lambda b, c: (b, 0, c)), pl.BlockSpec((8, 128), lambda b, c: (0, 0))],
        out_shape=[_sds((bl, lp, D), F32), _sds((8, 128), F32)],
        compiler_params=_cp(2),
    )(h, tgt)


def ffn_bwd_x(dh2, g, up, h1, nf, l, wd, wg, wu, w_o, tm, xch):
    t = dh2.shape[0]

    def body(dh2_ref, g_ref, up_ref, h1_ref, nf_ref, wd_ref, wg_ref, wu_ref, *rest):
        if w_o is None:
            dg_ref, du_ref, dh1_ref, dnf_ref = rest
        else:
            wo_ref, dg_ref, du_ref, dh1_ref, dnf_ref, do_ref = rest
        _init(dnf_ref, pl.program_id(0) == 0)
        dh2v = dh2_ref[...]
        dhid = _dot_nt(dh2v, wd_ref[...])
        gv = g_ref[...].astype(F32)
        uv = up_ref[...].astype(F32)
        sg = _sig(gv)
        dgv = (dhid * uv * (sg * (1.0 + gv * (1.0 - sg)))).astype(BF)
        duv = (dhid * (gv * sg)).astype(BF)
        dg_ref[...] = dgv
        du_ref[...] = duv
        dnorm = _dot(dgv, wg_ref[...]) + _dot(duv, wu_ref[...])
        dx, dnf = _rms_bwd(h1_ref[...], nf_ref[...], dnorm)
        dh1 = dh2v + dx
        dh1_ref[...] = dh1
        dnf_ref[...] += dnf
        if w_o is not None:
            do_ref[...] = _dot_nt(dh1, wo_ref[...]).astype(BF)

    attn = w_o is not None
    return _call(
        body, f"ffn_bwd_x{l}", (t // tm,),
        [_row(tm, D), _row(tm, DFF), _row(tm, DFF), _row(tm, D), _lay(l, (1, D)),
         _res((DFF, D)), _res((DFF, D)), _res((DFF, D))] + ([_res((D, D))] if attn else []),
        [_row(tm, DFF), _row(tm, DFF), _row(tm, D), _acc((1, D))] + ([_row(tm, D)] if attn else []),
        [_sds((t, DFF), BF), _sds((t, DFF), BF), _sds((t, D), F32), _sds((1, D), F32)] + ([_sds((t, D), BF)] if attn else []),
        (dh2, g, up, h1, nf, wd, wg, wu) + ((w_o,) if attn else ()), xch=xch)


def mm_tn(x, dy, tm, name, split=False, transposed=False, xch=()):
    t, kk = x.shape
    nn = dy.shape[1]
    n8 = nn // NDEV
    nsteps = t // tm

    def body(x_ref, dy_ref, o_ref, acc):
        i = pl.program_id(0)
        _init(acc, i == 0)
        acc[...] += _dot_tn(x_ref[...], dy_ref[...])

        @pl.when(i == nsteps - 1)
        def _():
            if split:
                for p in range(NDEV):
                    o_ref[p] = acc[:, p * n8:(p + 1) * n8].astype(BF)
            elif transposed:
                o_ref[...] = acc[...].T.astype(BF)
            else:
                o_ref[...] = acc[...].astype(BF)

    oshape = (NDEV, kk, n8) if split else ((nn, kk) if transposed else (kk, nn))
    (out,), got, _ = _call(body, name, (nsteps,), [_row(tm, kk), _row(tm, nn)], [_acc(oshape)], [_sds(oshape, BF)], (x, dy),
                           scratch=[pltpu.VMEM((kk, nn), F32)], xch=xch)
    return out, got


def proj_bwd(dy, w, h, g, lg, dh_in, tm, name, xch=()):
    t = h.shape[0]
    nn = dy.shape[1]
    wspec = _res(w.shape)
    gspec = _res((1, D)) if lg is None else _lay(lg, (1, D))

    def body(dy_ref, w_ref, h_ref, g_ref, dhin_ref, dh_ref, dg_ref):
        _init(dg_ref, pl.program_id(0) == 0)
        du = _dot_nt(dy_ref[...], w_ref[...])
        dx, dg = _rms_bwd(h_ref[...], g_ref[...], du)
        dh_ref[...] = dhin_ref[...] + dx
        dg_ref[...] += dg

    return _call(body, name, (t // tm,), [_row(tm, nn), wspec, _row(tm, D), gspec, _row(tm, D)],
                 [_row(tm, D), _acc((1, D))], [_sds((t, D), F32), _sds((1, D), F32)], (dy, w, h, g, dh_in), xch=xch)


def attn_bwd(q, k, kt, vt, do, o, lse, qg, sinks, j, bl, lp, xch):
    t = q.shape[0]
    nb = lp // QB

    def body(q_ref, k_ref, kt_ref, vt_ref, do_ref, o_ref, lse_ref, qg_ref, sk_ref, dq_ref, dk_ref, dv_ref, dqg_ref, dsk_ref):
        n = pl.program_id(0)
        for ref in (dk_ref, dv_ref, dqg_ref, dsk_ref):
            _init(ref, n == 0)
        start = pl.multiple_of(jnp.maximum(n - 1, 0) * QB, QB)
        mask = _attn_mask(n, start)
        band = pl.ds(start, 2 * QB)
        lane = lax.broadcasted_iota(jnp.int32, (1, NH), 1)
        dqg = jnp.zeros((1, HD), F32)
        dsk = jnp.zeros((1, NH), F32)
        pairs = [(b, g) for b in range(bl) for g in range(NKV)]
        idx = range(len(pairs))
        gsl = [slice(HD * g, HD * g + HD) for g in range(NKV)]
        qhs = [_stack_heads(q_ref.at[b], g, lambda x: x) for b, g in pairs]
        rss = [_rstd(qhs[i]) for i in idx]
        qns = [(qhs[i] * rss[i] * (qg_ref[...] * SCALE)).astype(BF) for i in idx]
        lss = [_stack_cols(lse_ref.at[b], g) for b, g in pairs]
        dohs = [_stack_heads(do_ref.at[b], g, lambda x: x) for b, g in pairs]
        deltas = [jnp.sum(dohs[i].astype(F32) * _stack_heads(o_ref.at[b], g, lambda x: x).astype(F32), axis=-1, keepdims=True)
                  for i, (b, g) in enumerate(pairs)]
        prs = [jnp.where(mask, jnp.exp(_dot(qns[i], _keys_t(kt_ref.at[b], band, gsl[g])) - lss[i]), 0.0)
               for i, (b, g) in enumerate(pairs)]
        dss = [(prs[i] * (_dot(dohs[i], _keys_t(vt_ref.at[b], band, gsl[g])) - deltas[i])).astype(BF)
               for i, (b, g) in enumerate(pairs)]
        for i, (b, g) in enumerate(pairs):
            gs = gsl[g]
            dkt = _dot_tn(qns[i], dss[i])
            dvt = _dot_tn(dohs[i], prs[i])
            dk_ref[b, gs, band] += dkt[:, :2 * QB]
            dv_ref[b, gs, band] += dvt[:, :2 * QB]
            dk_ref[b, gs, 0:NMETA] += dkt[:, 2 * QB:]
            dv_ref[b, gs, 0:NMETA] += dvt[:, 2 * QB:]
        dqns = [_dot(dss[i], _keys(k_ref.at[b], band, gsl[g])) * SCALE for i, (b, g) in enumerate(pairs)]
        for i, (b, g) in enumerate(pairs):
            qh, rs, dqn = qhs[i], rss[i], dqns[i]
            dsink = jnp.exp(_stack_sinks(sk_ref, g) - lss[i]) * deltas[i]
            z = dqn * qg_ref[...]
            dq = rs * z - qh * (rs * rs * rs * jnp.mean(z * qh, axis=-1, keepdims=True))
            dqg = dqg + jnp.sum(dqn * qh * rs, axis=0, keepdims=True)
            for r in range(RQ):
                h = g * RQ + r
                dq_ref[b, :, HD * h:HD * h + HD] = dq[r * QB:(r + 1) * QB]
                dsk = dsk + jnp.where(lane == h, -jnp.sum(dsink[r * QB:(r + 1) * QB]), 0.0)
        dqg_ref[...] += dqg
        dsk_ref[...] += dsk

    blk = lambda c: pl.BlockSpec((bl, QB, c), lambda n: (0, n, 0))
    seq = pl.BlockSpec((bl, lp, KVD), lambda n: (0, 0, 0))
    seq_t = pl.BlockSpec((bl, KVD, lp), lambda n: (0, 0, 0))
    as3 = lambda a: a.reshape(bl, lp, a.shape[-1])
    (dq, dk, dv, dqg, dsk), got, _ = _call(
        body, f"attn_bwd{j}", (nb,),
        [blk(D), seq, seq_t, seq_t, blk(D), blk(D), blk(NH),
         pl.BlockSpec((None, 1, HD), lambda n: (j, 0, 0)), pl.BlockSpec((None, 1, NH), lambda n: (j, 0, 0))],
        [blk(D), seq_t, seq_t, pl.BlockSpec((1, HD), lambda n: (0, 0)), pl.BlockSpec((1, NH), lambda n: (0, 0))],
        [_sds((bl, lp, D), F32), _sds((bl, KVD, lp), F32), _sds((bl, KVD, lp), F32), _sds((1, HD), F32), _sds((1, NH), F32)],
        (as3(q), k, kt, vt, as3(do), as3(o), as3(lse), qg, sinks), xch=xch)
    return (dq.reshape(t, D), dk, dv, dqg, dsk), got, ()


def kv_bwd_pre(dk, dv, kv, kng, tm):
    t = kv.shape[0]

    def body(dk_ref, dv_ref, kv_ref, g_ref, dkv_ref, dg_ref):
        _init(dg_ref, pl.program_id(0) == 0)
        dg = jnp.zeros((1, HD), F32)
        outs = []
        for s in range(NKV):
            sl = slice(HD * s, HD * s + HD)
            dx, dgs = _rms_bwd(kv_ref[:, sl], g_ref[...], dk_ref[:, sl])
            outs.append(dx)
            dg = dg + dgs
        dkv_ref[:, :KVD] = jnp.concatenate(outs, axis=1).astype(BF)
        dkv_ref[:, KVD:] = dv_ref[...].astype(BF)
        dg_ref[...] += dg

    return pl.pallas_call(
        body, name="kv_bwd_pre", grid=(t // tm,),
        in_specs=[_row(tm, KVD)] * 2 + [_row(tm, 2 * KVD), _res((1, HD))],
        out_specs=[_row(tm, 2 * KVD), _acc((1, HD))], out_shape=[_sds((t, 2 * KVD), BF), _sds((1, HD), F32)],
        compiler_params=_cp(1),
    )(dk, dv, kv, kng)


def conv_out_bwd(dh1, c, ln_g, ln_b, w_out, i, tm, xch):
    t = dh1.shape[0]

    def body(dh1_ref, c_ref, g_ref, b_ref, w_ref, dc_ref, dg_ref, db_ref, dbo_ref):
        first = pl.program_id(0) == 0
        _init(dg_ref, first)
        _init(db_ref, first)
        _init(dbo_ref, first)
        dh1v = dh1_ref[...]
        ds = _dot_nt(dh1v, w_ref[...])
        cv = c_ref[...]
        xc = cv - jnp.mean(cv, axis=-1, keepdims=True)
        rstd = lax.rsqrt(jnp.mean(xc * xc, axis=-1, keepdims=True) + EPS)
        xh = xc * rstd
        n = xh * g_ref[...] + b_ref[...]
        sg = _sig(n)
        dn = ds * (sg * (1.0 + n * (1.0 - sg)))
        dxh = dn * g_ref[...]
        dc_ref[...] = rstd * (dxh - jnp.mean(dxh, axis=-1, keepdims=True) - xh * jnp.mean(dxh * xh, axis=-1, keepdims=True))
        dg_ref[...] += jnp.sum(dn * xh, axis=0, keepdims=True)
        db_ref[...] += jnp.sum(dn, axis=0, keepdims=True)
        dbo_ref[...] += jnp.sum(dh1v, axis=0, keepdims=True)

    return _call(
        body, f"conv_out_bwd{i}", (t // tm,), [_row(tm, D), _row(tm, D), _lay(i, (1, D)), _lay(i, (1, D)), _res((D, D))],
        [_row(tm, D), _acc((1, D)), _acc((1, D)), _acc((1, D))], [_sds((t, D), F32)] + [_sds((1, D), F32)] * 3,
        (dh1, c, ln_g, ln_b, w_out), xch=xch)


def conv_mid_bwd(dc, a, big, dw, i, tm, tpb, xch):
    t = dc.shape[0]
    nsteps = t // tm

    def body(dc_ref, nxt_ref, a_ref, prv_ref, big_ref, dw_ref, da_ref, dbin_ref, ddw_ref, dce, ae, wacc, bacc, taps):
        i_ = pl.program_id(0)
        _spread_taps(dw_ref, taps)
        _init(wacc, i_ == 0)
        _init(bacc, i_ == 0)
        dce[0:tm] = dc_ref[...]
        dce[tm:] = jnp.where(i_ % tpb == tpb - 1, 0.0, nxt_ref[...])
        ae[0:HALO] = jnp.where(i_ % tpb == 0, 0.0, prv_ref[...])
        ae[HALO:] = a_ref[...]

        def chunk(k, carry):
            r0 = pl.multiple_of(k * CHUNK, CHUNK)
            wdc = _shifted(dce[pl.ds(r0, 2 * CHUNK), :])
            wa = _shifted(ae[pl.ds(r0, 2 * CHUNK), :])
            dcc = wdc[0][0:CHUNK]
            da = jnp.zeros((CHUNK, D), F32)
            for j in range(CW):
                da = da + _tap_weight(taps, j) * _tap(wdc, CW - 1 - j)
                wacc[j] += _fold8(dcc * _tap(wa, j + 2))
            bv = big_ref[pl.ds(r0, CHUNK), :].astype(F32)
            a1, sg = bv[:, :D], _sig(bv[:, D:])
            d1 = da * sg
            d2 = da * a1 * sg * (1.0 - sg)
            da_ref[pl.ds(r0, CHUNK), 0:D] = d1.astype(BF)
            da_ref[pl.ds(r0, CHUNK), D:2 * D] = d2.astype(BF)
            bacc[:, 0:D] += _fold8(d1)
            bacc[:, D:2 * D] += _fold8(d2)
            return carry

        lax.fori_loop(0, tm // CHUNK, chunk, 0)

        @pl.when(i_ == nsteps - 1)
        def _():
            dbin_ref[...] = jnp.sum(bacc[...], axis=0, keepdims=True)
            ddw_ref[...] = jnp.sum(wacc[...], axis=1)

    return _call(
        body, f"conv_mid_bwd{i}", (nsteps,),
        [_row(tm, D), _next_halo(tm, t), _row(tm, D), _prev_halo(tm), _row(tm, 2 * D), _lay(i, (CW, D))],
        [_row(tm, 2 * D), _acc((1, 2 * D)), _acc((CW + 1, D))],
        [_sds((t, 2 * D), BF), _sds((1, 2 * D), F32), _sds((CW + 1, D), F32)],
        (dc, dc, a, a, big, dw),
        scratch=[pltpu.VMEM((tm + HALO, D), F32), pltpu.VMEM((tm + HALO, D), F32),
                 pltpu.VMEM((CW + 1, 8, D), F32), pltpu.VMEM((8, 2 * D), F32), pltpu.VMEM((CW, 8, D), F32)], xch=xch)


def input_grads(dh0, seq):
    bl, lp, _ = dh0.shape
    cb = 256

    def body(dh_ref, gx_ref, gm_ref):
        _init(gm_ref, pl.program_id(1) == 0)
        gx_ref[...] = dh_ref[NMETA:NMETA + seq, :]
        gm_ref[...] += dh_ref[0:NMETA, :]

    return pl.pallas_call(
        body, name="input_grads", grid=(D // cb, bl),
        in_specs=[pl.BlockSpec((None, lp, cb), lambda c, b: (b, 0, c))],
        out_specs=[pl.BlockSpec((None, seq, cb), lambda c, b: (b, 0, c)), pl.BlockSpec((NMETA, cb), lambda c, b: (0, c))],
        out_shape=[_sds((bl, seq, D), F32), _sds((NMETA, D), F32)],
        compiler_params=_cp(2),
    )(dh0)


GATHER_PLAN = {
    "embed": [("conv_w_in", 0), ("conv_w_out", 0)],
    "conv_mid_fwd0": [("ffn_w_gate", 0), ("ffn_w_up", 0), ("ffn_w_down", 0)],
    "mixer_ffn_fwd0": [("conv_w_in", 1), ("conv_w_out", 1), ("ffn_w_gate", 1)],
    "conv_mid_fwd1": [("ffn_w_up", 1), ("ffn_w_down", 1), ("w_kv", 0), ("w_q", 0)],
    "mixer_ffn_fwd1": [("w_o", 0), ("ffn_w_down", 2)],
    "attn_fwd0": [("ffn_w_gate", 2), ("ffn_w_up", 2), ("w_q", 1), ("w_o", 1)],
    "attn_fwd1": [("ffn_w_gate", 3), ("ffn_w_up", 3), ("ffn_w_down", 3)],
}
EXCHANGE_PLAN = {
    "attn_bwd1": [("ffn_w_down3", ALL), ("ffn_w_gate3", ALL)],
    "dw_down2": [("w_o1", ALL)],
    "ffn_bwd_x2": [("ffn_w_up3", ALL), ("w_q1", ALL)],
    "attn_bwd0": [("ffn_w_down2", ALL), ("ffn_w_gate2", ALL)],
    "dw_down1": [("w_o0", ALL), ("w_q0", H1)],
    "ffn_bwd_x1": [("ffn_w_up2", ALL), ("w_q0", H2), ("w_kv", ALL)],
    "dw_gate1": [("ffn_w_down1", H1)],
    "dw_up1": [("ffn_w_down1", H2)],
    "conv_mid_bwd1": [("ffn_w_gate1", ALL), ("ffn_w_up1", H1)],
    "conv_in_bwd1": [("ffn_w_up1", H2)],
    "dw_down0": [("conv_w_out1", ALL)],
    "ffn_bwd_x0": [("conv_w_in1", ALL)],
    "dw_gate0": [("ffn_w_down0", H1)],
    "dw_up0": [("ffn_w_down0", H2)],
    "conv_out_bwd0": [("ffn_w_gate0", H1)],
    "conv_mid_bwd0": [("ffn_w_gate0", H2), ("ffn_w_up0", H1), ("conv_w_out0", ALL)],
    "dw_conv_in0": [("ffn_w_up0", H2)],
    "conv_in_bwd0": [("conv_w_in0", H1)],
    "tail": [("conv_w_in0", H2)],
}
BIG = {"conv_w_in": "pieces", "conv_w_out": "rows", "w_kv": "rows", "w_q": "rows", "w_o": "rows",
       "ffn_w_gate": "rows", "ffn_w_up": "rows", "ffn_w_down": "rows"}
EXCHANGE_KIND = BIG
TRANSPOSED = ("ffn_w_gate", "ffn_w_up")


def gathered_matrix(name, layer, blocks8):
    if BIG[name] == "rows":
        return blocks8.reshape(NDEV * blocks8.shape[1], blocks8.shape[2])
    return join_columns(blocks8, f"join_{name}{layer}")


def local_step(x, tgt, meta8, w, shards):
    bl, seq, _ = x.shape
    lp = -(-(NMETA + seq) // QB) * QB
    tpb = 4
    tm = lp // tpb
    t = bl * lp
    na = 2
    flat = lambda a: a.reshape(t, D)
    mats = {}

    def riders(carrier):
        return [shards[key] for key in GATHER_PLAN[carrier]]

    def landed(carrier, blocks):
        for key, b8 in zip(GATHER_PLAN[carrier], blocks):
            mats[key] = gathered_matrix(*key, b8)

    h0, got = embed(x, meta8, lp, riders("embed"))
    landed("embed", got)
    h = flat(h0)
    saved = []
    kvs = None
    for l in range(4):
        rec = {"h": h}
        if l < na:
            rec["u"], rec["big"], rec["a"] = conv_in_fwd(h, w["norm_mix"], l, mats["conv_w_in", l], w["conv_b_in"], l, tm)
            name = f"conv_mid_fwd{l}"
            (rec["c"], rec["s"]), _, got = conv_mid_fwd(rec["a"], w["conv_dw"], w["conv_ln_g"], w["conv_ln_b"], l, tm, tpb,
                                                         riders(name))
            landed(name, got)
            mixed, w_out, lw, bias = rec["s"], mats["conv_w_out", l], l, w["conv_b_out"]
        else:
            j = l - na
            if kvs is None:
                kvs = dict(zip(("kn", "kv", "k", "v"), kv_fwd(h, w["kv_norm"], mats["w_kv", 0], w["k_norm"], tm)))
                kvs["h"] = h
                kvs["k3"], kvs["v3"] = kvs["k"].reshape(bl, lp, KVD), kvs["v"].reshape(bl, lp, KVD)
                kvs["kt"], kvs["vt"] = transpose_seq(kvs["k3"], "transpose_k"), transpose_seq(kvs["v3"], "transpose_v")
            rec["u"], rec["q"] = q_fwd(h, w["norm_mix"], l, mats["w_q", j], j, tm)
            name = f"attn_fwd{j}"
            (rec["o"], rec["lse"]), _, got = attn_fwd(rec["q"], kvs["kt"], kvs["v3"], w["q_norm"], w["attn_sinks"], j, bl, lp,
                                                      riders(name) if name in GATHER_PLAN else [])
            if name in GATHER_PLAN:
                landed(name, got)
            mixed, w_out, lw, bias = rec["o"], mats["w_o", j], j, None
        name = f"mixer_ffn_fwd{l}"
        (rec["h1"], rec["u2"], rec["g"], rec["up"], rec["hid"], h), _, got = mixer_ffn_fwd(
            h, mixed, w_out, lw, bias, w["norm_ffn"], l, mats["ffn_w_gate", l], mats["ffn_w_up", l], mats["ffn_w_down", l], tm // 2,
            riders(name) if name in GATHER_PLAN else [])
        if name in GATHER_PLAN:
            landed(name, got)
        saved.append(rec)

    dh3, loss_blk = loss_fwd(h.reshape(bl, lp, D), tgt)
    dh = flat(dh3)

    big, small, arrived = {}, {}, {}
    dks, dvs = [], []

    def ride(kernel_name):
        return [(big[nm], EXCHANGE_KIND[nm.rstrip("0123456789")], ks) for nm, ks in EXCHANGE_PLAN.get(kernel_name, [])]

    def landed_x(kernel_name, arrivals):
        for (nm, _), got in zip(EXCHANGE_PLAN.get(kernel_name, []), arrivals):
            arrived.setdefault(nm, []).append(got)

    def dw(name, grad, x, dy, **kw):
        big[grad], got = mm_tn(x, dy, 2 * tm, name, xch=ride(name), **kw)
        landed_x(name, got)

    for l in reversed(range(4)):
        rec = saved[l]
        dw(f"dw_down{l}", f"ffn_w_down{l}", rec["hid"], dh)
        name = f"ffn_bwd_x{l}"
        outs, got, _ = ffn_bwd_x(
            dh, rec["g"], rec["up"], rec["h1"], w["norm_ffn"], l, mats["ffn_w_down", l], mats["ffn_w_gate", l], mats["ffn_w_up", l],
            mats["w_o", l - na] if l >= na else None, tm // 2, ride(name))
        landed_x(name, got)
        dg, du, dh1, small[f"norm_ffn{l}"] = outs[:4]
        dw(f"dw_gate{l}", f"ffn_w_gate{l}", rec["u2"], dg, transposed=True)
        dw(f"dw_up{l}", f"ffn_w_up{l}", rec["u2"], du, transposed=True)
        if l >= na:
            j = l - na
            dw(f"dw_o{j}", f"w_o{j}", rec["o"], dh1)
            name = f"attn_bwd{j}"
            (dq, dk, dv, small[f"q_norm{j}"], small[f"attn_sinks{j}"]), got, _ = attn_bwd(
                rec["q"], kvs["k3"], kvs["kt"], kvs["vt"], outs[4], rec["o"], rec["lse"], w["q_norm"], w["attn_sinks"], j, bl, lp,
                ride(name))
            landed_x(name, got)
            dks.append(dk)
            dvs.append(dv)
            dw(f"dw_q{j}", f"w_q{j}", rec["u"], dq)
            dh, small[f"norm_mix{l}"] = proj_bwd(dq, mats["w_q", j], rec["h"], w["norm_mix"], l, dh1, tm, f"q_bwd{j}")[0]
            if l == na:
                dkv, small["k_norm"] = kv_bwd_pre(sum_transposed(*dks), sum_transposed(*dvs), kvs["kv"], w["k_norm"], tm)
                dw("dw_kv", "w_kv", kvs["kn"], dkv)
                dh, small["kv_norm"] = proj_bwd(dkv, mats["w_kv", 0], kvs["h"], w["kv_norm"], None, dh, tm, "kv_bwd")[0]
        else:
            name = f"conv_out_bwd{l}"
            (dc, small[f"conv_ln_g{l}"], small[f"conv_ln_b{l}"], small[f"conv_b_out{l}"]), got, _ = conv_out_bwd(
                dh1, rec["c"], w["conv_ln_g"], w["conv_ln_b"], mats["conv_w_out", l], l, tm, ride(name))
            landed_x(name, got)
            dw(f"dw_conv_out{l}", f"conv_w_out{l}", rec["s"], dh1)
            name = f"conv_mid_bwd{l}"
            (da, small[f"conv_b_in{l}"], small[f"conv_dw{l}"]), got, _ = conv_mid_bwd(
                dc, rec["a"], rec["big"], w["conv_dw"], l, tm, tpb, ride(name))
            landed_x(name, got)
            dw(f"dw_conv_in{l}", f"conv_w_in{l}", rec["u"], da, split=True)
            name = f"conv_in_bwd{l}"
            (dh, small[f"norm_mix{l}"]), got, _ = proj_bwd(da, mats["conv_w_in", l], rec["h"], w["norm_mix"], l, dh1, tm, name,
                                                           ride(name))
            landed_x(name, got)
    grad_x, small["meta_tokens"] = input_grads(dh.reshape(bl, lp, D), seq)
    return loss_blk, grad_x, big, arrived, small


def all_gather_blocks(blocks):
    n = len(blocks)

    def body(*refs):
        srcs, outs, sems = refs[:n], refs[n:2 * n], refs[2 * n:]
        _gat_start(srcs, outs, sems)
        _gat_forward(srcs, outs, sems)
        _gat_wait(srcs, outs, sems)

    any_spec = pl.BlockSpec(memory_space=pl.ANY)
    return pl.pallas_call(
        body, name="all_gather_blocks", out_shape=[_sds((NDEV,) + tuple(a.shape), a.dtype) for a in blocks],
        in_specs=[any_spec] * n, out_specs=[any_spec] * n, scratch_shapes=_xch_scratch(n),
    )(*blocks)


def cast_bf16(ws):
    n = len(ws)
    counts = [1 if x.ndim == 2 else x.shape[0] for x in ws]

    def body(*refs):
        outs = iter(refs[n:])
        for a in range(n):
            for l in range(counts[a]):
                next(outs)[...] = (refs[a][...] if ws[a].ndim == 2 else refs[a][l]).astype(BF)

    flat = pl.pallas_call(
        body, name="cast_bf16", out_shape=[_sds(x.shape[-2:], BF) for x, k in zip(ws, counts) for _ in range(k)],
        compiler_params=pltpu.CompilerParams(vmem_limit_bytes=VMEM_LIMIT),
    )(*ws)
    it = iter(flat)
    return [[next(it) for _ in range(k)] for k in counts]


def join_columns(w8, name):
    _, kk, n8 = w8.shape

    def body(x_ref, o_ref):
        o_ref[...] = jnp.concatenate([x_ref[p] for p in range(NDEV)], axis=1)

    return pl.pallas_call(body, name=name, out_shape=_sds((kk, NDEV * n8), w8.dtype),
                          compiler_params=pltpu.CompilerParams(vmem_limit_bytes=VMEM_LIMIT))(w8)


def _adamw_math(w, m, v, g):
    m2 = B1 * m + (1.0 - B1) * g
    v2 = B2 * v + (1.0 - B2) * (g * g)
    mh = m2 / (1.0 - B1 ** STEP)
    vh = v2 / (1.0 - B2 ** STEP)
    return -LR * (mh / (jnp.sqrt(vh) + AEPS) + WD * w), m2, v2


def adamw_big(w, m, v, parts, name, xch=(), gat=()):
    lyr, r, c = w.shape
    by_cols = c >= 512
    blk = (lyr, r, 256) if by_cols else (lyr, 256 if r % 256 == 0 else r, c)
    imap = (lambda i: (0, 0, i)) if by_cols else (lambda i: (0, i, 0))
    counts = [len(per_layer) for per_layer in parts]

    def body(w_ref, m_ref, v_ref, *rest):
        p_refs, (g_ref, d_ref, m2_ref, v2_ref) = iter(rest[:sum(counts)]), rest[sum(counts):]
        for l in range(lyr):
            g = None
            for _ in range(counts[l]):
                ref = next(p_refs)
                for q in range(ref.shape[0]):
                    g = ref[q].astype(F32) if g is None else g + ref[q].astype(F32)
            g_ref[l] = g
            d_ref[l], m2_ref[l], v2_ref[l] = _adamw_math(w_ref[l], m_ref[l], v_ref[l], g)

    spec = pl.BlockSpec(blk, imap)
    flat = [a for per_layer in parts for a in per_layer]
    pspecs = [pl.BlockSpec((a.shape[0],) + blk[1:], imap) for a in flat]
    return _call(body, name, ((c // 256) if by_cols else (r // blk[1]),), [spec, spec, spec] + pspecs,
                 [spec] * 4, [_sds((lyr, r, c), F32)] * 4, (w, m, v, *flat), xch=xch, gat=gat)


SMALL_ROWS = 104
REPLICATED = {"norm_mix": (0, 4, D), "norm_ffn": (4, 4, D), "kv_norm": (8, 1, D), "k_norm": (9, 1, HD), "q_norm": (10, 2, HD),
              "attn_sinks": (12, 2, NH)}
LOSS_ROW = 14
SHARDED = {"meta_tokens": (16, NMETA), "conv_b_in": (32, 4), "conv_dw": (36, 2 * CW), "conv_ln_g": (98, 2), "conv_ln_b": (100, 2),
           "conv_b_out": (102, 2)}


def pack_small(gs, loss_blk):
    order = ([f"norm_mix{l}" for l in range(4)] + [f"norm_ffn{l}" for l in range(4)] + ["kv_norm", "k_norm", "q_norm0", "q_norm1",
             "attn_sinks0", "attn_sinks1", "meta_tokens", "conv_b_in0", "conv_b_in1", "conv_dw0", "conv_dw1", "conv_ln_g0",
             "conv_ln_g1", "conv_ln_b0", "conv_ln_b1", "conv_b_out0", "conv_b_out1"])

    def body(*refs):
        r = dict(zip(order, refs))
        loss_ref, o_ref = refs[len(order)], refs[len(order) + 1]
        o_ref[...] = jnp.zeros_like(o_ref)
        for l in range(4):
            o_ref[l:l + 1, :] = r[f"norm_mix{l}"][...]
            o_ref[4 + l:5 + l, :] = r[f"norm_ffn{l}"][...]
        o_ref[8:9, :] = r["kv_norm"][...]
        o_ref[9:10, 0:HD] = r["k_norm"][...]
        for j in range(2):
            o_ref[10 + j:11 + j, 0:HD] = r[f"q_norm{j}"][...]
            o_ref[12 + j:13 + j, 0:NH] = r[f"attn_sinks{j}"][...]
            o_ref[32 + 2 * j:33 + 2 * j, :] = r[f"conv_b_in{j}"][:, 0:D]
            o_ref[33 + 2 * j:34 + 2 * j, :] = r[f"conv_b_in{j}"][:, D:2 * D]
            o_ref[36 + CW * j:36 + CW * (j + 1), :] = r[f"conv_dw{j}"][0:CW, :]
            o_ref[98 + j:99 + j, :] = r[f"conv_ln_g{j}"][...]
            o_ref[100 + j:101 + j, :] = r[f"conv_ln_b{j}"][...]
            o_ref[102 + j:103 + j, :] = r[f"conv_b_out{j}"][...]
        o_ref[LOSS_ROW:LOSS_ROW + 1, 0:1] = loss_ref[0:1, 0:1]
        o_ref[16:16 + NMETA, :] = r["meta_tokens"][...]

    return pl.pallas_call(body, name="pack_small", out_shape=_sds((SMALL_ROWS, D), F32))(*[gs[k] for k in order], loss_blk)


def adamw_small(g8, wts, mom, var):
    names = list(REPLICATED) + list(SHARDED)
    shape2 = {"kv_norm": (1, D), "k_norm": (1, HD)}
    ins = [a[k].reshape(shape2.get(k, a[k].shape)) for a in (wts, mom, var) for k in names]
    n = len(names)

    def body(*refs):
        g8_ref, w_refs, m_refs, v_refs = refs[0], refs[1:1 + n], refs[1 + n:1 + 2 * n], refs[1 + 2 * n:1 + 3 * n]
        loss_ref, outs, red_ref = refs[1 + 3 * n], refs[2 + 3 * n:-1], refs[-1]
        me = _my_index()
        acc = g8_ref[0]
        for q in range(1, NDEV):
            acc = acc + g8_ref[q]
        red_ref[...] = acc
        loss_ref[...] = red_ref[LOSS_ROW:LOSS_ROW + 1, 0:1]

        def mine(rows, width):
            acc = jnp.zeros((rows.stop - rows.start, width), F32)
            for p_ in range(NDEV):
                acc = acc + jnp.where(me == p_, red_ref[rows, p_ * width:(p_ + 1) * width], 0.0)
            return acc

        for i, k in enumerate(names):
            if k in REPLICATED:
                r0, nr, width = REPLICATED[k]
                g = red_ref[r0:r0 + nr, 0:width]
            elif k == "conv_b_in":
                half = D // (2 * D // NDEV)
                acc = jnp.zeros((2, 2 * D // NDEV), F32)
                for p_ in range(NDEV):
                    c0 = (p_ % half) * (2 * D // NDEV)
                    part = jnp.concatenate([red_ref[32 + 2 * j + p_ // half:33 + 2 * j + p_ // half, c0:c0 + 2 * D // NDEV]
                                            for j in range(2)], axis=0)
                    acc = acc + jnp.where(me == p_, part, 0.0)
                g = acc
            else:
                r0, nr = SHARDED[k]
                g = mine(slice(r0, r0 + nr), D // NDEV)
            w_, m_, v_ = w_refs[i], m_refs[i], v_refs[i]
            g_out, d_out, m_out, v_out = outs[4 * i:4 * i + 4]
            if k == "conv_dw":
                for j in range(2):
                    gj = g[CW * j:CW * (j + 1)]
                    g_out[j] = gj
                    d_out[j], m_out[j], v_out[j] = _adamw_math(w_[j], m_[j], v_[j], gj)
            else:
                g_out[...] = g
                d_out[...], m_out[...], v_out[...] = _adamw_math(w_[...], m_[...], v_[...], g)

    out_shape = [_sds((1, 1), F32)] + [_sds(ins[i].shape, F32) for i in range(n) for _ in range(4)]
    res = pl.pallas_call(body, name="adamw_small", out_shape=out_shape, scratch_shapes=[pltpu.VMEM((SMALL_ROWS, D), F32)])(g8, *ins)
    out = {k: tuple(o.reshape(wts[k].shape) for o in res[1 + 4 * i:5 + 4 * i]) for i, k in enumerate(names)}
    return res[0], out


NAMES = ["meta_tokens", "norm_mix", "norm_ffn", "conv_w_in", "conv_b_in", "conv_dw", "conv_ln_g", "conv_ln_b", "conv_w_out",
         "conv_b_out", "kv_norm", "w_kv", "k_norm", "w_q", "q_norm", "attn_sinks", "w_o", "ffn_w_gate", "ffn_w_up", "ffn_w_down"]


def kernel(x, meta_tokens, norm_mix, norm_ffn, conv_w_in, conv_b_in, conv_dw, conv_ln_g, conv_ln_b, conv_w_out, conv_b_out, kv_norm, w_kv, k_norm, w_q, q_norm, attn_sinks, w_o, ffn_w_gate, ffn_w_up, ffn_w_down, loss_target, m_meta_tokens, m_norm_mix, m_norm_ffn, m_conv_w_in, m_conv_b_in, m_conv_dw, m_conv_ln_g, m_conv_ln_b, m_conv_w_out, m_conv_b_out, m_kv_norm, m_w_kv, m_k_norm, m_w_q, m_q_norm, m_attn_sinks, m_w_o, m_ffn_w_gate, m_ffn_w_up, m_ffn_w_down, v_meta_tokens, v_norm_mix, v_norm_ffn, v_conv_w_in, v_conv_b_in, v_conv_dw, v_conv_ln_g, v_conv_ln_b, v_conv_w_out, v_conv_b_out, v_kv_norm, v_w_kv, v_k_norm, v_w_q, v_q_norm, v_attn_sinks, v_w_o, v_ffn_w_gate, v_ffn_w_up, v_ffn_w_down):
    wts = dict(zip(NAMES, (meta_tokens, norm_mix, norm_ffn, conv_w_in, conv_b_in, conv_dw, conv_ln_g, conv_ln_b, conv_w_out,
                           conv_b_out, kv_norm, w_kv, k_norm, w_q, q_norm, attn_sinks, w_o, ffn_w_gate, ffn_w_up, ffn_w_down)))
    mom = dict(zip(NAMES, (m_meta_tokens, m_norm_mix, m_norm_ffn, m_conv_w_in, m_conv_b_in, m_conv_dw, m_conv_ln_g, m_conv_ln_b,
                           m_conv_w_out, m_conv_b_out, m_kv_norm, m_w_kv, m_k_norm, m_w_q, m_q_norm, m_attn_sinks, m_w_o,
                           m_ffn_w_gate, m_ffn_w_up, m_ffn_w_down)))
    var = dict(zip(NAMES, (v_meta_tokens, v_norm_mix, v_norm_ffn, v_conv_w_in, v_conv_b_in, v_conv_dw, v_conv_ln_g, v_conv_ln_b,
                           v_conv_w_out, v_conv_b_out, v_kv_norm, v_w_kv, v_k_norm, v_w_q, v_q_norm, v_attn_sinks, v_w_o,
                           v_ffn_w_gate, v_ffn_w_up, v_ffn_w_down)))
    for k in TRANSPOSED:
        wts[k], mom[k], var[k] = (jnp.swapaxes(a, 1, 2) for a in (wts[k], mom[k], var[k]))

    big_names = list(BIG)
    layers = cast_bf16([wts[k] for k in big_names])
    shards = {(k, l): blk for k, per_layer in zip(big_names, layers) for l, blk in enumerate(per_layer)}
    vec_names = ["meta_tokens", "conv_b_in", "conv_dw", "conv_ln_g", "conv_ln_b", "conv_b_out"]
    full = dict(zip(vec_names, all_gather_blocks([wts[k] for k in vec_names])))
    join_vec = lambda a: jnp.moveaxis(a, 0, -2).reshape(a.shape[1:-1] + (NDEV * a.shape[-1],))
    w = {}
    w["conv_b_in"] = join_vec(full["conv_b_in"]).reshape(2, 1, 2 * D)
    w["conv_dw"] = join_vec(full["conv_dw"])
    for k in ("conv_ln_g", "conv_ln_b", "conv_b_out"):
        w[k] = join_vec(full[k]).reshape(2, 1, D)
    w["norm_mix"] = norm_mix.reshape(4, 1, D)
    w["norm_ffn"] = norm_ffn.reshape(4, 1, D)
    w["kv_norm"] = kv_norm.reshape(1, D)
    w["k_norm"] = k_norm.reshape(1, HD)
    w["q_norm"] = q_norm.reshape(2, 1, HD)
    w["attn_sinks"] = attn_sinks.reshape(2, 1, NH)

    loss_blk, grad_x, gbig, arrived, gs = local_step(x, loss_target, full["meta_tokens"], w, shards)

    packed = pack_small(gs, loss_blk)

    grads, delta, new_m, new_v = {}, {}, {}, {}
    tail = EXCHANGE_PLAN["tail"]
    waiting = {nm.rstrip("0123456789") for nm, _ in tail}
    order = sorted([k for k in big_names if k not in waiting], key=lambda k: -wts[k].size) + [k for k in big_names if k in waiting]
    small8 = None
    for pos, k in enumerate(order):
        flat2 = wts[k].ndim == 2
        as3 = (lambda a: a[None]) if flat2 else (lambda a: a)
        riders = tail if pos == 0 else []
        gat = [packed] if pos == 1 else []
        parts = [arrived[k]] if flat2 else [arrived[f"{k}{i}"] for i in range(wts[k].shape[0])]
        outs, got_x, got_g = adamw_big(as3(wts[k]), as3(mom[k]), as3(var[k]), parts, "adamw_" + k,
                                       xch=[(gbig[nm], EXCHANGE_KIND[nm.rstrip("0123456789")], ks) for nm, ks in riders], gat=gat)
        for (nm, _), got in zip(riders, got_x):
            arrived[nm].append(got)
        if gat:
            small8 = got_g[0]
        grads[k], delta[k], new_m[k], new_v[k] = [o[0] if flat2 else (jnp.swapaxes(o, 1, 2) if k in TRANSPOSED else o) for o in outs]
    loss, small = adamw_small(small8, wts, mom, var)
    for k, (g_, d_, m_, v_) in small.items():
        grads[k], delta[k], new_m[k], new_v[k] = g_, d_, m_, v_
    return (loss.reshape(()), grad_x, *[grads[k] for k in NAMES], *[delta[k] for k in NAMES], *[new_m[k] for k in NAMES],
            *[new_v[k] for k in NAMES])
```

```python
import jax
import jax.numpy as jnp
from jax import lax
from jax.experimental import pallas as pl
from jax.experimental.pallas import tpu as pltpu

F32 = jnp.float32
BF = jnp.bfloat16

D = 1024
DFF = 2816
NH = 16
NKV = 4
HD = 64
KVD = NKV * HD
NMETA = 16
CW = 31
HALO = 32
CHUNK = 32
QB = 128
EPS = 1e-6
NEG = -1e30
NDEV = 8
SCALE = HD ** -0.5

LR, B1, B2, AEPS, WD, STEP = 0.001, 0.9, 0.999, 1e-08, 0.01, 10

VMEM_LIMIT = 56 * 2 ** 20
MESH = pl.DeviceIdType.MESH


def _cp(n):
    return pltpu.CompilerParams(dimension_semantics=("arbitrary",) * n, vmem_limit_bytes=VMEM_LIMIT)


def _row(tm, c):
    return pl.BlockSpec((tm, c), lambda i: (i, 0))


def _res(shape):
    return pl.BlockSpec(shape, lambda i: (0,) * len(shape), pipeline_mode=pl.Buffered(1))


def _lay(l, shape):
    return pl.BlockSpec((None,) + tuple(shape), lambda i: (l,) + (0,) * len(shape), pipeline_mode=pl.Buffered(1))


def _acc(shape):
    return pl.BlockSpec(shape, lambda i: (0,) * len(shape))


def _sds(shape, dt):
    return jax.ShapeDtypeStruct(tuple(shape), dt)


def _dot(a, b):
    return jnp.dot(a.astype(BF), b.astype(BF), preferred_element_type=F32)


def _dot_nt(a, b):
    return lax.dot_general(a.astype(BF), b.astype(BF), (((1,), (1,)), ((), ())), preferred_element_type=F32)


def _dot_tn(a, b):
    return lax.dot_general(a.astype(BF), b.astype(BF), (((0,), (0,)), ((), ())), preferred_element_type=F32)


def _rstd(x):
    return lax.rsqrt(jnp.mean(x * x, axis=-1, keepdims=True) + EPS)


def _rms_bwd(x, g, dy):
    r = _rstd(x)
    z = dy * g
    dx = r * z - x * (r * r * r * jnp.mean(z * x, axis=-1, keepdims=True))
    return dx, jnp.sum(dy * x * r, axis=0, keepdims=True)


def _sig(x):
    return jax.nn.sigmoid(x)


def _fold8(x):
    out = x[0:8]
    for k in range(1, x.shape[0] // 8):
        out = out + x[8 * k:8 * k + 8]
    return out


def _shifted(win):
    return [win] + [pltpu.roll(win, 2 * CHUNK - rho, 0) for rho in range(1, 8)]


def _tap(phases, o):
    return phases[o % 8][8 * (o // 8):8 * (o // 8) + CHUNK]


def _spread_taps(dw_ref, taps):
    @pl.when(pl.program_id(0) == 0)
    def _():
        for j in range(CW):
            taps[j] = jnp.broadcast_to(dw_ref[j:j + 1, :], taps.shape[1:])


def _tap_weight(taps, j):
    return jnp.concatenate([taps[j]] * (CHUNK // 8), axis=0)


def _init(ref, first):
    @pl.when(first)
    def _():
        ref[...] = jnp.zeros_like(ref)


def _my_index():
    return 4 * lax.axis_index("x") + 2 * lax.axis_index("y") + lax.axis_index("c")


def _coords(idx):
    return (idx // 4, (idx // 2) % 2, idx % 2)


ALL = tuple(range(NDEV))
H1, H2 = (0, 1, 2, 4, 6), (3, 5, 7)


def _xch_shapes(xch):
    return [_sds((len(ks),) + ((a.shape[0] // NDEV, a.shape[1]) if k == "rows" else tuple(a.shape[1:])), a.dtype) for a, k, ks in xch]


def _xch_scratch(n):
    return [pltpu.SemaphoreType.DMA((n, NDEV)), pltpu.SemaphoreType.DMA((n, NDEV)), pltpu.SemaphoreType.DMA((n,))]


def _xch_copies(meta, srcs, outs, sems, arrivals):
    send_sems, recv_sems, local_sems = sems
    me = _my_index()

    def piece(a, p):
        if meta[a][0] == "rows":
            r = srcs[a].shape[0] // NDEV
            return srcs[a].at[pl.ds(p * r, r), :]
        return srcs[a].at[p]

    def remote(a, i, k, src):
        return pltpu.make_async_remote_copy(
            src_ref=src, dst_ref=outs[a].at[i], send_sem=send_sems.at[a, k], recv_sem=recv_sems.at[a, k],
            device_id=_coords(me ^ k), device_id_type=MESH)

    local, sends, recvs = [], [], []
    for a, (_, ks) in enumerate(meta):
        for i, k in enumerate(ks):
            if k == 0:
                local.append(pltpu.make_async_copy(piece(a, me), outs[a].at[i], local_sems.at[a]))
            else:
                sends.append(remote(a, i, k, piece(a, me ^ k)))
                if arrivals:
                    recvs.append(remote(a, i, k, piece(a, me)))
    return local, sends, recvs


def _xch_start(meta, srcs, outs, sems):
    local, sends, _ = _xch_copies(meta, srcs, outs, sems, False)
    for cp in local + sends:
        cp.start()


def _xch_wait(meta, srcs, outs, sems):
    local, sends, recvs = _xch_copies(meta, srcs, outs, sems, True)
    for cp in recvs:
        cp.wait_recv()
    for cp in sends:
        cp.wait_send()
    for cp in local:
        cp.wait()


def _gat_copies(srcs, outs, sems):
    send_sems, recv_sems, local_sems = sems
    x, y, c = lax.axis_index("x"), lax.axis_index("y"), lax.axis_index("c")
    me, sibling = (x, y, c), (x, y, 1 - c)
    chips = [(1 - x, y), (x, 1 - y), (1 - x, 1 - y)]

    def copy(a, k, owner, to, from_block=False):
        slot = outs[a].at[4 * owner[0] + 2 * owner[1] + owner[2]]
        return pltpu.make_async_remote_copy(
            src_ref=srcs[a] if from_block else slot, dst_ref=slot, send_sem=send_sems.at[a, k], recv_sem=recv_sems.at[a, k],
            device_id=to, device_id_type=MESH)

    n = len(srcs)
    local = lambda: [pltpu.make_async_copy(srcs[a], outs[a].at[4 * x + 2 * y + c], local_sems.at[a]) for a in range(n)]
    first = lambda: [cp for a in range(n) for cp in
                     [copy(a, 0, me, sibling, True)] + [copy(a, 1 + j, me, (*chip, c), True) for j, chip in enumerate(chips)]]
    landed = lambda: [copy(a, 1 + j, (*chip, c), me) for a in range(n) for j, chip in enumerate(chips)]
    passed = lambda: [copy(a, 4 + j, (*chip, c), sibling) for a in range(n) for j, chip in enumerate(chips)]
    final = lambda: [cp for a in range(n) for cp in
                     [copy(a, 0, sibling, me)] + [copy(a, 4 + j, (*chip, 1 - c), me) for j, chip in enumerate(chips)]]
    return local, first, landed, passed, final


def _gat_start(srcs, outs, sems):
    local, first, _, _, _ = _gat_copies(srcs, outs, sems)
    for cp in local() + first():
        cp.start()


def _gat_forward(srcs, outs, sems):
    _, _, landed, passed, _ = _gat_copies(srcs, outs, sems)
    for got, on in zip(landed(), passed()):
        got.wait_recv()
        on.start()


def _gat_wait(srcs, outs, sems):
    local, first, _, passed, final = _gat_copies(srcs, outs, sems)
    for cp in final():
        cp.wait_recv()
    for cp in first() + passed():
        cp.wait_send()
    for cp in local():
        cp.wait()


def _call(body, name, grid, in_specs, out_specs, out_shape, args, scratch=(), xch=(), gat=()):
    n_in, n_out, n_x, n_g, n_s = len(in_specs), len(out_specs), len(xch), len(gat), len(scratch)
    kinds = [(k, ks) for _, k, ks in xch]
    total = 1
    for g in grid:
        total *= g

    def wrapped(*refs):
        ins, refs = refs[:n_in], refs[n_in:]
        x_src, refs = refs[:n_x], refs[n_x:]
        g_src, refs = refs[:n_g], refs[n_g:]
        outs, refs = refs[:n_out], refs[n_out:]
        x_out, refs = refs[:n_x], refs[n_x:]
        g_out, refs = refs[:n_g], refs[n_g:]
        own, refs = refs[:n_s], refs[n_s:]
        x_sems, g_sems = (refs[:3], refs[3:]) if n_x else ((), refs)
        step = pl.program_id(0)
        for d in range(1, len(grid)):
            step = step * grid[d] + pl.program_id(d)
        if n_x or n_g:
            @pl.when(step == 0)
            def _():
                if n_x:
                    _xch_start(kinds, x_src, x_out, x_sems)
                if n_g:
                    _gat_start(g_src, g_out, g_sems)

        body(*ins, *outs, *own)
        if n_g:
            @pl.when(step == max(total - 2, 0))
            def _():
                _gat_forward(g_src, g_out, g_sems)

        if n_x or n_g:
            @pl.when(step == total - 1)
            def _():
                if n_x:
                    _xch_wait(kinds, x_src, x_out, x_sems)
                if n_g:
                    _gat_wait(g_src, g_out, g_sems)

    any_spec = pl.BlockSpec(memory_space=pl.ANY)
    g_shapes = [_sds((NDEV,) + tuple(a.shape), a.dtype) for a in gat]
    res = pl.pallas_call(
        wrapped, name=name, grid=grid, in_specs=list(in_specs) + [any_spec] * (n_x + n_g),
        out_specs=list(out_specs) + [any_spec] * (n_x + n_g), out_shape=list(out_shape) + _xch_shapes(xch) + g_shapes,
        scratch_shapes=list(scratch) + (_xch_scratch(n_x) if n_x else []) + (_xch_scratch(n_g) if n_g else []),
        compiler_params=_cp(len(grid)),
    )(*args, *[a for a, _, _ in xch], *gat)
    return res[:n_out], res[n_out:n_out + n_x], res[n_out + n_x:]


def embed(x, meta8, lp, gat):
    bl, seq, _ = x.shape
    c8 = D // NDEV
    cb = 2 * c8

    def body(x_ref, m_ref, h_ref):
        h_ref[0:NMETA, :] = jnp.concatenate([m_ref[0], m_ref[1]], axis=1)
        h_ref[NMETA:NMETA + seq, :] = x_ref[...]
        h_ref[NMETA + seq:, :] = jnp.zeros((lp - NMETA - seq, cb), F32)

    (h0,), _, got = _call(
        body, "embed", (bl, D // cb),
        [pl.BlockSpec((None, seq, cb), lambda b, c: (b, 0, c)), pl.BlockSpec((2, NMETA, c8), lambda b, c: (c, 0, 0))],
        [pl.BlockSpec((None, lp, cb), lambda b, c: (b, 0, c))], [_sds((bl, lp, D), F32)], (x, meta8), gat=gat)
    return h0, got


def conv_in_fwd(h, nm, l, w_in, b_in, i, tm):
    t = h.shape[0]

    def body(h_ref, g_ref, w_ref, b_ref, u_ref, big_ref, a_ref):
        x = h_ref[...]
        ub = (x * _rstd(x) * g_ref[...]).astype(BF)
        u_ref[...] = ub
        big = jnp.dot(ub, w_ref[...], preferred_element_type=F32) + b_ref[...]
        big_ref[...] = big.astype(BF)
        a_ref[...] = big[:, :D] * _sig(big[:, D:])

    return pl.pallas_call(
        body, name=f"conv_in_fwd{i}", grid=(t // tm,),
        in_specs=[_row(tm, D), _lay(l, (1, D)), _res((D, 2 * D)), _lay(i, (1, 2 * D))],
        out_specs=[_row(tm, D), _row(tm, 2 * D), _row(tm, D)],
        out_shape=[_sds((t, D), BF), _sds((t, 2 * D), BF), _sds((t, D), F32)],
        compiler_params=_cp(1),
    )(h, nm, w_in, b_in)


def _prev_halo(tm):
    return pl.BlockSpec((HALO, D), lambda i: (jnp.maximum(i * (tm // HALO) - 1, 0), 0))


def _next_halo(tm, t):
    return pl.BlockSpec((HALO, D), lambda i: (jnp.minimum((i + 1) * (tm // HALO), t // HALO - 1), 0))


def conv_mid_fwd(a, dw, ln_g, ln_b, i, tm, tpb, gat):
    t = a.shape[0]

    def body(a_ref, halo_ref, dw_ref, g_ref, b_ref, c_ref, s_ref, ext, taps):
        _spread_taps(dw_ref, taps)
        first = pl.program_id(0) % tpb == 0
        ext[0:HALO] = jnp.where(first, 0.0, halo_ref[...])
        ext[HALO:] = a_ref[...]

        def chunk(k, carry):
            r0 = pl.multiple_of(k * CHUNK, CHUNK)
            win = _shifted(ext[pl.ds(r0, 2 * CHUNK), :])
            c = jnp.zeros((CHUNK, D), F32)
            for j in range(CW):
                c = c + _tap_weight(taps, j) * _tap(win, j + 2)
            c_ref[pl.ds(r0, CHUNK), :] = c
            mu = jnp.mean(c, axis=-1, keepdims=True)
            xc = c - mu
            n = xc * lax.rsqrt(jnp.mean(xc * xc, axis=-1, keepdims=True) + EPS) * g_ref[...] + b_ref[...]
            s_ref[pl.ds(r0, CHUNK), :] = (n * _sig(n)).astype(BF)
            return carry

        lax.fori_loop(0, tm // CHUNK, chunk, 0, unroll=2)

    return _call(
        body, f"conv_mid_fwd{i}", (t // tm,),
        [_row(tm, D), _prev_halo(tm), _lay(i, (CW, D)), _lay(i, (1, D)), _lay(i, (1, D))],
        [_row(tm, D), _row(tm, D)], [_sds((t, D), F32), _sds((t, D), BF)], (a, a, dw, ln_g, ln_b),
        scratch=[pltpu.VMEM((tm + HALO, D), F32), pltpu.VMEM((CW, 8, D), F32)], gat=gat)


def mixer_ffn_fwd(h, s, w_out, lw, bias, nf, l, wg, wu, wd, tm, gat):
    t = h.shape[0]

    def body(*refs):
        if bias is None:
            h_ref, s_ref, w_ref, nf_ref, wg_ref, wu_ref, wd_ref, h1_ref, u_ref, g_ref, up_ref, hid_ref, h2_ref = refs
            y = 0.0
        else:
            h_ref, s_ref, w_ref, b_ref, nf_ref, wg_ref, wu_ref, wd_ref, h1_ref, u_ref, g_ref, up_ref, hid_ref, h2_ref = refs
            y = b_ref[...]
        h1 = h_ref[...] + (jnp.dot(s_ref[...], w_ref[...], preferred_element_type=F32) + y)
        h1_ref[...] = h1
        ub = (h1 * _rstd(h1) * nf_ref[...]).astype(BF)
        u_ref[...] = ub
        g = _dot_nt(ub, wg_ref[...])
        up = _dot_nt(ub, wu_ref[...])
        g_ref[...] = g.astype(BF)
        up_ref[...] = up.astype(BF)
        hid = (g * _sig(g) * up).astype(BF)
        hid_ref[...] = hid
        h2_ref[...] = h1 + jnp.dot(hid, wd_ref[...], preferred_element_type=F32)

    ins = [h, s, w_out] + ([] if bias is None else [bias]) + [nf, wg, wu, wd]
    specs = ([_row(tm, D), _row(tm, D), _res((D, D))] + ([] if bias is None else [_lay(lw, (1, D))])
             + [_lay(l, (1, D)), _res((DFF, D)), _res((DFF, D)), _res((DFF, D))])
    return _call(
        body, f"mixer_ffn_fwd{l}", (t // tm,), specs,
        [_row(tm, D), _row(tm, D), _row(tm, DFF), _row(tm, DFF), _row(tm, DFF), _row(tm, D)],
        [_sds((t, D), F32), _sds((t, D), BF), _sds((t, DFF), BF), _sds((t, DFF), BF), _sds((t, DFF), BF), _sds((t, D), F32)],
        ins, gat=gat)


def _seg_rms(x, g, nseg):
    outs = []
    for s in range(nseg):
        xs = x[:, HD * s:HD * s + HD]
        outs.append(xs * _rstd(xs) * g)
    return jnp.concatenate(outs, axis=1)


def kv_fwd(h, kvn, w_kv, kng, tm):
    t = h.shape[0]

    def body(h_ref, g_ref, w_ref, kg_ref, kn_ref, kv_ref, k_ref, v_ref):
        x = h_ref[...]
        kn = (x * _rstd(x) * g_ref[...]).astype(BF)
        kn_ref[...] = kn
        kv = jnp.dot(kn, w_ref[...], preferred_element_type=F32)
        kv_ref[...] = kv
        k_ref[...] = _seg_rms(kv[:, :KVD], kg_ref[...], NKV).astype(BF)
        v_ref[...] = kv[:, KVD:].astype(BF)

    return pl.pallas_call(
        body, name="kv_fwd", grid=(t // tm,),
        in_specs=[_row(tm, D), _res((1, D)), _res((D, 2 * KVD)), _res((1, HD))],
        out_specs=[_row(tm, D), _row(tm, 2 * KVD), _row(tm, KVD), _row(tm, KVD)],
        out_shape=[_sds((t, D), BF), _sds((t, 2 * KVD), F32), _sds((t, KVD), BF), _sds((t, KVD), BF)],
        compiler_params=_cp(1),
    )(h, kvn, w_kv, kng)


def q_fwd(h, nm, l, w_q, j, tm):
    t = h.shape[0]

    def body(h_ref, g_ref, w_ref, u_ref, q_ref):
        x = h_ref[...]
        ub = (x * _rstd(x) * g_ref[...]).astype(BF)
        u_ref[...] = ub
        q_ref[...] = jnp.dot(ub, w_ref[...], preferred_element_type=F32)

    return pl.pallas_call(
        body, name=f"q_fwd{j}", grid=(t // tm,),
        in_specs=[_row(tm, D), _lay(l, (1, D)), _res((D, D))],
        out_specs=[_row(tm, D), _row(tm, D)], out_shape=[_sds((t, D), BF), _sds((t, D), F32)],
        compiler_params=_cp(1),
    )(h, nm, w_q)


RQ = NH // NKV


NKEYS = 2 * QB + NMETA


def _attn_mask(n, start):
    shape = (RQ * QB, NKEYS)
    qpos = n * QB + (lax.broadcasted_iota(jnp.int32, shape, 0) & (QB - 1))
    col = lax.broadcasted_iota(jnp.int32, shape, 1)
    in_band = col < 2 * QB
    kpos = jnp.where(in_band, start + col, col - 2 * QB)
    return (kpos <= qpos) & ((col >= 2 * QB) | ((qpos - kpos < QB) & (kpos >= NMETA)))


def _keys(ref, band, gs):
    return jnp.concatenate([ref[band, gs], ref[0:NMETA, gs]], axis=0)


def _keys_t(ref, band, gs):
    return jnp.concatenate([ref[gs, band], ref[gs, 0:NMETA]], axis=1)


def transpose_seq(a, name):
    bl, r, c = a.shape

    def body(a_ref, o_ref):
        o_ref[...] = a_ref[...].T

    return pl.pallas_call(
        body, name=name, grid=(bl,), in_specs=[pl.BlockSpec((None, r, c), lambda b: (b, 0, 0))],
        out_specs=pl.BlockSpec((None, c, r), lambda b: (b, 0, 0)), out_shape=_sds((bl, c, r), a.dtype), compiler_params=_cp(1),
    )(a)


def sum_transposed(a0, a1):
    bl, c, r = a0.shape

    def body(a0_ref, a1_ref, o_ref):
        o_ref[...] = (a0_ref[...] + a1_ref[...]).T

    spec = pl.BlockSpec((None, c, r), lambda b: (b, 0, 0))
    return pl.pallas_call(
        body, name="sum_transposed", grid=(bl,), in_specs=[spec, spec],
        out_specs=pl.BlockSpec((r, c), lambda b: (b, 0)), out_shape=_sds((bl * r, c), a0.dtype), compiler_params=_cp(1),
    )(a0, a1)


def _stack_heads(ref, g, fn):
    return jnp.concatenate([fn(ref[:, HD * (g * RQ + r):HD * (g * RQ + r) + HD]) for r in range(RQ)], axis=0)


def _stack_cols(ref, g):
    return jnp.concatenate([ref[:, g * RQ + r:g * RQ + r + 1] for r in range(RQ)], axis=0)


def _stack_sinks(sk_ref, g):
    return jnp.concatenate([jnp.broadcast_to(sk_ref[:, g * RQ + r:g * RQ + r + 1], (QB, 1)) for r in range(RQ)], axis=0)


def attn_fwd(q, kt, v, qg, sinks, j, bl, lp, gat):
    t = q.shape[0]
    nb = lp // QB

    def body(q_ref, kt_ref, v_ref, qg_ref, sk_ref, o_ref, lse_ref):
        n = pl.program_id(0)
        start = pl.multiple_of(jnp.maximum(n - 1, 0) * QB, QB)
        mask = _attn_mask(n, start)
        band = pl.ds(start, 2 * QB)
        lane = lax.broadcasted_iota(jnp.int32, (QB, NH), 1)
        ones = jnp.ones((NKEYS, HD), BF)
        pairs = [(b, g) for b in range(bl) for g in range(NKV)]
        gsl = [slice(HD * g, HD * g + HD) for g in range(NKV)]
        qns = [_stack_heads(q_ref.at[b], g, lambda x: (x * _rstd(x) * (qg_ref[...] * SCALE)).astype(BF)) for b, g in pairs]
        ss = [jnp.where(mask, _dot(qns[i], _keys_t(kt_ref.at[b], band, gsl[g])), NEG) for i, (b, g) in enumerate(pairs)]
        sinks = [_stack_sinks(sk_ref, g) for g in range(NKV)]
        mxs = [jnp.maximum(jnp.max(ss[i], -1, keepdims=True), sinks[g]) for i, (b, g) in enumerate(pairs)]
        oas = [_dot(jnp.exp(ss[i] - mxs[i]), jnp.concatenate([_keys(v_ref.at[b], band, gsl[g]), ones], axis=1))
               for i, (b, g) in enumerate(pairs)]
        lses = [jnp.zeros((QB, NH), F32) for _ in range(bl)]
        for i, (b, g) in enumerate(pairs):
            den = oas[i][:, HD:HD + 1] + jnp.exp(sinks[g] - mxs[i])
            o = oas[i][:, :HD] * (1.0 / den)
            l = mxs[i] + jnp.log(den)
            for r in range(RQ):
                h = g * RQ + r
                o_ref[b, :, HD * h:HD * h + HD] = o[r * QB:(r + 1) * QB].astype(BF)
                lses[b] = jnp.where(lane == h, l[r * QB:(r + 1) * QB], lses[b])
        for b in range(bl):
            lse_ref[b] = lses[b]

    blk = lambda c: pl.BlockSpec((bl, QB, c), lambda n: (0, n, 0))
    (o, lse), _, got = _call(
        body, f"attn_fwd{j}", (nb,),
        [blk(D), pl.BlockSpec((bl, KVD, lp), lambda n: (0, 0, 0)), pl.BlockSpec((bl, lp, KVD), lambda n: (0, 0, 0)),
         pl.BlockSpec((None, 1, HD), lambda n: (j, 0, 0)), pl.BlockSpec((None, 1, NH), lambda n: (j, 0, 0))],
        [blk(D), blk(NH)], [_sds((bl, lp, D), BF), _sds((bl, lp, NH), F32)], (q.reshape(bl, lp, D), kt, v, qg, sinks), gat=gat)
    return (o.reshape(t, D), lse.reshape(t, NH)), (), got


def loss_fwd(h, tgt):
    bl, lp, _ = h.shape
    seq = tgt.shape[1]
    cb = 256

    def body(h_ref, t_ref, dh_ref, loss_ref):
        _init(loss_ref, (pl.program_id(0) == 0) & (pl.program_id(1) == 0))
        err = h_ref[NMETA:NMETA + seq, :] - t_ref[...]
        dh_ref[...] = jnp.zeros_like(dh_ref)
        dh_ref[NMETA:NMETA + seq, :] = err * (1.0 / D)
        loss_ref[...] += (0.5 / D) * jnp.sum(err * err)

    return pl.pallas_call(
        body, name="loss_fwd", grid=(bl, D // cb),
        in_specs=[pl.BlockSpec((None, lp, cb), lambda b, c: (b, 0, c)), pl.BlockSpec((None, seq, cb), lambda b, c: (b, 0, c))],
        out_specs=[pl.BlockSpec((None, lp, cb), lambda b, c: (b, 0, c)), pl.BlockSpec((8, 128), lambda b, c: (0, 0))],
        out_shape=[_sds((bl, lp, D), F32), _sds((8, 128), F32)],
        compiler_params=_cp(2),
    )(h, tgt)


def ffn_bwd_x(dh2, g, up, h1, nf, l, wd, wg, wu, w_o, tm, xch):
    t = dh2.shape[0]

    def body(dh2_ref, g_ref, up_ref, h1_ref, nf_ref, wd_ref, wg_ref, wu_ref, *rest):
        if w_o is None:
            dg_ref, du_ref, dh1_ref, dnf_ref = rest
        else:
            wo_ref, dg_ref, du_ref, dh1_ref, dnf_ref, do_ref = rest
        _init(dnf_ref, pl.program_id(0) == 0)
        dh2v = dh2_ref[...]
        dhid = _dot_nt(dh2v, wd_ref[...])
        gv = g_ref[...].astype(F32)
        uv = up_ref[...].astype(F32)
        sg = _sig(gv)
        dgv = (dhid * uv * (sg * (1.0 + gv * (1.0 - sg)))).astype(BF)
        duv = (dhid * (gv * sg)).astype(BF)
        dg_ref[...] = dgv
        du_ref[...] = duv
        dnorm = _dot(dgv, wg_ref[...]) + _dot(duv, wu_ref[...])
        dx, dnf = _rms_bwd(h1_ref[...], nf_ref[...], dnorm)
        dh1 = dh2v + dx
        dh1_ref[...] = dh1
        dnf_ref[...] += dnf
        if w_o is not None:
            do_ref[...] = _dot_nt(dh1, wo_ref[...]).astype(BF)

    attn = w_o is not None
    return _call(
        body, f"ffn_bwd_x{l}", (t // tm,),
        [_row(tm, D), _row(tm, DFF), _row(tm, DFF), _row(tm, D), _lay(l, (1, D)),
         _res((DFF, D)), _res((DFF, D)), _res((DFF, D))] + ([_res((D, D))] if attn else []),
        [_row(tm, DFF), _row(tm, DFF), _row(tm, D), _acc((1, D))] + ([_row(tm, D)] if attn else []),
        [_sds((t, DFF), BF), _sds((t, DFF), BF), _sds((t, D), F32), _sds((1, D), F32)] + ([_sds((t, D), BF)] if attn else []),
        (dh2, g, up, h1, nf, wd, wg, wu) + ((w_o,) if attn else ()), xch=xch)


def mm_tn(x, dy, tm, name, split=False, transposed=False, xch=()):
    t, kk = x.shape
    nn = dy.shape[1]
    n8 = nn // NDEV
    nsteps = t // tm

    def body(x_ref, dy_ref, o_ref, acc):
        i = pl.program_id(0)
        _init(acc, i == 0)
        acc[...] += _dot_tn(x_ref[...], dy_ref[...])

        @pl.when(i == nsteps - 1)
        def _():
            if split:
                for p in range(NDEV):
                    o_ref[p] = acc[:, p * n8:(p + 1) * n8].astype(BF)
            elif transposed:
                o_ref[...] = acc[...].T.astype(BF)
            else:
                o_ref[...] = acc[...].astype(BF)

    oshape = (NDEV, kk, n8) if split else ((nn, kk) if transposed else (kk, nn))
    (out,), got, _ = _call(body, name, (nsteps,), [_row(tm, kk), _row(tm, nn)], [_acc(oshape)], [_sds(oshape, BF)], (x, dy),
                           scratch=[pltpu.VMEM((kk, nn), F32)], xch=xch)
    return out, got


def proj_bwd(dy, w, h, g, lg, dh_in, tm, name, xch=()):
    t = h.shape[0]
    nn = dy.shape[1]
    wspec = _res(w.shape)
    gspec = _res((1, D)) if lg is None else _lay(lg, (1, D))

    def body(dy_ref, w_ref, h_ref, g_ref, dhin_ref, dh_ref, dg_ref):
        _init(dg_ref, pl.program_id(0) == 0)
        du = _dot_nt(dy_ref[...], w_ref[...])
        dx, dg = _rms_bwd(h_ref[...], g_ref[...], du)
        dh_ref[...] = dhin_ref[...] + dx
        dg_ref[...] += dg

    return _call(body, name, (t // tm,), [_row(tm, nn), wspec, _row(tm, D), gspec, _row(tm, D)],
                 [_row(tm, D), _acc((1, D))], [_sds((t, D), F32), _sds((1, D), F32)], (dy, w, h, g, dh_in), xch=xch)


def attn_bwd(q, k, kt, vt, do, o, lse, qg, sinks, j, bl, lp, xch):
    t = q.shape[0]
    nb = lp // QB

    def body(q_ref, k_ref, kt_ref, vt_ref, do_ref, o_ref, lse_ref, qg_ref, sk_ref, dq_ref, dk_ref, dv_ref, dqg_ref, dsk_ref):
        n = pl.program_id(0)
        for ref in (dk_ref, dv_ref, dqg_ref, dsk_ref):
            _init(ref, n == 0)
        start = pl.multiple_of(jnp.maximum(n - 1, 0) * QB, QB)
        mask = _attn_mask(n, start)
        band = pl.ds(start, 2 * QB)
        lane = lax.broadcasted_iota(jnp.int32, (1, NH), 1)
        dqg = jnp.zeros((1, HD), F32)
        dsk = jnp.zeros((1, NH), F32)
        pairs = [(b, g) for b in range(bl) for g in range(NKV)]
        idx = range(len(pairs))
        gsl = [slice(HD * g, HD * g + HD) for g in range(NKV)]
        qhs = [_stack_heads(q_ref.at[b], g, lambda x: x) for b, g in pairs]
        rss = [_rstd(qhs[i]) for i in idx]
        qns = [(qhs[i] * rss[i] * (qg_ref[...] * SCALE)).astype(BF) for i in idx]
        lss = [_stack_cols(lse_ref.at[b], g) for b, g in pairs]
        dohs = [_stack_heads(do_ref.at[b], g, lambda x: x) for b, g in pairs]
        deltas = [jnp.sum(dohs[i].astype(F32) * _stack_heads(o_ref.at[b], g, lambda x: x).astype(F32), axis=-1, keepdims=True)
                  for i, (b, g) in enumerate(pairs)]
        prs = [jnp.where(mask, jnp.exp(_dot(qns[i], _keys_t(kt_ref.at[b], band, gsl[g])) - lss[i]), 0.0)
               for i, (b, g) in enumerate(pairs)]
        dss = [(prs[i] * (_dot(dohs[i], _keys_t(vt_ref.at[b], band, gsl[g])) - deltas[i])).astype(BF)
               for i, (b, g) in enumerate(pairs)]
        for i, (b, g) in enumerate(pairs):
            gs = gsl[g]
            dkt = _dot_tn(qns[i], dss[i])
            dvt = _dot_tn(dohs[i], prs[i])
            dk_ref[b, gs, band] += dkt[:, :2 * QB]
            dv_ref[b, gs, band] += dvt[:, :2 * QB]
            dk_ref[b, gs, 0:NMETA] += dkt[:, 2 * QB:]
            dv_ref[b, gs, 0:NMETA] += dvt[:, 2 * QB:]
        dqns = [_dot(dss[i], _keys(k_ref.at[b], band, gsl[g])) * SCALE for i, (b, g) in enumerate(pairs)]
        for i, (b, g) in enumerate(pairs):
            qh, rs, dqn = qhs[i], rss[i], dqns[i]
            dsink = jnp.exp(_stack_sinks(sk_ref, g) - lss[i]) * deltas[i]
            z = dqn * qg_ref[...]
            dq = rs * z - qh * (rs * rs * rs * jnp.mean(z * qh, axis=-1, keepdims=True))
            dqg = dqg + jnp.sum(dqn * qh * rs, axis=0, keepdims=True)
            for r in range(RQ):
                h = g * RQ + r
                dq_ref[b, :, HD * h:HD * h + HD] = dq[r * QB:(r + 1) * QB]
                dsk = dsk + jnp.where(lane == h, -jnp.sum(dsink[r * QB:(r + 1) * QB]), 0.0)
        dqg_ref[...] += dqg
        dsk_ref[...] += dsk

    blk = lambda c: pl.BlockSpec((bl, QB, c), lambda n: (0, n, 0))
    seq = pl.BlockSpec((bl, lp, KVD), lambda n: (0, 0, 0))
    seq_t = pl.BlockSpec((bl, KVD, lp), lambda n: (0, 0, 0))
    as3 = lambda a: a.reshape(bl, lp, a.shape[-1])
    (dq, dk, dv, dqg, dsk), got, _ = _call(
        body, f"attn_bwd{j}", (nb,),
        [blk(D), seq, seq_t, seq_t, blk(D), blk(D), blk(NH),
         pl.BlockSpec((None, 1, HD), lambda n: (j, 0, 0)), pl.BlockSpec((None, 1, NH), lambda n: (j, 0, 0))],
        [blk(D), seq_t, seq_t, pl.BlockSpec((1, HD), lambda n: (0, 0)), pl.BlockSpec((1, NH), lambda n: (0, 0))],
        [_sds((bl, lp, D), F32), _sds((bl, KVD, lp), F32), _sds((bl, KVD, lp), F32), _sds((1, HD), F32), _sds((1, NH), F32)],
        (as3(q), k, kt, vt, as3(do), as3(o), as3(lse), qg, sinks), xch=xch)
    return (dq.reshape(t, D), dk, dv, dqg, dsk), got, ()


def kv_bwd_pre(dk, dv, kv, kng, tm):
    t = kv.shape[0]

    def body(dk_ref, dv_ref, kv_ref, g_ref, dkv_ref, dg_ref):
        _init(dg_ref, pl.program_id(0) == 0)
        dg = jnp.zeros((1, HD), F32)
        outs = []
        for s in range(NKV):
            sl = slice(HD * s, HD * s + HD)
            dx, dgs = _rms_bwd(kv_ref[:, sl], g_ref[...], dk_ref[:, sl])
            outs.append(dx)
            dg = dg + dgs
        dkv_ref[:, :KVD] = jnp.concatenate(outs, axis=1).astype(BF)
        dkv_ref[:, KVD:] = dv_ref[...].astype(BF)
        dg_ref[...] += dg

    return pl.pallas_call(
        body, name="kv_bwd_pre", grid=(t // tm,),
        in_specs=[_row(tm, KVD)] * 2 + [_row(tm, 2 * KVD), _res((1, HD))],
        out_specs=[_row(tm, 2 * KVD), _acc((1, HD))], out_shape=[_sds((t, 2 * KVD), BF), _sds((1, HD), F32)],
        compiler_params=_cp(1),
    )(dk, dv, kv, kng)


def conv_out_bwd(dh1, c, ln_g, ln_b, w_out, i, tm, xch):
    t = dh1.shape[0]

    def body(dh1_ref, c_ref, g_ref, b_ref, w_ref, dc_ref, dg_ref, db_ref, dbo_ref):
        first = pl.program_id(0) == 0
        _init(dg_ref, first)
        _init(db_ref, first)
        _init(dbo_ref, first)
        dh1v = dh1_ref[...]
        ds = _dot_nt(dh1v, w_ref[...])
        cv = c_ref[...]
        xc = cv - jnp.mean(cv, axis=-1, keepdims=True)
        rstd = lax.rsqrt(jnp.mean(xc * xc, axis=-1, keepdims=True) + EPS)
        xh = xc * rstd
        n = xh * g_ref[...] + b_ref[...]
        sg = _sig(n)
        dn = ds * (sg * (1.0 + n * (1.0 - sg)))
        dxh = dn * g_ref[...]
        dc_ref[...] = rstd * (dxh - jnp.mean(dxh, axis=-1, keepdims=True) - xh * jnp.mean(dxh * xh, axis=-1, keepdims=True))
        dg_ref[...] += jnp.sum(dn * xh, axis=0, keepdims=True)
        db_ref[...] += jnp.sum(dn, axis=0, keepdims=True)
        dbo_ref[...] += jnp.sum(dh1v, axis=0, keepdims=True)

    return _call(
        body, f"conv_out_bwd{i}", (t // tm,), [_row(tm, D), _row(tm, D), _lay(i, (1, D)), _lay(i, (1, D)), _res((D, D))],
        [_row(tm, D), _acc((1, D)), _acc((1, D)), _acc((1, D))], [_sds((t, D), F32)] + [_sds((1, D), F32)] * 3,
        (dh1, c, ln_g, ln_b, w_out), xch=xch)


def conv_mid_bwd(dc, a, big, dw, i, tm, tpb, xch):
    t = dc.shape[0]
    nsteps = t // tm

    def body(dc_ref, nxt_ref, a_ref, prv_ref, big_ref, dw_ref, da_ref, dbin_ref, ddw_ref, dce, ae, wacc, bacc, taps):
        i_ = pl.program_id(0)
        _spread_taps(dw_ref, taps)
        _init(wacc, i_ == 0)
        _init(bacc, i_ == 0)
        dce[0:tm] = dc_ref[...]
        dce[tm:] = jnp.where(i_ % tpb == tpb - 1, 0.0, nxt_ref[...])
        ae[0:HALO] = jnp.where(i_ % tpb == 0, 0.0, prv_ref[...])
        ae[HALO:] = a_ref[...]

        def chunk(k, carry):
            r0 = pl.multiple_of(k * CHUNK, CHUNK)
            wdc = _shifted(dce[pl.ds(r0, 2 * CHUNK), :])
            wa = _shifted(ae[pl.ds(r0, 2 * CHUNK), :])
            dcc = wdc[0][0:CHUNK]
            da = jnp.zeros((CHUNK, D), F32)
            for j in range(CW):
                da = da + _tap_weight(taps, j) * _tap(wdc, CW - 1 - j)
                wacc[j] += _fold8(dcc * _tap(wa, j + 2))
            bv = big_ref[pl.ds(r0, CHUNK), :].astype(F32)
            a1, sg = bv[:, :D], _sig(bv[:, D:])
            d1 = da * sg
            d2 = da * a1 * sg * (1.0 - sg)
            da_ref[pl.ds(r0, CHUNK), 0:D] = d1.astype(BF)
            da_ref[pl.ds(r0, CHUNK), D:2 * D] = d2.astype(BF)
            bacc[:, 0:D] += _fold8(d1)
            bacc[:, D:2 * D] += _fold8(d2)
            return carry

        lax.fori_loop(0, tm // CHUNK, chunk, 0)

        @pl.when(i_ == nsteps - 1)
        def _():
            dbin_ref[...] = jnp.sum(bacc[...], axis=0, keepdims=True)
            ddw_ref[...] = jnp.sum(wacc[...], axis=1)

    return _call(
        body, f"conv_mid_bwd{i}", (nsteps,),
        [_row(tm, D), _next_halo(tm, t), _row(tm, D), _prev_halo(tm), _row(tm, 2 * D), _lay(i, (CW, D))],
        [_row(tm, 2 * D), _acc((1, 2 * D)), _acc((CW + 1, D))],
        [_sds((t, 2 * D), BF), _sds((1, 2 * D), F32), _sds((CW + 1, D), F32)],
        (dc, dc, a, a, big, dw),
        scratch=[pltpu.VMEM((tm + HALO, D), F32), pltpu.VMEM((tm + HALO, D), F32),
                 pltpu.VMEM((CW + 1, 8, D), F32), pltpu.VMEM((8, 2 * D), F32), pltpu.VMEM((CW, 8, D), F32)], xch=xch)


def input_grads(dh0, seq):
    bl, lp, _ = dh0.shape
    cb = 256

    def body(dh_ref, gx_ref, gm_ref):
        _init(gm_ref, pl.program_id(1) == 0)
        gx_ref[...] = dh_ref[NMETA:NMETA + seq, :]
        gm_ref[...] += dh_ref[0:NMETA, :]

    return pl.pallas_call(
        body, name="input_grads", grid=(D // cb, bl),
        in_specs=[pl.BlockSpec((None, lp, cb), lambda c, b: (b, 0, c))],
        out_specs=[pl.BlockSpec((None, seq, cb), lambda c, b: (b, 0, c)), pl.BlockSpec((NMETA, cb), lambda c, b: (0, c))],
        out_shape=[_sds((bl, seq, D), F32), _sds((NMETA, D), F32)],
        compiler_params=_cp(2),
    )(dh0)


GATHER_PLAN = {
    "embed": [("conv_w_in", 0), ("conv_w_out", 0)],
    "conv_mid_fwd0": [("ffn_w_gate", 0), ("ffn_w_up", 0), ("ffn_w_down", 0)],
    "mixer_ffn_fwd0": [("conv_w_in", 1), ("conv_w_out", 1), ("ffn_w_gate", 1)],
    "conv_mid_fwd1": [("ffn_w_up", 1), ("ffn_w_down", 1), ("w_kv", 0), ("w_q", 0)],
    "mixer_ffn_fwd1": [("w_o", 0), ("ffn_w_down", 2)],
    "attn_fwd0": [("ffn_w_gate", 2), ("ffn_w_up", 2), ("w_q", 1), ("w_o", 1)],
    "attn_fwd1": [("ffn_w_gate", 3), ("ffn_w_up", 3), ("ffn_w_down", 3)],
}
EXCHANGE_PLAN = {
    "attn_bwd1": [("ffn_w_down3", ALL), ("ffn_w_gate3", ALL)],
    "dw_down2": [("w_o1", ALL)],
    "ffn_bwd_x2": [("ffn_w_up3", ALL), ("w_q1", ALL)],
    "attn_bwd0": [("ffn_w_down2", ALL), ("ffn_w_gate2", ALL)],
    "dw_down1": [("w_o0", ALL), ("w_q0", H1)],
    "ffn_bwd_x1": [("ffn_w_up2", ALL), ("w_q0", H2), ("w_kv", ALL)],
    "dw_gate1": [("ffn_w_down1", H1)],
    "dw_up1": [("ffn_w_down1", H2)],
    "conv_mid_bwd1": [("ffn_w_gate1", ALL), ("ffn_w_up1", H1)],
    "conv_in_bwd1": [("ffn_w_up1", H2)],
    "dw_down0": [("conv_w_out1", ALL)],
    "ffn_bwd_x0": [("conv_w_in1", ALL)],
    "dw_gate0": [("ffn_w_down0", H1)],
    "dw_up0": [("ffn_w_down0", H2)],
    "conv_out_bwd0": [("ffn_w_gate0", H1)],
    "conv_mid_bwd0": [("ffn_w_gate0", H2), ("ffn_w_up0", H1), ("conv_w_out0", ALL)],
    "dw_conv_in0": [("ffn_w_up0", H2)],
    "conv_in_bwd0": [("conv_w_in0", H1)],
    "tail": [("conv_w_in0", H2)],
}
BIG = {"conv_w_in": "pieces", "conv_w_out": "rows", "w_kv": "rows", "w_q": "rows", "w_o": "rows",
       "ffn_w_gate": "rows", "ffn_w_up": "rows", "ffn_w_down": "rows"}
EXCHANGE_KIND = BIG
TRANSPOSED = ("ffn_w_gate", "ffn_w_up")


def gathered_matrix(name, layer, blocks8):
    if BIG[name] == "rows":
        return blocks8.reshape(NDEV * blocks8.shape[1], blocks8.shape[2])
    return join_columns(blocks8, f"join_{name}{layer}")


def local_step(x, tgt, meta8, w, shards):
    bl, seq, _ = x.shape
    lp = -(-(NMETA + seq) // QB) * QB
    tpb = 4
    tm = lp // tpb
    t = bl * lp
    na = 2
    flat = lambda a: a.reshape(t, D)
    mats = {}

    def riders(carrier):
        return [shards[key] for key in GATHER_PLAN[carrier]]

    def landed(carrier, blocks):
        for key, b8 in zip(GATHER_PLAN[carrier], blocks):
            mats[key] = gathered_matrix(*key, b8)

    h0, got = embed(x, meta8, lp, riders("embed"))
    landed("embed", got)
    h = flat(h0)
    saved = []
    kvs = None
    for l in range(4):
        rec = {"h": h}
        if l < na:
            rec["u"], rec["big"], rec["a"] = conv_in_fwd(h, w["norm_mix"], l, mats["conv_w_in", l], w["conv_b_in"], l, tm)
            name = f"conv_mid_fwd{l}"
            (rec["c"], rec["s"]), _, got = conv_mid_fwd(rec["a"], w["conv_dw"], w["conv_ln_g"], w["conv_ln_b"], l, tm, tpb,
                                                         riders(name))
            landed(name, got)
            mixed, w_out, lw, bias = rec["s"], mats["conv_w_out", l], l, w["conv_b_out"]
        else:
            j = l - na
            if kvs is None:
                kvs = dict(zip(("kn", "kv", "k", "v"), kv_fwd(h, w["kv_norm"], mats["w_kv", 0], w["k_norm"], tm)))
                kvs["h"] = h
                kvs["k3"], kvs["v3"] = kvs["k"].reshape(bl, lp, KVD), kvs["v"].reshape(bl, lp, KVD)
                kvs["kt"], kvs["vt"] = transpose_seq(kvs["k3"], "transpose_k"), transpose_seq(kvs["v3"], "transpose_v")
            rec["u"], rec["q"] = q_fwd(h, w["norm_mix"], l, mats["w_q", j], j, tm)
            name = f"attn_fwd{j}"
            (rec["o"], rec["lse"]), _, got = attn_fwd(rec["q"], kvs["kt"], kvs["v3"], w["q_norm"], w["attn_sinks"], j, bl, lp,
                                                      riders(name) if name in GATHER_PLAN else [])
            if name in GATHER_PLAN:
                landed(name, got)
            mixed, w_out, lw, bias = rec["o"], mats["w_o", j], j, None
        name = f"mixer_ffn_fwd{l}"
        (rec["h1"], rec["u2"], rec["g"], rec["up"], rec["hid"], h), _, got = mixer_ffn_fwd(
            h, mixed, w_out, lw, bias, w["norm_ffn"], l, mats["ffn_w_gate", l], mats["ffn_w_up", l], mats["ffn_w_down", l], tm // 2,
            riders(name) if name in GATHER_PLAN else [])
        if name in GATHER_PLAN:
            landed(name, got)
        saved.append(rec)

    dh3, loss_blk = loss_fwd(h.reshape(bl, lp, D), tgt)
    dh = flat(dh3)

    big, small, arrived = {}, {}, {}
    dks, dvs = [], []

    def ride(kernel_name):
        return [(big[nm], EXCHANGE_KIND[nm.rstrip("0123456789")], ks) for nm, ks in EXCHANGE_PLAN.get(kernel_name, [])]

    def landed_x(kernel_name, arrivals):
        for (nm, _), got in zip(EXCHANGE_PLAN.get(kernel_name, []), arrivals):
            arrived.setdefault(nm, []).append(got)

    def dw(name, grad, x, dy, **kw):
        big[grad], got = mm_tn(x, dy, 2 * tm, name, xch=ride(name), **kw)
        landed_x(name, got)

    for l in reversed(range(4)):
        rec = saved[l]
        dw(f"dw_down{l}", f"ffn_w_down{l}", rec["hid"], dh)
        name = f"ffn_bwd_x{l}"
        outs, got, _ = ffn_bwd_x(
            dh, rec["g"], rec["up"], rec["h1"], w["norm_ffn"], l, mats["ffn_w_down", l], mats["ffn_w_gate", l], mats["ffn_w_up", l],
            mats["w_o", l - na] if l >= na else None, tm // 2, ride(name))
        landed_x(name, got)
        dg, du, dh1, small[f"norm_ffn{l}"] = outs[:4]
        dw(f"dw_gate{l}", f"ffn_w_gate{l}", rec["u2"], dg, transposed=True)
        dw(f"dw_up{l}", f"ffn_w_up{l}", rec["u2"], du, transposed=True)
        if l >= na:
            j = l - na
            dw(f"dw_o{j}", f"w_o{j}", rec["o"], dh1)
            name = f"attn_bwd{j}"
            (dq, dk, dv, small[f"q_norm{j}"], small[f"attn_sinks{j}"]), got, _ = attn_bwd(
                rec["q"], kvs["k3"], kvs["kt"], kvs["vt"], outs[4], rec["o"], rec["lse"], w["q_norm"], w["attn_sinks"], j, bl, lp,
                ride(name))
            landed_x(name, got)
            dks.append(dk)
            dvs.append(dv)
            dw(f"dw_q{j}", f"w_q{j}", rec["u"], dq)
            dh, small[f"norm_mix{l}"] = proj_bwd(dq, mats["w_q", j], rec["h"], w["norm_mix"], l, dh1, tm, f"q_bwd{j}")[0]
            if l == na:
                dkv, small["k_norm"] = kv_bwd_pre(sum_transposed(*dks), sum_transposed(*dvs), kvs["kv"], w["k_norm"], tm)
                dw("dw_kv", "w_kv", kvs["kn"], dkv)
                dh, small["kv_norm"] = proj_bwd(dkv, mats["w_kv", 0], kvs["h"], w["kv_norm"], None, dh, tm, "kv_bwd")[0]
        else:
            name = f"conv_out_bwd{l}"
            (dc, small[f"conv_ln_g{l}"], small[f"conv_ln_b{l}"], small[f"conv_b_out{l}"]), got, _ = conv_out_bwd(
                dh1, rec["c"], w["conv_ln_g"], w["conv_ln_b"], mats["conv_w_out", l], l, tm, ride(name))
            landed_x(name, got)
            dw(f"dw_conv_out{l}", f"conv_w_out{l}", rec["s"], dh1)
            name = f"conv_mid_bwd{l}"
            (da, small[f"conv_b_in{l}"], small[f"conv_dw{l}"]), got, _ = conv_mid_bwd(
                dc, rec["a"], rec["big"], w["conv_dw"], l, tm, tpb, ride(name))
            landed_x(name, got)
            dw(f"dw_conv_in{l}", f"conv_w_in{l}", rec["u"], da, split=True)
            name = f"conv_in_bwd{l}"
            (dh, small[f"norm_mix{l}"]), got, _ = proj_bwd(da, mats["conv_w_in", l], rec["h"], w["norm_mix"], l, dh1, tm, name,
                                                           ride(name))
            landed_x(name, got)
    grad_x, small["meta_tokens"] = input_grads(dh.reshape(bl, lp, D), seq)
    return loss_blk, grad_x, big, arrived, small


def all_gather_blocks(blocks):
    n = len(blocks)

    def body(*refs):
        srcs, outs, sems = refs[:n], refs[n:2 * n], refs[2 * n:]
        _gat_start(srcs, outs, sems)
        _gat_forward(srcs, outs, sems)
        _gat_wait(srcs, outs, sems)

    any_spec = pl.BlockSpec(memory_space=pl.ANY)
    return pl.pallas_call(
        body, name="all_gather_blocks", out_shape=[_sds((NDEV,) + tuple(a.shape), a.dtype) for a in blocks],
        in_specs=[any_spec] * n, out_specs=[any_spec] * n, scratch_shapes=_xch_scratch(n),
    )(*blocks)


def cast_bf16(ws):
    n = len(ws)
    counts = [1 if x.ndim == 2 else x.shape[0] for x in ws]

    def body(*refs):
        outs = iter(refs[n:])
        for a in range(n):
            for l in range(counts[a]):
                next(outs)[...] = (refs[a][...] if ws[a].ndim == 2 else refs[a][l]).astype(BF)

    flat = pl.pallas_call(
        body, name="cast_bf16", out_shape=[_sds(x.shape[-2:], BF) for x, k in zip(ws, counts) for _ in range(k)],
        compiler_params=pltpu.CompilerParams(vmem_limit_bytes=VMEM_LIMIT),
    )(*ws)
    it = iter(flat)
    return [[next(it) for _ in range(k)] for k in counts]


def join_columns(w8, name):
    _, kk, n8 = w8.shape

    def body(x_ref, o_ref):
        o_ref[...] = jnp.concatenate([x_ref[p] for p in range(NDEV)], axis=1)

    return pl.pallas_call(body, name=name, out_shape=_sds((kk, NDEV * n8), w8.dtype),
                          compiler_params=pltpu.CompilerParams(vmem_limit_bytes=VMEM_LIMIT))(w8)


def _adamw_math(w, m, v, g):
    m2 = B1 * m + (1.0 - B1) * g
    v2 = B2 * v + (1.0 - B2) * (g * g)
    mh = m2 / (1.0 - B1 ** STEP)
    vh = v2 / (1.0 - B2 ** STEP)
    return -LR * (mh / (jnp.sqrt(vh) + AEPS) + WD * w), m2, v2


def adamw_big(w, m, v, parts, name, xch=(), gat=()):
    lyr, r, c = w.shape
    by_cols = c >= 512
    blk = (lyr, r, 256) if by_cols else (lyr, 256 if r % 256 == 0 else r, c)
    imap = (lambda i: (0, 0, i)) if by_cols else (lambda i: (0, i, 0))
    counts = [len(per_layer) for per_layer in parts]

    def body(w_ref, m_ref, v_ref, *rest):
        p_refs, (g_ref, d_ref, m2_ref, v2_ref) = iter(rest[:sum(counts)]), rest[sum(counts):]
        for l in range(lyr):
            g = None
            for _ in range(counts[l]):
                ref = next(p_refs)
                for q in range(ref.shape[0]):
                    g = ref[q].astype(F32) if g is None else g + ref[q].astype(F32)
            g_ref[l] = g
            d_ref[l], m2_ref[l], v2_ref[l] = _adamw_math(w_ref[l], m_ref[l], v_ref[l], g)

    spec = pl.BlockSpec(blk, imap)
    flat = [a for per_layer in parts for a in per_layer]
    pspecs = [pl.BlockSpec((a.shape[0],) + blk[1:], imap) for a in flat]
    return _call(body, name, ((c // 256) if by_cols else (r // blk[1]),), [spec, spec, spec] + pspecs,
                 [spec] * 4, [_sds((lyr, r, c), F32)] * 4, (w, m, v, *flat), xch=xch, gat=gat)


SMALL_ROWS = 104
REPLICATED = {"norm_mix": (0, 4, D), "norm_ffn": (4, 4, D), "kv_norm": (8, 1, D), "k_norm": (9, 1, HD), "q_norm": (10, 2, HD),
              "attn_sinks": (12, 2, NH)}
LOSS_ROW = 14
SHARDED = {"meta_tokens": (16, NMETA), "conv_b_in": (32, 4), "conv_dw": (36, 2 * CW), "conv_ln_g": (98, 2), "conv_ln_b": (100, 2),
           "conv_b_out": (102, 2)}


def pack_small(gs, loss_blk):
    order = ([f"norm_mix{l}" for l in range(4)] + [f"norm_ffn{l}" for l in range(4)] + ["kv_norm", "k_norm", "q_norm0", "q_norm1",
             "attn_sinks0", "attn_sinks1", "meta_tokens", "conv_b_in0", "conv_b_in1", "conv_dw0", "conv_dw1", "conv_ln_g0",
             "conv_ln_g1", "conv_ln_b0", "conv_ln_b1", "conv_b_out0", "conv_b_out1"])

    def body(*refs):
        r = dict(zip(order, refs))
        loss_ref, o_ref = refs[len(order)], refs[len(order) + 1]
        o_ref[...] = jnp.zeros_like(o_ref)
        for l in range(4):
            o_ref[l:l + 1, :] = r[f"norm_mix{l}"][...]
            o_ref[4 + l:5 + l, :] = r[f"norm_ffn{l}"][...]
        o_ref[8:9, :] = r["kv_norm"][...]
        o_ref[9:10, 0:HD] = r["k_norm"][...]
        for j in range(2):
            o_ref[10 + j:11 + j, 0:HD] = r[f"q_norm{j}"][...]
            o_ref[12 + j:13 + j, 0:NH] = r[f"attn_sinks{j}"][...]
            o_ref[32 + 2 * j:33 + 2 * j, :] = r[f"conv_b_in{j}"][:, 0:D]
            o_ref[33 + 2 * j:34 + 2 * j, :] = r[f"conv_b_in{j}"][:, D:2 * D]
            o_ref[36 + CW * j:36 + CW * (j + 1), :] = r[f"conv_dw{j}"][0:CW, :]
            o_ref[98 + j:99 + j, :] = r[f"conv_ln_g{j}"][...]
            o_ref[100 + j:101 + j, :] = r[f"conv_ln_b{j}"][...]
            o_ref[102 + j:103 + j, :] = r[f"conv_b_out{j}"][...]
        o_ref[LOSS_ROW:LOSS_ROW + 1, 0:1] = loss_ref[0:1, 0:1]
        o_ref[16:16 + NMETA, :] = r["meta_tokens"][...]

    return pl.pallas_call(body, name="pack_small", out_shape=_sds((SMALL_ROWS, D), F32))(*[gs[k] for k in order], loss_blk)


def adamw_small(g8, wts, mom, var):
    names = list(REPLICATED) + list(SHARDED)
    shape2 = {"kv_norm": (1, D), "k_norm": (1, HD)}
    ins = [a[k].reshape(shape2.get(k, a[k].shape)) for a in (wts, mom, var) for k in names]
    n = len(names)

    def body(*refs):
        g8_ref, w_refs, m_refs, v_refs = refs[0], refs[1:1 + n], refs[1 + n:1 + 2 * n], refs[1 + 2 * n:1 + 3 * n]
        loss_ref, outs, red_ref = refs[1 + 3 * n], refs[2 + 3 * n:-1], refs[-1]
        me = _my_index()
        acc = g8_ref[0]
        for q in range(1, NDEV):
            acc = acc + g8_ref[q]
        red_ref[...] = acc
        loss_ref[...] = red_ref[LOSS_ROW:LOSS_ROW + 1, 0:1]

        def mine(rows, width):
            acc = jnp.zeros((rows.stop - rows.start, width), F32)
            for p_ in range(NDEV):
                acc = acc + jnp.where(me == p_, red_ref[rows, p_ * width:(p_ + 1) * width], 0.0)
            return acc

        for i, k in enumerate(names):
            if k in REPLICATED:
                r0, nr, width = REPLICATED[k]
                g = red_ref[r0:r0 + nr, 0:width]
            elif k == "conv_b_in":
                half = D // (2 * D // NDEV)
                acc = jnp.zeros((2, 2 * D // NDEV), F32)
                for p_ in range(NDEV):
                    c0 = (p_ % half) * (2 * D // NDEV)
                    part = jnp.concatenate([red_ref[32 + 2 * j + p_ // half:33 + 2 * j + p_ // half, c0:c0 + 2 * D // NDEV]
                                            for j in range(2)], axis=0)
                    acc = acc + jnp.where(me == p_, part, 0.0)
                g = acc
            else:
                r0, nr = SHARDED[k]
                g = mine(slice(r0, r0 + nr), D // NDEV)
            w_, m_, v_ = w_refs[i], m_refs[i], v_refs[i]
            g_out, d_out, m_out, v_out = outs[4 * i:4 * i + 4]
            if k == "conv_dw":
                for j in range(2):
                    gj = g[CW * j:CW * (j + 1)]
                    g_out[j] = gj
                    d_out[j], m_out[j], v_out[j] = _adamw_math(w_[j], m_[j], v_[j], gj)
            else:
                g_out[...] = g
                d_out[...], m_out[...], v_out[...] = _adamw_math(w_[...], m_[...], v_[...], g)

    out_shape = [_sds((1, 1), F32)] + [_sds(ins[i].shape, F32) for i in range(n) for _ in range(4)]
    res = pl.pallas_call(body, name="adamw_small", out_shape=out_shape, scratch_shapes=[pltpu.VMEM((SMALL_ROWS, D), F32)])(g8, *ins)
    out = {k: tuple(o.reshape(wts[k].shape) for o in res[1 + 4 * i:5 + 4 * i]) for i, k in enumerate(names)}
    return res[0], out


NAMES = ["meta_tokens", "norm_mix", "norm_ffn", "conv_w_in", "conv_b_in", "conv_dw", "conv_ln_g", "conv_ln_b", "conv_w_out",
         "conv_b_out", "kv_norm", "w_kv", "k_norm", "w_q", "q_norm", "attn_sinks", "w_o", "ffn_w_gate", "ffn_w_up", "ffn_w_down"]


def kernel(x, meta_tokens, norm_mix, norm_ffn, conv_w_in, conv_b_in, conv_dw, conv_ln_g, conv_ln_b, conv_w_out, conv_b_out, kv_norm, w_kv, k_norm, w_q, q_norm, attn_sinks, w_o, ffn_w_gate, ffn_w_up, ffn_w_down, loss_target, m_meta_tokens, m_norm_mix, m_norm_ffn, m_conv_w_in, m_conv_b_in, m_conv_dw, m_conv_ln_g, m_conv_ln_b, m_conv_w_out, m_conv_b_out, m_kv_norm, m_w_kv, m_k_norm, m_w_q, m_q_norm, m_attn_sinks, m_w_o, m_ffn_w_gate, m_ffn_w_up, m_ffn_w_down, v_meta_tokens, v_norm_mix, v_norm_ffn, v_conv_w_in, v_conv_b_in, v_conv_dw, v_conv_ln_g, v_conv_ln_b, v_conv_w_out, v_conv_b_out, v_kv_norm, v_w_kv, v_k_norm, v_w_q, v_q_norm, v_attn_sinks, v_w_o, v_ffn_w_gate, v_ffn_w_up, v_ffn_w_down):
    wts = dict(zip(NAMES, (meta_tokens, norm_mix, norm_ffn, conv_w_in, conv_b_in, conv_dw, conv_ln_g, conv_ln_b, conv_w_out,
                           conv_b_out, kv_norm, w_kv, k_norm, w_q, q_norm, attn_sinks, w_o, ffn_w_gate, ffn_w_up, ffn_w_down)))
    mom = dict(zip(NAMES, (m_meta_tokens, m_norm_mix, m_norm_ffn, m_conv_w_in, m_conv_b_in, m_conv_dw, m_conv_ln_g, m_conv_ln_b,
                           m_conv_w_out, m_conv_b_out, m_kv_norm, m_w_kv, m_k_norm, m_w_q, m_q_norm, m_attn_sinks, m_w_o,
                           m_ffn_w_gate, m_ffn_w_up, m_ffn_w_down)))
    var = dict(zip(NAMES, (v_meta_tokens, v_norm_mix, v_norm_ffn, v_conv_w_in, v_conv_b_in, v_conv_dw, v_conv_ln_g, v_conv_ln_b,
                           v_conv_w_out, v_conv_b_out, v_kv_norm, v_w_kv, v_k_norm, v_w_q, v_q_norm, v_attn_sinks, v_w_o,
                           v_ffn_w_gate, v_ffn_w_up, v_ffn_w_down)))
    for k in TRANSPOSED:
        wts[k], mom[k], var[k] = (jnp.swapaxes(a, 1, 2) for a in (wts[k], mom[k], var[k]))

    big_names = list(BIG)
    layers = cast_bf16([wts[k] for k in big_names])
    shards = {(k, l): blk for k, per_layer in zip(big_names, layers) for l, blk in enumerate(per_layer)}
    vec_names = ["meta_tokens", "conv_b_in", "conv_dw", "conv_ln_g", "conv_ln_b", "conv_b_out"]
    full = dict(zip(vec_names, all_gather_blocks([wts[k] for k in vec_names])))
    join_vec = lambda a: jnp.moveaxis(a, 0, -2).reshape(a.shape[1:-1] + (NDEV * a.shape[-1],))
    w = {}
    w["conv_b_in"] = join_vec(full["conv_b_in"]).reshape(2, 1, 2 * D)
    w["conv_dw"] = join_vec(full["conv_dw"])
    for k in ("conv_ln_g", "conv_ln_b", "conv_b_out"):
        w[k] = join_vec(full[k]).reshape(2, 1, D)
    w["norm_mix"] = norm_mix.reshape(4, 1, D)
    w["norm_ffn"] = norm_ffn.reshape(4, 1, D)
    w["kv_norm"] = kv_norm.reshape(1, D)
    w["k_norm"] = k_norm.reshape(1, HD)
    w["q_norm"] = q_norm.reshape(2, 1, HD)
    w["attn_sinks"] = attn_sinks.reshape(2, 1, NH)

    loss_blk, grad_x, gbig, arrived, gs = local_step(x, loss_target, full["meta_tokens"], w, shards)

    packed = pack_small(gs, loss_blk)

    grads, delta, new_m, new_v = {}, {}, {}, {}
    tail = EXCHANGE_PLAN["tail"]
    waiting = {nm.rstrip("0123456789") for nm, _ in tail}
    order = sorted([k for k in big_names if k not in waiting], key=lambda k: -wts[k].size) + [k for k in big_names if k in waiting]
    small8 = None
    for pos, k in enumerate(order):
        flat2 = wts[k].ndim == 2
        as3 = (lambda a: a[None]) if flat2 else (lambda a: a)
        riders = tail if pos == 0 else []
        gat = [packed] if pos == 1 else []
        parts = [arrived[k]] if flat2 else [arrived[f"{k}{i}"] for i in range(wts[k].shape[0])]
        outs, got_x, got_g = adamw_big(as3(wts[k]), as3(mom[k]), as3(var[k]), parts, "adamw_" + k,
                                       xch=[(gbig[nm], EXCHANGE_KIND[nm.rstrip("0123456789")], ks) for nm, ks in riders], gat=gat)
        for (nm, _), got in zip(riders, got_x):
            arrived[nm].append(got)
        if gat:
            small8 = got_g[0]
        grads[k], delta[k], new_m[k], new_v[k] = [o[0] if flat2 else (jnp.swapaxes(o, 1, 2) if k in TRANSPOSED else o) for o in outs]
    loss, small = adamw_small(small8, wts, mom, var)
    for k, (g_, d_, m_, v_) in small.items():
        grads[k], delta[k], new_m[k], new_v[k] = g_, d_, m_, v_
    return (loss.reshape(()), grad_x, *[grads[k] for k in NAMES], *[delta[k] for k in NAMES], *[new_m[k] for k in NAMES],
            *[new_v[k] for k in NAMES])
```

```python
import jax
import jax.numpy as jnp
from jax import lax
from jax.experimental import pallas as pl
from jax.experimental.pallas import tpu as pltpu

F32 = jnp.float32
BF = jnp.bfloat16

D = 1024
DFF = 2816
NH = 16
NKV = 4
HD = 64
KVD = NKV * HD
NMETA = 16
CW = 31
HALO = 32
CHUNK = 32
QB = 128
EPS = 1e-6
NEG = -1e30
NDEV = 8
SCALE = HD ** -0.5

LR, B1, B2, AEPS, WD, STEP = 0.001, 0.9, 0.999, 1e-08, 0.01, 10

VMEM_LIMIT = 56 * 2 ** 20
MESH = pl.DeviceIdType.MESH


def _cp(n):
    return pltpu.CompilerParams(dimension_semantics=("arbitrary",) * n, vmem_limit_bytes=VMEM_LIMIT)


def _row(tm, c):
    return pl.BlockSpec((tm, c), lambda i: (i, 0))


def _res(shape):
    return pl.BlockSpec(shape, lambda i: (0,) * len(shape), pipeline_mode=pl.Buffered(1))


def _lay(l, shape):
    return pl.BlockSpec((None,) + tuple(shape), lambda i: (l,) + (0,) * len(shape), pipeline_mode=pl.Buffered(1))


def _acc(shape):
    return pl.BlockSpec(shape, lambda i: (0,) * len(shape))


def _sds(shape, dt):
    return jax.ShapeDtypeStruct(tuple(shape), dt)


def _dot(a, b):
    return jnp.dot(a.astype(BF), b.astype(BF), preferred_element_type=F32)


def _dot_nt(a, b):
    return lax.dot_general(a.astype(BF), b.astype(BF), (((1,), (1,)), ((), ())), preferred_element_type=F32)


def _dot_tn(a, b):
    return lax.dot_general(a.astype(BF), b.astype(BF), (((0,), (0,)), ((), ())), preferred_element_type=F32)


def _rstd(x):
    return lax.rsqrt(jnp.mean(x * x, axis=-1, keepdims=True) + EPS)


def _rms_bwd(x, g, dy):
    r = _rstd(x)
    z = dy * g
    dx = r * z - x * (r * r * r * jnp.mean(z * x, axis=-1, keepdims=True))
    return dx, jnp.sum(dy * x * r, axis=0, keepdims=True)


def _sig(x):
    return jax.nn.sigmoid(x)


def _fold8(x):
    out = x[0:8]
    for k in range(1, x.shape[0] // 8):
        out = out + x[8 * k:8 * k + 8]
    return out


def _shifted(win):
    return [win] + [pltpu.roll(win, 2 * CHUNK - rho, 0) for rho in range(1, 8)]


def _tap(phases, o):
    return phases[o % 8][8 * (o // 8):8 * (o // 8) + CHUNK]


def _spread_taps(dw_ref, taps):
    @pl.when(pl.program_id(0) == 0)
    def _():
        for j in range(CW):
            taps[j] = jnp.broadcast_to(dw_ref[j:j + 1, :], taps.shape[1:])


def _tap_weight(taps, j):
    return jnp.concatenate([taps[j]] * (CHUNK // 8), axis=0)


def _init(ref, first):
    @pl.when(first)
    def _():
        ref[...] = jnp.zeros_like(ref)


def _my_index():
    return 4 * lax.axis_index("x") + 2 * lax.axis_index("y") + lax.axis_index("c")


def _coords(idx):
    return (idx // 4, (idx // 2) % 2, idx % 2)


ALL = tuple(range(NDEV))
H1, H2 = (0, 1, 2, 4, 6), (3, 5, 7)


def _xch_shapes(xch):
    return [_sds((len(ks),) + ((a.shape[0] // NDEV, a.shape[1]) if k == "rows" else tuple(a.shape[1:])), a.dtype) for a, k, ks in xch]


def _xch_scratch(n):
    return [pltpu.SemaphoreType.DMA((n, NDEV)), pltpu.SemaphoreType.DMA((n, NDEV)), pltpu.SemaphoreType.DMA((n,))]


def _xch_copies(meta, srcs, outs, sems, arrivals):
    send_sems, recv_sems, local_sems = sems
    me = _my_index()

    def piece(a, p):
        if meta[a][0] == "rows":
            r = srcs[a].shape[0] // NDEV
            return srcs[a].at[pl.ds(p * r, r), :]
        return srcs[a].at[p]

    def remote(a, i, k, src):
        return pltpu.make_async_remote_copy(
            src_ref=src, dst_ref=outs[a].at[i], send_sem=send_sems.at[a, k], recv_sem=recv_sems.at[a, k],
            device_id=_coords(me ^ k), device_id_type=MESH)

    local, sends, recvs = [], [], []
    for a, (_, ks) in enumerate(meta):
        for i, k in enumerate(ks):
            if k == 0:
                local.append(pltpu.make_async_copy(piece(a, me), outs[a].at[i], local_sems.at[a]))
            else:
                sends.append(remote(a, i, k, piece(a, me ^ k)))
                if arrivals:
                    recvs.append(remote(a, i, k, piece(a, me)))
    return local, sends, recvs


def _xch_start(meta, srcs, outs, sems):
    local, sends, _ = _xch_copies(meta, srcs, outs, sems, False)
    for cp in local + sends:
        cp.start()


def _xch_wait(meta, srcs, outs, sems):
    local, sends, recvs = _xch_copies(meta, srcs, outs, sems, True)
    for cp in recvs:
        cp.wait_recv()
    for cp in sends:
        cp.wait_send()
    for cp in local:
        cp.wait()


def _gat_copies(srcs, outs, sems):
    send_sems, recv_sems, local_sems = sems
    x, y, c = lax.axis_index("x"), lax.axis_index("y"), lax.axis_index("c")
    me, sibling = (x, y, c), (x, y, 1 - c)
    chips = [(1 - x, y), (x, 1 - y), (1 - x, 1 - y)]

    def copy(a, k, owner, to, from_block=False):
        slot = outs[a].at[4 * owner[0] + 2 * owner[1] + owner[2]]
        return pltpu.make_async_remote_copy(
            src_ref=srcs[a] if from_block else slot, dst_ref=slot, send_sem=send_sems.at[a, k], recv_sem=recv_sems.at[a, k],
            device_id=to, device_id_type=MESH)

    n = len(srcs)
    local = lambda: [pltpu.make_async_copy(srcs[a], outs[a].at[4 * x + 2 * y + c], local_sems.at[a]) for a in range(n)]
    first = lambda: [cp for a in range(n) for cp in
                     [copy(a, 0, me, sibling, True)] + [copy(a, 1 + j, me, (*chip, c), True) for j, chip in enumerate(chips)]]
    landed = lambda: [copy(a, 1 + j, (*chip, c), me) for a in range(n) for j, chip in enumerate(chips)]
    passed = lambda: [copy(a, 4 + j, (*chip, c), sibling) for a in range(n) for j, chip in enumerate(chips)]
    final = lambda: [cp for a in range(n) for cp in
                     [copy(a, 0, sibling, me)] + [copy(a, 4 + j, (*chip, 1 - c), me) for j, chip in enumerate(chips)]]
    return local, first, landed, passed, final


def _gat_start(srcs, outs, sems):
    local, first, _, _, _ = _gat_copies(srcs, outs, sems)
    for cp in local() + first():
        cp.start()


def _gat_forward(srcs, outs, sems):
    _, _, landed, passed, _ = _gat_copies(srcs, outs, sems)
    for got, on in zip(landed(), passed()):
        got.wait_recv()
        on.start()


def _gat_wait(srcs, outs, sems):
    local, first, _, passed, final = _gat_copies(srcs, outs, sems)
    for cp in final():
        cp.wait_recv()
    for cp in first() + passed():
        cp.wait_send()
    for cp in local():
        cp.wait()


def _call(body, name, grid, in_specs, out_specs, out_shape, args, scratch=(), xch=(), gat=()):
    n_in, n_out, n_x, n_g, n_s = len(in_specs), len(out_specs), len(xch), len(gat), len(scratch)
    kinds = [(k, ks) for _, k, ks in xch]
    total = 1
    for g in grid:
        total *= g

    def wrapped(*refs):
        ins, refs = refs[:n_in], refs[n_in:]
        x_src, refs = refs[:n_x], refs[n_x:]
        g_src, refs = refs[:n_g], refs[n_g:]
        outs, refs = refs[:n_out], refs[n_out:]
        x_out, refs = refs[:n_x], refs[n_x:]
        g_out, refs = refs[:n_g], refs[n_g:]
        own, refs = refs[:n_s], refs[n_s:]
        x_sems, g_sems = (refs[:3], refs[3:]) if n_x else ((), refs)
        step = pl.program_id(0)
        for d in range(1, len(grid)):
            step = step * grid[d] + pl.program_id(d)
        if n_x or n_g:
            @pl.when(step == 0)
            def _():
                if n_x:
                    _xch_start(kinds, x_src, x_out, x_sems)
                if n_g:
                    _gat_start(g_src, g_out, g_sems)

        body(*ins, *outs, *own)
        if n_g:
            @pl.when(step == max(total - 2, 0))
            def _():
                _gat_forward(g_src, g_out, g_sems)

        if n_x or n_g:
            @pl.when(step == total - 1)
            def _():
                if n_x:
                    _xch_wait(kinds, x_src, x_out, x_sems)
                if n_g:
                    _gat_wait(g_src, g_out, g_sems)

    any_spec = pl.BlockSpec(memory_space=pl.ANY)
    g_shapes = [_sds((NDEV,) + tuple(a.shape), a.dtype) for a in gat]
    res = pl.pallas_call(
        wrapped, name=name, grid=grid, in_specs=list(in_specs) + [any_spec] * (n_x + n_g),
        out_specs=list(out_specs) + [any_spec] * (n_x + n_g), out_shape=list(out_shape) + _xch_shapes(xch) + g_shapes,
        scratch_shapes=list(scratch) + (_xch_scratch(n_x) if n_x else []) + (_xch_scratch(n_g) if n_g else []),
        compiler_params=_cp(len(grid)),
    )(*args, *[a for a, _, _ in xch], *gat)
    return res[:n_out], res[n_out:n_out + n_x], res[n_out + n_x:]


def embed(x, meta8, lp, gat):
    bl, seq, _ = x.shape
    c8 = D // NDEV
    cb = 2 * c8

    def body(x_ref, m_ref, h_ref):
        h_ref[0:NMETA, :] = jnp.concatenate([m_ref[0], m_ref[1]], axis=1)
        h_ref[NMETA:NMETA + seq, :] = x_ref[...]
        h_ref[NMETA + seq:, :] = jnp.zeros((lp - NMETA - seq, cb), F32)

    (h0,), _, got = _call(
        body, "embed", (bl, D // cb),
        [pl.BlockSpec((None, seq, cb), lambda b, c: (b, 0, c)), pl.BlockSpec((2, NMETA, c8), lambda b, c: (c, 0, 0))],
        [pl.BlockSpec((None, lp, cb), lambda b, c: (b, 0, c))], [_sds((bl, lp, D), F32)], (x, meta8), gat=gat)
    return h0, got


def conv_in_fwd(h, nm, l, w_in, b_in, i, tm, gat):
    t = h.shape[0]

    def body(h_ref, g_ref, w_ref, b_ref, u_ref, big_ref, a_ref):
        x = h_ref[...]
        ub = (x * _rstd(x) * g_ref[...]).astype(BF)
        u_ref[...] = ub
        big = jnp.dot(ub, w_ref[...], preferred_element_type=F32) + b_ref[...]
        big_ref[...] = big.astype(BF)
        a_ref[...] = big[:, :D] * _sig(big[:, D:])

    return _call(
        body, f"conv_in_fwd{i}", (t // tm,), [_row(tm, D), _lay(l, (1, D)), _res((D, 2 * D)), _lay(i, (1, 2 * D))],
        [_row(tm, D), _row(tm, 2 * D), _row(tm, D)], [_sds((t, D), BF), _sds((t, 2 * D), BF), _sds((t, D), F32)],
        (h, nm, w_in, b_in), gat=gat)


def _prev_halo(tm):
    return pl.BlockSpec((HALO, D), lambda i: (jnp.maximum(i * (tm // HALO) - 1, 0), 0))


def _next_halo(tm, t):
    return pl.BlockSpec((HALO, D), lambda i: (jnp.minimum((i + 1) * (tm // HALO), t // HALO - 1), 0))


def conv_mid_fwd(a, dw, ln_g, ln_b, i, tm, tpb, gat):
    t = a.shape[0]

    def body(a_ref, halo_ref, dw_ref, g_ref, b_ref, c_ref, s_ref, ext, taps):
        _spread_taps(dw_ref, taps)
        first = pl.program_id(0) % tpb == 0
        ext[0:HALO] = jnp.where(first, 0.0, halo_ref[...])
        ext[HALO:] = a_ref[...]

        def chunk(k, carry):
            r0 = pl.multiple_of(k * CHUNK, CHUNK)
            win = _shifted(ext[pl.ds(r0, 2 * CHUNK), :])
            c = jnp.zeros((CHUNK, D), F32)
            for j in range(CW):
                c = c + _tap_weight(taps, j) * _tap(win, j + 2)
            c_ref[pl.ds(r0, CHUNK), :] = c
            mu = jnp.mean(c, axis=-1, keepdims=True)
            xc = c - mu
            n = xc * lax.rsqrt(jnp.mean(xc * xc, axis=-1, keepdims=True) + EPS) * g_ref[...] + b_ref[...]
            s_ref[pl.ds(r0, CHUNK), :] = (n * _sig(n)).astype(BF)
            return carry

        lax.fori_loop(0, tm // CHUNK, chunk, 0, unroll=2)

    return _call(
        body, f"conv_mid_fwd{i}", (t // tm,),
        [_row(tm, D), _prev_halo(tm), _lay(i, (CW, D)), _lay(i, (1, D)), _lay(i, (1, D))],
        [_row(tm, D), _row(tm, D)], [_sds((t, D), F32), _sds((t, D), BF)], (a, a, dw, ln_g, ln_b),
        scratch=[pltpu.VMEM((tm + HALO, D), F32), pltpu.VMEM((CW, 8, D), F32)], gat=gat)


def mixer_ffn_fwd(h, s, w_out, lw, bias, nf, l, wg, wu, wd, tm, gat):
    t = h.shape[0]

    def body(*refs):
        if bias is None:
            h_ref, s_ref, w_ref, nf_ref, wg_ref, wu_ref, wd_ref, h1_ref, u_ref, g_ref, up_ref, hid_ref, h2_ref = refs
            y = 0.0
        else:
            h_ref, s_ref, w_ref, b_ref, nf_ref, wg_ref, wu_ref, wd_ref, h1_ref, u_ref, g_ref, up_ref, hid_ref, h2_ref = refs
            y = b_ref[...]
        h1 = h_ref[...] + (jnp.dot(s_ref[...], w_ref[...], preferred_element_type=F32) + y)
        h1_ref[...] = h1
        ub = (h1 * _rstd(h1) * nf_ref[...]).astype(BF)
        u_ref[...] = ub
        g = _dot_nt(ub, wg_ref[...])
        up = _dot_nt(ub, wu_ref[...])
        g_ref[...] = g.astype(BF)
        up_ref[...] = up.astype(BF)
        hid = (g * _sig(g) * up).astype(BF)
        hid_ref[...] = hid
        h2_ref[...] = h1 + jnp.dot(hid, wd_ref[...], preferred_element_type=F32)

    ins = [h, s, w_out] + ([] if bias is None else [bias]) + [nf, wg, wu, wd]
    specs = ([_row(tm, D), _row(tm, D), _res((D, D))] + ([] if bias is None else [_lay(lw, (1, D))])
             + [_lay(l, (1, D)), _res((DFF, D)), _res((DFF, D)), _res((DFF, D))])
    return _call(
        body, f"mixer_ffn_fwd{l}", (t // tm,), specs,
        [_row(tm, D), _row(tm, D), _row(tm, DFF), _row(tm, DFF), _row(tm, DFF), _row(tm, D)],
        [_sds((t, D), F32), _sds((t, D), BF), _sds((t, DFF), BF), _sds((t, DFF), BF), _sds((t, DFF), BF), _sds((t, D), F32)],
        ins, gat=gat)


def _seg_rms(x, g, nseg):
    outs = []
    for s in range(nseg):
        xs = x[:, HD * s:HD * s + HD]
        outs.append(xs * _rstd(xs) * g)
    return jnp.concatenate(outs, axis=1)


def kv_fwd(h, kvn, w_kv, kng, tm):
    t = h.shape[0]

    def body(h_ref, g_ref, w_ref, kg_ref, kn_ref, kv_ref, k_ref, v_ref):
        x = h_ref[...]
        kn = (x * _rstd(x) * g_ref[...]).astype(BF)
        kn_ref[...] = kn
        kv = jnp.dot(kn, w_ref[...], preferred_element_type=F32)
        kv_ref[...] = kv
        k_ref[...] = _seg_rms(kv[:, :KVD], kg_ref[...], NKV).astype(BF)
        v_ref[...] = kv[:, KVD:].astype(BF)

    return pl.pallas_call(
        body, name="kv_fwd", grid=(t // tm,),
        in_specs=[_row(tm, D), _res((1, D)), _res((D, 2 * KVD)), _res((1, HD))],
        out_specs=[_row(tm, D), _row(tm, 2 * KVD), _row(tm, KVD), _row(tm, KVD)],
        out_shape=[_sds((t, D), BF), _sds((t, 2 * KVD), F32), _sds((t, KVD), BF), _sds((t, KVD), BF)],
        compiler_params=_cp(1),
    )(h, kvn, w_kv, kng)


def q_fwd(h, nm, l, w_q, j, tm):
    t = h.shape[0]

    def body(h_ref, g_ref, w_ref, u_ref, q_ref):
        x = h_ref[...]
        ub = (x * _rstd(x) * g_ref[...]).astype(BF)
        u_ref[...] = ub
        q_ref[...] = jnp.dot(ub, w_ref[...], preferred_element_type=F32)

    return pl.pallas_call(
        body, name=f"q_fwd{j}", grid=(t // tm,),
        in_specs=[_row(tm, D), _lay(l, (1, D)), _res((D, D))],
        out_specs=[_row(tm, D), _row(tm, D)], out_shape=[_sds((t, D), BF), _sds((t, D), F32)],
        compiler_params=_cp(1),
    )(h, nm, w_q)


RQ = NH // NKV


NKEYS = 2 * QB + NMETA


def _attn_mask(n, start):
    shape = (RQ * QB, NKEYS)
    qpos = n * QB + (lax.broadcasted_iota(jnp.int32, shape, 0) & (QB - 1))
    col = lax.broadcasted_iota(jnp.int32, shape, 1)
    in_band = col < 2 * QB
    kpos = jnp.where(in_band, start + col, col - 2 * QB)
    return (kpos <= qpos) & ((col >= 2 * QB) | ((qpos - kpos < QB) & (kpos >= NMETA)))


def _keys(ref, band, gs):
    return jnp.concatenate([ref[band, gs], ref[0:NMETA, gs]], axis=0)


def _keys_t(ref, band, gs):
    return jnp.concatenate([ref[gs, band], ref[gs, 0:NMETA]], axis=1)


def transpose_seq(a, name):
    bl, r, c = a.shape

    def body(a_ref, o_ref):
        o_ref[...] = a_ref[...].T

    return pl.pallas_call(
        body, name=name, grid=(bl,), in_specs=[pl.BlockSpec((None, r, c), lambda b: (b, 0, 0))],
        out_specs=pl.BlockSpec((None, c, r), lambda b: (b, 0, 0)), out_shape=_sds((bl, c, r), a.dtype), compiler_params=_cp(1),
    )(a)


def sum_transposed(a0, a1):
    bl, c, r = a0.shape

    def body(a0_ref, a1_ref, o_ref):
        o_ref[...] = (a0_ref[...] + a1_ref[...]).T

    spec = pl.BlockSpec((None, c, r), lambda b: (b, 0, 0))
    return pl.pallas_call(
        body, name="sum_transposed", grid=(bl,), in_specs=[spec, spec],
        out_specs=pl.BlockSpec((r, c), lambda b: (b, 0)), out_shape=_sds((bl * r, c), a0.dtype), compiler_params=_cp(1),
    )(a0, a1)


def _stack_heads(ref, g, fn):
    return jnp.concatenate([fn(ref[:, HD * (g * RQ + r):HD * (g * RQ + r) + HD]) for r in range(RQ)], axis=0)


def _stack_cols(ref, g):
    return jnp.concatenate([ref[:, g * RQ + r:g * RQ + r + 1] for r in range(RQ)], axis=0)


def _stack_sinks(sk_ref, g):
    return jnp.concatenate([jnp.broadcast_to(sk_ref[:, g * RQ + r:g * RQ + r + 1], (QB, 1)) for r in range(RQ)], axis=0)


def attn_fwd(q, kt, v, qg, sinks, j, bl, lp, gat):
    t = q.shape[0]
    nb = lp // QB

    def body(q_ref, kt_ref, v_ref, qg_ref, sk_ref, o_ref, lse_ref):
        n = pl.program_id(0)
        start = pl.multiple_of(jnp.maximum(n - 1, 0) * QB, QB)
        mask = _attn_mask(n, start)
        band = pl.ds(start, 2 * QB)
        lane = lax.broadcasted_iota(jnp.int32, (QB, NH), 1)
        ones = jnp.ones((NKEYS, HD), BF)
        pairs = [(b, g) for b in range(bl) for g in range(NKV)]
        gsl = [slice(HD * g, HD * g + HD) for g in range(NKV)]
        qns = [_stack_heads(q_ref.at[b], g, lambda x: (x * _rstd(x) * (qg_ref[...] * SCALE)).astype(BF)) for b, g in pairs]
        ss = [jnp.where(mask, _dot(qns[i], _keys_t(kt_ref.at[b], band, gsl[g])), NEG) for i, (b, g) in enumerate(pairs)]
        sinks = [_stack_sinks(sk_ref, g) for g in range(NKV)]
        mxs = [jnp.maximum(jnp.max(ss[i], -1, keepdims=True), sinks[g]) for i, (b, g) in enumerate(pairs)]
        oas = [_dot(jnp.exp(ss[i] - mxs[i]), jnp.concatenate([_keys(v_ref.at[b], band, gsl[g]), ones], axis=1))
               for i, (b, g) in enumerate(pairs)]
        lses = [jnp.zeros((QB, NH), F32) for _ in range(bl)]
        for i, (b, g) in enumerate(pairs):
            den = oas[i][:, HD:HD + 1] + jnp.exp(sinks[g] - mxs[i])
            o = oas[i][:, :HD] * (1.0 / den)
            l = mxs[i] + jnp.log(den)
            for r in range(RQ):
                h = g * RQ + r
                o_ref[b, :, HD * h:HD * h + HD] = o[r * QB:(r + 1) * QB].astype(BF)
                lses[b] = jnp.where(lane == h, l[r * QB:(r + 1) * QB], lses[b])
        for b in range(bl):
            lse_ref[b] = lses[b]

    blk = lambda c: pl.BlockSpec((bl, QB, c), lambda n: (0, n, 0))
    (o, lse), _, got = _call(
        body, f"attn_fwd{j}", (nb,),
        [blk(D), pl.BlockSpec((bl, KVD, lp), lambda n: (0, 0, 0)), pl.BlockSpec((bl, lp, KVD), lambda n: (0, 0, 0)),
         pl.BlockSpec((None, 1, HD), lambda n: (j, 0, 0)), pl.BlockSpec((None, 1, NH), lambda n: (j, 0, 0))],
        [blk(D), blk(NH)], [_sds((bl, lp, D), BF), _sds((bl, lp, NH), F32)], (q.reshape(bl, lp, D), kt, v, qg, sinks), gat=gat)
    return (o.reshape(t, D), lse.reshape(t, NH)), (), got


def loss_fwd(h, tgt):
    bl, lp, _ = h.shape
    seq = tgt.shape[1]
    cb = 256

    def body(h_ref, t_ref, dh_ref, loss_ref):
        _init(loss_ref, (pl.program_id(0) == 0) & (pl.program_id(1) == 0))
        err = h_ref[NMETA:NMETA + seq, :] - t_ref[...]
        dh_ref[...] = jnp.zeros_like(dh_ref)
        dh_ref[NMETA:NMETA + seq, :] = err * (1.0 / D)
        loss_ref[...] += (0.5 / D) * jnp.sum(err * err)

    return pl.pallas_call(
        body, name="loss_fwd", grid=(bl, D // cb),
        in_specs=[pl.BlockSpec((None, lp, cb), lambda b, c: (b, 0, c)), pl.BlockSpec((None, seq, cb), lambda b, c: (b, 0, c))],
        out_specs=[pl.BlockSpec((None, lp, cb), lambda b, c: (b, 0, c)), pl.BlockSpec((8, 128), lambda b, c: (0, 0))],
        out_shape=[_sds((bl, lp, D), F32), _sds((8, 128), F32)],
        compiler_params=_cp(2),
    )(h, tgt)


def ffn_bwd_x(dh2, g, up, h1, nf, l, wd, wg, wu, w_o, tm, xch):
    t = dh2.shape[0]

    def body(dh2_ref, g_ref, up_ref, h1_ref, nf_ref, wd_ref, wg_ref, wu_ref, *rest):
        if w_o is None:
            dg_ref, du_ref, dh1_ref, dnf_ref = rest
        else:
            wo_ref, dg_ref, du_ref, dh1_ref, dnf_ref, do_ref = rest
        _init(dnf_ref, pl.program_id(0) == 0)
        dh2v = dh2_ref[...]
        dhid = _dot_nt(dh2v, wd_ref[...])
        gv = g_ref[...].astype(F32)
        uv = up_ref[...].astype(F32)
        sg = _sig(gv)
        dgv = (dhid * uv * (sg * (1.0 + gv * (1.0 - sg)))).astype(BF)
        duv = (dhid * (gv * sg)).astype(BF)
        dg_ref[...] = dgv
        du_ref[...] = duv
        dnorm = _dot(dgv, wg_ref[...]) + _dot(duv, wu_ref[...])
        dx, dnf = _rms_bwd(h1_ref[...], nf_ref[...], dnorm)
        dh1 = dh2v + dx
        dh1_ref[...] = dh1
        dnf_ref[...] += dnf
        if w_o is not None:
            do_ref[...] = _dot_nt(dh1, wo_ref[...]).astype(BF)

    attn = w_o is not None
    return _call(
        body, f"ffn_bwd_x{l}", (t // tm,),
        [_row(tm, D), _row(tm, DFF), _row(tm, DFF), _row(tm, D), _lay(l, (1, D)),
         _res((DFF, D)), _res((DFF, D)), _res((DFF, D))] + ([_res((D, D))] if attn else []),
        [_row(tm, DFF), _row(tm, DFF), _row(tm, D), _acc((1, D))] + ([_row(tm, D)] if attn else []),
        [_sds((t, DFF), BF), _sds((t, DFF), BF), _sds((t, D), F32), _sds((1, D), F32)] + ([_sds((t, D), BF)] if attn else []),
        (dh2, g, up, h1, nf, wd, wg, wu) + ((w_o,) if attn else ()), xch=xch)


def mm_tn(x, dy, tm, name, split=False, transposed=False, xch=()):
    t, kk = x.shape
    nn = dy.shape[1]
    n8 = nn // NDEV
    nsteps = t // tm

    def body(x_ref, dy_ref, o_ref, acc):
        i = pl.program_id(0)
        _init(acc, i == 0)
        acc[...] += _dot_tn(x_ref[...], dy_ref[...])

        @pl.when(i == nsteps - 1)
        def _():
            if split:
                for p in range(NDEV):
                    o_ref[p] = acc[:, p * n8:(p + 1) * n8].astype(BF)
            elif transposed:
                o_ref[...] = acc[...].T.astype(BF)
            else:
                o_ref[...] = acc[...].astype(BF)

    oshape = (NDEV, kk, n8) if split else ((nn, kk) if transposed else (kk, nn))
    (out,), got, _ = _call(body, name, (nsteps,), [_row(tm, kk), _row(tm, nn)], [_acc(oshape)], [_sds(oshape, BF)], (x, dy),
                           scratch=[pltpu.VMEM((kk, nn), F32)], xch=xch)
    return out, got


def proj_bwd(dy, w, h, g, lg, dh_in, tm, name, xch=()):
    t = h.shape[0]
    nn = dy.shape[1]
    wspec = _res(w.shape)
    gspec = _res((1, D)) if lg is None else _lay(lg, (1, D))

    def body(dy_ref, w_ref, h_ref, g_ref, dhin_ref, dh_ref, dg_ref):
        _init(dg_ref, pl.program_id(0) == 0)
        du = _dot_nt(dy_ref[...], w_ref[...])
        dx, dg = _rms_bwd(h_ref[...], g_ref[...], du)
        dh_ref[...] = dhin_ref[...] + dx
        dg_ref[...] += dg

    return _call(body, name, (t // tm,), [_row(tm, nn), wspec, _row(tm, D), gspec, _row(tm, D)],
                 [_row(tm, D), _acc((1, D))], [_sds((t, D), F32), _sds((1, D), F32)], (dy, w, h, g, dh_in), xch=xch)


def attn_bwd(q, k, kt, vt, do, o, lse, qg, sinks, j, bl, lp, xch):
    t = q.shape[0]
    nb = lp // QB

    def body(q_ref, k_ref, kt_ref, vt_ref, do_ref, o_ref, lse_ref, qg_ref, sk_ref, dq_ref, dk_ref, dv_ref, dqg_ref, dsk_ref):
        n = pl.program_id(0)
        for ref in (dk_ref, dv_ref, dqg_ref, dsk_ref):
            _init(ref, n == 0)
        start = pl.multiple_of(jnp.maximum(n - 1, 0) * QB, QB)
        mask = _attn_mask(n, start)
        band = pl.ds(start, 2 * QB)
        lane = lax.broadcasted_iota(jnp.int32, (1, NH), 1)
        dqg = jnp.zeros((1, HD), F32)
        dsk = jnp.zeros((1, NH), F32)
        pairs = [(b, g) for b in range(bl) for g in range(NKV)]
        idx = range(len(pairs))
        gsl = [slice(HD * g, HD * g + HD) for g in range(NKV)]
        qhs = [_stack_heads(q_ref.at[b], g, lambda x: x) for b, g in pairs]
        rss = [_rstd(qhs[i]) for i in idx]
        qns = [(qhs[i] * rss[i] * (qg_ref[...] * SCALE)).astype(BF) for i in idx]
        lss = [_stack_cols(lse_ref.at[b], g) for b, g in pairs]
        dohs = [_stack_heads(do_ref.at[b], g, lambda x: x) for b, g in pairs]
        deltas = [jnp.sum(dohs[i].astype(F32) * _stack_heads(o_ref.at[b], g, lambda x: x).astype(F32), axis=-1, keepdims=True)
                  for i, (b, g) in enumerate(pairs)]
        prs = [jnp.where(mask, jnp.exp(_dot(qns[i], _keys_t(kt_ref.at[b], band, gsl[g])) - lss[i]), 0.0)
               for i, (b, g) in enumerate(pairs)]
        dss = [(prs[i] * (_dot(dohs[i], _keys_t(vt_ref.at[b], band, gsl[g])) - deltas[i])).astype(BF)
               for i, (b, g) in enumerate(pairs)]
        for i, (b, g) in enumerate(pairs):
            gs = gsl[g]
            dkt = _dot_tn(qns[i], dss[i])
            dvt = _dot_tn(dohs[i], prs[i])
            dk_ref[b, gs, band] += dkt[:, :2 * QB]
            dv_ref[b, gs, band] += dvt[:, :2 * QB]
            dk_ref[b, gs, 0:NMETA] += dkt[:, 2 * QB:]
            dv_ref[b, gs, 0:NMETA] += dvt[:, 2 * QB:]
        dqns = [_dot(dss[i], _keys(k_ref.at[b], band, gsl[g])) * SCALE for i, (b, g) in enumerate(pairs)]
        for i, (b, g) in enumerate(pairs):
            qh, rs, dqn = qhs[i], rss[i], dqns[i]
            dsink = jnp.exp(_stack_sinks(sk_ref, g) - lss[i]) * deltas[i]
            z = dqn * qg_ref[...]
            dq = rs * z - qh * (rs * rs * rs * jnp.mean(z * qh, axis=-1, keepdims=True))
            dqg = dqg + jnp.sum(dqn * qh * rs, axis=0, keepdims=True)
            for r in range(RQ):
                h = g * RQ + r
                dq_ref[b, :, HD * h:HD * h + HD] = dq[r * QB:(r + 1) * QB]
                dsk = dsk + jnp.where(lane == h, -jnp.sum(dsink[r * QB:(r + 1) * QB]), 0.0)
        dqg_ref[...] += dqg
        dsk_ref[...] += dsk

    blk = lambda c: pl.BlockSpec((bl, QB, c), lambda n: (0, n, 0))
    seq = pl.BlockSpec((bl, lp, KVD), lambda n: (0, 0, 0))
    seq_t = pl.BlockSpec((bl, KVD, lp), lambda n: (0, 0, 0))
    as3 = lambda a: a.reshape(bl, lp, a.shape[-1])
    (dq, dk, dv, dqg, dsk), got, _ = _call(
        body, f"attn_bwd{j}", (nb,),
        [blk(D), seq, seq_t, seq_t, blk(D), blk(D), blk(NH),
         pl.BlockSpec((None, 1, HD), lambda n: (j, 0, 0)), pl.BlockSpec((None, 1, NH), lambda n: (j, 0, 0))],
        [blk(D), seq_t, seq_t, pl.BlockSpec((1, HD), lambda n: (0, 0)), pl.BlockSpec((1, NH), lambda n: (0, 0))],
        [_sds((bl, lp, D), F32), _sds((bl, KVD, lp), F32), _sds((bl, KVD, lp), F32), _sds((1, HD), F32), _sds((1, NH), F32)],
        (as3(q), k, kt, vt, as3(do), as3(o), as3(lse), qg, sinks), xch=xch)
    return (dq.reshape(t, D), dk, dv, dqg, dsk), got, ()


def kv_bwd_pre(dk, dv, kv, kng, tm):
    t = kv.shape[0]

    def body(dk_ref, dv_ref, kv_ref, g_ref, dkv_ref, dg_ref):
        _init(dg_ref, pl.program_id(0) == 0)
        dg = jnp.zeros((1, HD), F32)
        outs = []
        for s in range(NKV):
            sl = slice(HD * s, HD * s + HD)
            dx, dgs = _rms_bwd(kv_ref[:, sl], g_ref[...], dk_ref[:, sl])
            outs.append(dx)
            dg = dg + dgs
        dkv_ref[:, :KVD] = jnp.concatenate(outs, axis=1).astype(BF)
        dkv_ref[:, KVD:] = dv_ref[...].astype(BF)
        dg_ref[...] += dg

    return pl.pallas_call(
        body, name="kv_bwd_pre", grid=(t // tm,),
        in_specs=[_row(tm, KVD)] * 2 + [_row(tm, 2 * KVD), _res((1, HD))],
        out_specs=[_row(tm, 2 * KVD), _acc((1, HD))], out_shape=[_sds((t, 2 * KVD), BF), _sds((1, HD), F32)],
        compiler_params=_cp(1),
    )(dk, dv, kv, kng)


def conv_out_bwd(dh1, c, ln_g, ln_b, w_out, i, tm, xch):
    t = dh1.shape[0]

    def body(dh1_ref, c_ref, g_ref, b_ref, w_ref, dc_ref, dg_ref, db_ref, dbo_ref):
        first = pl.program_id(0) == 0
        _init(dg_ref, first)
        _init(db_ref, first)
        _init(dbo_ref, first)
        dh1v = dh1_ref[...]
        ds = _dot_nt(dh1v, w_ref[...])
        cv = c_ref[...]
        xc = cv - jnp.mean(cv, axis=-1, keepdims=True)
        rstd = lax.rsqrt(jnp.mean(xc * xc, axis=-1, keepdims=True) + EPS)
        xh = xc * rstd
        n = xh * g_ref[...] + b_ref[...]
        sg = _sig(n)
        dn = ds * (sg * (1.0 + n * (1.0 - sg)))
        dxh = dn * g_ref[...]
        dc_ref[...] = rstd * (dxh - jnp.mean(dxh, axis=-1, keepdims=True) - xh * jnp.mean(dxh * xh, axis=-1, keepdims=True))
        dg_ref[...] += jnp.sum(dn * xh, axis=0, keepdims=True)
        db_ref[...] += jnp.sum(dn, axis=0, keepdims=True)
        dbo_ref[...] += jnp.sum(dh1v, axis=0, keepdims=True)

    return _call(
        body, f"conv_out_bwd{i}", (t // tm,), [_row(tm, D), _row(tm, D), _lay(i, (1, D)), _lay(i, (1, D)), _res((D, D))],
        [_row(tm, D), _acc((1, D)), _acc((1, D)), _acc((1, D))], [_sds((t, D), F32)] + [_sds((1, D), F32)] * 3,
        (dh1, c, ln_g, ln_b, w_out), xch=xch)


def conv_mid_bwd(dc, a, big, dw, i, tm, tpb, xch):
    t = dc.shape[0]
    nsteps = t // tm

    def body(dc_ref, nxt_ref, a_ref, prv_ref, big_ref, dw_ref, da_ref, dbin_ref, ddw_ref, dce, ae, wacc, bacc, taps):
        i_ = pl.program_id(0)
        _spread_taps(dw_ref, taps)
        _init(wacc, i_ == 0)
        _init(bacc, i_ == 0)
        dce[0:tm] = dc_ref[...]
        dce[tm:] = jnp.where(i_ % tpb == tpb - 1, 0.0, nxt_ref[...])
        ae[0:HALO] = jnp.where(i_ % tpb == 0, 0.0, prv_ref[...])
        ae[HALO:] = a_ref[...]

        def chunk(k, carry):
            r0 = pl.multiple_of(k * CHUNK, CHUNK)
            wdc = _shifted(dce[pl.ds(r0, 2 * CHUNK), :])
            wa = _shifted(ae[pl.ds(r0, 2 * CHUNK), :])
            dcc = wdc[0][0:CHUNK]
            da = jnp.zeros((CHUNK, D), F32)
            for j in range(CW):
                da = da + _tap_weight(taps, j) * _tap(wdc, CW - 1 - j)
                wacc[j] += _fold8(dcc * _tap(wa, j + 2))
            bv = big_ref[pl.ds(r0, CHUNK), :].astype(F32)
            a1, sg = bv[:, :D], _sig(bv[:, D:])
            d1 = da * sg
            d2 = da * a1 * sg * (1.0 - sg)
            da_ref[pl.ds(r0, CHUNK), 0:D] = d1.astype(BF)
            da_ref[pl.ds(r0, CHUNK), D:2 * D] = d2.astype(BF)
            bacc[:, 0:D] += _fold8(d1)
            bacc[:, D:2 * D] += _fold8(d2)
            return carry

        lax.fori_loop(0, tm // CHUNK, chunk, 0)

        @pl.when(i_ == nsteps - 1)
        def _():
            dbin_ref[...] = jnp.sum(bacc[...], axis=0, keepdims=True)
            ddw_ref[...] = jnp.sum(wacc[...], axis=1)

    return _call(
        body, f"conv_mid_bwd{i}", (nsteps,),
        [_row(tm, D), _next_halo(tm, t), _row(tm, D), _prev_halo(tm), _row(tm, 2 * D), _lay(i, (CW, D))],
        [_row(tm, 2 * D), _acc((1, 2 * D)), _acc((CW + 1, D))],
        [_sds((t, 2 * D), BF), _sds((1, 2 * D), F32), _sds((CW + 1, D), F32)],
        (dc, dc, a, a, big, dw),
        scratch=[pltpu.VMEM((tm + HALO, D), F32), pltpu.VMEM((tm + HALO, D), F32),
                 pltpu.VMEM((CW + 1, 8, D), F32), pltpu.VMEM((8, 2 * D), F32), pltpu.VMEM((CW, 8, D), F32)], xch=xch)


def input_grads(dh0, seq):
    bl, lp, _ = dh0.shape
    cb = 256

    def body(dh_ref, gx_ref, gm_ref):
        _init(gm_ref, pl.program_id(1) == 0)
        gx_ref[...] = dh_ref[NMETA:NMETA + seq, :]
        gm_ref[...] += dh_ref[0:NMETA, :]

    return pl.pallas_call(
        body, name="input_grads", grid=(D // cb, bl),
        in_specs=[pl.BlockSpec((None, lp, cb), lambda c, b: (b, 0, c))],
        out_specs=[pl.BlockSpec((None, seq, cb), lambda c, b: (b, 0, c)), pl.BlockSpec((NMETA, cb), lambda c, b: (0, c))],
        out_shape=[_sds((bl, seq, D), F32), _sds((NMETA, D), F32)],
        compiler_params=_cp(2),
    )(dh0)


GATHER_PLAN = {
    "embed": [("conv_w_in", 0), ("conv_w_out", 0)],
    "conv_in_fwd0": [("ffn_w_down", 0)],
    "conv_mid_fwd0": [("ffn_w_gate", 0), ("ffn_w_up", 0)],
    "mixer_ffn_fwd0": [("conv_w_in", 1), ("conv_w_out", 1), ("ffn_w_gate", 1)],
    "conv_in_fwd1": [("w_kv", 0), ("w_q", 0)],
    "conv_mid_fwd1": [("ffn_w_up", 1), ("ffn_w_down", 1)],
    "mixer_ffn_fwd1": [("w_o", 0), ("ffn_w_down", 2)],
    "attn_fwd0": [("ffn_w_gate", 2), ("ffn_w_up", 2), ("w_q", 1), ("w_o", 1)],
    "attn_fwd1": [("ffn_w_gate", 3), ("ffn_w_up", 3), ("ffn_w_down", 3)],
}
EXCHANGE_PLAN = {
    "attn_bwd1": [("ffn_w_down3", ALL), ("ffn_w_gate3", ALL)],
    "dw_down2": [("w_o1", ALL)],
    "ffn_bwd_x2": [("ffn_w_up3", ALL), ("w_q1", ALL)],
    "attn_bwd0": [("ffn_w_down2", ALL), ("ffn_w_gate2", ALL)],
    "dw_down1": [("w_o0", ALL), ("w_q0", H1)],
    "ffn_bwd_x1": [("ffn_w_up2", ALL), ("w_q0", H2), ("w_kv", ALL)],
    "dw_gate1": [("ffn_w_down1", H1)],
    "dw_up1": [("ffn_w_down1", H2)],
    "conv_mid_bwd1": [("ffn_w_gate1", ALL), ("ffn_w_up1", H1)],
    "conv_in_bwd1": [("ffn_w_up1", H2)],
    "dw_down0": [("conv_w_out1", ALL)],
    "ffn_bwd_x0": [("conv_w_in1", ALL)],
    "dw_gate0": [("ffn_w_down0", H1)],
    "dw_up0": [("ffn_w_down0", H2)],
    "conv_out_bwd0": [("ffn_w_gate0", H1)],
    "conv_mid_bwd0": [("ffn_w_gate0", H2), ("ffn_w_up0", H1), ("conv_w_out0", ALL)],
    "dw_conv_in0": [("ffn_w_up0", H2)],
    "conv_in_bwd0": [("conv_w_in0", H1)],
    "tail": [("conv_w_in0", H2)],
}
BIG = {"conv_w_in": "pieces", "conv_w_out": "rows", "w_kv": "rows", "w_q": "rows", "w_o": "rows",
       "ffn_w_gate": "rows", "ffn_w_up": "rows", "ffn_w_down": "rows"}
EXCHANGE_KIND = BIG
TRANSPOSED = ("ffn_w_gate", "ffn_w_up")


def gathered_matrix(name, layer, blocks8):
    if BIG[name] == "rows":
        return blocks8.reshape(NDEV * blocks8.shape[1], blocks8.shape[2])
    return join_columns(blocks8, f"join_{name}{layer}")


def local_step(x, tgt, meta8, w, shards):
    bl, seq, _ = x.shape
    lp = -(-(NMETA + seq) // QB) * QB
    tpb = 4
    tm = lp // tpb
    t = bl * lp
    na = 2
    flat = lambda a: a.reshape(t, D)
    mats = {}

    def riders(carrier):
        return [shards[key] for key in GATHER_PLAN[carrier]]

    def landed(carrier, blocks):
        for key, b8 in zip(GATHER_PLAN[carrier], blocks):
            mats[key] = gathered_matrix(*key, b8)

    h0, got = embed(x, meta8, lp, riders("embed"))
    landed("embed", got)
    h = flat(h0)
    saved = []
    kvs = None
    for l in range(4):
        rec = {"h": h}
        if l < na:
            name = f"conv_in_fwd{l}"
            (rec["u"], rec["big"], rec["a"]), _, got = conv_in_fwd(h, w["norm_mix"], l, mats["conv_w_in", l], w["conv_b_in"], l, tm,
                                                                  riders(name))
            landed(name, got)
            name = f"conv_mid_fwd{l}"
            (rec["c"], rec["s"]), _, got = conv_mid_fwd(rec["a"], w["conv_dw"], w["conv_ln_g"], w["conv_ln_b"], l, tm, tpb,
                                                         riders(name))
            landed(name, got)
            mixed, w_out, lw, bias = rec["s"], mats["conv_w_out", l], l, w["conv_b_out"]
        else:
            j = l - na
            if kvs is None:
                kvs = dict(zip(("kn", "kv", "k", "v"), kv_fwd(h, w["kv_norm"], mats["w_kv", 0], w["k_norm"], tm)))
                kvs["h"] = h
                kvs["k3"], kvs["v3"] = kvs["k"].reshape(bl, lp, KVD), kvs["v"].reshape(bl, lp, KVD)
                kvs["kt"], kvs["vt"] = transpose_seq(kvs["k3"], "transpose_k"), transpose_seq(kvs["v3"], "transpose_v")
            rec["u"], rec["q"] = q_fwd(h, w["norm_mix"], l, mats["w_q", j], j, tm)
            name = f"attn_fwd{j}"
            (rec["o"], rec["lse"]), _, got = attn_fwd(rec["q"], kvs["kt"], kvs["v3"], w["q_norm"], w["attn_sinks"], j, bl, lp,
                                                      riders(name) if name in GATHER_PLAN else [])
            if name in GATHER_PLAN:
                landed(name, got)
            mixed, w_out, lw, bias = rec["o"], mats["w_o", j], j, None
        name = f"mixer_ffn_fwd{l}"
        (rec["h1"], rec["u2"], rec["g"], rec["up"], rec["hid"], h), _, got = mixer_ffn_fwd(
            h, mixed, w_out, lw, bias, w["norm_ffn"], l, mats["ffn_w_gate", l], mats["ffn_w_up", l], mats["ffn_w_down", l], tm // 2,
            riders(name) if name in GATHER_PLAN else [])
        if name in GATHER_PLAN:
            landed(name, got)
        saved.append(rec)

    dh3, loss_blk = loss_fwd(h.reshape(bl, lp, D), tgt)
    dh = flat(dh3)

    big, small, arrived = {}, {}, {}
    dks, dvs = [], []

    def ride(kernel_name):
        return [(big[nm], EXCHANGE_KIND[nm.rstrip("0123456789")], ks) for nm, ks in EXCHANGE_PLAN.get(kernel_name, [])]

    def landed_x(kernel_name, arrivals):
        for (nm, _), got in zip(EXCHANGE_PLAN.get(kernel_name, []), arrivals):
            arrived.setdefault(nm, []).append(got)

    def dw(name, grad, x, dy, **kw):
        big[grad], got = mm_tn(x, dy, 2 * tm, name, xch=ride(name), **kw)
        landed_x(name, got)

    for l in reversed(range(4)):
        rec = saved[l]
        dw(f"dw_down{l}", f"ffn_w_down{l}", rec["hid"], dh)
        name = f"ffn_bwd_x{l}"
        outs, got, _ = ffn_bwd_x(
            dh, rec["g"], rec["up"], rec["h1"], w["norm_ffn"], l, mats["ffn_w_down", l], mats["ffn_w_gate", l], mats["ffn_w_up", l],
            mats["w_o", l - na] if l >= na else None, tm // 2, ride(name))
        landed_x(name, got)
        dg, du, dh1, small[f"norm_ffn{l}"] = outs[:4]
        dw(f"dw_gate{l}", f"ffn_w_gate{l}", rec["u2"], dg, transposed=True)
        dw(f"dw_up{l}", f"ffn_w_up{l}", rec["u2"], du, transposed=True)
        if l >= na:
            j = l - na
            dw(f"dw_o{j}", f"w_o{j}", rec["o"], dh1)
            name = f"attn_bwd{j}"
            (dq, dk, dv, small[f"q_norm{j}"], small[f"attn_sinks{j}"]), got, _ = attn_bwd(
                rec["q"], kvs["k3"], kvs["kt"], kvs["vt"], outs[4], rec["o"], rec["lse"], w["q_norm"], w["attn_sinks"], j, bl, lp,
                ride(name))
            landed_x(name, got)
            dks.append(dk)
            dvs.append(dv)
            dw(f"dw_q{j}", f"w_q{j}", rec["u"], dq)
            dh, small[f"norm_mix{l}"] = proj_bwd(dq, mats["w_q", j], rec["h"], w["norm_mix"], l, dh1, tm, f"q_bwd{j}")[0]
            if l == na:
                dkv, small["k_norm"] = kv_bwd_pre(sum_transposed(*dks), sum_transposed(*dvs), kvs["kv"], w["k_norm"], tm)
                dw("dw_kv", "w_kv", kvs["kn"], dkv)
                dh, small["kv_norm"] = proj_bwd(dkv, mats["w_kv", 0], kvs["h"], w["kv_norm"], None, dh, tm, "kv_bwd")[0]
        else:
            name = f"conv_out_bwd{l}"
            (dc, small[f"conv_ln_g{l}"], small[f"conv_ln_b{l}"], small[f"conv_b_out{l}"]), got, _ = conv_out_bwd(
                dh1, rec["c"], w["conv_ln_g"], w["conv_ln_b"], mats["conv_w_out", l], l, tm, ride(name))
            landed_x(name, got)
            dw(f"dw_conv_out{l}", f"conv_w_out{l}", rec["s"], dh1)
            name = f"conv_mid_bwd{l}"
            (da, small[f"conv_b_in{l}"], small[f"conv_dw{l}"]), got, _ = conv_mid_bwd(
                dc, rec["a"], rec["big"], w["conv_dw"], l, tm, tpb, ride(name))
            landed_x(name, got)
            dw(f"dw_conv_in{l}", f"conv_w_in{l}", rec["u"], da, split=True)
            name = f"conv_in_bwd{l}"
            (dh, small[f"norm_mix{l}"]), got, _ = proj_bwd(da, mats["conv_w_in", l], rec["h"], w["norm_mix"], l, dh1, tm, name,
                                                           ride(name))
            landed_x(name, got)
    grad_x, small["meta_tokens"] = input_grads(dh.reshape(bl, lp, D), seq)
    return loss_blk, grad_x, big, arrived, small


def all_gather_blocks(blocks):
    n = len(blocks)

    def body(*refs):
        srcs, outs, sems = refs[:n], refs[n:2 * n], refs[2 * n:]
        _gat_start(srcs, outs, sems)
        _gat_forward(srcs, outs, sems)
        _gat_wait(srcs, outs, sems)

    any_spec = pl.BlockSpec(memory_space=pl.ANY)
    return pl.pallas_call(
        body, name="all_gather_blocks", out_shape=[_sds((NDEV,) + tuple(a.shape), a.dtype) for a in blocks],
        in_specs=[any_spec] * n, out_specs=[any_spec] * n, scratch_shapes=_xch_scratch(n),
    )(*blocks)


def cast_bf16(ws):
    n = len(ws)
    counts = [1 if x.ndim == 2 else x.shape[0] for x in ws]

    def body(*refs):
        outs = iter(refs[n:])
        for a in range(n):
            for l in range(counts[a]):
                next(outs)[...] = (refs[a][...] if ws[a].ndim == 2 else refs[a][l]).astype(BF)

    flat = pl.pallas_call(
        body, name="cast_bf16", out_shape=[_sds(x.shape[-2:], BF) for x, k in zip(ws, counts) for _ in range(k)],
        compiler_params=pltpu.CompilerParams(vmem_limit_bytes=VMEM_LIMIT),
    )(*ws)
    it = iter(flat)
    return [[next(it) for _ in range(k)] for k in counts]


def join_columns(w8, name):
    _, kk, n8 = w8.shape

    def body(x_ref, o_ref):
        o_ref[...] = jnp.concatenate([x_ref[p] for p in range(NDEV)], axis=1)

    return pl.pallas_call(body, name=name, out_shape=_sds((kk, NDEV * n8), w8.dtype),
                          compiler_params=pltpu.CompilerParams(vmem_limit_bytes=VMEM_LIMIT))(w8)


def _adamw_math(w, m, v, g):
    m2 = B1 * m + (1.0 - B1) * g
    v2 = B2 * v + (1.0 - B2) * (g * g)
    mh = m2 / (1.0 - B1 ** STEP)
    vh = v2 / (1.0 - B2 ** STEP)
    return -LR * (mh / (jnp.sqrt(vh) + AEPS) + WD * w), m2, v2


def adamw_big(w, m, v, parts, name, xch=(), gat=()):
    lyr, r, c = w.shape
    by_cols = c >= 512
    blk = (lyr, r, 256) if by_cols else (lyr, 256 if r % 256 == 0 else r, c)
    imap = (lambda i: (0, 0, i)) if by_cols else (lambda i: (0, i, 0))
    counts = [len(per_layer) for per_layer in parts]

    def body(w_ref, m_ref, v_ref, *rest):
        p_refs, (g_ref, d_ref, m2_ref, v2_ref) = iter(rest[:sum(counts)]), rest[sum(counts):]
        for l in range(lyr):
            g = None
            for _ in range(counts[l]):
                ref = next(p_refs)
                for q in range(ref.shape[0]):
                    g = ref[q].astype(F32) if g is None else g + ref[q].astype(F32)
            g_ref[l] = g
            d_ref[l], m2_ref[l], v2_ref[l] = _adamw_math(w_ref[l], m_ref[l], v_ref[l], g)

    spec = pl.BlockSpec(blk, imap)
    flat = [a for per_layer in parts for a in per_layer]
    pspecs = [pl.BlockSpec((a.shape[0],) + blk[1:], imap) for a in flat]
    return _call(body, name, ((c // 256) if by_cols else (r // blk[1]),), [spec, spec, spec] + pspecs,
                 [spec] * 4, [_sds((lyr, r, c), F32)] * 4, (w, m, v, *flat), xch=xch, gat=gat)


SMALL_ROWS = 104
REPLICATED = {"norm_mix": (0, 4, D), "norm_ffn": (4, 4, D), "kv_norm": (8, 1, D), "k_norm": (9, 1, HD), "q_norm": (10, 2, HD),
              "attn_sinks": (12, 2, NH)}
LOSS_ROW = 14
SHARDED = {"meta_tokens": (16, NMETA), "conv_b_in": (32, 4), "conv_dw": (36, 2 * CW), "conv_ln_g": (98, 2), "conv_ln_b": (100, 2),
           "conv_b_out": (102, 2)}


def pack_small(gs, loss_blk):
    order = ([f"norm_mix{l}" for l in range(4)] + [f"norm_ffn{l}" for l in range(4)] + ["kv_norm", "k_norm", "q_norm0", "q_norm1",
             "attn_sinks0", "attn_sinks1", "meta_tokens", "conv_b_in0", "conv_b_in1", "conv_dw0", "conv_dw1", "conv_ln_g0",
             "conv_ln_g1", "conv_ln_b0", "conv_ln_b1", "conv_b_out0", "conv_b_out1"])

    def body(*refs):
        r = dict(zip(order, refs))
        loss_ref, o_ref = refs[len(order)], refs[len(order) + 1]
        o_ref[...] = jnp.zeros_like(o_ref)
        for l in range(4):
            o_ref[l:l + 1, :] = r[f"norm_mix{l}"][...]
            o_ref[4 + l:5 + l, :] = r[f"norm_ffn{l}"][...]
        o_ref[8:9, :] = r["kv_norm"][...]
        o_ref[9:10, 0:HD] = r["k_norm"][...]
        for j in range(2):
            o_ref[10 + j:11 + j, 0:HD] = r[f"q_norm{j}"][...]
            o_ref[12 + j:13 + j, 0:NH] = r[f"attn_sinks{j}"][...]
            o_ref[32 + 2 * j:33 + 2 * j, :] = r[f"conv_b_in{j}"][:, 0:D]
            o_ref[33 + 2 * j:34 + 2 * j, :] = r[f"conv_b_in{j}"][:, D:2 * D]
            o_ref[36 + CW * j:36 + CW * (j + 1), :] = r[f"conv_dw{j}"][0:CW, :]
            o_ref[98 + j:99 + j, :] = r[f"conv_ln_g{j}"][...]
            o_ref[100 + j:101 + j, :] = r[f"conv_ln_b{j}"][...]
            o_ref[102 + j:103 + j, :] = r[f"conv_b_out{j}"][...]
        o_ref[LOSS_ROW:LOSS_ROW + 1, 0:1] = loss_ref[0:1, 0:1]
        o_ref[16:16 + NMETA, :] = r["meta_tokens"][...]

    return pl.pallas_call(body, name="pack_small", out_shape=_sds((SMALL_ROWS, D), F32))(*[gs[k] for k in order], loss_blk)


def adamw_small(g8, wts, mom, var):
    names = list(REPLICATED) + list(SHARDED)
    shape2 = {"kv_norm": (1, D), "k_norm": (1, HD)}
    ins = [a[k].reshape(shape2.get(k, a[k].shape)) for a in (wts, mom, var) for k in names]
    n = len(names)

    def body(*refs):
        g8_ref, w_refs, m_refs, v_refs = refs[0], refs[1:1 + n], refs[1 + n:1 + 2 * n], refs[1 + 2 * n:1 + 3 * n]
        loss_ref, outs, red_ref = refs[1 + 3 * n], refs[2 + 3 * n:-1], refs[-1]
        me = _my_index()
        acc = g8_ref[0]
        for q in range(1, NDEV):
            acc = acc + g8_ref[q]
        red_ref[...] = acc
        loss_ref[...] = red_ref[LOSS_ROW:LOSS_ROW + 1, 0:1]

        def mine(rows, width):
            acc = jnp.zeros((rows.stop - rows.start, width), F32)
            for p_ in range(NDEV):
                acc = acc + jnp.where(me == p_, red_ref[rows, p_ * width:(p_ + 1) * width], 0.0)
            return acc

        for i, k in enumerate(names):
            if k in REPLICATED:
                r0, nr, width = REPLICATED[k]
                g = red_ref[r0:r0 + nr, 0:width]
            elif k == "conv_b_in":
                half = D // (2 * D // NDEV)
                acc = jnp.zeros((2, 2 * D // NDEV), F32)
                for p_ in range(NDEV):
                    c0 = (p_ % half) * (2 * D // NDEV)
                    part = jnp.concatenate([red_ref[32 + 2 * j + p_ // half:33 + 2 * j + p_ // half, c0:c0 + 2 * D // NDEV]
                                            for j in range(2)], axis=0)
                    acc = acc + jnp.where(me == p_, part, 0.0)
                g = acc
            else:
                r0, nr = SHARDED[k]
                g = mine(slice(r0, r0 + nr), D // NDEV)
            w_, m_, v_ = w_refs[i], m_refs[i], v_refs[i]
            g_out, d_out, m_out, v_out = outs[4 * i:4 * i + 4]
            if k == "conv_dw":
                for j in range(2):
                    gj = g[CW * j:CW * (j + 1)]
                    g_out[j] = gj
                    d_out[j], m_out[j], v_out[j] = _adamw_math(w_[j], m_[j], v_[j], gj)
            else:
                g_out[...] = g
                d_out[...], m_out[...], v_out[...] = _adamw_math(w_[...], m_[...], v_[...], g)

    out_shape = [_sds((1, 1), F32)] + [_sds(ins[i].shape, F32) for i in range(n) for _ in range(4)]
    res = pl.pallas_call(body, name="adamw_small", out_shape=out_shape, scratch_shapes=[pltpu.VMEM((SMALL_ROWS, D), F32)])(g8, *ins)
    out = {k: tuple(o.reshape(wts[k].shape) for o in res[1 + 4 * i:5 + 4 * i]) for i, k in enumerate(names)}
    return res[0], out


NAMES = ["meta_tokens", "norm_mix", "norm_ffn", "conv_w_in", "conv_b_in", "conv_dw", "conv_ln_g", "conv_ln_b", "conv_w_out",
         "conv_b_out", "kv_norm", "w_kv", "k_norm", "w_q", "q_norm", "attn_sinks", "w_o", "ffn_w_gate", "ffn_w_up", "ffn_w_down"]


def kernel(x, meta_tokens, norm_mix, norm_ffn, conv_w_in, conv_b_in, conv_dw, conv_ln_g, conv_ln_b, conv_w_out, conv_b_out, kv_norm, w_kv, k_norm, w_q, q_norm, attn_sinks, w_o, ffn_w_gate, ffn_w_up, ffn_w_down, loss_target, m_meta_tokens, m_norm_mix, m_norm_ffn, m_conv_w_in, m_conv_b_in, m_conv_dw, m_conv_ln_g, m_conv_ln_b, m_conv_w_out, m_conv_b_out, m_kv_norm, m_w_kv, m_k_norm, m_w_q, m_q_norm, m_attn_sinks, m_w_o, m_ffn_w_gate, m_ffn_w_up, m_ffn_w_down, v_meta_tokens, v_norm_mix, v_norm_ffn, v_conv_w_in, v_conv_b_in, v_conv_dw, v_conv_ln_g, v_conv_ln_b, v_conv_w_out, v_conv_b_out, v_kv_norm, v_w_kv, v_k_norm, v_w_q, v_q_norm, v_attn_sinks, v_w_o, v_ffn_w_gate, v_ffn_w_up, v_ffn_w_down):
    wts = dict(zip(NAMES, (meta_tokens, norm_mix, norm_ffn, conv_w_in, conv_b_in, conv_dw, conv_ln_g, conv_ln_b, conv_w_out,
                           conv_b_out, kv_norm, w_kv, k_norm, w_q, q_norm, attn_sinks, w_o, ffn_w_gate, ffn_w_up, ffn_w_down)))
    mom = dict(zip(NAMES, (m_meta_tokens, m_norm_mix, m_norm_ffn, m_conv_w_in, m_conv_b_in, m_conv_dw, m_conv_ln_g, m_conv_ln_b,
                           m_conv_w_out, m_conv_b_out, m_kv_norm, m_w_kv, m_k_norm, m_w_q, m_q_norm, m_attn_sinks, m_w_o,
                           m_ffn_w_gate, m_ffn_w_up, m_ffn_w_down)))
    var = dict(zip(NAMES, (v_meta_tokens, v_norm_mix, v_norm_ffn, v_conv_w_in, v_conv_b_in, v_conv_dw, v_conv_ln_g, v_conv_ln_b,
                           v_conv_w_out, v_conv_b_out, v_kv_norm, v_w_kv, v_k_norm, v_w_q, v_q_norm, v_attn_sinks, v_w_o,
                           v_ffn_w_gate, v_ffn_w_up, v_ffn_w_down)))
    for k in TRANSPOSED:
        wts[k], mom[k], var[k] = (jnp.swapaxes(a, 1, 2) for a in (wts[k], mom[k], var[k]))

    big_names = list(BIG)
    layers = cast_bf16([wts[k] for k in big_names])
    shards = {(k, l): blk for k, per_layer in zip(big_names, layers) for l, blk in enumerate(per_layer)}
    vec_names = ["meta_tokens", "conv_b_in", "conv_dw", "conv_ln_g", "conv_ln_b", "conv_b_out"]
    full = dict(zip(vec_names, all_gather_blocks([wts[k] for k in vec_names])))
    join_vec = lambda a: jnp.moveaxis(a, 0, -2).reshape(a.shape[1:-1] + (NDEV * a.shape[-1],))
    w = {}
    w["conv_b_in"] = join_vec(full["conv_b_in"]).reshape(2, 1, 2 * D)
    w["conv_dw"] = join_vec(full["conv_dw"])
    for k in ("conv_ln_g", "conv_ln_b", "conv_b_out"):
        w[k] = join_vec(full[k]).reshape(2, 1, D)
    w["norm_mix"] = norm_mix.reshape(4, 1, D)
    w["norm_ffn"] = norm_ffn.reshape(4, 1, D)
    w["kv_norm"] = kv_norm.reshape(1, D)
    w["k_norm"] = k_norm.reshape(1, HD)
    w["q_norm"] = q_norm.reshape(2, 1, HD)
    w["attn_sinks"] = attn_sinks.reshape(2, 1, NH)

    loss_blk, grad_x, gbig, arrived, gs = local_step(x, loss_target, full["meta_tokens"], w, shards)

    packed = pack_small(gs, loss_blk)

    grads, delta, new_m, new_v = {}, {}, {}, {}
    tail = EXCHANGE_PLAN["tail"]
    waiting = {nm.rstrip("0123456789") for nm, _ in tail}
    order = sorted([k for k in big_names if k not in waiting], key=lambda k: -wts[k].size) + [k for k in big_names if k in waiting]
    small8 = None
    for pos, k in enumerate(order):
        flat2 = wts[k].ndim == 2
        as3 = (lambda a: a[None]) if flat2 else (lambda a: a)
        riders = tail if pos == 0 else []
        gat = [packed] if pos == 1 else []
        parts = [arrived[k]] if flat2 else [arrived[f"{k}{i}"] for i in range(wts[k].shape[0])]
        outs, got_x, got_g = adamw_big(as3(wts[k]), as3(mom[k]), as3(var[k]), parts, "adamw_" + k,
                                       xch=[(gbig[nm], EXCHANGE_KIND[nm.rstrip("0123456789")], ks) for nm, ks in riders], gat=gat)
        for (nm, _), got in zip(riders, got_x):
            arrived[nm].append(got)
        if gat:
            small8 = got_g[0]
        grads[k], delta[k], new_m[k], new_v[k] = [o[0] if flat2 else (jnp.swapaxes(o, 1, 2) if k in TRANSPOSED else o) for o in outs]
    loss, small = adamw_small(small8, wts, mom, var)
    for k, (g_, d_, m_, v_) in small.items():
        grads[k], delta[k], new_m[k], new_v[k] = g_, d_, m_, v_
    return (loss.reshape(()), grad_x, *[grads[k] for k in NAMES], *[delta[k] for k in NAMES], *[new_m[k] for k in NAMES],
            *[new_v[k] for k in NAMES])
```

```python
import jax
import jax.numpy as jnp
from jax import lax
from jax.experimental import pallas as pl
from jax.experimental.pallas import tpu as pltpu

F32 = jnp.float32
BF = jnp.bfloat16

D = 1024
DFF = 2816
NH = 16
NKV = 4
HD = 64
KVD = NKV * HD
NMETA = 16
CW = 31
HALO = 32
CHUNK = 32
QB = 128
EPS = 1e-6
NEG = -1e30
NDEV = 8
SCALE = HD ** -0.5

LR, B1, B2, AEPS, WD, STEP = 0.001, 0.9, 0.999, 1e-08, 0.01, 10

VMEM_LIMIT = 56 * 2 ** 20
MESH = pl.DeviceIdType.MESH


def _cp(n):
    return pltpu.CompilerParams(dimension_semantics=("arbitrary",) * n, vmem_limit_bytes=VMEM_LIMIT)


def _row(tm, c):
    return pl.BlockSpec((tm, c), lambda i: (i, 0))


def _res(shape):
    return pl.BlockSpec(shape, lambda i: (0,) * len(shape), pipeline_mode=pl.Buffered(1))


def _lay(l, shape):
    return pl.BlockSpec((None,) + tuple(shape), lambda i: (l,) + (0,) * len(shape), pipeline_mode=pl.Buffered(1))


def _acc(shape):
    return pl.BlockSpec(shape, lambda i: (0,) * len(shape))


def _sds(shape, dt):
    return jax.ShapeDtypeStruct(tuple(shape), dt)


def _dot(a, b):
    return jnp.dot(a.astype(BF), b.astype(BF), preferred_element_type=F32)


def _dot_nt(a, b):
    return lax.dot_general(a.astype(BF), b.astype(BF), (((1,), (1,)), ((), ())), preferred_element_type=F32)


def _dot_tn(a, b):
    return lax.dot_general(a.astype(BF), b.astype(BF), (((0,), (0,)), ((), ())), preferred_element_type=F32)


def _rstd(x):
    return lax.rsqrt(jnp.mean(x * x, axis=-1, keepdims=True) + EPS)


def _rms_bwd(x, g, dy):
    r = _rstd(x)
    z = dy * g
    dx = r * z - x * (r * r * r * jnp.mean(z * x, axis=-1, keepdims=True))
    return dx, jnp.sum(dy * x * r, axis=0, keepdims=True)


def _sig(x):
    return jax.nn.sigmoid(x)


def _fold8(x):
    out = x[0:8]
    for k in range(1, x.shape[0] // 8):
        out = out + x[8 * k:8 * k + 8]
    return out


def _shifted(win):
    return [win] + [pltpu.roll(win, 2 * CHUNK - rho, 0) for rho in range(1, 8)]


def _tap(phases, o):
    return phases[o % 8][8 * (o // 8):8 * (o // 8) + CHUNK]


def _spread_taps(dw_ref, taps):
    @pl.when(pl.program_id(0) == 0)
    def _():
        for j in range(CW):
            taps[j] = jnp.broadcast_to(dw_ref[j:j + 1, :], taps.shape[1:])


def _tap_weight(taps, j):
    return jnp.concatenate([taps[j]] * (CHUNK // 8), axis=0)


def _init(ref, first):
    @pl.when(first)
    def _():
        ref[...] = jnp.zeros_like(ref)


def _my_index():
    return 4 * lax.axis_index("x") + 2 * lax.axis_index("y") + lax.axis_index("c")


def _coords(idx):
    return (idx // 4, (idx // 2) % 2, idx % 2)


ALL = tuple(range(NDEV))
H1, H2 = (0, 1, 2, 4, 6), (3, 5, 7)


def _xch_shapes(xch):
    return [_sds((len(ks),) + ((a.shape[0] // NDEV, a.shape[1]) if k == "rows" else tuple(a.shape[1:])), a.dtype) for a, k, ks in xch]


def _xch_scratch(n):
    return [pltpu.SemaphoreType.DMA((n, NDEV)), pltpu.SemaphoreType.DMA((n, NDEV)), pltpu.SemaphoreType.DMA((n,))]


def _xch_copies(meta, srcs, outs, sems, arrivals):
    send_sems, recv_sems, local_sems = sems
    me = _my_index()

    def piece(a, p):
        if meta[a][0] == "rows":
            r = srcs[a].shape[0] // NDEV
            return srcs[a].at[pl.ds(p * r, r), :]
        return srcs[a].at[p]

    def remote(a, i, k, src):
        return pltpu.make_async_remote_copy(
            src_ref=src, dst_ref=outs[a].at[i], send_sem=send_sems.at[a, k], recv_sem=recv_sems.at[a, k],
            device_id=_coords(me ^ k), device_id_type=MESH)

    local, sends, recvs = [], [], []
    for a, (_, ks) in enumerate(meta):
        for i, k in enumerate(ks):
            if k == 0:
                local.append(pltpu.make_async_copy(piece(a, me), outs[a].at[i], local_sems.at[a]))
            else:
                sends.append(remote(a, i, k, piece(a, me ^ k)))
                if arrivals:
                    recvs.append(remote(a, i, k, piece(a, me)))
    return local, sends, recvs


def _xch_start(meta, srcs, outs, sems):
    local, sends, _ = _xch_copies(meta, srcs, outs, sems, False)
    for cp in local + sends:
        cp.start()


def _xch_wait(meta, srcs, outs, sems):
    local, sends, recvs = _xch_copies(meta, srcs, outs, sems, True)
    for cp in recvs:
        cp.wait_recv()
    for cp in sends:
        cp.wait_send()
    for cp in local:
        cp.wait()


def _gat_copies(srcs, outs, sems):
    send_sems, recv_sems, local_sems = sems
    x, y, c = lax.axis_index("x"), lax.axis_index("y"), lax.axis_index("c")
    me, sibling = (x, y, c), (x, y, 1 - c)
    chips = [(1 - x, y), (x, 1 - y), (1 - x, 1 - y)]

    def copy(a, k, owner, to, from_block=False):
        slot = outs[a].at[4 * owner[0] + 2 * owner[1] + owner[2]]
        return pltpu.make_async_remote_copy(
            src_ref=srcs[a] if from_block else slot, dst_ref=slot, send_sem=send_sems.at[a, k], recv_sem=recv_sems.at[a, k],
            device_id=to, device_id_type=MESH)

    n = len(srcs)
    local = lambda: [pltpu.make_async_copy(srcs[a], outs[a].at[4 * x + 2 * y + c], local_sems.at[a]) for a in range(n)]
    first = lambda: [cp for a in range(n) for cp in
                     [copy(a, 0, me, sibling, True)] + [copy(a, 1 + j, me, (*chip, c), True) for j, chip in enumerate(chips)]]
    landed = lambda: [copy(a, 1 + j, (*chip, c), me) for a in range(n) for j, chip in enumerate(chips)]
    passed = lambda: [copy(a, 4 + j, (*chip, c), sibling) for a in range(n) for j, chip in enumerate(chips)]
    final = lambda: [cp for a in range(n) for cp in
                     [copy(a, 0, sibling, me)] + [copy(a, 4 + j, (*chip, 1 - c), me) for j, chip in enumerate(chips)]]
    return local, first, landed, passed, final


def _gat_start(srcs, outs, sems):
    local, first, _, _, _ = _gat_copies(srcs, outs, sems)
    for cp in local() + first():
        cp.start()


def _gat_forward(srcs, outs, sems):
    _, _, landed, passed, _ = _gat_copies(srcs, outs, sems)
    for got, on in zip(landed(), passed()):
        got.wait_recv()
        on.start()


def _gat_wait(srcs, outs, sems):
    local, first, _, passed, final = _gat_copies(srcs, outs, sems)
    for cp in final():
        cp.wait_recv()
    for cp in first() + passed():
        cp.wait_send()
    for cp in local():
        cp.wait()


def _call(body, name, grid, in_specs, out_specs, out_shape, args, scratch=(), xch=(), gat=()):
    n_in, n_out, n_x, n_g, n_s = len(in_specs), len(out_specs), len(xch), len(gat), len(scratch)
    kinds = [(k, ks) for _, k, ks in xch]
    total = 1
    for g in grid:
        total *= g

    def wrapped(*refs):
        ins, refs = refs[:n_in], refs[n_in:]
        x_src, refs = refs[:n_x], refs[n_x:]
        g_src, refs = refs[:n_g], refs[n_g:]
        outs, refs = refs[:n_out], refs[n_out:]
        x_out, refs = refs[:n_x], refs[n_x:]
        g_out, refs = refs[:n_g], refs[n_g:]
        own, refs = refs[:n_s], refs[n_s:]
        x_sems, g_sems = (refs[:3], refs[3:]) if n_x else ((), refs)
        step = pl.program_id(0)
        for d in range(1, len(grid)):
            step = step * grid[d] + pl.program_id(d)
        if n_x or n_g:
            @pl.when(step == 0)
            def _():
                if n_x:
                    _xch_start(kinds, x_src, x_out, x_sems)
                if n_g:
                    _gat_start(g_src, g_out, g_sems)

        body(*ins, *outs, *own)
        if n_g:
            @pl.when(step == max(total - 2, 0))
            def _():
                _gat_forward(g_src, g_out, g_sems)

        if n_x or n_g:
            @pl.when(step == total - 1)
            def _():
                if n_x:
                    _xch_wait(kinds, x_src, x_out, x_sems)
                if n_g:
                    _gat_wait(g_src, g_out, g_sems)

    any_spec = pl.BlockSpec(memory_space=pl.ANY)
    g_shapes = [_sds((NDEV,) + tuple(a.shape), a.dtype) for a in gat]
    res = pl.pallas_call(
        wrapped, name=name, grid=grid, in_specs=list(in_specs) + [any_spec] * (n_x + n_g),
        out_specs=list(out_specs) + [any_spec] * (n_x + n_g), out_shape=list(out_shape) + _xch_shapes(xch) + g_shapes,
        scratch_shapes=list(scratch) + (_xch_scratch(n_x) if n_x else []) + (_xch_scratch(n_g) if n_g else []),
        compiler_params=_cp(len(grid)),
    )(*args, *[a for a, _, _ in xch], *gat)
    return res[:n_out], res[n_out:n_out + n_x], res[n_out + n_x:]


def embed(x, meta8, lp, gat):
    bl, seq, _ = x.shape
    c8 = D // NDEV
    cb = 2 * c8

    def body(x_ref, m_ref, h_ref):
        h_ref[0:NMETA, :] = jnp.concatenate([m_ref[0], m_ref[1]], axis=1)
        h_ref[NMETA:NMETA + seq, :] = x_ref[...]
        h_ref[NMETA + seq:, :] = jnp.zeros((lp - NMETA - seq, cb), F32)

    (h0,), _, got = _call(
        body, "embed", (bl, D // cb),
        [pl.BlockSpec((None, seq, cb), lambda b, c: (b, 0, c)), pl.BlockSpec((2, NMETA, c8), lambda b, c: (c, 0, 0))],
        [pl.BlockSpec((None, lp, cb), lambda b, c: (b, 0, c))], [_sds((bl, lp, D), F32)], (x, meta8), gat=gat)
    return h0, got


def conv_in_fwd(h, nm, l, w_in, b_in, i, tm, gat):
    t = h.shape[0]

    def body(h_ref, g_ref, w_ref, b_ref, u_ref, big_ref, a_ref):
        x = h_ref[...]
        ub = (x * _rstd(x) * g_ref[...]).astype(BF)
        u_ref[...] = ub
        big = jnp.dot(ub, w_ref[...], preferred_element_type=F32) + b_ref[...]
        big_ref[...] = big.astype(BF)
        a_ref[...] = big[:, :D] * _sig(big[:, D:])

    return _call(
        body, f"conv_in_fwd{i}", (t // tm,), [_row(tm, D), _lay(l, (1, D)), _res((D, 2 * D)), _lay(i, (1, 2 * D))],
        [_row(tm, D), _row(tm, 2 * D), _row(tm, D)], [_sds((t, D), BF), _sds((t, 2 * D), BF), _sds((t, D), F32)],
        (h, nm, w_in, b_in), gat=gat)


def _prev_halo(tm):
    return pl.BlockSpec((HALO, D), lambda i: (jnp.maximum(i * (tm // HALO) - 1, 0), 0))


def _next_halo(tm, t):
    return pl.BlockSpec((HALO, D), lambda i: (jnp.minimum((i + 1) * (tm // HALO), t // HALO - 1), 0))


def conv_mid_fwd(a, dw, ln_g, ln_b, i, tm, tpb, gat):
    t = a.shape[0]

    def body(a_ref, halo_ref, dw_ref, g_ref, b_ref, c_ref, s_ref, ext, taps):
        _spread_taps(dw_ref, taps)
        first = pl.program_id(0) % tpb == 0
        ext[0:HALO] = jnp.where(first, 0.0, halo_ref[...])
        ext[HALO:] = a_ref[...]

        def chunk(k, carry):
            r0 = pl.multiple_of(k * CHUNK, CHUNK)
            win = _shifted(ext[pl.ds(r0, 2 * CHUNK), :])
            c = jnp.zeros((CHUNK, D), F32)
            for j in range(CW):
                c = c + _tap_weight(taps, j) * _tap(win, j + 2)
            c_ref[pl.ds(r0, CHUNK), :] = c
            mu = jnp.mean(c, axis=-1, keepdims=True)
            xc = c - mu
            n = xc * lax.rsqrt(jnp.mean(xc * xc, axis=-1, keepdims=True) + EPS) * g_ref[...] + b_ref[...]
            s_ref[pl.ds(r0, CHUNK), :] = (n * _sig(n)).astype(BF)
            return carry

        lax.fori_loop(0, tm // CHUNK, chunk, 0, unroll=2)

    return _call(
        body, f"conv_mid_fwd{i}", (t // tm,),
        [_row(tm, D), _prev_halo(tm), _lay(i, (CW, D)), _lay(i, (1, D)), _lay(i, (1, D))],
        [_row(tm, D), _row(tm, D)], [_sds((t, D), F32), _sds((t, D), BF)], (a, a, dw, ln_g, ln_b),
        scratch=[pltpu.VMEM((tm + HALO, D), F32), pltpu.VMEM((CW, 8, D), F32)], gat=gat)


def mixer_ffn_fwd(h, s, w_out, lw, bias, nf, l, wg, wu, wd, tm, gat):
    t = h.shape[0]

    def body(*refs):
        if bias is None:
            h_ref, s_ref, w_ref, nf_ref, wg_ref, wu_ref, wd_ref, h1_ref, u_ref, g_ref, up_ref, hid_ref, h2_ref = refs
            y = 0.0
        else:
            h_ref, s_ref, w_ref, b_ref, nf_ref, wg_ref, wu_ref, wd_ref, h1_ref, u_ref, g_ref, up_ref, hid_ref, h2_ref = refs
            y = b_ref[...]
        h1 = h_ref[...] + (jnp.dot(s_ref[...], w_ref[...], preferred_element_type=F32) + y)
        h1_ref[...] = h1
        ub = (h1 * _rstd(h1) * nf_ref[...]).astype(BF)
        u_ref[...] = ub
        g = _dot_nt(ub, wg_ref[...])
        up = _dot_nt(ub, wu_ref[...])
        g_ref[...] = g.astype(BF)
        up_ref[...] = up.astype(BF)
        hid = (g * _sig(g) * up).astype(BF)
        hid_ref[...] = hid
        h2_ref[...] = h1 + jnp.dot(hid, wd_ref[...], preferred_element_type=F32)

    ins = [h, s, w_out] + ([] if bias is None else [bias]) + [nf, wg, wu, wd]
    specs = ([_row(tm, D), _row(tm, D), _res((D, D))] + ([] if bias is None else [_lay(lw, (1, D))])
             + [_lay(l, (1, D)), _res((DFF, D)), _res((DFF, D)), _res((DFF, D))])
    return _call(
        body, f"mixer_ffn_fwd{l}", (t // tm,), specs,
        [_row(tm, D), _row(tm, D), _row(tm, DFF), _row(tm, DFF), _row(tm, DFF), _row(tm, D)],
        [_sds((t, D), F32), _sds((t, D), BF), _sds((t, DFF), BF), _sds((t, DFF), BF), _sds((t, DFF), BF), _sds((t, D), F32)],
        ins, gat=gat)


def _seg_rms(x, g, nseg):
    outs = []
    for s in range(nseg):
        xs = x[:, HD * s:HD * s + HD]
        outs.append(xs * _rstd(xs) * g)
    return jnp.concatenate(outs, axis=1)


def kv_fwd(h, kvn, w_kv, kng, tm):
    t = h.shape[0]

    def body(h_ref, g_ref, w_ref, kg_ref, kn_ref, kv_ref, k_ref, v_ref):
        x = h_ref[...]
        kn = (x * _rstd(x) * g_ref[...]).astype(BF)
        kn_ref[...] = kn
        kv = jnp.dot(kn, w_ref[...], preferred_element_type=F32)
        kv_ref[...] = kv
        k_ref[...] = _seg_rms(kv[:, :KVD], kg_ref[...], NKV).astype(BF)
        v_ref[...] = kv[:, KVD:].astype(BF)

    return pl.pallas_call(
        body, name="kv_fwd", grid=(t // tm,),
        in_specs=[_row(tm, D), _res((1, D)), _res((D, 2 * KVD)), _res((1, HD))],
        out_specs=[_row(tm, D), _row(tm, 2 * KVD), _row(tm, KVD), _row(tm, KVD)],
        out_shape=[_sds((t, D), BF), _sds((t, 2 * KVD), F32), _sds((t, KVD), BF), _sds((t, KVD), BF)],
        compiler_params=_cp(1),
    )(h, kvn, w_kv, kng)


def q_fwd(h, nm, l, w_q, j, tm):
    t = h.shape[0]

    def body(h_ref, g_ref, w_ref, u_ref, q_ref):
        x = h_ref[...]
        ub = (x * _rstd(x) * g_ref[...]).astype(BF)
        u_ref[...] = ub
        q_ref[...] = jnp.dot(ub, w_ref[...], preferred_element_type=F32)

    return pl.pallas_call(
        body, name=f"q_fwd{j}", grid=(t // tm,),
        in_specs=[_row(tm, D), _lay(l, (1, D)), _res((D, D))],
        out_specs=[_row(tm, D), _row(tm, D)], out_shape=[_sds((t, D), BF), _sds((t, D), F32)],
        compiler_params=_cp(1),
    )(h, nm, w_q)


RQ = NH // NKV


NKEYS = 2 * QB + NMETA


def _attn_mask(n, start):
    shape = (RQ * QB, NKEYS)
    qpos = n * QB + (lax.broadcasted_iota(jnp.int32, shape, 0) & (QB - 1))
    col = lax.broadcasted_iota(jnp.int32, shape, 1)
    in_band = col < 2 * QB
    kpos = jnp.where(in_band, start + col, col - 2 * QB)
    return (kpos <= qpos) & ((col >= 2 * QB) | ((qpos - kpos < QB) & (kpos >= NMETA)))


def _keys(ref, band, gs):
    return jnp.concatenate([ref[band, gs], ref[0:NMETA, gs]], axis=0)


def _keys_t(ref, band, gs):
    return jnp.concatenate([ref[gs, band], ref[gs, 0:NMETA]], axis=1)


def transpose_seq(a, name):
    bl, r, c = a.shape

    def body(a_ref, o_ref):
        o_ref[...] = a_ref[...].T

    return pl.pallas_call(
        body, name=name, grid=(bl,), in_specs=[pl.BlockSpec((None, r, c), lambda b: (b, 0, 0))],
        out_specs=pl.BlockSpec((None, c, r), lambda b: (b, 0, 0)), out_shape=_sds((bl, c, r), a.dtype), compiler_params=_cp(1),
    )(a)


def sum_transposed(a0, a1):
    bl, c, r = a0.shape

    def body(a0_ref, a1_ref, o_ref):
        o_ref[...] = (a0_ref[...] + a1_ref[...]).T

    spec = pl.BlockSpec((None, c, r), lambda b: (b, 0, 0))
    return pl.pallas_call(
        body, name="sum_transposed", grid=(bl,), in_specs=[spec, spec],
        out_specs=pl.BlockSpec((r, c), lambda b: (b, 0)), out_shape=_sds((bl * r, c), a0.dtype), compiler_params=_cp(1),
    )(a0, a1)


def _stack_heads(ref, g, fn):
    return jnp.concatenate([fn(ref[:, HD * (g * RQ + r):HD * (g * RQ + r) + HD]) for r in range(RQ)], axis=0)


def _stack_cols(ref, g):
    return jnp.concatenate([ref[:, g * RQ + r:g * RQ + r + 1] for r in range(RQ)], axis=0)


def _stack_sinks(sk_ref, g):
    return jnp.concatenate([jnp.broadcast_to(sk_ref[:, g * RQ + r:g * RQ + r + 1], (QB, 1)) for r in range(RQ)], axis=0)


def attn_fwd(q, kt, v, qg, sinks, j, bl, lp, gat):
    t = q.shape[0]
    nb = lp // QB

    def body(q_ref, kt_ref, v_ref, qg_ref, sk_ref, o_ref, lse_ref):
        n = pl.program_id(0)
        start = pl.multiple_of(jnp.maximum(n - 1, 0) * QB, QB)
        mask = _attn_mask(n, start)
        band = pl.ds(start, 2 * QB)
        lane = lax.broadcasted_iota(jnp.int32, (QB, NH), 1)
        ones = jnp.ones((NKEYS, HD), BF)
        pairs = [(b, g) for b in range(bl) for g in range(NKV)]
        gsl = [slice(HD * g, HD * g + HD) for g in range(NKV)]
        qns = [_stack_heads(q_ref.at[b], g, lambda x: (x * _rstd(x) * (qg_ref[...] * SCALE)).astype(BF)) for b, g in pairs]
        ss = [jnp.where(mask, _dot(qns[i], _keys_t(kt_ref.at[b], band, gsl[g])), NEG) for i, (b, g) in enumerate(pairs)]
        sinks = [_stack_sinks(sk_ref, g) for g in range(NKV)]
        mxs = [jnp.maximum(jnp.max(ss[i], -1, keepdims=True), sinks[g]) for i, (b, g) in enumerate(pairs)]
        oas = [_dot(jnp.exp(ss[i] - mxs[i]), jnp.concatenate([_keys(v_ref.at[b], band, gsl[g]), ones], axis=1))
               for i, (b, g) in enumerate(pairs)]
        lses = [jnp.zeros((QB, NH), F32) for _ in range(bl)]
        for i, (b, g) in enumerate(pairs):
            den = oas[i][:, HD:HD + 1] + jnp.exp(sinks[g] - mxs[i])
            o = oas[i][:, :HD] * (1.0 / den)
            l = mxs[i] + jnp.log(den)
            for r in range(RQ):
                h = g * RQ + r
                o_ref[b, :, HD * h:HD * h + HD] = o[r * QB:(r + 1) * QB].astype(BF)
                lses[b] = jnp.where(lane == h, l[r * QB:(r + 1) * QB], lses[b])
        for b in range(bl):
            lse_ref[b] = lses[b]

    blk = lambda c: pl.BlockSpec((bl, QB, c), lambda n: (0, n, 0))
    (o, lse), _, got = _call(
        body, f"attn_fwd{j}", (nb,),
        [blk(D), pl.BlockSpec((bl, KVD, lp), lambda n: (0, 0, 0)), pl.BlockSpec((bl, lp, KVD), lambda n: (0, 0, 0)),
         pl.BlockSpec((None, 1, HD), lambda n: (j, 0, 0)), pl.BlockSpec((None, 1, NH), lambda n: (j, 0, 0))],
        [blk(D), blk(NH)], [_sds((bl, lp, D), BF), _sds((bl, lp, NH), F32)], (q.reshape(bl, lp, D), kt, v, qg, sinks), gat=gat)
    return (o.reshape(t, D), lse.reshape(t, NH)), (), got


def loss_fwd(h, tgt):
    bl, lp, _ = h.shape
    seq = tgt.shape[1]
    cb = 256

    def body(h_ref, t_ref, dh_ref, loss_ref):
        _init(loss_ref, (pl.program_id(0) == 0) & (pl.program_id(1) == 0))
        err = h_ref[NMETA:NMETA + seq, :] - t_ref[...]
        dh_ref[...] = jnp.zeros_like(dh_ref)
        dh_ref[NMETA:NMETA + seq, :] = err * (1.0 / D)
        loss_ref[...] += (0.5 / D) * jnp.sum(err * err)

    return pl.pallas_call(
        body, name="loss_fwd", grid=(bl, D // cb),
        in_specs=[pl.BlockSpec((None, lp, cb), lambda b, c: (b, 0, c)), pl.BlockSpec((None, seq, cb), lambda b, c: (b, 0, c))],
        out_specs=[pl.BlockSpec((None, lp, cb), lambda b, c: (b, 0, c)), pl.BlockSpec((8, 128), lambda b, c: (0, 0))],
        out_shape=[_sds((bl, lp, D), F32), _sds((8, 128), F32)],
        compiler_params=_cp(2),
    )(h, tgt)


def ffn_bwd_x(dh2, g, up, h1, nf, l, wd, wg, wu, w_o, tm, xch):
    t = dh2.shape[0]

    def body(dh2_ref, g_ref, up_ref, h1_ref, nf_ref, wd_ref, wg_ref, wu_ref, *rest):
        if w_o is None:
            dg_ref, du_ref, dh1_ref, dnf_ref = rest
        else:
            wo_ref, dg_ref, du_ref, dh1_ref, dnf_ref, do_ref = rest
        _init(dnf_ref, pl.program_id(0) == 0)
        dh2v = dh2_ref[...]
        dhid = _dot_nt(dh2v, wd_ref[...])
        gv = g_ref[...].astype(F32)
        uv = up_ref[...].astype(F32)
        sg = _sig(gv)
        dgv = (dhid * uv * (sg * (1.0 + gv * (1.0 - sg)))).astype(BF)
        duv = (dhid * (gv * sg)).astype(BF)
        dg_ref[...] = dgv
        du_ref[...] = duv
        dnorm = _dot(dgv, wg_ref[...]) + _dot(duv, wu_ref[...])
        dx, dnf = _rms_bwd(h1_ref[...], nf_ref[...], dnorm)
        dh1 = dh2v + dx
        dh1_ref[...] = dh1
        dnf_ref[...] += dnf
        if w_o is not None:
            do_ref[...] = _dot_nt(dh1, wo_ref[...]).astype(BF)

    attn = w_o is not None
    return _call(
        body, f"ffn_bwd_x{l}", (t // tm,),
        [_row(tm, D), _row(tm, DFF), _row(tm, DFF), _row(tm, D), _lay(l, (1, D)),
         _res((DFF, D)), _res((DFF, D)), _res((DFF, D))] + ([_res((D, D))] if attn else []),
        [_row(tm, DFF), _row(tm, DFF), _row(tm, D), _acc((1, D))] + ([_row(tm, D)] if attn else []),
        [_sds((t, DFF), BF), _sds((t, DFF), BF), _sds((t, D), F32), _sds((1, D), F32)] + ([_sds((t, D), BF)] if attn else []),
        (dh2, g, up, h1, nf, wd, wg, wu) + ((w_o,) if attn else ()), xch=xch)


def mm_tn(x, dy, tm, name, split=False, transposed=False, xch=()):
    t, kk = x.shape
    nn = dy.shape[1]
    n8 = nn // NDEV
    nsteps = t // tm

    def body(x_ref, dy_ref, o_ref, acc):
        i = pl.program_id(0)
        _init(acc, i == 0)
        acc[...] += _dot_tn(x_ref[...], dy_ref[...])

        @pl.when(i == nsteps - 1)
        def _():
            if split:
                for p in range(NDEV):
                    o_ref[p] = acc[:, p * n8:(p + 1) * n8].astype(BF)
            elif transposed:
                o_ref[...] = acc[...].T.astype(BF)
            else:
                o_ref[...] = acc[...].astype(BF)

    oshape = (NDEV, kk, n8) if split else ((nn, kk) if transposed else (kk, nn))
    (out,), got, _ = _call(body, name, (nsteps,), [_row(tm, kk), _row(tm, nn)], [_acc(oshape)], [_sds(oshape, BF)], (x, dy),
                           scratch=[pltpu.VMEM((kk, nn), F32)], xch=xch)
    return out, got


def proj_bwd(dy, w, h, g, lg, dh_in, tm, name, xch=()):
    t = h.shape[0]
    nn = dy.shape[1]
    wspec = _res(w.shape)
    gspec = _res((1, D)) if lg is None else _lay(lg, (1, D))

    def body(dy_ref, w_ref, h_ref, g_ref, dhin_ref, dh_ref, dg_ref):
        _init(dg_ref, pl.program_id(0) == 0)
        du = _dot_nt(dy_ref[...], w_ref[...])
        dx, dg = _rms_bwd(h_ref[...], g_ref[...], du)
        dh_ref[...] = dhin_ref[...] + dx
        dg_ref[...] += dg

    return _call(body, name, (t // tm,), [_row(tm, nn), wspec, _row(tm, D), gspec, _row(tm, D)],
                 [_row(tm, D), _acc((1, D))], [_sds((t, D), F32), _sds((1, D), F32)], (dy, w, h, g, dh_in), xch=xch)


def attn_bwd(q, k, kt, vt, do, o, lse, qg, sinks, j, bl, lp, xch):
    t = q.shape[0]
    nb = lp // QB

    def body(q_ref, k_ref, kt_ref, vt_ref, do_ref, o_ref, lse_ref, qg_ref, sk_ref, dq_ref, dk_ref, dv_ref, dqg_ref, dsk_ref):
        n = pl.program_id(0)
        for ref in (dk_ref, dv_ref, dqg_ref, dsk_ref):
            _init(ref, n == 0)
        start = pl.multiple_of(jnp.maximum(n - 1, 0) * QB, QB)
        mask = _attn_mask(n, start)
        band = pl.ds(start, 2 * QB)
        lane = lax.broadcasted_iota(jnp.int32, (1, NH), 1)
        dqg = jnp.zeros((1, HD), F32)
        dsk = jnp.zeros((1, NH), F32)
        pairs = [(b, g) for b in range(bl) for g in range(NKV)]
        idx = range(len(pairs))
        gsl = [slice(HD * g, HD * g + HD) for g in range(NKV)]
        qhs = [_stack_heads(q_ref.at[b], g, lambda x: x) for b, g in pairs]
        rss = [_rstd(qhs[i]) for i in idx]
        qns = [(qhs[i] * rss[i] * (qg_ref[...] * SCALE)).astype(BF) for i in idx]
        lss = [_stack_cols(lse_ref.at[b], g) for b, g in pairs]
        dohs = [_stack_heads(do_ref.at[b], g, lambda x: x) for b, g in pairs]
        deltas = [jnp.sum(dohs[i].astype(F32) * _stack_heads(o_ref.at[b], g, lambda x: x).astype(F32), axis=-1, keepdims=True)
                  for i, (b, g) in enumerate(pairs)]
        prs = [jnp.where(mask, jnp.exp(_dot(qns[i], _keys_t(kt_ref.at[b], band, gsl[g])) - lss[i]), 0.0)
               for i, (b, g) in enumerate(pairs)]
        dss = [(prs[i] * (_dot(dohs[i], _keys_t(vt_ref.at[b], band, gsl[g])) - deltas[i])).astype(BF)
               for i, (b, g) in enumerate(pairs)]
        for i, (b, g) in enumerate(pairs):
            gs = gsl[g]
            dkt = _dot_tn(qns[i], dss[i])
            dvt = _dot_tn(dohs[i], prs[i])
            dk_ref[b, gs, band] += dkt[:, :2 * QB]
            dv_ref[b, gs, band] += dvt[:, :2 * QB]
            dk_ref[b, gs, 0:NMETA] += dkt[:, 2 * QB:]
            dv_ref[b, gs, 0:NMETA] += dvt[:, 2 * QB:]
        dqns = [_dot(dss[i], _keys(k_ref.at[b], band, gsl[g])) * SCALE for i, (b, g) in enumerate(pairs)]
        for i, (b, g) in enumerate(pairs):
            qh, rs, dqn = qhs[i], rss[i], dqns[i]
            dsink = jnp.exp(_stack_sinks(sk_ref, g) - lss[i]) * deltas[i]
            z = dqn * qg_ref[...]
            dq = rs * z - qh * (rs * rs * rs * jnp.mean(z * qh, axis=-1, keepdims=True))
            dqg = dqg + jnp.sum(dqn * qh * rs, axis=0, keepdims=True)
            for r in range(RQ):
                h = g * RQ + r
                dq_ref[b, :, HD * h:HD * h + HD] = dq[r * QB:(r + 1) * QB]
                dsk = dsk + jnp.where(lane == h, -jnp.sum(dsink[r * QB:(r + 1) * QB]), 0.0)
        dqg_ref[...] += dqg
        dsk_ref[...] += dsk

    blk = lambda c: pl.BlockSpec((bl, QB, c), lambda n: (0, n, 0))
    seq = pl.BlockSpec((bl, lp, KVD), lambda n: (0, 0, 0))
    seq_t = pl.BlockSpec((bl, KVD, lp), lambda n: (0, 0, 0))
    as3 = lambda a: a.reshape(bl, lp, a.shape[-1])
    (dq, dk, dv, dqg, dsk), got, _ = _call(
        body, f"attn_bwd{j}", (nb,),
        [blk(D), seq, seq_t, seq_t, blk(D), blk(D), blk(NH),
         pl.BlockSpec((None, 1, HD), lambda n: (j, 0, 0)), pl.BlockSpec((None, 1, NH), lambda n: (j, 0, 0))],
        [blk(D), seq_t, seq_t, pl.BlockSpec((1, HD), lambda n: (0, 0)), pl.BlockSpec((1, NH), lambda n: (0, 0))],
        [_sds((bl, lp, D), F32), _sds((bl, KVD, lp), F32), _sds((bl, KVD, lp), F32), _sds((1, HD), F32), _sds((1, NH), F32)],
        (as3(q), k, kt, vt, as3(do), as3(o), as3(lse), qg, sinks), xch=xch)
    return (dq.reshape(t, D), dk, dv, dqg, dsk), got, ()


def kv_bwd_pre(dk, dv, kv, kng, tm):
    t = kv.shape[0]

    def body(dk_ref, dv_ref, kv_ref, g_ref, dkv_ref, dg_ref):
        _init(dg_ref, pl.program_id(0) == 0)
        dg = jnp.zeros((1, HD), F32)
        outs = []
        for s in range(NKV):
            sl = slice(HD * s, HD * s + HD)
            dx, dgs = _rms_bwd(kv_ref[:, sl], g_ref[...], dk_ref[:, sl])
            outs.append(dx)
            dg = dg + dgs
        dkv_ref[:, :KVD] = jnp.concatenate(outs, axis=1).astype(BF)
        dkv_ref[:, KVD:] = dv_ref[...].astype(BF)
        dg_ref[...] += dg

    return pl.pallas_call(
        body, name="kv_bwd_pre", grid=(t // tm,),
        in_specs=[_row(tm, KVD)] * 2 + [_row(tm, 2 * KVD), _res((1, HD))],
        out_specs=[_row(tm, 2 * KVD), _acc((1, HD))], out_shape=[_sds((t, 2 * KVD), BF), _sds((1, HD), F32)],
        compiler_params=_cp(1),
    )(dk, dv, kv, kng)


def conv_out_bwd(dh1, c, ln_g, ln_b, w_out, i, tm, xch):
    t = dh1.shape[0]

    def body(dh1_ref, c_ref, g_ref, b_ref, w_ref, dc_ref, dg_ref, db_ref, dbo_ref):
        first = pl.program_id(0) == 0
        _init(dg_ref, first)
        _init(db_ref, first)
        _init(dbo_ref, first)
        dh1v = dh1_ref[...]
        ds = _dot_nt(dh1v, w_ref[...])
        cv = c_ref[...]
        xc = cv - jnp.mean(cv, axis=-1, keepdims=True)
        rstd = lax.rsqrt(jnp.mean(xc * xc, axis=-1, keepdims=True) + EPS)
        xh = xc * rstd
        n = xh * g_ref[...] + b_ref[...]
        sg = _sig(n)
        dn = ds * (sg * (1.0 + n * (1.0 - sg)))
        dxh = dn * g_ref[...]
        dc_ref[...] = rstd * (dxh - jnp.mean(dxh, axis=-1, keepdims=True) - xh * jnp.mean(dxh * xh, axis=-1, keepdims=True))
        dg_ref[...] += jnp.sum(dn * xh, axis=0, keepdims=True)
        db_ref[...] += jnp.sum(dn, axis=0, keepdims=True)
        dbo_ref[...] += jnp.sum(dh1v, axis=0, keepdims=True)

    return _call(
        body, f"conv_out_bwd{i}", (t // tm,), [_row(tm, D), _row(tm, D), _lay(i, (1, D)), _lay(i, (1, D)), _res((D, D))],
        [_row(tm, D), _acc((1, D)), _acc((1, D)), _acc((1, D))], [_sds((t, D), F32)] + [_sds((1, D), F32)] * 3,
        (dh1, c, ln_g, ln_b, w_out), xch=xch)


def conv_mid_bwd(dc, a, big, dw, i, tm, tpb, xch):
    t = dc.shape[0]
    nsteps = t // tm

    def body(dc_ref, nxt_ref, a_ref, prv_ref, big_ref, dw_ref, da_ref, dbin_ref, ddw_ref, dce, ae, wacc, bacc, taps):
        i_ = pl.program_id(0)
        _spread_taps(dw_ref, taps)
        _init(wacc, i_ == 0)
        _init(bacc, i_ == 0)
        dce[0:tm] = dc_ref[...]
        dce[tm:] = jnp.where(i_ % tpb == tpb - 1, 0.0, nxt_ref[...])
        ae[0:HALO] = jnp.where(i_ % tpb == 0, 0.0, prv_ref[...])
        ae[HALO:] = a_ref[...]

        def chunk(k, carry):
            r0 = pl.multiple_of(k * CHUNK, CHUNK)
            wdc = _shifted(dce[pl.ds(r0, 2 * CHUNK), :])
            wa = _shifted(ae[pl.ds(r0, 2 * CHUNK), :])
            dcc = wdc[0][0:CHUNK]
            da = jnp.zeros((CHUNK, D), F32)
            for j in range(CW):
                da = da + _tap_weight(taps, j) * _tap(wdc, CW - 1 - j)
                wacc[j] += _fold8(dcc * _tap(wa, j + 2))
            bv = big_ref[pl.ds(r0, CHUNK), :].astype(F32)
            a1, sg = bv[:, :D], _sig(bv[:, D:])
            d1 = da * sg
            d2 = da * a1 * sg * (1.0 - sg)
            da_ref[pl.ds(r0, CHUNK), 0:D] = d1.astype(BF)
            da_ref[pl.ds(r0, CHUNK), D:2 * D] = d2.astype(BF)
            bacc[:, 0:D] += _fold8(d1)
            bacc[:, D:2 * D] += _fold8(d2)
            return carry

        lax.fori_loop(0, tm // CHUNK, chunk, 0)

        @pl.when(i_ == nsteps - 1)
        def _():
            dbin_ref[...] = jnp.sum(bacc[...], axis=0, keepdims=True)
            ddw_ref[...] = jnp.sum(wacc[...], axis=1)

    return _call(
        body, f"conv_mid_bwd{i}", (nsteps,),
        [_row(tm, D), _next_halo(tm, t), _row(tm, D), _prev_halo(tm), _row(tm, 2 * D), _lay(i, (CW, D))],
        [_row(tm, 2 * D), _acc((1, 2 * D)), _acc((CW + 1, D))],
        [_sds((t, 2 * D), BF), _sds((1, 2 * D), F32), _sds((CW + 1, D), F32)],
        (dc, dc, a, a, big, dw),
        scratch=[pltpu.VMEM((tm + HALO, D), F32), pltpu.VMEM((tm + HALO, D), F32),
                 pltpu.VMEM((CW + 1, 8, D), F32), pltpu.VMEM((8, 2 * D), F32), pltpu.VMEM((CW, 8, D), F32)], xch=xch)


def input_grads(dh0, seq):
    bl, lp, _ = dh0.shape
    cb = 256

    def body(dh_ref, gx_ref, gm_ref):
        _init(gm_ref, pl.program_id(1) == 0)
        gx_ref[...] = dh_ref[NMETA:NMETA + seq, :]
        gm_ref[...] += dh_ref[0:NMETA, :]

    return pl.pallas_call(
        body, name="input_grads", grid=(D // cb, bl),
        in_specs=[pl.BlockSpec((None, lp, cb), lambda c, b: (b, 0, c))],
        out_specs=[pl.BlockSpec((None, seq, cb), lambda c, b: (b, 0, c)), pl.BlockSpec((NMETA, cb), lambda c, b: (0, c))],
        out_shape=[_sds((bl, seq, D), F32), _sds((NMETA, D), F32)],
        compiler_params=_cp(2),
    )(dh0)


GATHER_PLAN = {
    "embed": [("conv_w_in", 0)],
    "conv_in_fwd0": [("ffn_w_down", 0)],
    "conv_mid_fwd0": [("ffn_w_gate", 0), ("ffn_w_up", 0), ("conv_w_out", 0)],
    "mixer_ffn_fwd0": [("conv_w_in", 1), ("conv_w_out", 1), ("ffn_w_gate", 1)],
    "conv_in_fwd1": [("w_kv", 0), ("w_q", 0)],
    "conv_mid_fwd1": [("ffn_w_up", 1), ("ffn_w_down", 1)],
    "mixer_ffn_fwd1": [("w_o", 0), ("ffn_w_down", 2)],
    "attn_fwd0": [("ffn_w_gate", 2), ("ffn_w_up", 2), ("w_q", 1), ("w_o", 1)],
    "attn_fwd1": [("ffn_w_gate", 3), ("ffn_w_up", 3), ("ffn_w_down", 3)],
}
EXCHANGE_PLAN = {
    "attn_bwd1": [("ffn_w_down3", ALL), ("ffn_w_gate3", ALL)],
    "dw_down2": [("w_o1", ALL)],
    "ffn_bwd_x2": [("ffn_w_up3", ALL), ("w_q1", ALL)],
    "attn_bwd0": [("ffn_w_down2", ALL), ("ffn_w_gate2", ALL)],
    "dw_down1": [("w_o0", ALL), ("w_q0", H1)],
    "ffn_bwd_x1": [("ffn_w_up2", ALL), ("w_q0", H2), ("w_kv", ALL)],
    "dw_gate1": [("ffn_w_down1", H1)],
    "dw_up1": [("ffn_w_down1", H2)],
    "conv_mid_bwd1": [("ffn_w_gate1", ALL), ("ffn_w_up1", H1)],
    "conv_in_bwd1": [("ffn_w_up1", H2)],
    "dw_down0": [("conv_w_out1", ALL)],
    "ffn_bwd_x0": [("conv_w_in1", ALL)],
    "dw_gate0": [("ffn_w_down0", H1)],
    "dw_up0": [("ffn_w_down0", H2)],
    "conv_out_bwd0": [("ffn_w_gate0", H1)],
    "conv_mid_bwd0": [("ffn_w_gate0", H2), ("ffn_w_up0", H1), ("conv_w_out0", ALL)],
    "dw_conv_in0": [("ffn_w_up0", H2)],
    "conv_in_bwd0": [("conv_w_in0", H1)],
    "tail": [("conv_w_in0", H2)],
}
BIG = {"conv_w_in": "pieces", "conv_w_out": "rows", "w_kv": "rows", "w_q": "rows", "w_o": "rows",
       "ffn_w_gate": "rows", "ffn_w_up": "rows", "ffn_w_down": "rows"}
EXCHANGE_KIND = BIG
TRANSPOSED = ("ffn_w_gate", "ffn_w_up")


def gathered_matrix(name, layer, blocks8):
    if BIG[name] == "rows":
        return blocks8.reshape(NDEV * blocks8.shape[1], blocks8.shape[2])
    return join_columns(blocks8, f"join_{name}{layer}")


def local_step(x, tgt, meta8, w, shards):
    bl, seq, _ = x.shape
    lp = -(-(NMETA + seq) // QB) * QB
    tpb = 4
    tm = lp // tpb
    t = bl * lp
    na = 2
    flat = lambda a: a.reshape(t, D)
    mats = {}

    def riders(carrier):
        return [shards[key] for key in GATHER_PLAN[carrier]]

    def landed(carrier, blocks):
        for key, b8 in zip(GATHER_PLAN[carrier], blocks):
            mats[key] = gathered_matrix(*key, b8)

    h0, got = embed(x, meta8, lp, riders("embed"))
    landed("embed", got)
    h = flat(h0)
    saved = []
    kvs = None
    for l in range(4):
        rec = {"h": h}
        if l < na:
            name = f"conv_in_fwd{l}"
            (rec["u"], rec["big"], rec["a"]), _, got = conv_in_fwd(h, w["norm_mix"], l, mats["conv_w_in", l], w["conv_b_in"], l, tm,
                                                                  riders(name))
            landed(name, got)
            name = f"conv_mid_fwd{l}"
            (rec["c"], rec["s"]), _, got = conv_mid_fwd(rec["a"], w["conv_dw"], w["conv_ln_g"], w["conv_ln_b"], l, tm, tpb,
                                                         riders(name))
            landed(name, got)
            mixed, w_out, lw, bias = rec["s"], mats["conv_w_out", l], l, w["conv_b_out"]
        else:
            j = l - na
            if kvs is None:
                kvs = dict(zip(("kn", "kv", "k", "v"), kv_fwd(h, w["kv_norm"], mats["w_kv", 0], w["k_norm"], tm)))
                kvs["h"] = h
                kvs["k3"], kvs["v3"] = kvs["k"].reshape(bl, lp, KVD), kvs["v"].reshape(bl, lp, KVD)
                kvs["kt"], kvs["vt"] = transpose_seq(kvs["k3"], "transpose_k"), transpose_seq(kvs["v3"], "transpose_v")
            rec["u"], rec["q"] = q_fwd(h, w["norm_mix"], l, mats["w_q", j], j, tm)
            name = f"attn_fwd{j}"
            (rec["o"], rec["lse"]), _, got = attn_fwd(rec["q"], kvs["kt"], kvs["v3"], w["q_norm"], w["attn_sinks"], j, bl, lp,
                                                      riders(name) if name in GATHER_PLAN else [])
            if name in GATHER_PLAN:
                landed(name, got)
            mixed, w_out, lw, bias = rec["o"], mats["w_o", j], j, None
        name = f"mixer_ffn_fwd{l}"
        (rec["h1"], rec["u2"], rec["g"], rec["up"], rec["hid"], h), _, got = mixer_ffn_fwd(
            h, mixed, w_out, lw, bias, w["norm_ffn"], l, mats["ffn_w_gate", l], mats["ffn_w_up", l], mats["ffn_w_down", l], tm // 2,
            riders(name) if name in GATHER_PLAN else [])
        if name in GATHER_PLAN:
            landed(name, got)
        saved.append(rec)

    dh3, loss_blk = loss_fwd(h.reshape(bl, lp, D), tgt)
    dh = flat(dh3)

    big, small, arrived = {}, {}, {}
    dks, dvs = [], []

    def ride(kernel_name):
        return [(big[nm], EXCHANGE_KIND[nm.rstrip("0123456789")], ks) for nm, ks in EXCHANGE_PLAN.get(kernel_name, [])]

    def landed_x(kernel_name, arrivals):
        for (nm, _), got in zip(EXCHANGE_PLAN.get(kernel_name, []), arrivals):
            arrived.setdefault(nm, []).append(got)

    def dw(name, grad, x, dy, **kw):
        big[grad], got = mm_tn(x, dy, 2 * tm, name, xch=ride(name), **kw)
        landed_x(name, got)

    for l in reversed(range(4)):
        rec = saved[l]
        dw(f"dw_down{l}", f"ffn_w_down{l}", rec["hid"], dh)
        name = f"ffn_bwd_x{l}"
        outs, got, _ = ffn_bwd_x(
            dh, rec["g"], rec["up"], rec["h1"], w["norm_ffn"], l, mats["ffn_w_down", l], mats["ffn_w_gate", l], mats["ffn_w_up", l],
            mats["w_o", l - na] if l >= na else None, tm // 2, ride(name))
        landed_x(name, got)
        dg, du, dh1, small[f"norm_ffn{l}"] = outs[:4]
        dw(f"dw_gate{l}", f"ffn_w_gate{l}", rec["u2"], dg, transposed=True)
        dw(f"dw_up{l}", f"ffn_w_up{l}", rec["u2"], du, transposed=True)
        if l >= na:
            j = l - na
            dw(f"dw_o{j}", f"w_o{j}", rec["o"], dh1)
            name = f"attn_bwd{j}"
            (dq, dk, dv, small[f"q_norm{j}"], small[f"attn_sinks{j}"]), got, _ = attn_bwd(
                rec["q"], kvs["k3"], kvs["kt"], kvs["vt"], outs[4], rec["o"], rec["lse"], w["q_norm"], w["attn_sinks"], j, bl, lp,
                ride(name))
            landed_x(name, got)
            dks.append(dk)
            dvs.append(dv)
            dw(f"dw_q{j}", f"w_q{j}", rec["u"], dq)
            dh, small[f"norm_mix{l}"] = proj_bwd(dq, mats["w_q", j], rec["h"], w["norm_mix"], l, dh1, tm, f"q_bwd{j}")[0]
            if l == na:
                dkv, small["k_norm"] = kv_bwd_pre(sum_transposed(*dks), sum_transposed(*dvs), kvs["kv"], w["k_norm"], tm)
                dw("dw_kv", "w_kv", kvs["kn"], dkv)
                dh, small["kv_norm"] = proj_bwd(dkv, mats["w_kv", 0], kvs["h"], w["kv_norm"], None, dh, tm, "kv_bwd")[0]
        else:
            name = f"conv_out_bwd{l}"
            (dc, small[f"conv_ln_g{l}"], small[f"conv_ln_b{l}"], small[f"conv_b_out{l}"]), got, _ = conv_out_bwd(
                dh1, rec["c"], w["conv_ln_g"], w["conv_ln_b"], mats["conv_w_out", l], l, tm, ride(name))
            landed_x(name, got)
            dw(f"dw_conv_out{l}", f"conv_w_out{l}", rec["s"], dh1)
            name = f"conv_mid_bwd{l}"
            (da, small[f"conv_b_in{l}"], small[f"conv_dw{l}"]), got, _ = conv_mid_bwd(
                dc, rec["a"], rec["big"], w["conv_dw"], l, tm, tpb, ride(name))
            landed_x(name, got)
            dw(f"dw_conv_in{l}", f"conv_w_in{l}", rec["u"], da, split=True)
            name = f"conv_in_bwd{l}"
            (dh, small[f"norm_mix{l}"]), got, _ = proj_bwd(da, mats["conv_w_in", l], rec["h"], w["norm_mix"], l, dh1, tm, name,
                                                           ride(name))
            landed_x(name, got)
    grad_x, small["meta_tokens"] = input_grads(dh.reshape(bl, lp, D), seq)
    return loss_blk, grad_x, big, arrived, small


def all_gather_blocks(blocks):
    n = len(blocks)

    def body(*refs):
        srcs, outs, sems = refs[:n], refs[n:2 * n], refs[2 * n:]
        _gat_start(srcs, outs, sems)
        _gat_forward(srcs, outs, sems)
        _gat_wait(srcs, outs, sems)

    any_spec = pl.BlockSpec(memory_space=pl.ANY)
    return pl.pallas_call(
        body, name="all_gather_blocks", out_shape=[_sds((NDEV,) + tuple(a.shape), a.dtype) for a in blocks],
        in_specs=[any_spec] * n, out_specs=[any_spec] * n, scratch_shapes=_xch_scratch(n),
    )(*blocks)


def cast_bf16(ws):
    n = len(ws)
    counts = [1 if x.ndim == 2 else x.shape[0] for x in ws]

    def body(*refs):
        outs = iter(refs[n:])
        for a in range(n):
            for l in range(counts[a]):
                next(outs)[...] = (refs[a][...] if ws[a].ndim == 2 else refs[a][l]).astype(BF)

    flat = pl.pallas_call(
        body, name="cast_bf16", out_shape=[_sds(x.shape[-2:], BF) for x, k in zip(ws, counts) for _ in range(k)],
        compiler_params=pltpu.CompilerParams(vmem_limit_bytes=VMEM_LIMIT),
    )(*ws)
    it = iter(flat)
    return [[next(it) for _ in range(k)] for k in counts]


def join_columns(w8, name):
    _, kk, n8 = w8.shape

    def body(x_ref, o_ref):
        o_ref[...] = jnp.concatenate([x_ref[p] for p in range(NDEV)], axis=1)

    return pl.pallas_call(body, name=name, out_shape=_sds((kk, NDEV * n8), w8.dtype),
                          compiler_params=pltpu.CompilerParams(vmem_limit_bytes=VMEM_LIMIT))(w8)


def _adamw_math(w, m, v, g):
    m2 = B1 * m + (1.0 - B1) * g
    v2 = B2 * v + (1.0 - B2) * (g * g)
    mh = m2 / (1.0 - B1 ** STEP)
    vh = v2 / (1.0 - B2 ** STEP)
    return -LR * (mh / (jnp.sqrt(vh) + AEPS) + WD * w), m2, v2


def adamw_big(w, m, v, parts, name, xch=(), gat=()):
    lyr, r, c = w.shape
    by_cols = c >= 512
    blk = (lyr, r, 256) if by_cols else (lyr, 256 if r % 256 == 0 else r, c)
    imap = (lambda i: (0, 0, i)) if by_cols else (lambda i: (0, i, 0))
    counts = [len(per_layer) for per_layer in parts]

    def body(w_ref, m_ref, v_ref, *rest):
        p_refs, (g_ref, d_ref, m2_ref, v2_ref) = iter(rest[:sum(counts)]), rest[sum(counts):]
        for l in range(lyr):
            g = None
            for _ in range(counts[l]):
                ref = next(p_refs)
                for q in range(ref.shape[0]):
                    g = ref[q].astype(F32) if g is None else g + ref[q].astype(F32)
            g_ref[l] = g
            d_ref[l], m2_ref[l], v2_ref[l] = _adamw_math(w_ref[l], m_ref[l], v_ref[l], g)

    spec = pl.BlockSpec(blk, imap)
    flat = [a for per_layer in parts for a in per_layer]
    pspecs = [pl.BlockSpec((a.shape[0],) + blk[1:], imap) for a in flat]
    return _call(body, name, ((c // 256) if by_cols else (r // blk[1]),), [spec, spec, spec] + pspecs,
                 [spec] * 4, [_sds((lyr, r, c), F32)] * 4, (w, m, v, *flat), xch=xch, gat=gat)


SMALL_ROWS = 104
REPLICATED = {"norm_mix": (0, 4, D), "norm_ffn": (4, 4, D), "kv_norm": (8, 1, D), "k_norm": (9, 1, HD), "q_norm": (10, 2, HD),
              "attn_sinks": (12, 2, NH)}
LOSS_ROW = 14
SHARDED = {"meta_tokens": (16, NMETA), "conv_b_in": (32, 4), "conv_dw": (36, 2 * CW), "conv_ln_g": (98, 2), "conv_ln_b": (100, 2),
           "conv_b_out": (102, 2)}


def pack_small(gs, loss_blk):
    order = ([f"norm_mix{l}" for l in range(4)] + [f"norm_ffn{l}" for l in range(4)] + ["kv_norm", "k_norm", "q_norm0", "q_norm1",
             "attn_sinks0", "attn_sinks1", "meta_tokens", "conv_b_in0", "conv_b_in1", "conv_dw0", "conv_dw1", "conv_ln_g0",
             "conv_ln_g1", "conv_ln_b0", "conv_ln_b1", "conv_b_out0", "conv_b_out1"])

    def body(*refs):
        r = dict(zip(order, refs))
        loss_ref, o_ref = refs[len(order)], refs[len(order) + 1]
        o_ref[...] = jnp.zeros_like(o_ref)
        for l in range(4):
            o_ref[l:l + 1, :] = r[f"norm_mix{l}"][...]
            o_ref[4 + l:5 + l, :] = r[f"norm_ffn{l}"][...]
        o_ref[8:9, :] = r["kv_norm"][...]
        o_ref[9:10, 0:HD] = r["k_norm"][...]
        for j in range(2):
            o_ref[10 + j:11 + j, 0:HD] = r[f"q_norm{j}"][...]
            o_ref[12 + j:13 + j, 0:NH] = r[f"attn_sinks{j}"][...]
            o_ref[32 + 2 * j:33 + 2 * j, :] = r[f"conv_b_in{j}"][:, 0:D]
            o_ref[33 + 2 * j:34 + 2 * j, :] = r[f"conv_b_in{j}"][:, D:2 * D]
            o_ref[36 + CW * j:36 + CW * (j + 1), :] = r[f"conv_dw{j}"][0:CW, :]
            o_ref[98 + j:99 + j, :] = r[f"conv_ln_g{j}"][...]
            o_ref[100 + j:101 + j, :] = r[f"conv_ln_b{j}"][...]
            o_ref[102 + j:103 + j, :] = r[f"conv_b_out{j}"][...]
        o_ref[LOSS_ROW:LOSS_ROW + 1, 0:1] = loss_ref[0:1, 0:1]
        o_ref[16:16 + NMETA, :] = r["meta_tokens"][...]

    return pl.pallas_call(body, name="pack_small", out_shape=_sds((SMALL_ROWS, D), F32))(*[gs[k] for k in order], loss_blk)


def adamw_small(g8, wts, mom, var):
    names = list(REPLICATED) + list(SHARDED)
    shape2 = {"kv_norm": (1, D), "k_norm": (1, HD)}
    ins = [a[k].reshape(shape2.get(k, a[k].shape)) for a in (wts, mom, var) for k in names]
    n = len(names)

    def body(*refs):
        g8_ref, w_refs, m_refs, v_refs = refs[0], refs[1:1 + n], refs[1 + n:1 + 2 * n], refs[1 + 2 * n:1 + 3 * n]
        loss_ref, outs, red_ref = refs[1 + 3 * n], refs[2 + 3 * n:-1], refs[-1]
        me = _my_index()
        acc = g8_ref[0]
        for q in range(1, NDEV):
            acc = acc + g8_ref[q]
        red_ref[...] = acc
        loss_ref[...] = red_ref[LOSS_ROW:LOSS_ROW + 1, 0:1]

        def mine(rows, width):
            acc = jnp.zeros((rows.stop - rows.start, width), F32)
            for p_ in range(NDEV):
                acc = acc + jnp.where(me == p_, red_ref[rows, p_ * width:(p_ + 1) * width], 0.0)
            return acc

        for i, k in enumerate(names):
            if k in REPLICATED:
                r0, nr, width = REPLICATED[k]
                g = red_ref[r0:r0 + nr, 0:width]
            elif k == "conv_b_in":
                half = D // (2 * D // NDEV)
                acc = jnp.zeros((2, 2 * D // NDEV), F32)
                for p_ in range(NDEV):
                    c0 = (p_ % half) * (2 * D // NDEV)
                    part = jnp.concatenate([red_ref[32 + 2 * j + p_ // half:33 + 2 * j + p_ // half, c0:c0 + 2 * D // NDEV]
                                            for j in range(2)], axis=0)
                    acc = acc + jnp.where(me == p_, part, 0.0)
                g = acc
            else:
                r0, nr = SHARDED[k]
                g = mine(slice(r0, r0 + nr), D // NDEV)
            w_, m_, v_ = w_refs[i], m_refs[i], v_refs[i]
            g_out, d_out, m_out, v_out = outs[4 * i:4 * i + 4]
            if k == "conv_dw":
                for j in range(2):
                    gj = g[CW * j:CW * (j + 1)]
                    g_out[j] = gj
                    d_out[j], m_out[j], v_out[j] = _adamw_math(w_[j], m_[j], v_[j], gj)
            else:
                g_out[...] = g
                d_out[...], m_out[...], v_out[...] = _adamw_math(w_[...], m_[...], v_[...], g)

    out_shape = [_sds((1, 1), F32)] + [_sds(ins[i].shape, F32) for i in range(n) for _ in range(4)]
    res = pl.pallas_call(body, name="adamw_small", out_shape=out_shape, scratch_shapes=[pltpu.VMEM((SMALL_ROWS, D), F32)])(g8, *ins)
    out = {k: tuple(o.reshape(wts[k].shape) for o in res[1 + 4 * i:5 + 4 * i]) for i, k in enumerate(names)}
    return res[0], out


NAMES = ["meta_tokens", "norm_mix", "norm_ffn", "conv_w_in", "conv_b_in", "conv_dw", "conv_ln_g", "conv_ln_b", "conv_w_out",
         "conv_b_out", "kv_norm", "w_kv", "k_norm", "w_q", "q_norm", "attn_sinks", "w_o", "ffn_w_gate", "ffn_w_up", "ffn_w_down"]


def kernel(x, meta_tokens, norm_mix, norm_ffn, conv_w_in, conv_b_in, conv_dw, conv_ln_g, conv_ln_b, conv_w_out, conv_b_out, kv_norm, w_kv, k_norm, w_q, q_norm, attn_sinks, w_o, ffn_w_gate, ffn_w_up, ffn_w_down, loss_target, m_meta_tokens, m_norm_mix, m_norm_ffn, m_conv_w_in, m_conv_b_in, m_conv_dw, m_conv_ln_g, m_conv_ln_b, m_conv_w_out, m_conv_b_out, m_kv_norm, m_w_kv, m_k_norm, m_w_q, m_q_norm, m_attn_sinks, m_w_o, m_ffn_w_gate, m_ffn_w_up, m_ffn_w_down, v_meta_tokens, v_norm_mix, v_norm_ffn, v_conv_w_in, v_conv_b_in, v_conv_dw, v_conv_ln_g, v_conv_ln_b, v_conv_w_out, v_conv_b_out, v_kv_norm, v_w_kv, v_k_norm, v_w_q, v_q_norm, v_attn_sinks, v_w_o, v_ffn_w_gate, v_ffn_w_up, v_ffn_w_down):
    wts = dict(zip(NAMES, (meta_tokens, norm_mix, norm_ffn, conv_w_in, conv_b_in, conv_dw, conv_ln_g, conv_ln_b, conv_w_out,
                           conv_b_out, kv_norm, w_kv, k_norm, w_q, q_norm, attn_sinks, w_o, ffn_w_gate, ffn_w_up, ffn_w_down)))
    mom = dict(zip(NAMES, (m_meta_tokens, m_norm_mix, m_norm_ffn, m_conv_w_in, m_conv_b_in, m_conv_dw, m_conv_ln_g, m_conv_ln_b,
                           m_conv_w_out, m_conv_b_out, m_kv_norm, m_w_kv, m_k_norm, m_w_q, m_q_norm, m_attn_sinks, m_w_o,
                           m_ffn_w_gate, m_ffn_w_up, m_ffn_w_down)))
    var = dict(zip(NAMES, (v_meta_tokens, v_norm_mix, v_norm_ffn, v_conv_w_in, v_conv_b_in, v_conv_dw, v_conv_ln_g, v_conv_ln_b,
                           v_conv_w_out, v_conv_b_out, v_kv_norm, v_w_kv, v_k_norm, v_w_q, v_q_norm, v_attn_sinks, v_w_o,
                           v_ffn_w_gate, v_ffn_w_up, v_ffn_w_down)))
    for k in TRANSPOSED:
        wts[k], mom[k], var[k] = (jnp.swapaxes(a, 1, 2) for a in (wts[k], mom[k], var[k]))

    big_names = list(BIG)
    layers = cast_bf16([wts[k] for k in big_names])
    shards = {(k, l): blk for k, per_layer in zip(big_names, layers) for l, blk in enumerate(per_layer)}
    vec_names = ["meta_tokens", "conv_b_in", "conv_dw", "conv_ln_g", "conv_ln_b", "conv_b_out"]
    full = dict(zip(vec_names, all_gather_blocks([wts[k] for k in vec_names])))
    join_vec = lambda a: jnp.moveaxis(a, 0, -2).reshape(a.shape[1:-1] + (NDEV * a.shape[-1],))
    w = {}
    w["conv_b_in"] = join_vec(full["conv_b_in"]).reshape(2, 1, 2 * D)
    w["conv_dw"] = join_vec(full["conv_dw"])
    for k in ("conv_ln_g", "conv_ln_b", "conv_b_out"):
        w[k] = join_vec(full[k]).reshape(2, 1, D)
    w["norm_mix"] = norm_mix.reshape(4, 1, D)
    w["norm_ffn"] = norm_ffn.reshape(4, 1, D)
    w["kv_norm"] = kv_norm.reshape(1, D)
    w["k_norm"] = k_norm.reshape(1, HD)
    w["q_norm"] = q_norm.reshape(2, 1, HD)
    w["attn_sinks"] = attn_sinks.reshape(2, 1, NH)

    loss_blk, grad_x, gbig, arrived, gs = local_step(x, loss_target, full["meta_tokens"], w, shards)

    packed = pack_small(gs, loss_blk)

    grads, delta, new_m, new_v = {}, {}, {}, {}
    tail = EXCHANGE_PLAN["tail"]
    waiting = {nm.rstrip("0123456789") for nm, _ in tail}
    order = sorted([k for k in big_names if k not in waiting], key=lambda k: -wts[k].size) + [k for k in big_names if k in waiting]
    small8 = None
    for pos, k in enumerate(order):
        flat2 = wts[k].ndim == 2
        as3 = (lambda a: a[None]) if flat2 else (lambda a: a)
        riders = tail if pos == 0 else []
        gat = [packed] if pos == 1 else []
        parts = [arrived[k]] if flat2 else [arrived[f"{k}{i}"] for i in range(wts[k].shape[0])]
        outs, got_x, got_g = adamw_big(as3(wts[k]), as3(mom[k]), as3(var[k]), parts, "adamw_" + k,
                                       xch=[(gbig[nm], EXCHANGE_KIND[nm.rstrip("0123456789")], ks) for nm, ks in riders], gat=gat)
        for (nm, _), got in zip(riders, got_x):
            arrived[nm].append(got)
        if gat:
            small8 = got_g[0]
        grads[k], delta[k], new_m[k], new_v[k] = [o[0] if flat2 else (jnp.swapaxes(o, 1, 2) if k in TRANSPOSED else o) for o in outs]
    loss, small = adamw_small(small8, wts, mom, var)
    for k, (g_, d_, m_, v_) in small.items():
        grads[k], delta[k], new_m[k], new_v[k] = g_, d_, m_, v_
    return (loss.reshape(()), grad_x, *[grads[k] for k in NAMES], *[delta[k] for k in NAMES], *[new_m[k] for k in NAMES],
            *[new_v[k] for k in NAMES])
```

```python
import jax
import jax.numpy as jnp
from jax import lax
from jax.experimental import pallas as pl
from jax.experimental.pallas import tpu as pltpu

F32 = jnp.float32
BF = jnp.bfloat16

D = 1024
DFF = 2816
NH = 16
NKV = 4
HD = 64
KVD = NKV * HD
NMETA = 16
CW = 31
HALO = 32
CHUNK = 32
QB = 128
EPS = 1e-6
NEG = -1e30
NDEV = 8
SCALE = HD ** -0.5

LR, B1, B2, AEPS, WD, STEP = 0.001, 0.9, 0.999, 1e-08, 0.01, 10

VMEM_LIMIT = 56 * 2 ** 20
MESH = pl.DeviceIdType.MESH


def _cp(n):
    return pltpu.CompilerParams(dimension_semantics=("arbitrary",) * n, vmem_limit_bytes=VMEM_LIMIT)


def _row(tm, c):
    return pl.BlockSpec((tm, c), lambda i: (i, 0))


def _res(shape):
    return pl.BlockSpec(shape, lambda i: (0,) * len(shape), pipeline_mode=pl.Buffered(1))


def _lay(l, shape):
    return pl.BlockSpec((None,) + tuple(shape), lambda i: (l,) + (0,) * len(shape), pipeline_mode=pl.Buffered(1))


def _acc(shape):
    return pl.BlockSpec(shape, lambda i: (0,) * len(shape))


def _sds(shape, dt):
    return jax.ShapeDtypeStruct(tuple(shape), dt)


def _dot(a, b):
    return jnp.dot(a.astype(BF), b.astype(BF), preferred_element_type=F32)


def _dot_nt(a, b):
    return lax.dot_general(a.astype(BF), b.astype(BF), (((1,), (1,)), ((), ())), preferred_element_type=F32)


def _dot_tn(a, b):
    return lax.dot_general(a.astype(BF), b.astype(BF), (((0,), (0,)), ((), ())), preferred_element_type=F32)


def _rstd(x):
    return lax.rsqrt(jnp.mean(x * x, axis=-1, keepdims=True) + EPS)


def _rms_bwd(x, g, dy):
    r = _rstd(x)
    z = dy * g
    dx = r * z - x * (r * r * r * jnp.mean(z * x, axis=-1, keepdims=True))
    return dx, jnp.sum(dy * x * r, axis=0, keepdims=True)


def _sig(x):
    return jax.nn.sigmoid(x)


def _fold8(x):
    out = x[0:8]
    for k in range(1, x.shape[0] // 8):
        out = out + x[8 * k:8 * k + 8]
    return out


def _shifted(win):
    return [win] + [pltpu.roll(win, 2 * CHUNK - rho, 0) for rho in range(1, 8)]


def _tap(phases, o):
    return phases[o % 8][8 * (o // 8):8 * (o // 8) + CHUNK]


def _spread_taps(dw_ref, taps):
    @pl.when(pl.program_id(0) == 0)
    def _():
        for j in range(CW):
            taps[j] = jnp.broadcast_to(dw_ref[j:j + 1, :], taps.shape[1:])


def _tap_weight(taps, j):
    return jnp.concatenate([taps[j]] * (CHUNK // 8), axis=0)


def _init(ref, first):
    @pl.when(first)
    def _():
        ref[...] = jnp.zeros_like(ref)


def _my_index():
    return 4 * lax.axis_index("x") + 2 * lax.axis_index("y") + lax.axis_index("c")


def _coords(idx):
    return (idx // 4, (idx // 2) % 2, idx % 2)


ALL = tuple(range(NDEV))
H1, H2 = (0, 1, 2, 4, 6), (3, 5, 7)


def _xch_shapes(xch):
    return [_sds((len(ks),) + ((a.shape[0] // NDEV, a.shape[1]) if k == "rows" else tuple(a.shape[1:])), a.dtype) for a, k, ks in xch]


def _xch_scratch(n):
    return [pltpu.SemaphoreType.DMA((n, NDEV)), pltpu.SemaphoreType.DMA((n, NDEV)), pltpu.SemaphoreType.DMA((n,))]


def _xch_copies(meta, srcs, outs, sems, arrivals):
    send_sems, recv_sems, local_sems = sems
    me = _my_index()

    def piece(a, p):
        if meta[a][0] == "rows":
            r = srcs[a].shape[0] // NDEV
            return srcs[a].at[pl.ds(p * r, r), :]
        return srcs[a].at[p]

    def remote(a, i, k, src):
        return pltpu.make_async_remote_copy(
            src_ref=src, dst_ref=outs[a].at[i], send_sem=send_sems.at[a, k], recv_sem=recv_sems.at[a, k],
            device_id=_coords(me ^ k), device_id_type=MESH)

    local, sends, recvs = [], [], []
    for a, (_, ks) in enumerate(meta):
        for i, k in enumerate(ks):
            if k == 0:
                local.append(pltpu.make_async_copy(piece(a, me), outs[a].at[i], local_sems.at[a]))
            else:
                sends.append(remote(a, i, k, piece(a, me ^ k)))
                if arrivals:
                    recvs.append(remote(a, i, k, piece(a, me)))
    return local, sends, recvs


def _xch_start(meta, srcs, outs, sems):
    local, sends, _ = _xch_copies(meta, srcs, outs, sems, False)
    for cp in local + sends:
        cp.start()


def _xch_wait(meta, srcs, outs, sems):
    local, sends, recvs = _xch_copies(meta, srcs, outs, sems, True)
    for cp in recvs:
        cp.wait_recv()
    for cp in sends:
        cp.wait_send()
    for cp in local:
        cp.wait()


def _gat_copies(srcs, outs, sems):
    send_sems, recv_sems, local_sems = sems
    x, y, c = lax.axis_index("x"), lax.axis_index("y"), lax.axis_index("c")
    me, sibling = (x, y, c), (x, y, 1 - c)
    chips = [(1 - x, y), (x, 1 - y), (1 - x, 1 - y)]

    def copy(a, k, owner, to, from_block=False):
        slot = outs[a].at[4 * owner[0] + 2 * owner[1] + owner[2]]
        return pltpu.make_async_remote_copy(
            src_ref=srcs[a] if from_block else slot, dst_ref=slot, send_sem=send_sems.at[a, k], recv_sem=recv_sems.at[a, k],
            device_id=to, device_id_type=MESH)

    n = len(srcs)
    local = lambda: [pltpu.make_async_copy(srcs[a], outs[a].at[4 * x + 2 * y + c], local_sems.at[a]) for a in range(n)]
    first = lambda: [cp for a in range(n) for cp in
                     [copy(a, 0, me, sibling, True)] + [copy(a, 1 + j, me, (*chip, c), True) for j, chip in enumerate(chips)]]
    landed = lambda: [copy(a, 1 + j, (*chip, c), me) for a in range(n) for j, chip in enumerate(chips)]
    passed = lambda: [copy(a, 4 + j, (*chip, c), sibling) for a in range(n) for j, chip in enumerate(chips)]
    final = lambda: [cp for a in range(n) for cp in
                     [copy(a, 0, sibling, me)] + [copy(a, 4 + j, (*chip, 1 - c), me) for j, chip in enumerate(chips)]]
    return local, first, landed, passed, final


def _gat_start(srcs, outs, sems):
    local, first, _, _, _ = _gat_copies(srcs, outs, sems)
    for cp in local() + first():
        cp.start()


def _gat_forward(srcs, outs, sems):
    _, _, landed, passed, _ = _gat_copies(srcs, outs, sems)
    for got, on in zip(landed(), passed()):
        got.wait_recv()
        on.start()


def _gat_wait(srcs, outs, sems):
    local, first, _, passed, final = _gat_copies(srcs, outs, sems)
    for cp in final():
        cp.wait_recv()
    for cp in first() + passed():
        cp.wait_send()
    for cp in local():
        cp.wait()


def _call(body, name, grid, in_specs, out_specs, out_shape, args, scratch=(), xch=(), gat=()):
    n_in, n_out, n_x, n_g, n_s = len(in_specs), len(out_specs), len(xch), len(gat), len(scratch)
    kinds = [(k, ks) for _, k, ks in xch]
    total = 1
    for g in grid:
        total *= g

    def wrapped(*refs):
        ins, refs = refs[:n_in], refs[n_in:]
        x_src, refs = refs[:n_x], refs[n_x:]
        g_src, refs = refs[:n_g], refs[n_g:]
        outs, refs = refs[:n_out], refs[n_out:]
        x_out, refs = refs[:n_x], refs[n_x:]
        g_out, refs = refs[:n_g], refs[n_g:]
        own, refs = refs[:n_s], refs[n_s:]
        x_sems, g_sems = (refs[:3], refs[3:]) if n_x else ((), refs)
        step = pl.program_id(0)
        for d in range(1, len(grid)):
            step = step * grid[d] + pl.program_id(d)
        if n_x or n_g:
            @pl.when(step == 0)
            def _():
                if n_x:
                    _xch_start(kinds, x_src, x_out, x_sems)
                if n_g:
                    _gat_start(g_src, g_out, g_sems)

        body(*ins, *outs, *own)
        if n_g:
            @pl.when(step == max(total - 2, 0))
            def _():
                _gat_forward(g_src, g_out, g_sems)

        if n_x or n_g:
            @pl.when(step == total - 1)
            def _():
                if n_x:
                    _xch_wait(kinds, x_src, x_out, x_sems)
                if n_g:
                    _gat_wait(g_src, g_out, g_sems)

    any_spec = pl.BlockSpec(memory_space=pl.ANY)
    g_shapes = [_sds((NDEV,) + tuple(a.shape), a.dtype) for a in gat]
    res = pl.pallas_call(
        wrapped, name=name, grid=grid, in_specs=list(in_specs) + [any_spec] * (n_x + n_g),
        out_specs=list(out_specs) + [any_spec] * (n_x + n_g), out_shape=list(out_shape) + _xch_shapes(xch) + g_shapes,
        scratch_shapes=list(scratch) + (_xch_scratch(n_x) if n_x else []) + (_xch_scratch(n_g) if n_g else []),
        compiler_params=_cp(len(grid)),
    )(*args, *[a for a, _, _ in xch], *gat)
    return res[:n_out], res[n_out:n_out + n_x], res[n_out + n_x:]


def embed(x, meta8, lp, gat):
    bl, seq, _ = x.shape
    c8 = D // NDEV
    cb = 2 * c8

    def body(x_ref, m_ref, h_ref):
        h_ref[0:NMETA, :] = jnp.concatenate([m_ref[0], m_ref[1]], axis=1)
        h_ref[NMETA:NMETA + seq, :] = x_ref[...]
        h_ref[NMETA + seq:, :] = jnp.zeros((lp - NMETA - seq, cb), F32)

    (h0,), _, got = _call(
        body, "embed", (bl, D // cb),
        [pl.BlockSpec((None, seq, cb), lambda b, c: (b, 0, c)), pl.BlockSpec((2, NMETA, c8), lambda b, c: (c, 0, 0))],
        [pl.BlockSpec((None, lp, cb), lambda b, c: (b, 0, c))], [_sds((bl, lp, D), F32)], (x, meta8), gat=gat)
    return h0, got


def conv_in_fwd(h, nm, l, w_in, b_in, i, tm, gat):
    t = h.shape[0]

    def body(h_ref, g_ref, w_ref, b_ref, u_ref, big_ref, a_ref):
        x = h_ref[...]
        ub = (x * _rstd(x) * g_ref[...]).astype(BF)
        u_ref[...] = ub
        big = jnp.dot(ub, w_ref[...], preferred_element_type=F32) + b_ref[...]
        big_ref[...] = big.astype(BF)
        a_ref[...] = big[:, :D] * _sig(big[:, D:])

    return _call(
        body, f"conv_in_fwd{i}", (t // tm,), [_row(tm, D), _lay(l, (1, D)), _res((D, 2 * D)), _lay(i, (1, 2 * D))],
        [_row(tm, D), _row(tm, 2 * D), _row(tm, D)], [_sds((t, D), BF), _sds((t, 2 * D), BF), _sds((t, D), F32)],
        (h, nm, w_in, b_in), gat=gat)


def _prev_halo(tm):
    return pl.BlockSpec((HALO, D), lambda i: (jnp.maximum(i * (tm // HALO) - 1, 0), 0))


def _next_halo(tm, t):
    return pl.BlockSpec((HALO, D), lambda i: (jnp.minimum((i + 1) * (tm // HALO), t // HALO - 1), 0))


def conv_mid_fwd(a, dw, ln_g, ln_b, i, tm, tpb, gat):
    t = a.shape[0]

    def body(a_ref, halo_ref, dw_ref, g_ref, b_ref, c_ref, s_ref, ext, taps):
        _spread_taps(dw_ref, taps)
        first = pl.program_id(0) % tpb == 0
        ext[0:HALO] = jnp.where(first, 0.0, halo_ref[...])
        ext[HALO:] = a_ref[...]

        def chunk(k, carry):
            r0 = pl.multiple_of(k * CHUNK, CHUNK)
            win = _shifted(ext[pl.ds(r0, 2 * CHUNK), :])
            c = jnp.zeros((CHUNK, D), F32)
            for j in range(CW):
                c = c + _tap_weight(taps, j) * _tap(win, j + 2)
            c_ref[pl.ds(r0, CHUNK), :] = c
            mu = jnp.mean(c, axis=-1, keepdims=True)
            xc = c - mu
            n = xc * lax.rsqrt(jnp.mean(xc * xc, axis=-1, keepdims=True) + EPS) * g_ref[...] + b_ref[...]
            s_ref[pl.ds(r0, CHUNK), :] = (n * _sig(n)).astype(BF)
            return carry

        lax.fori_loop(0, tm // CHUNK, chunk, 0, unroll=2)

    return _call(
        body, f"conv_mid_fwd{i}", (t // tm,),
        [_row(tm, D), _prev_halo(tm), _lay(i, (CW, D)), _lay(i, (1, D)), _lay(i, (1, D))],
        [_row(tm, D), _row(tm, D)], [_sds((t, D), F32), _sds((t, D), BF)], (a, a, dw, ln_g, ln_b),
        scratch=[pltpu.VMEM((tm + HALO, D), F32), pltpu.VMEM((CW, 8, D), F32)], gat=gat)


def mixer_ffn_fwd(h, s, w_out, lw, bias, nf, l, wg, wu, wd, tm, gat):
    t = h.shape[0]

    def body(*refs):
        if bias is None:
            h_ref, s_ref, w_ref, nf_ref, wg_ref, wu_ref, wd_ref, h1_ref, u_ref, g_ref, up_ref, hid_ref, h2_ref = refs
            y = 0.0
        else:
            h_ref, s_ref, w_ref, b_ref, nf_ref, wg_ref, wu_ref, wd_ref, h1_ref, u_ref, g_ref, up_ref, hid_ref, h2_ref = refs
            y = b_ref[...]
        h1 = h_ref[...] + (jnp.dot(s_ref[...], w_ref[...], preferred_element_type=F32) + y)
        h1_ref[...] = h1
        ub = (h1 * _rstd(h1) * nf_ref[...]).astype(BF)
        u_ref[...] = ub
        g = _dot_nt(ub, wg_ref[...])
        up = _dot_nt(ub, wu_ref[...])
        g_ref[...] = g.astype(BF)
        up_ref[...] = up.astype(BF)
        hid = (g * _sig(g) * up).astype(BF)
        hid_ref[...] = hid
        h2_ref[...] = h1 + jnp.dot(hid, wd_ref[...], preferred_element_type=F32)

    ins = [h, s, w_out] + ([] if bias is None else [bias]) + [nf, wg, wu, wd]
    specs = ([_row(tm, D), _row(tm, D), _res((D, D))] + ([] if bias is None else [_lay(lw, (1, D))])
             + [_lay(l, (1, D)), _res((DFF, D)), _res((DFF, D)), _res((DFF, D))])
    return _call(
        body, f"mixer_ffn_fwd{l}", (t // tm,), specs,
        [_row(tm, D), _row(tm, D), _row(tm, DFF), _row(tm, DFF), _row(tm, DFF), _row(tm, D)],
        [_sds((t, D), F32), _sds((t, D), BF), _sds((t, DFF), BF), _sds((t, DFF), BF), _sds((t, DFF), BF), _sds((t, D), F32)],
        ins, gat=gat)


def _seg_rms(x, g, nseg):
    outs = []
    for s in range(nseg):
        xs = x[:, HD * s:HD * s + HD]
        outs.append(xs * _rstd(xs) * g)
    return jnp.concatenate(outs, axis=1)


def kv_fwd(h, kvn, w_kv, kng, tm):
    t = h.shape[0]

    def body(h_ref, g_ref, w_ref, kg_ref, kn_ref, kv_ref, k_ref, v_ref):
        x = h_ref[...]
        kn = (x * _rstd(x) * g_ref[...]).astype(BF)
        kn_ref[...] = kn
        kv = jnp.dot(kn, w_ref[...], preferred_element_type=F32)
        kv_ref[...] = kv
        k_ref[...] = _seg_rms(kv[:, :KVD], kg_ref[...], NKV).astype(BF)
        v_ref[...] = kv[:, KVD:].astype(BF)

    return pl.pallas_call(
        body, name="kv_fwd", grid=(t // tm,),
        in_specs=[_row(tm, D), _res((1, D)), _res((D, 2 * KVD)), _res((1, HD))],
        out_specs=[_row(tm, D), _row(tm, 2 * KVD), _row(tm, KVD), _row(tm, KVD)],
        out_shape=[_sds((t, D), BF), _sds((t, 2 * KVD), F32), _sds((t, KVD), BF), _sds((t, KVD), BF)],
        compiler_params=_cp(1),
    )(h, kvn, w_kv, kng)


def q_fwd(h, nm, l, w_q, j, tm):
    t = h.shape[0]

    def body(h_ref, g_ref, w_ref, u_ref, q_ref):
        x = h_ref[...]
        ub = (x * _rstd(x) * g_ref[...]).astype(BF)
        u_ref[...] = ub
        q_ref[...] = jnp.dot(ub, w_ref[...], preferred_element_type=F32)

    return pl.pallas_call(
        body, name=f"q_fwd{j}", grid=(t // tm,),
        in_specs=[_row(tm, D), _lay(l, (1, D)), _res((D, D))],
        out_specs=[_row(tm, D), _row(tm, D)], out_shape=[_sds((t, D), BF), _sds((t, D), F32)],
        compiler_params=_cp(1),
    )(h, nm, w_q)


RQ = NH // NKV


NKEYS = 2 * QB + NMETA


def _attn_mask(n, start):
    shape = (RQ * QB, NKEYS)
    qpos = n * QB + (lax.broadcasted_iota(jnp.int32, shape, 0) & (QB - 1))
    col = lax.broadcasted_iota(jnp.int32, shape, 1)
    in_band = col < 2 * QB
    kpos = jnp.where(in_band, start + col, col - 2 * QB)
    return (kpos <= qpos) & ((col >= 2 * QB) | ((qpos - kpos < QB) & (kpos >= NMETA)))


def attn_bias():
    def body(o_ref):
        n = pl.program_id(0)
        o_ref[...] = jnp.where(_attn_mask(n, jnp.maximum(n - 1, 0) * QB), 0.0, NEG)

    return pl.pallas_call(
        body, name="attn_bias", grid=(3,), out_specs=pl.BlockSpec((None, RQ * QB, NKEYS), lambda n: (n, 0, 0)),
        out_shape=_sds((3, RQ * QB, NKEYS), F32), compiler_params=_cp(1))()


def _bias_spec():
    return pl.BlockSpec((None, RQ * QB, NKEYS), lambda n: (jnp.minimum(n, 2), 0, 0))


def _keys(ref, band, gs):
    return jnp.concatenate([ref[band, gs], ref[0:NMETA, gs]], axis=0)


def _keys_t(ref, band, gs):
    return jnp.concatenate([ref[gs, band], ref[gs, 0:NMETA]], axis=1)


def transpose_seq(a, name):
    bl, r, c = a.shape

    def body(a_ref, o_ref):
        o_ref[...] = a_ref[...].T

    return pl.pallas_call(
        body, name=name, grid=(bl,), in_specs=[pl.BlockSpec((None, r, c), lambda b: (b, 0, 0))],
        out_specs=pl.BlockSpec((None, c, r), lambda b: (b, 0, 0)), out_shape=_sds((bl, c, r), a.dtype), compiler_params=_cp(1),
    )(a)


def sum_transposed(a0, a1):
    bl, c, r = a0.shape

    def body(a0_ref, a1_ref, o_ref):
        o_ref[...] = (a0_ref[...] + a1_ref[...]).T

    spec = pl.BlockSpec((None, c, r), lambda b: (b, 0, 0))
    return pl.pallas_call(
        body, name="sum_transposed", grid=(bl,), in_specs=[spec, spec],
        out_specs=pl.BlockSpec((r, c), lambda b: (b, 0)), out_shape=_sds((bl * r, c), a0.dtype), compiler_params=_cp(1),
    )(a0, a1)


def _stack_heads(ref, g, fn):
    return jnp.concatenate([fn(ref[:, HD * (g * RQ + r):HD * (g * RQ + r) + HD]) for r in range(RQ)], axis=0)


def _stack_cols(ref, g):
    return jnp.concatenate([ref[:, g * RQ + r:g * RQ + r + 1] for r in range(RQ)], axis=0)


def _stack_sinks(sk_ref, g):
    return jnp.concatenate([jnp.broadcast_to(sk_ref[:, g * RQ + r:g * RQ + r + 1], (QB, 1)) for r in range(RQ)], axis=0)


def attn_fwd(q, kt, v, bias, qg, sinks, j, bl, lp, gat):
    t = q.shape[0]
    nb = lp // QB

    def body(q_ref, kt_ref, v_ref, bias_ref, qg_ref, sk_ref, o_ref, lse_ref):
        n = pl.program_id(0)
        start = pl.multiple_of(jnp.maximum(n - 1, 0) * QB, QB)
        band = pl.ds(start, 2 * QB)
        lane = lax.broadcasted_iota(jnp.int32, (QB, NH), 1)
        ones = jnp.ones((NKEYS, HD), BF)
        pairs = [(b, g) for b in range(bl) for g in range(NKV)]
        gsl = [slice(HD * g, HD * g + HD) for g in range(NKV)]
        qns = [_stack_heads(q_ref.at[b], g, lambda x: (x * _rstd(x) * (qg_ref[...] * SCALE)).astype(BF)) for b, g in pairs]
        ss = [_dot(qns[i], _keys_t(kt_ref.at[b], band, gsl[g])) + bias_ref[...] for i, (b, g) in enumerate(pairs)]
        sinks = [_stack_sinks(sk_ref, g) for g in range(NKV)]
        mxs = [jnp.maximum(jnp.max(ss[i], -1, keepdims=True), sinks[g]) for i, (b, g) in enumerate(pairs)]
        oas = [_dot(jnp.exp(ss[i] - mxs[i]), jnp.concatenate([_keys(v_ref.at[b], band, gsl[g]), ones], axis=1))
               for i, (b, g) in enumerate(pairs)]
        lses = [jnp.zeros((QB, NH), F32) for _ in range(bl)]
        for i, (b, g) in enumerate(pairs):
            den = oas[i][:, HD:HD + 1] + jnp.exp(sinks[g] - mxs[i])
            o = oas[i][:, :HD] * (1.0 / den)
            l = mxs[i] + jnp.log(den)
            for r in range(RQ):
                h = g * RQ + r
                o_ref[b, :, HD * h:HD * h + HD] = o[r * QB:(r + 1) * QB].astype(BF)
                lses[b] = jnp.where(lane == h, l[r * QB:(r + 1) * QB], lses[b])
        for b in range(bl):
            lse_ref[b] = lses[b]

    blk = lambda c: pl.BlockSpec((bl, QB, c), lambda n: (0, n, 0))
    (o, lse), _, got = _call(
        body, f"attn_fwd{j}", (nb,),
        [blk(D), pl.BlockSpec((bl, KVD, lp), lambda n: (0, 0, 0)), pl.BlockSpec((bl, lp, KVD), lambda n: (0, 0, 0)), _bias_spec(),
         pl.BlockSpec((None, 1, HD), lambda n: (j, 0, 0)), pl.BlockSpec((None, 1, NH), lambda n: (j, 0, 0))],
        [blk(D), blk(NH)], [_sds((bl, lp, D), BF), _sds((bl, lp, NH), F32)], (q.reshape(bl, lp, D), kt, v, bias, qg, sinks),
        gat=gat)
    return (o.reshape(t, D), lse.reshape(t, NH)), (), got


def loss_fwd(h, tgt):
    bl, lp, _ = h.shape
    seq = tgt.shape[1]
    cb = 256

    def body(h_ref, t_ref, dh_ref, loss_ref):
        _init(loss_ref, (pl.program_id(0) == 0) & (pl.program_id(1) == 0))
        err = h_ref[NMETA:NMETA + seq, :] - t_ref[...]
        dh_ref[...] = jnp.zeros_like(dh_ref)
        dh_ref[NMETA:NMETA + seq, :] = err * (1.0 / D)
        loss_ref[...] += (0.5 / D) * jnp.sum(err * err)

    return pl.pallas_call(
        body, name="loss_fwd", grid=(bl, D // cb),
        in_specs=[pl.BlockSpec((None, lp, cb), lambda b, c: (b, 0, c)), pl.BlockSpec((None, seq, cb), lambda b, c: (b, 0, c))],
        out_specs=[pl.BlockSpec((None, lp, cb), lambda b, c: (b, 0, c)), pl.BlockSpec((8, 128), lambda b, c: (0, 0))],
        out_shape=[_sds((bl, lp, D), F32), _sds((8, 128), F32)],
        compiler_params=_cp(2),
    )(h, tgt)


def ffn_bwd_x(dh2, g, up, h1, nf, l, wd, wg, wu, w_o, tm, xch):
    t = dh2.shape[0]

    def body(dh2_ref, g_ref, up_ref, h1_ref, nf_ref, wd_ref, wg_ref, wu_ref, *rest):
        if w_o is None:
            dg_ref, du_ref, dh1_ref, dnf_ref = rest
        else:
            wo_ref, dg_ref, du_ref, dh1_ref, dnf_ref, do_ref = rest
        _init(dnf_ref, pl.program_id(0) == 0)
        dh2v = dh2_ref[...]
        dhid = _dot_nt(dh2v, wd_ref[...])
        gv = g_ref[...].astype(F32)
        uv = up_ref[...].astype(F32)
        sg = _sig(gv)
        dgv = (dhid * uv * (sg * (1.0 + gv * (1.0 - sg)))).astype(BF)
        duv = (dhid * (gv * sg)).astype(BF)
        dg_ref[...] = dgv
        du_ref[...] = duv
        dnorm = _dot(dgv, wg_ref[...]) + _dot(duv, wu_ref[...])
        dx, dnf = _rms_bwd(h1_ref[...], nf_ref[...], dnorm)
        dh1 = dh2v + dx
        dh1_ref[...] = dh1
        dnf_ref[...] += dnf
        if w_o is not None:
            do_ref[...] = _dot_nt(dh1, wo_ref[...]).astype(BF)

    attn = w_o is not None
    return _call(
        body, f"ffn_bwd_x{l}", (t // tm,),
        [_row(tm, D), _row(tm, DFF), _row(tm, DFF), _row(tm, D), _lay(l, (1, D)),
         _res((DFF, D)), _res((DFF, D)), _res((DFF, D))] + ([_res((D, D))] if attn else []),
        [_row(tm, DFF), _row(tm, DFF), _row(tm, D), _acc((1, D))] + ([_row(tm, D)] if attn else []),
        [_sds((t, DFF), BF), _sds((t, DFF), BF), _sds((t, D), F32), _sds((1, D), F32)] + ([_sds((t, D), BF)] if attn else []),
        (dh2, g, up, h1, nf, wd, wg, wu) + ((w_o,) if attn else ()), xch=xch)


def mm_tn(x, dy, tm, name, split=False, transposed=False, xch=()):
    t, kk = x.shape
    nn = dy.shape[1]
    n8 = nn // NDEV
    nsteps = t // tm

    def body(x_ref, dy_ref, o_ref, acc):
        i = pl.program_id(0)
        _init(acc, i == 0)
        acc[...] += _dot_tn(x_ref[...], dy_ref[...])

        @pl.when(i == nsteps - 1)
        def _():
            if split:
                for p in range(NDEV):
                    o_ref[p] = acc[:, p * n8:(p + 1) * n8].astype(BF)
            elif transposed:
                o_ref[...] = acc[...].T.astype(BF)
            else:
                o_ref[...] = acc[...].astype(BF)

    oshape = (NDEV, kk, n8) if split else ((nn, kk) if transposed else (kk, nn))
    (out,), got, _ = _call(body, name, (nsteps,), [_row(tm, kk), _row(tm, nn)], [_acc(oshape)], [_sds(oshape, BF)], (x, dy),
                           scratch=[pltpu.VMEM((kk, nn), F32)], xch=xch)
    return out, got


def proj_bwd(dy, w, h, g, lg, dh_in, tm, name, xch=()):
    t = h.shape[0]
    nn = dy.shape[1]
    wspec = _res(w.shape)
    gspec = _res((1, D)) if lg is None else _lay(lg, (1, D))

    def body(dy_ref, w_ref, h_ref, g_ref, dhin_ref, dh_ref, dg_ref):
        _init(dg_ref, pl.program_id(0) == 0)
        du = _dot_nt(dy_ref[...], w_ref[...])
        dx, dg = _rms_bwd(h_ref[...], g_ref[...], du)
        dh_ref[...] = dhin_ref[...] + dx
        dg_ref[...] += dg

    return _call(body, name, (t // tm,), [_row(tm, nn), wspec, _row(tm, D), gspec, _row(tm, D)],
                 [_row(tm, D), _acc((1, D))], [_sds((t, D), F32), _sds((1, D), F32)], (dy, w, h, g, dh_in), xch=xch)


def attn_bwd(q, k, kt, vt, bias, do, o, lse, qg, sinks, j, bl, lp, xch):
    t = q.shape[0]
    nb = lp // QB

    def body(q_ref, k_ref, kt_ref, vt_ref, bias_ref, do_ref, o_ref, lse_ref, qg_ref, sk_ref, dq_ref, dk_ref, dv_ref, dqg_ref,
             dsk_ref):
        n = pl.program_id(0)
        for ref in (dk_ref, dv_ref, dqg_ref, dsk_ref):
            _init(ref, n == 0)
        start = pl.multiple_of(jnp.maximum(n - 1, 0) * QB, QB)
        band = pl.ds(start, 2 * QB)
        lane = lax.broadcasted_iota(jnp.int32, (1, NH), 1)
        dqg = jnp.zeros((1, HD), F32)
        dsk = jnp.zeros((1, NH), F32)
        pairs = [(b, g) for b in range(bl) for g in range(NKV)]
        idx = range(len(pairs))
        gsl = [slice(HD * g, HD * g + HD) for g in range(NKV)]
        qhs = [_stack_heads(q_ref.at[b], g, lambda x: x) for b, g in pairs]
        rss = [_rstd(qhs[i]) for i in idx]
        qns = [(qhs[i] * rss[i] * (qg_ref[...] * SCALE)).astype(BF) for i in idx]
        lss = [_stack_cols(lse_ref.at[b], g) for b, g in pairs]
        dohs = [_stack_heads(do_ref.at[b], g, lambda x: x) for b, g in pairs]
        deltas = [jnp.sum(dohs[i].astype(F32) * _stack_heads(o_ref.at[b], g, lambda x: x).astype(F32), axis=-1, keepdims=True)
                  for i, (b, g) in enumerate(pairs)]
        prs = [jnp.exp(_dot(qns[i], _keys_t(kt_ref.at[b], band, gsl[g])) + bias_ref[...] - lss[i]) for i, (b, g) in enumerate(pairs)]
        dss = [(prs[i] * (_dot(dohs[i], _keys_t(vt_ref.at[b], band, gsl[g])) - deltas[i])).astype(BF)
               for i, (b, g) in enumerate(pairs)]
        for i, (b, g) in enumerate(pairs):
            gs = gsl[g]
            dkt = _dot_tn(qns[i], dss[i])
            dvt = _dot_tn(dohs[i], prs[i])
            dk_ref[b, gs, band] += dkt[:, :2 * QB]
            dv_ref[b, gs, band] += dvt[:, :2 * QB]
            dk_ref[b, gs, 0:NMETA] += dkt[:, 2 * QB:]
            dv_ref[b, gs, 0:NMETA] += dvt[:, 2 * QB:]
        dqns = [_dot(dss[i], _keys(k_ref.at[b], band, gsl[g])) * SCALE for i, (b, g) in enumerate(pairs)]
        for i, (b, g) in enumerate(pairs):
            qh, rs, dqn = qhs[i], rss[i], dqns[i]
            dsink = jnp.exp(_stack_sinks(sk_ref, g) - lss[i]) * deltas[i]
            z = dqn * qg_ref[...]
            dq = rs * z - qh * (rs * rs * rs * jnp.mean(z * qh, axis=-1, keepdims=True))
            dqg = dqg + jnp.sum(dqn * qh * rs, axis=0, keepdims=True)
            for r in range(RQ):
                h = g * RQ + r
                dq_ref[b, :, HD * h:HD * h + HD] = dq[r * QB:(r + 1) * QB]
                dsk = dsk + jnp.where(lane == h, -jnp.sum(dsink[r * QB:(r + 1) * QB]), 0.0)
        dqg_ref[...] += dqg
        dsk_ref[...] += dsk

    blk = lambda c: pl.BlockSpec((bl, QB, c), lambda n: (0, n, 0))
    seq = pl.BlockSpec((bl, lp, KVD), lambda n: (0, 0, 0))
    seq_t = pl.BlockSpec((bl, KVD, lp), lambda n: (0, 0, 0))
    as3 = lambda a: a.reshape(bl, lp, a.shape[-1])
    (dq, dk, dv, dqg, dsk), got, _ = _call(
        body, f"attn_bwd{j}", (nb,),
        [blk(D), seq, seq_t, seq_t, _bias_spec(), blk(D), blk(D), blk(NH),
         pl.BlockSpec((None, 1, HD), lambda n: (j, 0, 0)), pl.BlockSpec((None, 1, NH), lambda n: (j, 0, 0))],
        [blk(D), seq_t, seq_t, pl.BlockSpec((1, HD), lambda n: (0, 0)), pl.BlockSpec((1, NH), lambda n: (0, 0))],
        [_sds((bl, lp, D), F32), _sds((bl, KVD, lp), F32), _sds((bl, KVD, lp), F32), _sds((1, HD), F32), _sds((1, NH), F32)],
        (as3(q), k, kt, vt, bias, as3(do), as3(o), as3(lse), qg, sinks), xch=xch)
    return (dq.reshape(t, D), dk, dv, dqg, dsk), got, ()


def kv_bwd_pre(dk, dv, kv, kng, tm):
    t = kv.shape[0]

    def body(dk_ref, dv_ref, kv_ref, g_ref, dkv_ref, dg_ref):
        _init(dg_ref, pl.program_id(0) == 0)
        dg = jnp.zeros((1, HD), F32)
        outs = []
        for s in range(NKV):
            sl = slice(HD * s, HD * s + HD)
            dx, dgs = _rms_bwd(kv_ref[:, sl], g_ref[...], dk_ref[:, sl])
            outs.append(dx)
            dg = dg + dgs
        dkv_ref[:, :KVD] = jnp.concatenate(outs, axis=1).astype(BF)
        dkv_ref[:, KVD:] = dv_ref[...].astype(BF)
        dg_ref[...] += dg

    return pl.pallas_call(
        body, name="kv_bwd_pre", grid=(t // tm,),
        in_specs=[_row(tm, KVD)] * 2 + [_row(tm, 2 * KVD), _res((1, HD))],
        out_specs=[_row(tm, 2 * KVD), _acc((1, HD))], out_shape=[_sds((t, 2 * KVD), BF), _sds((1, HD), F32)],
        compiler_params=_cp(1),
    )(dk, dv, kv, kng)


def conv_out_bwd(dh1, c, ln_g, ln_b, w_out, i, tm, xch):
    t = dh1.shape[0]

    def body(dh1_ref, c_ref, g_ref, b_ref, w_ref, dc_ref, dg_ref, db_ref, dbo_ref):
        first = pl.program_id(0) == 0
        _init(dg_ref, first)
        _init(db_ref, first)
        _init(dbo_ref, first)
        dh1v = dh1_ref[...]
        ds = _dot_nt(dh1v, w_ref[...])
        cv = c_ref[...]
        xc = cv - jnp.mean(cv, axis=-1, keepdims=True)
        rstd = lax.rsqrt(jnp.mean(xc * xc, axis=-1, keepdims=True) + EPS)
        xh = xc * rstd
        n = xh * g_ref[...] + b_ref[...]
        sg = _sig(n)
        dn = ds * (sg * (1.0 + n * (1.0 - sg)))
        dxh = dn * g_ref[...]
        dc_ref[...] = rstd * (dxh - jnp.mean(dxh, axis=-1, keepdims=True) - xh * jnp.mean(dxh * xh, axis=-1, keepdims=True))
        dg_ref[...] += jnp.sum(dn * xh, axis=0, keepdims=True)
        db_ref[...] += jnp.sum(dn, axis=0, keepdims=True)
        dbo_ref[...] += jnp.sum(dh1v, axis=0, keepdims=True)

    return _call(
        body, f"conv_out_bwd{i}", (t // tm,), [_row(tm, D), _row(tm, D), _lay(i, (1, D)), _lay(i, (1, D)), _res((D, D))],
        [_row(tm, D), _acc((1, D)), _acc((1, D)), _acc((1, D))], [_sds((t, D), F32)] + [_sds((1, D), F32)] * 3,
        (dh1, c, ln_g, ln_b, w_out), xch=xch)


def conv_mid_bwd(dc, a, big, dw, i, tm, tpb, xch):
    t = dc.shape[0]
    nsteps = t // tm

    def body(dc_ref, nxt_ref, a_ref, prv_ref, big_ref, dw_ref, da_ref, dbin_ref, ddw_ref, dce, ae, wacc, bacc, taps):
        i_ = pl.program_id(0)
        _spread_taps(dw_ref, taps)
        _init(wacc, i_ == 0)
        _init(bacc, i_ == 0)
        dce[0:tm] = dc_ref[...]
        dce[tm:] = jnp.where(i_ % tpb == tpb - 1, 0.0, nxt_ref[...])
        ae[0:HALO] = jnp.where(i_ % tpb == 0, 0.0, prv_ref[...])
        ae[HALO:] = a_ref[...]

        def chunk(k, carry):
            r0 = pl.multiple_of(k * CHUNK, CHUNK)
            wdc = _shifted(dce[pl.ds(r0, 2 * CHUNK), :])
            wa = _shifted(ae[pl.ds(r0, 2 * CHUNK), :])
            dcc = wdc[0][0:CHUNK]
            da = jnp.zeros((CHUNK, D), F32)
            for j in range(CW):
                da = da + _tap_weight(taps, j) * _tap(wdc, CW - 1 - j)
                wacc[j] += _fold8(dcc * _tap(wa, j + 2))
            bv = big_ref[pl.ds(r0, CHUNK), :].astype(F32)
            a1, sg = bv[:, :D], _sig(bv[:, D:])
            d1 = da * sg
            d2 = da * a1 * sg * (1.0 - sg)
            da_ref[pl.ds(r0, CHUNK), 0:D] = d1.astype(BF)
            da_ref[pl.ds(r0, CHUNK), D:2 * D] = d2.astype(BF)
            bacc[:, 0:D] += _fold8(d1)
            bacc[:, D:2 * D] += _fold8(d2)
            return carry

        lax.fori_loop(0, tm // CHUNK, chunk, 0)

        @pl.when(i_ == nsteps - 1)
        def _():
            dbin_ref[...] = jnp.sum(bacc[...], axis=0, keepdims=True)
            ddw_ref[...] = jnp.sum(wacc[...], axis=1)

    return _call(
        body, f"conv_mid_bwd{i}", (nsteps,),
        [_row(tm, D), _next_halo(tm, t), _row(tm, D), _prev_halo(tm), _row(tm, 2 * D), _lay(i, (CW, D))],
        [_row(tm, 2 * D), _acc((1, 2 * D)), _acc((CW + 1, D))],
        [_sds((t, 2 * D), BF), _sds((1, 2 * D), F32), _sds((CW + 1, D), F32)],
        (dc, dc, a, a, big, dw),
        scratch=[pltpu.VMEM((tm + HALO, D), F32), pltpu.VMEM((tm + HALO, D), F32),
                 pltpu.VMEM((CW + 1, 8, D), F32), pltpu.VMEM((8, 2 * D), F32), pltpu.VMEM((CW, 8, D), F32)], xch=xch)


def input_grads(dh0, seq):
    bl, lp, _ = dh0.shape
    cb = 256

    def body(dh_ref, gx_ref, gm_ref):
        _init(gm_ref, pl.program_id(1) == 0)
        gx_ref[...] = dh_ref[NMETA:NMETA + seq, :]
        gm_ref[...] += dh_ref[0:NMETA, :]

    return pl.pallas_call(
        body, name="input_grads", grid=(D // cb, bl),
        in_specs=[pl.BlockSpec((None, lp, cb), lambda c, b: (b, 0, c))],
        out_specs=[pl.BlockSpec((None, seq, cb), lambda c, b: (b, 0, c)), pl.BlockSpec((NMETA, cb), lambda c, b: (0, c))],
        out_shape=[_sds((bl, seq, D), F32), _sds((NMETA, D), F32)],
        compiler_params=_cp(2),
    )(dh0)


GATHER_PLAN = {
    "embed": [("conv_w_in", 0)],
    "conv_in_fwd0": [("ffn_w_down", 0)],
    "conv_mid_fwd0": [("ffn_w_gate", 0), ("ffn_w_up", 0), ("conv_w_out", 0)],
    "mixer_ffn_fwd0": [("conv_w_in", 1), ("conv_w_out", 1), ("ffn_w_gate", 1)],
    "conv_in_fwd1": [("w_kv", 0), ("w_q", 0)],
    "conv_mid_fwd1": [("ffn_w_up", 1), ("ffn_w_down", 1)],
    "mixer_ffn_fwd1": [("w_o", 0), ("ffn_w_down", 2)],
    "attn_fwd0": [("ffn_w_gate", 2), ("ffn_w_up", 2), ("w_q", 1), ("w_o", 1)],
    "attn_fwd1": [("ffn_w_gate", 3), ("ffn_w_up", 3), ("ffn_w_down", 3)],
}
EXCHANGE_PLAN = {
    "attn_bwd1": [("ffn_w_down3", ALL), ("ffn_w_gate3", ALL)],
    "dw_down2": [("w_o1", ALL)],
    "ffn_bwd_x2": [("ffn_w_up3", ALL), ("w_q1", ALL)],
    "attn_bwd0": [("ffn_w_down2", ALL), ("ffn_w_gate2", ALL)],
    "dw_down1": [("w_o0", ALL), ("w_q0", H1)],
    "ffn_bwd_x1": [("ffn_w_up2", ALL), ("w_q0", H2), ("w_kv", ALL)],
    "dw_gate1": [("ffn_w_down1", H1)],
    "dw_up1": [("ffn_w_down1", H2)],
    "conv_mid_bwd1": [("ffn_w_gate1", ALL), ("ffn_w_up1", H1)],
    "conv_in_bwd1": [("ffn_w_up1", H2)],
    "dw_down0": [("conv_w_out1", ALL)],
    "ffn_bwd_x0": [("conv_w_in1", ALL)],
    "dw_gate0": [("ffn_w_down0", H1)],
    "dw_up0": [("ffn_w_down0", H2)],
    "conv_out_bwd0": [("ffn_w_gate0", H1)],
    "conv_mid_bwd0": [("ffn_w_gate0", H2), ("ffn_w_up0", H1), ("conv_w_out0", ALL)],
    "dw_conv_in0": [("ffn_w_up0", H2)],
    "conv_in_bwd0": [("conv_w_in0", H1)],
    "tail": [("conv_w_in0", H2)],
}
BIG = {"conv_w_in": "pieces", "conv_w_out": "rows", "w_kv": "rows", "w_q": "rows", "w_o": "rows",
       "ffn_w_gate": "rows", "ffn_w_up": "rows", "ffn_w_down": "rows"}
EXCHANGE_KIND = BIG
TRANSPOSED = ("ffn_w_gate", "ffn_w_up")


def gathered_matrix(name, layer, blocks8):
    if BIG[name] == "rows":
        return blocks8.reshape(NDEV * blocks8.shape[1], blocks8.shape[2])
    return join_columns(blocks8, f"join_{name}{layer}")


def local_step(x, tgt, meta8, w, shards):
    bl, seq, _ = x.shape
    lp = -(-(NMETA + seq) // QB) * QB
    tpb = 4
    tm = lp // tpb
    t = bl * lp
    na = 2
    flat = lambda a: a.reshape(t, D)
    mats = {}

    def riders(carrier):
        return [shards[key] for key in GATHER_PLAN[carrier]]

    def landed(carrier, blocks):
        for key, b8 in zip(GATHER_PLAN[carrier], blocks):
            mats[key] = gathered_matrix(*key, b8)

    h0, got = embed(x, meta8, lp, riders("embed"))
    landed("embed", got)
    h = flat(h0)
    saved = []
    kvs = None
    for l in range(4):
        rec = {"h": h}
        if l < na:
            name = f"conv_in_fwd{l}"
            (rec["u"], rec["big"], rec["a"]), _, got = conv_in_fwd(h, w["norm_mix"], l, mats["conv_w_in", l], w["conv_b_in"], l, tm,
                                                                  riders(name))
            landed(name, got)
            name = f"conv_mid_fwd{l}"
            (rec["c"], rec["s"]), _, got = conv_mid_fwd(rec["a"], w["conv_dw"], w["conv_ln_g"], w["conv_ln_b"], l, tm, tpb,
                                                         riders(name))
            landed(name, got)
            mixed, w_out, lw, bias = rec["s"], mats["conv_w_out", l], l, w["conv_b_out"]
        else:
            j = l - na
            if kvs is None:
                kvs = dict(zip(("kn", "kv", "k", "v"), kv_fwd(h, w["kv_norm"], mats["w_kv", 0], w["k_norm"], tm)))
                kvs["h"] = h
                kvs["k3"], kvs["v3"] = kvs["k"].reshape(bl, lp, KVD), kvs["v"].reshape(bl, lp, KVD)
                kvs["kt"], kvs["vt"] = transpose_seq(kvs["k3"], "transpose_k"), transpose_seq(kvs["v3"], "transpose_v")
                kvs["bias"] = attn_bias()
            rec["u"], rec["q"] = q_fwd(h, w["norm_mix"], l, mats["w_q", j], j, tm)
            name = f"attn_fwd{j}"
            (rec["o"], rec["lse"]), _, got = attn_fwd(rec["q"], kvs["kt"], kvs["v3"], kvs["bias"], w["q_norm"], w["attn_sinks"], j, bl, lp,
                                                      riders(name) if name in GATHER_PLAN else [])
            if name in GATHER_PLAN:
                landed(name, got)
            mixed, w_out, lw, bias = rec["o"], mats["w_o", j], j, None
        name = f"mixer_ffn_fwd{l}"
        (rec["h1"], rec["u2"], rec["g"], rec["up"], rec["hid"], h), _, got = mixer_ffn_fwd(
            h, mixed, w_out, lw, bias, w["norm_ffn"], l, mats["ffn_w_gate", l], mats["ffn_w_up", l], mats["ffn_w_down", l], tm // 2,
            riders(name) if name in GATHER_PLAN else [])
        if name in GATHER_PLAN:
            landed(name, got)
        saved.append(rec)

    dh3, loss_blk = loss_fwd(h.reshape(bl, lp, D), tgt)
    dh = flat(dh3)

    big, small, arrived = {}, {}, {}
    dks, dvs = [], []

    def ride(kernel_name):
        return [(big[nm], EXCHANGE_KIND[nm.rstrip("0123456789")], ks) for nm, ks in EXCHANGE_PLAN.get(kernel_name, [])]

    def landed_x(kernel_name, arrivals):
        for (nm, _), got in zip(EXCHANGE_PLAN.get(kernel_name, []), arrivals):
            arrived.setdefault(nm, []).append(got)

    def dw(name, grad, x, dy, **kw):
        big[grad], got = mm_tn(x, dy, 2 * tm, name, xch=ride(name), **kw)
        landed_x(name, got)

    for l in reversed(range(4)):
        rec = saved[l]
        dw(f"dw_down{l}", f"ffn_w_down{l}", rec["hid"], dh)
        name = f"ffn_bwd_x{l}"
        outs, got, _ = ffn_bwd_x(
            dh, rec["g"], rec["up"], rec["h1"], w["norm_ffn"], l, mats["ffn_w_down", l], mats["ffn_w_gate", l], mats["ffn_w_up", l],
            mats["w_o", l - na] if l >= na else None, tm // 2, ride(name))
        landed_x(name, got)
        dg, du, dh1, small[f"norm_ffn{l}"] = outs[:4]
        dw(f"dw_gate{l}", f"ffn_w_gate{l}", rec["u2"], dg, transposed=True)
        dw(f"dw_up{l}", f"ffn_w_up{l}", rec["u2"], du, transposed=True)
        if l >= na:
            j = l - na
            dw(f"dw_o{j}", f"w_o{j}", rec["o"], dh1)
            name = f"attn_bwd{j}"
            (dq, dk, dv, small[f"q_norm{j}"], small[f"attn_sinks{j}"]), got, _ = attn_bwd(
                rec["q"], kvs["k3"], kvs["kt"], kvs["vt"], kvs["bias"], outs[4], rec["o"], rec["lse"], w["q_norm"], w["attn_sinks"],
                j, bl, lp, ride(name))
            landed_x(name, got)
            dks.append(dk)
            dvs.append(dv)
            dw(f"dw_q{j}", f"w_q{j}", rec["u"], dq)
            dh, small[f"norm_mix{l}"] = proj_bwd(dq, mats["w_q", j], rec["h"], w["norm_mix"], l, dh1, tm, f"q_bwd{j}")[0]
            if l == na:
                dkv, small["k_norm"] = kv_bwd_pre(sum_transposed(*dks), sum_transposed(*dvs), kvs["kv"], w["k_norm"], tm)
                dw("dw_kv", "w_kv", kvs["kn"], dkv)
                dh, small["kv_norm"] = proj_bwd(dkv, mats["w_kv", 0], kvs["h"], w["kv_norm"], None, dh, tm, "kv_bwd")[0]
        else:
            name = f"conv_out_bwd{l}"
            (dc, small[f"conv_ln_g{l}"], small[f"conv_ln_b{l}"], small[f"conv_b_out{l}"]), got, _ = conv_out_bwd(
                dh1, rec["c"], w["conv_ln_g"], w["conv_ln_b"], mats["conv_w_out", l], l, tm, ride(name))
            landed_x(name, got)
            dw(f"dw_conv_out{l}", f"conv_w_out{l}", rec["s"], dh1)
            name = f"conv_mid_bwd{l}"
            (da, small[f"conv_b_in{l}"], small[f"conv_dw{l}"]), got, _ = conv_mid_bwd(
                dc, rec["a"], rec["big"], w["conv_dw"], l, tm, tpb, ride(name))
            landed_x(name, got)
            dw(f"dw_conv_in{l}", f"conv_w_in{l}", rec["u"], da, split=True)
            name = f"conv_in_bwd{l}"
            (dh, small[f"norm_mix{l}"]), got, _ = proj_bwd(da, mats["conv_w_in", l], rec["h"], w["norm_mix"], l, dh1, tm, name,
                                                           ride(name))
            landed_x(name, got)
    grad_x, small["meta_tokens"] = input_grads(dh.reshape(bl, lp, D), seq)
    return loss_blk, grad_x, big, arrived, small


def all_gather_blocks(blocks):
    n = len(blocks)

    def body(*refs):
        srcs, outs, sems = refs[:n], refs[n:2 * n], refs[2 * n:]
        _gat_start(srcs, outs, sems)
        _gat_forward(srcs, outs, sems)
        _gat_wait(srcs, outs, sems)

    any_spec = pl.BlockSpec(memory_space=pl.ANY)
    return pl.pallas_call(
        body, name="all_gather_blocks", out_shape=[_sds((NDEV,) + tuple(a.shape), a.dtype) for a in blocks],
        in_specs=[any_spec] * n, out_specs=[any_spec] * n, scratch_shapes=_xch_scratch(n),
    )(*blocks)


def cast_bf16(ws):
    n = len(ws)
    counts = [1 if x.ndim == 2 else x.shape[0] for x in ws]

    def body(*refs):
        outs = iter(refs[n:])
        for a in range(n):
            for l in range(counts[a]):
                next(outs)[...] = (refs[a][...] if ws[a].ndim == 2 else refs[a][l]).astype(BF)

    flat = pl.pallas_call(
        body, name="cast_bf16", out_shape=[_sds(x.shape[-2:], BF) for x, k in zip(ws, counts) for _ in range(k)],
        compiler_params=pltpu.CompilerParams(vmem_limit_bytes=VMEM_LIMIT),
    )(*ws)
    it = iter(flat)
    return [[next(it) for _ in range(k)] for k in counts]


def join_columns(w8, name):
    _, kk, n8 = w8.shape

    def body(x_ref, o_ref):
        o_ref[...] = jnp.concatenate([x_ref[p] for p in range(NDEV)], axis=1)

    return pl.pallas_call(body, name=name, out_shape=_sds((kk, NDEV * n8), w8.dtype),
                          compiler_params=pltpu.CompilerParams(vmem_limit_bytes=VMEM_LIMIT))(w8)


def _adamw_math(w, m, v, g):
    m2 = B1 * m + (1.0 - B1) * g
    v2 = B2 * v + (1.0 - B2) * (g * g)
    mh = m2 / (1.0 - B1 ** STEP)
    vh = v2 / (1.0 - B2 ** STEP)
    return -LR * (mh / (jnp.sqrt(vh) + AEPS) + WD * w), m2, v2


def adamw_big(w, m, v, parts, name, xch=(), gat=()):
    lyr, r, c = w.shape
    by_cols = c >= 512
    blk = (lyr, r, 256) if by_cols else (lyr, 256 if r % 256 == 0 else r, c)
    imap = (lambda i: (0, 0, i)) if by_cols else (lambda i: (0, i, 0))
    counts = [len(per_layer) for per_layer in parts]

    def body(w_ref, m_ref, v_ref, *rest):
        p_refs, (g_ref, d_ref, m2_ref, v2_ref) = iter(rest[:sum(counts)]), rest[sum(counts):]
        for l in range(lyr):
            g = None
            for _ in range(counts[l]):
                ref = next(p_refs)
                for q in range(ref.shape[0]):
                    g = ref[q].astype(F32) if g is None else g + ref[q].astype(F32)
            g_ref[l] = g
            d_ref[l], m2_ref[l], v2_ref[l] = _adamw_math(w_ref[l], m_ref[l], v_ref[l], g)

    spec = pl.BlockSpec(blk, imap)
    flat = [a for per_layer in parts for a in per_layer]
    pspecs = [pl.BlockSpec((a.shape[0],) + blk[1:], imap) for a in flat]
    return _call(body, name, ((c // 256) if by_cols else (r // blk[1]),), [spec, spec, spec] + pspecs,
                 [spec] * 4, [_sds((lyr, r, c), F32)] * 4, (w, m, v, *flat), xch=xch, gat=gat)


SMALL_ROWS = 104
REPLICATED = {"norm_mix": (0, 4, D), "norm_ffn": (4, 4, D), "kv_norm": (8, 1, D), "k_norm": (9, 1, HD), "q_norm": (10, 2, HD),
              "attn_sinks": (12, 2, NH)}
LOSS_ROW = 14
SHARDED = {"meta_tokens": (16, NMETA), "conv_b_in": (32, 4), "conv_dw": (36, 2 * CW), "conv_ln_g": (98, 2), "conv_ln_b": (100, 2),
           "conv_b_out": (102, 2)}


def pack_small(gs, loss_blk):
    order = ([f"norm_mix{l}" for l in range(4)] + [f"norm_ffn{l}" for l in range(4)] + ["kv_norm", "k_norm", "q_norm0", "q_norm1",
             "attn_sinks0", "attn_sinks1", "meta_tokens", "conv_b_in0", "conv_b_in1", "conv_dw0", "conv_dw1", "conv_ln_g0",
             "conv_ln_g1", "conv_ln_b0", "conv_ln_b1", "conv_b_out0", "conv_b_out1"])

    def body(*refs):
        r = dict(zip(order, refs))
        loss_ref, o_ref = refs[len(order)], refs[len(order) + 1]
        o_ref[...] = jnp.zeros_like(o_ref)
        for l in range(4):
            o_ref[l:l + 1, :] = r[f"norm_mix{l}"][...]
            o_ref[4 + l:5 + l, :] = r[f"norm_ffn{l}"][...]
        o_ref[8:9, :] = r["kv_norm"][...]
        o_ref[9:10, 0:HD] = r["k_norm"][...]
        for j in range(2):
            o_ref[10 + j:11 + j, 0:HD] = r[f"q_norm{j}"][...]
            o_ref[12 + j:13 + j, 0:NH] = r[f"attn_sinks{j}"][...]
            o_ref[32 + 2 * j:33 + 2 * j, :] = r[f"conv_b_in{j}"][:, 0:D]
            o_ref[33 + 2 * j:34 + 2 * j, :] = r[f"conv_b_in{j}"][:, D:2 * D]
            o_ref[36 + CW * j:36 + CW * (j + 1), :] = r[f"conv_dw{j}"][0:CW, :]
            o_ref[98 + j:99 + j, :] = r[f"conv_ln_g{j}"][...]
            o_ref[100 + j:101 + j, :] = r[f"conv_ln_b{j}"][...]
            o_ref[102 + j:103 + j, :] = r[f"conv_b_out{j}"][...]
        o_ref[LOSS_ROW:LOSS_ROW + 1, 0:1] = loss_ref[0:1, 0:1]
        o_ref[16:16 + NMETA, :] = r["meta_tokens"][...]

    return pl.pallas_call(body, name="pack_small", out_shape=_sds((SMALL_ROWS, D), F32))(*[gs[k] for k in order], loss_blk)


def adamw_small(g8, wts, mom, var):
    names = list(REPLICATED) + list(SHARDED)
    shape2 = {"kv_norm": (1, D), "k_norm": (1, HD)}
    ins = [a[k].reshape(shape2.get(k, a[k].shape)) for a in (wts, mom, var) for k in names]
    n = len(names)

    def body(*refs):
        g8_ref, w_refs, m_refs, v_refs = refs[0], refs[1:1 + n], refs[1 + n:1 + 2 * n], refs[1 + 2 * n:1 + 3 * n]
        loss_ref, outs, red_ref = refs[1 + 3 * n], refs[2 + 3 * n:-1], refs[-1]
        me = _my_index()
        acc = g8_ref[0]
        for q in range(1, NDEV):
            acc = acc + g8_ref[q]
        red_ref[...] = acc
        loss_ref[...] = red_ref[LOSS_ROW:LOSS_ROW + 1, 0:1]

        def mine(rows, width):
            acc = jnp.zeros((rows.stop - rows.start, width), F32)
            for p_ in range(NDEV):
                acc = acc + jnp.where(me == p_, red_ref[rows, p_ * width:(p_ + 1) * width], 0.0)
            return acc

        for i, k in enumerate(names):
            if k in REPLICATED:
                r0, nr, width = REPLICATED[k]
                g = red_ref[r0:r0 + nr, 0:width]
            elif k == "conv_b_in":
                half = D // (2 * D // NDEV)
                acc = jnp.zeros((2, 2 * D // NDEV), F32)
                for p_ in range(NDEV):
                    c0 = (p_ % half) * (2 * D // NDEV)
                    part = jnp.concatenate([red_ref[32 + 2 * j + p_ // half:33 + 2 * j + p_ // half, c0:c0 + 2 * D // NDEV]
                                            for j in range(2)], axis=0)
                    acc = acc + jnp.where(me == p_, part, 0.0)
                g = acc
            else:
                r0, nr = SHARDED[k]
                g = mine(slice(r0, r0 + nr), D // NDEV)
            w_, m_, v_ = w_refs[i], m_refs[i], v_refs[i]
            g_out, d_out, m_out, v_out = outs[4 * i:4 * i + 4]
            if k == "conv_dw":
                for j in range(2):
                    gj = g[CW * j:CW * (j + 1)]
                    g_out[j] = gj
                    d_out[j], m_out[j], v_out[j] = _adamw_math(w_[j], m_[j], v_[j], gj)
            else:
                g_out[...] = g
                d_out[...], m_out[...], v_out[...] = _adamw_math(w_[...], m_[...], v_[...], g)

    out_shape = [_sds((1, 1), F32)] + [_sds(ins[i].shape, F32) for i in range(n) for _ in range(4)]
    res = pl.pallas_call(body, name="adamw_small", out_shape=out_shape, scratch_shapes=[pltpu.VMEM((SMALL_ROWS, D), F32)])(g8, *ins)
    out = {k: tuple(o.reshape(wts[k].shape) for o in res[1 + 4 * i:5 + 4 * i]) for i, k in enumerate(names)}
    return res[0], out


NAMES = ["meta_tokens", "norm_mix", "norm_ffn", "conv_w_in", "conv_b_in", "conv_dw", "conv_ln_g", "conv_ln_b", "conv_w_out",
         "conv_b_out", "kv_norm", "w_kv", "k_norm", "w_q", "q_norm", "attn_sinks", "w_o", "ffn_w_gate", "ffn_w_up", "ffn_w_down"]


def kernel(x, meta_tokens, norm_mix, norm_ffn, conv_w_in, conv_b_in, conv_dw, conv_ln_g, conv_ln_b, conv_w_out, conv_b_out, kv_norm, w_kv, k_norm, w_q, q_norm, attn_sinks, w_o, ffn_w_gate, ffn_w_up, ffn_w_down, loss_target, m_meta_tokens, m_norm_mix, m_norm_ffn, m_conv_w_in, m_conv_b_in, m_conv_dw, m_conv_ln_g, m_conv_ln_b, m_conv_w_out, m_conv_b_out, m_kv_norm, m_w_kv, m_k_norm, m_w_q, m_q_norm, m_attn_sinks, m_w_o, m_ffn_w_gate, m_ffn_w_up, m_ffn_w_down, v_meta_tokens, v_norm_mix, v_norm_ffn, v_conv_w_in, v_conv_b_in, v_conv_dw, v_conv_ln_g, v_conv_ln_b, v_conv_w_out, v_conv_b_out, v_kv_norm, v_w_kv, v_k_norm, v_w_q, v_q_norm, v_attn_sinks, v_w_o, v_ffn_w_gate, v_ffn_w_up, v_ffn_w_down):
    wts = dict(zip(NAMES, (meta_tokens, norm_mix, norm_ffn, conv_w_in, conv_b_in, conv_dw, conv_ln_g, conv_ln_b, conv_w_out,
                           conv_b_out, kv_norm, w_kv, k_norm, w_q, q_norm, attn_sinks, w_o, ffn_w_gate, ffn_w_up, ffn_w_down)))
    mom = dict(zip(NAMES, (m_meta_tokens, m_norm_mix, m_norm_ffn, m_conv_w_in, m_conv_b_in, m_conv_dw, m_conv_ln_g, m_conv_ln_b,
                           m_conv_w_out, m_conv_b_out, m_kv_norm, m_w_kv, m_k_norm, m_w_q, m_q_norm, m_attn_sinks, m_w_o,
                           m_ffn_w_gate, m_ffn_w_up, m_ffn_w_down)))
    var = dict(zip(NAMES, (v_meta_tokens, v_norm_mix, v_norm_ffn, v_conv_w_in, v_conv_b_in, v_conv_dw, v_conv_ln_g, v_conv_ln_b,
                           v_conv_w_out, v_conv_b_out, v_kv_norm, v_w_kv, v_k_norm, v_w_q, v_q_norm, v_attn_sinks, v_w_o,
                           v_ffn_w_gate, v_ffn_w_up, v_ffn_w_down)))
    for k in TRANSPOSED:
        wts[k], mom[k], var[k] = (jnp.swapaxes(a, 1, 2) for a in (wts[k], mom[k], var[k]))

    big_names = list(BIG)
    layers = cast_bf16([wts[k] for k in big_names])
    shards = {(k, l): blk for k, per_layer in zip(big_names, layers) for l, blk in enumerate(per_layer)}
    vec_names = ["meta_tokens", "conv_b_in", "conv_dw", "conv_ln_g", "conv_ln_b", "conv_b_out"]
    full = dict(zip(vec_names, all_gather_blocks([wts[k] for k in vec_names])))
    join_vec = lambda a: jnp.moveaxis(a, 0, -2).reshape(a.shape[1:-1] + (NDEV * a.shape[-1],))
    w = {}
    w["conv_b_in"] = join_vec(full["conv_b_in"]).reshape(2, 1, 2 * D)
    w["conv_dw"] = join_vec(full["conv_dw"])
    for k in ("conv_ln_g", "conv_ln_b", "conv_b_out"):
        w[k] = join_vec(full[k]).reshape(2, 1, D)
    w["norm_mix"] = norm_mix.reshape(4, 1, D)
    w["norm_ffn"] = norm_ffn.reshape(4, 1, D)
    w["kv_norm"] = kv_norm.reshape(1, D)
    w["k_norm"] = k_norm.reshape(1, HD)
    w["q_norm"] = q_norm.reshape(2, 1, HD)
    w["attn_sinks"] = attn_sinks.reshape(2, 1, NH)

    loss_blk, grad_x, gbig, arrived, gs = local_step(x, loss_target, full["meta_tokens"], w, shards)

    packed = pack_small(gs, loss_blk)

    grads, delta, new_m, new_v = {}, {}, {}, {}
    tail = EXCHANGE_PLAN["tail"]
    waiting = {nm.rstrip("0123456789") for nm, _ in tail}
    order = sorted([k for k in big_names if k not in waiting], key=lambda k: -wts[k].size) + [k for k in big_names if k in waiting]
    small8 = None
    for pos, k in enumerate(order):
        flat2 = wts[k].ndim == 2
        as3 = (lambda a: a[None]) if flat2 else (lambda a: a)
        riders = tail if pos == 0 else []
        gat = [packed] if pos == 1 else []
        parts = [arrived[k]] if flat2 else [arrived[f"{k}{i}"] for i in range(wts[k].shape[0])]
        outs, got_x, got_g = adamw_big(as3(wts[k]), as3(mom[k]), as3(var[k]), parts, "adamw_" + k,
                                       xch=[(gbig[nm], EXCHANGE_KIND[nm.rstrip("0123456789")], ks) for nm, ks in riders], gat=gat)
        for (nm, _), got in zip(riders, got_x):
            arrived[nm].append(got)
        if gat:
            small8 = got_g[0]
        grads[k], delta[k], new_m[k], new_v[k] = [o[0] if flat2 else (jnp.swapaxes(o, 1, 2) if k in TRANSPOSED else o) for o in outs]
    loss, small = adamw_small(small8, wts, mom, var)
    for k, (g_, d_, m_, v_) in small.items():
        grads[k], delta[k], new_m[k], new_v[k] = g_, d_, m_, v_
    return (loss.reshape(()), grad_x, *[grads[k] for k in NAMES], *[delta[k] for k in NAMES], *[new_m[k] for k in NAMES],
            *[new_v[k] for k in NAMES])
```

```python
import jax
import jax.numpy as jnp
from jax import lax
from jax.experimental import pallas as pl
from jax.experimental.pallas import tpu as pltpu

F32 = jnp.float32
BF = jnp.bfloat16

D = 1024
DFF = 2816
NH = 16
NKV = 4
HD = 64
KVD = NKV * HD
NMETA = 16
CW = 31
HALO = 32
CHUNK = 32
QB = 128
EPS = 1e-6
NEG = -1e30
NDEV = 8
SCALE = HD ** -0.5

LR, B1, B2, AEPS, WD, STEP = 0.001, 0.9, 0.999, 1e-08, 0.01, 10

VMEM_LIMIT = 56 * 2 ** 20
MESH = pl.DeviceIdType.MESH


def _cp(n):
    return pltpu.CompilerParams(dimension_semantics=("arbitrary",) * n, vmem_limit_bytes=VMEM_LIMIT)


def _row(tm, c):
    return pl.BlockSpec((tm, c), lambda i: (i, 0))


def _res(shape):
    return pl.BlockSpec(shape, lambda i: (0,) * len(shape), pipeline_mode=pl.Buffered(1))


def _lay(l, shape):
    return pl.BlockSpec((None,) + tuple(shape), lambda i: (l,) + (0,) * len(shape), pipeline_mode=pl.Buffered(1))


def _acc(shape):
    return pl.BlockSpec(shape, lambda i: (0,) * len(shape))


def _sds(shape, dt):
    return jax.ShapeDtypeStruct(tuple(shape), dt)


def _dot(a, b):
    return jnp.dot(a.astype(BF), b.astype(BF), preferred_element_type=F32)


def _dot_nt(a, b):
    return lax.dot_general(a.astype(BF), b.astype(BF), (((1,), (1,)), ((), ())), preferred_element_type=F32)


def _dot_tn(a, b):
    return lax.dot_general(a.astype(BF), b.astype(BF), (((0,), (0,)), ((), ())), preferred_element_type=F32)


def _rstd(x):
    return lax.rsqrt(jnp.mean(x * x, axis=-1, keepdims=True) + EPS)


def _rms_bwd(x, g, dy):
    r = _rstd(x)
    z = dy * g
    dx = r * z - x * (r * r * r * jnp.mean(z * x, axis=-1, keepdims=True))
    return dx, jnp.sum(dy * x * r, axis=0, keepdims=True)


def _sig(x):
    return jax.nn.sigmoid(x)


def _fold8(x):
    out = x[0:8]
    for k in range(1, x.shape[0] // 8):
        out = out + x[8 * k:8 * k + 8]
    return out


def _shifted(win):
    return [win] + [pltpu.roll(win, 2 * CHUNK - rho, 0) for rho in range(1, 8)]


def _tap(phases, o):
    return phases[o % 8][8 * (o // 8):8 * (o // 8) + CHUNK]


def _spread_taps(dw_ref, taps):
    @pl.when(pl.program_id(0) == 0)
    def _():
        for j in range(CW):
            taps[j] = jnp.broadcast_to(dw_ref[j:j + 1, :], taps.shape[1:])


def _tap_weight(taps, j):
    return jnp.concatenate([taps[j]] * (CHUNK // 8), axis=0)


def _init(ref, first):
    @pl.when(first)
    def _():
        ref[...] = jnp.zeros_like(ref)


def _my_index():
    return 4 * lax.axis_index("x") + 2 * lax.axis_index("y") + lax.axis_index("c")


def _coords(idx):
    return (idx // 4, (idx // 2) % 2, idx % 2)


ALL = tuple(range(NDEV))
H1, H2 = (0, 1, 2, 4, 6), (3, 5, 7)


def _xch_shapes(xch):
    return [_sds((len(ks),) + ((a.shape[0] // NDEV, a.shape[1]) if k == "rows" else tuple(a.shape[1:])), a.dtype) for a, k, ks in xch]


def _xch_scratch(n):
    return [pltpu.SemaphoreType.DMA((n, NDEV)), pltpu.SemaphoreType.DMA((n, NDEV)), pltpu.SemaphoreType.DMA((n,))]


def _xch_copies(meta, srcs, outs, sems, arrivals):
    send_sems, recv_sems, local_sems = sems
    me = _my_index()

    def piece(a, p):
        if meta[a][0] == "rows":
            r = srcs[a].shape[0] // NDEV
            return srcs[a].at[pl.ds(p * r, r), :]
        return srcs[a].at[p]

    def remote(a, i, k, src):
        return pltpu.make_async_remote_copy(
            src_ref=src, dst_ref=outs[a].at[i], send_sem=send_sems.at[a, k], recv_sem=recv_sems.at[a, k],
            device_id=_coords(me ^ k), device_id_type=MESH)

    local, sends, recvs = [], [], []
    for a, (_, ks) in enumerate(meta):
        for i, k in enumerate(ks):
            if k == 0:
                local.append(pltpu.make_async_copy(piece(a, me), outs[a].at[i], local_sems.at[a]))
            else:
                sends.append(remote(a, i, k, piece(a, me ^ k)))
                if arrivals:
                    recvs.append(remote(a, i, k, piece(a, me)))
    return local, sends, recvs


def _xch_start(meta, srcs, outs, sems):
    local, sends, _ = _xch_copies(meta, srcs, outs, sems, False)
    for cp in local + sends:
        cp.start()


def _xch_wait(meta, srcs, outs, sems):
    local, sends, recvs = _xch_copies(meta, srcs, outs, sems, True)
    for cp in recvs:
        cp.wait_recv()
    for cp in sends:
        cp.wait_send()
    for cp in local:
        cp.wait()


def _gat_copies(srcs, outs, sems):
    send_sems, recv_sems, local_sems = sems
    x, y, c = lax.axis_index("x"), lax.axis_index("y"), lax.axis_index("c")
    me, sibling = (x, y, c), (x, y, 1 - c)
    chips = [(1 - x, y), (x, 1 - y), (1 - x, 1 - y)]

    def copy(a, k, owner, to, from_block=False):
        slot = outs[a].at[4 * owner[0] + 2 * owner[1] + owner[2]]
        return pltpu.make_async_remote_copy(
            src_ref=srcs[a] if from_block else slot, dst_ref=slot, send_sem=send_sems.at[a, k], recv_sem=recv_sems.at[a, k],
            device_id=to, device_id_type=MESH)

    n = len(srcs)
    local = lambda: [pltpu.make_async_copy(srcs[a], outs[a].at[4 * x + 2 * y + c], local_sems.at[a]) for a in range(n)]
    first = lambda: [cp for a in range(n) for cp in
                     [copy(a, 0, me, sibling, True)] + [copy(a, 1 + j, me, (*chip, c), True) for j, chip in enumerate(chips)]]
    landed = lambda: [copy(a, 1 + j, (*chip, c), me) for a in range(n) for j, chip in enumerate(chips)]
    passed = lambda: [copy(a, 4 + j, (*chip, c), sibling) for a in range(n) for j, chip in enumerate(chips)]
    final = lambda: [cp for a in range(n) for cp in
                     [copy(a, 0, sibling, me)] + [copy(a, 4 + j, (*chip, 1 - c), me) for j, chip in enumerate(chips)]]
    return local, first, landed, passed, final


def _gat_start(srcs, outs, sems):
    local, first, _, _, _ = _gat_copies(srcs, outs, sems)
    for cp in local() + first():
        cp.start()


def _gat_forward(srcs, outs, sems):
    _, _, landed, passed, _ = _gat_copies(srcs, outs, sems)
    for got, on in zip(landed(), passed()):
        got.wait_recv()
        on.start()


def _gat_wait(srcs, outs, sems):
    local, first, _, passed, final = _gat_copies(srcs, outs, sems)
    for cp in final():
        cp.wait_recv()
    for cp in first() + passed():
        cp.wait_send()
    for cp in local():
        cp.wait()


def _call(body, name, grid, in_specs, out_specs, out_shape, args, scratch=(), xch=(), gat=()):
    n_in, n_out, n_x, n_g, n_s = len(in_specs), len(out_specs), len(xch), len(gat), len(scratch)
    kinds = [(k, ks) for _, k, ks in xch]
    total = 1
    for g in grid:
        total *= g

    def wrapped(*refs):
        ins, refs = refs[:n_in], refs[n_in:]
        x_src, refs = refs[:n_x], refs[n_x:]
        g_src, refs = refs[:n_g], refs[n_g:]
        outs, refs = refs[:n_out], refs[n_out:]
        x_out, refs = refs[:n_x], refs[n_x:]
        g_out, refs = refs[:n_g], refs[n_g:]
        own, refs = refs[:n_s], refs[n_s:]
        x_sems, g_sems = (refs[:3], refs[3:]) if n_x else ((), refs)
        step = pl.program_id(0)
        for d in range(1, len(grid)):
            step = step * grid[d] + pl.program_id(d)
        if n_x or n_g:
            @pl.when(step == 0)
            def _():
                if n_x:
                    _xch_start(kinds, x_src, x_out, x_sems)
                if n_g:
                    _gat_start(g_src, g_out, g_sems)

        body(*ins, *outs, *own)
        if n_g:
            @pl.when(step == max(total - 2, 0))
            def _():
                _gat_forward(g_src, g_out, g_sems)

        if n_x or n_g:
            @pl.when(step == total - 1)
            def _():
                if n_x:
                    _xch_wait(kinds, x_src, x_out, x_sems)
                if n_g:
                    _gat_wait(g_src, g_out, g_sems)

    any_spec = pl.BlockSpec(memory_space=pl.ANY)
    g_shapes = [_sds((NDEV,) + tuple(a.shape), a.dtype) for a in gat]
    res = pl.pallas_call(
        wrapped, name=name, grid=grid, in_specs=list(in_specs) + [any_spec] * (n_x + n_g),
        out_specs=list(out_specs) + [any_spec] * (n_x + n_g), out_shape=list(out_shape) + _xch_shapes(xch) + g_shapes,
        scratch_shapes=list(scratch) + (_xch_scratch(n_x) if n_x else []) + (_xch_scratch(n_g) if n_g else []),
        compiler_params=_cp(len(grid)),
    )(*args, *[a for a, _, _ in xch], *gat)
    return res[:n_out], res[n_out:n_out + n_x], res[n_out + n_x:]


def embed(x, meta8, lp, gat):
    bl, seq, _ = x.shape
    c8 = D // NDEV
    cb = 2 * c8

    def body(x_ref, m_ref, h_ref):
        h_ref[0:NMETA, :] = jnp.concatenate([m_ref[0], m_ref[1]], axis=1)
        h_ref[NMETA:NMETA + seq, :] = x_ref[...]
        h_ref[NMETA + seq:, :] = jnp.zeros((lp - NMETA - seq, cb), F32)

    (h0,), _, got = _call(
        body, "embed", (bl, D // cb),
        [pl.BlockSpec((None, seq, cb), lambda b, c: (b, 0, c)), pl.BlockSpec((2, NMETA, c8), lambda b, c: (c, 0, 0))],
        [pl.BlockSpec((None, lp, cb), lambda b, c: (b, 0, c))], [_sds((bl, lp, D), F32)], (x, meta8), gat=gat)
    return h0, got


def conv_in_fwd(h, nm, l, w_in, b_in, i, tm, gat):
    t = h.shape[0]

    def body(h_ref, g_ref, w_ref, b_ref, u_ref, big_ref, a_ref):
        x = h_ref[...]
        ub = (x * _rstd(x) * g_ref[...]).astype(BF)
        u_ref[...] = ub
        big = jnp.dot(ub, w_ref[...], preferred_element_type=F32) + b_ref[...]
        big_ref[...] = big.astype(BF)
        a_ref[...] = big[:, :D] * _sig(big[:, D:])

    return _call(
        body, f"conv_in_fwd{i}", (t // tm,), [_row(tm, D), _lay(l, (1, D)), _res((D, 2 * D)), _lay(i, (1, 2 * D))],
        [_row(tm, D), _row(tm, 2 * D), _row(tm, D)], [_sds((t, D), BF), _sds((t, 2 * D), BF), _sds((t, D), F32)],
        (h, nm, w_in, b_in), gat=gat)


def _prev_halo(tm):
    return pl.BlockSpec((HALO, D), lambda i: (jnp.maximum(i * (tm // HALO) - 1, 0), 0))


def _next_halo(tm, t):
    return pl.BlockSpec((HALO, D), lambda i: (jnp.minimum((i + 1) * (tm // HALO), t // HALO - 1), 0))


def conv_mid_fwd(a, dw, ln_g, ln_b, i, tm, tpb, gat):
    t = a.shape[0]

    def body(a_ref, halo_ref, dw_ref, g_ref, b_ref, c_ref, s_ref, ext, taps):
        _spread_taps(dw_ref, taps)
        first = pl.program_id(0) % tpb == 0
        ext[0:HALO] = jnp.where(first, 0.0, halo_ref[...])
        ext[HALO:] = a_ref[...]

        def chunk(k, carry):
            r0 = pl.multiple_of(k * CHUNK, CHUNK)
            win = _shifted(ext[pl.ds(r0, 2 * CHUNK), :])
            c = jnp.zeros((CHUNK, D), F32)
            for j in range(CW):
                c = c + _tap_weight(taps, j) * _tap(win, j + 2)
            c_ref[pl.ds(r0, CHUNK), :] = c
            mu = jnp.mean(c, axis=-1, keepdims=True)
            xc = c - mu
            n = xc * lax.rsqrt(jnp.mean(xc * xc, axis=-1, keepdims=True) + EPS) * g_ref[...] + b_ref[...]
            s_ref[pl.ds(r0, CHUNK), :] = (n * _sig(n)).astype(BF)
            return carry

        lax.fori_loop(0, tm // CHUNK, chunk, 0, unroll=2)

    return _call(
        body, f"conv_mid_fwd{i}", (t // tm,),
        [_row(tm, D), _prev_halo(tm), _lay(i, (CW, D)), _lay(i, (1, D)), _lay(i, (1, D))],
        [_row(tm, D), _row(tm, D)], [_sds((t, D), F32), _sds((t, D), BF)], (a, a, dw, ln_g, ln_b),
        scratch=[pltpu.VMEM((tm + HALO, D), F32), pltpu.VMEM((CW, 8, D), F32)], gat=gat)


def mixer_ffn_fwd(h, s, w_out, lw, bias, nf, l, wg, wu, wd, tm, gat):
    t = h.shape[0]

    def body(*refs):
        if bias is None:
            h_ref, s_ref, w_ref, nf_ref, wg_ref, wu_ref, wd_ref, h1_ref, u_ref, g_ref, up_ref, hid_ref, h2_ref = refs
            y = 0.0
        else:
            h_ref, s_ref, w_ref, b_ref, nf_ref, wg_ref, wu_ref, wd_ref, h1_ref, u_ref, g_ref, up_ref, hid_ref, h2_ref = refs
            y = b_ref[...]
        h1 = h_ref[...] + (jnp.dot(s_ref[...], w_ref[...], preferred_element_type=F32) + y)
        h1_ref[...] = h1
        ub = (h1 * _rstd(h1) * nf_ref[...]).astype(BF)
        u_ref[...] = ub
        g = _dot_nt(ub, wg_ref[...])
        up = _dot_nt(ub, wu_ref[...])
        g_ref[...] = g.astype(BF)
        up_ref[...] = up.astype(BF)
        hid = (g * _sig(g) * up).astype(BF)
        hid_ref[...] = hid
        h2_ref[...] = h1 + jnp.dot(hid, wd_ref[...], preferred_element_type=F32)

    ins = [h, s, w_out] + ([] if bias is None else [bias]) + [nf, wg, wu, wd]
    specs = ([_row(tm, D), _row(tm, D), _res((D, D))] + ([] if bias is None else [_lay(lw, (1, D))])
             + [_lay(l, (1, D)), _res((DFF, D)), _res((DFF, D)), _res((DFF, D))])
    return _call(
        body, f"mixer_ffn_fwd{l}", (t // tm,), specs,
        [_row(tm, D), _row(tm, D), _row(tm, DFF), _row(tm, DFF), _row(tm, DFF), _row(tm, D)],
        [_sds((t, D), F32), _sds((t, D), BF), _sds((t, DFF), BF), _sds((t, DFF), BF), _sds((t, DFF), BF), _sds((t, D), F32)],
        ins, gat=gat)


def _seg_rms(x, g, nseg):
    outs = []
    for s in range(nseg):
        xs = x[:, HD * s:HD * s + HD]
        outs.append(xs * _rstd(xs) * g)
    return jnp.concatenate(outs, axis=1)


def kv_fwd(h, kvn, w_kv, kng, tm):
    t = h.shape[0]

    def body(h_ref, g_ref, w_ref, kg_ref, kn_ref, kv_ref, k_ref, v_ref):
        x = h_ref[...]
        kn = (x * _rstd(x) * g_ref[...]).astype(BF)
        kn_ref[...] = kn
        kv = jnp.dot(kn, w_ref[...], preferred_element_type=F32)
        kv_ref[...] = kv
        k_ref[...] = _seg_rms(kv[:, :KVD], kg_ref[...], NKV).astype(BF)
        v_ref[...] = kv[:, KVD:].astype(BF)

    return pl.pallas_call(
        body, name="kv_fwd", grid=(t // tm,),
        in_specs=[_row(tm, D), _res((1, D)), _res((D, 2 * KVD)), _res((1, HD))],
        out_specs=[_row(tm, D), _row(tm, 2 * KVD), _row(tm, KVD), _row(tm, KVD)],
        out_shape=[_sds((t, D), BF), _sds((t, 2 * KVD), F32), _sds((t, KVD), BF), _sds((t, KVD), BF)],
        compiler_params=_cp(1),
    )(h, kvn, w_kv, kng)


def q_fwd(h, nm, l, w_q, j, tm):
    t = h.shape[0]

    def body(h_ref, g_ref, w_ref, u_ref, q_ref):
        x = h_ref[...]
        ub = (x * _rstd(x) * g_ref[...]).astype(BF)
        u_ref[...] = ub
        q_ref[...] = jnp.dot(ub, w_ref[...], preferred_element_type=F32)

    return pl.pallas_call(
        body, name=f"q_fwd{j}", grid=(t // tm,),
        in_specs=[_row(tm, D), _lay(l, (1, D)), _res((D, D))],
        out_specs=[_row(tm, D), _row(tm, D)], out_shape=[_sds((t, D), BF), _sds((t, D), F32)],
        compiler_params=_cp(1),
    )(h, nm, w_q)


RQ = NH // NKV


NKEYS = 2 * QB + NMETA


def _attn_mask(n, start):
    shape = (RQ * QB, NKEYS)
    qpos = n * QB + (lax.broadcasted_iota(jnp.int32, shape, 0) & (QB - 1))
    col = lax.broadcasted_iota(jnp.int32, shape, 1)
    in_band = col < 2 * QB
    kpos = jnp.where(in_band, start + col, col - 2 * QB)
    return (kpos <= qpos) & ((col >= 2 * QB) | ((qpos - kpos < QB) & (kpos >= NMETA)))


def attn_bias():
    def body(o_ref):
        n = pl.program_id(0)
        o_ref[...] = jnp.where(_attn_mask(n, jnp.maximum(n - 1, 0) * QB), 0.0, NEG)

    return pl.pallas_call(
        body, name="attn_bias", grid=(3,), out_specs=pl.BlockSpec((None, RQ * QB, NKEYS), lambda n: (n, 0, 0)),
        out_shape=_sds((3, RQ * QB, NKEYS), F32), compiler_params=_cp(1))()


def _bias_spec():
    return pl.BlockSpec((None, RQ * QB, NKEYS), lambda n: (jnp.minimum(n, 2), 0, 0))


def _keys(ref, band, gs):
    return jnp.concatenate([ref[band, gs], ref[0:NMETA, gs]], axis=0)


def _keys_t(ref, band, gs):
    return jnp.concatenate([ref[gs, band], ref[gs, 0:NMETA]], axis=1)


def transpose_seq(a, name):
    bl, r, c = a.shape

    def body(a_ref, o_ref):
        o_ref[...] = a_ref[...].T

    return pl.pallas_call(
        body, name=name, grid=(bl,), in_specs=[pl.BlockSpec((None, r, c), lambda b: (b, 0, 0))],
        out_specs=pl.BlockSpec((None, c, r), lambda b: (b, 0, 0)), out_shape=_sds((bl, c, r), a.dtype), compiler_params=_cp(1),
    )(a)


def sum_transposed(a0, a1):
    bl, c, r = a0.shape

    def body(a0_ref, a1_ref, o_ref):
        o_ref[...] = (a0_ref[...] + a1_ref[...]).T

    spec = pl.BlockSpec((None, c, r), lambda b: (b, 0, 0))
    return pl.pallas_call(
        body, name="sum_transposed", grid=(bl,), in_specs=[spec, spec],
        out_specs=pl.BlockSpec((r, c), lambda b: (b, 0)), out_shape=_sds((bl * r, c), a0.dtype), compiler_params=_cp(1),
    )(a0, a1)


def _stack_heads(ref, g, fn):
    return jnp.concatenate([fn(ref[:, HD * (g * RQ + r):HD * (g * RQ + r) + HD]) for r in range(RQ)], axis=0)


def _stack_cols(ref, g):
    return jnp.concatenate([ref[:, g * RQ + r:g * RQ + r + 1] for r in range(RQ)], axis=0)


def _stack_sinks(sk_ref, g):
    return jnp.concatenate([jnp.broadcast_to(sk_ref[:, g * RQ + r:g * RQ + r + 1], (QB, 1)) for r in range(RQ)], axis=0)


def attn_fwd(q, kt, v, bias, qg, sinks, j, bl, lp, gat):
    t = q.shape[0]
    nb = lp // QB

    def body(q_ref, kt_ref, v_ref, bias_ref, qg_ref, sk_ref, o_ref, lse_ref):
        n = pl.program_id(0)
        start = pl.multiple_of(jnp.maximum(n - 1, 0) * QB, QB)
        band = pl.ds(start, 2 * QB)
        lane = lax.broadcasted_iota(jnp.int32, (QB, NH), 1)
        ones = jnp.ones((NKEYS, HD), BF)
        pairs = [(b, g) for b in range(bl) for g in range(NKV)]
        gsl = [slice(HD * g, HD * g + HD) for g in range(NKV)]
        qns = [_stack_heads(q_ref.at[b], g, lambda x: (x * _rstd(x) * (qg_ref[...] * SCALE)).astype(BF)) for b, g in pairs]
        ss = [_dot(qns[i], _keys_t(kt_ref.at[b], band, gsl[g])) + bias_ref[...] for i, (b, g) in enumerate(pairs)]
        sinks = [_stack_sinks(sk_ref, g) for g in range(NKV)]
        mxs = [jnp.maximum(jnp.max(ss[i], -1, keepdims=True), sinks[g]) for i, (b, g) in enumerate(pairs)]
        oas = [_dot(jnp.exp(ss[i] - mxs[i]), jnp.concatenate([_keys(v_ref.at[b], band, gsl[g]), ones], axis=1))
               for i, (b, g) in enumerate(pairs)]
        lses = [jnp.zeros((QB, NH), F32) for _ in range(bl)]
        for i, (b, g) in enumerate(pairs):
            den = oas[i][:, HD:HD + 1] + jnp.exp(sinks[g] - mxs[i])
            o = oas[i][:, :HD] * (1.0 / den)
            l = mxs[i] + jnp.log(den)
            for r in range(RQ):
                h = g * RQ + r
                o_ref[b, :, HD * h:HD * h + HD] = o[r * QB:(r + 1) * QB].astype(BF)
                lses[b] = jnp.where(lane == h, l[r * QB:(r + 1) * QB], lses[b])
        for b in range(bl):
            lse_ref[b] = lses[b]

    blk = lambda c: pl.BlockSpec((bl, QB, c), lambda n: (0, n, 0))
    (o, lse), _, got = _call(
        body, f"attn_fwd{j}", (nb,),
        [blk(D), pl.BlockSpec((bl, KVD, lp), lambda n: (0, 0, 0)), pl.BlockSpec((bl, lp, KVD), lambda n: (0, 0, 0)), _bias_spec(),
         pl.BlockSpec((None, 1, HD), lambda n: (j, 0, 0)), pl.BlockSpec((None, 1, NH), lambda n: (j, 0, 0))],
        [blk(D), blk(NH)], [_sds((bl, lp, D), BF), _sds((bl, lp, NH), F32)], (q.reshape(bl, lp, D), kt, v, bias, qg, sinks),
        gat=gat)
    return (o.reshape(t, D), lse.reshape(t, NH)), (), got


def loss_fwd(h, tgt):
    bl, lp, _ = h.shape
    seq = tgt.shape[1]
    cb = 256

    def body(h_ref, t_ref, dh_ref, loss_ref):
        _init(loss_ref, (pl.program_id(0) == 0) & (pl.program_id(1) == 0))
        err = h_ref[NMETA:NMETA + seq, :] - t_ref[...]
        dh_ref[...] = jnp.zeros_like(dh_ref)
        dh_ref[NMETA:NMETA + seq, :] = err * (1.0 / D)
        loss_ref[...] += (0.5 / D) * jnp.sum(err * err)

    return pl.pallas_call(
        body, name="loss_fwd", grid=(bl, D // cb),
        in_specs=[pl.BlockSpec((None, lp, cb), lambda b, c: (b, 0, c)), pl.BlockSpec((None, seq, cb), lambda b, c: (b, 0, c))],
        out_specs=[pl.BlockSpec((None, lp, cb), lambda b, c: (b, 0, c)), pl.BlockSpec((8, 128), lambda b, c: (0, 0))],
        out_shape=[_sds((bl, lp, D), F32), _sds((8, 128), F32)],
        compiler_params=_cp(2),
    )(h, tgt)


def ffn_bwd_x(dh2, g, up, h1, nf, l, wd, wg, wu, w_o, tm, xch):
    t = dh2.shape[0]

    def body(dh2_ref, g_ref, up_ref, h1_ref, nf_ref, wd_ref, wg_ref, wu_ref, *rest):
        if w_o is None:
            dg_ref, du_ref, dh1_ref, dnf_ref = rest
        else:
            wo_ref, dg_ref, du_ref, dh1_ref, dnf_ref, do_ref = rest
        _init(dnf_ref, pl.program_id(0) == 0)
        dh2v = dh2_ref[...]
        dhid = _dot_nt(dh2v, wd_ref[...])
        gv = g_ref[...].astype(F32)
        uv = up_ref[...].astype(F32)
        sg = _sig(gv)
        dgv = (dhid * uv * (sg * (1.0 + gv * (1.0 - sg)))).astype(BF)
        duv = (dhid * (gv * sg)).astype(BF)
        dg_ref[...] = dgv
        du_ref[...] = duv
        dnorm = _dot(dgv, wg_ref[...]) + _dot(duv, wu_ref[...])
        dx, dnf = _rms_bwd(h1_ref[...], nf_ref[...], dnorm)
        dh1 = dh2v + dx
        dh1_ref[...] = dh1
        dnf_ref[...] += dnf
        if w_o is not None:
            do_ref[...] = _dot_nt(dh1, wo_ref[...]).astype(BF)

    attn = w_o is not None
    return _call(
        body, f"ffn_bwd_x{l}", (t // tm,),
        [_row(tm, D), _row(tm, DFF), _row(tm, DFF), _row(tm, D), _lay(l, (1, D)),
         _res((DFF, D)), _res((DFF, D)), _res((DFF, D))] + ([_res((D, D))] if attn else []),
        [_row(tm, DFF), _row(tm, DFF), _row(tm, D), _acc((1, D))] + ([_row(tm, D)] if attn else []),
        [_sds((t, DFF), BF), _sds((t, DFF), BF), _sds((t, D), F32), _sds((1, D), F32)] + ([_sds((t, D), BF)] if attn else []),
        (dh2, g, up, h1, nf, wd, wg, wu) + ((w_o,) if attn else ()), xch=xch)


def mm_tn(x, dy, tm, name, split=False, transposed=False, xch=()):
    t, kk = x.shape
    nn = dy.shape[1]
    n8 = nn // NDEV
    nsteps = t // tm

    def body(x_ref, dy_ref, o_ref, acc):
        i = pl.program_id(0)
        _init(acc, i == 0)
        acc[...] += _dot_tn(x_ref[...], dy_ref[...])

        @pl.when(i == nsteps - 1)
        def _():
            if split:
                for p in range(NDEV):
                    o_ref[p] = acc[:, p * n8:(p + 1) * n8].astype(BF)
            elif transposed:
                o_ref[...] = acc[...].T.astype(BF)
            else:
                o_ref[...] = acc[...].astype(BF)

    oshape = (NDEV, kk, n8) if split else ((nn, kk) if transposed else (kk, nn))
    (out,), got, _ = _call(body, name, (nsteps,), [_row(tm, kk), _row(tm, nn)], [_acc(oshape)], [_sds(oshape, BF)], (x, dy),
                           scratch=[pltpu.VMEM((kk, nn), F32)], xch=xch)
    return out, got


def proj_bwd(dy, w, h, g, lg, dh_in, tm, name, xch=()):
    t = h.shape[0]
    nn = dy.shape[1]
    wspec = _res(w.shape)
    gspec = _res((1, D)) if lg is None else _lay(lg, (1, D))

    def body(dy_ref, w_ref, h_ref, g_ref, dhin_ref, dh_ref, dg_ref):
        _init(dg_ref, pl.program_id(0) == 0)
        du = _dot_nt(dy_ref[...], w_ref[...])
        dx, dg = _rms_bwd(h_ref[...], g_ref[...], du)
        dh_ref[...] = dhin_ref[...] + dx
        dg_ref[...] += dg

    return _call(body, name, (t // tm,), [_row(tm, nn), wspec, _row(tm, D), gspec, _row(tm, D)],
                 [_row(tm, D), _acc((1, D))], [_sds((t, D), F32), _sds((1, D), F32)], (dy, w, h, g, dh_in), xch=xch)


def attn_bwd(q, k, kt, vt, bias, do, o, lse, qg, sinks, j, bl, lp, xch):
    t = q.shape[0]
    nb = lp // QB

    def body(q_ref, k_ref, kt_ref, vt_ref, bias_ref, do_ref, o_ref, lse_ref, qg_ref, sk_ref, dq_ref, dk_ref, dv_ref, dqg_ref,
             dsk_ref):
        n = pl.program_id(0)
        for ref in (dk_ref, dv_ref, dqg_ref, dsk_ref):
            _init(ref, n == 0)
        start = pl.multiple_of(jnp.maximum(n - 1, 0) * QB, QB)
        band = pl.ds(start, 2 * QB)
        lane = lax.broadcasted_iota(jnp.int32, (1, NH), 1)
        dqg = jnp.zeros((1, HD), F32)
        dsk = jnp.zeros((1, NH), F32)
        pairs = [(b, g) for b in range(bl) for g in range(NKV)]
        idx = range(len(pairs))
        gsl = [slice(HD * g, HD * g + HD) for g in range(NKV)]
        qhs = [_stack_heads(q_ref.at[b], g, lambda x: x) for b, g in pairs]
        rss = [_rstd(qhs[i]) for i in idx]
        qns = [(qhs[i] * rss[i] * (qg_ref[...] * SCALE)).astype(BF) for i in idx]
        lss = [_stack_cols(lse_ref.at[b], g) for b, g in pairs]
        dohs = [_stack_heads(do_ref.at[b], g, lambda x: x) for b, g in pairs]
        deltas = [jnp.sum(dohs[i].astype(F32) * _stack_heads(o_ref.at[b], g, lambda x: x).astype(F32), axis=-1, keepdims=True)
                  for i, (b, g) in enumerate(pairs)]
        prs = [jnp.exp(_dot(qns[i], _keys_t(kt_ref.at[b], band, gsl[g])) + bias_ref[...] - lss[i]) for i, (b, g) in enumerate(pairs)]
        dss = [(prs[i] * (_dot(dohs[i], _keys_t(vt_ref.at[b], band, gsl[g])) - deltas[i])).astype(BF)
               for i, (b, g) in enumerate(pairs)]
        for i, (b, g) in enumerate(pairs):
            gs = gsl[g]
            dkt = _dot_tn(qns[i], dss[i])
            dvt = _dot_tn(dohs[i], prs[i])
            dk_ref[b, gs, band] += dkt[:, :2 * QB]
            dv_ref[b, gs, band] += dvt[:, :2 * QB]
            dk_ref[b, gs, 0:NMETA] += dkt[:, 2 * QB:]
            dv_ref[b, gs, 0:NMETA] += dvt[:, 2 * QB:]
        dqns = [_dot(dss[i], _keys(k_ref.at[b], band, gsl[g])) * SCALE for i, (b, g) in enumerate(pairs)]
        for i, (b, g) in enumerate(pairs):
            qh, rs, dqn = qhs[i], rss[i], dqns[i]
            dsink = jnp.exp(_stack_sinks(sk_ref, g) - lss[i]) * deltas[i]
            z = dqn * qg_ref[...]
            dq = rs * z - qh * (rs * rs * rs * jnp.mean(z * qh, axis=-1, keepdims=True))
            dqg = dqg + jnp.sum(dqn * qh * rs, axis=0, keepdims=True)
            for r in range(RQ):
                h = g * RQ + r
                dq_ref[b, :, HD * h:HD * h + HD] = dq[r * QB:(r + 1) * QB]
                dsk = dsk + jnp.where(lane == h, -jnp.sum(dsink[r * QB:(r + 1) * QB]), 0.0)
        dqg_ref[...] += dqg
        dsk_ref[...] += dsk

    blk = lambda c: pl.BlockSpec((bl, QB, c), lambda n: (0, n, 0))
    seq = pl.BlockSpec((bl, lp, KVD), lambda n: (0, 0, 0))
    seq_t = pl.BlockSpec((bl, KVD, lp), lambda n: (0, 0, 0))
    as3 = lambda a: a.reshape(bl, lp, a.shape[-1])
    (dq, dk, dv, dqg, dsk), got, _ = _call(
        body, f"attn_bwd{j}", (nb,),
        [blk(D), seq, seq_t, seq_t, _bias_spec(), blk(D), blk(D), blk(NH),
         pl.BlockSpec((None, 1, HD), lambda n: (j, 0, 0)), pl.BlockSpec((None, 1, NH), lambda n: (j, 0, 0))],
        [blk(D), seq_t, seq_t, pl.BlockSpec((1, HD), lambda n: (0, 0)), pl.BlockSpec((1, NH), lambda n: (0, 0))],
        [_sds((bl, lp, D), F32), _sds((bl, KVD, lp), F32), _sds((bl, KVD, lp), F32), _sds((1, HD), F32), _sds((1, NH), F32)],
        (as3(q), k, kt, vt, bias, as3(do), as3(o), as3(lse), qg, sinks), xch=xch)
    return (dq.reshape(t, D), dk, dv, dqg, dsk), got, ()


def kv_bwd_pre(dk, dv, kv, kng, tm):
    t = kv.shape[0]

    def body(dk_ref, dv_ref, kv_ref, g_ref, dkv_ref, dg_ref):
        _init(dg_ref, pl.program_id(0) == 0)
        dg = jnp.zeros((1, HD), F32)
        outs = []
        for s in range(NKV):
            sl = slice(HD * s, HD * s + HD)
            dx, dgs = _rms_bwd(kv_ref[:, sl], g_ref[...], dk_ref[:, sl])
            outs.append(dx)
            dg = dg + dgs
        dkv_ref[:, :KVD] = jnp.concatenate(outs, axis=1).astype(BF)
        dkv_ref[:, KVD:] = dv_ref[...].astype(BF)
        dg_ref[...] += dg

    return pl.pallas_call(
        body, name="kv_bwd_pre", grid=(t // tm,),
        in_specs=[_row(tm, KVD)] * 2 + [_row(tm, 2 * KVD), _res((1, HD))],
        out_specs=[_row(tm, 2 * KVD), _acc((1, HD))], out_shape=[_sds((t, 2 * KVD), BF), _sds((1, HD), F32)],
        compiler_params=_cp(1),
    )(dk, dv, kv, kng)


def conv_out_bwd(dh1, c, ln_g, ln_b, w_out, i, tm, xch):
    t = dh1.shape[0]

    def body(dh1_ref, c_ref, g_ref, b_ref, w_ref, dc_ref, dg_ref, db_ref, dbo_ref):
        first = pl.program_id(0) == 0
        _init(dg_ref, first)
        _init(db_ref, first)
        _init(dbo_ref, first)
        dh1v = dh1_ref[...]
        ds = _dot_nt(dh1v, w_ref[...])
        cv = c_ref[...]
        xc = cv - jnp.mean(cv, axis=-1, keepdims=True)
        rstd = lax.rsqrt(jnp.mean(xc * xc, axis=-1, keepdims=True) + EPS)
        xh = xc * rstd
        n = xh * g_ref[...] + b_ref[...]
        sg = _sig(n)
        dn = ds * (sg * (1.0 + n * (1.0 - sg)))
        dxh = dn * g_ref[...]
        dc_ref[...] = rstd * (dxh - jnp.mean(dxh, axis=-1, keepdims=True) - xh * jnp.mean(dxh * xh, axis=-1, keepdims=True))
        dg_ref[...] += jnp.sum(dn * xh, axis=0, keepdims=True)
        db_ref[...] += jnp.sum(dn, axis=0, keepdims=True)
        dbo_ref[...] += jnp.sum(dh1v, axis=0, keepdims=True)

    return _call(
        body, f"conv_out_bwd{i}", (t // tm,), [_row(tm, D), _row(tm, D), _lay(i, (1, D)), _lay(i, (1, D)), _res((D, D))],
        [_row(tm, D), _acc((1, D)), _acc((1, D)), _acc((1, D))], [_sds((t, D), F32)] + [_sds((1, D), F32)] * 3,
        (dh1, c, ln_g, ln_b, w_out), xch=xch)


def conv_mid_bwd(dc, a, big, dw, i, tm, tpb, xch):
    t = dc.shape[0]
    nsteps = t // tm

    def body(dc_ref, nxt_ref, a_ref, prv_ref, big_ref, dw_ref, da_ref, dbin_ref, ddw_ref, dce, ae, wacc, bacc, taps):
        i_ = pl.program_id(0)
        _spread_taps(dw_ref, taps)
        _init(wacc, i_ == 0)
        _init(bacc, i_ == 0)
        dce[0:tm] = dc_ref[...]
        dce[tm:] = jnp.where(i_ % tpb == tpb - 1, 0.0, nxt_ref[...])
        ae[0:HALO] = jnp.where(i_ % tpb == 0, 0.0, prv_ref[...])
        ae[HALO:] = a_ref[...]

        def chunk(k, carry):
            r0 = pl.multiple_of(k * CHUNK, CHUNK)
            wdc = _shifted(dce[pl.ds(r0, 2 * CHUNK), :])
            wa = _shifted(ae[pl.ds(r0, 2 * CHUNK), :])
            dcc = wdc[0][0:CHUNK]
            da = jnp.zeros((CHUNK, D), F32)
            for j in range(CW):
                da = da + _tap_weight(taps, j) * _tap(wdc, CW - 1 - j)
                wacc[j] += _fold8(dcc * _tap(wa, j + 2))
            bv = big_ref[pl.ds(r0, CHUNK), :].astype(F32)
            a1, sg = bv[:, :D], _sig(bv[:, D:])
            d1 = da * sg
            d2 = da * a1 * sg * (1.0 - sg)
            da_ref[pl.ds(r0, CHUNK), 0:D] = d1.astype(BF)
            da_ref[pl.ds(r0, CHUNK), D:2 * D] = d2.astype(BF)
            bacc[:, 0:D] += _fold8(d1)
            bacc[:, D:2 * D] += _fold8(d2)
            return carry

        lax.fori_loop(0, tm // CHUNK, chunk, 0)

        @pl.when(i_ == nsteps - 1)
        def _():
            dbin_ref[...] = jnp.sum(bacc[...], axis=0, keepdims=True)
            ddw_ref[...] = jnp.sum(wacc[...], axis=1)

    return _call(
        body, f"conv_mid_bwd{i}", (nsteps,),
        [_row(tm, D), _next_halo(tm, t), _row(tm, D), _prev_halo(tm), _row(tm, 2 * D), _lay(i, (CW, D))],
        [_row(tm, 2 * D), _acc((1, 2 * D)), _acc((CW + 1, D))],
        [_sds((t, 2 * D), BF), _sds((1, 2 * D), F32), _sds((CW + 1, D), F32)],
        (dc, dc, a, a, big, dw),
        scratch=[pltpu.VMEM((tm + HALO, D), F32), pltpu.VMEM((tm + HALO, D), F32),
                 pltpu.VMEM((CW + 1, 8, D), F32), pltpu.VMEM((8, 2 * D), F32), pltpu.VMEM((CW, 8, D), F32)], xch=xch)


def input_grads(dh0, seq):
    bl, lp, _ = dh0.shape
    cb = 256

    def body(dh_ref, gx_ref, gm_ref):
        _init(gm_ref, pl.program_id(1) == 0)
        gx_ref[...] = dh_ref[NMETA:NMETA + seq, :]
        gm_ref[...] += dh_ref[0:NMETA, :]

    return pl.pallas_call(
        body, name="input_grads", grid=(D // cb, bl),
        in_specs=[pl.BlockSpec((None, lp, cb), lambda c, b: (b, 0, c))],
        out_specs=[pl.BlockSpec((None, seq, cb), lambda c, b: (b, 0, c)), pl.BlockSpec((NMETA, cb), lambda c, b: (0, c))],
        out_shape=[_sds((bl, seq, D), F32), _sds((NMETA, D), F32)],
        compiler_params=_cp(2),
    )(dh0)


GATHER_PLAN = {
    "embed": [("conv_w_in", 0)],
    "conv_in_fwd0": [("ffn_w_down", 0)],
    "conv_mid_fwd0": [("ffn_w_gate", 0), ("ffn_w_up", 0), ("conv_w_out", 0)],
    "mixer_ffn_fwd0": [("conv_w_in", 1), ("conv_w_out", 1), ("ffn_w_gate", 1)],
    "conv_in_fwd1": [("w_kv", 0), ("w_q", 0)],
    "conv_mid_fwd1": [("ffn_w_up", 1), ("ffn_w_down", 1)],
    "mixer_ffn_fwd1": [("w_o", 0), ("ffn_w_down", 2)],
    "attn_fwd0": [("ffn_w_gate", 2), ("ffn_w_up", 2), ("w_q", 1), ("w_o", 1)],
    "attn_fwd1": [("ffn_w_gate", 3), ("ffn_w_up", 3), ("ffn_w_down", 3)],
}
EXCHANGE_PLAN = {
    "attn_bwd1": [("ffn_w_down3", ALL), ("ffn_w_gate3", ALL)],
    "dw_down2": [("w_o1", ALL)],
    "ffn_bwd_x2": [("ffn_w_up3", ALL), ("w_q1", ALL)],
    "attn_bwd0": [("ffn_w_down2", ALL), ("ffn_w_gate2", ALL)],
    "dw_down1": [("w_o0", ALL), ("w_q0", H1)],
    "ffn_bwd_x1": [("ffn_w_up2", ALL), ("w_q0", H2), ("w_kv", ALL)],
    "dw_gate1": [("ffn_w_down1", H1)],
    "dw_up1": [("ffn_w_down1", H2)],
    "conv_mid_bwd1": [("ffn_w_gate1", ALL), ("ffn_w_up1", H1)],
    "conv_in_bwd1": [("ffn_w_up1", H2)],
    "dw_down0": [("conv_w_out1", ALL)],
    "ffn_bwd_x0": [("conv_w_in1", ALL)],
    "dw_gate0": [("ffn_w_down0", H1)],
    "dw_up0": [("ffn_w_down0", H2)],
    "conv_out_bwd0": [("ffn_w_gate0", H1)],
    "conv_mid_bwd0": [("ffn_w_gate0", H2), ("ffn_w_up0", H1), ("conv_w_out0", ALL)],
    "dw_conv_in0": [("ffn_w_up0", H2)],
    "conv_in_bwd0": [("conv_w_in0", H1)],
    "tail": [("conv_w_in0", H2)],
}
BIG = {"conv_w_in": "pieces", "conv_w_out": "rows", "w_kv": "rows", "w_q": "rows", "w_o": "rows",
       "ffn_w_gate": "rows", "ffn_w_up": "rows", "ffn_w_down": "rows"}
EXCHANGE_KIND = BIG
TRANSPOSED = ("ffn_w_gate", "ffn_w_up")


def gathered_matrix(name, layer, blocks8):
    if BIG[name] == "rows":
        return blocks8.reshape(NDEV * blocks8.shape[1], blocks8.shape[2])
    return join_columns(blocks8, f"join_{name}{layer}")


def local_step(x, tgt, meta8, w, shards):
    bl, seq, _ = x.shape
    lp = -(-(NMETA + seq) // QB) * QB
    tpb = 4
    tm = lp // tpb
    t = bl * lp
    na = 2
    flat = lambda a: a.reshape(t, D)
    mats = {}

    def riders(carrier):
        return [shards[key] for key in GATHER_PLAN[carrier]]

    def landed(carrier, blocks):
        for key, b8 in zip(GATHER_PLAN[carrier], blocks):
            mats[key] = gathered_matrix(*key, b8)

    h0, got = embed(x, meta8, lp, riders("embed"))
    landed("embed", got)
    h = flat(h0)
    saved = []
    kvs = None
    for l in range(4):
        rec = {"h": h}
        if l < na:
            name = f"conv_in_fwd{l}"
            (rec["u"], rec["big"], rec["a"]), _, got = conv_in_fwd(h, w["norm_mix"], l, mats["conv_w_in", l], w["conv_b_in"], l, tm,
                                                                  riders(name))
            landed(name, got)
            name = f"conv_mid_fwd{l}"
            (rec["c"], rec["s"]), _, got = conv_mid_fwd(rec["a"], w["conv_dw"], w["conv_ln_g"], w["conv_ln_b"], l, tm, tpb,
                                                         riders(name))
            landed(name, got)
            mixed, w_out, lw, bias = rec["s"], mats["conv_w_out", l], l, w["conv_b_out"]
        else:
            j = l - na
            if kvs is None:
                kvs = dict(zip(("kn", "kv", "k", "v"), kv_fwd(h, w["kv_norm"], mats["w_kv", 0], w["k_norm"], tm)))
                kvs["h"] = h
                kvs["k3"], kvs["v3"] = kvs["k"].reshape(bl, lp, KVD), kvs["v"].reshape(bl, lp, KVD)
                kvs["kt"], kvs["vt"] = transpose_seq(kvs["k3"], "transpose_k"), transpose_seq(kvs["v3"], "transpose_v")
                kvs["bias"] = attn_bias()
            rec["u"], rec["q"] = q_fwd(h, w["norm_mix"], l, mats["w_q", j], j, tm)
            name = f"attn_fwd{j}"
            (rec["o"], rec["lse"]), _, got = attn_fwd(rec["q"], kvs["kt"], kvs["v3"], kvs["bias"], w["q_norm"], w["attn_sinks"], j, bl, lp,
                                                      riders(name) if name in GATHER_PLAN else [])
            if name in GATHER_PLAN:
                landed(name, got)
            mixed, w_out, lw, bias = rec["o"], mats["w_o", j], j, None
        name = f"mixer_ffn_fwd{l}"
        (rec["h1"], rec["u2"], rec["g"], rec["up"], rec["hid"], h), _, got = mixer_ffn_fwd(
            h, mixed, w_out, lw, bias, w["norm_ffn"], l, mats["ffn_w_gate", l], mats["ffn_w_up", l], mats["ffn_w_down", l], tm // 2,
            riders(name) if name in GATHER_PLAN else [])
        if name in GATHER_PLAN:
            landed(name, got)
        saved.append(rec)

    dh3, loss_blk = loss_fwd(h.reshape(bl, lp, D), tgt)
    dh = flat(dh3)

    big, small, arrived = {}, {}, {}
    dks, dvs = [], []

    def ride(kernel_name):
        return [(big[nm], EXCHANGE_KIND[nm.rstrip("0123456789")], ks) for nm, ks in EXCHANGE_PLAN.get(kernel_name, [])]

    def landed_x(kernel_name, arrivals):
        for (nm, _), got in zip(EXCHANGE_PLAN.get(kernel_name, []), arrivals):
            arrived.setdefault(nm, []).append(got)

    def dw(name, grad, x, dy, **kw):
        big[grad], got = mm_tn(x, dy, 2 * tm, name, xch=ride(name), **kw)
        landed_x(name, got)

    for l in reversed(range(4)):
        rec = saved[l]
        dw(f"dw_down{l}", f"ffn_w_down{l}", rec["hid"], dh)
        name = f"ffn_bwd_x{l}"
        outs, got, _ = ffn_bwd_x(
            dh, rec["g"], rec["up"], rec["h1"], w["norm_ffn"], l, mats["ffn_w_down", l], mats["ffn_w_gate", l], mats["ffn_w_up", l],
            mats["w_o", l - na] if l >= na else None, tm // 2, ride(name))
        landed_x(name, got)
        dg, du, dh1, small[f"norm_ffn{l}"] = outs[:4]
        dw(f"dw_gate{l}", f"ffn_w_gate{l}", rec["u2"], dg, transposed=True)
        dw(f"dw_up{l}", f"ffn_w_up{l}", rec["u2"], du, transposed=True)
        if l >= na:
            j = l - na
            dw(f"dw_o{j}", f"w_o{j}", rec["o"], dh1)
            name = f"attn_bwd{j}"
            (dq, dk, dv, small[f"q_norm{j}"], small[f"attn_sinks{j}"]), got, _ = attn_bwd(
                rec["q"], kvs["k3"], kvs["kt"], kvs["vt"], kvs["bias"], outs[4], rec["o"], rec["lse"], w["q_norm"], w["attn_sinks"],
                j, bl, lp, ride(name))
            landed_x(name, got)
            dks.append(dk)
            dvs.append(dv)
            dw(f"dw_q{j}", f"w_q{j}", rec["u"], dq)
            dh, small[f"norm_mix{l}"] = proj_bwd(dq, mats["w_q", j], rec["h"], w["norm_mix"], l, dh1, tm, f"q_bwd{j}")[0]
            if l == na:
                dkv, small["k_norm"] = kv_bwd_pre(sum_transposed(*dks), sum_transposed(*dvs), kvs["kv"], w["k_norm"], tm)
                dw("dw_kv", "w_kv", kvs["kn"], dkv)
                dh, small["kv_norm"] = proj_bwd(dkv, mats["w_kv", 0], kvs["h"], w["kv_norm"], None, dh, tm, "kv_bwd")[0]
        else:
            name = f"conv_out_bwd{l}"
            (dc, small[f"conv_ln_g{l}"], small[f"conv_ln_b{l}"], small[f"conv_b_out{l}"]), got, _ = conv_out_bwd(
                dh1, rec["c"], w["conv_ln_g"], w["conv_ln_b"], mats["conv_w_out", l], l, tm, ride(name))
            landed_x(name, got)
            dw(f"dw_conv_out{l}", f"conv_w_out{l}", rec["s"], dh1)
            name = f"conv_mid_bwd{l}"
            (da, small[f"conv_b_in{l}"], small[f"conv_dw{l}"]), got, _ = conv_mid_bwd(
                dc, rec["a"], rec["big"], w["conv_dw"], l, tm, tpb, ride(name))
            landed_x(name, got)
            dw(f"dw_conv_in{l}", f"conv_w_in{l}", rec["u"], da, split=True)
            name = f"conv_in_bwd{l}"
            (dh, small[f"norm_mix{l}"]), got, _ = proj_bwd(da, mats["conv_w_in", l], rec["h"], w["norm_mix"], l, dh1, tm, name,
                                                           ride(name))
            landed_x(name, got)
    grad_x, small["meta_tokens"] = input_grads(dh.reshape(bl, lp, D), seq)
    return loss_blk, grad_x, big, arrived, small


def all_gather_blocks(blocks):
    n = len(blocks)

    def body(*refs):
        srcs, outs, sems = refs[:n], refs[n:2 * n], refs[2 * n:]
        _gat_start(srcs, outs, sems)
        _gat_forward(srcs, outs, sems)
        _gat_wait(srcs, outs, sems)

    any_spec = pl.BlockSpec(memory_space=pl.ANY)
    return pl.pallas_call(
        body, name="all_gather_blocks", out_shape=[_sds((NDEV,) + tuple(a.shape), a.dtype) for a in blocks],
        in_specs=[any_spec] * n, out_specs=[any_spec] * n, scratch_shapes=_xch_scratch(n),
    )(*blocks)


def cast_bf16(ws):
    n = len(ws)
    counts = [1 if x.ndim == 2 else x.shape[0] for x in ws]

    def body(*refs):
        outs = iter(refs[n:])
        for a in range(n):
            for l in range(counts[a]):
                next(outs)[...] = (refs[a][...] if ws[a].ndim == 2 else refs[a][l]).astype(BF)

    flat = pl.pallas_call(
        body, name="cast_bf16", out_shape=[_sds(x.shape[-2:], BF) for x, k in zip(ws, counts) for _ in range(k)],
        compiler_params=pltpu.CompilerParams(vmem_limit_bytes=VMEM_LIMIT),
    )(*ws)
    it = iter(flat)
    return [[next(it) for _ in range(k)] for k in counts]


def join_columns(w8, name):
    _, kk, n8 = w8.shape

    def body(x_ref, o_ref):
        o_ref[...] = jnp.concatenate([x_ref[p] for p in range(NDEV)], axis=1)

    return pl.pallas_call(body, name=name, out_shape=_sds((kk, NDEV * n8), w8.dtype),
                          compiler_params=pltpu.CompilerParams(vmem_limit_bytes=VMEM_LIMIT))(w8)


def _adamw_math(w, m, v, g):
    m2 = B1 * m + (1.0 - B1) * g
    v2 = B2 * v + (1.0 - B2) * (g * g)
    mh = m2 / (1.0 - B1 ** STEP)
    vh = v2 / (1.0 - B2 ** STEP)
    return -LR * (mh / (jnp.sqrt(vh) + AEPS) + WD * w), m2, v2


def adamw_big(w, m, v, parts, name, xch=(), gat=()):
    lyr, r, c = w.shape
    by_cols = c >= 512
    blk = (lyr, r, 256) if by_cols else (lyr, 256 if r % 256 == 0 else r, c)
    imap = (lambda i: (0, 0, i)) if by_cols else (lambda i: (0, i, 0))
    counts = [len(per_layer) for per_layer in parts]

    def body(w_ref, m_ref, v_ref, *rest):
        p_refs, (g_ref, d_ref, m2_ref, v2_ref) = iter(rest[:sum(counts)]), rest[sum(counts):]
        for l in range(lyr):
            g = None
            for _ in range(counts[l]):
                ref = next(p_refs)
                for q in range(ref.shape[0]):
                    g = ref[q].astype(F32) if g is None else g + ref[q].astype(F32)
            g_ref[l] = g
            d_ref[l], m2_ref[l], v2_ref[l] = _adamw_math(w_ref[l], m_ref[l], v_ref[l], g)

    spec = pl.BlockSpec(blk, imap)
    flat = [a for per_layer in parts for a in per_layer]
    pspecs = [pl.BlockSpec((a.shape[0],) + blk[1:], imap) for a in flat]
    return _call(body, name, ((c // 256) if by_cols else (r // blk[1]),), [spec, spec, spec] + pspecs,
                 [spec] * 4, [_sds((lyr, r, c), F32)] * 4, (w, m, v, *flat), xch=xch, gat=gat)


SMALL_ROWS = 104
REPLICATED = {"norm_mix": (0, 4, D), "norm_ffn": (4, 4, D), "kv_norm": (8, 1, D), "k_norm": (9, 1, HD), "q_norm": (10, 2, HD),
              "attn_sinks": (12, 2, NH)}
LOSS_ROW = 14
SHARDED = {"meta_tokens": (16, NMETA), "conv_b_in": (32, 4), "conv_dw": (36, 2 * CW), "conv_ln_g": (98, 2), "conv_ln_b": (100, 2),
           "conv_b_out": (102, 2)}


def pack_small(gs, loss_blk):
    order = ([f"norm_mix{l}" for l in range(4)] + [f"norm_ffn{l}" for l in range(4)] + ["kv_norm", "k_norm", "q_norm0", "q_norm1",
             "attn_sinks0", "attn_sinks1", "meta_tokens", "conv_b_in0", "conv_b_in1", "conv_dw0", "conv_dw1", "conv_ln_g0",
             "conv_ln_g1", "conv_ln_b0", "conv_ln_b1", "conv_b_out0", "conv_b_out1"])

    def body(*refs):
        r = dict(zip(order, refs))
        loss_ref, o_ref = refs[len(order)], refs[len(order) + 1]
        o_ref[...] = jnp.zeros_like(o_ref)
        for l in range(4):
            o_ref[l:l + 1, :] = r[f"norm_mix{l}"][...]
            o_ref[4 + l:5 + l, :] = r[f"norm_ffn{l}"][...]
        o_ref[8:9, :] = r["kv_norm"][...]
        o_ref[9:10, 0:HD] = r["k_norm"][...]
        for j in range(2):
            o_ref[10 + j:11 + j, 0:HD] = r[f"q_norm{j}"][...]
            o_ref[12 + j:13 + j, 0:NH] = r[f"attn_sinks{j}"][...]
            o_ref[32 + 2 * j:33 + 2 * j, :] = r[f"conv_b_in{j}"][:, 0:D]
            o_ref[33 + 2 * j:34 + 2 * j, :] = r[f"conv_b_in{j}"][:, D:2 * D]
            o_ref[36 + CW * j:36 + CW * (j + 1), :] = r[f"conv_dw{j}"][0:CW, :]
            o_ref[98 + j:99 + j, :] = r[f"conv_ln_g{j}"][...]
            o_ref[100 + j:101 + j, :] = r[f"conv_ln_b{j}"][...]
            o_ref[102 + j:103 + j, :] = r[f"conv_b_out{j}"][...]
        o_ref[LOSS_ROW:LOSS_ROW + 1, 0:1] = loss_ref[0:1, 0:1]
        o_ref[16:16 + NMETA, :] = r["meta_tokens"][...]

    return pl.pallas_call(body, name="pack_small", out_shape=_sds((SMALL_ROWS, D), F32))(*[gs[k] for k in order], loss_blk)


def adamw_small(g8, wts, mom, var):
    names = list(REPLICATED) + list(SHARDED)
    shape2 = {"kv_norm": (1, D), "k_norm": (1, HD)}
    ins = [a[k].reshape(shape2.get(k, a[k].shape)) for a in (wts, mom, var) for k in names]
    n = len(names)

    def body(*refs):
        g8_ref, w_refs, m_refs, v_refs = refs[0], refs[1:1 + n], refs[1 + n:1 + 2 * n], refs[1 + 2 * n:1 + 3 * n]
        loss_ref, outs, red_ref = refs[1 + 3 * n], refs[2 + 3 * n:-1], refs[-1]
        me = _my_index()
        acc = g8_ref[0]
        for q in range(1, NDEV):
            acc = acc + g8_ref[q]
        red_ref[...] = acc
        loss_ref[...] = red_ref[LOSS_ROW:LOSS_ROW + 1, 0:1]

        def mine(rows, width):
            acc = jnp.zeros((rows.stop - rows.start, width), F32)
            for p_ in range(NDEV):
                acc = acc + jnp.where(me == p_, red_ref[rows, p_ * width:(p_ + 1) * width], 0.0)
            return acc

        for i, k in enumerate(names):
            if k in REPLICATED:
                r0, nr, width = REPLICATED[k]
                g = red_ref[r0:r0 + nr, 0:width]
            elif k == "conv_b_in":
                half = D // (2 * D // NDEV)
                acc = jnp.zeros((2, 2 * D // NDEV), F32)
                for p_ in range(NDEV):
                    c0 = (p_ % half) * (2 * D // NDEV)
                    part = jnp.concatenate([red_ref[32 + 2 * j + p_ // half:33 + 2 * j + p_ // half, c0:c0 + 2 * D // NDEV]
                                            for j in range(2)], axis=0)
                    acc = acc + jnp.where(me == p_, part, 0.0)
                g = acc
            else:
                r0, nr = SHARDED[k]
                g = mine(slice(r0, r0 + nr), D // NDEV)
            w_, m_, v_ = w_refs[i], m_refs[i], v_refs[i]
            g_out, d_out, m_out, v_out = outs[4 * i:4 * i + 4]
            if k == "conv_dw":
                for j in range(2):
                    gj = g[CW * j:CW * (j + 1)]
                    g_out[j] = gj
                    d_out[j], m_out[j], v_out[j] = _adamw_math(w_[j], m_[j], v_[j], gj)
            else:
                g_out[...] = g
                d_out[...], m_out[...], v_out[...] = _adamw_math(w_[...], m_[...], v_[...], g)

    out_shape = [_sds((1, 1), F32)] + [_sds(ins[i].shape, F32) for i in range(n) for _ in range(4)]
    res = pl.pallas_call(body, name="adamw_small", out_shape=out_shape, scratch_shapes=[pltpu.VMEM((SMALL_ROWS, D), F32)])(g8, *ins)
    out = {k: tuple(o.reshape(wts[k].shape) for o in res[1 + 4 * i:5 + 4 * i]) for i, k in enumerate(names)}
    return res[0], out


NAMES = ["meta_tokens", "norm_mix", "norm_ffn", "conv_w_in", "conv_b_in", "conv_dw", "conv_ln_g", "conv_ln_b", "conv_w_out",
         "conv_b_out", "kv_norm", "w_kv", "k_norm", "w_q", "q_norm", "attn_sinks", "w_o", "ffn_w_gate", "ffn_w_up", "ffn_w_down"]


def kernel(x, meta_tokens, norm_mix, norm_ffn, conv_w_in, conv_b_in, conv_dw, conv_ln_g, conv_ln_b, conv_w_out, conv_b_out, kv_norm, w_kv, k_norm, w_q, q_norm, attn_sinks, w_o, ffn_w_gate, ffn_w_up, ffn_w_down, loss_target, m_meta_tokens, m_norm_mix, m_norm_ffn, m_conv_w_in, m_conv_b_in, m_conv_dw, m_conv_ln_g, m_conv_ln_b, m_conv_w_out, m_conv_b_out, m_kv_norm, m_w_kv, m_k_norm, m_w_q, m_q_norm, m_attn_sinks, m_w_o, m_ffn_w_gate, m_ffn_w_up, m_ffn_w_down, v_meta_tokens, v_norm_mix, v_norm_ffn, v_conv_w_in, v_conv_b_in, v_conv_dw, v_conv_ln_g, v_conv_ln_b, v_conv_w_out, v_conv_b_out, v_kv_norm, v_w_kv, v_k_norm, v_w_q, v_q_norm, v_attn_sinks, v_w_o, v_ffn_w_gate, v_ffn_w_up, v_ffn_w_down):
    wts = dict(zip(NAMES, (meta_tokens, norm_mix, norm_ffn, conv_w_in, conv_b_in, conv_dw, conv_ln_g, conv_ln_b, conv_w_out,
                           conv_b_out, kv_norm, w_kv, k_norm, w_q, q_norm, attn_sinks, w_o, ffn_w_gate, ffn_w_up, ffn_w_down)))
    mom = dict(zip(NAMES, (m_meta_tokens, m_norm_mix, m_norm_ffn, m_conv_w_in, m_conv_b_in, m_conv_dw, m_conv_ln_g, m_conv_ln_b,
                           m_conv_w_out, m_conv_b_out, m_kv_norm, m_w_kv, m_k_norm, m_w_q, m_q_norm, m_attn_sinks, m_w_o,
                           m_ffn_w_gate, m_ffn_w_up, m_ffn_w_down)))
    var = dict(zip(NAMES, (v_meta_tokens, v_norm_mix, v_norm_ffn, v_conv_w_in, v_conv_b_in, v_conv_dw, v_conv_ln_g, v_conv_ln_b,
                           v_conv_w_out, v_conv_b_out, v_kv_norm, v_w_kv, v_k_norm, v_w_q, v_q_norm, v_attn_sinks, v_w_o,
                           v_ffn_w_gate, v_ffn_w_up, v_ffn_w_down)))
    for k in TRANSPOSED:
        wts[k], mom[k], var[k] = (jnp.swapaxes(a, 1, 2) for a in (wts[k], mom[k], var[k]))

    big_names = list(BIG)
    layers = cast_bf16([wts[k] for k in big_names])
    shards = {(k, l): blk for k, per_layer in zip(big_names, layers) for l, blk in enumerate(per_layer)}
    vec_names = ["meta_tokens", "conv_b_in", "conv_dw", "conv_ln_g", "conv_ln_b", "conv_b_out"]
    full = dict(zip(vec_names, all_gather_blocks([wts[k] for k in vec_names])))
    join_vec = lambda a: jnp.moveaxis(a, 0, -2).reshape(a.shape[1:-1] + (NDEV * a.shape[-1],))
    w = {}
    w["conv_b_in"] = join_vec(full["conv_b_in"]).reshape(2, 1, 2 * D)
    w["conv_dw"] = join_vec(full["conv_dw"])
    for k in ("conv_ln_g", "conv_ln_b", "conv_b_out"):
        w[k] = join_vec(full[k]).reshape(2, 1, D)
    w["norm_mix"] = norm_mix.reshape(4, 1, D)
    w["norm_ffn"] = norm_ffn.reshape(4, 1, D)
    w["kv_norm"] = kv_norm.reshape(1, D)
    w["k_norm"] = k_norm.reshape(1, HD)
    w["q_norm"] = q_norm.reshape(2, 1, HD)
    w["attn_sinks"] = attn_sinks.reshape(2, 1, NH)

    loss_blk, grad_x, gbig, arrived, gs = local_step(x, loss_target, full["meta_tokens"], w, shards)

    packed = pack_small(gs, loss_blk)

    grads, delta, new_m, new_v = {}, {}, {}, {}
    tail = EXCHANGE_PLAN["tail"]
    waiting = {nm.rstrip("0123456789") for nm, _ in tail}
    carriers = ["w_q", "w_o"]
    order = carriers + [k for k in big_names if k not in waiting and k not in carriers] + [k for k in big_names if k in waiting]
    small8 = None
    for pos, k in enumerate(order):
        flat2 = wts[k].ndim == 2
        as3 = (lambda a: a[None]) if flat2 else (lambda a: a)
        riders = tail if pos == 0 else []
        gat = [packed] if pos == 1 else []
        parts = [arrived[k]] if flat2 else [arrived[f"{k}{i}"] for i in range(wts[k].shape[0])]
        outs, got_x, got_g = adamw_big(as3(wts[k]), as3(mom[k]), as3(var[k]), parts, "adamw_" + k,
                                       xch=[(gbig[nm], EXCHANGE_KIND[nm.rstrip("0123456789")], ks) for nm, ks in riders], gat=gat)
        for (nm, _), got in zip(riders, got_x):
            arrived[nm].append(got)
        if gat:
            small8 = got_g[0]
        grads[k], delta[k], new_m[k], new_v[k] = [o[0] if flat2 else (jnp.swapaxes(o, 1, 2) if k in TRANSPOSED else o) for o in outs]
    loss, small = adamw_small(small8, wts, mom, var)
    for k, (g_, d_, m_, v_) in small.items():
        grads[k], delta[k], new_m[k], new_v[k] = g_, d_, m_, v_
    return (loss.reshape(()), grad_x, *[grads[k] for k in NAMES], *[delta[k] for k in NAMES], *[new_m[k] for k in NAMES],
            *[new_v[k] for k in NAMES])
```

```python
import jax
import jax.numpy as jnp
from jax import lax
from jax.experimental import pallas as pl
from jax.experimental.pallas import tpu as pltpu

F32 = jnp.float32
BF = jnp.bfloat16

D = 1024
DFF = 2816
NH = 16
NKV = 4
HD = 64
KVD = NKV * HD
NMETA = 16
CW = 31
HALO = 32
CHUNK = 32
QB = 128
EPS = 1e-6
NEG = -1e30
NDEV = 8
SCALE = HD ** -0.5

LR, B1, B2, AEPS, WD, STEP = 0.001, 0.9, 0.999, 1e-08, 0.01, 10

VMEM_LIMIT = 56 * 2 ** 20
MESH = pl.DeviceIdType.MESH


def _cp(n):
    return pltpu.CompilerParams(dimension_semantics=("arbitrary",) * n, vmem_limit_bytes=VMEM_LIMIT)


def _row(tm, c):
    return pl.BlockSpec((tm, c), lambda i: (i, 0))


def _res(shape):
    return pl.BlockSpec(shape, lambda i: (0,) * len(shape), pipeline_mode=pl.Buffered(1))


def _lay(l, shape):
    return pl.BlockSpec((None,) + tuple(shape), lambda i: (l,) + (0,) * len(shape), pipeline_mode=pl.Buffered(1))


def _acc(shape):
    return pl.BlockSpec(shape, lambda i: (0,) * len(shape))


def _sds(shape, dt):
    return jax.ShapeDtypeStruct(tuple(shape), dt)


def _dot(a, b):
    return jnp.dot(a.astype(BF), b.astype(BF), preferred_element_type=F32)


def _dot_nt(a, b):
    return lax.dot_general(a.astype(BF), b.astype(BF), (((1,), (1,)), ((), ())), preferred_element_type=F32)


def _dot_tn(a, b):
    return lax.dot_general(a.astype(BF), b.astype(BF), (((0,), (0,)), ((), ())), preferred_element_type=F32)


def _rstd(x):
    return lax.rsqrt(jnp.mean(x * x, axis=-1, keepdims=True) + EPS)


def _rms_bwd(x, g, dy):
    r = _rstd(x)
    z = dy * g
    dx = r * z - x * (r * r * r * jnp.mean(z * x, axis=-1, keepdims=True))
    return dx, jnp.sum(dy * x * r, axis=0, keepdims=True)


def _sig(x):
    return jax.nn.sigmoid(x)


def _fold8(x):
    out = x[0:8]
    for k in range(1, x.shape[0] // 8):
        out = out + x[8 * k:8 * k + 8]
    return out


def _shifted(win):
    return [win] + [pltpu.roll(win, 2 * CHUNK - rho, 0) for rho in range(1, 8)]


def _tap(phases, o):
    return phases[o % 8][8 * (o // 8):8 * (o // 8) + CHUNK]


def _spread_taps(dw_ref, taps):
    @pl.when(pl.program_id(0) == 0)
    def _():
        for j in range(CW):
            taps[j] = jnp.broadcast_to(dw_ref[j:j + 1, :], taps.shape[1:])


def _tap_weight(taps, j):
    return jnp.concatenate([taps[j]] * (CHUNK // 8), axis=0)


def _init(ref, first):
    @pl.when(first)
    def _():
        ref[...] = jnp.zeros_like(ref)


def _my_index():
    return 4 * lax.axis_index("x") + 2 * lax.axis_index("y") + lax.axis_index("c")


def _coords(idx):
    return (idx // 4, (idx // 2) % 2, idx % 2)


ALL = tuple(range(NDEV))
H1, H2 = (0, 1, 2, 4, 6), (3, 5, 7)


def _xch_shapes(xch):
    return [_sds((len(ks),) + ((a.shape[0] // NDEV, a.shape[1]) if k == "rows" else tuple(a.shape[1:])), a.dtype) for a, k, ks in xch]


def _xch_scratch(n):
    return [pltpu.SemaphoreType.DMA((n, NDEV)), pltpu.SemaphoreType.DMA((n, NDEV)), pltpu.SemaphoreType.DMA((n,))]


def _xch_copies(meta, srcs, outs, sems, arrivals):
    send_sems, recv_sems, local_sems = sems
    me = _my_index()

    def piece(a, p):
        if meta[a][0] == "rows":
            r = srcs[a].shape[0] // NDEV
            return srcs[a].at[pl.ds(p * r, r), :]
        return srcs[a].at[p]

    def remote(a, i, k, src):
        return pltpu.make_async_remote_copy(
            src_ref=src, dst_ref=outs[a].at[i], send_sem=send_sems.at[a, k], recv_sem=recv_sems.at[a, k],
            device_id=_coords(me ^ k), device_id_type=MESH)

    local, sends, recvs = [], [], []
    for a, (_, ks) in enumerate(meta):
        for i, k in enumerate(ks):
            if k == 0:
                local.append(pltpu.make_async_copy(piece(a, me), outs[a].at[i], local_sems.at[a]))
            else:
                sends.append(remote(a, i, k, piece(a, me ^ k)))
                if arrivals:
                    recvs.append(remote(a, i, k, piece(a, me)))
    return local, sends, recvs


def _xch_start(meta, srcs, outs, sems):
    local, sends, _ = _xch_copies(meta, srcs, outs, sems, False)
    for cp in local + sends:
        cp.start()


def _xch_wait(meta, srcs, outs, sems):
    local, sends, recvs = _xch_copies(meta, srcs, outs, sems, True)
    for cp in recvs:
        cp.wait_recv()
    for cp in sends:
        cp.wait_send()
    for cp in local:
        cp.wait()


def _gat_copies(srcs, outs, sems):
    send_sems, recv_sems, local_sems = sems
    x, y, c = lax.axis_index("x"), lax.axis_index("y"), lax.axis_index("c")
    me, sibling = (x, y, c), (x, y, 1 - c)
    chips = [(1 - x, y), (x, 1 - y), (1 - x, 1 - y)]

    def copy(a, k, owner, to, from_block=False):
        slot = outs[a].at[4 * owner[0] + 2 * owner[1] + owner[2]]
        return pltpu.make_async_remote_copy(
            src_ref=srcs[a] if from_block else slot, dst_ref=slot, send_sem=send_sems.at[a, k], recv_sem=recv_sems.at[a, k],
            device_id=to, device_id_type=MESH)

    n = len(srcs)
    local = lambda: [pltpu.make_async_copy(srcs[a], outs[a].at[4 * x + 2 * y + c], local_sems.at[a]) for a in range(n)]
    first = lambda: [cp for a in range(n) for cp in
                     [copy(a, 0, me, sibling, True)] + [copy(a, 1 + j, me, (*chip, c), True) for j, chip in enumerate(chips)]]
    landed = lambda: [copy(a, 1 + j, (*chip, c), me) for a in range(n) for j, chip in enumerate(chips)]
    passed = lambda: [copy(a, 4 + j, (*chip, c), sibling) for a in range(n) for j, chip in enumerate(chips)]
    final = lambda: [cp for a in range(n) for cp in
                     [copy(a, 0, sibling, me)] + [copy(a, 4 + j, (*chip, 1 - c), me) for j, chip in enumerate(chips)]]
    return local, first, landed, passed, final


def _gat_start(srcs, outs, sems):
    local, first, _, _, _ = _gat_copies(srcs, outs, sems)
    for cp in local() + first():
        cp.start()


def _gat_forward(srcs, outs, sems):
    _, _, landed, passed, _ = _gat_copies(srcs, outs, sems)
    for got, on in zip(landed(), passed()):
        got.wait_recv()
        on.start()


def _gat_wait(srcs, outs, sems):
    local, first, _, passed, final = _gat_copies(srcs, outs, sems)
    for cp in final():
        cp.wait_recv()
    for cp in first() + passed():
        cp.wait_send()
    for cp in local():
        cp.wait()


def _call(body, name, grid, in_specs, out_specs, out_shape, args, scratch=(), xch=(), gat=()):
    n_in, n_out, n_x, n_g, n_s = len(in_specs), len(out_specs), len(xch), len(gat), len(scratch)
    kinds = [(k, ks) for _, k, ks in xch]
    total = 1
    for g in grid:
        total *= g

    def wrapped(*refs):
        ins, refs = refs[:n_in], refs[n_in:]
        x_src, refs = refs[:n_x], refs[n_x:]
        g_src, refs = refs[:n_g], refs[n_g:]
        outs, refs = refs[:n_out], refs[n_out:]
        x_out, refs = refs[:n_x], refs[n_x:]
        g_out, refs = refs[:n_g], refs[n_g:]
        own, refs = refs[:n_s], refs[n_s:]
        x_sems, g_sems = (refs[:3], refs[3:]) if n_x else ((), refs)
        step = pl.program_id(0)
        for d in range(1, len(grid)):
            step = step * grid[d] + pl.program_id(d)
        if n_x or n_g:
            @pl.when(step == 0)
            def _():
                if n_x:
                    _xch_start(kinds, x_src, x_out, x_sems)
                if n_g:
                    _gat_start(g_src, g_out, g_sems)

        body(*ins, *outs, *own)
        if n_g:
            @pl.when(step == max(total - 2, 0))
            def _():
                _gat_forward(g_src, g_out, g_sems)

        if n_x or n_g:
            @pl.when(step == total - 1)
            def _():
                if n_x:
                    _xch_wait(kinds, x_src, x_out, x_sems)
                if n_g:
                    _gat_wait(g_src, g_out, g_sems)

    any_spec = pl.BlockSpec(memory_space=pl.ANY)
    g_shapes = [_sds((NDEV,) + tuple(a.shape), a.dtype) for a in gat]
    res = pl.pallas_call(
        wrapped, name=name, grid=grid, in_specs=list(in_specs) + [any_spec] * (n_x + n_g),
        out_specs=list(out_specs) + [any_spec] * (n_x + n_g), out_shape=list(out_shape) + _xch_shapes(xch) + g_shapes,
        scratch_shapes=list(scratch) + (_xch_scratch(n_x) if n_x else []) + (_xch_scratch(n_g) if n_g else []),
        compiler_params=_cp(len(grid)),
    )(*args, *[a for a, _, _ in xch], *gat)
    return res[:n_out], res[n_out:n_out + n_x], res[n_out + n_x:]


def embed(x, meta8, lp, gat):
    bl, seq, _ = x.shape
    c8 = D // NDEV
    cb = 2 * c8

    def body(x_ref, m_ref, h_ref):
        h_ref[0:NMETA, :] = jnp.concatenate([m_ref[0], m_ref[1]], axis=1)
        h_ref[NMETA:NMETA + seq, :] = x_ref[...]
        h_ref[NMETA + seq:, :] = jnp.zeros((lp - NMETA - seq, cb), F32)

    (h0,), _, got = _call(
        body, "embed", (bl, D // cb),
        [pl.BlockSpec((None, seq, cb), lambda b, c: (b, 0, c)), pl.BlockSpec((2, NMETA, c8), lambda b, c: (c, 0, 0))],
        [pl.BlockSpec((None, lp, cb), lambda b, c: (b, 0, c))], [_sds((bl, lp, D), F32)], (x, meta8), gat=gat)
    return h0, got


def conv_in_fwd(h, nm, l, w_in, b_in, i, tm, gat):
    t = h.shape[0]

    def body(h_ref, g_ref, w_ref, b_ref, u_ref, big_ref, a_ref):
        x = h_ref[...]
        ub = (x * _rstd(x) * g_ref[...]).astype(BF)
        u_ref[...] = ub
        big = jnp.dot(ub, w_ref[...], preferred_element_type=F32) + b_ref[...]
        big_ref[...] = big.astype(BF)
        a_ref[...] = big[:, :D] * _sig(big[:, D:])

    return _call(
        body, f"conv_in_fwd{i}", (t // tm,), [_row(tm, D), _lay(l, (1, D)), _res((D, 2 * D)), _lay(i, (1, 2 * D))],
        [_row(tm, D), _row(tm, 2 * D), _row(tm, D)], [_sds((t, D), BF), _sds((t, 2 * D), BF), _sds((t, D), F32)],
        (h, nm, w_in, b_in), gat=gat)


def _prev_halo(tm):
    return pl.BlockSpec((HALO, D), lambda i: (jnp.maximum(i * (tm // HALO) - 1, 0), 0))


def _next_halo(tm, t):
    return pl.BlockSpec((HALO, D), lambda i: (jnp.minimum((i + 1) * (tm // HALO), t // HALO - 1), 0))


def conv_mid_fwd(a, dw, ln_g, ln_b, i, tm, tpb, gat):
    t = a.shape[0]

    def body(a_ref, halo_ref, dw_ref, g_ref, b_ref, c_ref, s_ref, ext, taps):
        _spread_taps(dw_ref, taps)
        first = pl.program_id(0) % tpb == 0
        ext[0:HALO] = jnp.where(first, 0.0, halo_ref[...])
        ext[HALO:] = a_ref[...]

        def chunk(k, carry):
            r0 = pl.multiple_of(k * CHUNK, CHUNK)
            win = _shifted(ext[pl.ds(r0, 2 * CHUNK), :])
            c = jnp.zeros((CHUNK, D), F32)
            for j in range(CW):
                c = c + _tap_weight(taps, j) * _tap(win, j + 2)
            c_ref[pl.ds(r0, CHUNK), :] = c
            mu = jnp.mean(c, axis=-1, keepdims=True)
            xc = c - mu
            n = xc * lax.rsqrt(jnp.mean(xc * xc, axis=-1, keepdims=True) + EPS) * g_ref[...] + b_ref[...]
            s_ref[pl.ds(r0, CHUNK), :] = (n * _sig(n)).astype(BF)
            return carry

        lax.fori_loop(0, tm // CHUNK, chunk, 0, unroll=4)

    return _call(
        body, f"conv_mid_fwd{i}", (t // tm,),
        [_row(tm, D), _prev_halo(tm), _lay(i, (CW, D)), _lay(i, (1, D)), _lay(i, (1, D))],
        [_row(tm, D), _row(tm, D)], [_sds((t, D), F32), _sds((t, D), BF)], (a, a, dw, ln_g, ln_b),
        scratch=[pltpu.VMEM((tm + HALO, D), F32), pltpu.VMEM((CW, 8, D), F32)], gat=gat)


def mixer_ffn_fwd(h, s, w_out, lw, bias, nf, l, wg, wu, wd, tm, gat):
    t = h.shape[0]

    def body(*refs):
        if bias is None:
            h_ref, s_ref, w_ref, nf_ref, wg_ref, wu_ref, wd_ref, h1_ref, u_ref, g_ref, up_ref, hid_ref, h2_ref = refs
            y = 0.0
        else:
            h_ref, s_ref, w_ref, b_ref, nf_ref, wg_ref, wu_ref, wd_ref, h1_ref, u_ref, g_ref, up_ref, hid_ref, h2_ref = refs
            y = b_ref[...]
        h1 = h_ref[...] + (jnp.dot(s_ref[...], w_ref[...], preferred_element_type=F32) + y)
        h1_ref[...] = h1
        ub = (h1 * _rstd(h1) * nf_ref[...]).astype(BF)
        u_ref[...] = ub
        g = _dot_nt(ub, wg_ref[...])
        up = _dot_nt(ub, wu_ref[...])
        g_ref[...] = g.astype(BF)
        up_ref[...] = up.astype(BF)
        hid = (g * _sig(g) * up).astype(BF)
        hid_ref[...] = hid
        h2_ref[...] = h1 + jnp.dot(hid, wd_ref[...], preferred_element_type=F32)

    ins = [h, s, w_out] + ([] if bias is None else [bias]) + [nf, wg, wu, wd]
    specs = ([_row(tm, D), _row(tm, D), _res((D, D))] + ([] if bias is None else [_lay(lw, (1, D))])
             + [_lay(l, (1, D)), _res((DFF, D)), _res((DFF, D)), _res((DFF, D))])
    return _call(
        body, f"mixer_ffn_fwd{l}", (t // tm,), specs,
        [_row(tm, D), _row(tm, D), _row(tm, DFF), _row(tm, DFF), _row(tm, DFF), _row(tm, D)],
        [_sds((t, D), F32), _sds((t, D), BF), _sds((t, DFF), BF), _sds((t, DFF), BF), _sds((t, DFF), BF), _sds((t, D), F32)],
        ins, gat=gat)


def _seg_rms(x, g, nseg):
    outs = []
    for s in range(nseg):
        xs = x[:, HD * s:HD * s + HD]
        outs.append(xs * _rstd(xs) * g)
    return jnp.concatenate(outs, axis=1)


def kv_fwd(h, kvn, w_kv, kng, tm):
    t = h.shape[0]

    def body(h_ref, g_ref, w_ref, kg_ref, kn_ref, kv_ref, k_ref, v_ref):
        x = h_ref[...]
        kn = (x * _rstd(x) * g_ref[...]).astype(BF)
        kn_ref[...] = kn
        kv = jnp.dot(kn, w_ref[...], preferred_element_type=F32)
        kv_ref[...] = kv
        k_ref[...] = _seg_rms(kv[:, :KVD], kg_ref[...], NKV).astype(BF)
        v_ref[...] = kv[:, KVD:].astype(BF)

    return pl.pallas_call(
        body, name="kv_fwd", grid=(t // tm,),
        in_specs=[_row(tm, D), _res((1, D)), _res((D, 2 * KVD)), _res((1, HD))],
        out_specs=[_row(tm, D), _row(tm, 2 * KVD), _row(tm, KVD), _row(tm, KVD)],
        out_shape=[_sds((t, D), BF), _sds((t, 2 * KVD), F32), _sds((t, KVD), BF), _sds((t, KVD), BF)],
        compiler_params=_cp(1),
    )(h, kvn, w_kv, kng)


def q_fwd(h, nm, l, w_q, j, tm):
    t = h.shape[0]

    def body(h_ref, g_ref, w_ref, u_ref, q_ref):
        x = h_ref[...]
        ub = (x * _rstd(x) * g_ref[...]).astype(BF)
        u_ref[...] = ub
        q_ref[...] = jnp.dot(ub, w_ref[...], preferred_element_type=F32)

    return pl.pallas_call(
        body, name=f"q_fwd{j}", grid=(t // tm,),
        in_specs=[_row(tm, D), _lay(l, (1, D)), _res((D, D))],
        out_specs=[_row(tm, D), _row(tm, D)], out_shape=[_sds((t, D), BF), _sds((t, D), F32)],
        compiler_params=_cp(1),
    )(h, nm, w_q)


RQ = NH // NKV


NKEYS = 2 * QB + NMETA


def _attn_mask(n, start):
    shape = (RQ * QB, NKEYS)
    qpos = n * QB + (lax.broadcasted_iota(jnp.int32, shape, 0) & (QB - 1))
    col = lax.broadcasted_iota(jnp.int32, shape, 1)
    in_band = col < 2 * QB
    kpos = jnp.where(in_band, start + col, col - 2 * QB)
    return (kpos <= qpos) & ((col >= 2 * QB) | ((qpos - kpos < QB) & (kpos >= NMETA)))


def attn_bias():
    def body(o_ref):
        n = pl.program_id(0)
        o_ref[...] = jnp.where(_attn_mask(n, jnp.maximum(n - 1, 0) * QB), 0.0, NEG)

    return pl.pallas_call(
        body, name="attn_bias", grid=(3,), out_specs=pl.BlockSpec((None, RQ * QB, NKEYS), lambda n: (n, 0, 0)),
        out_shape=_sds((3, RQ * QB, NKEYS), F32), compiler_params=_cp(1))()


def _bias_spec():
    return pl.BlockSpec((None, RQ * QB, NKEYS), lambda n: (jnp.minimum(n, 2), 0, 0))


def _keys(ref, band, gs):
    return jnp.concatenate([ref[band, gs], ref[0:NMETA, gs]], axis=0)


def _keys_t(ref, band, gs):
    return jnp.concatenate([ref[gs, band], ref[gs, 0:NMETA]], axis=1)


def transpose_seq(a, name):
    bl, r, c = a.shape

    def body(a_ref, o_ref):
        o_ref[...] = a_ref[...].T

    return pl.pallas_call(
        body, name=name, grid=(bl,), in_specs=[pl.BlockSpec((None, r, c), lambda b: (b, 0, 0))],
        out_specs=pl.BlockSpec((None, c, r), lambda b: (b, 0, 0)), out_shape=_sds((bl, c, r), a.dtype), compiler_params=_cp(1),
    )(a)


def sum_transposed(a0, a1):
    bl, c, r = a0.shape

    def body(a0_ref, a1_ref, o_ref):
        o_ref[...] = (a0_ref[...] + a1_ref[...]).T

    spec = pl.BlockSpec((None, c, r), lambda b: (b, 0, 0))
    return pl.pallas_call(
        body, name="sum_transposed", grid=(bl,), in_specs=[spec, spec],
        out_specs=pl.BlockSpec((r, c), lambda b: (b, 0)), out_shape=_sds((bl * r, c), a0.dtype), compiler_params=_cp(1),
    )(a0, a1)


def _stack_heads(ref, g, fn):
    return jnp.concatenate([fn(ref[:, HD * (g * RQ + r):HD * (g * RQ + r) + HD]) for r in range(RQ)], axis=0)


def _stack_cols(ref, g):
    return jnp.concatenate([ref[:, g * RQ + r:g * RQ + r + 1] for r in range(RQ)], axis=0)


def _stack_sinks(sk_ref, g):
    return jnp.concatenate([jnp.broadcast_to(sk_ref[:, g * RQ + r:g * RQ + r + 1], (QB, 1)) for r in range(RQ)], axis=0)


def attn_fwd(q, kt, v, bias, qg, sinks, j, bl, lp, gat):
    t = q.shape[0]
    nb = lp // QB

    def body(q_ref, kt_ref, v_ref, bias_ref, qg_ref, sk_ref, o_ref, lse_ref):
        n = pl.program_id(0)
        start = pl.multiple_of(jnp.maximum(n - 1, 0) * QB, QB)
        band = pl.ds(start, 2 * QB)
        lane = lax.broadcasted_iota(jnp.int32, (QB, NH), 1)
        ones = jnp.ones((NKEYS, HD), BF)
        pairs = [(b, g) for b in range(bl) for g in range(NKV)]
        gsl = [slice(HD * g, HD * g + HD) for g in range(NKV)]
        qns = [_stack_heads(q_ref.at[b], g, lambda x: (x * _rstd(x) * (qg_ref[...] * SCALE)).astype(BF)) for b, g in pairs]
        ss = [_dot(qns[i], _keys_t(kt_ref.at[b], band, gsl[g])) + bias_ref[...] for i, (b, g) in enumerate(pairs)]
        sinks = [_stack_sinks(sk_ref, g) for g in range(NKV)]
        mxs = [jnp.maximum(jnp.max(ss[i], -1, keepdims=True), sinks[g]) for i, (b, g) in enumerate(pairs)]
        oas = [_dot(jnp.exp(ss[i] - mxs[i]), jnp.concatenate([_keys(v_ref.at[b], band, gsl[g]), ones], axis=1))
               for i, (b, g) in enumerate(pairs)]
        lses = [jnp.zeros((QB, NH), F32) for _ in range(bl)]
        for i, (b, g) in enumerate(pairs):
            den = oas[i][:, HD:HD + 1] + jnp.exp(sinks[g] - mxs[i])
            o = oas[i][:, :HD] * (1.0 / den)
            l = mxs[i] + jnp.log(den)
            for r in range(RQ):
                h = g * RQ + r
                o_ref[b, :, HD * h:HD * h + HD] = o[r * QB:(r + 1) * QB].astype(BF)
                lses[b] = jnp.where(lane == h, l[r * QB:(r + 1) * QB], lses[b])
        for b in range(bl):
            lse_ref[b] = lses[b]

    blk = lambda c: pl.BlockSpec((bl, QB, c), lambda n: (0, n, 0))
    (o, lse), _, got = _call(
        body, f"attn_fwd{j}", (nb,),
        [blk(D), pl.BlockSpec((bl, KVD, lp), lambda n: (0, 0, 0)), pl.BlockSpec((bl, lp, KVD), lambda n: (0, 0, 0)), _bias_spec(),
         pl.BlockSpec((None, 1, HD), lambda n: (j, 0, 0)), pl.BlockSpec((None, 1, NH), lambda n: (j, 0, 0))],
        [blk(D), blk(NH)], [_sds((bl, lp, D), BF), _sds((bl, lp, NH), F32)], (q.reshape(bl, lp, D), kt, v, bias, qg, sinks),
        gat=gat)
    return (o.reshape(t, D), lse.reshape(t, NH)), (), got


def loss_fwd(h, tgt):
    bl, lp, _ = h.shape
    seq = tgt.shape[1]
    cb = 256

    def body(h_ref, t_ref, dh_ref, loss_ref):
        _init(loss_ref, (pl.program_id(0) == 0) & (pl.program_id(1) == 0))
        err = h_ref[NMETA:NMETA + seq, :] - t_ref[...]
        dh_ref[...] = jnp.zeros_like(dh_ref)
        dh_ref[NMETA:NMETA + seq, :] = err * (1.0 / D)
        loss_ref[...] += (0.5 / D) * jnp.sum(err * err)

    return pl.pallas_call(
        body, name="loss_fwd", grid=(bl, D // cb),
        in_specs=[pl.BlockSpec((None, lp, cb), lambda b, c: (b, 0, c)), pl.BlockSpec((None, seq, cb), lambda b, c: (b, 0, c))],
        out_specs=[pl.BlockSpec((None, lp, cb), lambda b, c: (b, 0, c)), pl.BlockSpec((8, 128), lambda b, c: (0, 0))],
        out_shape=[_sds((bl, lp, D), F32), _sds((8, 128), F32)],
        compiler_params=_cp(2),
    )(h, tgt)


def ffn_bwd_x(dh2, g, up, h1, nf, l, wd, wg, wu, w_o, tm, xch):
    t = dh2.shape[0]

    def body(dh2_ref, g_ref, up_ref, h1_ref, nf_ref, wd_ref, wg_ref, wu_ref, *rest):
        if w_o is None:
            dg_ref, du_ref, dh1_ref, dnf_ref = rest
        else:
            wo_ref, dg_ref, du_ref, dh1_ref, dnf_ref, do_ref = rest
        _init(dnf_ref, pl.program_id(0) == 0)
        dh2v = dh2_ref[...]
        dhid = _dot_nt(dh2v, wd_ref[...])
        gv = g_ref[...].astype(F32)
        uv = up_ref[...].astype(F32)
        sg = _sig(gv)
        dgv = (dhid * uv * (sg * (1.0 + gv * (1.0 - sg)))).astype(BF)
        duv = (dhid * (gv * sg)).astype(BF)
        dg_ref[...] = dgv
        du_ref[...] = duv
        dnorm = _dot(dgv, wg_ref[...]) + _dot(duv, wu_ref[...])
        dx, dnf = _rms_bwd(h1_ref[...], nf_ref[...], dnorm)
        dh1 = dh2v + dx
        dh1_ref[...] = dh1
        dnf_ref[...] += dnf
        if w_o is not None:
            do_ref[...] = _dot_nt(dh1, wo_ref[...]).astype(BF)

    attn = w_o is not None
    return _call(
        body, f"ffn_bwd_x{l}", (t // tm,),
        [_row(tm, D), _row(tm, DFF), _row(tm, DFF), _row(tm, D), _lay(l, (1, D)),
         _res((DFF, D)), _res((DFF, D)), _res((DFF, D))] + ([_res((D, D))] if attn else []),
        [_row(tm, DFF), _row(tm, DFF), _row(tm, D), _acc((1, D))] + ([_row(tm, D)] if attn else []),
        [_sds((t, DFF), BF), _sds((t, DFF), BF), _sds((t, D), F32), _sds((1, D), F32)] + ([_sds((t, D), BF)] if attn else []),
        (dh2, g, up, h1, nf, wd, wg, wu) + ((w_o,) if attn else ()), xch=xch)


def mm_tn(x, dy, tm, name, split=False, transposed=False, xch=()):
    t, kk = x.shape
    nn = dy.shape[1]
    n8 = nn // NDEV
    nsteps = t // tm

    def body(x_ref, dy_ref, o_ref, acc):
        i = pl.program_id(0)
        _init(acc, i == 0)
        acc[...] += _dot_tn(x_ref[...], dy_ref[...])

        @pl.when(i == nsteps - 1)
        def _():
            if split:
                for p in range(NDEV):
                    o_ref[p] = acc[:, p * n8:(p + 1) * n8].astype(BF)
            elif transposed:
                o_ref[...] = acc[...].T.astype(BF)
            else:
                o_ref[...] = acc[...].astype(BF)

    oshape = (NDEV, kk, n8) if split else ((nn, kk) if transposed else (kk, nn))
    (out,), got, _ = _call(body, name, (nsteps,), [_row(tm, kk), _row(tm, nn)], [_acc(oshape)], [_sds(oshape, BF)], (x, dy),
                           scratch=[pltpu.VMEM((kk, nn), F32)], xch=xch)
    return out, got


def proj_bwd(dy, w, h, g, lg, dh_in, tm, name, xch=()):
    t = h.shape[0]
    nn = dy.shape[1]
    wspec = _res(w.shape)
    gspec = _res((1, D)) if lg is None else _lay(lg, (1, D))

    def body(dy_ref, w_ref, h_ref, g_ref, dhin_ref, dh_ref, dg_ref):
        _init(dg_ref, pl.program_id(0) == 0)
        du = _dot_nt(dy_ref[...], w_ref[...])
        dx, dg = _rms_bwd(h_ref[...], g_ref[...], du)
        dh_ref[...] = dhin_ref[...] + dx
        dg_ref[...] += dg

    return _call(body, name, (t // tm,), [_row(tm, nn), wspec, _row(tm, D), gspec, _row(tm, D)],
                 [_row(tm, D), _acc((1, D))], [_sds((t, D), F32), _sds((1, D), F32)], (dy, w, h, g, dh_in), xch=xch)


def attn_bwd(q, k, kt, vt, bias, do, o, lse, qg, sinks, j, bl, lp, xch):
    t = q.shape[0]
    nb = lp // QB

    def body(q_ref, k_ref, kt_ref, vt_ref, bias_ref, do_ref, o_ref, lse_ref, qg_ref, sk_ref, dq_ref, dk_ref, dv_ref, dqg_ref,
             dsk_ref):
        n = pl.program_id(0)
        for ref in (dk_ref, dv_ref, dqg_ref, dsk_ref):
            _init(ref, n == 0)
        start = pl.multiple_of(jnp.maximum(n - 1, 0) * QB, QB)
        band = pl.ds(start, 2 * QB)
        lane = lax.broadcasted_iota(jnp.int32, (1, NH), 1)
        dqg = jnp.zeros((1, HD), F32)
        dsk = jnp.zeros((1, NH), F32)
        pairs = [(b, g) for b in range(bl) for g in range(NKV)]
        idx = range(len(pairs))
        gsl = [slice(HD * g, HD * g + HD) for g in range(NKV)]
        qhs = [_stack_heads(q_ref.at[b], g, lambda x: x) for b, g in pairs]
        rss = [_rstd(qhs[i]) for i in idx]
        qns = [(qhs[i] * rss[i] * (qg_ref[...] * SCALE)).astype(BF) for i in idx]
        lss = [_stack_cols(lse_ref.at[b], g) for b, g in pairs]
        dohs = [_stack_heads(do_ref.at[b], g, lambda x: x) for b, g in pairs]
        deltas = [jnp.sum(dohs[i].astype(F32) * _stack_heads(o_ref.at[b], g, lambda x: x).astype(F32), axis=-1, keepdims=True)
                  for i, (b, g) in enumerate(pairs)]
        prs = [jnp.exp(_dot(qns[i], _keys_t(kt_ref.at[b], band, gsl[g])) + bias_ref[...] - lss[i]) for i, (b, g) in enumerate(pairs)]
        dss = [(prs[i] * (_dot(dohs[i], _keys_t(vt_ref.at[b], band, gsl[g])) - deltas[i])).astype(BF)
               for i, (b, g) in enumerate(pairs)]
        for i, (b, g) in enumerate(pairs):
            gs = gsl[g]
            dkt = _dot_tn(qns[i], dss[i])
            dvt = _dot_tn(dohs[i], prs[i])
            dk_ref[b, gs, band] += dkt[:, :2 * QB]
            dv_ref[b, gs, band] += dvt[:, :2 * QB]
            dk_ref[b, gs, 0:NMETA] += dkt[:, 2 * QB:]
            dv_ref[b, gs, 0:NMETA] += dvt[:, 2 * QB:]
        dqns = [_dot(dss[i], _keys(k_ref.at[b], band, gsl[g])) * SCALE for i, (b, g) in enumerate(pairs)]
        for i, (b, g) in enumerate(pairs):
            qh, rs, dqn = qhs[i], rss[i], dqns[i]
            dsink = jnp.exp(_stack_sinks(sk_ref, g) - lss[i]) * deltas[i]
            z = dqn * qg_ref[...]
            dq = rs * z - qh * (rs * rs * rs * jnp.mean(z * qh, axis=-1, keepdims=True))
            dqg = dqg + jnp.sum(dqn * qh * rs, axis=0, keepdims=True)
            for r in range(RQ):
                h = g * RQ + r
                dq_ref[b, :, HD * h:HD * h + HD] = dq[r * QB:(r + 1) * QB]
                dsk = dsk + jnp.where(lane == h, -jnp.sum(dsink[r * QB:(r + 1) * QB]), 0.0)
        dqg_ref[...] += dqg
        dsk_ref[...] += dsk

    blk = lambda c: pl.BlockSpec((bl, QB, c), lambda n: (0, n, 0))
    seq = pl.BlockSpec((bl, lp, KVD), lambda n: (0, 0, 0))
    seq_t = pl.BlockSpec((bl, KVD, lp), lambda n: (0, 0, 0))
    as3 = lambda a: a.reshape(bl, lp, a.shape[-1])
    (dq, dk, dv, dqg, dsk), got, _ = _call(
        body, f"attn_bwd{j}", (nb,),
        [blk(D), seq, seq_t, seq_t, _bias_spec(), blk(D), blk(D), blk(NH),
         pl.BlockSpec((None, 1, HD), lambda n: (j, 0, 0)), pl.BlockSpec((None, 1, NH), lambda n: (j, 0, 0))],
        [blk(D), seq_t, seq_t, pl.BlockSpec((1, HD), lambda n: (0, 0)), pl.BlockSpec((1, NH), lambda n: (0, 0))],
        [_sds((bl, lp, D), F32), _sds((bl, KVD, lp), F32), _sds((bl, KVD, lp), F32), _sds((1, HD), F32), _sds((1, NH), F32)],
        (as3(q), k, kt, vt, bias, as3(do), as3(o), as3(lse), qg, sinks), xch=xch)
    return (dq.reshape(t, D), dk, dv, dqg, dsk), got, ()


def kv_bwd_pre(dk, dv, kv, kng, tm):
    t = kv.shape[0]

    def body(dk_ref, dv_ref, kv_ref, g_ref, dkv_ref, dg_ref):
        _init(dg_ref, pl.program_id(0) == 0)
        dg = jnp.zeros((1, HD), F32)
        outs = []
        for s in range(NKV):
            sl = slice(HD * s, HD * s + HD)
            dx, dgs = _rms_bwd(kv_ref[:, sl], g_ref[...], dk_ref[:, sl])
            outs.append(dx)
            dg = dg + dgs
        dkv_ref[:, :KVD] = jnp.concatenate(outs, axis=1).astype(BF)
        dkv_ref[:, KVD:] = dv_ref[...].astype(BF)
        dg_ref[...] += dg

    return pl.pallas_call(
        body, name="kv_bwd_pre", grid=(t // tm,),
        in_specs=[_row(tm, KVD)] * 2 + [_row(tm, 2 * KVD), _res((1, HD))],
        out_specs=[_row(tm, 2 * KVD), _acc((1, HD))], out_shape=[_sds((t, 2 * KVD), BF), _sds((1, HD), F32)],
        compiler_params=_cp(1),
    )(dk, dv, kv, kng)


def conv_out_bwd(dh1, c, ln_g, ln_b, w_out, i, tm, xch):
    t = dh1.shape[0]

    def body(dh1_ref, c_ref, g_ref, b_ref, w_ref, dc_ref, dg_ref, db_ref, dbo_ref):
        first = pl.program_id(0) == 0
        _init(dg_ref, first)
        _init(db_ref, first)
        _init(dbo_ref, first)
        dh1v = dh1_ref[...]
        ds = _dot_nt(dh1v, w_ref[...])
        cv = c_ref[...]
        xc = cv - jnp.mean(cv, axis=-1, keepdims=True)
        rstd = lax.rsqrt(jnp.mean(xc * xc, axis=-1, keepdims=True) + EPS)
        xh = xc * rstd
        n = xh * g_ref[...] + b_ref[...]
        sg = _sig(n)
        dn = ds * (sg * (1.0 + n * (1.0 - sg)))
        dxh = dn * g_ref[...]
        dc_ref[...] = rstd * (dxh - jnp.mean(dxh, axis=-1, keepdims=True) - xh * jnp.mean(dxh * xh, axis=-1, keepdims=True))
        dg_ref[...] += jnp.sum(dn * xh, axis=0, keepdims=True)
        db_ref[...] += jnp.sum(dn, axis=0, keepdims=True)
        dbo_ref[...] += jnp.sum(dh1v, axis=0, keepdims=True)

    return _call(
        body, f"conv_out_bwd{i}", (t // tm,), [_row(tm, D), _row(tm, D), _lay(i, (1, D)), _lay(i, (1, D)), _res((D, D))],
        [_row(tm, D), _acc((1, D)), _acc((1, D)), _acc((1, D))], [_sds((t, D), F32)] + [_sds((1, D), F32)] * 3,
        (dh1, c, ln_g, ln_b, w_out), xch=xch)


def conv_mid_bwd(dc, a, big, dw, i, tm, tpb, xch):
    t = dc.shape[0]
    nsteps = t // tm

    def body(dc_ref, nxt_ref, a_ref, prv_ref, big_ref, dw_ref, da_ref, dbin_ref, ddw_ref, dce, ae, wacc, bacc, taps):
        i_ = pl.program_id(0)
        _spread_taps(dw_ref, taps)
        _init(wacc, i_ == 0)
        _init(bacc, i_ == 0)
        dce[0:tm] = dc_ref[...]
        dce[tm:] = jnp.where(i_ % tpb == tpb - 1, 0.0, nxt_ref[...])
        ae[0:HALO] = jnp.where(i_ % tpb == 0, 0.0, prv_ref[...])
        ae[HALO:] = a_ref[...]

        def chunk(k, carry):
            r0 = pl.multiple_of(k * CHUNK, CHUNK)
            wdc = _shifted(dce[pl.ds(r0, 2 * CHUNK), :])
            wa = _shifted(ae[pl.ds(r0, 2 * CHUNK), :])
            dcc = wdc[0][0:CHUNK]
            da = jnp.zeros((CHUNK, D), F32)
            for j in range(CW):
                da = da + _tap_weight(taps, j) * _tap(wdc, CW - 1 - j)
                wacc[j] += _fold8(dcc * _tap(wa, j + 2))
            bv = big_ref[pl.ds(r0, CHUNK), :].astype(F32)
            a1, sg = bv[:, :D], _sig(bv[:, D:])
            d1 = da * sg
            d2 = da * a1 * sg * (1.0 - sg)
            da_ref[pl.ds(r0, CHUNK), 0:D] = d1.astype(BF)
            da_ref[pl.ds(r0, CHUNK), D:2 * D] = d2.astype(BF)
            bacc[:, 0:D] += _fold8(d1)
            bacc[:, D:2 * D] += _fold8(d2)
            return carry

        lax.fori_loop(0, tm // CHUNK, chunk, 0)

        @pl.when(i_ == nsteps - 1)
        def _():
            dbin_ref[...] = jnp.sum(bacc[...], axis=0, keepdims=True)
            ddw_ref[...] = jnp.sum(wacc[...], axis=1)

    return _call(
        body, f"conv_mid_bwd{i}", (nsteps,),
        [_row(tm, D), _next_halo(tm, t), _row(tm, D), _prev_halo(tm), _row(tm, 2 * D), _lay(i, (CW, D))],
        [_row(tm, 2 * D), _acc((1, 2 * D)), _acc((CW + 1, D))],
        [_sds((t, 2 * D), BF), _sds((1, 2 * D), F32), _sds((CW + 1, D), F32)],
        (dc, dc, a, a, big, dw),
        scratch=[pltpu.VMEM((tm + HALO, D), F32), pltpu.VMEM((tm + HALO, D), F32),
                 pltpu.VMEM((CW + 1, 8, D), F32), pltpu.VMEM((8, 2 * D), F32), pltpu.VMEM((CW, 8, D), F32)], xch=xch)


def input_grads(dh0, seq):
    bl, lp, _ = dh0.shape
    cb = 256

    def body(dh_ref, gx_ref, gm_ref):
        _init(gm_ref, pl.program_id(1) == 0)
        gx_ref[...] = dh_ref[NMETA:NMETA + seq, :]
        gm_ref[...] += dh_ref[0:NMETA, :]

    return pl.pallas_call(
        body, name="input_grads", grid=(D // cb, bl),
        in_specs=[pl.BlockSpec((None, lp, cb), lambda c, b: (b, 0, c))],
        out_specs=[pl.BlockSpec((None, seq, cb), lambda c, b: (b, 0, c)), pl.BlockSpec((NMETA, cb), lambda c, b: (0, c))],
        out_shape=[_sds((bl, seq, D), F32), _sds((NMETA, D), F32)],
        compiler_params=_cp(2),
    )(dh0)


GATHER_PLAN = {
    "embed": [("conv_w_in", 0)],
    "conv_in_fwd0": [("ffn_w_down", 0)],
    "conv_mid_fwd0": [("ffn_w_gate", 0), ("ffn_w_up", 0), ("conv_w_out", 0)],
    "mixer_ffn_fwd0": [("conv_w_in", 1), ("conv_w_out", 1), ("ffn_w_gate", 1)],
    "conv_in_fwd1": [("w_kv", 0), ("w_q", 0)],
    "conv_mid_fwd1": [("ffn_w_up", 1), ("ffn_w_down", 1)],
    "mixer_ffn_fwd1": [("w_o", 0), ("ffn_w_down", 2)],
    "attn_fwd0": [("ffn_w_gate", 2), ("ffn_w_up", 2), ("w_q", 1), ("w_o", 1)],
    "attn_fwd1": [("ffn_w_gate", 3), ("ffn_w_up", 3), ("ffn_w_down", 3)],
}
EXCHANGE_PLAN = {
    "attn_bwd1": [("ffn_w_down3", ALL), ("ffn_w_gate3", ALL)],
    "dw_down2": [("w_o1", ALL)],
    "ffn_bwd_x2": [("ffn_w_up3", ALL), ("w_q1", ALL)],
    "attn_bwd0": [("ffn_w_down2", ALL), ("ffn_w_gate2", ALL)],
    "dw_down1": [("w_o0", ALL), ("w_q0", H1)],
    "ffn_bwd_x1": [("ffn_w_up2", ALL), ("w_q0", H2), ("w_kv", ALL)],
    "dw_gate1": [("ffn_w_down1", H1)],
    "dw_up1": [("ffn_w_down1", H2)],
    "conv_mid_bwd1": [("ffn_w_gate1", ALL), ("ffn_w_up1", H1)],
    "conv_in_bwd1": [("ffn_w_up1", H2)],
    "dw_down0": [("conv_w_out1", ALL)],
    "ffn_bwd_x0": [("conv_w_in1", ALL)],
    "dw_gate0": [("ffn_w_down0", H1)],
    "dw_up0": [("ffn_w_down0", H2)],
    "conv_out_bwd0": [("ffn_w_gate0", H1)],
    "conv_mid_bwd0": [("ffn_w_gate0", H2), ("ffn_w_up0", H1), ("conv_w_out0", ALL)],
    "dw_conv_in0": [("ffn_w_up0", H2)],
    "conv_in_bwd0": [("conv_w_in0", H1)],
    "tail": [("conv_w_in0", H2)],
}
BIG = {"conv_w_in": "pieces", "conv_w_out": "rows", "w_kv": "rows", "w_q": "rows", "w_o": "rows",
       "ffn_w_gate": "rows", "ffn_w_up": "rows", "ffn_w_down": "rows"}
EXCHANGE_KIND = BIG
TRANSPOSED = ("ffn_w_gate", "ffn_w_up")


def gathered_matrix(name, layer, blocks8):
    if BIG[name] == "rows":
        return blocks8.reshape(NDEV * blocks8.shape[1], blocks8.shape[2])
    return join_columns(blocks8, f"join_{name}{layer}")


def local_step(x, tgt, meta8, w, shards):
    bl, seq, _ = x.shape
    lp = -(-(NMETA + seq) // QB) * QB
    tpb = 4
    tm = lp // tpb
    t = bl * lp
    na = 2
    flat = lambda a: a.reshape(t, D)
    mats = {}

    def riders(carrier):
        return [shards[key] for key in GATHER_PLAN[carrier]]

    def landed(carrier, blocks):
        for key, b8 in zip(GATHER_PLAN[carrier], blocks):
            mats[key] = gathered_matrix(*key, b8)

    h0, got = embed(x, meta8, lp, riders("embed"))
    landed("embed", got)
    h = flat(h0)
    saved = []
    kvs = None
    for l in range(4):
        rec = {"h": h}
        if l < na:
            name = f"conv_in_fwd{l}"
            (rec["u"], rec["big"], rec["a"]), _, got = conv_in_fwd(h, w["norm_mix"], l, mats["conv_w_in", l], w["conv_b_in"], l, tm,
                                                                  riders(name))
            landed(name, got)
            name = f"conv_mid_fwd{l}"
            (rec["c"], rec["s"]), _, got = conv_mid_fwd(rec["a"], w["conv_dw"], w["conv_ln_g"], w["conv_ln_b"], l, tm, tpb,
                                                         riders(name))
            landed(name, got)
            mixed, w_out, lw, bias = rec["s"], mats["conv_w_out", l], l, w["conv_b_out"]
        else:
            j = l - na
            if kvs is None:
                kvs = dict(zip(("kn", "kv", "k", "v"), kv_fwd(h, w["kv_norm"], mats["w_kv", 0], w["k_norm"], tm)))
                kvs["h"] = h
                kvs["k3"], kvs["v3"] = kvs["k"].reshape(bl, lp, KVD), kvs["v"].reshape(bl, lp, KVD)
                kvs["kt"], kvs["vt"] = transpose_seq(kvs["k3"], "transpose_k"), transpose_seq(kvs["v3"], "transpose_v")
                kvs["bias"] = attn_bias()
            rec["u"], rec["q"] = q_fwd(h, w["norm_mix"], l, mats["w_q", j], j, tm)
            name = f"attn_fwd{j}"
            (rec["o"], rec["lse"]), _, got = attn_fwd(rec["q"], kvs["kt"], kvs["v3"], kvs["bias"], w["q_norm"], w["attn_sinks"], j, bl, lp,
                                                      riders(name) if name in GATHER_PLAN else [])
            if name in GATHER_PLAN:
                landed(name, got)
            mixed, w_out, lw, bias = rec["o"], mats["w_o", j], j, None
        name = f"mixer_ffn_fwd{l}"
        (rec["h1"], rec["u2"], rec["g"], rec["up"], rec["hid"], h), _, got = mixer_ffn_fwd(
            h, mixed, w_out, lw, bias, w["norm_ffn"], l, mats["ffn_w_gate", l], mats["ffn_w_up", l], mats["ffn_w_down", l], tm // 2,
            riders(name) if name in GATHER_PLAN else [])
        if name in GATHER_PLAN:
            landed(name, got)
        saved.append(rec)

    dh3, loss_blk = loss_fwd(h.reshape(bl, lp, D), tgt)
    dh = flat(dh3)

    big, small, arrived = {}, {}, {}
    dks, dvs = [], []

    def ride(kernel_name):
        return [(big[nm], EXCHANGE_KIND[nm.rstrip("0123456789")], ks) for nm, ks in EXCHANGE_PLAN.get(kernel_name, [])]

    def landed_x(kernel_name, arrivals):
        for (nm, _), got in zip(EXCHANGE_PLAN.get(kernel_name, []), arrivals):
            arrived.setdefault(nm, []).append(got)

    def dw(name, grad, x, dy, **kw):
        big[grad], got = mm_tn(x, dy, 2 * tm, name, xch=ride(name), **kw)
        landed_x(name, got)

    for l in reversed(range(4)):
        rec = saved[l]
        dw(f"dw_down{l}", f"ffn_w_down{l}", rec["hid"], dh)
        name = f"ffn_bwd_x{l}"
        outs, got, _ = ffn_bwd_x(
            dh, rec["g"], rec["up"], rec["h1"], w["norm_ffn"], l, mats["ffn_w_down", l], mats["ffn_w_gate", l], mats["ffn_w_up", l],
            mats["w_o", l - na] if l >= na else None, tm // 2, ride(name))
        landed_x(name, got)
        dg, du, dh1, small[f"norm_ffn{l}"] = outs[:4]
        dw(f"dw_gate{l}", f"ffn_w_gate{l}", rec["u2"], dg, transposed=True)
        dw(f"dw_up{l}", f"ffn_w_up{l}", rec["u2"], du, transposed=True)
        if l >= na:
            j = l - na
            dw(f"dw_o{j}", f"w_o{j}", rec["o"], dh1)
            name = f"attn_bwd{j}"
            (dq, dk, dv, small[f"q_norm{j}"], small[f"attn_sinks{j}"]), got, _ = attn_bwd(
                rec["q"], kvs["k3"], kvs["kt"], kvs["vt"], kvs["bias"], outs[4], rec["o"], rec["lse"], w["q_norm"], w["attn_sinks"],
                j, bl, lp, ride(name))
            landed_x(name, got)
            dks.append(dk)
            dvs.append(dv)
            dw(f"dw_q{j}", f"w_q{j}", rec["u"], dq)
            dh, small[f"norm_mix{l}"] = proj_bwd(dq, mats["w_q", j], rec["h"], w["norm_mix"], l, dh1, tm, f"q_bwd{j}")[0]
            if l == na:
                dkv, small["k_norm"] = kv_bwd_pre(sum_transposed(*dks), sum_transposed(*dvs), kvs["kv"], w["k_norm"], tm)
                dw("dw_kv", "w_kv", kvs["kn"], dkv)
                dh, small["kv_norm"] = proj_bwd(dkv, mats["w_kv", 0], kvs["h"], w["kv_norm"], None, dh, tm, "kv_bwd")[0]
        else:
            name = f"conv_out_bwd{l}"
            (dc, small[f"conv_ln_g{l}"], small[f"conv_ln_b{l}"], small[f"conv_b_out{l}"]), got, _ = conv_out_bwd(
                dh1, rec["c"], w["conv_ln_g"], w["conv_ln_b"], mats["conv_w_out", l], l, tm, ride(name))
            landed_x(name, got)
            dw(f"dw_conv_out{l}", f"conv_w_out{l}", rec["s"], dh1)
            name = f"conv_mid_bwd{l}"
            (da, small[f"conv_b_in{l}"], small[f"conv_dw{l}"]), got, _ = conv_mid_bwd(
                dc, rec["a"], rec["big"], w["conv_dw"], l, tm, tpb, ride(name))
            landed_x(name, got)
            dw(f"dw_conv_in{l}", f"conv_w_in{l}", rec["u"], da, split=True)
            name = f"conv_in_bwd{l}"
            (dh, small[f"norm_mix{l}"]), got, _ = proj_bwd(da, mats["conv_w_in", l], rec["h"], w["norm_mix"], l, dh1, tm, name,
                                                           ride(name))
            landed_x(name, got)
    grad_x, small["meta_tokens"] = input_grads(dh.reshape(bl, lp, D), seq)
    return loss_blk, grad_x, big, arrived, small


def all_gather_blocks(blocks):
    n = len(blocks)

    def body(*refs):
        srcs, outs, sems = refs[:n], refs[n:2 * n], refs[2 * n:]
        _gat_start(srcs, outs, sems)
        _gat_forward(srcs, outs, sems)
        _gat_wait(srcs, outs, sems)

    any_spec = pl.BlockSpec(memory_space=pl.ANY)
    return pl.pallas_call(
        body, name="all_gather_blocks", out_shape=[_sds((NDEV,) + tuple(a.shape), a.dtype) for a in blocks],
        in_specs=[any_spec] * n, out_specs=[any_spec] * n, scratch_shapes=_xch_scratch(n),
    )(*blocks)


def cast_bf16(ws):
    n = len(ws)
    counts = [1 if x.ndim == 2 else x.shape[0] for x in ws]

    def body(*refs):
        outs = iter(refs[n:])
        for a in range(n):
            for l in range(counts[a]):
                next(outs)[...] = (refs[a][...] if ws[a].ndim == 2 else refs[a][l]).astype(BF)

    flat = pl.pallas_call(
        body, name="cast_bf16", out_shape=[_sds(x.shape[-2:], BF) for x, k in zip(ws, counts) for _ in range(k)],
        compiler_params=pltpu.CompilerParams(vmem_limit_bytes=VMEM_LIMIT),
    )(*ws)
    it = iter(flat)
    return [[next(it) for _ in range(k)] for k in counts]


def join_columns(w8, name):
    _, kk, n8 = w8.shape

    def body(x_ref, o_ref):
        o_ref[...] = jnp.concatenate([x_ref[p] for p in range(NDEV)], axis=1)

    return pl.pallas_call(body, name=name, out_shape=_sds((kk, NDEV * n8), w8.dtype),
                          compiler_params=pltpu.CompilerParams(vmem_limit_bytes=VMEM_LIMIT))(w8)


def _adamw_math(w, m, v, g):
    m2 = B1 * m + (1.0 - B1) * g
    v2 = B2 * v + (1.0 - B2) * (g * g)
    mh = m2 / (1.0 - B1 ** STEP)
    vh = v2 / (1.0 - B2 ** STEP)
    return -LR * (mh / (jnp.sqrt(vh) + AEPS) + WD * w), m2, v2


def adamw_big(w, m, v, parts, name, xch=(), gat=()):
    lyr, r, c = w.shape
    by_cols = c >= 512
    blk = (lyr, r, 256) if by_cols else (lyr, 256 if r % 256 == 0 else r, c)
    imap = (lambda i: (0, 0, i)) if by_cols else (lambda i: (0, i, 0))
    counts = [len(per_layer) for per_layer in parts]

    def body(w_ref, m_ref, v_ref, *rest):
        p_refs, (g_ref, d_ref, m2_ref, v2_ref) = iter(rest[:sum(counts)]), rest[sum(counts):]
        for l in range(lyr):
            g = None
            for _ in range(counts[l]):
                ref = next(p_refs)
                for q in range(ref.shape[0]):
                    g = ref[q].astype(F32) if g is None else g + ref[q].astype(F32)
            g_ref[l] = g
            d_ref[l], m2_ref[l], v2_ref[l] = _adamw_math(w_ref[l], m_ref[l], v_ref[l], g)

    spec = pl.BlockSpec(blk, imap)
    flat = [a for per_layer in parts for a in per_layer]
    pspecs = [pl.BlockSpec((a.shape[0],) + blk[1:], imap) for a in flat]
    return _call(body, name, ((c // 256) if by_cols else (r // blk[1]),), [spec, spec, spec] + pspecs,
                 [spec] * 4, [_sds((lyr, r, c), F32)] * 4, (w, m, v, *flat), xch=xch, gat=gat)


SMALL_ROWS = 104
REPLICATED = {"norm_mix": (0, 4, D), "norm_ffn": (4, 4, D), "kv_norm": (8, 1, D), "k_norm": (9, 1, HD), "q_norm": (10, 2, HD),
              "attn_sinks": (12, 2, NH)}
LOSS_ROW = 14
SHARDED = {"meta_tokens": (16, NMETA), "conv_b_in": (32, 4), "conv_dw": (36, 2 * CW), "conv_ln_g": (98, 2), "conv_ln_b": (100, 2),
           "conv_b_out": (102, 2)}


def pack_small(gs, loss_blk):
    order = ([f"norm_mix{l}" for l in range(4)] + [f"norm_ffn{l}" for l in range(4)] + ["kv_norm", "k_norm", "q_norm0", "q_norm1",
             "attn_sinks0", "attn_sinks1", "meta_tokens", "conv_b_in0", "conv_b_in1", "conv_dw0", "conv_dw1", "conv_ln_g0",
             "conv_ln_g1", "conv_ln_b0", "conv_ln_b1", "conv_b_out0", "conv_b_out1"])

    def body(*refs):
        r = dict(zip(order, refs))
        loss_ref, o_ref = refs[len(order)], refs[len(order) + 1]
        o_ref[...] = jnp.zeros_like(o_ref)
        for l in range(4):
            o_ref[l:l + 1, :] = r[f"norm_mix{l}"][...]
            o_ref[4 + l:5 + l, :] = r[f"norm_ffn{l}"][...]
        o_ref[8:9, :] = r["kv_norm"][...]
        o_ref[9:10, 0:HD] = r["k_norm"][...]
        for j in range(2):
            o_ref[10 + j:11 + j, 0:HD] = r[f"q_norm{j}"][...]
            o_ref[12 + j:13 + j, 0:NH] = r[f"attn_sinks{j}"][...]
            o_ref[32 + 2 * j:33 + 2 * j, :] = r[f"conv_b_in{j}"][:, 0:D]
            o_ref[33 + 2 * j:34 + 2 * j, :] = r[f"conv_b_in{j}"][:, D:2 * D]
            o_ref[36 + CW * j:36 + CW * (j + 1), :] = r[f"conv_dw{j}"][0:CW, :]
            o_ref[98 + j:99 + j, :] = r[f"conv_ln_g{j}"][...]
            o_ref[100 + j:101 + j, :] = r[f"conv_ln_b{j}"][...]
            o_ref[102 + j:103 + j, :] = r[f"conv_b_out{j}"][...]
        o_ref[LOSS_ROW:LOSS_ROW + 1, 0:1] = loss_ref[0:1, 0:1]
        o_ref[16:16 + NMETA, :] = r["meta_tokens"][...]

    return pl.pallas_call(body, name="pack_small", out_shape=_sds((SMALL_ROWS, D), F32))(*[gs[k] for k in order], loss_blk)


def adamw_small(g8, wts, mom, var):
    names = list(REPLICATED) + list(SHARDED)
    shape2 = {"kv_norm": (1, D), "k_norm": (1, HD)}
    ins = [a[k].reshape(shape2.get(k, a[k].shape)) for a in (wts, mom, var) for k in names]
    n = len(names)

    def body(*refs):
        g8_ref, w_refs, m_refs, v_refs = refs[0], refs[1:1 + n], refs[1 + n:1 + 2 * n], refs[1 + 2 * n:1 + 3 * n]
        loss_ref, outs, red_ref = refs[1 + 3 * n], refs[2 + 3 * n:-1], refs[-1]
        me = _my_index()
        acc = g8_ref[0]
        for q in range(1, NDEV):
            acc = acc + g8_ref[q]
        red_ref[...] = acc
        loss_ref[...] = red_ref[LOSS_ROW:LOSS_ROW + 1, 0:1]

        def mine(rows, width):
            acc = jnp.zeros((rows.stop - rows.start, width), F32)
            for p_ in range(NDEV):
                acc = acc + jnp.where(me == p_, red_ref[rows, p_ * width:(p_ + 1) * width], 0.0)
            return acc

        for i, k in enumerate(names):
            if k in REPLICATED:
                r0, nr, width = REPLICATED[k]
                g = red_ref[r0:r0 + nr, 0:width]
            elif k == "conv_b_in":
                half = D // (2 * D // NDEV)
                acc = jnp.zeros((2, 2 * D // NDEV), F32)
                for p_ in range(NDEV):
                    c0 = (p_ % half) * (2 * D // NDEV)
                    part = jnp.concatenate([red_ref[32 + 2 * j + p_ // half:33 + 2 * j + p_ // half, c0:c0 + 2 * D // NDEV]
                                            for j in range(2)], axis=0)
                    acc = acc + jnp.where(me == p_, part, 0.0)
                g = acc
            else:
                r0, nr = SHARDED[k]
                g = mine(slice(r0, r0 + nr), D // NDEV)
            w_, m_, v_ = w_refs[i], m_refs[i], v_refs[i]
            g_out, d_out, m_out, v_out = outs[4 * i:4 * i + 4]
            if k == "conv_dw":
                for j in range(2):
                    gj = g[CW * j:CW * (j + 1)]
                    g_out[j] = gj
                    d_out[j], m_out[j], v_out[j] = _adamw_math(w_[j], m_[j], v_[j], gj)
            else:
                g_out[...] = g
                d_out[...], m_out[...], v_out[...] = _adamw_math(w_[...], m_[...], v_[...], g)

    out_shape = [_sds((1, 1), F32)] + [_sds(ins[i].shape, F32) for i in range(n) for _ in range(4)]
    res = pl.pallas_call(body, name="adamw_small", out_shape=out_shape, scratch_shapes=[pltpu.VMEM((SMALL_ROWS, D), F32)])(g8, *ins)
    out = {k: tuple(o.reshape(wts[k].shape) for o in res[1 + 4 * i:5 + 4 * i]) for i, k in enumerate(names)}
    return res[0], out


NAMES = ["meta_tokens", "norm_mix", "norm_ffn", "conv_w_in", "conv_b_in", "conv_dw", "conv_ln_g", "conv_ln_b", "conv_w_out",
         "conv_b_out", "kv_norm", "w_kv", "k_norm", "w_q", "q_norm", "attn_sinks", "w_o", "ffn_w_gate", "ffn_w_up", "ffn_w_down"]


def kernel(x, meta_tokens, norm_mix, norm_ffn, conv_w_in, conv_b_in, conv_dw, conv_ln_g, conv_ln_b, conv_w_out, conv_b_out, kv_norm, w_kv, k_norm, w_q, q_norm, attn_sinks, w_o, ffn_w_gate, ffn_w_up, ffn_w_down, loss_target, m_meta_tokens, m_norm_mix, m_norm_ffn, m_conv_w_in, m_conv_b_in, m_conv_dw, m_conv_ln_g, m_conv_ln_b, m_conv_w_out, m_conv_b_out, m_kv_norm, m_w_kv, m_k_norm, m_w_q, m_q_norm, m_attn_sinks, m_w_o, m_ffn_w_gate, m_ffn_w_up, m_ffn_w_down, v_meta_tokens, v_norm_mix, v_norm_ffn, v_conv_w_in, v_conv_b_in, v_conv_dw, v_conv_ln_g, v_conv_ln_b, v_conv_w_out, v_conv_b_out, v_kv_norm, v_w_kv, v_k_norm, v_w_q, v_q_norm, v_attn_sinks, v_w_o, v_ffn_w_gate, v_ffn_w_up, v_ffn_w_down):
    wts = dict(zip(NAMES, (meta_tokens, norm_mix, norm_ffn, conv_w_in, conv_b_in, conv_dw, conv_ln_g, conv_ln_b, conv_w_out,
                           conv_b_out, kv_norm, w_kv, k_norm, w_q, q_norm, attn_sinks, w_o, ffn_w_gate, ffn_w_up, ffn_w_down)))
    mom = dict(zip(NAMES, (m_meta_tokens, m_norm_mix, m_norm_ffn, m_conv_w_in, m_conv_b_in, m_conv_dw, m_conv_ln_g, m_conv_ln_b,
                           m_conv_w_out, m_conv_b_out, m_kv_norm, m_w_kv, m_k_norm, m_w_q, m_q_norm, m_attn_sinks, m_w_o,
                           m_ffn_w_gate, m_ffn_w_up, m_ffn_w_down)))
    var = dict(zip(NAMES, (v_meta_tokens, v_norm_mix, v_norm_ffn, v_conv_w_in, v_conv_b_in, v_conv_dw, v_conv_ln_g, v_conv_ln_b,
                           v_conv_w_out, v_conv_b_out, v_kv_norm, v_w_kv, v_k_norm, v_w_q, v_q_norm, v_attn_sinks, v_w_o,
                           v_ffn_w_gate, v_ffn_w_up, v_ffn_w_down)))
    for k in TRANSPOSED:
        wts[k], mom[k], var[k] = (jnp.swapaxes(a, 1, 2) for a in (wts[k], mom[k], var[k]))

    big_names = list(BIG)
    layers = cast_bf16([wts[k] for k in big_names])
    shards = {(k, l): blk for k, per_layer in zip(big_names, layers) for l, blk in enumerate(per_layer)}
    vec_names = ["meta_tokens", "conv_b_in", "conv_dw", "conv_ln_g", "conv_ln_b", "conv_b_out"]
    full = dict(zip(vec_names, all_gather_blocks([wts[k] for k in vec_names])))
    join_vec = lambda a: jnp.moveaxis(a, 0, -2).reshape(a.shape[1:-1] + (NDEV * a.shape[-1],))
    w = {}
    w["conv_b_in"] = join_vec(full["conv_b_in"]).reshape(2, 1, 2 * D)
    w["conv_dw"] = join_vec(full["conv_dw"])
    for k in ("conv_ln_g", "conv_ln_b", "conv_b_out"):
        w[k] = join_vec(full[k]).reshape(2, 1, D)
    w["norm_mix"] = norm_mix.reshape(4, 1, D)
    w["norm_ffn"] = norm_ffn.reshape(4, 1, D)
    w["kv_norm"] = kv_norm.reshape(1, D)
    w["k_norm"] = k_norm.reshape(1, HD)
    w["q_norm"] = q_norm.reshape(2, 1, HD)
    w["attn_sinks"] = attn_sinks.reshape(2, 1, NH)

    loss_blk, grad_x, gbig, arrived, gs = local_step(x, loss_target, full["meta_tokens"], w, shards)

    packed = pack_small(gs, loss_blk)

    grads, delta, new_m, new_v = {}, {}, {}, {}
    tail = EXCHANGE_PLAN["tail"]
    waiting = {nm.rstrip("0123456789") for nm, _ in tail}
    order = sorted([k for k in big_names if k not in waiting], key=lambda k: -wts[k].size) + [k for k in big_names if k in waiting]
    small8 = None
    for pos, k in enumerate(order):
        flat2 = wts[k].ndim == 2
        as3 = (lambda a: a[None]) if flat2 else (lambda a: a)
        riders = tail if pos == 0 else []
        gat = [packed] if pos == 1 else []
        parts = [arrived[k]] if flat2 else [arrived[f"{k}{i}"] for i in range(wts[k].shape[0])]
        outs, got_x, got_g = adamw_big(as3(wts[k]), as3(mom[k]), as3(var[k]), parts, "adamw_" + k,
                                       xch=[(gbig[nm], EXCHANGE_KIND[nm.rstrip("0123456789")], ks) for nm, ks in riders], gat=gat)
        for (nm, _), got in zip(riders, got_x):
            arrived[nm].append(got)
        if gat:
            small8 = got_g[0]
        grads[k], delta[k], new_m[k], new_v[k] = [o[0] if flat2 else (jnp.swapaxes(o, 1, 2) if k in TRANSPOSED else o) for o in outs]
    loss, small = adamw_small(small8, wts, mom, var)
    for k, (g_, d_, m_, v_) in small.items():
        grads[k], delta[k], new_m[k], new_v[k] = g_, d_, m_, v_
    return (loss.reshape(()), grad_x, *[grads[k] for k in NAMES], *[delta[k] for k in NAMES], *[new_m[k] for k in NAMES],
            *[new_v[k] for k in NAMES])
```
